```python
import jax, jax.numpy as jnp
from jax import lax
import numpy as np

D_MODEL = 1024
BATCH = 4
SEQ = 8192
DEPTH = 1
DEC_BATCH = 128
DEC_SEQ = 4
PAST_LEN = 16384
PAGE_SIZE = 128

HEAD_DIM = 64
ATTN_DIM = D_MODEL // 2
N_Q_HEADS = ATTN_DIM // HEAD_DIM
N_KV_HEADS = N_Q_HEADS // 4
KV_DIM = N_KV_HEADS * HEAD_DIM
CONV_DIM = D_MODEL - ATTN_DIM
CONV_GROUPS = CONV_DIM // HEAD_DIM
CONV_WIDTH = 3
WINDOW = 128
ROPE_THETA = 10000.0
IN_PROJ_DIM = ATTN_DIM + 2 * KV_DIM + 3 * CONV_DIM
N_EXPERT_GROUPS = 4
EXPERTS_PER_GROUP = 8
N_EXPERTS = N_EXPERT_GROUPS * EXPERTS_PER_GROUP
TOP_K_IN_GROUP = 2
EXPERT_HIDDEN = D_MODEL // 2
MOE_BLOCK = 128
RMS_EPS = 1e-5

kernel_name = "hybrid_swa_sink_shortconv_hmoe_step"


def _rmsnorm(x, gain):
    xf = x.astype(jnp.float32)
    xf = xf * lax.rsqrt(jnp.mean(xf * xf, axis=-1, keepdims=True) + RMS_EPS)
    return (xf * gain.astype(jnp.float32)).astype(x.dtype)


def _group_rmsnorm(x, gain, n_groups):
    shp = x.shape
    xg = x.astype(jnp.float32).reshape(shp[:-1] + (n_groups, shp[-1] // n_groups))
    xg = xg * lax.rsqrt(jnp.mean(xg * xg, axis=-1, keepdims=True) + RMS_EPS)
    return (xg.reshape(shp) * gain.astype(jnp.float32)).astype(x.dtype)


def _adaln(c, w_mod, b_mod):
    m = jax.nn.silu(c) @ w_mod + b_mod
    return jnp.split(m[:, None, :], 6, axis=-1)


def _modulated_norm(x, gain, shift, scale):
    return _rmsnorm(x, gain) * (1 + scale) + shift


def _rope(x, pos):
    half = HEAD_DIM // 2
    inv_freq = ROPE_THETA ** (-jnp.arange(half, dtype=jnp.float32) / half)
    ang = pos.astype(jnp.float32)[:, None] * inv_freq[None, :]
    cos = jnp.cos(ang)[None, :, None, :]
    sin = jnp.sin(ang)[None, :, None, :]
    xf = x.astype(jnp.float32)
    x1, x2 = xf[..., :half], xf[..., half:]
    return jnp.concatenate([x1 * cos - x2 * sin, x2 * cos + x1 * sin], axis=-1).astype(x.dtype)


def _mixer_inputs(h, w_in, pos):
    n, s, _ = h.shape
    splits = [ATTN_DIM, ATTN_DIM + KV_DIM, ATTN_DIM + 2 * KV_DIM,
              ATTN_DIM + 2 * KV_DIM + CONV_DIM, ATTN_DIM + 2 * KV_DIM + 2 * CONV_DIM]
    q, k, v, h_conv, gate_b, gate_c = jnp.split(h @ w_in, splits, axis=-1)
    q = _rope(q.reshape(n, s, N_Q_HEADS, HEAD_DIM), pos)
    k = _rope(k.reshape(n, s, N_KV_HEADS, HEAD_DIM), pos)
    v = v.reshape(n, s, N_KV_HEADS, HEAD_DIM)
    return q, k, v, gate_c * h_conv, gate_b


def _sink_attention(q, k, v, q_pos, k_pos, sinks):
    n, nb, nq, _, _ = q.shape
    g = N_Q_HEADS // N_KV_HEADS
    qg = q.reshape(n, nb, nq, N_KV_HEADS, g, HEAD_DIM)
    s = jnp.einsum("nbqkgd,nbskd->nbkgqs", qg, k).astype(jnp.float32) * (HEAD_DIM ** -0.5)
    qp = q_pos[:, :, None]
    kp = k_pos[:, None, :]
    valid = (kp <= qp) & (qp - kp < WINDOW) & (kp >= 0)
    s = jnp.where(valid[None, :, None, None], s, -jnp.inf)
    sink = sinks.astype(jnp.float32).reshape(N_KV_HEADS, g)[None, None, :, :, None, None]
    m = jnp.maximum(jnp.max(s, axis=-1, keepdims=True), sink)
    p = jnp.exp(s - m)
    p = p / (jnp.sum(p, axis=-1, keepdims=True) + jnp.exp(sink - m))
    o = jnp.einsum("nbkgqs,nbskd->nbqkgd", p.astype(v.dtype), v)
    return o.reshape(n, nb * nq, N_Q_HEADS * HEAD_DIM)


def _band_attention(q, k, v, sinks):
    n, s = q.shape[:2]
    nb = s // WINDOW
    qb = q.reshape(n, nb, WINDOW, N_Q_HEADS, HEAD_DIM)

    def band(t):
        tb = t.reshape(n, nb, WINDOW, N_KV_HEADS, HEAD_DIM)
        prev = jnp.concatenate([jnp.zeros_like(tb[:, :1]), tb[:, :-1]], axis=1)
        return jnp.concatenate([prev, tb], axis=2)

    q_pos = jnp.arange(s, dtype=jnp.int32).reshape(nb, WINDOW)
    k_pos = jnp.concatenate([q_pos - WINDOW, q_pos], axis=1)
    return _sink_attention(qb, band(k), band(v), q_pos, k_pos, sinks)


def _depthwise_conv(u_ext, conv_w):
    return lax.conv_general_dilated(
        u_ext, conv_w[:, None, :].astype(u_ext.dtype), window_strides=(1,), padding="VALID",
        dimension_numbers=("NWC", "WIO", "NWC"), feature_group_count=CONV_DIM)


def _hier_moe(h, w_group, b_group, w_expert, b_expert, w_gate, w_up, w_down):
    shp = h.shape
    x = h.reshape(-1, D_MODEL)
    m = x.shape[0]
    lg = (x @ w_group + b_group).astype(jnp.float32)
    pg = jax.nn.softmax(lg, axis=-1)
    grp = jnp.argmax(lg, axis=-1).astype(jnp.int32)
    pg_sel = jnp.take_along_axis(pg, grp[:, None], axis=-1)
    le = (x @ w_expert + b_expert).astype(jnp.float32).reshape(m, N_EXPERT_GROUPS, EXPERTS_PER_GROUP)
    le = jnp.take_along_axis(le, grp[:, None, None], axis=1)[:, 0]
    top_v, top_i = lax.top_k(le, TOP_K_IN_GROUP)
    gate = pg_sel * jax.nn.softmax(top_v, axis=-1)
    expert = grp[:, None] * EXPERTS_PER_GROUP + top_i.astype(jnp.int32)
    a = m * TOP_K_IN_GROUP
    flat_e = expert.reshape(-1)
    order = jnp.argsort(flat_e)
    sorted_e = flat_e[order]
    counts = jnp.bincount(flat_e, length=N_EXPERTS)
    start = jnp.cumsum(counts) - counts
    padded = (counts + MOE_BLOCK - 1) // MOE_BLOCK * MOE_BLOCK
    pend = jnp.cumsum(padded)
    pstart = pend - padded
    dest = pstart[sorted_e] + jnp.arange(a, dtype=jnp.int32) - start[sorted_e]
    n_blocks = -(-(a + N_EXPERTS * (MOE_BLOCK - 1)) // MOE_BLOCK)
    p_rows = n_blocks * MOE_BLOCK
    tok_buf = jnp.full((p_rows,), m, jnp.int32).at[dest].set((order // TOP_K_IN_GROUP).astype(jnp.int32))
    w_buf = jnp.zeros((p_rows,), jnp.float32).at[dest].set(gate.reshape(-1)[order])
    block_e = jnp.minimum(
        jnp.searchsorted(pend, jnp.arange(n_blocks, dtype=jnp.int32) * MOE_BLOCK, side="right"),
        N_EXPERTS - 1)
    x_pad = jnp.concatenate([x, jnp.zeros((1, D_MODEL), x.dtype)], axis=0)
    xb = x_pad[tok_buf].reshape(n_blocks, MOE_BLOCK, D_MODEL)

    def expert_block(args):
        xblk, e = args
        return (jax.nn.silu(xblk @ w_gate[e]) * (xblk @ w_up[e])) @ w_down[e]

    yb = lax.map(expert_block, (xb, block_e))
    y = jnp.zeros((m + 1, D_MODEL), jnp.float32).at[tok_buf].add(
        yb.reshape(p_rows, D_MODEL).astype(jnp.float32) * w_buf[:, None])
    return y[:m].astype(h.dtype).reshape(shp)


def _layer(x, c, pos, k_past, v_past, u_past, p):
    (attn_norm, ffn_norm, w_mod, b_mod, w_in, conv_w, sinks, gn_attn, gn_conv, w_out,
     w_group, b_group, w_expert, b_expert, w_gate, w_up, w_down) = p
    shift_a, scale_a, gate_a, shift_f, scale_f, gate_f = _adaln(c, w_mod, b_mod)
    h = _modulated_norm(x, attn_norm, shift_a, scale_a)
    q, k, v, u, gate_b = _mixer_inputs(h, w_in, pos)
    if k_past is None:
        attn_o = _band_attention(q, k, v, sinks)
        k_all, v_all = k, v
        u_all = jnp.pad(u, ((0, 0), (CONV_WIDTH - 1, 0), (0, 0)))
        keep = min(WINDOW, x.shape[1])
    else:
        keep = k_past.shape[1]
        k_all = jnp.concatenate([k_past, k], axis=1)
        v_all = jnp.concatenate([v_past, v], axis=1)
        k_pos = jnp.concatenate([pos[0] - keep + jnp.arange(keep, dtype=jnp.int32), pos])
        attn_o = _sink_attention(q[:, None], k_all[:, None], v_all[:, None], pos[None], k_pos[None], sinks)
        u_all = jnp.concatenate([u_past, u], axis=1)
    conv_o = gate_b * _depthwise_conv(u_all, conv_w)
    mix = jnp.concatenate([_group_rmsnorm(attn_o, gn_attn, N_Q_HEADS),
                           _group_rmsnorm(conv_o, gn_conv, CONV_GROUPS)], axis=-1) @ w_out
    x = x + gate_a * mix
    h = _modulated_norm(x, ffn_norm, shift_f, scale_f)
    x = x + gate_f * _hier_moe(h, w_group, b_group, w_expert, b_expert, w_gate, w_up, w_down)
    return x, k_all[:, -keep:], v_all[:, -keep:], u_all[:, -(CONV_WIDTH - 1):]


def setup_inputs(seed: int = 0) -> dict:
    key = jax.random.key(seed)
    ks = jax.random.split(key, 25)
    f32 = jnp.float32
    keep = min(WINDOW, PAST_LEN)
    d_in = D_MODEL ** -0.5

    def nrm(k, shape, s):
        return jax.random.normal(k, shape, f32) * s

    return {
        "x_prompt": nrm(ks[0], (BATCH, SEQ, D_MODEL), 1.0),
        "x_sample": nrm(ks[1], (DEC_BATCH, DEC_SEQ, D_MODEL), 1.0),
        "cache_k": nrm(ks[2], (DEPTH, DEC_BATCH, keep, N_KV_HEADS, HEAD_DIM), 1.0),
        "cache_v": nrm(ks[3], (DEPTH, DEC_BATCH, keep, N_KV_HEADS, HEAD_DIM), 1.0),
        "state_conv": nrm(ks[4], (DEPTH, DEC_BATCH, CONV_WIDTH - 1, CONV_DIM), 1.0),
        "c_prompt": nrm(ks[5], (BATCH, D_MODEL), 1.0),
        "c_sample": nrm(ks[6], (DEC_BATCH, D_MODEL), 1.0),
        "attn_norm": 1.0 + nrm(ks[7], (DEPTH, D_MODEL), 0.1),
        "ffn_norm": 1.0 + nrm(ks[8], (DEPTH, D_MODEL), 0.1),
        "w_mod": nrm(ks[9], (DEPTH, D_MODEL, 6 * D_MODEL), 0.5 * d_in),
        "b_mod": nrm(ks[10], (DEPTH, 6 * D_MODEL), 0.02),
        "w_in": nrm(ks[11], (DEPTH, D_MODEL, IN_PROJ_DIM), d_in),
        "conv_w": nrm(ks[12], (DEPTH, CONV_WIDTH, CONV_DIM), CONV_WIDTH ** -0.5),
        "attn_sinks": nrm(ks[13], (DEPTH, N_Q_HEADS), 1.0),
        "out_norm_attn": 1.0 + nrm(ks[14], (DEPTH, ATTN_DIM), 0.1),
        "out_norm_conv": 1.0 + nrm(ks[15], (DEPTH, CONV_DIM), 0.1),
        "w_out": nrm(ks[16], (DEPTH, D_MODEL, D_MODEL), d_in),
        "w_group": nrm(ks[17], (DEPTH, D_MODEL, N_EXPERT_GROUPS), d_in),
        "b_group": nrm(ks[18], (DEPTH, N_EXPERT_GROUPS), 0.01),
        "w_expert": nrm(ks[19], (DEPTH, D_MODEL, N_EXPERTS), d_in),
        "b_expert": nrm(ks[20], (DEPTH, N_EXPERTS), 0.01),
        "w_gate": nrm(ks[21], (DEPTH, N_EXPERTS, D_MODEL, EXPERT_HIDDEN), d_in),
        "w_up": nrm(ks[22], (DEPTH, N_EXPERTS, D_MODEL, EXPERT_HIDDEN), d_in),
        "w_down": nrm(ks[23], (DEPTH, N_EXPERTS, EXPERT_HIDDEN, D_MODEL), EXPERT_HIDDEN ** -0.5),
        "final_norm": 1.0 + nrm(ks[24], (D_MODEL,), 0.1),
    }


def reference(x_prompt, x_sample, cache_k, cache_v, state_conv, c_prompt, c_sample,
              attn_norm, ffn_norm, w_mod, b_mod, w_in, conv_w, attn_sinks,
              out_norm_attn, out_norm_conv, w_out, w_group, b_group, w_expert, b_expert,
              w_gate, w_up, w_down, final_norm):
    pos_p = jnp.arange(x_prompt.shape[1], dtype=jnp.int32)
    pos_s = PAST_LEN + jnp.arange(x_sample.shape[1], dtype=jnp.int32)
    xp, xs = x_prompt, x_sample
    kp_l, vp_l, up_l, ks_l, vs_l, us_l = [], [], [], [], [], []
    for l in range(DEPTH):
        p = (attn_norm[l], ffn_norm[l], w_mod[l], b_mod[l], w_in[l], conv_w[l], attn_sinks[l],
             out_norm_attn[l], out_norm_conv[l], w_out[l], w_group[l], b_group[l],
             w_expert[l], b_expert[l], w_gate[l], w_up[l], w_down[l])
        xp, kp, vp, up = _layer(xp, c_prompt, pos_p, None, None, None, p)
        xs, k_s, v_s, u_s = _layer(xs, c_sample, pos_s, cache_k[l], cache_v[l], state_conv[l], p)
        kp_l.append(kp)
        vp_l.append(vp)
        up_l.append(up)
        ks_l.append(k_s)
        vs_l.append(v_s)
        us_l.append(u_s)
    y_prompt = _rmsnorm(xp, final_norm)
    y_sample = _rmsnorm(xs, final_norm)
    return (y_prompt, y_sample, jnp.stack(kp_l), jnp.stack(vp_l), jnp.stack(up_l),
            jnp.stack(ks_l), jnp.stack(vs_l), jnp.stack(us_l))
```

```python
import functools

import jax
import jax.numpy as jnp
import numpy as np
from jax import lax
from jax.experimental import pallas as pl
from jax.experimental.pallas import tpu as pltpu

D_MODEL = 1024
HEAD_DIM = 64
ATTN_DIM = 512
N_Q_HEADS = 8
N_KV_HEADS = 2
KV_DIM = 128
CONV_DIM = 512
WINDOW = 128
IN_PROJ_DIM = ATTN_DIM + 2 * KV_DIM + 3 * CONV_DIM
N_GROUPS = 4
EXPERTS_PER_GROUP = 8
N_EXPERTS = 32
EXPERT_HIDDEN = 512
RMS_EPS = 1e-5
ROPE_THETA = 10000.0
PAST_LEN = 16384

LANES = 128
ROUTER_ROWS = 48
NEG = -1e30

MIXER_TILE = 512
SAMPLE_TILE = 128
SAMPLE_CHUNK = 8
SAMPLE_KEYS = 1152
EXPERT_BLOCK = 256
COMBINE_TILE = 256
DISPATCH_CHUNK = 256
VMEM_LIMIT = 56 * 1024 * 1024


def _rms(x, gain):
    ms = jnp.mean(x * x, axis=-1, keepdims=True)
    return x * lax.rsqrt(ms + RMS_EPS) * gain


def _half_lane_mask(rows=1):
    return lax.broadcasted_iota(jnp.int32, (rows, LANES), 1) < HEAD_DIM


def _rope(x, cos, sin_signed):
    t, w = x.shape
    reps = w // LANES
    lane = lax.broadcasted_iota(jnp.int32, (t, w), 1)
    upper = (lane % HEAD_DIM) >= (HEAD_DIM // 2)
    partner = jnp.where(upper, pltpu.roll(x, HEAD_DIM // 2, 1), pltpu.roll(x, w - HEAD_DIM // 2, 1))
    if reps > 1:
        cos = jnp.concatenate([cos] * reps, axis=1)
        sin_signed = jnp.concatenate([sin_signed] * reps, axis=1)
    return x * cos + partner * sin_signed


def _swap_halves(x):
    return pltpu.roll(x, HEAD_DIM, 1)


def _group_norm64(x, gain):
    t, w = x.shape
    lo = _half_lane_mask(t)
    outs = []
    for c in range(w // LANES):
        xc = x[:, c * LANES:(c + 1) * LANES]
        sq = xc * xc
        s_lo = jnp.sum(jnp.where(lo, sq, 0.0), axis=-1, keepdims=True)
        s_hi = jnp.sum(jnp.where(lo, 0.0, sq), axis=-1, keepdims=True)
        r = jnp.where(lo, lax.rsqrt(s_lo * (1.0 / HEAD_DIM) + RMS_EPS), lax.rsqrt(s_hi * (1.0 / HEAD_DIM) + RMS_EPS))
        outs.append(xc * r)
    return jnp.concatenate(outs, axis=1) * gain


def _attend(q_rows, keys, keys_sw, vals, vals_sw, bias, sinks_ref, group):
    r = q_rows.shape[0]
    lo = _half_lane_mask(r)
    outs = []
    for parity in range(2):
        heads = [4 * group + parity, 4 * group + parity + 2]
        kk, vv = (keys, vals) if parity == group else (keys_sw, vals_sw)
        half = lo if parity == 0 else jnp.logical_not(lo)
        qs = []
        for h in heads:
            pair = q_rows[:, (h // 2) * LANES:(h // 2 + 1) * LANES]
            qs.append(jnp.where(half, pair, jnp.zeros_like(pair)))
        qcat = jnp.concatenate(qs, axis=0)
        s = lax.dot_general(qcat, kk, (((1,), (1,)), ((), ())), preferred_element_type=jnp.float32)
        ps, dens = [], []
        for i, h in enumerate(heads):
            sh = s[i * r:(i + 1) * r] + bias
            sink = sinks_ref[h]
            m = jnp.maximum(jnp.max(sh, axis=-1, keepdims=True), sink)
            p = jnp.exp(sh - m)
            dens.append(jnp.sum(p, axis=-1, keepdims=True) + jnp.exp(sink - m))
            ps.append(p.astype(jnp.bfloat16))
        o = jnp.dot(jnp.concatenate(ps, axis=0), vv, preferred_element_type=jnp.float32)
        for i, h in enumerate(heads):
            oh = o[i * r:(i + 1) * r] / dens[i]
            ms = jnp.sum(jnp.where(half, oh * oh, 0.0), axis=-1, keepdims=True) * (1.0 / HEAD_DIM)
            outs.append((h, oh * lax.rsqrt(ms + RMS_EPS)))
    outs.sort(key=lambda t: t[0])
    return [o for _, o in outs]


def _merge_heads(head_outs, lo):
    pairs = [jnp.where(lo, head_outs[2 * j], head_outs[2 * j + 1]) for j in range(4)]
    return jnp.concatenate(pairs, axis=1)


def _route(h2b, wr_ref, br_ref, tri_ref, cnt_ref, ri_ref, rf_ref):
    t = h2b.shape[0]
    lg = lax.dot_general(wr_ref[...], h2b, (((1,), (1,)), ((), ())), preferred_element_type=jnp.float32)
    lg = lg + br_ref[...][:, 0:1]
    r8 = lax.broadcasted_iota(jnp.int32, (8, t), 0)
    lgrp = jnp.where(r8 < N_GROUPS, lg[0:8], -jnp.inf)
    gmax = jnp.max(lgrp, axis=0, keepdims=True)
    grp = jnp.min(jnp.where(lgrp == gmax, r8, 8), axis=0, keepdims=True)
    pg_sel = 1.0 / jnp.sum(jnp.exp(lgrp - gmax), axis=0, keepdims=True)
    r32 = lax.broadcasted_iota(jnp.int32, (N_EXPERTS, t), 0)
    le = jnp.where((r32 // EXPERTS_PER_GROUP) == grp, lg[8:8 + N_EXPERTS], -jnp.inf)
    v1 = jnp.max(le, axis=0, keepdims=True)
    i1 = jnp.min(jnp.where(le == v1, r32, N_EXPERTS), axis=0, keepdims=True)
    le2 = jnp.where(r32 == i1, -jnp.inf, le)
    v2 = jnp.max(le2, axis=0, keepdims=True)
    i2 = jnp.min(jnp.where(le2 == v2, r32, N_EXPERTS), axis=0, keepdims=True)
    e21 = jnp.exp(v2 - v1)
    gate1 = pg_sel / (1.0 + e21)
    gate2 = pg_sel * e21 / (1.0 + e21)
    hot1 = r32 == i1
    hot2 = r32 == i2
    onehot = jnp.where(jnp.logical_or(hot1, hot2), 1.0, 0.0)
    before = jnp.dot(onehot.astype(jnp.bfloat16), tri_ref[...], preferred_element_type=jnp.float32)
    pos = before + cnt_ref[...][:, 0:1]
    rank1 = jnp.sum(jnp.where(hot1, pos, 0.0), axis=0, keepdims=True)
    rank2 = jnp.sum(jnp.where(hot2, pos, 0.0), axis=0, keepdims=True)
    cnt_ref[...] = cnt_ref[...] + jnp.sum(onehot, axis=1, keepdims=True)
    zi = jnp.zeros((4, t), jnp.int32)
    ri_ref[...] = jnp.concatenate([i1, i2, rank1.astype(jnp.int32), rank2.astype(jnp.int32), zi], axis=0)
    rf_ref[...] = jnp.concatenate([gate1, gate2, jnp.zeros((6, t), jnp.float32)], axis=0)


def _split_proj(proj):
    a = ATTN_DIM
    q = proj[:, :a]
    k = proj[:, a:a + KV_DIM]
    v = proj[:, a + KV_DIM:a + 2 * KV_DIM]
    c0 = a + 2 * KV_DIM
    h_conv = proj[:, c0:c0 + CONV_DIM]
    gate_b = proj[:, c0 + CONV_DIM:c0 + 2 * CONV_DIM]
    gate_c = proj[:, c0 + 2 * CONV_DIM:c0 + 3 * CONV_DIM]
    return q, k, v, gate_c * h_conv, gate_b


def _mixer_tail(x, attn_n, conv_n, mod, wout_ref, fnorm_ref, wr_ref, br_ref, tri_ref, cnt_ref,
                x1_ref, h2_ref, ri_ref, rf_ref):
    gate_a, shift_f, scale_f = mod
    cat = jnp.concatenate([attn_n.astype(jnp.bfloat16), conv_n.astype(jnp.bfloat16)], axis=1)
    mix = jnp.dot(cat, wout_ref[...], preferred_element_type=jnp.float32)
    x1 = x + gate_a * mix
    x1_ref[...] = x1
    h2 = _rms(x1, fnorm_ref[...]) * (1.0 + scale_f) + shift_f
    h2_ref[...] = h2
    _route(h2.astype(jnp.bfloat16), wr_ref, br_ref, tri_ref, cnt_ref, ri_ref, rf_ref)


def _adaln_kernel(c_ref, w_ref, b_ref, o_ref):
    c = c_ref[...]
    a = (c * jax.nn.sigmoid(c)).astype(jnp.bfloat16)
    o_ref[...] = jnp.dot(a, w_ref[...].astype(jnp.bfloat16), preferred_element_type=jnp.float32) + b_ref[...]


def _adaln(c, w_mod, b_mod):
    n = c.shape[0]
    tn = 1536
    return pl.pallas_call(
        _adaln_kernel,
        grid=(w_mod.shape[1] // tn,),
        in_specs=[pl.BlockSpec((n, D_MODEL), lambda j: (0, 0)),
                  pl.BlockSpec((D_MODEL, tn), lambda j: (0, j)),
                  pl.BlockSpec((1, tn), lambda j: (0, j))],
        out_specs=pl.BlockSpec((n, tn), lambda j: (0, j)),
        out_shape=jax.ShapeDtypeStruct((n, w_mod.shape[1]), jnp.float32),
        compiler_params=pltpu.CompilerParams(dimension_semantics=("arbitrary",), vmem_limit_bytes=VMEM_LIMIT),
        name="adaln",
    )(c, w_mod, b_mod.reshape(1, -1))


def _prompt_mixer_kernel(sinks_ref, x_ref, mod_ref, anorm_ref, fnorm_ref, win_ref, cos_ref, sin_ref, bias_ref,
                         convw_ref, gna_ref, gnc_ref, wout_ref, wr_ref, br_ref, tri_ref,
                         x1_ref, h2_ref, ri_ref, rf_ref, cnt_out_ref, knew_ref, vnew_ref, unew_ref,
                         q_s, k_s, ksw_s, v_s, vsw_s, u_s, att_s, cnt_s):
    b = pl.program_id(0)
    t = pl.program_id(1)
    tm = MIXER_TILE
    nblk = tm // WINDOW

    @pl.when(jnp.logical_and(b == 0, t == 0))
    def _():
        cnt_s[...] = jnp.zeros_like(cnt_s)

    @pl.when(t == 0)
    def _():
        z = jnp.zeros((WINDOW, KV_DIM), jnp.bfloat16)
        k_s[0:WINDOW, :] = z
        ksw_s[0:WINDOW, :] = z
        v_s[0:WINDOW, :] = z
        vsw_s[0:WINDOW, :] = z
        u_s[0:8, :] = jnp.zeros((8, CONV_DIM), jnp.float32)

    x = x_ref[...]
    mod = mod_ref[...]
    h = _rms(x, anorm_ref[...]) * (1.0 + mod[1:2]) + mod[0:1]
    proj = jnp.dot(h.astype(jnp.bfloat16), win_ref[...], preferred_element_type=jnp.float32)
    q, k, v, u, gate_b = _split_proj(proj)
    cos = cos_ref[...]
    sin = sin_ref[...]
    q = _rope(q, cos, sin) * (HEAD_DIM ** -0.5)
    k = _rope(k, cos, sin)
    q_s[...] = q.astype(jnp.bfloat16)
    k_s[WINDOW:, :] = k.astype(jnp.bfloat16)
    ksw_s[WINDOW:, :] = _swap_halves(k).astype(jnp.bfloat16)
    v_s[WINDOW:, :] = v.astype(jnp.bfloat16)
    vsw_s[WINDOW:, :] = _swap_halves(v).astype(jnp.bfloat16)
    knew_ref[...] = k[tm - WINDOW:, :]
    vnew_ref[...] = v[tm - WINDOW:, :]

    u_s[8:, :] = u
    unew_ref[...] = u[tm - 8:, :]
    cw = convw_ref[...]
    conv = (u_s[6:6 + tm, :] * cw[0:1] + u_s[7:7 + tm, :] * cw[1:2] + u * cw[2:3]) * gate_b
    u_s[0:8, :] = u[tm - 8:, :]
    conv_n = _group_norm64(conv, gnc_ref[...])

    lo = _half_lane_mask(WINDOW)
    first = t == 0

    def block(i, carry):
        r0 = pl.multiple_of(i * WINDOW, WINDOW)
        bias = bias_ref[jnp.where(jnp.logical_and(first, i == 0), 0, 1)]
        qb = q_s[pl.ds(r0, WINDOW), :]
        keys = k_s[pl.ds(r0, 2 * WINDOW), :]
        keys_sw = ksw_s[pl.ds(r0, 2 * WINDOW), :]
        vals = v_s[pl.ds(r0, 2 * WINDOW), :]
        vals_sw = vsw_s[pl.ds(r0, 2 * WINDOW), :]
        outs = []
        for g in range(N_KV_HEADS):
            outs += _attend(qb, keys, keys_sw, vals, vals_sw, bias, sinks_ref, g)
        att_s[pl.ds(r0, WINDOW), :] = _merge_heads(outs, lo)
        return carry

    lax.fori_loop(0, nblk, block, 0)
    k_s[0:WINDOW, :] = k_s[tm:tm + WINDOW, :]
    ksw_s[0:WINDOW, :] = ksw_s[tm:tm + WINDOW, :]
    v_s[0:WINDOW, :] = v_s[tm:tm + WINDOW, :]
    vsw_s[0:WINDOW, :] = vsw_s[tm:tm + WINDOW, :]

    attn_n = att_s[...] * gna_ref[...]
    _mixer_tail(x, attn_n, conv_n, (mod[2:3], mod[3:4], mod[4:5]), wout_ref, fnorm_ref, wr_ref, br_ref, tri_ref,
                cnt_s, x1_ref, h2_ref, ri_ref, rf_ref)
    cnt_out_ref[...] = cnt_s[...]


def _prompt_mixer(x, mod, sinks, consts):
    bsz, seq, _ = x.shape
    tm = MIXER_TILE
    nt = seq // tm
    full = lambda shape: pl.BlockSpec(shape, lambda b, t, s: (0,) * len(shape))
    in_specs = [
        pl.BlockSpec((None, tm, D_MODEL), lambda b, t, s: (b, t, 0)),
        pl.BlockSpec((None, 6, D_MODEL), lambda b, t, s: (b, 0, 0)),
        full((1, D_MODEL)), full((1, D_MODEL)),
        full((D_MODEL, IN_PROJ_DIM)),
        pl.BlockSpec((tm, LANES), lambda b, t, s: (t, 0)),
        pl.BlockSpec((tm, LANES), lambda b, t, s: (t, 0)),
        full((2, WINDOW, 2 * WINDOW)),
        full((3, CONV_DIM)), full((1, ATTN_DIM)), full((1, CONV_DIM)),
        full((D_MODEL, D_MODEL)),
        full((ROUTER_ROWS, D_MODEL)), full((ROUTER_ROWS, LANES)),
        full((tm, tm)),
    ]
    out_shape = [
        jax.ShapeDtypeStruct((bsz * seq, D_MODEL), jnp.float32),
        jax.ShapeDtypeStruct((bsz * seq, D_MODEL), jnp.float32),
        jax.ShapeDtypeStruct((bsz * nt, 8, tm), jnp.int32),
        jax.ShapeDtypeStruct((bsz * nt, 8, tm), jnp.float32),
        jax.ShapeDtypeStruct((N_EXPERTS, LANES), jnp.float32),
        jax.ShapeDtypeStruct((bsz, WINDOW, KV_DIM), jnp.float32),
        jax.ShapeDtypeStruct((bsz, WINDOW, KV_DIM), jnp.float32),
        jax.ShapeDtypeStruct((bsz, 8, CONV_DIM), jnp.float32),
    ]
    out_specs = [
        pl.BlockSpec((tm, D_MODEL), lambda b, t, s: (b * nt + t, 0)),
        pl.BlockSpec((tm, D_MODEL), lambda b, t, s: (b * nt + t, 0)),
        pl.BlockSpec((None, 8, tm), lambda b, t, s: (b * nt + t, 0, 0)),
        pl.BlockSpec((None, 8, tm), lambda b, t, s: (b * nt + t, 0, 0)),
        pl.BlockSpec((N_EXPERTS, LANES), lambda b, t, s: (0, 0)),
        pl.BlockSpec((None, WINDOW, KV_DIM), lambda b, t, s: (b, 0, 0)),
        pl.BlockSpec((None, WINDOW, KV_DIM), lambda b, t, s: (b, 0, 0)),
        pl.BlockSpec((None, 8, CONV_DIM), lambda b, t, s: (b, 0, 0)),
    ]
    scratch = [
        pltpu.VMEM((tm, ATTN_DIM), jnp.bfloat16),
        pltpu.VMEM((tm + WINDOW, KV_DIM), jnp.bfloat16), pltpu.VMEM((tm + WINDOW, KV_DIM), jnp.bfloat16),
        pltpu.VMEM((tm + WINDOW, KV_DIM), jnp.bfloat16), pltpu.VMEM((tm + WINDOW, KV_DIM), jnp.bfloat16),
        pltpu.VMEM((tm + 8, CONV_DIM), jnp.float32),
        pltpu.VMEM((tm, ATTN_DIM), jnp.float32),
        pltpu.VMEM((N_EXPERTS, LANES), jnp.float32),
    ]
    return pl.pallas_call(
        _prompt_mixer_kernel,
        grid_spec=pltpu.PrefetchScalarGridSpec(
            num_scalar_prefetch=1, grid=(bsz, nt), in_specs=in_specs, out_specs=out_specs, scratch_shapes=scratch),
        out_shape=out_shape,
        compiler_params=pltpu.CompilerParams(dimension_semantics=("arbitrary", "arbitrary"),
                                             vmem_limit_bytes=VMEM_LIMIT),
        name="prompt_mixer",
    )(sinks, x, mod, consts["anorm"], consts["fnorm"], consts["w_in"], consts["cos_p"], consts["sin_p"],
      consts["bias_p"], consts["conv_w"], consts["gn_attn"], consts["gn_conv"], consts["w_out"],
      consts["w_r"], consts["b_r"], consts["tri"])


def _sample_mixer_kernel(sinks_ref, x_ref, mod_ref, anorm_ref, fnorm_ref, win_ref, cos_ref, sin_ref, bias_ref,
                         convw_ref, gna_ref, gnc_ref, wout_ref, wr_ref, br_ref, tri_ref,
                         ck_ref, cv_ref, e1_ref, e2_ref, cnt_in_ref,
                         x1_ref, h2_ref, ri_ref, rf_ref, cnt_out_ref, knew_ref, vnew_ref, unew_ref,
                         q_s, k_s, v_s, att_s, cnt_s, *, dec):
    n = x_ref.shape[0]
    cb = SAMPLE_CHUNK
    rows = cb * dec
    pad_keys = SAMPLE_KEYS - cb * WINDOW - rows

    @pl.when(pl.program_id(0) == 0)
    def _():
        cnt_s[...] = cnt_in_ref[...]

    x = x_ref[...]
    mod = mod_ref[...]
    md = lambda j: mod[:, j * D_MODEL:(j + 1) * D_MODEL]
    h = _rms(x, anorm_ref[...]) * (1.0 + md(1)) + md(0)
    proj = jnp.dot(h.astype(jnp.bfloat16), win_ref[...], preferred_element_type=jnp.float32)
    q, k, v, u, gate_b = _split_proj(proj)
    cos = cos_ref[...]
    sin = sin_ref[...]
    q = _rope(q, cos, sin) * (HEAD_DIM ** -0.5)
    k = _rope(k, cos, sin)
    q_s[...] = q.astype(jnp.bfloat16)
    k_s[...] = k
    v_s[...] = v
    knew_ref[...] = k
    vnew_ref[...] = v
    unew_ref[...] = u

    pos = lax.broadcasted_iota(jnp.int32, (n, CONV_DIM), 0) % dec
    um1 = jnp.where(pos >= 1, pltpu.roll(u, 1, 0), 0.0) + e1_ref[...]
    um2 = jnp.where(pos >= 2, pltpu.roll(u, 2, 0), 0.0) + e2_ref[...]
    cw = convw_ref[...]
    conv = (um2 * cw[0:1] + um1 * cw[1:2] + u * cw[2:3]) * gate_b
    conv_n = _group_norm64(conv, gnc_ref[...])

    lo = _half_lane_mask(rows)
    bias = bias_ref[...]

    def chunk(c, carry):
        r0 = pl.multiple_of(c * rows, rows)
        c0 = pl.multiple_of(c * (cb * WINDOW), cb * WINDOW)
        zpad = jnp.zeros((pad_keys, KV_DIM), jnp.float32)
        kc = jnp.concatenate([ck_ref[pl.ds(c0, cb * WINDOW), :], k_s[pl.ds(r0, rows), :], zpad], axis=0)
        vc = jnp.concatenate([cv_ref[pl.ds(c0, cb * WINDOW), :], v_s[pl.ds(r0, rows), :], zpad], axis=0)
        keys, keys_sw = kc.astype(jnp.bfloat16), _swap_halves(kc).astype(jnp.bfloat16)
        vals, vals_sw = vc.astype(jnp.bfloat16), _swap_halves(vc).astype(jnp.bfloat16)
        qb = q_s[pl.ds(r0, rows), :]
        outs = []
        for g in range(N_KV_HEADS):
            outs += _attend(qb, keys, keys_sw, vals, vals_sw, bias, sinks_ref, g)
        att_s[pl.ds(r0, rows), :] = _merge_heads(outs, lo)
        return carry

    lax.fori_loop(0, n // rows, chunk, 0)
    attn_n = att_s[...] * gna_ref[...]
    _mixer_tail(x, attn_n, conv_n, (md(2), md(3), md(4)), wout_ref, fnorm_ref, wr_ref, br_ref, tri_ref,
                cnt_s, x1_ref, h2_ref, ri_ref, rf_ref)
    cnt_out_ref[...] = cnt_s[...]


def _sample_mixer(x, mod, sinks, consts, ck, cv, e1, e2, cnt_in, dec):
    n_s = x.shape[0]
    ts = SAMPLE_TILE
    seqs = ts // dec
    full = lambda a: pl.BlockSpec(a.shape, lambda i, s: (0,) * a.ndim)
    rows = lambda width: pl.BlockSpec((ts, width), lambda i, s: (i, 0))
    in_specs = [
        rows(D_MODEL), rows(6 * D_MODEL),
        full(consts["anorm"]), full(consts["fnorm"]), full(consts["w_in"]),
        rows(LANES), rows(LANES), full(consts["bias_s"]),
        full(consts["conv_w"]), full(consts["gn_attn"]), full(consts["gn_conv"]), full(consts["w_out"]),
        full(consts["w_r"]), full(consts["b_r"]),
        pl.BlockSpec((ts, ts), lambda i, s: (0, 0)),
        pl.BlockSpec((seqs * WINDOW, KV_DIM), lambda i, s: (i, 0)),
        pl.BlockSpec((seqs * WINDOW, KV_DIM), lambda i, s: (i, 0)),
        rows(CONV_DIM), rows(CONV_DIM), full(cnt_in),
    ]
    args = [x, mod, consts["anorm"], consts["fnorm"], consts["w_in"], consts["cos_s"], consts["sin_s"],
            consts["bias_s"], consts["conv_w"], consts["gn_attn"], consts["gn_conv"], consts["w_out"],
            consts["w_r"], consts["b_r"], consts["tri"], ck, cv, e1, e2, cnt_in]
    nt = n_s // ts
    out_shape = [
        jax.ShapeDtypeStruct((n_s, D_MODEL), jnp.float32),
        jax.ShapeDtypeStruct((n_s, D_MODEL), jnp.float32),
        jax.ShapeDtypeStruct((nt, 8, ts), jnp.int32),
        jax.ShapeDtypeStruct((nt, 8, ts), jnp.float32),
        jax.ShapeDtypeStruct((N_EXPERTS, LANES), jnp.float32),
        jax.ShapeDtypeStruct((n_s, KV_DIM), jnp.float32),
        jax.ShapeDtypeStruct((n_s, KV_DIM), jnp.float32),
        jax.ShapeDtypeStruct((n_s, CONV_DIM), jnp.float32),
    ]
    out_specs = [
        rows(D_MODEL),
        rows(D_MODEL),
        pl.BlockSpec((None, 8, ts), lambda i, s: (i, 0, 0)),
        pl.BlockSpec((None, 8, ts), lambda i, s: (i, 0, 0)),
        pl.BlockSpec((N_EXPERTS, LANES), lambda i, s: (0, 0)),
        rows(KV_DIM), rows(KV_DIM), rows(CONV_DIM),
    ]
    scratch = [
        pltpu.VMEM((ts, ATTN_DIM), jnp.bfloat16),
        pltpu.VMEM((ts, KV_DIM), jnp.float32), pltpu.VMEM((ts, KV_DIM), jnp.float32),
        pltpu.VMEM((ts, ATTN_DIM), jnp.float32),
        pltpu.VMEM((N_EXPERTS, LANES), jnp.float32),
    ]
    return pl.pallas_call(
        functools.partial(_sample_mixer_kernel, dec=dec),
        grid_spec=pltpu.PrefetchScalarGridSpec(
            num_scalar_prefetch=1, grid=(nt,), in_specs=in_specs, out_specs=out_specs, scratch_shapes=scratch),
        out_shape=out_shape,
        compiler_params=pltpu.CompilerParams(dimension_semantics=("arbitrary",), vmem_limit_bytes=VMEM_LIMIT),
        name="sample_mixer",
    )(sinks, *args)


def _dispatch_kernel(dest_ref, h2p_ref, h2s_ref, xs_in_ref, xs_ref, sem):
    del xs_in_ref
    c = pl.program_id(0)
    n = pl.num_programs(0)
    chunk = DISPATCH_CHUNK
    m_prompt = h2p_ref.shape[0]
    n_tok = m_prompt + h2s_ref.shape[0]
    s = c % 2

    def wait_chunk(slot):
        pltpu.make_async_copy(h2p_ref.at[pl.ds(0, 2 * chunk)], xs_ref.at[pl.ds(0, 2 * chunk)], sem.at[slot]).wait()

    @pl.when(c >= 2)
    def _():
        wait_chunk(s)

    def issue_from(src_ref, src_base):
        def issue(i, carry):
            tok = c * chunk + i
            for k in range(2):
                d = dest_ref[k * n_tok + tok]
                pltpu.make_async_copy(src_ref.at[pl.ds(tok - src_base, 1)], xs_ref.at[pl.ds(d, 1)], sem.at[s]).start()
            return carry
        lax.fori_loop(0, chunk, issue, 0, unroll=8)

    @pl.when(c < m_prompt // chunk)
    def _():
        issue_from(h2p_ref, 0)

    @pl.when(c >= m_prompt // chunk)
    def _():
        issue_from(h2s_ref, m_prompt)

    @pl.when(c == n - 1)
    def _():
        wait_chunk(1 - s)
        wait_chunk(s)


def _dispatch(dest_flat, h2_p, h2_s, xs_init):
    n_tok = h2_p.shape[0] + h2_s.shape[0]
    assert h2_p.shape[0] % DISPATCH_CHUNK == 0 and h2_s.shape[0] % DISPATCH_CHUNK == 0
    assert n_tok // DISPATCH_CHUNK >= 2
    hbm = pl.BlockSpec(memory_space=pl.ANY)
    return pl.pallas_call(
        _dispatch_kernel,
        grid_spec=pltpu.PrefetchScalarGridSpec(
            num_scalar_prefetch=1, grid=(n_tok // DISPATCH_CHUNK,),
            in_specs=[hbm, hbm, hbm], out_specs=hbm,
            scratch_shapes=[pltpu.SemaphoreType.DMA((2,))]),
        out_shape=jax.ShapeDtypeStruct(xs_init.shape, xs_init.dtype),
        input_output_aliases={3: 0},
        compiler_params=pltpu.CompilerParams(dimension_semantics=("arbitrary",), has_side_effects=True),
        name="dispatch",
    )(dest_flat, h2_p, h2_s, xs_init)


def _experts_kernel(be_ref, nu_ref, xs_ref, wg_ref, wu_ref, wd_ref, y_ref, wg_s, wu_s, wd_s):
    j = pl.program_id(0)
    used = j < nu_ref[0]
    changed = jnp.logical_or(j == 0, be_ref[j] != be_ref[jnp.maximum(j - 1, 0)])

    @pl.when(jnp.logical_and(used, changed))
    def _():
        wg_s[...] = wg_ref[...].astype(jnp.bfloat16)
        wu_s[...] = wu_ref[...].astype(jnp.bfloat16)
        wd_s[...] = wd_ref[...].astype(jnp.bfloat16)

    @pl.when(used)
    def _():
        x = xs_ref[...].astype(jnp.bfloat16)
        g = jnp.dot(x, wg_s[...], preferred_element_type=jnp.float32)
        u = jnp.dot(x, wu_s[...], preferred_element_type=jnp.float32)
        a = (g * jax.nn.sigmoid(g) * u).astype(jnp.bfloat16)
        y_ref[...] = jnp.dot(a, wd_s[...], preferred_element_type=jnp.float32)

    @pl.when(jnp.logical_not(used))
    def _():
        y_ref[...] = jnp.zeros_like(y_ref)


def _experts(block_e, n_used, xs, w_gate, w_up, w_down):
    bm = EXPERT_BLOCK
    n_blocks = xs.shape[0] // bm
    row = lambda j, be, nu: (jnp.minimum(j, nu[0] - 1), 0)
    wmap = lambda j, be, nu: (be[j], 0, 0)
    return pl.pallas_call(
        _experts_kernel,
        grid_spec=pltpu.PrefetchScalarGridSpec(
            num_scalar_prefetch=2, grid=(n_blocks,),
            in_specs=[pl.BlockSpec((bm, D_MODEL), row),
                      pl.BlockSpec((None, D_MODEL, EXPERT_HIDDEN), wmap),
                      pl.BlockSpec((None, D_MODEL, EXPERT_HIDDEN), wmap),
                      pl.BlockSpec((None, EXPERT_HIDDEN, D_MODEL), wmap)],
            out_specs=pl.BlockSpec((bm, D_MODEL), lambda j, be, nu: (j, 0)),
            scratch_shapes=[pltpu.VMEM((D_MODEL, EXPERT_HIDDEN), jnp.bfloat16),
                            pltpu.VMEM((D_MODEL, EXPERT_HIDDEN), jnp.bfloat16),
                            pltpu.VMEM((EXPERT_HIDDEN, D_MODEL), jnp.bfloat16)]),
        out_shape=jax.ShapeDtypeStruct(xs.shape, jnp.float32),
        compiler_params=pltpu.CompilerParams(dimension_semantics=("arbitrary",), vmem_limit_bytes=VMEM_LIMIT),
        name="experts",
    )(block_e, n_used, xs, w_gate, w_up, w_down)


def _combine_kernel(dest_ref, x1_ref, gate_ref, rf_ref, fin_ref, yb_ref, o_ref, ybuf, sem, *, n_tok, per_token_gate):
    i = pl.program_id(0)
    n = pl.num_programs(0)
    tc = COMBINE_TILE

    def issue(step, slot):
        def one(r, carry):
            tok = step * tc + r
            for k in range(2):
                d = dest_ref[k * n_tok + tok]
                pltpu.make_async_copy(yb_ref.at[pl.ds(d, 1)], ybuf.at[slot, k, pl.ds(r, 1)], sem.at[slot]).start()
            return carry
        lax.fori_loop(0, tc, one, 0, unroll=8)

    slot = i % 2

    @pl.when(i == 0)
    def _():
        issue(0, 0)

    @pl.when(i + 1 < n)
    def _():
        issue(i + 1, 1 - slot)

    pltpu.make_async_copy(yb_ref.at[pl.ds(0, tc)], ybuf.at[slot, 0], sem.at[slot]).wait()
    pltpu.make_async_copy(yb_ref.at[pl.ds(0, tc)], ybuf.at[slot, 1], sem.at[slot]).wait()

    gates = jnp.transpose(rf_ref[...])
    moe = ybuf[slot, 0] * gates[:, 0:1] + ybuf[slot, 1] * gates[:, 1:2]
    gate_f = gate_ref[...] if per_token_gate else gate_ref[...][5:6]
    o_ref[...] = _rms(x1_ref[...] + gate_f * moe, fin_ref[...])


def _combine(dest_flat, x1, gate, rf, final_norm, yb, *, tiles_per_gate):
    n_tok = x1.shape[0]
    tc = COMBINE_TILE
    per_token = tiles_per_gate == 0
    if per_token:
        gate_spec = pl.BlockSpec((tc, D_MODEL), lambda i, d: (i, 0))
    else:
        gate_spec = pl.BlockSpec((None, 6, D_MODEL), lambda i, d: (i // tiles_per_gate, 0, 0))
    return pl.pallas_call(
        functools.partial(_combine_kernel, n_tok=n_tok, per_token_gate=per_token),
        grid_spec=pltpu.PrefetchScalarGridSpec(
            num_scalar_prefetch=1, grid=(n_tok // tc,),
            in_specs=[pl.BlockSpec((tc, D_MODEL), lambda i, d: (i, 0)),
                      gate_spec,
                      pl.BlockSpec((8, tc), lambda i, d: (0, i)),
                      pl.BlockSpec((1, D_MODEL), lambda i, d: (0, 0)),
                      pl.BlockSpec(memory_space=pl.ANY)],
            out_specs=pl.BlockSpec((tc, D_MODEL), lambda i, d: (i, 0)),
            scratch_shapes=[pltpu.VMEM((2, 2, tc, D_MODEL), jnp.float32), pltpu.SemaphoreType.DMA((2,))]),
        out_shape=jax.ShapeDtypeStruct((n_tok, D_MODEL), jnp.float32),
        compiler_params=pltpu.CompilerParams(dimension_semantics=("arbitrary",), vmem_limit_bytes=VMEM_LIMIT),
        name="combine",
    )(dest_flat, x1, gate, rf, final_norm, yb)


def _rope_tables(pos):
    half = HEAD_DIM // 2
    inv_freq = ROPE_THETA ** (-jnp.arange(half, dtype=jnp.float32) / half)
    ang = pos.astype(jnp.float32)[:, None] * inv_freq[None, :]
    cos, sin = jnp.cos(ang), jnp.sin(ang)
    cos = jnp.concatenate([cos, cos, cos, cos], axis=1)
    sin = jnp.concatenate([-sin, sin, -sin, sin], axis=1)
    return cos, sin


def _prompt_bias():
    r = np.arange(WINDOW)[:, None]
    j = np.arange(2 * WINDOW)[None, :]
    band = (j > r) & (j <= r + WINDOW)
    later = np.where(band, 0.0, NEG).astype(np.float32)
    first = np.where(band & (j >= WINDOW), 0.0, NEG).astype(np.float32)
    return jnp.asarray(np.stack([first, later]))


def _sample_bias(dec):
    cb = SAMPLE_CHUNK
    qb = np.repeat(np.arange(cb), dec)[:, None]
    qi = np.tile(np.arange(dec), cb)[:, None]
    kb = np.concatenate([np.repeat(np.arange(cb), WINDOW), np.repeat(np.arange(cb), dec)])[None, :]
    kj = np.concatenate([np.tile(np.arange(WINDOW), cb), np.tile(np.arange(dec), cb)])[None, :]
    is_new = (np.arange(cb * WINDOW + cb * dec) >= cb * WINDOW)[None, :]
    ok = (kb == qb) & np.where(is_new, kj <= qi, kj > qi)
    ok = np.pad(ok, ((0, 0), (0, SAMPLE_KEYS - ok.shape[1])))
    return jnp.asarray(np.where(ok, 0.0, NEG).astype(np.float32))


def kernel(x_prompt, x_sample, cache_k, cache_v, state_conv, c_prompt, c_sample, attn_norm, ffn_norm, w_mod, b_mod, w_in, conv_w, attn_sinks, out_norm_attn, out_norm_conv, w_out, w_group, b_group, w_expert, b_expert, w_gate, w_up, w_down, final_norm):
    assert attn_norm.shape[0] == 1, "one layer"
    bsz, seq, _ = x_prompt.shape
    nseq, dec, _ = x_sample.shape
    m_prompt = bsz * seq
    n_s = nseq * dec
    m_total = m_prompt + n_s
    tm = MIXER_TILE
    assert seq % tm == 0 and n_s % SAMPLE_TILE == 0 and SAMPLE_TILE % (SAMPLE_CHUNK * dec) == 0
    assert cache_k.shape[2] == WINDOW and m_prompt % SAMPLE_TILE == 0 and m_total % COMBINE_TILE == 0
    bf = jnp.bfloat16

    w_r = jnp.zeros((ROUTER_ROWS, D_MODEL), jnp.float32)
    w_r = w_r.at[0:N_GROUPS].set(w_group[0].T).at[8:8 + N_EXPERTS].set(w_expert[0].T).astype(bf)
    b_r = jnp.zeros((ROUTER_ROWS,), jnp.float32).at[0:N_GROUPS].set(b_group[0]).at[8:8 + N_EXPERTS].set(b_expert[0])
    b_r = jnp.broadcast_to(b_r[:, None], (ROUTER_ROWS, LANES))
    cos_p, sin_p = _rope_tables(jnp.arange(seq, dtype=jnp.int32))
    cos_s, sin_s = _rope_tables(PAST_LEN + jnp.arange(dec, dtype=jnp.int32))
    tri = jnp.asarray(np.triu(np.ones((tm, tm), np.float32), 1)).astype(bf)
    consts = dict(
        anorm=attn_norm[0][None], fnorm=ffn_norm[0][None], w_in=w_in[0].astype(bf),
        cos_p=cos_p, sin_p=sin_p, cos_s=jnp.tile(cos_s, (nseq, 1)), sin_s=jnp.tile(sin_s, (nseq, 1)),
        bias_p=_prompt_bias(), bias_s=_sample_bias(dec), conv_w=conv_w[0],
        gn_attn=out_norm_attn[0][None], gn_conv=out_norm_conv[0][None], w_out=w_out[0].astype(bf),
        w_r=w_r, b_r=b_r, tri=tri)
    sinks = attn_sinks[0]

    mod = _adaln(jnp.concatenate([c_prompt, c_sample], axis=0), w_mod[0], b_mod[0])
    mod_p = mod[:bsz].reshape(bsz, 6, D_MODEL)
    mod_s = jnp.repeat(mod[bsz:], dec, axis=0)

    x1_p, h2_p, ri_p, rf_p, cnt_p, k_p, v_p, u_p = _prompt_mixer(x_prompt, mod_p, sinks, consts)
    st = state_conv[0]
    zrow = jnp.zeros((nseq, 1, CONV_DIM), jnp.float32)
    e1 = jnp.concatenate([st[:, 1:2]] + [zrow] * (dec - 1), axis=1).reshape(n_s, CONV_DIM)
    e2 = jnp.concatenate([st[:, 0:1], st[:, 1:2]] + [zrow] * (dec - 2), axis=1).reshape(n_s, CONV_DIM)
    ck = cache_k[0].reshape(nseq * WINDOW, KV_DIM)
    cv = cache_v[0].reshape(nseq * WINDOW, KV_DIM)
    x1_s, h2_s, ri_s, rf_s, cnt, k_s, v_s, u_s = _sample_mixer(
        x_sample.reshape(n_s, D_MODEL), mod_s, sinks, consts, ck, cv, e1, e2, cnt_p, dec)

    bm = EXPERT_BLOCK
    counts = cnt[:, 0].astype(jnp.int32)
    padded = (counts + bm - 1) // bm * bm
    pend = jnp.cumsum(padded)
    pstart = pend - padded
    n_blocks = -(-(2 * m_total + N_EXPERTS * (bm - 1)) // bm)
    n_used = (pend[-1] // bm).astype(jnp.int32)
    blk = jnp.arange(n_blocks, dtype=jnp.int32)
    block_e = jnp.minimum(jnp.searchsorted(pend, blk * bm, side="right"), N_EXPERTS - 1).astype(jnp.int32)
    block_e = jnp.where(blk < n_used, block_e, block_e[jnp.maximum(n_used - 1, 0)])
    by_token = lambda a: jnp.transpose(a, (1, 0, 2)).reshape(8, -1)
    ri = jnp.concatenate([by_token(ri_p), by_token(ri_s)], axis=1)
    rf_flat = jnp.concatenate([by_token(rf_p), by_token(rf_s)], axis=1)
    dest_flat = (pstart[ri[0:2]] + ri[2:4]).reshape(2 * m_total)

    xs = _dispatch(dest_flat, h2_p, h2_s, jnp.zeros((n_blocks * bm, D_MODEL), jnp.float32))
    yb = _experts(block_e, n_used.reshape(1), xs, w_gate[0], w_up[0], w_down[0])

    fin = final_norm[None]
    dest_p = jnp.concatenate([dest_flat[:m_prompt], dest_flat[m_total:m_total + m_prompt]])
    dest_s = jnp.concatenate([dest_flat[m_prompt:m_total], dest_flat[m_total + m_prompt:]])
    y_p = _combine(dest_p, x1_p, mod_p, rf_flat[:, :m_prompt], fin, yb, tiles_per_gate=seq // COMBINE_TILE)
    gate_s = mod_s[:, 5 * D_MODEL:]
    y_s = _combine(dest_s, x1_s, gate_s, rf_flat[:, m_prompt:], fin, yb, tiles_per_gate=0)

    keep = WINDOW
    new_k_p = k_p.reshape(1, bsz, keep, N_KV_HEADS, HEAD_DIM)
    new_v_p = v_p.reshape(1, bsz, keep, N_KV_HEADS, HEAD_DIM)
    new_u_p = u_p[:, 6:8][None]
    ksn = k_s.reshape(nseq, dec, N_KV_HEADS, HEAD_DIM)
    vsn = v_s.reshape(nseq, dec, N_KV_HEADS, HEAD_DIM)
    new_k_s = jnp.concatenate([cache_k[0][:, dec:], ksn], axis=1)[None]
    new_v_s = jnp.concatenate([cache_v[0][:, dec:], vsn], axis=1)[None]
    u_all = jnp.concatenate([st, u_s.reshape(nseq, dec, CONV_DIM)], axis=1)
    new_u_s = u_all[:, -2:][None]
    return (y_p.reshape(bsz, seq, D_MODEL), y_s.reshape(nseq, dec, D_MODEL),
            new_k_p, new_v_p, new_u_p, new_k_s, new_v_s, new_u_s)
```

```python
import functools

import jax
import jax.numpy as jnp
import numpy as np
from jax import lax
from jax.experimental import pallas as pl
from jax.experimental.pallas import tpu as pltpu

D_MODEL = 1024
HEAD_DIM = 64
ATTN_DIM = 512
N_Q_HEADS = 8
N_KV_HEADS = 2
KV_DIM = 128
CONV_DIM = 512
WINDOW = 128
IN_PROJ_DIM = ATTN_DIM + 2 * KV_DIM + 3 * CONV_DIM
N_GROUPS = 4
EXPERTS_PER_GROUP = 8
N_EXPERTS = 32
EXPERT_HIDDEN = 512
RMS_EPS = 1e-5
ROPE_THETA = 10000.0
PAST_LEN = 16384

LANES = 128
ROUTER_ROWS = 48
NEG = -1e30

MIXER_TILE = 512
SAMPLE_TILE = 128
SAMPLE_CHUNK = 8
SAMPLE_KEYS = 1152
EXPERT_BLOCK = 256
COMBINE_TILE = 256
DISPATCH_CHUNK = 512
VMEM_LIMIT = 56 * 1024 * 1024


def _rms(x, gain):
    ms = jnp.mean(x * x, axis=-1, keepdims=True)
    return x * lax.rsqrt(ms + RMS_EPS) * gain


def _half_lane_mask(rows=1):
    return lax.broadcasted_iota(jnp.int32, (rows, LANES), 1) < HEAD_DIM


def _rope(x, cos, sin_signed):
    t, w = x.shape
    reps = w // LANES
    lane = lax.broadcasted_iota(jnp.int32, (t, w), 1)
    upper = (lane % HEAD_DIM) >= (HEAD_DIM // 2)
    partner = jnp.where(upper, pltpu.roll(x, HEAD_DIM // 2, 1), pltpu.roll(x, w - HEAD_DIM // 2, 1))
    if reps > 1:
        cos = jnp.concatenate([cos] * reps, axis=1)
        sin_signed = jnp.concatenate([sin_signed] * reps, axis=1)
    return x * cos + partner * sin_signed


def _swap_halves(x):
    return pltpu.roll(x, HEAD_DIM, 1)


def _group_norm64(x, gain):
    t, w = x.shape
    lo = _half_lane_mask(t)
    outs = []
    for c in range(w // LANES):
        xc = x[:, c * LANES:(c + 1) * LANES]
        sq = xc * xc
        s_lo = jnp.sum(jnp.where(lo, sq, 0.0), axis=-1, keepdims=True)
        s_hi = jnp.sum(jnp.where(lo, 0.0, sq), axis=-1, keepdims=True)
        r = jnp.where(lo, lax.rsqrt(s_lo * (1.0 / HEAD_DIM) + RMS_EPS), lax.rsqrt(s_hi * (1.0 / HEAD_DIM) + RMS_EPS))
        outs.append(xc * r)
    return jnp.concatenate(outs, axis=1) * gain


def _attend(q_rows, keys, keys_sw, vals, vals_sw, bias, sinks_ref, group):
    r = q_rows.shape[0]
    lo = _half_lane_mask(r)
    outs = []
    for parity in range(2):
        heads = [4 * group + parity, 4 * group + parity + 2]
        kk, vv = (keys, vals) if parity == group else (keys_sw, vals_sw)
        half = lo if parity == 0 else jnp.logical_not(lo)
        qs = []
        for h in heads:
            pair = q_rows[:, (h // 2) * LANES:(h // 2 + 1) * LANES]
            qs.append(jnp.where(half, pair, jnp.zeros_like(pair)))
        qcat = jnp.concatenate(qs, axis=0)
        s = lax.dot_general(qcat, kk, (((1,), (1,)), ((), ())), preferred_element_type=jnp.float32)
        ps, dens = [], []
        for i, h in enumerate(heads):
            sh = s[i * r:(i + 1) * r] + bias
            sink = sinks_ref[h]
            m = jnp.maximum(jnp.max(sh, axis=-1, keepdims=True), sink)
            p = jnp.exp(sh - m)
            dens.append(jnp.sum(p, axis=-1, keepdims=True) + jnp.exp(sink - m))
            ps.append(p.astype(jnp.bfloat16))
        o = jnp.dot(jnp.concatenate(ps, axis=0), vv, preferred_element_type=jnp.float32)
        for i, h in enumerate(heads):
            oh = o[i * r:(i + 1) * r] / dens[i]
            ms = jnp.sum(jnp.where(half, oh * oh, 0.0), axis=-1, keepdims=True) * (1.0 / HEAD_DIM)
            outs.append((h, oh * lax.rsqrt(ms + RMS_EPS)))
    outs.sort(key=lambda t: t[0])
    return [o for _, o in outs]


def _merge_heads(head_outs, lo):
    pairs = [jnp.where(lo, head_outs[2 * j], head_outs[2 * j + 1]) for j in range(4)]
    return jnp.concatenate(pairs, axis=1)


def _route(h2b, wr_ref, br_ref, tri_ref, cnt_ref, ri_ref, rf_ref):
    t = h2b.shape[0]
    lg = lax.dot_general(wr_ref[...], h2b, (((1,), (1,)), ((), ())), preferred_element_type=jnp.float32)
    lg = lg + br_ref[...][:, 0:1]
    r8 = lax.broadcasted_iota(jnp.int32, (8, t), 0)
    lgrp = jnp.where(r8 < N_GROUPS, lg[0:8], -jnp.inf)
    gmax = jnp.max(lgrp, axis=0, keepdims=True)
    grp = jnp.min(jnp.where(lgrp == gmax, r8, 8), axis=0, keepdims=True)
    pg_sel = 1.0 / jnp.sum(jnp.exp(lgrp - gmax), axis=0, keepdims=True)
    r32 = lax.broadcasted_iota(jnp.int32, (N_EXPERTS, t), 0)
    le = jnp.where((r32 // EXPERTS_PER_GROUP) == grp, lg[8:8 + N_EXPERTS], -jnp.inf)
    v1 = jnp.max(le, axis=0, keepdims=True)
    i1 = jnp.min(jnp.where(le == v1, r32, N_EXPERTS), axis=0, keepdims=True)
    le2 = jnp.where(r32 == i1, -jnp.inf, le)
    v2 = jnp.max(le2, axis=0, keepdims=True)
    i2 = jnp.min(jnp.where(le2 == v2, r32, N_EXPERTS), axis=0, keepdims=True)
    e21 = jnp.exp(v2 - v1)
    gate1 = pg_sel / (1.0 + e21)
    gate2 = pg_sel * e21 / (1.0 + e21)
    hot1 = r32 == i1
    hot2 = r32 == i2
    onehot = jnp.where(jnp.logical_or(hot1, hot2), 1.0, 0.0)
    before = jnp.dot(onehot.astype(jnp.bfloat16), tri_ref[...], preferred_element_type=jnp.float32)
    pos = before + cnt_ref[...][:, 0:1]
    rank1 = jnp.sum(jnp.where(hot1, pos, 0.0), axis=0, keepdims=True)
    rank2 = jnp.sum(jnp.where(hot2, pos, 0.0), axis=0, keepdims=True)
    cnt_ref[...] = cnt_ref[...] + jnp.sum(onehot, axis=1, keepdims=True)
    zi = jnp.zeros((4, t), jnp.int32)
    ri_ref[...] = jnp.concatenate([i1, i2, rank1.astype(jnp.int32), rank2.astype(jnp.int32), zi], axis=0)
    rf_ref[...] = jnp.concatenate([gate1, gate2, jnp.zeros((6, t), jnp.float32)], axis=0)


def _split_proj(proj):
    a = ATTN_DIM
    q = proj[:, :a]
    k = proj[:, a:a + KV_DIM]
    v = proj[:, a + KV_DIM:a + 2 * KV_DIM]
    c0 = a + 2 * KV_DIM
    h_conv = proj[:, c0:c0 + CONV_DIM]
    gate_b = proj[:, c0 + CONV_DIM:c0 + 2 * CONV_DIM]
    gate_c = proj[:, c0 + 2 * CONV_DIM:c0 + 3 * CONV_DIM]
    return q, k, v, gate_c * h_conv, gate_b


def _mixer_tail(x, attn_n, conv_n, mod, wout_ref, fnorm_ref, wr_ref, br_ref, tri_ref, cnt_ref,
                x1_ref, h2_ref, ri_ref, rf_ref):
    gate_a, shift_f, scale_f = mod
    cat = jnp.concatenate([attn_n.astype(jnp.bfloat16), conv_n.astype(jnp.bfloat16)], axis=1)
    mix = jnp.dot(cat, wout_ref[...], preferred_element_type=jnp.float32)
    x1 = x + gate_a * mix
    x1_ref[...] = x1
    h2 = _rms(x1, fnorm_ref[...]) * (1.0 + scale_f) + shift_f
    h2_ref[...] = h2
    _route(h2.astype(jnp.bfloat16), wr_ref, br_ref, tri_ref, cnt_ref, ri_ref, rf_ref)


def _adaln_kernel(c_ref, w_ref, b_ref, o_ref):
    c = c_ref[...]
    a = (c * jax.nn.sigmoid(c)).astype(jnp.bfloat16)
    o_ref[...] = jnp.dot(a, w_ref[...].astype(jnp.bfloat16), preferred_element_type=jnp.float32) + b_ref[...]


def _adaln(c, w_mod, b_mod):
    n = c.shape[0]
    tn = 1536
    return pl.pallas_call(
        _adaln_kernel,
        grid=(w_mod.shape[1] // tn,),
        in_specs=[pl.BlockSpec((n, D_MODEL), lambda j: (0, 0)),
                  pl.BlockSpec((D_MODEL, tn), lambda j: (0, j)),
                  pl.BlockSpec((1, tn), lambda j: (0, j))],
        out_specs=pl.BlockSpec((n, tn), lambda j: (0, j)),
        out_shape=jax.ShapeDtypeStruct((n, w_mod.shape[1]), jnp.float32),
        compiler_params=pltpu.CompilerParams(dimension_semantics=("arbitrary",), vmem_limit_bytes=VMEM_LIMIT),
        name="adaln",
    )(c, w_mod, b_mod.reshape(1, -1))


def _prompt_mixer_kernel(sinks_ref, x_ref, mod_ref, anorm_ref, fnorm_ref, win_ref, cos_ref, sin_ref, bias_ref,
                         convw_ref, gna_ref, gnc_ref, wout_ref, wr_ref, br_ref, tri_ref,
                         x1_ref, h2_ref, ri_ref, rf_ref, cnt_out_ref, knew_ref, vnew_ref, unew_ref,
                         q_s, k_s, ksw_s, v_s, vsw_s, u_s, att_s, cnt_s):
    b = pl.program_id(0)
    t = pl.program_id(1)
    tm = MIXER_TILE
    nblk = tm // WINDOW

    @pl.when(jnp.logical_and(b == 0, t == 0))
    def _():
        cnt_s[...] = jnp.zeros_like(cnt_s)

    @pl.when(t == 0)
    def _():
        z = jnp.zeros((WINDOW, KV_DIM), jnp.bfloat16)
        k_s[0:WINDOW, :] = z
        ksw_s[0:WINDOW, :] = z
        v_s[0:WINDOW, :] = z
        vsw_s[0:WINDOW, :] = z
        u_s[0:8, :] = jnp.zeros((8, CONV_DIM), jnp.float32)

    x = x_ref[...]
    mod = mod_ref[...]
    h = _rms(x, anorm_ref[...]) * (1.0 + mod[1:2]) + mod[0:1]
    proj = jnp.dot(h.astype(jnp.bfloat16), win_ref[...], preferred_element_type=jnp.float32)
    q, k, v, u, gate_b = _split_proj(proj)
    cos = cos_ref[...]
    sin = sin_ref[...]
    q = _rope(q, cos, sin) * (HEAD_DIM ** -0.5)
    k = _rope(k, cos, sin)
    q_s[...] = q.astype(jnp.bfloat16)
    k_s[WINDOW:, :] = k.astype(jnp.bfloat16)
    ksw_s[WINDOW:, :] = _swap_halves(k).astype(jnp.bfloat16)
    v_s[WINDOW:, :] = v.astype(jnp.bfloat16)
    vsw_s[WINDOW:, :] = _swap_halves(v).astype(jnp.bfloat16)
    knew_ref[...] = k[tm - WINDOW:, :]
    vnew_ref[...] = v[tm - WINDOW:, :]

    u_s[8:, :] = u
    unew_ref[...] = u[tm - 8:, :]
    cw = convw_ref[...]
    conv = (u_s[6:6 + tm, :] * cw[0:1] + u_s[7:7 + tm, :] * cw[1:2] + u * cw[2:3]) * gate_b
    u_s[0:8, :] = u[tm - 8:, :]
    conv_n = _group_norm64(conv, gnc_ref[...])

    lo = _half_lane_mask(WINDOW)
    first = t == 0

    def block(i, carry):
        r0 = pl.multiple_of(i * WINDOW, WINDOW)
        bias = bias_ref[jnp.where(jnp.logical_and(first, i == 0), 0, 1)]
        qb = q_s[pl.ds(r0, WINDOW), :]
        keys = k_s[pl.ds(r0, 2 * WINDOW), :]
        keys_sw = ksw_s[pl.ds(r0, 2 * WINDOW), :]
        vals = v_s[pl.ds(r0, 2 * WINDOW), :]
        vals_sw = vsw_s[pl.ds(r0, 2 * WINDOW), :]
        outs = []
        for g in range(N_KV_HEADS):
            outs += _attend(qb, keys, keys_sw, vals, vals_sw, bias, sinks_ref, g)
        att_s[pl.ds(r0, WINDOW), :] = _merge_heads(outs, lo)
        return carry

    lax.fori_loop(0, nblk, block, 0)
    k_s[0:WINDOW, :] = k_s[tm:tm + WINDOW, :]
    ksw_s[0:WINDOW, :] = ksw_s[tm:tm + WINDOW, :]
    v_s[0:WINDOW, :] = v_s[tm:tm + WINDOW, :]
    vsw_s[0:WINDOW, :] = vsw_s[tm:tm + WINDOW, :]

    attn_n = att_s[...] * gna_ref[...]
    _mixer_tail(x, attn_n, conv_n, (mod[2:3], mod[3:4], mod[4:5]), wout_ref, fnorm_ref, wr_ref, br_ref, tri_ref,
                cnt_s, x1_ref, h2_ref, ri_ref, rf_ref)
    cnt_out_ref[...] = cnt_s[...]


def _prompt_mixer(x, mod, sinks, consts):
    bsz, seq, _ = x.shape
    tm = MIXER_TILE
    nt = seq // tm
    full = lambda shape: pl.BlockSpec(shape, lambda b, t, s: (0,) * len(shape))
    in_specs = [
        pl.BlockSpec((None, tm, D_MODEL), lambda b, t, s: (b, t, 0)),
        pl.BlockSpec((None, 6, D_MODEL), lambda b, t, s: (b, 0, 0)),
        full((1, D_MODEL)), full((1, D_MODEL)),
        full((D_MODEL, IN_PROJ_DIM)),
        pl.BlockSpec((tm, LANES), lambda b, t, s: (t, 0)),
        pl.BlockSpec((tm, LANES), lambda b, t, s: (t, 0)),
        full((2, WINDOW, 2 * WINDOW)),
        full((3, CONV_DIM)), full((1, ATTN_DIM)), full((1, CONV_DIM)),
        full((D_MODEL, D_MODEL)),
        full((ROUTER_ROWS, D_MODEL)), full((ROUTER_ROWS, LANES)),
        full((tm, tm)),
    ]
    out_shape = [
        jax.ShapeDtypeStruct((bsz * seq, D_MODEL), jnp.float32),
        jax.ShapeDtypeStruct((bsz * seq, D_MODEL), jnp.float32),
        jax.ShapeDtypeStruct((bsz * nt, 8, tm), jnp.int32),
        jax.ShapeDtypeStruct((bsz * nt, 8, tm), jnp.float32),
        jax.ShapeDtypeStruct((N_EXPERTS, LANES), jnp.float32),
        jax.ShapeDtypeStruct((bsz, WINDOW, KV_DIM), jnp.float32),
        jax.ShapeDtypeStruct((bsz, WINDOW, KV_DIM), jnp.float32),
        jax.ShapeDtypeStruct((bsz, 8, CONV_DIM), jnp.float32),
    ]
    out_specs = [
        pl.BlockSpec((tm, D_MODEL), lambda b, t, s: (b * nt + t, 0)),
        pl.BlockSpec((tm, D_MODEL), lambda b, t, s: (b * nt + t, 0)),
        pl.BlockSpec((None, 8, tm), lambda b, t, s: (b * nt + t, 0, 0)),
        pl.BlockSpec((None, 8, tm), lambda b, t, s: (b * nt + t, 0, 0)),
        pl.BlockSpec((N_EXPERTS, LANES), lambda b, t, s: (0, 0)),
        pl.BlockSpec((None, WINDOW, KV_DIM), lambda b, t, s: (b, 0, 0)),
        pl.BlockSpec((None, WINDOW, KV_DIM), lambda b, t, s: (b, 0, 0)),
        pl.BlockSpec((None, 8, CONV_DIM), lambda b, t, s: (b, 0, 0)),
    ]
    scratch = [
        pltpu.VMEM((tm, ATTN_DIM), jnp.bfloat16),
        pltpu.VMEM((tm + WINDOW, KV_DIM), jnp.bfloat16), pltpu.VMEM((tm + WINDOW, KV_DIM), jnp.bfloat16),
        pltpu.VMEM((tm + WINDOW, KV_DIM), jnp.bfloat16), pltpu.VMEM((tm + WINDOW, KV_DIM), jnp.bfloat16),
        pltpu.VMEM((tm + 8, CONV_DIM), jnp.float32),
        pltpu.VMEM((tm, ATTN_DIM), jnp.float32),
        pltpu.VMEM((N_EXPERTS, LANES), jnp.float32),
    ]
    return pl.pallas_call(
        _prompt_mixer_kernel,
        grid_spec=pltpu.PrefetchScalarGridSpec(
            num_scalar_prefetch=1, grid=(bsz, nt), in_specs=in_specs, out_specs=out_specs, scratch_shapes=scratch),
        out_shape=out_shape,
        compiler_params=pltpu.CompilerParams(dimension_semantics=("arbitrary", "arbitrary"),
                                             vmem_limit_bytes=VMEM_LIMIT),
        name="prompt_mixer",
    )(sinks, x, mod, consts["anorm"], consts["fnorm"], consts["w_in"], consts["cos_p"], consts["sin_p"],
      consts["bias_p"], consts["conv_w"], consts["gn_attn"], consts["gn_conv"], consts["w_out"],
      consts["w_r"], consts["b_r"], consts["tri"])


def _sample_mixer_kernel(sinks_ref, x_ref, mod_ref, anorm_ref, fnorm_ref, win_ref, cos_ref, sin_ref, bias_ref,
                         convw_ref, gna_ref, gnc_ref, wout_ref, wr_ref, br_ref, tri_ref,
                         ck_ref, cv_ref, e1_ref, e2_ref, cnt_in_ref,
                         x1_ref, h2_ref, ri_ref, rf_ref, cnt_out_ref, knew_ref, vnew_ref, unew_ref,
                         q_s, k_s, v_s, att_s, cnt_s, *, dec):
    n = x_ref.shape[0]
    cb = SAMPLE_CHUNK
    rows = cb * dec
    pad_keys = SAMPLE_KEYS - cb * WINDOW - rows

    @pl.when(pl.program_id(0) == 0)
    def _():
        cnt_s[...] = cnt_in_ref[...]

    x = x_ref[...]
    mod = mod_ref[...]
    md = lambda j: mod[:, j * D_MODEL:(j + 1) * D_MODEL]
    h = _rms(x, anorm_ref[...]) * (1.0 + md(1)) + md(0)
    proj = jnp.dot(h.astype(jnp.bfloat16), win_ref[...], preferred_element_type=jnp.float32)
    q, k, v, u, gate_b = _split_proj(proj)
    cos = cos_ref[...]
    sin = sin_ref[...]
    q = _rope(q, cos, sin) * (HEAD_DIM ** -0.5)
    k = _rope(k, cos, sin)
    q_s[...] = q.astype(jnp.bfloat16)
    k_s[...] = k
    v_s[...] = v
    knew_ref[...] = k
    vnew_ref[...] = v
    unew_ref[...] = u

    pos = lax.broadcasted_iota(jnp.int32, (n, CONV_DIM), 0) % dec
    um1 = jnp.where(pos >= 1, pltpu.roll(u, 1, 0), 0.0) + e1_ref[...]
    um2 = jnp.where(pos >= 2, pltpu.roll(u, 2, 0), 0.0) + e2_ref[...]
    cw = convw_ref[...]
    conv = (um2 * cw[0:1] + um1 * cw[1:2] + u * cw[2:3]) * gate_b
    conv_n = _group_norm64(conv, gnc_ref[...])

    lo = _half_lane_mask(rows)
    bias = bias_ref[...]

    def chunk(c, carry):
        r0 = pl.multiple_of(c * rows, rows)
        c0 = pl.multiple_of(c * (cb * WINDOW), cb * WINDOW)
        zpad = jnp.zeros((pad_keys, KV_DIM), jnp.float32)
        kc = jnp.concatenate([ck_ref[pl.ds(c0, cb * WINDOW), :], k_s[pl.ds(r0, rows), :], zpad], axis=0)
        vc = jnp.concatenate([cv_ref[pl.ds(c0, cb * WINDOW), :], v_s[pl.ds(r0, rows), :], zpad], axis=0)
        keys, keys_sw = kc.astype(jnp.bfloat16), _swap_halves(kc).astype(jnp.bfloat16)
        vals, vals_sw = vc.astype(jnp.bfloat16), _swap_halves(vc).astype(jnp.bfloat16)
        qb = q_s[pl.ds(r0, rows), :]
        outs = []
        for g in range(N_KV_HEADS):
            outs += _attend(qb, keys, keys_sw, vals, vals_sw, bias, sinks_ref, g)
        att_s[pl.ds(r0, rows), :] = _merge_heads(outs, lo)
        return carry

    lax.fori_loop(0, n // rows, chunk, 0)
    attn_n = att_s[...] * gna_ref[...]
    _mixer_tail(x, attn_n, conv_n, (md(2), md(3), md(4)), wout_ref, fnorm_ref, wr_ref, br_ref, tri_ref,
                cnt_s, x1_ref, h2_ref, ri_ref, rf_ref)
    cnt_out_ref[...] = cnt_s[...]


def _sample_mixer(x, mod, sinks, consts, ck, cv, e1, e2, cnt_in, dec):
    n_s = x.shape[0]
    ts = SAMPLE_TILE
    seqs = ts // dec
    full = lambda a: pl.BlockSpec(a.shape, lambda i, s: (0,) * a.ndim)
    rows = lambda width: pl.BlockSpec((ts, width), lambda i, s: (i, 0))
    in_specs = [
        rows(D_MODEL), rows(6 * D_MODEL),
        full(consts["anorm"]), full(consts["fnorm"]), full(consts["w_in"]),
        rows(LANES), rows(LANES), full(consts["bias_s"]),
        full(consts["conv_w"]), full(consts["gn_attn"]), full(consts["gn_conv"]), full(consts["w_out"]),
        full(consts["w_r"]), full(consts["b_r"]),
        pl.BlockSpec((ts, ts), lambda i, s: (0, 0)),
        pl.BlockSpec((seqs * WINDOW, KV_DIM), lambda i, s: (i, 0)),
        pl.BlockSpec((seqs * WINDOW, KV_DIM), lambda i, s: (i, 0)),
        rows(CONV_DIM), rows(CONV_DIM), full(cnt_in),
    ]
    args = [x, mod, consts["anorm"], consts["fnorm"], consts["w_in"], consts["cos_s"], consts["sin_s"],
            consts["bias_s"], consts["conv_w"], consts["gn_attn"], consts["gn_conv"], consts["w_out"],
            consts["w_r"], consts["b_r"], consts["tri"], ck, cv, e1, e2, cnt_in]
    nt = n_s // ts
    out_shape = [
        jax.ShapeDtypeStruct((n_s, D_MODEL), jnp.float32),
        jax.ShapeDtypeStruct((n_s, D_MODEL), jnp.float32),
        jax.ShapeDtypeStruct((nt, 8, ts), jnp.int32),
        jax.ShapeDtypeStruct((nt, 8, ts), jnp.float32),
        jax.ShapeDtypeStruct((N_EXPERTS, LANES), jnp.float32),
        jax.ShapeDtypeStruct((n_s, KV_DIM), jnp.float32),
        jax.ShapeDtypeStruct((n_s, KV_DIM), jnp.float32),
        jax.ShapeDtypeStruct((n_s, CONV_DIM), jnp.float32),
    ]
    out_specs = [
        rows(D_MODEL),
        rows(D_MODEL),
        pl.BlockSpec((None, 8, ts), lambda i, s: (i, 0, 0)),
        pl.BlockSpec((None, 8, ts), lambda i, s: (i, 0, 0)),
        pl.BlockSpec((N_EXPERTS, LANES), lambda i, s: (0, 0)),
        rows(KV_DIM), rows(KV_DIM), rows(CONV_DIM),
    ]
    scratch = [
        pltpu.VMEM((ts, ATTN_DIM), jnp.bfloat16),
        pltpu.VMEM((ts, KV_DIM), jnp.float32), pltpu.VMEM((ts, KV_DIM), jnp.float32),
        pltpu.VMEM((ts, ATTN_DIM), jnp.float32),
        pltpu.VMEM((N_EXPERTS, LANES), jnp.float32),
    ]
    return pl.pallas_call(
        functools.partial(_sample_mixer_kernel, dec=dec),
        grid_spec=pltpu.PrefetchScalarGridSpec(
            num_scalar_prefetch=1, grid=(nt,), in_specs=in_specs, out_specs=out_specs, scratch_shapes=scratch),
        out_shape=out_shape,
        compiler_params=pltpu.CompilerParams(dimension_semantics=("arbitrary",), vmem_limit_bytes=VMEM_LIMIT),
        name="sample_mixer",
    )(sinks, *args)


def _dispatch_kernel(dest_ref, cnt_ref, padded_ref, start_ref, h2p_ref, h2s_ref, xs_ref,
                     vbuf, zbuf, sem_in, sem_out, sem_z):
    c = pl.program_id(0)
    n = pl.num_programs(0)
    chunk = DISPATCH_CHUNK
    n_p = h2p_ref.shape[0] // chunk
    n_tok = h2p_ref.shape[0] + h2s_ref.shape[0]
    nbuf = vbuf.shape[0]

    def load(step, slot):
        @pl.when(step < n_p)
        def _():
            pltpu.make_async_copy(h2p_ref.at[pl.ds(step * chunk, chunk)], vbuf.at[slot], sem_in.at[slot]).start()

        @pl.when(step >= n_p)
        def _():
            pltpu.make_async_copy(h2s_ref.at[pl.ds((step - n_p) * chunk, chunk)], vbuf.at[slot],
                                  sem_in.at[slot]).start()

    def wait_scatter(slot):
        for _ in range(2):
            pltpu.make_async_copy(vbuf.at[slot], xs_ref.at[pl.ds(0, chunk)], sem_out.at[slot]).wait()

    slot = c % nbuf
    nxt = (c + 1) % nbuf

    @pl.when(c == 0)
    def _():
        load(0, 0)
        zbuf[...] = jnp.zeros_like(zbuf)

    @pl.when(c >= nbuf - 1)
    def _():
        wait_scatter(nxt)

    @pl.when(c + 1 < n)
    def _():
        load(c + 1, nxt)

    pltpu.make_async_copy(h2p_ref.at[pl.ds(0, chunk)], vbuf.at[slot], sem_in.at[slot]).wait()

    def issue(i, carry):
        tok = c * chunk + i
        for k in range(2):
            d = dest_ref[k * n_tok + tok]
            pltpu.make_async_copy(vbuf.at[slot, pl.ds(i, 1)], xs_ref.at[pl.ds(d, 1)], sem_out.at[slot]).start()
        return carry

    lax.fori_loop(0, chunk, issue, 0, unroll=8)

    @pl.when(c == n - 1)
    def _():
        for back in range(nbuf - 2, -1, -1):
            @pl.when(c >= back)
            def _():
                wait_scatter((c - back) % nbuf)

        def zero_row(e, r):
            return pltpu.make_async_copy(zbuf.at[pl.ds(0, 1)], xs_ref.at[pl.ds(start_ref[e] + r, 1)], sem_z.at[0])

        def per_expert(wait):
            def body(e, carry):
                def row(r, carry2):
                    if wait:
                        zero_row(e, r).wait()
                    else:
                        zero_row(e, r).start()
                    return carry2
                lax.fori_loop(cnt_ref[e], padded_ref[e], row, 0)
                return carry
            lax.fori_loop(0, N_EXPERTS, body, 0)

        per_expert(False)
        per_expert(True)

        bm = EXPERT_BLOCK
        vbuf[0] = jnp.zeros((chunk, D_MODEL), jnp.float32)
        first_free = (start_ref[N_EXPERTS - 1] + padded_ref[N_EXPERTS - 1]) // bm

        def zero_block(j):
            return pltpu.make_async_copy(vbuf.at[0, pl.ds(0, bm)], xs_ref.at[pl.ds(j * bm, bm)], sem_z.at[0])

        def start_block(j, carry):
            zero_block(j).start()
            return carry

        def wait_block(j, carry):
            zero_block(j).wait()
            return carry

        lax.fori_loop(first_free, xs_ref.shape[0] // bm, start_block, 0)
        lax.fori_loop(first_free, xs_ref.shape[0] // bm, wait_block, 0)


def _dispatch(dest_flat, counts, padded, pstart, h2_p, h2_s, n_rows):
    n_tok = h2_p.shape[0] + h2_s.shape[0]
    assert h2_p.shape[0] % DISPATCH_CHUNK == 0 and h2_s.shape[0] % DISPATCH_CHUNK == 0
    hbm = pl.BlockSpec(memory_space=pl.ANY)
    return pl.pallas_call(
        _dispatch_kernel,
        grid_spec=pltpu.PrefetchScalarGridSpec(
            num_scalar_prefetch=4, grid=(n_tok // DISPATCH_CHUNK,),
            in_specs=[hbm, hbm], out_specs=hbm,
            scratch_shapes=[pltpu.VMEM((3, DISPATCH_CHUNK, D_MODEL), jnp.float32),
                            pltpu.VMEM((8, D_MODEL), jnp.float32),
                            pltpu.SemaphoreType.DMA((3,)), pltpu.SemaphoreType.DMA((3,)),
                            pltpu.SemaphoreType.DMA((1,))]),
        out_shape=jax.ShapeDtypeStruct((n_rows, D_MODEL), jnp.float32),
        compiler_params=pltpu.CompilerParams(dimension_semantics=("arbitrary",), vmem_limit_bytes=VMEM_LIMIT),
        name="dispatch",
    )(dest_flat, counts, padded, pstart, h2_p, h2_s)


def _experts_kernel(be_ref, nu_ref, xs_ref, wg_ref, wu_ref, wd_ref, y_ref, wg_s, wu_s, wd_s):
    j = pl.program_id(0)
    used = j < nu_ref[0]
    changed = jnp.logical_or(j == 0, be_ref[j] != be_ref[jnp.maximum(j - 1, 0)])

    @pl.when(jnp.logical_and(used, changed))
    def _():
        wg_s[...] = wg_ref[...].astype(jnp.bfloat16)
        wu_s[...] = wu_ref[...].astype(jnp.bfloat16)
        wd_s[...] = wd_ref[...].astype(jnp.bfloat16)

    @pl.when(used)
    def _():
        x = xs_ref[...].astype(jnp.bfloat16)
        g = jnp.dot(x, wg_s[...], preferred_element_type=jnp.float32)
        u = jnp.dot(x, wu_s[...], preferred_element_type=jnp.float32)
        a = (g * jax.nn.sigmoid(g) * u).astype(jnp.bfloat16)
        y_ref[...] = jnp.dot(a, wd_s[...], preferred_element_type=jnp.float32)

    @pl.when(jnp.logical_not(used))
    def _():
        y_ref[...] = jnp.zeros_like(y_ref)


def _experts(block_e, n_used, xs, w_gate, w_up, w_down):
    bm = EXPERT_BLOCK
    n_blocks = xs.shape[0] // bm
    row = lambda j, be, nu: (jnp.minimum(j, nu[0] - 1), 0)
    wmap = lambda j, be, nu: (be[j], 0, 0)
    return pl.pallas_call(
        _experts_kernel,
        grid_spec=pltpu.PrefetchScalarGridSpec(
            num_scalar_prefetch=2, grid=(n_blocks,),
            in_specs=[pl.BlockSpec((bm, D_MODEL), row),
                      pl.BlockSpec((None, D_MODEL, EXPERT_HIDDEN), wmap),
                      pl.BlockSpec((None, D_MODEL, EXPERT_HIDDEN), wmap),
                      pl.BlockSpec((None, EXPERT_HIDDEN, D_MODEL), wmap)],
            out_specs=pl.BlockSpec((bm, D_MODEL), lambda j, be, nu: (j, 0)),
            scratch_shapes=[pltpu.VMEM((D_MODEL, EXPERT_HIDDEN), jnp.bfloat16),
                            pltpu.VMEM((D_MODEL, EXPERT_HIDDEN), jnp.bfloat16),
                            pltpu.VMEM((EXPERT_HIDDEN, D_MODEL), jnp.bfloat16)]),
        out_shape=jax.ShapeDtypeStruct(xs.shape, jnp.float32),
        compiler_params=pltpu.CompilerParams(dimension_semantics=("arbitrary",), vmem_limit_bytes=VMEM_LIMIT),
        name="experts",
    )(block_e, n_used, xs, w_gate, w_up, w_down)


def _combine_kernel(dest_ref, x1_ref, gate_ref, rf_ref, fin_ref, yb_ref, o_ref, ybuf, sem, *, n_tok, per_token_gate):
    i = pl.program_id(0)
    n = pl.num_programs(0)
    tc = COMBINE_TILE

    def issue(step, slot):
        def one(r, carry):
            tok = step * tc + r
            for k in range(2):
                d = dest_ref[k * n_tok + tok]
                pltpu.make_async_copy(yb_ref.at[pl.ds(d, 1)], ybuf.at[slot, k, pl.ds(r, 1)], sem.at[slot]).start()
            return carry
        lax.fori_loop(0, tc, one, 0, unroll=8)

    slot = i % 2

    @pl.when(i == 0)
    def _():
        issue(0, 0)

    @pl.when(i + 1 < n)
    def _():
        issue(i + 1, 1 - slot)

    pltpu.make_async_copy(yb_ref.at[pl.ds(0, tc)], ybuf.at[slot, 0], sem.at[slot]).wait()
    pltpu.make_async_copy(yb_ref.at[pl.ds(0, tc)], ybuf.at[slot, 1], sem.at[slot]).wait()

    gates = jnp.transpose(rf_ref[...])
    moe = ybuf[slot, 0] * gates[:, 0:1] + ybuf[slot, 1] * gates[:, 1:2]
    gate_f = gate_ref[...] if per_token_gate else gate_ref[...][5:6]
    o_ref[...] = _rms(x1_ref[...] + gate_f * moe, fin_ref[...])


def _combine(dest_flat, x1, gate, rf, final_norm, yb, *, tiles_per_gate):
    n_tok = x1.shape[0]
    tc = COMBINE_TILE
    per_token = tiles_per_gate == 0
    if per_token:
        gate_spec = pl.BlockSpec((tc, D_MODEL), lambda i, d: (i, 0))
    else:
        gate_spec = pl.BlockSpec((None, 6, D_MODEL), lambda i, d: (i // tiles_per_gate, 0, 0))
    return pl.pallas_call(
        functools.partial(_combine_kernel, n_tok=n_tok, per_token_gate=per_token),
        grid_spec=pltpu.PrefetchScalarGridSpec(
            num_scalar_prefetch=1, grid=(n_tok // tc,),
            in_specs=[pl.BlockSpec((tc, D_MODEL), lambda i, d: (i, 0)),
                      gate_spec,
                      pl.BlockSpec((8, tc), lambda i, d: (0, i)),
                      pl.BlockSpec((1, D_MODEL), lambda i, d: (0, 0)),
                      pl.BlockSpec(memory_space=pl.ANY)],
            out_specs=pl.BlockSpec((tc, D_MODEL), lambda i, d: (i, 0)),
            scratch_shapes=[pltpu.VMEM((2, 2, tc, D_MODEL), jnp.float32), pltpu.SemaphoreType.DMA((2,))]),
        out_shape=jax.ShapeDtypeStruct((n_tok, D_MODEL), jnp.float32),
        compiler_params=pltpu.CompilerParams(dimension_semantics=("arbitrary",), vmem_limit_bytes=VMEM_LIMIT),
        name="combine",
    )(dest_flat, x1, gate, rf, final_norm, yb)


def _rope_tables(pos):
    half = HEAD_DIM // 2
    inv_freq = ROPE_THETA ** (-jnp.arange(half, dtype=jnp.float32) / half)
    ang = pos.astype(jnp.float32)[:, None] * inv_freq[None, :]
    cos, sin = jnp.cos(ang), jnp.sin(ang)
    cos = jnp.concatenate([cos, cos, cos, cos], axis=1)
    sin = jnp.concatenate([-sin, sin, -sin, sin], axis=1)
    return cos, sin


def _prompt_bias():
    r = np.arange(WINDOW)[:, None]
    j = np.arange(2 * WINDOW)[None, :]
    band = (j > r) & (j <= r + WINDOW)
    later = np.where(band, 0.0, NEG).astype(np.float32)
    first = np.where(band & (j >= WINDOW), 0.0, NEG).astype(np.float32)
    return jnp.asarray(np.stack([first, later]))


def _sample_bias(dec):
    cb = SAMPLE_CHUNK
    qb = np.repeat(np.arange(cb), dec)[:, None]
    qi = np.tile(np.arange(dec), cb)[:, None]
    kb = np.concatenate([np.repeat(np.arange(cb), WINDOW), np.repeat(np.arange(cb), dec)])[None, :]
    kj = np.concatenate([np.tile(np.arange(WINDOW), cb), np.tile(np.arange(dec), cb)])[None, :]
    is_new = (np.arange(cb * WINDOW + cb * dec) >= cb * WINDOW)[None, :]
    ok = (kb == qb) & np.where(is_new, kj <= qi, kj > qi)
    ok = np.pad(ok, ((0, 0), (0, SAMPLE_KEYS - ok.shape[1])))
    return jnp.asarray(np.where(ok, 0.0, NEG).astype(np.float32))


def kernel(x_prompt, x_sample, cache_k, cache_v, state_conv, c_prompt, c_sample, attn_norm, ffn_norm, w_mod, b_mod, w_in, conv_w, attn_sinks, out_norm_attn, out_norm_conv, w_out, w_group, b_group, w_expert, b_expert, w_gate, w_up, w_down, final_norm):
    assert attn_norm.shape[0] == 1, "one layer"
    bsz, seq, _ = x_prompt.shape
    nseq, dec, _ = x_sample.shape
    m_prompt = bsz * seq
    n_s = nseq * dec
    m_total = m_prompt + n_s
    tm = MIXER_TILE
    assert seq % tm == 0 and n_s % SAMPLE_TILE == 0 and SAMPLE_TILE % (SAMPLE_CHUNK * dec) == 0
    assert cache_k.shape[2] == WINDOW and m_prompt % SAMPLE_TILE == 0 and m_total % COMBINE_TILE == 0
    bf = jnp.bfloat16

    w_r = jnp.zeros((ROUTER_ROWS, D_MODEL), jnp.float32)
    w_r = w_r.at[0:N_GROUPS].set(w_group[0].T).at[8:8 + N_EXPERTS].set(w_expert[0].T).astype(bf)
    b_r = jnp.zeros((ROUTER_ROWS,), jnp.float32).at[0:N_GROUPS].set(b_group[0]).at[8:8 + N_EXPERTS].set(b_expert[0])
    b_r = jnp.broadcast_to(b_r[:, None], (ROUTER_ROWS, LANES))
    cos_p, sin_p = _rope_tables(jnp.arange(seq, dtype=jnp.int32))
    cos_s, sin_s = _rope_tables(PAST_LEN + jnp.arange(dec, dtype=jnp.int32))
    tri = jnp.asarray(np.triu(np.ones((tm, tm), np.float32), 1)).astype(bf)
    consts = dict(
        anorm=attn_norm[0][None], fnorm=ffn_norm[0][None], w_in=w_in[0].astype(bf),
        cos_p=cos_p, sin_p=sin_p, cos_s=jnp.tile(cos_s, (nseq, 1)), sin_s=jnp.tile(sin_s, (nseq, 1)),
        bias_p=_prompt_bias(), bias_s=_sample_bias(dec), conv_w=conv_w[0],
        gn_attn=out_norm_attn[0][None], gn_conv=out_norm_conv[0][None], w_out=w_out[0].astype(bf),
        w_r=w_r, b_r=b_r, tri=tri)
    sinks = attn_sinks[0]

    mod = _adaln(jnp.concatenate([c_prompt, c_sample], axis=0), w_mod[0], b_mod[0])
    mod_p = mod[:bsz].reshape(bsz, 6, D_MODEL)
    mod_s = jnp.repeat(mod[bsz:], dec, axis=0)

    x1_p, h2_p, ri_p, rf_p, cnt_p, k_p, v_p, u_p = _prompt_mixer(x_prompt, mod_p, sinks, consts)
    st = state_conv[0]
    zrow = jnp.zeros((nseq, 1, CONV_DIM), jnp.float32)
    e1 = jnp.concatenate([st[:, 1:2]] + [zrow] * (dec - 1), axis=1).reshape(n_s, CONV_DIM)
    e2 = jnp.concatenate([st[:, 0:1], st[:, 1:2]] + [zrow] * (dec - 2), axis=1).reshape(n_s, CONV_DIM)
    ck = cache_k[0].reshape(nseq * WINDOW, KV_DIM)
    cv = cache_v[0].reshape(nseq * WINDOW, KV_DIM)
    x1_s, h2_s, ri_s, rf_s, cnt, k_s, v_s, u_s = _sample_mixer(
        x_sample.reshape(n_s, D_MODEL), mod_s, sinks, consts, ck, cv, e1, e2, cnt_p, dec)

    bm = EXPERT_BLOCK
    counts = cnt[:, 0].astype(jnp.int32)
    padded = (counts + bm - 1) // bm * bm
    pend = jnp.cumsum(padded)
    pstart = pend - padded
    n_blocks = -(-(2 * m_total + N_EXPERTS * (bm - 1)) // bm)
    n_used = (pend[-1] // bm).astype(jnp.int32)
    blk = jnp.arange(n_blocks, dtype=jnp.int32)
    eids = jnp.arange(N_EXPERTS, dtype=jnp.int32)
    block_e = jnp.sum((pend[None, :] <= (blk * bm)[:, None]).astype(jnp.int32), axis=1)
    last_e = jnp.max(jnp.where(padded > 0, eids, 0))
    block_e = jnp.where(blk < n_used, jnp.minimum(block_e, N_EXPERTS - 1), last_e)
    by_token = lambda a: jnp.transpose(a, (1, 0, 2)).reshape(8, -1)
    ri = jnp.concatenate([by_token(ri_p), by_token(ri_s)], axis=1)
    rf_flat = jnp.concatenate([by_token(rf_p), by_token(rf_s)], axis=1)
    start_of = jnp.sum(jnp.where(ri[0:2][..., None] == eids, pstart, 0), axis=-1)
    dest_flat = (start_of + ri[2:4]).reshape(2 * m_total)

    xs = _dispatch(dest_flat, counts, padded, pstart, h2_p, h2_s, n_blocks * bm)
    yb = _experts(block_e, n_used.reshape(1), xs, w_gate[0], w_up[0], w_down[0])

    fin = final_norm[None]
    dest_p = jnp.concatenate([dest_flat[:m_prompt], dest_flat[m_total:m_total + m_prompt]])
    dest_s = jnp.concatenate([dest_flat[m_prompt:m_total], dest_flat[m_total + m_prompt:]])
    y_p = _combine(dest_p, x1_p, mod_p, rf_flat[:, :m_prompt], fin, yb, tiles_per_gate=seq // COMBINE_TILE)
    gate_s = mod_s[:, 5 * D_MODEL:]
    y_s = _combine(dest_s, x1_s, gate_s, rf_flat[:, m_prompt:], fin, yb, tiles_per_gate=0)

    keep = WINDOW
    new_k_p = k_p.reshape(1, bsz, keep, N_KV_HEADS, HEAD_DIM)
    new_v_p = v_p.reshape(1, bsz, keep, N_KV_HEADS, HEAD_DIM)
    new_u_p = u_p[:, 6:8][None]
    ksn = k_s.reshape(nseq, dec, N_KV_HEADS, HEAD_DIM)
    vsn = v_s.reshape(nseq, dec, N_KV_HEADS, HEAD_DIM)
    new_k_s = jnp.concatenate([cache_k[0][:, dec:], ksn], axis=1)[None]
    new_v_s = jnp.concatenate([cache_v[0][:, dec:], vsn], axis=1)[None]
    u_all = jnp.concatenate([st, u_s.reshape(nseq, dec, CONV_DIM)], axis=1)
    new_u_s = u_all[:, -2:][None]
    return (y_p.reshape(bsz, seq, D_MODEL), y_s.reshape(nseq, dec, D_MODEL),
            new_k_p, new_v_p, new_u_p, new_k_s, new_v_s, new_u_s)
```

```python
import functools

import jax
import jax.numpy as jnp
import numpy as np
from jax import lax
from jax.experimental import pallas as pl
from jax.experimental.pallas import tpu as pltpu

D_MODEL = 1024
HEAD_DIM = 64
ATTN_DIM = 512
N_Q_HEADS = 8
N_KV_HEADS = 2
KV_DIM = 128
CONV_DIM = 512
WINDOW = 128
IN_PROJ_DIM = ATTN_DIM + 2 * KV_DIM + 3 * CONV_DIM
N_GROUPS = 4
EXPERTS_PER_GROUP = 8
N_EXPERTS = 32
EXPERT_HIDDEN = 512
RMS_EPS = 1e-5
ROPE_THETA = 10000.0
PAST_LEN = 16384

LANES = 128
ROUTER_ROWS = 48
NEG = -1e30

MIXER_TILE = 512
SAMPLE_TILE = 128
SAMPLE_CHUNK = 8
SAMPLE_KEYS = 1152
EXPERT_BLOCK = 256
COMBINE_TILE = 256
DISPATCH_CHUNK = 512
VMEM_LIMIT = 56 * 1024 * 1024


def _rms(x, gain):
    ms = jnp.mean(x * x, axis=-1, keepdims=True)
    return x * lax.rsqrt(ms + RMS_EPS) * gain


def _half_lane_mask(rows=1):
    return lax.broadcasted_iota(jnp.int32, (rows, LANES), 1) < HEAD_DIM


def _rope(x, cos, sin_signed):
    t, w = x.shape
    reps = w // LANES
    lane = lax.broadcasted_iota(jnp.int32, (t, w), 1)
    upper = (lane % HEAD_DIM) >= (HEAD_DIM // 2)
    partner = jnp.where(upper, pltpu.roll(x, HEAD_DIM // 2, 1), pltpu.roll(x, w - HEAD_DIM // 2, 1))
    if reps > 1:
        cos = jnp.concatenate([cos] * reps, axis=1)
        sin_signed = jnp.concatenate([sin_signed] * reps, axis=1)
    return x * cos + partner * sin_signed


def _swap_halves(x):
    return pltpu.roll(x, HEAD_DIM, 1)


def _group_norm64(x, gain):
    t, w = x.shape
    lo = _half_lane_mask(t)
    outs = []
    for c in range(w // LANES):
        xc = x[:, c * LANES:(c + 1) * LANES]
        sq = xc * xc
        s_lo = jnp.sum(jnp.where(lo, sq, 0.0), axis=-1, keepdims=True)
        s_hi = jnp.sum(jnp.where(lo, 0.0, sq), axis=-1, keepdims=True)
        r = jnp.where(lo, lax.rsqrt(s_lo * (1.0 / HEAD_DIM) + RMS_EPS), lax.rsqrt(s_hi * (1.0 / HEAD_DIM) + RMS_EPS))
        outs.append(xc * r)
    return jnp.concatenate(outs, axis=1) * gain


def _attend(q_rows, keys, keys_sw, vals, vals_sw, bias, sinks_ref, group):
    r = q_rows.shape[0]
    lo = _half_lane_mask(r)
    outs = []
    for parity in range(2):
        heads = [4 * group + parity, 4 * group + parity + 2]
        kk, vv = (keys, vals) if parity == group else (keys_sw, vals_sw)
        half = lo if parity == 0 else jnp.logical_not(lo)
        qs = []
        for h in heads:
            pair = q_rows[:, (h // 2) * LANES:(h // 2 + 1) * LANES]
            qs.append(jnp.where(half, pair, jnp.zeros_like(pair)))
        qcat = jnp.concatenate(qs, axis=0)
        s = lax.dot_general(qcat, kk, (((1,), (1,)), ((), ())), preferred_element_type=jnp.float32)
        ps, dens = [], []
        for i, h in enumerate(heads):
            sh = s[i * r:(i + 1) * r] + bias
            sink = sinks_ref[h]
            m = jnp.maximum(jnp.max(sh, axis=-1, keepdims=True), sink)
            p = jnp.exp(sh - m)
            dens.append(jnp.sum(p, axis=-1, keepdims=True) + jnp.exp(sink - m))
            ps.append(p.astype(jnp.bfloat16))
        o = jnp.dot(jnp.concatenate(ps, axis=0), vv, preferred_element_type=jnp.float32)
        for i, h in enumerate(heads):
            oh = o[i * r:(i + 1) * r] / dens[i]
            ms = jnp.sum(jnp.where(half, oh * oh, 0.0), axis=-1, keepdims=True) * (1.0 / HEAD_DIM)
            outs.append((h, oh * lax.rsqrt(ms + RMS_EPS)))
    outs.sort(key=lambda t: t[0])
    return [o for _, o in outs]


def _merge_heads(head_outs, lo):
    pairs = [jnp.where(lo, head_outs[2 * j], head_outs[2 * j + 1]) for j in range(4)]
    return jnp.concatenate(pairs, axis=1)


def _route(h2b, wr_ref, br_ref, tri_ref, cnt_ref, ri_ref, rf_ref):
    t = h2b.shape[0]
    lg = lax.dot_general(wr_ref[...], h2b, (((1,), (1,)), ((), ())), preferred_element_type=jnp.float32)
    lg = lg + br_ref[...][:, 0:1]
    r8 = lax.broadcasted_iota(jnp.int32, (8, t), 0)
    lgrp = jnp.where(r8 < N_GROUPS, lg[0:8], -jnp.inf)
    gmax = jnp.max(lgrp, axis=0, keepdims=True)
    grp = jnp.min(jnp.where(lgrp == gmax, r8, 8), axis=0, keepdims=True)
    pg_sel = 1.0 / jnp.sum(jnp.exp(lgrp - gmax), axis=0, keepdims=True)
    r32 = lax.broadcasted_iota(jnp.int32, (N_EXPERTS, t), 0)
    le = jnp.where((r32 // EXPERTS_PER_GROUP) == grp, lg[8:8 + N_EXPERTS], -jnp.inf)
    v1 = jnp.max(le, axis=0, keepdims=True)
    i1 = jnp.min(jnp.where(le == v1, r32, N_EXPERTS), axis=0, keepdims=True)
    le2 = jnp.where(r32 == i1, -jnp.inf, le)
    v2 = jnp.max(le2, axis=0, keepdims=True)
    i2 = jnp.min(jnp.where(le2 == v2, r32, N_EXPERTS), axis=0, keepdims=True)
    e21 = jnp.exp(v2 - v1)
    gate1 = pg_sel / (1.0 + e21)
    gate2 = pg_sel * e21 / (1.0 + e21)
    hot1 = r32 == i1
    hot2 = r32 == i2
    onehot = jnp.where(jnp.logical_or(hot1, hot2), 1.0, 0.0)
    before = jnp.dot(onehot.astype(jnp.bfloat16), tri_ref[...], preferred_element_type=jnp.float32)
    pos = before + cnt_ref[...][:, 0:1]
    rank1 = jnp.sum(jnp.where(hot1, pos, 0.0), axis=0, keepdims=True)
    rank2 = jnp.sum(jnp.where(hot2, pos, 0.0), axis=0, keepdims=True)
    cnt_ref[...] = cnt_ref[...] + jnp.sum(onehot, axis=1, keepdims=True)
    zi = jnp.zeros((4, t), jnp.int32)
    ri_ref[...] = jnp.concatenate([i1, i2, rank1.astype(jnp.int32), rank2.astype(jnp.int32), zi], axis=0)
    rf_ref[...] = jnp.concatenate([gate1, gate2, jnp.zeros((6, t), jnp.float32)], axis=0)


def _split_proj(proj):
    a = ATTN_DIM
    q = proj[:, :a]
    k = proj[:, a:a + KV_DIM]
    v = proj[:, a + KV_DIM:a + 2 * KV_DIM]
    c0 = a + 2 * KV_DIM
    h_conv = proj[:, c0:c0 + CONV_DIM]
    gate_b = proj[:, c0 + CONV_DIM:c0 + 2 * CONV_DIM]
    gate_c = proj[:, c0 + 2 * CONV_DIM:c0 + 3 * CONV_DIM]
    return q, k, v, gate_c * h_conv, gate_b


def _mixer_tail(x, attn_n, conv_n, mod, wout_ref, fnorm_ref, wr_ref, br_ref, tri_ref, cnt_ref,
                x1_ref, h2_ref, ri_ref, rf_ref):
    gate_a, shift_f, scale_f = mod
    cat = jnp.concatenate([attn_n.astype(jnp.bfloat16), conv_n.astype(jnp.bfloat16)], axis=1)
    mix = jnp.dot(cat, wout_ref[...], preferred_element_type=jnp.float32)
    x1 = x + gate_a * mix
    x1_ref[...] = x1
    h2 = _rms(x1, fnorm_ref[...]) * (1.0 + scale_f) + shift_f
    h2_ref[:, 0, :] = h2
    _route(h2.astype(jnp.bfloat16), wr_ref, br_ref, tri_ref, cnt_ref, ri_ref, rf_ref)


def _adaln_kernel(c_ref, w_ref, b_ref, o_ref):
    c = c_ref[...]
    a = (c * jax.nn.sigmoid(c)).astype(jnp.bfloat16)
    o_ref[...] = jnp.dot(a, w_ref[...].astype(jnp.bfloat16), preferred_element_type=jnp.float32) + b_ref[...]


def _adaln(c, w_mod, b_mod):
    n = c.shape[0]
    tn = 1536
    return pl.pallas_call(
        _adaln_kernel,
        grid=(w_mod.shape[1] // tn,),
        in_specs=[pl.BlockSpec((n, D_MODEL), lambda j: (0, 0)),
                  pl.BlockSpec((D_MODEL, tn), lambda j: (0, j)),
                  pl.BlockSpec((1, tn), lambda j: (0, j))],
        out_specs=pl.BlockSpec((n, tn), lambda j: (0, j)),
        out_shape=jax.ShapeDtypeStruct((n, w_mod.shape[1]), jnp.float32),
        compiler_params=pltpu.CompilerParams(dimension_semantics=("arbitrary",), vmem_limit_bytes=VMEM_LIMIT),
        name="adaln",
    )(c, w_mod, b_mod.reshape(1, -1))


def _prompt_mixer_kernel(sinks_ref, x_ref, mod_ref, anorm_ref, fnorm_ref, win_ref, cos_ref, sin_ref, bias_ref,
                         convw_ref, gna_ref, gnc_ref, wout_ref, wr_ref, br_ref, tri_ref,
                         x1_ref, h2_ref, ri_ref, rf_ref, cnt_out_ref, knew_ref, vnew_ref, unew_ref,
                         q_s, k_s, ksw_s, v_s, vsw_s, u_s, att_s, cnt_s):
    b = pl.program_id(0)
    t = pl.program_id(1)
    tm = MIXER_TILE
    nblk = tm // WINDOW

    @pl.when(jnp.logical_and(b == 0, t == 0))
    def _():
        cnt_s[...] = jnp.zeros_like(cnt_s)

    @pl.when(t == 0)
    def _():
        z = jnp.zeros((WINDOW, KV_DIM), jnp.bfloat16)
        k_s[0:WINDOW, :] = z
        ksw_s[0:WINDOW, :] = z
        v_s[0:WINDOW, :] = z
        vsw_s[0:WINDOW, :] = z
        u_s[0:8, :] = jnp.zeros((8, CONV_DIM), jnp.float32)

    x = x_ref[...]
    mod = mod_ref[...]
    h = _rms(x, anorm_ref[...]) * (1.0 + mod[1:2]) + mod[0:1]
    proj = jnp.dot(h.astype(jnp.bfloat16), win_ref[...], preferred_element_type=jnp.float32)
    q, k, v, u, gate_b = _split_proj(proj)
    cos = cos_ref[...]
    sin = sin_ref[...]
    q = _rope(q, cos, sin) * (HEAD_DIM ** -0.5)
    k = _rope(k, cos, sin)
    q_s[...] = q.astype(jnp.bfloat16)
    k_s[WINDOW:, :] = k.astype(jnp.bfloat16)
    ksw_s[WINDOW:, :] = _swap_halves(k).astype(jnp.bfloat16)
    v_s[WINDOW:, :] = v.astype(jnp.bfloat16)
    vsw_s[WINDOW:, :] = _swap_halves(v).astype(jnp.bfloat16)
    knew_ref[...] = k[tm - WINDOW:, :]
    vnew_ref[...] = v[tm - WINDOW:, :]

    u_s[8:, :] = u
    unew_ref[...] = u[tm - 8:, :]
    cw = convw_ref[...]
    conv = (u_s[6:6 + tm, :] * cw[0:1] + u_s[7:7 + tm, :] * cw[1:2] + u * cw[2:3]) * gate_b
    u_s[0:8, :] = u[tm - 8:, :]
    conv_n = _group_norm64(conv, gnc_ref[...])

    lo = _half_lane_mask(WINDOW)
    first = t == 0

    def block(i, carry):
        r0 = pl.multiple_of(i * WINDOW, WINDOW)
        bias = bias_ref[jnp.where(jnp.logical_and(first, i == 0), 0, 1)]
        qb = q_s[pl.ds(r0, WINDOW), :]
        keys = k_s[pl.ds(r0, 2 * WINDOW), :]
        keys_sw = ksw_s[pl.ds(r0, 2 * WINDOW), :]
        vals = v_s[pl.ds(r0, 2 * WINDOW), :]
        vals_sw = vsw_s[pl.ds(r0, 2 * WINDOW), :]
        outs = []
        for g in range(N_KV_HEADS):
            outs += _attend(qb, keys, keys_sw, vals, vals_sw, bias, sinks_ref, g)
        att_s[pl.ds(r0, WINDOW), :] = _merge_heads(outs, lo)
        return carry

    lax.fori_loop(0, nblk, block, 0)
    k_s[0:WINDOW, :] = k_s[tm:tm + WINDOW, :]
    ksw_s[0:WINDOW, :] = ksw_s[tm:tm + WINDOW, :]
    v_s[0:WINDOW, :] = v_s[tm:tm + WINDOW, :]
    vsw_s[0:WINDOW, :] = vsw_s[tm:tm + WINDOW, :]

    attn_n = att_s[...] * gna_ref[...]
    _mixer_tail(x, attn_n, conv_n, (mod[2:3], mod[3:4], mod[4:5]), wout_ref, fnorm_ref, wr_ref, br_ref, tri_ref,
                cnt_s, x1_ref, h2_ref, ri_ref, rf_ref)
    cnt_out_ref[...] = cnt_s[...]


def _prompt_mixer(x, mod, sinks, consts):
    bsz, seq, _ = x.shape
    tm = MIXER_TILE
    nt = seq // tm
    full = lambda shape: pl.BlockSpec(shape, lambda b, t, s: (0,) * len(shape))
    in_specs = [
        pl.BlockSpec((None, tm, D_MODEL), lambda b, t, s: (b, t, 0)),
        pl.BlockSpec((None, 6, D_MODEL), lambda b, t, s: (b, 0, 0)),
        full((1, D_MODEL)), full((1, D_MODEL)),
        full((D_MODEL, IN_PROJ_DIM)),
        pl.BlockSpec((tm, LANES), lambda b, t, s: (t, 0)),
        pl.BlockSpec((tm, LANES), lambda b, t, s: (t, 0)),
        full((2, WINDOW, 2 * WINDOW)),
        full((3, CONV_DIM)), full((1, ATTN_DIM)), full((1, CONV_DIM)),
        full((D_MODEL, D_MODEL)),
        full((ROUTER_ROWS, D_MODEL)), full((ROUTER_ROWS, LANES)),
        full((tm, tm)),
    ]
    out_shape = [
        jax.ShapeDtypeStruct((bsz * seq, D_MODEL), jnp.float32),
        jax.ShapeDtypeStruct((bsz * seq, 1, D_MODEL), jnp.float32),
        jax.ShapeDtypeStruct((bsz * nt, 8, tm), jnp.int32),
        jax.ShapeDtypeStruct((bsz * nt, 8, tm), jnp.float32),
        jax.ShapeDtypeStruct((N_EXPERTS, LANES), jnp.float32),
        jax.ShapeDtypeStruct((bsz, WINDOW, KV_DIM), jnp.float32),
        jax.ShapeDtypeStruct((bsz, WINDOW, KV_DIM), jnp.float32),
        jax.ShapeDtypeStruct((bsz, 8, CONV_DIM), jnp.float32),
    ]
    out_specs = [
        pl.BlockSpec((tm, D_MODEL), lambda b, t, s: (b * nt + t, 0)),
        pl.BlockSpec((tm, 1, D_MODEL), lambda b, t, s: (b * nt + t, 0, 0)),
        pl.BlockSpec((None, 8, tm), lambda b, t, s: (b * nt + t, 0, 0)),
        pl.BlockSpec((None, 8, tm), lambda b, t, s: (b * nt + t, 0, 0)),
        pl.BlockSpec((N_EXPERTS, LANES), lambda b, t, s: (0, 0)),
        pl.BlockSpec((None, WINDOW, KV_DIM), lambda b, t, s: (b, 0, 0)),
        pl.BlockSpec((None, WINDOW, KV_DIM), lambda b, t, s: (b, 0, 0)),
        pl.BlockSpec((None, 8, CONV_DIM), lambda b, t, s: (b, 0, 0)),
    ]
    scratch = [
        pltpu.VMEM((tm, ATTN_DIM), jnp.bfloat16),
        pltpu.VMEM((tm + WINDOW, KV_DIM), jnp.bfloat16), pltpu.VMEM((tm + WINDOW, KV_DIM), jnp.bfloat16),
        pltpu.VMEM((tm + WINDOW, KV_DIM), jnp.bfloat16), pltpu.VMEM((tm + WINDOW, KV_DIM), jnp.bfloat16),
        pltpu.VMEM((tm + 8, CONV_DIM), jnp.float32),
        pltpu.VMEM((tm, ATTN_DIM), jnp.float32),
        pltpu.VMEM((N_EXPERTS, LANES), jnp.float32),
    ]
    return pl.pallas_call(
        _prompt_mixer_kernel,
        grid_spec=pltpu.PrefetchScalarGridSpec(
            num_scalar_prefetch=1, grid=(bsz, nt), in_specs=in_specs, out_specs=out_specs, scratch_shapes=scratch),
        out_shape=out_shape,
        compiler_params=pltpu.CompilerParams(dimension_semantics=("arbitrary", "arbitrary"),
                                             vmem_limit_bytes=VMEM_LIMIT),
        name="prompt_mixer",
    )(sinks, x, mod, consts["anorm"], consts["fnorm"], consts["w_in"], consts["cos_p"], consts["sin_p"],
      consts["bias_p"], consts["conv_w"], consts["gn_attn"], consts["gn_conv"], consts["w_out"],
      consts["w_r"], consts["b_r"], consts["tri"])


def _sample_mixer_kernel(sinks_ref, x_ref, mod_ref, anorm_ref, fnorm_ref, win_ref, cos_ref, sin_ref, bias_ref,
                         convw_ref, gna_ref, gnc_ref, wout_ref, wr_ref, br_ref, tri_ref,
                         ck_ref, cv_ref, e1_ref, e2_ref, cnt_in_ref,
                         x1_ref, h2_ref, ri_ref, rf_ref, cnt_out_ref, knew_ref, vnew_ref, unew_ref,
                         q_s, k_s, v_s, att_s, cnt_s, *, dec):
    n = x_ref.shape[0]
    cb = SAMPLE_CHUNK
    rows = cb * dec
    pad_keys = SAMPLE_KEYS - cb * WINDOW - rows

    @pl.when(pl.program_id(0) == 0)
    def _():
        cnt_s[...] = cnt_in_ref[...]

    x = x_ref[...]
    mod = mod_ref[...]
    md = lambda j: mod[:, j * D_MODEL:(j + 1) * D_MODEL]
    h = _rms(x, anorm_ref[...]) * (1.0 + md(1)) + md(0)
    proj = jnp.dot(h.astype(jnp.bfloat16), win_ref[...], preferred_element_type=jnp.float32)
    q, k, v, u, gate_b = _split_proj(proj)
    cos = cos_ref[...]
    sin = sin_ref[...]
    q = _rope(q, cos, sin) * (HEAD_DIM ** -0.5)
    k = _rope(k, cos, sin)
    q_s[...] = q.astype(jnp.bfloat16)
    k_s[...] = k
    v_s[...] = v
    knew_ref[...] = k
    vnew_ref[...] = v
    unew_ref[...] = u

    pos = lax.broadcasted_iota(jnp.int32, (n, CONV_DIM), 0) % dec
    um1 = jnp.where(pos >= 1, pltpu.roll(u, 1, 0), 0.0) + e1_ref[...]
    um2 = jnp.where(pos >= 2, pltpu.roll(u, 2, 0), 0.0) + e2_ref[...]
    cw = convw_ref[...]
    conv = (um2 * cw[0:1] + um1 * cw[1:2] + u * cw[2:3]) * gate_b
    conv_n = _group_norm64(conv, gnc_ref[...])

    lo = _half_lane_mask(rows)
    bias = bias_ref[...]

    def chunk(c, carry):
        r0 = pl.multiple_of(c * rows, rows)
        c0 = pl.multiple_of(c * (cb * WINDOW), cb * WINDOW)
        zpad = jnp.zeros((pad_keys, KV_DIM), jnp.float32)
        kc = jnp.concatenate([ck_ref[pl.ds(c0, cb * WINDOW), :], k_s[pl.ds(r0, rows), :], zpad], axis=0)
        vc = jnp.concatenate([cv_ref[pl.ds(c0, cb * WINDOW), :], v_s[pl.ds(r0, rows), :], zpad], axis=0)
        keys, keys_sw = kc.astype(jnp.bfloat16), _swap_halves(kc).astype(jnp.bfloat16)
        vals, vals_sw = vc.astype(jnp.bfloat16), _swap_halves(vc).astype(jnp.bfloat16)
        qb = q_s[pl.ds(r0, rows), :]
        outs = []
        for g in range(N_KV_HEADS):
            outs += _attend(qb, keys, keys_sw, vals, vals_sw, bias, sinks_ref, g)
        att_s[pl.ds(r0, rows), :] = _merge_heads(outs, lo)
        return carry

    lax.fori_loop(0, n // rows, chunk, 0)
    attn_n = att_s[...] * gna_ref[...]
    _mixer_tail(x, attn_n, conv_n, (md(2), md(3), md(4)), wout_ref, fnorm_ref, wr_ref, br_ref, tri_ref,
                cnt_s, x1_ref, h2_ref, ri_ref, rf_ref)
    cnt_out_ref[...] = cnt_s[...]


def _sample_mixer(x, mod, sinks, consts, ck, cv, e1, e2, cnt_in, dec):
    n_s = x.shape[0]
    ts = SAMPLE_TILE
    seqs = ts // dec
    full = lambda a: pl.BlockSpec(a.shape, lambda i, s: (0,) * a.ndim)
    rows = lambda width: pl.BlockSpec((ts, width), lambda i, s: (i, 0))
    in_specs = [
        rows(D_MODEL), rows(6 * D_MODEL),
        full(consts["anorm"]), full(consts["fnorm"]), full(consts["w_in"]),
        rows(LANES), rows(LANES), full(consts["bias_s"]),
        full(consts["conv_w"]), full(consts["gn_attn"]), full(consts["gn_conv"]), full(consts["w_out"]),
        full(consts["w_r"]), full(consts["b_r"]),
        pl.BlockSpec((ts, ts), lambda i, s: (0, 0)),
        pl.BlockSpec((seqs * WINDOW, KV_DIM), lambda i, s: (i, 0)),
        pl.BlockSpec((seqs * WINDOW, KV_DIM), lambda i, s: (i, 0)),
        rows(CONV_DIM), rows(CONV_DIM), full(cnt_in),
    ]
    args = [x, mod, consts["anorm"], consts["fnorm"], consts["w_in"], consts["cos_s"], consts["sin_s"],
            consts["bias_s"], consts["conv_w"], consts["gn_attn"], consts["gn_conv"], consts["w_out"],
            consts["w_r"], consts["b_r"], consts["tri"], ck, cv, e1, e2, cnt_in]
    nt = n_s // ts
    out_shape = [
        jax.ShapeDtypeStruct((n_s, D_MODEL), jnp.float32),
        jax.ShapeDtypeStruct((n_s, 1, D_MODEL), jnp.float32),
        jax.ShapeDtypeStruct((nt, 8, ts), jnp.int32),
        jax.ShapeDtypeStruct((nt, 8, ts), jnp.float32),
        jax.ShapeDtypeStruct((N_EXPERTS, LANES), jnp.float32),
        jax.ShapeDtypeStruct((n_s, KV_DIM), jnp.float32),
        jax.ShapeDtypeStruct((n_s, KV_DIM), jnp.float32),
        jax.ShapeDtypeStruct((n_s, CONV_DIM), jnp.float32),
    ]
    out_specs = [
        rows(D_MODEL),
        pl.BlockSpec((ts, 1, D_MODEL), lambda i, s: (i, 0, 0)),
        pl.BlockSpec((None, 8, ts), lambda i, s: (i, 0, 0)),
        pl.BlockSpec((None, 8, ts), lambda i, s: (i, 0, 0)),
        pl.BlockSpec((N_EXPERTS, LANES), lambda i, s: (0, 0)),
        rows(KV_DIM), rows(KV_DIM), rows(CONV_DIM),
    ]
    scratch = [
        pltpu.VMEM((ts, ATTN_DIM), jnp.bfloat16),
        pltpu.VMEM((ts, KV_DIM), jnp.float32), pltpu.VMEM((ts, KV_DIM), jnp.float32),
        pltpu.VMEM((ts, ATTN_DIM), jnp.float32),
        pltpu.VMEM((N_EXPERTS, LANES), jnp.float32),
    ]
    return pl.pallas_call(
        functools.partial(_sample_mixer_kernel, dec=dec),
        grid_spec=pltpu.PrefetchScalarGridSpec(
            num_scalar_prefetch=1, grid=(nt,), in_specs=in_specs, out_specs=out_specs, scratch_shapes=scratch),
        out_shape=out_shape,
        compiler_params=pltpu.CompilerParams(dimension_semantics=("arbitrary",), vmem_limit_bytes=VMEM_LIMIT),
        name="sample_mixer",
    )(sinks, *args)


def _dispatch_kernel(dest_ref, cnt_ref, padded_ref, start_ref, h2p_ref, h2s_ref, xs_ref,
                     vbuf, zbuf, sem_in, sem_out, sem_z):
    c = pl.program_id(0)
    n = pl.num_programs(0)
    chunk = DISPATCH_CHUNK
    n_p = h2p_ref.shape[0] // chunk
    n_tok = h2p_ref.shape[0] + h2s_ref.shape[0]
    nbuf = vbuf.shape[0]

    def load(step, slot):
        @pl.when(step < n_p)
        def _():
            pltpu.make_async_copy(h2p_ref.at[pl.ds(step * chunk, chunk)], vbuf.at[slot], sem_in.at[slot]).start()

        @pl.when(step >= n_p)
        def _():
            pltpu.make_async_copy(h2s_ref.at[pl.ds((step - n_p) * chunk, chunk)], vbuf.at[slot],
                                  sem_in.at[slot]).start()

    def wait_scatter(slot):
        for _ in range(2):
            pltpu.make_async_copy(vbuf.at[slot], xs_ref.at[pl.ds(0, chunk)], sem_out.at[slot]).wait()

    slot = c % nbuf
    nxt = (c + 1) % nbuf

    @pl.when(c == 0)
    def _():
        load(0, 0)
        zbuf[...] = jnp.zeros_like(zbuf)

    @pl.when(c >= nbuf - 1)
    def _():
        wait_scatter(nxt)

    @pl.when(c + 1 < n)
    def _():
        load(c + 1, nxt)

    pltpu.make_async_copy(h2p_ref.at[pl.ds(0, chunk)], vbuf.at[slot], sem_in.at[slot]).wait()

    def issue(i, carry):
        tok = c * chunk + i
        for k in range(2):
            d = dest_ref[k * n_tok + tok]
            pltpu.make_async_copy(vbuf.at[slot, pl.ds(i, 1)], xs_ref.at[pl.ds(d, 1)], sem_out.at[slot]).start()
        return carry

    lax.fori_loop(0, chunk, issue, 0, unroll=8)

    @pl.when(c == n - 1)
    def _():
        for back in range(nbuf - 2, -1, -1):
            @pl.when(c >= back)
            def _():
                wait_scatter((c - back) % nbuf)

        def zero_row(e, r):
            return pltpu.make_async_copy(zbuf.at[pl.ds(0, 1)], xs_ref.at[pl.ds(start_ref[e] + r, 1)], sem_z.at[0])

        def per_expert(wait):
            def body(e, carry):
                def row(r, carry2):
                    if wait:
                        zero_row(e, r).wait()
                    else:
                        zero_row(e, r).start()
                    return carry2
                lax.fori_loop(cnt_ref[e], padded_ref[e], row, 0)
                return carry
            lax.fori_loop(0, N_EXPERTS, body, 0)

        per_expert(False)
        per_expert(True)

        bm = EXPERT_BLOCK
        vbuf[0] = jnp.zeros((chunk, 1, D_MODEL), jnp.float32)
        first_free = (start_ref[N_EXPERTS - 1] + padded_ref[N_EXPERTS - 1]) // bm

        def zero_block(j):
            return pltpu.make_async_copy(vbuf.at[0, pl.ds(0, bm)], xs_ref.at[pl.ds(j * bm, bm)], sem_z.at[0])

        def start_block(j, carry):
            zero_block(j).start()
            return carry

        def wait_block(j, carry):
            zero_block(j).wait()
            return carry

        lax.fori_loop(first_free, xs_ref.shape[0] // bm, start_block, 0)
        lax.fori_loop(first_free, xs_ref.shape[0] // bm, wait_block, 0)


def _dispatch(dest_flat, counts, padded, pstart, h2_p, h2_s, n_rows):
    n_tok = h2_p.shape[0] + h2_s.shape[0]
    assert h2_p.shape[0] % DISPATCH_CHUNK == 0 and h2_s.shape[0] % DISPATCH_CHUNK == 0
    hbm = pl.BlockSpec(memory_space=pl.ANY)
    return pl.pallas_call(
        _dispatch_kernel,
        grid_spec=pltpu.PrefetchScalarGridSpec(
            num_scalar_prefetch=4, grid=(n_tok // DISPATCH_CHUNK,),
            in_specs=[hbm, hbm], out_specs=hbm,
            scratch_shapes=[pltpu.VMEM((3, DISPATCH_CHUNK, 1, D_MODEL), jnp.float32),
                            pltpu.VMEM((8, 1, D_MODEL), jnp.float32),
                            pltpu.SemaphoreType.DMA((3,)), pltpu.SemaphoreType.DMA((3,)),
                            pltpu.SemaphoreType.DMA((1,))]),
        out_shape=jax.ShapeDtypeStruct((n_rows, 1, D_MODEL), jnp.float32),
        compiler_params=pltpu.CompilerParams(dimension_semantics=("arbitrary",), vmem_limit_bytes=VMEM_LIMIT),
        name="dispatch",
    )(dest_flat, counts, padded, pstart, h2_p, h2_s)


def _experts_kernel(be_ref, nu_ref, xs_ref, wg_ref, wu_ref, wd_ref, y_ref, wg_s, wu_s, wd_s, x_s):
    j = pl.program_id(0)
    used = j < nu_ref[0]
    changed = jnp.logical_or(j == 0, be_ref[j] != be_ref[jnp.maximum(j - 1, 0)])

    @pl.when(jnp.logical_and(used, changed))
    def _():
        wg_s[...] = wg_ref[...].astype(jnp.bfloat16)
        wu_s[...] = wu_ref[...].astype(jnp.bfloat16)
        wd_s[...] = wd_ref[...].astype(jnp.bfloat16)

    @pl.when(used)
    def _():
        x_s[...] = xs_ref[...].reshape(x_s.shape)
        x = x_s[...].astype(jnp.bfloat16)
        g = jnp.dot(x, wg_s[...], preferred_element_type=jnp.float32)
        u = jnp.dot(x, wu_s[...], preferred_element_type=jnp.float32)
        a = (g * jax.nn.sigmoid(g) * u).astype(jnp.bfloat16)
        y_ref[...] = jnp.dot(a, wd_s[...], preferred_element_type=jnp.float32).reshape(y_ref.shape)

    @pl.when(jnp.logical_not(used))
    def _():
        y_ref[...] = jnp.zeros_like(y_ref)


def _experts(block_e, n_used, xs, w_gate, w_up, w_down):
    bm = EXPERT_BLOCK
    n_blocks = xs.shape[0] // bm
    row = lambda j, be, nu: (jnp.minimum(j, nu[0] - 1), 0, 0)
    wmap = lambda j, be, nu: (be[j], 0, 0)
    return pl.pallas_call(
        _experts_kernel,
        grid_spec=pltpu.PrefetchScalarGridSpec(
            num_scalar_prefetch=2, grid=(n_blocks,),
            in_specs=[pl.BlockSpec((bm, 1, D_MODEL), row),
                      pl.BlockSpec((None, D_MODEL, EXPERT_HIDDEN), wmap),
                      pl.BlockSpec((None, D_MODEL, EXPERT_HIDDEN), wmap),
                      pl.BlockSpec((None, EXPERT_HIDDEN, D_MODEL), wmap)],
            out_specs=pl.BlockSpec((bm, 1, D_MODEL), lambda j, be, nu: (j, 0, 0)),
            scratch_shapes=[pltpu.VMEM((D_MODEL, EXPERT_HIDDEN), jnp.bfloat16),
                            pltpu.VMEM((D_MODEL, EXPERT_HIDDEN), jnp.bfloat16),
                            pltpu.VMEM((EXPERT_HIDDEN, D_MODEL), jnp.bfloat16),
                            pltpu.VMEM((bm, D_MODEL), jnp.float32)]),
        out_shape=jax.ShapeDtypeStruct(xs.shape, jnp.float32),
        compiler_params=pltpu.CompilerParams(dimension_semantics=("arbitrary",), vmem_limit_bytes=VMEM_LIMIT),
        name="experts",
    )(block_e, n_used, xs, w_gate, w_up, w_down)


def _combine_kernel(dest_ref, x1_ref, gate_ref, rf_ref, fin_ref, yb_ref, o_ref, ybuf, y_s, sem, *,
                    n_tok, per_token_gate):
    i = pl.program_id(0)
    n = pl.num_programs(0)
    tc = COMBINE_TILE

    def issue(step, slot):
        def one(r, carry):
            tok = step * tc + r
            for k in range(2):
                d = dest_ref[k * n_tok + tok]
                pltpu.make_async_copy(yb_ref.at[pl.ds(d, 1)], ybuf.at[slot, k, pl.ds(r, 1)], sem.at[slot]).start()
            return carry
        lax.fori_loop(0, tc, one, 0, unroll=8)

    slot = i % 2

    @pl.when(i == 0)
    def _():
        issue(0, 0)

    @pl.when(i + 1 < n)
    def _():
        issue(i + 1, 1 - slot)

    pltpu.make_async_copy(yb_ref.at[pl.ds(0, tc)], ybuf.at[slot, 0], sem.at[slot]).wait()
    pltpu.make_async_copy(yb_ref.at[pl.ds(0, tc)], ybuf.at[slot, 1], sem.at[slot]).wait()

    gates = jnp.transpose(rf_ref[...])
    y_s[0] = ybuf[slot, 0].reshape(tc, D_MODEL)
    y_s[1] = ybuf[slot, 1].reshape(tc, D_MODEL)
    y0 = y_s[0]
    y1 = y_s[1]
    moe = y0 * gates[:, 0:1] + y1 * gates[:, 1:2]
    gate_f = gate_ref[...] if per_token_gate else gate_ref[...][5:6]
    o_ref[...] = _rms(x1_ref[...] + gate_f * moe, fin_ref[...])


def _combine(dest_flat, x1, gate, rf, final_norm, yb, *, tiles_per_gate):
    n_tok = x1.shape[0]
    tc = COMBINE_TILE
    per_token = tiles_per_gate == 0
    if per_token:
        gate_spec = pl.BlockSpec((tc, D_MODEL), lambda i, d: (i, 0))
    else:
        gate_spec = pl.BlockSpec((None, 6, D_MODEL), lambda i, d: (i // tiles_per_gate, 0, 0))
    return pl.pallas_call(
        functools.partial(_combine_kernel, n_tok=n_tok, per_token_gate=per_token),
        grid_spec=pltpu.PrefetchScalarGridSpec(
            num_scalar_prefetch=1, grid=(n_tok // tc,),
            in_specs=[pl.BlockSpec((tc, D_MODEL), lambda i, d: (i, 0)),
                      gate_spec,
                      pl.BlockSpec((8, tc), lambda i, d: (0, i)),
                      pl.BlockSpec((1, D_MODEL), lambda i, d: (0, 0)),
                      pl.BlockSpec(memory_space=pl.ANY)],
            out_specs=pl.BlockSpec((tc, D_MODEL), lambda i, d: (i, 0)),
            scratch_shapes=[pltpu.VMEM((2, 2, tc, 1, D_MODEL), jnp.float32),
                            pltpu.VMEM((2, tc, D_MODEL), jnp.float32), pltpu.SemaphoreType.DMA((2,))]),
        out_shape=jax.ShapeDtypeStruct((n_tok, D_MODEL), jnp.float32),
        compiler_params=pltpu.CompilerParams(dimension_semantics=("arbitrary",), vmem_limit_bytes=VMEM_LIMIT),
        name="combine",
    )(dest_flat, x1, gate, rf, final_norm, yb)


def _rope_tables(pos):
    half = HEAD_DIM // 2
    inv_freq = ROPE_THETA ** (-jnp.arange(half, dtype=jnp.float32) / half)
    ang = pos.astype(jnp.float32)[:, None] * inv_freq[None, :]
    cos, sin = jnp.cos(ang), jnp.sin(ang)
    cos = jnp.concatenate([cos, cos, cos, cos], axis=1)
    sin = jnp.concatenate([-sin, sin, -sin, sin], axis=1)
    return cos, sin


def _prompt_bias():
    r = np.arange(WINDOW)[:, None]
    j = np.arange(2 * WINDOW)[None, :]
    band = (j > r) & (j <= r + WINDOW)
    later = np.where(band, 0.0, NEG).astype(np.float32)
    first = np.where(band & (j >= WINDOW), 0.0, NEG).astype(np.float32)
    return jnp.asarray(np.stack([first, later]))


def _sample_bias(dec):
    cb = SAMPLE_CHUNK
    qb = np.repeat(np.arange(cb), dec)[:, None]
    qi = np.tile(np.arange(dec), cb)[:, None]
    kb = np.concatenate([np.repeat(np.arange(cb), WINDOW), np.repeat(np.arange(cb), dec)])[None, :]
    kj = np.concatenate([np.tile(np.arange(WINDOW), cb), np.tile(np.arange(dec), cb)])[None, :]
    is_new = (np.arange(cb * WINDOW + cb * dec) >= cb * WINDOW)[None, :]
    ok = (kb == qb) & np.where(is_new, kj <= qi, kj > qi)
    ok = np.pad(ok, ((0, 0), (0, SAMPLE_KEYS - ok.shape[1])))
    return jnp.asarray(np.where(ok, 0.0, NEG).astype(np.float32))


def kernel(x_prompt, x_sample, cache_k, cache_v, state_conv, c_prompt, c_sample, attn_norm, ffn_norm, w_mod, b_mod, w_in, conv_w, attn_sinks, out_norm_attn, out_norm_conv, w_out, w_group, b_group, w_expert, b_expert, w_gate, w_up, w_down, final_norm):
    assert attn_norm.shape[0] == 1, "one layer"
    bsz, seq, _ = x_prompt.shape
    nseq, dec, _ = x_sample.shape
    m_prompt = bsz * seq
    n_s = nseq * dec
    m_total = m_prompt + n_s
    tm = MIXER_TILE
    assert seq % tm == 0 and n_s % SAMPLE_TILE == 0 and SAMPLE_TILE % (SAMPLE_CHUNK * dec) == 0
    assert cache_k.shape[2] == WINDOW and m_prompt % SAMPLE_TILE == 0 and m_total % COMBINE_TILE == 0
    bf = jnp.bfloat16

    w_r = jnp.zeros((ROUTER_ROWS, D_MODEL), jnp.float32)
    w_r = w_r.at[0:N_GROUPS].set(w_group[0].T).at[8:8 + N_EXPERTS].set(w_expert[0].T).astype(bf)
    b_r = jnp.zeros((ROUTER_ROWS,), jnp.float32).at[0:N_GROUPS].set(b_group[0]).at[8:8 + N_EXPERTS].set(b_expert[0])
    b_r = jnp.broadcast_to(b_r[:, None], (ROUTER_ROWS, LANES))
    cos_p, sin_p = _rope_tables(jnp.arange(seq, dtype=jnp.int32))
    cos_s, sin_s = _rope_tables(PAST_LEN + jnp.arange(dec, dtype=jnp.int32))
    tri = jnp.asarray(np.triu(np.ones((tm, tm), np.float32), 1)).astype(bf)
    consts = dict(
        anorm=attn_norm[0][None], fnorm=ffn_norm[0][None], w_in=w_in[0].astype(bf),
        cos_p=cos_p, sin_p=sin_p, cos_s=jnp.tile(cos_s, (nseq, 1)), sin_s=jnp.tile(sin_s, (nseq, 1)),
        bias_p=_prompt_bias(), bias_s=_sample_bias(dec), conv_w=conv_w[0],
        gn_attn=out_norm_attn[0][None], gn_conv=out_norm_conv[0][None], w_out=w_out[0].astype(bf),
        w_r=w_r, b_r=b_r, tri=tri)
    sinks = attn_sinks[0]

    mod = _adaln(jnp.concatenate([c_prompt, c_sample], axis=0), w_mod[0], b_mod[0])
    mod_p = mod[:bsz].reshape(bsz, 6, D_MODEL)
    mod_s = jnp.repeat(mod[bsz:], dec, axis=0)

    x1_p, h2_p, ri_p, rf_p, cnt_p, k_p, v_p, u_p = _prompt_mixer(x_prompt, mod_p, sinks, consts)
    st = state_conv[0]
    zrow = jnp.zeros((nseq, 1, CONV_DIM), jnp.float32)
    e1 = jnp.concatenate([st[:, 1:2]] + [zrow] * (dec - 1), axis=1).reshape(n_s, CONV_DIM)
    e2 = jnp.concatenate([st[:, 0:1], st[:, 1:2]] + [zrow] * (dec - 2), axis=1).reshape(n_s, CONV_DIM)
    ck = cache_k[0].reshape(nseq * WINDOW, KV_DIM)
    cv = cache_v[0].reshape(nseq * WINDOW, KV_DIM)
    x1_s, h2_s, ri_s, rf_s, cnt, k_s, v_s, u_s = _sample_mixer(
        x_sample.reshape(n_s, D_MODEL), mod_s, sinks, consts, ck, cv, e1, e2, cnt_p, dec)

    bm = EXPERT_BLOCK
    counts = cnt[:, 0].astype(jnp.int32)
    padded = (counts + bm - 1) // bm * bm
    pend = jnp.cumsum(padded)
    pstart = pend - padded
    n_blocks = -(-(2 * m_total + N_EXPERTS * (bm - 1)) // bm)
    n_used = (pend[-1] // bm).astype(jnp.int32)
    blk = jnp.arange(n_blocks, dtype=jnp.int32)
    eids = jnp.arange(N_EXPERTS, dtype=jnp.int32)
    block_e = jnp.sum((pend[None, :] <= (blk * bm)[:, None]).astype(jnp.int32), axis=1)
    last_e = jnp.max(jnp.where(padded > 0, eids, 0))
    block_e = jnp.where(blk < n_used, jnp.minimum(block_e, N_EXPERTS - 1), last_e)
    by_token = lambda a: jnp.transpose(a, (1, 0, 2)).reshape(8, -1)
    ri = jnp.concatenate([by_token(ri_p), by_token(ri_s)], axis=1)
    rf_flat = jnp.concatenate([by_token(rf_p), by_token(rf_s)], axis=1)
    start_of = jnp.sum(jnp.where(ri[0:2][..., None] == eids, pstart, 0), axis=-1)
    dest_flat = (start_of + ri[2:4]).reshape(2 * m_total)

    xs = _dispatch(dest_flat, counts, padded, pstart, h2_p, h2_s, n_blocks * bm)
    yb = _experts(block_e, n_used.reshape(1), xs, w_gate[0], w_up[0], w_down[0])

    fin = final_norm[None]
    dest_p = jnp.concatenate([dest_flat[:m_prompt], dest_flat[m_total:m_total + m_prompt]])
    dest_s = jnp.concatenate([dest_flat[m_prompt:m_total], dest_flat[m_total + m_prompt:]])
    y_p = _combine(dest_p, x1_p, mod_p, rf_flat[:, :m_prompt], fin, yb, tiles_per_gate=seq // COMBINE_TILE)
    gate_s = mod_s[:, 5 * D_MODEL:]
    y_s = _combine(dest_s, x1_s, gate_s, rf_flat[:, m_prompt:], fin, yb, tiles_per_gate=0)

    keep = WINDOW
    new_k_p = k_p.reshape(1, bsz, keep, N_KV_HEADS, HEAD_DIM)
    new_v_p = v_p.reshape(1, bsz, keep, N_KV_HEADS, HEAD_DIM)
    new_u_p = u_p[:, 6:8][None]
    ksn = k_s.reshape(nseq, dec, N_KV_HEADS, HEAD_DIM)
    vsn = v_s.reshape(nseq, dec, N_KV_HEADS, HEAD_DIM)
    new_k_s = jnp.concatenate([cache_k[0][:, dec:], ksn], axis=1)[None]
    new_v_s = jnp.concatenate([cache_v[0][:, dec:], vsn], axis=1)[None]
    u_all = jnp.concatenate([st, u_s.reshape(nseq, dec, CONV_DIM)], axis=1)
    new_u_s = u_all[:, -2:][None]
    return (y_p.reshape(bsz, seq, D_MODEL), y_s.reshape(nseq, dec, D_MODEL),
            new_k_p, new_v_p, new_u_p, new_k_s, new_v_s, new_u_s)
```

```python
import functools

import jax
import jax.numpy as jnp
import numpy as np
from jax import lax
from jax.experimental import pallas as pl
from jax.experimental.pallas import tpu as pltpu

D_MODEL = 1024
HEAD_DIM = 64
ATTN_DIM = 512
N_Q_HEADS = 8
N_KV_HEADS = 2
KV_DIM = 128
CONV_DIM = 512
WINDOW = 128
IN_PROJ_DIM = ATTN_DIM + 2 * KV_DIM + 3 * CONV_DIM
N_GROUPS = 4
EXPERTS_PER_GROUP = 8
N_EXPERTS = 32
EXPERT_HIDDEN = 512
RMS_EPS = 1e-5
ROPE_THETA = 10000.0
PAST_LEN = 16384

LANES = 128
ROUTER_ROWS = 48
NEG = -1e30

MIXER_TILE = 512
SAMPLE_TILE = 128
SAMPLE_CHUNK = 8
SAMPLE_KEYS = 1152
EXPERT_BLOCK = 512
COMBINE_TILE = 256
DISPATCH_CHUNK = 512
VMEM_LIMIT = 56 * 1024 * 1024


def _rms(x, gain):
    ms = jnp.mean(x * x, axis=-1, keepdims=True)
    return x * lax.rsqrt(ms + RMS_EPS) * gain


def _half_lane_mask(rows=1):
    return lax.broadcasted_iota(jnp.int32, (rows, LANES), 1) < HEAD_DIM


def _rope(x, cos, sin_signed):
    t, w = x.shape
    reps = w // LANES
    lane = lax.broadcasted_iota(jnp.int32, (t, w), 1)
    upper = (lane % HEAD_DIM) >= (HEAD_DIM // 2)
    partner = jnp.where(upper, pltpu.roll(x, HEAD_DIM // 2, 1), pltpu.roll(x, w - HEAD_DIM // 2, 1))
    if reps > 1:
        cos = jnp.concatenate([cos] * reps, axis=1)
        sin_signed = jnp.concatenate([sin_signed] * reps, axis=1)
    return x * cos + partner * sin_signed


def _swap_halves(x):
    return pltpu.roll(x, HEAD_DIM, 1)


def _group_norm64(x, gain):
    t, w = x.shape
    lo = _half_lane_mask(t)
    outs = []
    for c in range(w // LANES):
        xc = x[:, c * LANES:(c + 1) * LANES]
        sq = xc * xc
        s_lo = jnp.sum(jnp.where(lo, sq, 0.0), axis=-1, keepdims=True)
        s_hi = jnp.sum(jnp.where(lo, 0.0, sq), axis=-1, keepdims=True)
        r = jnp.where(lo, lax.rsqrt(s_lo * (1.0 / HEAD_DIM) + RMS_EPS), lax.rsqrt(s_hi * (1.0 / HEAD_DIM) + RMS_EPS))
        outs.append(xc * r)
    return jnp.concatenate(outs, axis=1) * gain


def _attend(q_rows, keys, keys_sw, vals, vals_sw, bias, sinks_ref, group):
    r = q_rows.shape[0]
    lo = _half_lane_mask(r)
    outs = []
    for parity in range(2):
        heads = [4 * group + parity, 4 * group + parity + 2]
        kk, vv = (keys, vals) if parity == group else (keys_sw, vals_sw)
        half = lo if parity == 0 else jnp.logical_not(lo)
        qs = []
        for h in heads:
            pair = q_rows[:, (h // 2) * LANES:(h // 2 + 1) * LANES]
            qs.append(jnp.where(half, pair, jnp.zeros_like(pair)))
        qcat = jnp.concatenate(qs, axis=0)
        s = lax.dot_general(qcat, kk, (((1,), (1,)), ((), ())), preferred_element_type=jnp.float32)
        ps, dens = [], []
        for i, h in enumerate(heads):
            sh = s[i * r:(i + 1) * r] + bias
            sink = sinks_ref[h]
            m = jnp.maximum(jnp.max(sh, axis=-1, keepdims=True), sink)
            p = jnp.exp(sh - m)
            dens.append(jnp.sum(p, axis=-1, keepdims=True) + jnp.exp(sink - m))
            ps.append(p.astype(jnp.bfloat16))
        o = jnp.dot(jnp.concatenate(ps, axis=0), vv, preferred_element_type=jnp.float32)
        for i, h in enumerate(heads):
            oh = o[i * r:(i + 1) * r] / dens[i]
            ms = jnp.sum(jnp.where(half, oh * oh, 0.0), axis=-1, keepdims=True) * (1.0 / HEAD_DIM)
            outs.append((h, oh * lax.rsqrt(ms + RMS_EPS)))
    outs.sort(key=lambda t: t[0])
    return [o for _, o in outs]


def _merge_heads(head_outs, lo):
    pairs = [jnp.where(lo, head_outs[2 * j], head_outs[2 * j + 1]) for j in range(4)]
    return jnp.concatenate(pairs, axis=1)


def _route(h2b, wr_ref, br_ref, tri_ref, cnt_ref, ri_ref, rf_ref):
    t = h2b.shape[0]
    lg = lax.dot_general(wr_ref[...], h2b, (((1,), (1,)), ((), ())), preferred_element_type=jnp.float32)
    lg = lg + br_ref[...][:, 0:1]
    r8 = lax.broadcasted_iota(jnp.int32, (8, t), 0)
    lgrp = jnp.where(r8 < N_GROUPS, lg[0:8], -jnp.inf)
    gmax = jnp.max(lgrp, axis=0, keepdims=True)
    grp = jnp.min(jnp.where(lgrp == gmax, r8, 8), axis=0, keepdims=True)
    pg_sel = 1.0 / jnp.sum(jnp.exp(lgrp - gmax), axis=0, keepdims=True)
    r32 = lax.broadcasted_iota(jnp.int32, (N_EXPERTS, t), 0)
    le = jnp.where((r32 // EXPERTS_PER_GROUP) == grp, lg[8:8 + N_EXPERTS], -jnp.inf)
    v1 = jnp.max(le, axis=0, keepdims=True)
    i1 = jnp.min(jnp.where(le == v1, r32, N_EXPERTS), axis=0, keepdims=True)
    le2 = jnp.where(r32 == i1, -jnp.inf, le)
    v2 = jnp.max(le2, axis=0, keepdims=True)
    i2 = jnp.min(jnp.where(le2 == v2, r32, N_EXPERTS), axis=0, keepdims=True)
    e21 = jnp.exp(v2 - v1)
    gate1 = pg_sel / (1.0 + e21)
    gate2 = pg_sel * e21 / (1.0 + e21)
    hot1 = r32 == i1
    hot2 = r32 == i2
    onehot = jnp.where(jnp.logical_or(hot1, hot2), 1.0, 0.0)
    before = jnp.dot(onehot.astype(jnp.bfloat16), tri_ref[...], preferred_element_type=jnp.float32)
    pos = before + cnt_ref[...][:, 0:1]
    rank1 = jnp.sum(jnp.where(hot1, pos, 0.0), axis=0, keepdims=True)
    rank2 = jnp.sum(jnp.where(hot2, pos, 0.0), axis=0, keepdims=True)
    cnt_ref[...] = cnt_ref[...] + jnp.sum(onehot, axis=1, keepdims=True)
    zi = jnp.zeros((4, t), jnp.int32)
    ri_ref[...] = jnp.concatenate([i1, i2, rank1.astype(jnp.int32), rank2.astype(jnp.int32), zi], axis=0)
    rf_ref[...] = jnp.concatenate([gate1, gate2, jnp.zeros((6, t), jnp.float32)], axis=0)


def _split_proj(proj):
    a = ATTN_DIM
    q = proj[:, :a]
    k = proj[:, a:a + KV_DIM]
    v = proj[:, a + KV_DIM:a + 2 * KV_DIM]
    c0 = a + 2 * KV_DIM
    h_conv = proj[:, c0:c0 + CONV_DIM]
    gate_b = proj[:, c0 + CONV_DIM:c0 + 2 * CONV_DIM]
    gate_c = proj[:, c0 + 2 * CONV_DIM:c0 + 3 * CONV_DIM]
    return q, k, v, gate_c * h_conv, gate_b


def _mixer_tail(x, attn_n, conv_n, mod, wout_ref, fnorm_ref, wr_ref, br_ref, tri_ref, cnt_ref,
                x1_ref, h2_ref, ri_ref, rf_ref):
    gate_a, shift_f, scale_f = mod
    cat = jnp.concatenate([attn_n.astype(jnp.bfloat16), conv_n.astype(jnp.bfloat16)], axis=1)
    mix = jnp.dot(cat, wout_ref[...], preferred_element_type=jnp.float32)
    x1 = x + gate_a * mix
    x1_ref[...] = x1
    h2 = _rms(x1, fnorm_ref[...]) * (1.0 + scale_f) + shift_f
    h2_ref[:, 0, :] = h2
    _route(h2.astype(jnp.bfloat16), wr_ref, br_ref, tri_ref, cnt_ref, ri_ref, rf_ref)


def _adaln_kernel(c_ref, w_ref, b_ref, o_ref):
    c = c_ref[...]
    a = (c * jax.nn.sigmoid(c)).astype(jnp.bfloat16)
    o_ref[...] = jnp.dot(a, w_ref[...].astype(jnp.bfloat16), preferred_element_type=jnp.float32) + b_ref[...]


def _adaln(c, w_mod, b_mod):
    n = c.shape[0]
    tn = 1536
    return pl.pallas_call(
        _adaln_kernel,
        grid=(w_mod.shape[1] // tn,),
        in_specs=[pl.BlockSpec((n, D_MODEL), lambda j: (0, 0)),
                  pl.BlockSpec((D_MODEL, tn), lambda j: (0, j)),
                  pl.BlockSpec((1, tn), lambda j: (0, j))],
        out_specs=pl.BlockSpec((n, tn), lambda j: (0, j)),
        out_shape=jax.ShapeDtypeStruct((n, w_mod.shape[1]), jnp.float32),
        compiler_params=pltpu.CompilerParams(dimension_semantics=("arbitrary",), vmem_limit_bytes=VMEM_LIMIT),
        name="adaln",
    )(c, w_mod, b_mod.reshape(1, -1))


def _prompt_mixer_kernel(sinks_ref, x_ref, mod_ref, anorm_ref, fnorm_ref, win_ref, cos_ref, sin_ref, bias_ref,
                         convw_ref, gna_ref, gnc_ref, wout_ref, wr_ref, br_ref, tri_ref,
                         x1_ref, h2_ref, ri_ref, rf_ref, cnt_out_ref, knew_ref, vnew_ref, unew_ref,
                         q_s, k_s, ksw_s, v_s, vsw_s, u_s, att_s, cnt_s):
    b = pl.program_id(0)
    t = pl.program_id(1)
    tm = MIXER_TILE
    nblk = tm // WINDOW

    @pl.when(jnp.logical_and(b == 0, t == 0))
    def _():
        cnt_s[...] = jnp.zeros_like(cnt_s)

    @pl.when(t == 0)
    def _():
        z = jnp.zeros((WINDOW, KV_DIM), jnp.bfloat16)
        k_s[0:WINDOW, :] = z
        ksw_s[0:WINDOW, :] = z
        v_s[0:WINDOW, :] = z
        vsw_s[0:WINDOW, :] = z
        u_s[0:8, :] = jnp.zeros((8, CONV_DIM), jnp.float32)

    x = x_ref[...]
    mod = mod_ref[...]
    h = _rms(x, anorm_ref[...]) * (1.0 + mod[1:2]) + mod[0:1]
    proj = jnp.dot(h.astype(jnp.bfloat16), win_ref[...], preferred_element_type=jnp.float32)
    q, k, v, u, gate_b = _split_proj(proj)
    cos = cos_ref[...]
    sin = sin_ref[...]
    q = _rope(q, cos, sin) * (HEAD_DIM ** -0.5)
    k = _rope(k, cos, sin)
    q_s[...] = q.astype(jnp.bfloat16)
    k_s[WINDOW:, :] = k.astype(jnp.bfloat16)
    ksw_s[WINDOW:, :] = _swap_halves(k).astype(jnp.bfloat16)
    v_s[WINDOW:, :] = v.astype(jnp.bfloat16)
    vsw_s[WINDOW:, :] = _swap_halves(v).astype(jnp.bfloat16)
    knew_ref[...] = k[tm - WINDOW:, :]
    vnew_ref[...] = v[tm - WINDOW:, :]

    u_s[8:, :] = u
    unew_ref[...] = u[tm - 8:, :]
    cw = convw_ref[...]
    conv = (u_s[6:6 + tm, :] * cw[0:1] + u_s[7:7 + tm, :] * cw[1:2] + u * cw[2:3]) * gate_b
    u_s[0:8, :] = u[tm - 8:, :]
    conv_n = _group_norm64(conv, gnc_ref[...])

    lo = _half_lane_mask(WINDOW)
    first = t == 0

    def block(i, carry):
        r0 = pl.multiple_of(i * WINDOW, WINDOW)
        bias = bias_ref[jnp.where(jnp.logical_and(first, i == 0), 0, 1)]
        qb = q_s[pl.ds(r0, WINDOW), :]
        keys = k_s[pl.ds(r0, 2 * WINDOW), :]
        keys_sw = ksw_s[pl.ds(r0, 2 * WINDOW), :]
        vals = v_s[pl.ds(r0, 2 * WINDOW), :]
        vals_sw = vsw_s[pl.ds(r0, 2 * WINDOW), :]
        outs = []
        for g in range(N_KV_HEADS):
            outs += _attend(qb, keys, keys_sw, vals, vals_sw, bias, sinks_ref, g)
        att_s[pl.ds(r0, WINDOW), :] = _merge_heads(outs, lo)
        return carry

    lax.fori_loop(0, nblk, block, 0, unroll=True)
    k_s[0:WINDOW, :] = k_s[tm:tm + WINDOW, :]
    ksw_s[0:WINDOW, :] = ksw_s[tm:tm + WINDOW, :]
    v_s[0:WINDOW, :] = v_s[tm:tm + WINDOW, :]
    vsw_s[0:WINDOW, :] = vsw_s[tm:tm + WINDOW, :]

    attn_n = att_s[...] * gna_ref[...]
    _mixer_tail(x, attn_n, conv_n, (mod[2:3], mod[3:4], mod[4:5]), wout_ref, fnorm_ref, wr_ref, br_ref, tri_ref,
                cnt_s, x1_ref, h2_ref, ri_ref, rf_ref)
    cnt_out_ref[...] = cnt_s[...]


def _prompt_mixer(x, mod, sinks, consts):
    bsz, seq, _ = x.shape
    tm = MIXER_TILE
    nt = seq // tm
    full = lambda shape: pl.BlockSpec(shape, lambda b, t, s: (0,) * len(shape))
    in_specs = [
        pl.BlockSpec((None, tm, D_MODEL), lambda b, t, s: (b, t, 0)),
        pl.BlockSpec((None, 6, D_MODEL), lambda b, t, s: (b, 0, 0)),
        full((1, D_MODEL)), full((1, D_MODEL)),
        full((D_MODEL, IN_PROJ_DIM)),
        pl.BlockSpec((tm, LANES), lambda b, t, s: (t, 0)),
        pl.BlockSpec((tm, LANES), lambda b, t, s: (t, 0)),
        full((2, WINDOW, 2 * WINDOW)),
        full((3, CONV_DIM)), full((1, ATTN_DIM)), full((1, CONV_DIM)),
        full((D_MODEL, D_MODEL)),
        full((ROUTER_ROWS, D_MODEL)), full((ROUTER_ROWS, LANES)),
        full((tm, tm)),
    ]
    out_shape = [
        jax.ShapeDtypeStruct((bsz * seq, D_MODEL), jnp.float32),
        jax.ShapeDtypeStruct((bsz * seq, 1, D_MODEL), jnp.float32),
        jax.ShapeDtypeStruct((bsz * nt, 8, tm), jnp.int32),
        jax.ShapeDtypeStruct((bsz * nt, 8, tm), jnp.float32),
        jax.ShapeDtypeStruct((N_EXPERTS, LANES), jnp.float32),
        jax.ShapeDtypeStruct((bsz, WINDOW, KV_DIM), jnp.float32),
        jax.ShapeDtypeStruct((bsz, WINDOW, KV_DIM), jnp.float32),
        jax.ShapeDtypeStruct((bsz, 8, CONV_DIM), jnp.float32),
    ]
    out_specs = [
        pl.BlockSpec((tm, D_MODEL), lambda b, t, s: (b * nt + t, 0)),
        pl.BlockSpec((tm, 1, D_MODEL), lambda b, t, s: (b * nt + t, 0, 0)),
        pl.BlockSpec((None, 8, tm), lambda b, t, s: (b * nt + t, 0, 0)),
        pl.BlockSpec((None, 8, tm), lambda b, t, s: (b * nt + t, 0, 0)),
        pl.BlockSpec((N_EXPERTS, LANES), lambda b, t, s: (0, 0)),
        pl.BlockSpec((None, WINDOW, KV_DIM), lambda b, t, s: (b, 0, 0)),
        pl.BlockSpec((None, WINDOW, KV_DIM), lambda b, t, s: (b, 0, 0)),
        pl.BlockSpec((None, 8, CONV_DIM), lambda b, t, s: (b, 0, 0)),
    ]
    scratch = [
        pltpu.VMEM((tm, ATTN_DIM), jnp.bfloat16),
        pltpu.VMEM((tm + WINDOW, KV_DIM), jnp.bfloat16), pltpu.VMEM((tm + WINDOW, KV_DIM), jnp.bfloat16),
        pltpu.VMEM((tm + WINDOW, KV_DIM), jnp.bfloat16), pltpu.VMEM((tm + WINDOW, KV_DIM), jnp.bfloat16),
        pltpu.VMEM((tm + 8, CONV_DIM), jnp.float32),
        pltpu.VMEM((tm, ATTN_DIM), jnp.float32),
        pltpu.VMEM((N_EXPERTS, LANES), jnp.float32),
    ]
    return pl.pallas_call(
        _prompt_mixer_kernel,
        grid_spec=pltpu.PrefetchScalarGridSpec(
            num_scalar_prefetch=1, grid=(bsz, nt), in_specs=in_specs, out_specs=out_specs, scratch_shapes=scratch),
        out_shape=out_shape,
        compiler_params=pltpu.CompilerParams(dimension_semantics=("arbitrary", "arbitrary"),
                                             vmem_limit_bytes=VMEM_LIMIT),
        name="prompt_mixer",
    )(sinks, x, mod, consts["anorm"], consts["fnorm"], consts["w_in"], consts["cos_p"], consts["sin_p"],
      consts["bias_p"], consts["conv_w"], consts["gn_attn"], consts["gn_conv"], consts["w_out"],
      consts["w_r"], consts["b_r"], consts["tri"])


def _sample_mixer_kernel(sinks_ref, x_ref, mod_ref, anorm_ref, fnorm_ref, win_ref, cos_ref, sin_ref, bias_ref,
                         convw_ref, gna_ref, gnc_ref, wout_ref, wr_ref, br_ref, tri_ref,
                         ck_ref, cv_ref, e1_ref, e2_ref, cnt_in_ref,
                         x1_ref, h2_ref, ri_ref, rf_ref, cnt_out_ref, knew_ref, vnew_ref, unew_ref,
                         q_s, k_s, v_s, att_s, cnt_s, *, dec):
    n = x_ref.shape[0]
    cb = SAMPLE_CHUNK
    rows = cb * dec
    pad_keys = SAMPLE_KEYS - cb * WINDOW - rows

    @pl.when(pl.program_id(0) == 0)
    def _():
        cnt_s[...] = cnt_in_ref[...]

    x = x_ref[...]
    mod = mod_ref[...]
    md = lambda j: mod[:, j * D_MODEL:(j + 1) * D_MODEL]
    h = _rms(x, anorm_ref[...]) * (1.0 + md(1)) + md(0)
    proj = jnp.dot(h.astype(jnp.bfloat16), win_ref[...], preferred_element_type=jnp.float32)
    q, k, v, u, gate_b = _split_proj(proj)
    cos = cos_ref[...]
    sin = sin_ref[...]
    q = _rope(q, cos, sin) * (HEAD_DIM ** -0.5)
    k = _rope(k, cos, sin)
    q_s[...] = q.astype(jnp.bfloat16)
    k_s[...] = k
    v_s[...] = v
    knew_ref[...] = k
    vnew_ref[...] = v
    unew_ref[...] = u

    pos = lax.broadcasted_iota(jnp.int32, (n, CONV_DIM), 0) % dec
    um1 = jnp.where(pos >= 1, pltpu.roll(u, 1, 0), 0.0) + e1_ref[...]
    um2 = jnp.where(pos >= 2, pltpu.roll(u, 2, 0), 0.0) + e2_ref[...]
    cw = convw_ref[...]
    conv = (um2 * cw[0:1] + um1 * cw[1:2] + u * cw[2:3]) * gate_b
    conv_n = _group_norm64(conv, gnc_ref[...])

    lo = _half_lane_mask(rows)
    bias = bias_ref[...]

    def chunk(c, carry):
        r0 = pl.multiple_of(c * rows, rows)
        c0 = pl.multiple_of(c * (cb * WINDOW), cb * WINDOW)
        zpad = jnp.zeros((pad_keys, KV_DIM), jnp.float32)
        kc = jnp.concatenate([ck_ref[pl.ds(c0, cb * WINDOW), :], k_s[pl.ds(r0, rows), :], zpad], axis=0)
        vc = jnp.concatenate([cv_ref[pl.ds(c0, cb * WINDOW), :], v_s[pl.ds(r0, rows), :], zpad], axis=0)
        keys, keys_sw = kc.astype(jnp.bfloat16), _swap_halves(kc).astype(jnp.bfloat16)
        vals, vals_sw = vc.astype(jnp.bfloat16), _swap_halves(vc).astype(jnp.bfloat16)
        qb = q_s[pl.ds(r0, rows), :]
        outs = []
        for g in range(N_KV_HEADS):
            outs += _attend(qb, keys, keys_sw, vals, vals_sw, bias, sinks_ref, g)
        att_s[pl.ds(r0, rows), :] = _merge_heads(outs, lo)
        return carry

    lax.fori_loop(0, n // rows, chunk, 0)
    attn_n = att_s[...] * gna_ref[...]
    _mixer_tail(x, attn_n, conv_n, (md(2), md(3), md(4)), wout_ref, fnorm_ref, wr_ref, br_ref, tri_ref,
                cnt_s, x1_ref, h2_ref, ri_ref, rf_ref)
    cnt_out_ref[...] = cnt_s[...]


def _sample_mixer(x, mod, sinks, consts, ck, cv, e1, e2, cnt_in, dec):
    n_s = x.shape[0]
    ts = SAMPLE_TILE
    seqs = ts // dec
    full = lambda a: pl.BlockSpec(a.shape, lambda i, s: (0,) * a.ndim)
    rows = lambda width: pl.BlockSpec((ts, width), lambda i, s: (i, 0))
    in_specs = [
        rows(D_MODEL), rows(6 * D_MODEL),
        full(consts["anorm"]), full(consts["fnorm"]), full(consts["w_in"]),
        rows(LANES), rows(LANES), full(consts["bias_s"]),
        full(consts["conv_w"]), full(consts["gn_attn"]), full(consts["gn_conv"]), full(consts["w_out"]),
        full(consts["w_r"]), full(consts["b_r"]),
        pl.BlockSpec((ts, ts), lambda i, s: (0, 0)),
        pl.BlockSpec((seqs * WINDOW, KV_DIM), lambda i, s: (i, 0)),
        pl.BlockSpec((seqs * WINDOW, KV_DIM), lambda i, s: (i, 0)),
        rows(CONV_DIM), rows(CONV_DIM), full(cnt_in),
    ]
    args = [x, mod, consts["anorm"], consts["fnorm"], consts["w_in"], consts["cos_s"], consts["sin_s"],
            consts["bias_s"], consts["conv_w"], consts["gn_attn"], consts["gn_conv"], consts["w_out"],
            consts["w_r"], consts["b_r"], consts["tri"], ck, cv, e1, e2, cnt_in]
    nt = n_s // ts
    out_shape = [
        jax.ShapeDtypeStruct((n_s, D_MODEL), jnp.float32),
        jax.ShapeDtypeStruct((n_s, 1, D_MODEL), jnp.float32),
        jax.ShapeDtypeStruct((nt, 8, ts), jnp.int32),
        jax.ShapeDtypeStruct((nt, 8, ts), jnp.float32),
        jax.ShapeDtypeStruct((N_EXPERTS, LANES), jnp.float32),
        jax.ShapeDtypeStruct((n_s, KV_DIM), jnp.float32),
        jax.ShapeDtypeStruct((n_s, KV_DIM), jnp.float32),
        jax.ShapeDtypeStruct((n_s, CONV_DIM), jnp.float32),
    ]
    out_specs = [
        rows(D_MODEL),
        pl.BlockSpec((ts, 1, D_MODEL), lambda i, s: (i, 0, 0)),
        pl.BlockSpec((None, 8, ts), lambda i, s: (i, 0, 0)),
        pl.BlockSpec((None, 8, ts), lambda i, s: (i, 0, 0)),
        pl.BlockSpec((N_EXPERTS, LANES), lambda i, s: (0, 0)),
        rows(KV_DIM), rows(KV_DIM), rows(CONV_DIM),
    ]
    scratch = [
        pltpu.VMEM((ts, ATTN_DIM), jnp.bfloat16),
        pltpu.VMEM((ts, KV_DIM), jnp.float32), pltpu.VMEM((ts, KV_DIM), jnp.float32),
        pltpu.VMEM((ts, ATTN_DIM), jnp.float32),
        pltpu.VMEM((N_EXPERTS, LANES), jnp.float32),
    ]
    return pl.pallas_call(
        functools.partial(_sample_mixer_kernel, dec=dec),
        grid_spec=pltpu.PrefetchScalarGridSpec(
            num_scalar_prefetch=1, grid=(nt,), in_specs=in_specs, out_specs=out_specs, scratch_shapes=scratch),
        out_shape=out_shape,
        compiler_params=pltpu.CompilerParams(dimension_semantics=("arbitrary",), vmem_limit_bytes=VMEM_LIMIT),
        name="sample_mixer",
    )(sinks, *args)


def _dispatch_kernel(dest_ref, cnt_ref, padded_ref, start_ref, h2p_ref, h2s_ref, xs_ref,
                     vbuf, zbuf, sem_in, sem_out, sem_z):
    c = pl.program_id(0)
    n = pl.num_programs(0)
    chunk = DISPATCH_CHUNK
    n_p = h2p_ref.shape[0] // chunk
    n_tok = h2p_ref.shape[0] + h2s_ref.shape[0]
    nbuf = vbuf.shape[0]

    def load(step, slot):
        @pl.when(step < n_p)
        def _():
            pltpu.make_async_copy(h2p_ref.at[pl.ds(step * chunk, chunk)], vbuf.at[slot], sem_in.at[slot]).start()

        @pl.when(step >= n_p)
        def _():
            pltpu.make_async_copy(h2s_ref.at[pl.ds((step - n_p) * chunk, chunk)], vbuf.at[slot],
                                  sem_in.at[slot]).start()

    def wait_scatter(slot):
        for _ in range(2):
            pltpu.make_async_copy(vbuf.at[slot], xs_ref.at[pl.ds(0, chunk)], sem_out.at[slot]).wait()

    slot = c % nbuf
    nxt = (c + 1) % nbuf

    @pl.when(c == 0)
    def _():
        load(0, 0)
        zbuf[...] = jnp.zeros_like(zbuf)

    @pl.when(c >= nbuf - 1)
    def _():
        wait_scatter(nxt)

    @pl.when(c + 1 < n)
    def _():
        load(c + 1, nxt)

    pltpu.make_async_copy(h2p_ref.at[pl.ds(0, chunk)], vbuf.at[slot], sem_in.at[slot]).wait()

    def issue(i, carry):
        tok = c * chunk + i
        for k in range(2):
            d = dest_ref[k * n_tok + tok]
            pltpu.make_async_copy(vbuf.at[slot, pl.ds(i, 1)], xs_ref.at[pl.ds(d, 1)], sem_out.at[slot]).start()
        return carry

    lax.fori_loop(0, chunk, issue, 0, unroll=8)

    @pl.when(c == n - 1)
    def _():
        for back in range(nbuf - 2, -1, -1):
            @pl.when(c >= back)
            def _():
                wait_scatter((c - back) % nbuf)

        def zero_row(e, r):
            return pltpu.make_async_copy(zbuf.at[pl.ds(0, 1)], xs_ref.at[pl.ds(start_ref[e] + r, 1)], sem_z.at[0])

        def per_expert(wait):
            def body(e, carry):
                def row(r, carry2):
                    if wait:
                        zero_row(e, r).wait()
                    else:
                        zero_row(e, r).start()
                    return carry2
                lax.fori_loop(cnt_ref[e], padded_ref[e], row, 0)
                return carry
            lax.fori_loop(0, N_EXPERTS, body, 0)

        per_expert(False)
        per_expert(True)

        bm = EXPERT_BLOCK
        vbuf[0] = jnp.zeros((chunk, 1, D_MODEL), jnp.float32)
        first_free = (start_ref[N_EXPERTS - 1] + padded_ref[N_EXPERTS - 1]) // bm

        def zero_block(j):
            return pltpu.make_async_copy(vbuf.at[0, pl.ds(0, bm)], xs_ref.at[pl.ds(j * bm, bm)], sem_z.at[0])

        def start_block(j, carry):
            zero_block(j).start()
            return carry

        def wait_block(j, carry):
            zero_block(j).wait()
            return carry

        lax.fori_loop(first_free, xs_ref.shape[0] // bm, start_block, 0)
        lax.fori_loop(first_free, xs_ref.shape[0] // bm, wait_block, 0)


def _dispatch(dest_flat, counts, padded, pstart, h2_p, h2_s, n_rows):
    n_tok = h2_p.shape[0] + h2_s.shape[0]
    assert h2_p.shape[0] % DISPATCH_CHUNK == 0 and h2_s.shape[0] % DISPATCH_CHUNK == 0
    hbm = pl.BlockSpec(memory_space=pl.ANY)
    return pl.pallas_call(
        _dispatch_kernel,
        grid_spec=pltpu.PrefetchScalarGridSpec(
            num_scalar_prefetch=4, grid=(n_tok // DISPATCH_CHUNK,),
            in_specs=[hbm, hbm], out_specs=hbm,
            scratch_shapes=[pltpu.VMEM((3, DISPATCH_CHUNK, 1, D_MODEL), jnp.float32),
                            pltpu.VMEM((8, 1, D_MODEL), jnp.float32),
                            pltpu.SemaphoreType.DMA((3,)), pltpu.SemaphoreType.DMA((3,)),
                            pltpu.SemaphoreType.DMA((1,))]),
        out_shape=jax.ShapeDtypeStruct((n_rows, 1, D_MODEL), jnp.float32),
        compiler_params=pltpu.CompilerParams(dimension_semantics=("arbitrary",), vmem_limit_bytes=VMEM_LIMIT),
        name="dispatch",
    )(dest_flat, counts, padded, pstart, h2_p, h2_s)


def _experts_kernel(be_ref, nu_ref, xs_ref, wg_ref, wu_ref, wd_ref, y_ref, wg_s, wu_s, wd_s, x_s):
    j = pl.program_id(0)
    used = j < nu_ref[0]
    changed = jnp.logical_or(j == 0, be_ref[j] != be_ref[jnp.maximum(j - 1, 0)])

    @pl.when(jnp.logical_and(used, changed))
    def _():
        wg_s[...] = wg_ref[...].astype(jnp.bfloat16)
        wu_s[...] = wu_ref[...].astype(jnp.bfloat16)
        wd_s[...] = wd_ref[...].astype(jnp.bfloat16)

    @pl.when(used)
    def _():
        x_s[...] = xs_ref[...].reshape(x_s.shape)
        x = x_s[...].astype(jnp.bfloat16)
        g = jnp.dot(x, wg_s[...], preferred_element_type=jnp.float32)
        u = jnp.dot(x, wu_s[...], preferred_element_type=jnp.float32)
        a = (g * jax.nn.sigmoid(g) * u).astype(jnp.bfloat16)
        y_ref[...] = jnp.dot(a, wd_s[...], preferred_element_type=jnp.float32).reshape(y_ref.shape)

    @pl.when(jnp.logical_not(used))
    def _():
        y_ref[...] = jnp.zeros_like(y_ref)


def _experts(block_e, n_used, xs, w_gate, w_up, w_down):
    bm = EXPERT_BLOCK
    n_blocks = xs.shape[0] // bm
    row = lambda j, be, nu: (jnp.minimum(j, nu[0] - 1), 0, 0)
    wmap = lambda j, be, nu: (be[j], 0, 0)
    return pl.pallas_call(
        _experts_kernel,
        grid_spec=pltpu.PrefetchScalarGridSpec(
            num_scalar_prefetch=2, grid=(n_blocks,),
            in_specs=[pl.BlockSpec((bm, 1, D_MODEL), row),
                      pl.BlockSpec((None, D_MODEL, EXPERT_HIDDEN), wmap),
                      pl.BlockSpec((None, D_MODEL, EXPERT_HIDDEN), wmap),
                      pl.BlockSpec((None, EXPERT_HIDDEN, D_MODEL), wmap)],
            out_specs=pl.BlockSpec((bm, 1, D_MODEL), lambda j, be, nu: (j, 0, 0)),
            scratch_shapes=[pltpu.VMEM((D_MODEL, EXPERT_HIDDEN), jnp.bfloat16),
                            pltpu.VMEM((D_MODEL, EXPERT_HIDDEN), jnp.bfloat16),
                            pltpu.VMEM((EXPERT_HIDDEN, D_MODEL), jnp.bfloat16),
                            pltpu.VMEM((bm, D_MODEL), jnp.float32)]),
        out_shape=jax.ShapeDtypeStruct(xs.shape, jnp.float32),
        compiler_params=pltpu.CompilerParams(dimension_semantics=("arbitrary",), vmem_limit_bytes=VMEM_LIMIT),
        name="experts",
    )(block_e, n_used, xs, w_gate, w_up, w_down)


def _combine_kernel(dest_ref, x1_ref, gate_ref, rf_ref, fin_ref, yb_ref, o_ref, ybuf, y_s, sem, *,
                    n_tok, per_token_gate):
    i = pl.program_id(0)
    n = pl.num_programs(0)
    tc = COMBINE_TILE

    def issue(step, slot):
        def one(r, carry):
            tok = step * tc + r
            for k in range(2):
                d = dest_ref[k * n_tok + tok]
                pltpu.make_async_copy(yb_ref.at[pl.ds(d, 1)], ybuf.at[slot, k, pl.ds(r, 1)], sem.at[slot]).start()
            return carry
        lax.fori_loop(0, tc, one, 0, unroll=8)

    slot = i % 2

    @pl.when(i == 0)
    def _():
        issue(0, 0)

    @pl.when(i + 1 < n)
    def _():
        issue(i + 1, 1 - slot)

    pltpu.make_async_copy(yb_ref.at[pl.ds(0, tc)], ybuf.at[slot, 0], sem.at[slot]).wait()
    pltpu.make_async_copy(yb_ref.at[pl.ds(0, tc)], ybuf.at[slot, 1], sem.at[slot]).wait()

    gates = jnp.transpose(rf_ref[...])
    y_s[0] = ybuf[slot, 0].reshape(tc, D_MODEL)
    y_s[1] = ybuf[slot, 1].reshape(tc, D_MODEL)
    y0 = y_s[0]
    y1 = y_s[1]
    moe = y0 * gates[:, 0:1] + y1 * gates[:, 1:2]
    gate_f = gate_ref[...] if per_token_gate else gate_ref[...][5:6]
    o_ref[...] = _rms(x1_ref[...] + gate_f * moe, fin_ref[...])


def _combine(dest_flat, x1, gate, rf, final_norm, yb, *, tiles_per_gate):
    n_tok = x1.shape[0]
    tc = COMBINE_TILE
    per_token = tiles_per_gate == 0
    if per_token:
        gate_spec = pl.BlockSpec((tc, D_MODEL), lambda i, d: (i, 0))
    else:
        gate_spec = pl.BlockSpec((None, 6, D_MODEL), lambda i, d: (i // tiles_per_gate, 0, 0))
    return pl.pallas_call(
        functools.partial(_combine_kernel, n_tok=n_tok, per_token_gate=per_token),
        grid_spec=pltpu.PrefetchScalarGridSpec(
            num_scalar_prefetch=1, grid=(n_tok // tc,),
            in_specs=[pl.BlockSpec((tc, D_MODEL), lambda i, d: (i, 0)),
                      gate_spec,
                      pl.BlockSpec((8, tc), lambda i, d: (0, i)),
                      pl.BlockSpec((1, D_MODEL), lambda i, d: (0, 0)),
                      pl.BlockSpec(memory_space=pl.ANY)],
            out_specs=pl.BlockSpec((tc, D_MODEL), lambda i, d: (i, 0)),
            scratch_shapes=[pltpu.VMEM((2, 2, tc, 1, D_MODEL), jnp.float32),
                            pltpu.VMEM((2, tc, D_MODEL), jnp.float32), pltpu.SemaphoreType.DMA((2,))]),
        out_shape=jax.ShapeDtypeStruct((n_tok, D_MODEL), jnp.float32),
        compiler_params=pltpu.CompilerParams(dimension_semantics=("arbitrary",), vmem_limit_bytes=VMEM_LIMIT),
        name="combine",
    )(dest_flat, x1, gate, rf, final_norm, yb)


def _rope_tables(pos):
    half = HEAD_DIM // 2
    inv_freq = np.power(np.float32(ROPE_THETA), -(np.arange(half, dtype=np.float32) / np.float32(half)))
    ang = (pos.astype(np.float32)[:, None] * inv_freq[None, :].astype(np.float32)).astype(np.float64)
    cos, sin = np.cos(ang).astype(np.float32), np.sin(ang).astype(np.float32)
    cos = np.concatenate([cos, cos, cos, cos], axis=1)
    sin = np.concatenate([-sin, sin, -sin, sin], axis=1)
    return cos, sin


def _prompt_bias():
    r = np.arange(WINDOW)[:, None]
    j = np.arange(2 * WINDOW)[None, :]
    band = (j > r) & (j <= r + WINDOW)
    later = np.where(band, 0.0, NEG).astype(np.float32)
    first = np.where(band & (j >= WINDOW), 0.0, NEG).astype(np.float32)
    return jnp.asarray(np.stack([first, later]))


def _sample_bias(dec):
    cb = SAMPLE_CHUNK
    qb = np.repeat(np.arange(cb), dec)[:, None]
    qi = np.tile(np.arange(dec), cb)[:, None]
    kb = np.concatenate([np.repeat(np.arange(cb), WINDOW), np.repeat(np.arange(cb), dec)])[None, :]
    kj = np.concatenate([np.tile(np.arange(WINDOW), cb), np.tile(np.arange(dec), cb)])[None, :]
    is_new = (np.arange(cb * WINDOW + cb * dec) >= cb * WINDOW)[None, :]
    ok = (kb == qb) & np.where(is_new, kj <= qi, kj > qi)
    ok = np.pad(ok, ((0, 0), (0, SAMPLE_KEYS - ok.shape[1])))
    return jnp.asarray(np.where(ok, 0.0, NEG).astype(np.float32))


def kernel(x_prompt, x_sample, cache_k, cache_v, state_conv, c_prompt, c_sample, attn_norm, ffn_norm, w_mod, b_mod, w_in, conv_w, attn_sinks, out_norm_attn, out_norm_conv, w_out, w_group, b_group, w_expert, b_expert, w_gate, w_up, w_down, final_norm):
    assert attn_norm.shape[0] == 1, "one layer"
    bsz, seq, _ = x_prompt.shape
    nseq, dec, _ = x_sample.shape
    m_prompt = bsz * seq
    n_s = nseq * dec
    m_total = m_prompt + n_s
    tm = MIXER_TILE
    assert seq % tm == 0 and n_s % SAMPLE_TILE == 0 and SAMPLE_TILE % (SAMPLE_CHUNK * dec) == 0
    assert cache_k.shape[2] == WINDOW and m_prompt % SAMPLE_TILE == 0 and m_total % COMBINE_TILE == 0
    bf = jnp.bfloat16

    w_r = jnp.zeros((ROUTER_ROWS, D_MODEL), jnp.float32)
    w_r = w_r.at[0:N_GROUPS].set(w_group[0].T).at[8:8 + N_EXPERTS].set(w_expert[0].T).astype(bf)
    b_r = jnp.zeros((ROUTER_ROWS,), jnp.float32).at[0:N_GROUPS].set(b_group[0]).at[8:8 + N_EXPERTS].set(b_expert[0])
    b_r = jnp.broadcast_to(b_r[:, None], (ROUTER_ROWS, LANES))
    cos_p, sin_p = _rope_tables(np.arange(seq))
    cos_s, sin_s = _rope_tables(PAST_LEN + np.arange(dec))
    tri = jnp.asarray(np.triu(np.ones((tm, tm), np.float32), 1)).astype(bf)
    consts = dict(
        anorm=attn_norm[0][None], fnorm=ffn_norm[0][None], w_in=w_in[0].astype(bf),
        cos_p=jnp.asarray(cos_p), sin_p=jnp.asarray(sin_p),
        cos_s=jnp.asarray(np.tile(cos_s, (nseq, 1))), sin_s=jnp.asarray(np.tile(sin_s, (nseq, 1))),
        bias_p=_prompt_bias(), bias_s=_sample_bias(dec), conv_w=conv_w[0],
        gn_attn=out_norm_attn[0][None], gn_conv=out_norm_conv[0][None], w_out=w_out[0].astype(bf),
        w_r=w_r, b_r=b_r, tri=tri)
    sinks = attn_sinks[0]

    mod = _adaln(jnp.concatenate([c_prompt, c_sample], axis=0), w_mod[0], b_mod[0])
    mod_p = mod[:bsz].reshape(bsz, 6, D_MODEL)
    mod_s = jnp.repeat(mod[bsz:], dec, axis=0)

    x1_p, h2_p, ri_p, rf_p, cnt_p, k_p, v_p, u_p = _prompt_mixer(x_prompt, mod_p, sinks, consts)
    st = state_conv[0]
    zrow = jnp.zeros((nseq, 1, CONV_DIM), jnp.float32)
    e1 = jnp.concatenate([st[:, 1:2]] + [zrow] * (dec - 1), axis=1).reshape(n_s, CONV_DIM)
    e2 = jnp.concatenate([st[:, 0:1], st[:, 1:2]] + [zrow] * (dec - 2), axis=1).reshape(n_s, CONV_DIM)
    ck = cache_k[0].reshape(nseq * WINDOW, KV_DIM)
    cv = cache_v[0].reshape(nseq * WINDOW, KV_DIM)
    x1_s, h2_s, ri_s, rf_s, cnt, k_s, v_s, u_s = _sample_mixer(
        x_sample.reshape(n_s, D_MODEL), mod_s, sinks, consts, ck, cv, e1, e2, cnt_p, dec)

    bm = EXPERT_BLOCK
    counts = cnt[:, 0].astype(jnp.int32)
    padded = (counts + bm - 1) // bm * bm
    pend = jnp.cumsum(padded)
    pstart = pend - padded
    n_blocks = -(-(2 * m_total + N_EXPERTS * (bm - 1)) // bm)
    n_used = (pend[-1] // bm).astype(jnp.int32)
    blk = jnp.arange(n_blocks, dtype=jnp.int32)
    eids = jnp.arange(N_EXPERTS, dtype=jnp.int32)
    block_e = jnp.sum((pend[None, :] <= (blk * bm)[:, None]).astype(jnp.int32), axis=1)
    last_e = jnp.max(jnp.where(padded > 0, eids, 0))
    block_e = jnp.where(blk < n_used, jnp.minimum(block_e, N_EXPERTS - 1), last_e)
    by_token = lambda a: jnp.transpose(a, (1, 0, 2)).reshape(8, -1)
    ri = jnp.concatenate([by_token(ri_p), by_token(ri_s)], axis=1)
    rf_flat = jnp.concatenate([by_token(rf_p), by_token(rf_s)], axis=1)
    start_of = jnp.sum(jnp.where(ri[0:2][..., None] == eids, pstart, 0), axis=-1)
    dest_flat = (start_of + ri[2:4]).reshape(2 * m_total)

    xs = _dispatch(dest_flat, counts, padded, pstart, h2_p, h2_s, n_blocks * bm)
    yb = _experts(block_e, n_used.reshape(1), xs, w_gate[0], w_up[0], w_down[0])

    fin = final_norm[None]
    dest_p = jnp.concatenate([dest_flat[:m_prompt], dest_flat[m_total:m_total + m_prompt]])
    dest_s = jnp.concatenate([dest_flat[m_prompt:m_total], dest_flat[m_total + m_prompt:]])
    y_p = _combine(dest_p, x1_p, mod_p, rf_flat[:, :m_prompt], fin, yb, tiles_per_gate=seq // COMBINE_TILE)
    gate_s = mod_s[:, 5 * D_MODEL:]
    y_s = _combine(dest_s, x1_s, gate_s, rf_flat[:, m_prompt:], fin, yb, tiles_per_gate=0)

    keep = WINDOW
    new_k_p = k_p.reshape(1, bsz, keep, N_KV_HEADS, HEAD_DIM)
    new_v_p = v_p.reshape(1, bsz, keep, N_KV_HEADS, HEAD_DIM)
    new_u_p = u_p[:, 6:8][None]
    ksn = k_s.reshape(nseq, dec, N_KV_HEADS, HEAD_DIM)
    vsn = v_s.reshape(nseq, dec, N_KV_HEADS, HEAD_DIM)
    new_k_s = jnp.concatenate([cache_k[0][:, dec:], ksn], axis=1)[None]
    new_v_s = jnp.concatenate([cache_v[0][:, dec:], vsn], axis=1)[None]
    u_all = jnp.concatenate([st, u_s.reshape(nseq, dec, CONV_DIM)], axis=1)
    new_u_s = u_all[:, -2:][None]
    return (y_p.reshape(bsz, seq, D_MODEL), y_s.reshape(nseq, dec, D_MODEL),
            new_k_p, new_v_p, new_u_p, new_k_s, new_v_s, new_u_s)
```

```python
import functools

import jax
import jax.numpy as jnp
import numpy as np
from jax import lax
from jax.experimental import pallas as pl
from jax.experimental.pallas import tpu as pltpu
from jax.experimental.pallas import tpu_sc as plsc

D_MODEL = 1024
HEAD_DIM = 64
ATTN_DIM = 512
N_Q_HEADS = 8
N_KV_HEADS = 2
KV_DIM = 128
CONV_DIM = 512
WINDOW = 128
IN_PROJ_DIM = ATTN_DIM + 2 * KV_DIM + 3 * CONV_DIM
N_GROUPS = 4
EXPERTS_PER_GROUP = 8
N_EXPERTS = 32
EXPERT_HIDDEN = 512
RMS_EPS = 1e-5
ROPE_THETA = 10000.0
PAST_LEN = 16384

LANES = 128
ROUTER_ROWS = 48
NEG = -1e30

MIXER_TILE = 512
SAMPLE_TILE = 128
SAMPLE_CHUNK = 8
SAMPLE_KEYS = 1152
EXPERT_BLOCK = 512
COMBINE_TILE = 256
SC_ROWS = 32
SC_INDEX_WINDOW = 128
VMEM_LIMIT = 56 * 1024 * 1024


def _rms(x, gain):
    ms = jnp.mean(x * x, axis=-1, keepdims=True)
    return x * lax.rsqrt(ms + RMS_EPS) * gain


def _half_lane_mask(rows=1):
    return lax.broadcasted_iota(jnp.int32, (rows, LANES), 1) < HEAD_DIM


def _rope(x, cos, sin_signed):
    t, w = x.shape
    reps = w // LANES
    lane = lax.broadcasted_iota(jnp.int32, (t, w), 1)
    upper = (lane % HEAD_DIM) >= (HEAD_DIM // 2)
    partner = jnp.where(upper, pltpu.roll(x, HEAD_DIM // 2, 1), pltpu.roll(x, w - HEAD_DIM // 2, 1))
    if reps > 1:
        cos = jnp.concatenate([cos] * reps, axis=1)
        sin_signed = jnp.concatenate([sin_signed] * reps, axis=1)
    return x * cos + partner * sin_signed


def _swap_halves(x):
    return pltpu.roll(x, HEAD_DIM, 1)


def _group_norm64(x, gain):
    t, w = x.shape
    lo = _half_lane_mask(t)
    outs = []
    for c in range(w // LANES):
        xc = x[:, c * LANES:(c + 1) * LANES]
        sq = xc * xc
        s_lo = jnp.sum(jnp.where(lo, sq, 0.0), axis=-1, keepdims=True)
        s_hi = jnp.sum(jnp.where(lo, 0.0, sq), axis=-1, keepdims=True)
        r = jnp.where(lo, lax.rsqrt(s_lo * (1.0 / HEAD_DIM) + RMS_EPS), lax.rsqrt(s_hi * (1.0 / HEAD_DIM) + RMS_EPS))
        outs.append(xc * r)
    return jnp.concatenate(outs, axis=1) * gain


def _attend(q_rows, keys, keys_sw, vals, vals_sw, bias, sinks_ref, group):
    r = q_rows.shape[0]
    lo = _half_lane_mask(r)
    outs = []
    for parity in range(2):
        heads = [4 * group + parity, 4 * group + parity + 2]
        kk, vv = (keys, vals) if parity == group else (keys_sw, vals_sw)
        half = lo if parity == 0 else jnp.logical_not(lo)
        qs = []
        for h in heads:
            pair = q_rows[:, (h // 2) * LANES:(h // 2 + 1) * LANES]
            qs.append(jnp.where(half, pair, jnp.zeros_like(pair)))
        qcat = jnp.concatenate(qs, axis=0)
        s = lax.dot_general(qcat, kk, (((1,), (1,)), ((), ())), preferred_element_type=jnp.float32)
        ps, dens = [], []
        for i, h in enumerate(heads):
            sh = s[i * r:(i + 1) * r] + bias
            sink = sinks_ref[h]
            m = jnp.maximum(jnp.max(sh, axis=-1, keepdims=True), sink)
            p = jnp.exp(sh - m)
            dens.append(jnp.sum(p, axis=-1, keepdims=True) + jnp.exp(sink - m))
            ps.append(p.astype(jnp.bfloat16))
        o = jnp.dot(jnp.concatenate(ps, axis=0), vv, preferred_element_type=jnp.float32)
        for i, h in enumerate(heads):
            oh = o[i * r:(i + 1) * r] / dens[i]
            ms = jnp.sum(jnp.where(half, oh * oh, 0.0), axis=-1, keepdims=True) * (1.0 / HEAD_DIM)
            outs.append((h, oh * lax.rsqrt(ms + RMS_EPS)))
    outs.sort(key=lambda t: t[0])
    return [o for _, o in outs]


def _merge_heads(head_outs, lo):
    pairs = [jnp.where(lo, head_outs[2 * j], head_outs[2 * j + 1]) for j in range(4)]
    return jnp.concatenate(pairs, axis=1)


def _route(h2b, wr_ref, br_ref, tri_ref, cnt_ref, ri_ref, rf_ref):
    t = h2b.shape[0]
    lg = lax.dot_general(wr_ref[...], h2b, (((1,), (1,)), ((), ())), preferred_element_type=jnp.float32)
    lg = lg + br_ref[...][:, 0:1]
    r8 = lax.broadcasted_iota(jnp.int32, (8, t), 0)
    lgrp = jnp.where(r8 < N_GROUPS, lg[0:8], -jnp.inf)
    gmax = jnp.max(lgrp, axis=0, keepdims=True)
    grp = jnp.min(jnp.where(lgrp == gmax, r8, 8), axis=0, keepdims=True)
    pg_sel = 1.0 / jnp.sum(jnp.exp(lgrp - gmax), axis=0, keepdims=True)
    r32 = lax.broadcasted_iota(jnp.int32, (N_EXPERTS, t), 0)
    le = jnp.where((r32 // EXPERTS_PER_GROUP) == grp, lg[8:8 + N_EXPERTS], -jnp.inf)
    v1 = jnp.max(le, axis=0, keepdims=True)
    i1 = jnp.min(jnp.where(le == v1, r32, N_EXPERTS), axis=0, keepdims=True)
    le2 = jnp.where(r32 == i1, -jnp.inf, le)
    v2 = jnp.max(le2, axis=0, keepdims=True)
    i2 = jnp.min(jnp.where(le2 == v2, r32, N_EXPERTS), axis=0, keepdims=True)
    e21 = jnp.exp(v2 - v1)
    gate1 = pg_sel / (1.0 + e21)
    gate2 = pg_sel * e21 / (1.0 + e21)
    hot1 = r32 == i1
    hot2 = r32 == i2
    onehot = jnp.where(jnp.logical_or(hot1, hot2), 1.0, 0.0)
    before = jnp.dot(onehot.astype(jnp.bfloat16), tri_ref[...], preferred_element_type=jnp.float32)
    pos = before + cnt_ref[...][:, 0:1]
    rank1 = jnp.sum(jnp.where(hot1, pos, 0.0), axis=0, keepdims=True)
    rank2 = jnp.sum(jnp.where(hot2, pos, 0.0), axis=0, keepdims=True)
    cnt_ref[...] = cnt_ref[...] + jnp.sum(onehot, axis=1, keepdims=True)
    zi = jnp.zeros((4, t), jnp.int32)
    ri_ref[...] = jnp.concatenate([i1, i2, rank1.astype(jnp.int32), rank2.astype(jnp.int32), zi], axis=0)
    rf_ref[...] = jnp.concatenate([gate1, gate2, jnp.zeros((6, t), jnp.float32)], axis=0)


def _split_proj(proj):
    a = ATTN_DIM
    q = proj[:, :a]
    k = proj[:, a:a + KV_DIM]
    v = proj[:, a + KV_DIM:a + 2 * KV_DIM]
    c0 = a + 2 * KV_DIM
    h_conv = proj[:, c0:c0 + CONV_DIM]
    gate_b = proj[:, c0 + CONV_DIM:c0 + 2 * CONV_DIM]
    gate_c = proj[:, c0 + 2 * CONV_DIM:c0 + 3 * CONV_DIM]
    return q, k, v, gate_c * h_conv, gate_b


def _mixer_tail(x, attn_n, conv_n, mod, wout_ref, fnorm_ref, wr_ref, br_ref, tri_ref, cnt_ref,
                x1_ref, h2_ref, ri_ref, rf_ref):
    gate_a, shift_f, scale_f = mod
    cat = jnp.concatenate([attn_n.astype(jnp.bfloat16), conv_n.astype(jnp.bfloat16)], axis=1)
    mix = jnp.dot(cat, wout_ref[...], preferred_element_type=jnp.float32)
    x1 = x + gate_a * mix
    x1_ref[...] = x1
    h2 = _rms(x1, fnorm_ref[...]) * (1.0 + scale_f) + shift_f
    h2_ref[...] = h2
    _route(h2.astype(jnp.bfloat16), wr_ref, br_ref, tri_ref, cnt_ref, ri_ref, rf_ref)


def _adaln_kernel(c_ref, w_ref, b_ref, o_ref):
    c = c_ref[...]
    a = (c * jax.nn.sigmoid(c)).astype(jnp.bfloat16)
    o_ref[...] = jnp.dot(a, w_ref[...].astype(jnp.bfloat16), preferred_element_type=jnp.float32) + b_ref[...]


def _adaln(c, w_mod, b_mod):
    n = c.shape[0]
    tn = 1536
    return pl.pallas_call(
        _adaln_kernel,
        grid=(w_mod.shape[1] // tn,),
        in_specs=[pl.BlockSpec((n, D_MODEL), lambda j: (0, 0)),
                  pl.BlockSpec((D_MODEL, tn), lambda j: (0, j)),
                  pl.BlockSpec((1, tn), lambda j: (0, j))],
        out_specs=pl.BlockSpec((n, tn), lambda j: (0, j)),
        out_shape=jax.ShapeDtypeStruct((n, w_mod.shape[1]), jnp.float32),
        compiler_params=pltpu.CompilerParams(dimension_semantics=("arbitrary",), vmem_limit_bytes=VMEM_LIMIT),
        name="adaln",
    )(c, w_mod, b_mod.reshape(1, -1))


def _prompt_mixer_kernel(sinks_ref, x_ref, mod_ref, anorm_ref, fnorm_ref, win_ref, cos_ref, sin_ref, bias_ref,
                         convw_ref, gna_ref, gnc_ref, wout_ref, wr_ref, br_ref, tri_ref,
                         x1_ref, h2_ref, ri_ref, rf_ref, cnt_out_ref, knew_ref, vnew_ref, unew_ref,
                         q_s, k_s, ksw_s, v_s, vsw_s, u_s, att_s, cnt_s):
    b = pl.program_id(0)
    t = pl.program_id(1)
    tm = MIXER_TILE
    nblk = tm // WINDOW

    @pl.when(jnp.logical_and(b == 0, t == 0))
    def _():
        cnt_s[...] = jnp.zeros_like(cnt_s)

    @pl.when(t == 0)
    def _():
        z = jnp.zeros((WINDOW, KV_DIM), jnp.bfloat16)
        k_s[0:WINDOW, :] = z
        ksw_s[0:WINDOW, :] = z
        v_s[0:WINDOW, :] = z
        vsw_s[0:WINDOW, :] = z
        u_s[0:8, :] = jnp.zeros((8, CONV_DIM), jnp.float32)

    x = x_ref[...]
    mod = mod_ref[...]
    h = _rms(x, anorm_ref[...]) * (1.0 + mod[1:2]) + mod[0:1]
    proj = jnp.dot(h.astype(jnp.bfloat16), win_ref[...], preferred_element_type=jnp.float32)
    q, k, v, u, gate_b = _split_proj(proj)
    cos = cos_ref[...]
    sin = sin_ref[...]
    q = _rope(q, cos, sin) * (HEAD_DIM ** -0.5)
    k = _rope(k, cos, sin)
    q_s[...] = q.astype(jnp.bfloat16)
    k_s[WINDOW:, :] = k.astype(jnp.bfloat16)
    ksw_s[WINDOW:, :] = _swap_halves(k).astype(jnp.bfloat16)
    v_s[WINDOW:, :] = v.astype(jnp.bfloat16)
    vsw_s[WINDOW:, :] = _swap_halves(v).astype(jnp.bfloat16)
    knew_ref[...] = k[tm - WINDOW:, :]
    vnew_ref[...] = v[tm - WINDOW:, :]

    u_s[8:, :] = u
    unew_ref[...] = u[tm - 8:, :]
    cw = convw_ref[...]
    conv = (u_s[6:6 + tm, :] * cw[0:1] + u_s[7:7 + tm, :] * cw[1:2] + u * cw[2:3]) * gate_b
    u_s[0:8, :] = u[tm - 8:, :]
    conv_n = _group_norm64(conv, gnc_ref[...])

    lo = _half_lane_mask(WINDOW)
    first = t == 0

    def block(i, carry):
        r0 = pl.multiple_of(i * WINDOW, WINDOW)
        bias = bias_ref[jnp.where(jnp.logical_and(first, i == 0), 0, 1)]
        qb = q_s[pl.ds(r0, WINDOW), :]
        keys = k_s[pl.ds(r0, 2 * WINDOW), :]
        keys_sw = ksw_s[pl.ds(r0, 2 * WINDOW), :]
        vals = v_s[pl.ds(r0, 2 * WINDOW), :]
        vals_sw = vsw_s[pl.ds(r0, 2 * WINDOW), :]
        outs = []
        for g in range(N_KV_HEADS):
            outs += _attend(qb, keys, keys_sw, vals, vals_sw, bias, sinks_ref, g)
        att_s[pl.ds(r0, WINDOW), :] = _merge_heads(outs, lo)
        return carry

    lax.fori_loop(0, nblk, block, 0, unroll=True)
    k_s[0:WINDOW, :] = k_s[tm:tm + WINDOW, :]
    ksw_s[0:WINDOW, :] = ksw_s[tm:tm + WINDOW, :]
    v_s[0:WINDOW, :] = v_s[tm:tm + WINDOW, :]
    vsw_s[0:WINDOW, :] = vsw_s[tm:tm + WINDOW, :]

    attn_n = att_s[...] * gna_ref[...]
    _mixer_tail(x, attn_n, conv_n, (mod[2:3], mod[3:4], mod[4:5]), wout_ref, fnorm_ref, wr_ref, br_ref, tri_ref,
                cnt_s, x1_ref, h2_ref, ri_ref, rf_ref)
    cnt_out_ref[...] = cnt_s[...]


def _prompt_mixer(x, mod, sinks, consts):
    bsz, seq, _ = x.shape
    tm = MIXER_TILE
    nt = seq // tm
    full = lambda shape: pl.BlockSpec(shape, lambda b, t, s: (0,) * len(shape))
    in_specs = [
        pl.BlockSpec((None, tm, D_MODEL), lambda b, t, s: (b, t, 0)),
        pl.BlockSpec((None, 6, D_MODEL), lambda b, t, s: (b, 0, 0)),
        full((1, D_MODEL)), full((1, D_MODEL)),
        full((D_MODEL, IN_PROJ_DIM)),
        pl.BlockSpec((tm, LANES), lambda b, t, s: (t, 0)),
        pl.BlockSpec((tm, LANES), lambda b, t, s: (t, 0)),
        full((2, WINDOW, 2 * WINDOW)),
        full((3, CONV_DIM)), full((1, ATTN_DIM)), full((1, CONV_DIM)),
        full((D_MODEL, D_MODEL)),
        full((ROUTER_ROWS, D_MODEL)), full((ROUTER_ROWS, LANES)),
        full((tm, tm)),
    ]
    out_shape = [
        jax.ShapeDtypeStruct((bsz * seq, D_MODEL), jnp.float32),
        jax.ShapeDtypeStruct((bsz * seq, D_MODEL), jnp.float32),
        jax.ShapeDtypeStruct((bsz * nt, 8, tm), jnp.int32),
        jax.ShapeDtypeStruct((bsz * nt, 8, tm), jnp.float32),
        jax.ShapeDtypeStruct((N_EXPERTS, LANES), jnp.float32),
        jax.ShapeDtypeStruct((bsz, WINDOW, KV_DIM), jnp.float32),
        jax.ShapeDtypeStruct((bsz, WINDOW, KV_DIM), jnp.float32),
        jax.ShapeDtypeStruct((bsz, 8, CONV_DIM), jnp.float32),
    ]
    out_specs = [
        pl.BlockSpec((tm, D_MODEL), lambda b, t, s: (b * nt + t, 0)),
        pl.BlockSpec((tm, D_MODEL), lambda b, t, s: (b * nt + t, 0)),
        pl.BlockSpec((None, 8, tm), lambda b, t, s: (b * nt + t, 0, 0)),
        pl.BlockSpec((None, 8, tm), lambda b, t, s: (b * nt + t, 0, 0)),
        pl.BlockSpec((N_EXPERTS, LANES), lambda b, t, s: (0, 0)),
        pl.BlockSpec((None, WINDOW, KV_DIM), lambda b, t, s: (b, 0, 0)),
        pl.BlockSpec((None, WINDOW, KV_DIM), lambda b, t, s: (b, 0, 0)),
        pl.BlockSpec((None, 8, CONV_DIM), lambda b, t, s: (b, 0, 0)),
    ]
    scratch = [
        pltpu.VMEM((tm, ATTN_DIM), jnp.bfloat16),
        pltpu.VMEM((tm + WINDOW, KV_DIM), jnp.bfloat16), pltpu.VMEM((tm + WINDOW, KV_DIM), jnp.bfloat16),
        pltpu.VMEM((tm + WINDOW, KV_DIM), jnp.bfloat16), pltpu.VMEM((tm + WINDOW, KV_DIM), jnp.bfloat16),
        pltpu.VMEM((tm + 8, CONV_DIM), jnp.float32),
        pltpu.VMEM((tm, ATTN_DIM), jnp.float32),
        pltpu.VMEM((N_EXPERTS, LANES), jnp.float32),
    ]
    return pl.pallas_call(
        _prompt_mixer_kernel,
        grid_spec=pltpu.PrefetchScalarGridSpec(
            num_scalar_prefetch=1, grid=(bsz, nt), in_specs=in_specs, out_specs=out_specs, scratch_shapes=scratch),
        out_shape=out_shape,
        compiler_params=pltpu.CompilerParams(dimension_semantics=("arbitrary", "arbitrary"),
                                             vmem_limit_bytes=VMEM_LIMIT),
        name="prompt_mixer",
    )(sinks, x, mod, consts["anorm"], consts["fnorm"], consts["w_in"], consts["cos_p"], consts["sin_p"],
      consts["bias_p"], consts["conv_w"], consts["gn_attn"], consts["gn_conv"], consts["w_out"],
      consts["w_r"], consts["b_r"], consts["tri"])


def _sample_mixer_kernel(sinks_ref, x_ref, mod_ref, anorm_ref, fnorm_ref, win_ref, cos_ref, sin_ref, bias_ref,
                         convw_ref, gna_ref, gnc_ref, wout_ref, wr_ref, br_ref, tri_ref,
                         ck_ref, cv_ref, e1_ref, e2_ref, cnt_in_ref,
                         x1_ref, h2_ref, ri_ref, rf_ref, cnt_out_ref, knew_ref, vnew_ref, unew_ref,
                         q_s, k_s, v_s, att_s, cnt_s, *, dec):
    n = x_ref.shape[0]
    cb = SAMPLE_CHUNK
    rows = cb * dec
    pad_keys = SAMPLE_KEYS - cb * WINDOW - rows

    @pl.when(pl.program_id(0) == 0)
    def _():
        cnt_s[...] = cnt_in_ref[...]

    x = x_ref[...]
    mod = mod_ref[...]
    md = lambda j: mod[:, j * D_MODEL:(j + 1) * D_MODEL]
    h = _rms(x, anorm_ref[...]) * (1.0 + md(1)) + md(0)
    proj = jnp.dot(h.astype(jnp.bfloat16), win_ref[...], preferred_element_type=jnp.float32)
    q, k, v, u, gate_b = _split_proj(proj)
    cos = cos_ref[...]
    sin = sin_ref[...]
    q = _rope(q, cos, sin) * (HEAD_DIM ** -0.5)
    k = _rope(k, cos, sin)
    q_s[...] = q.astype(jnp.bfloat16)
    k_s[...] = k
    v_s[...] = v
    knew_ref[...] = k
    vnew_ref[...] = v
    unew_ref[...] = u

    pos = lax.broadcasted_iota(jnp.int32, (n, CONV_DIM), 0) % dec
    um1 = jnp.where(pos >= 1, pltpu.roll(u, 1, 0), 0.0) + e1_ref[...]
    um2 = jnp.where(pos >= 2, pltpu.roll(u, 2, 0), 0.0) + e2_ref[...]
    cw = convw_ref[...]
    conv = (um2 * cw[0:1] + um1 * cw[1:2] + u * cw[2:3]) * gate_b
    conv_n = _group_norm64(conv, gnc_ref[...])

    lo = _half_lane_mask(rows)
    bias = bias_ref[...]

    def chunk(c, carry):
        r0 = pl.multiple_of(c * rows, rows)
        c0 = pl.multiple_of(c * (cb * WINDOW), cb * WINDOW)
        zpad = jnp.zeros((pad_keys, KV_DIM), jnp.float32)
        kc = jnp.concatenate([ck_ref[pl.ds(c0, cb * WINDOW), :], k_s[pl.ds(r0, rows), :], zpad], axis=0)
        vc = jnp.concatenate([cv_ref[pl.ds(c0, cb * WINDOW), :], v_s[pl.ds(r0, rows), :], zpad], axis=0)
        keys, keys_sw = kc.astype(jnp.bfloat16), _swap_halves(kc).astype(jnp.bfloat16)
        vals, vals_sw = vc.astype(jnp.bfloat16), _swap_halves(vc).astype(jnp.bfloat16)
        qb = q_s[pl.ds(r0, rows), :]
        outs = []
        for g in range(N_KV_HEADS):
            outs += _attend(qb, keys, keys_sw, vals, vals_sw, bias, sinks_ref, g)
        att_s[pl.ds(r0, rows), :] = _merge_heads(outs, lo)
        return carry

    lax.fori_loop(0, n // rows, chunk, 0)
    attn_n = att_s[...] * gna_ref[...]
    _mixer_tail(x, attn_n, conv_n, (md(2), md(3), md(4)), wout_ref, fnorm_ref, wr_ref, br_ref, tri_ref,
                cnt_s, x1_ref, h2_ref, ri_ref, rf_ref)
    cnt_out_ref[...] = cnt_s[...]


def _sample_mixer(x, mod, sinks, consts, ck, cv, e1, e2, cnt_in, dec):
    n_s = x.shape[0]
    ts = SAMPLE_TILE
    seqs = ts // dec
    full = lambda a: pl.BlockSpec(a.shape, lambda i, s: (0,) * a.ndim)
    rows = lambda width: pl.BlockSpec((ts, width), lambda i, s: (i, 0))
    in_specs = [
        rows(D_MODEL), rows(6 * D_MODEL),
        full(consts["anorm"]), full(consts["fnorm"]), full(consts["w_in"]),
        rows(LANES), rows(LANES), full(consts["bias_s"]),
        full(consts["conv_w"]), full(consts["gn_attn"]), full(consts["gn_conv"]), full(consts["w_out"]),
        full(consts["w_r"]), full(consts["b_r"]),
        pl.BlockSpec((ts, ts), lambda i, s: (0, 0)),
        pl.BlockSpec((seqs * WINDOW, KV_DIM), lambda i, s: (i, 0)),
        pl.BlockSpec((seqs * WINDOW, KV_DIM), lambda i, s: (i, 0)),
        rows(CONV_DIM), rows(CONV_DIM), full(cnt_in),
    ]
    args = [x, mod, consts["anorm"], consts["fnorm"], consts["w_in"], consts["cos_s"], consts["sin_s"],
            consts["bias_s"], consts["conv_w"], consts["gn_attn"], consts["gn_conv"], consts["w_out"],
            consts["w_r"], consts["b_r"], consts["tri"], ck, cv, e1, e2, cnt_in]
    nt = n_s // ts
    out_shape = [
        jax.ShapeDtypeStruct((n_s, D_MODEL), jnp.float32),
        jax.ShapeDtypeStruct((n_s, D_MODEL), jnp.float32),
        jax.ShapeDtypeStruct((nt, 8, ts), jnp.int32),
        jax.ShapeDtypeStruct((nt, 8, ts), jnp.float32),
        jax.ShapeDtypeStruct((N_EXPERTS, LANES), jnp.float32),
        jax.ShapeDtypeStruct((n_s, KV_DIM), jnp.float32),
        jax.ShapeDtypeStruct((n_s, KV_DIM), jnp.float32),
        jax.ShapeDtypeStruct((n_s, CONV_DIM), jnp.float32),
    ]
    out_specs = [
        rows(D_MODEL),
        rows(D_MODEL),
        pl.BlockSpec((None, 8, ts), lambda i, s: (i, 0, 0)),
        pl.BlockSpec((None, 8, ts), lambda i, s: (i, 0, 0)),
        pl.BlockSpec((N_EXPERTS, LANES), lambda i, s: (0, 0)),
        rows(KV_DIM), rows(KV_DIM), rows(CONV_DIM),
    ]
    scratch = [
        pltpu.VMEM((ts, ATTN_DIM), jnp.bfloat16),
        pltpu.VMEM((ts, KV_DIM), jnp.float32), pltpu.VMEM((ts, KV_DIM), jnp.float32),
        pltpu.VMEM((ts, ATTN_DIM), jnp.float32),
        pltpu.VMEM((N_EXPERTS, LANES), jnp.float32),
    ]
    return pl.pallas_call(
        functools.partial(_sample_mixer_kernel, dec=dec),
        grid_spec=pltpu.PrefetchScalarGridSpec(
            num_scalar_prefetch=1, grid=(nt,), in_specs=in_specs, out_specs=out_specs, scratch_shapes=scratch),
        out_shape=out_shape,
        compiler_params=pltpu.CompilerParams(dimension_semantics=("arbitrary",), vmem_limit_bytes=VMEM_LIMIT),
        name="sample_mixer",
    )(sinks, *args)


def _dispatch(idx_p, idx_s, h2_p, h2_s, n_rows):
    mesh = plsc.VectorSubcoreMesh(core_axis_name="core", subcore_axis_name="subcore")
    rows, window = SC_ROWS, SC_INDEX_WINDOW
    sub = window // rows

    @pl.kernel(out_type=jax.ShapeDtypeStruct((n_rows, D_MODEL), jnp.float32), mesh=mesh, scratch_types=[])
    def scatter(hp_hbm, hs_hbm, ip_hbm, is_hbm, xs_hbm):
        def run(h_hbm, i_hbm):
            n_tok = h_hbm.shape[0]

            def body(indices, h_vmem, i_vmem):
                pltpu.sync_copy(h_vmem, xs_hbm.at[i_vmem.at[0, pl.ds(indices[1] * rows, rows)]])

            pltpu.emit_pipeline(
                body, grid=(2 * n_tok // window, sub),
                in_specs=[pl.BlockSpec((rows, D_MODEL), lambda i, j: ((i * sub + j) % (n_tok // rows), 0)),
                          pl.BlockSpec((1, window), lambda i, j: (0, i))],
                out_specs=[], core_axis_name=("core", "subcore"),
                dimension_semantics=(pltpu.PARALLEL, pltpu.ARBITRARY), _explicit_indices=True,
            )(h_hbm, i_hbm)

        run(hp_hbm, ip_hbm)
        run(hs_hbm, is_hbm)

    return scatter(h2_p, h2_s, idx_p, idx_s)


def _experts_kernel(be_ref, nu_ref, nv_ref, xs_ref, wg_ref, wu_ref, wd_ref, y_ref, wg_s, wu_s, wd_s):
    j = pl.program_id(0)
    used = j < nu_ref[0]
    changed = jnp.logical_or(j == 0, be_ref[j] != be_ref[jnp.maximum(j - 1, 0)])

    @pl.when(jnp.logical_and(used, changed))
    def _():
        wg_s[...] = wg_ref[...].astype(jnp.bfloat16)
        wu_s[...] = wu_ref[...].astype(jnp.bfloat16)
        wd_s[...] = wd_ref[...].astype(jnp.bfloat16)

    @pl.when(used)
    def _():
        live = lax.broadcasted_iota(jnp.int32, (xs_ref.shape[0], 1), 0) < nv_ref[j]
        x = jnp.where(live, xs_ref[...], 0.0).astype(jnp.bfloat16)
        g = jnp.dot(x, wg_s[...], preferred_element_type=jnp.float32)
        u = jnp.dot(x, wu_s[...], preferred_element_type=jnp.float32)
        a = (g * jax.nn.sigmoid(g) * u).astype(jnp.bfloat16)
        y_ref[...] = jnp.dot(a, wd_s[...], preferred_element_type=jnp.float32).reshape(y_ref.shape)

    @pl.when(jnp.logical_not(used))
    def _():
        y_ref[...] = jnp.zeros_like(y_ref)


def _experts(block_e, n_used, n_valid, xs, w_gate, w_up, w_down):
    bm = EXPERT_BLOCK
    n_blocks = xs.shape[0] // bm
    row = lambda j, be, nu, nv: (jnp.minimum(j, nu[0] - 1), 0)
    wmap = lambda j, be, nu, nv: (be[j], 0, 0)
    return pl.pallas_call(
        _experts_kernel,
        grid_spec=pltpu.PrefetchScalarGridSpec(
            num_scalar_prefetch=3, grid=(n_blocks,),
            in_specs=[pl.BlockSpec((bm, D_MODEL), row),
                      pl.BlockSpec((None, D_MODEL, EXPERT_HIDDEN), wmap),
                      pl.BlockSpec((None, D_MODEL, EXPERT_HIDDEN), wmap),
                      pl.BlockSpec((None, EXPERT_HIDDEN, D_MODEL), wmap)],
            out_specs=pl.BlockSpec((bm, 1, D_MODEL), lambda j, be, nu, nv: (j, 0, 0)),
            scratch_shapes=[pltpu.VMEM((D_MODEL, EXPERT_HIDDEN), jnp.bfloat16),
                            pltpu.VMEM((D_MODEL, EXPERT_HIDDEN), jnp.bfloat16),
                            pltpu.VMEM((EXPERT_HIDDEN, D_MODEL), jnp.bfloat16)]),
        out_shape=jax.ShapeDtypeStruct((xs.shape[0], 1, D_MODEL), jnp.float32),
        compiler_params=pltpu.CompilerParams(dimension_semantics=("arbitrary",), vmem_limit_bytes=VMEM_LIMIT),
        name="experts",
    )(block_e, n_used, n_valid, xs, w_gate, w_up, w_down)


def _combine_kernel(dest_ref, x1_ref, gate_ref, rf_ref, fin_ref, yb_ref, o_ref, ybuf, y_s, sem, *,
                    n_tok, per_token_gate):
    i = pl.program_id(0)
    n = pl.num_programs(0)
    tc = COMBINE_TILE

    def issue(step, slot):
        def one(r, carry):
            tok = step * tc + r
            for k in range(2):
                d = dest_ref[k * n_tok + tok]
                pltpu.make_async_copy(yb_ref.at[pl.ds(d, 1)], ybuf.at[slot, k, pl.ds(r, 1)], sem.at[slot]).start()
            return carry
        lax.fori_loop(0, tc, one, 0, unroll=8)

    slot = i % 2

    @pl.when(i == 0)
    def _():
        issue(0, 0)

    @pl.when(i + 1 < n)
    def _():
        issue(i + 1, 1 - slot)

    pltpu.make_async_copy(yb_ref.at[pl.ds(0, tc)], ybuf.at[slot, 0], sem.at[slot]).wait()
    pltpu.make_async_copy(yb_ref.at[pl.ds(0, tc)], ybuf.at[slot, 1], sem.at[slot]).wait()

    gates = jnp.transpose(rf_ref[...])
    y_s[0] = ybuf[slot, 0].reshape(tc, D_MODEL)
    y_s[1] = ybuf[slot, 1].reshape(tc, D_MODEL)
    y0 = y_s[0]
    y1 = y_s[1]
    moe = y0 * gates[:, 0:1] + y1 * gates[:, 1:2]
    gate_f = gate_ref[...] if per_token_gate else gate_ref[...][5:6]
    o_ref[...] = _rms(x1_ref[...] + gate_f * moe, fin_ref[...])


def _combine(dest_flat, x1, gate, rf, final_norm, yb, *, tiles_per_gate):
    n_tok = x1.shape[0]
    tc = COMBINE_TILE
    per_token = tiles_per_gate == 0
    if per_token:
        gate_spec = pl.BlockSpec((tc, D_MODEL), lambda i, d: (i, 0))
    else:
        gate_spec = pl.BlockSpec((None, 6, D_MODEL), lambda i, d: (i // tiles_per_gate, 0, 0))
    return pl.pallas_call(
        functools.partial(_combine_kernel, n_tok=n_tok, per_token_gate=per_token),
        grid_spec=pltpu.PrefetchScalarGridSpec(
            num_scalar_prefetch=1, grid=(n_tok // tc,),
            in_specs=[pl.BlockSpec((tc, D_MODEL), lambda i, d: (i, 0)),
                      gate_spec,
                      pl.BlockSpec((8, tc), lambda i, d: (0, i)),
                      pl.BlockSpec((1, D_MODEL), lambda i, d: (0, 0)),
                      pl.BlockSpec(memory_space=pl.ANY)],
            out_specs=pl.BlockSpec((tc, D_MODEL), lambda i, d: (i, 0)),
            scratch_shapes=[pltpu.VMEM((2, 2, tc, 1, D_MODEL), jnp.float32),
                            pltpu.VMEM((2, tc, D_MODEL), jnp.float32), pltpu.SemaphoreType.DMA((2,))]),
        out_shape=jax.ShapeDtypeStruct((n_tok, D_MODEL), jnp.float32),
        compiler_params=pltpu.CompilerParams(dimension_semantics=("arbitrary",), vmem_limit_bytes=VMEM_LIMIT),
        name="combine",
    )(dest_flat, x1, gate, rf, final_norm, yb)


def _rope_tables(pos):
    half = HEAD_DIM // 2
    inv_freq = np.power(np.float32(ROPE_THETA), -(np.arange(half, dtype=np.float32) / np.float32(half)))
    ang = (pos.astype(np.float32)[:, None] * inv_freq[None, :].astype(np.float32)).astype(np.float64)
    cos, sin = np.cos(ang).astype(np.float32), np.sin(ang).astype(np.float32)
    cos = np.concatenate([cos, cos, cos, cos], axis=1)
    sin = np.concatenate([-sin, sin, -sin, sin], axis=1)
    return cos, sin


def _prompt_bias():
    r = np.arange(WINDOW)[:, None]
    j = np.arange(2 * WINDOW)[None, :]
    band = (j > r) & (j <= r + WINDOW)
    later = np.where(band, 0.0, NEG).astype(np.float32)
    first = np.where(band & (j >= WINDOW), 0.0, NEG).astype(np.float32)
    return jnp.asarray(np.stack([first, later]))


def _sample_bias(dec):
    cb = SAMPLE_CHUNK
    qb = np.repeat(np.arange(cb), dec)[:, None]
    qi = np.tile(np.arange(dec), cb)[:, None]
    kb = np.concatenate([np.repeat(np.arange(cb), WINDOW), np.repeat(np.arange(cb), dec)])[None, :]
    kj = np.concatenate([np.tile(np.arange(WINDOW), cb), np.tile(np.arange(dec), cb)])[None, :]
    is_new = (np.arange(cb * WINDOW + cb * dec) >= cb * WINDOW)[None, :]
    ok = (kb == qb) & np.where(is_new, kj <= qi, kj > qi)
    ok = np.pad(ok, ((0, 0), (0, SAMPLE_KEYS - ok.shape[1])))
    return jnp.asarray(np.where(ok, 0.0, NEG).astype(np.float32))


def kernel(x_prompt, x_sample, cache_k, cache_v, state_conv, c_prompt, c_sample, attn_norm, ffn_norm, w_mod, b_mod, w_in, conv_w, attn_sinks, out_norm_attn, out_norm_conv, w_out, w_group, b_group, w_expert, b_expert, w_gate, w_up, w_down, final_norm):
    assert attn_norm.shape[0] == 1, "one layer"
    bsz, seq, _ = x_prompt.shape
    nseq, dec, _ = x_sample.shape
    m_prompt = bsz * seq
    n_s = nseq * dec
    m_total = m_prompt + n_s
    tm = MIXER_TILE
    assert seq % tm == 0 and n_s % SAMPLE_TILE == 0 and SAMPLE_TILE % (SAMPLE_CHUNK * dec) == 0
    assert cache_k.shape[2] == WINDOW and m_prompt % SAMPLE_TILE == 0 and m_total % COMBINE_TILE == 0
    bf = jnp.bfloat16

    w_r = jnp.zeros((ROUTER_ROWS, D_MODEL), jnp.float32)
    w_r = w_r.at[0:N_GROUPS].set(w_group[0].T).at[8:8 + N_EXPERTS].set(w_expert[0].T).astype(bf)
    b_r = jnp.zeros((ROUTER_ROWS,), jnp.float32).at[0:N_GROUPS].set(b_group[0]).at[8:8 + N_EXPERTS].set(b_expert[0])
    b_r = jnp.broadcast_to(b_r[:, None], (ROUTER_ROWS, LANES))
    cos_p, sin_p = _rope_tables(np.arange(seq))
    cos_s, sin_s = _rope_tables(PAST_LEN + np.arange(dec))
    tri = jnp.asarray(np.triu(np.ones((tm, tm), np.float32), 1)).astype(bf)
    consts = dict(
        anorm=attn_norm[0][None], fnorm=ffn_norm[0][None], w_in=w_in[0].astype(bf),
        cos_p=jnp.asarray(cos_p), sin_p=jnp.asarray(sin_p),
        cos_s=jnp.asarray(np.tile(cos_s, (nseq, 1))), sin_s=jnp.asarray(np.tile(sin_s, (nseq, 1))),
        bias_p=_prompt_bias(), bias_s=_sample_bias(dec), conv_w=conv_w[0],
        gn_attn=out_norm_attn[0][None], gn_conv=out_norm_conv[0][None], w_out=w_out[0].astype(bf),
        w_r=w_r, b_r=b_r, tri=tri)
    sinks = attn_sinks[0]

    mod = _adaln(jnp.concatenate([c_prompt, c_sample], axis=0), w_mod[0], b_mod[0])
    mod_p = mod[:bsz].reshape(bsz, 6, D_MODEL)
    mod_s = jnp.repeat(mod[bsz:], dec, axis=0)

    x1_p, h2_p, ri_p, rf_p, cnt_p, k_p, v_p, u_p = _prompt_mixer(x_prompt, mod_p, sinks, consts)
    st = state_conv[0]
    zrow = jnp.zeros((nseq, 1, CONV_DIM), jnp.float32)
    e1 = jnp.concatenate([st[:, 1:2]] + [zrow] * (dec - 1), axis=1).reshape(n_s, CONV_DIM)
    e2 = jnp.concatenate([st[:, 0:1], st[:, 1:2]] + [zrow] * (dec - 2), axis=1).reshape(n_s, CONV_DIM)
    ck = cache_k[0].reshape(nseq * WINDOW, KV_DIM)
    cv = cache_v[0].reshape(nseq * WINDOW, KV_DIM)
    x1_s, h2_s, ri_s, rf_s, cnt, k_s, v_s, u_s = _sample_mixer(
        x_sample.reshape(n_s, D_MODEL), mod_s, sinks, consts, ck, cv, e1, e2, cnt_p, dec)

    bm = EXPERT_BLOCK
    counts = cnt[:, 0].astype(jnp.int32)
    padded = (counts + bm - 1) // bm * bm
    pend = jnp.cumsum(padded)
    pstart = pend - padded
    n_blocks = -(-(2 * m_total + N_EXPERTS * (bm - 1)) // bm)
    n_used = (pend[-1] // bm).astype(jnp.int32)
    blk = jnp.arange(n_blocks, dtype=jnp.int32)
    eids = jnp.arange(N_EXPERTS, dtype=jnp.int32)
    block_e = jnp.sum((pend[None, :] <= (blk * bm)[:, None]).astype(jnp.int32), axis=1)
    last_e = jnp.max(jnp.where(padded > 0, eids, 0))
    block_e = jnp.where(blk < n_used, jnp.minimum(block_e, N_EXPERTS - 1), last_e)
    by_token = lambda a: jnp.transpose(a, (1, 0, 2)).reshape(8, -1)
    ri = jnp.concatenate([by_token(ri_p), by_token(ri_s)], axis=1)
    rf_flat = jnp.concatenate([by_token(rf_p), by_token(rf_s)], axis=1)
    start_of = jnp.sum(jnp.where(ri[0:2][..., None] == eids, pstart, 0), axis=-1)
    dest_flat = (start_of + ri[2:4]).reshape(2 * m_total)

    n_valid = jnp.clip(counts[block_e] - (blk * bm - pstart[block_e]), 0, bm)
    n_valid = jnp.where(blk < n_used, n_valid, 0).astype(jnp.int32)

    dest_p = jnp.concatenate([dest_flat[:m_prompt], dest_flat[m_total:m_total + m_prompt]])
    dest_s = jnp.concatenate([dest_flat[m_prompt:m_total], dest_flat[m_total + m_prompt:]])
    xs = _dispatch(dest_p[None], dest_s[None], h2_p, h2_s, n_blocks * bm)
    yb = _experts(block_e, n_used.reshape(1), n_valid, xs, w_gate[0], w_up[0], w_down[0])

    fin = final_norm[None]
    y_p = _combine(dest_p, x1_p, mod_p, rf_flat[:, :m_prompt], fin, yb, tiles_per_gate=seq // COMBINE_TILE)
    gate_s = mod_s[:, 5 * D_MODEL:]
    y_s = _combine(dest_s, x1_s, gate_s, rf_flat[:, m_prompt:], fin, yb, tiles_per_gate=0)

    keep = WINDOW
    new_k_p = k_p.reshape(1, bsz, keep, N_KV_HEADS, HEAD_DIM)
    new_v_p = v_p.reshape(1, bsz, keep, N_KV_HEADS, HEAD_DIM)
    new_u_p = u_p[:, 6:8][None]
    ksn = k_s.reshape(nseq, dec, N_KV_HEADS, HEAD_DIM)
    vsn = v_s.reshape(nseq, dec, N_KV_HEADS, HEAD_DIM)
    new_k_s = jnp.concatenate([cache_k[0][:, dec:], ksn], axis=1)[None]
    new_v_s = jnp.concatenate([cache_v[0][:, dec:], vsn], axis=1)[None]
    u_all = jnp.concatenate([st, u_s.reshape(nseq, dec, CONV_DIM)], axis=1)
    new_u_s = u_all[:, -2:][None]
    return (y_p.reshape(bsz, seq, D_MODEL), y_s.reshape(nseq, dec, D_MODEL),
            new_k_p, new_v_p, new_u_p, new_k_s, new_v_s, new_u_s)
```

```python
import functools

import jax
import jax.numpy as jnp
import numpy as np
from jax import lax
from jax.experimental import pallas as pl
from jax.experimental.pallas import tpu as pltpu
from jax.experimental.pallas import tpu_sc as plsc

D_MODEL = 1024
HEAD_DIM = 64
ATTN_DIM = 512
N_Q_HEADS = 8
N_KV_HEADS = 2
KV_DIM = 128
CONV_DIM = 512
WINDOW = 128
IN_PROJ_DIM = ATTN_DIM + 2 * KV_DIM + 3 * CONV_DIM
N_GROUPS = 4
EXPERTS_PER_GROUP = 8
N_EXPERTS = 32
EXPERT_HIDDEN = 512
RMS_EPS = 1e-5
ROPE_THETA = 10000.0
PAST_LEN = 16384

LANES = 128
ROUTER_ROWS = 48
NEG = -1e30

MIXER_TILE = 512
SAMPLE_TILE = 128
SAMPLE_CHUNK = 8
SAMPLE_KEYS = 1152
EXPERT_BLOCK = 512
COMBINE_TILE = 256
SC_ROWS = 64
SC_INDEX_WINDOW = 128
VMEM_LIMIT = 56 * 1024 * 1024


def _rms(x, gain):
    ms = jnp.mean(x * x, axis=-1, keepdims=True)
    return x * lax.rsqrt(ms + RMS_EPS) * gain


def _half_lane_mask(rows=1):
    return lax.broadcasted_iota(jnp.int32, (rows, LANES), 1) < HEAD_DIM


def _rope(x, cos, sin_signed):
    t, w = x.shape
    reps = w // LANES
    lane = lax.broadcasted_iota(jnp.int32, (t, w), 1)
    upper = (lane % HEAD_DIM) >= (HEAD_DIM // 2)
    partner = jnp.where(upper, pltpu.roll(x, HEAD_DIM // 2, 1), pltpu.roll(x, w - HEAD_DIM // 2, 1))
    if reps > 1:
        cos = jnp.concatenate([cos] * reps, axis=1)
        sin_signed = jnp.concatenate([sin_signed] * reps, axis=1)
    return x * cos + partner * sin_signed


def _swap_halves(x):
    return pltpu.roll(x, HEAD_DIM, 1)


def _group_norm64(x, gain):
    t, w = x.shape
    lo = _half_lane_mask(t)
    outs = []
    for c in range(w // LANES):
        xc = x[:, c * LANES:(c + 1) * LANES]
        sq = xc * xc
        s_lo = jnp.sum(jnp.where(lo, sq, 0.0), axis=-1, keepdims=True)
        s_hi = jnp.sum(jnp.where(lo, 0.0, sq), axis=-1, keepdims=True)
        r = jnp.where(lo, lax.rsqrt(s_lo * (1.0 / HEAD_DIM) + RMS_EPS), lax.rsqrt(s_hi * (1.0 / HEAD_DIM) + RMS_EPS))
        outs.append(xc * r)
    return jnp.concatenate(outs, axis=1) * gain


def _attend(q_rows, keys, keys_sw, vals, vals_sw, bias, sinks_ref, group):
    r = q_rows.shape[0]
    lo = _half_lane_mask(r)
    outs = []
    for parity in range(2):
        heads = [4 * group + parity, 4 * group + parity + 2]
        kk, vv = (keys, vals) if parity == group else (keys_sw, vals_sw)
        half = lo if parity == 0 else jnp.logical_not(lo)
        qs = []
        for h in heads:
            pair = q_rows[:, (h // 2) * LANES:(h // 2 + 1) * LANES]
            qs.append(jnp.where(half, pair, jnp.zeros_like(pair)))
        qcat = jnp.concatenate(qs, axis=0)
        s = lax.dot_general(qcat, kk, (((1,), (1,)), ((), ())), preferred_element_type=jnp.float32)
        ps, dens = [], []
        for i, h in enumerate(heads):
            sh = s[i * r:(i + 1) * r] + bias
            sink = sinks_ref[h]
            m = jnp.maximum(jnp.max(sh, axis=-1, keepdims=True), sink)
            p = jnp.exp(sh - m)
            dens.append(jnp.sum(p, axis=-1, keepdims=True) + jnp.exp(sink - m))
            ps.append(p.astype(jnp.bfloat16))
        o = jnp.dot(jnp.concatenate(ps, axis=0), vv, preferred_element_type=jnp.float32)
        for i, h in enumerate(heads):
            oh = o[i * r:(i + 1) * r] / dens[i]
            ms = jnp.sum(jnp.where(half, oh * oh, 0.0), axis=-1, keepdims=True) * (1.0 / HEAD_DIM)
            outs.append((h, oh * lax.rsqrt(ms + RMS_EPS)))
    outs.sort(key=lambda t: t[0])
    return [o for _, o in outs]


def _merge_heads(head_outs, lo):
    pairs = [jnp.where(lo, head_outs[2 * j], head_outs[2 * j + 1]) for j in range(4)]
    return jnp.concatenate(pairs, axis=1)


def _route(h2b, wr_ref, br_ref, tri_ref, cnt_ref, ri_ref, rf_ref):
    t = h2b.shape[0]
    lg = lax.dot_general(wr_ref[...], h2b, (((1,), (1,)), ((), ())), preferred_element_type=jnp.float32)
    lg = lg + br_ref[...][:, 0:1]
    r8 = lax.broadcasted_iota(jnp.int32, (8, t), 0)
    lgrp = jnp.where(r8 < N_GROUPS, lg[0:8], -jnp.inf)
    gmax = jnp.max(lgrp, axis=0, keepdims=True)
    grp = jnp.min(jnp.where(lgrp == gmax, r8, 8), axis=0, keepdims=True)
    pg_sel = 1.0 / jnp.sum(jnp.exp(lgrp - gmax), axis=0, keepdims=True)
    r32 = lax.broadcasted_iota(jnp.int32, (N_EXPERTS, t), 0)
    le = jnp.where((r32 // EXPERTS_PER_GROUP) == grp, lg[8:8 + N_EXPERTS], -jnp.inf)
    v1 = jnp.max(le, axis=0, keepdims=True)
    i1 = jnp.min(jnp.where(le == v1, r32, N_EXPERTS), axis=0, keepdims=True)
    le2 = jnp.where(r32 == i1, -jnp.inf, le)
    v2 = jnp.max(le2, axis=0, keepdims=True)
    i2 = jnp.min(jnp.where(le2 == v2, r32, N_EXPERTS), axis=0, keepdims=True)
    e21 = jnp.exp(v2 - v1)
    gate1 = pg_sel / (1.0 + e21)
    gate2 = pg_sel * e21 / (1.0 + e21)
    hot1 = r32 == i1
    hot2 = r32 == i2
    onehot = jnp.where(jnp.logical_or(hot1, hot2), 1.0, 0.0)
    before = jnp.dot(onehot.astype(jnp.bfloat16), tri_ref[...], preferred_element_type=jnp.float32)
    pos = before + cnt_ref[...][:, 0:1]
    rank1 = jnp.sum(jnp.where(hot1, pos, 0.0), axis=0, keepdims=True)
    rank2 = jnp.sum(jnp.where(hot2, pos, 0.0), axis=0, keepdims=True)
    cnt_ref[...] = cnt_ref[...] + jnp.sum(onehot, axis=1, keepdims=True)
    zi = jnp.zeros((4, t), jnp.int32)
    ri_ref[...] = jnp.concatenate([i1, i2, rank1.astype(jnp.int32), rank2.astype(jnp.int32), zi], axis=0)
    rf_ref[...] = jnp.concatenate([gate1, gate2, jnp.zeros((6, t), jnp.float32)], axis=0)


def _pack_bf16_pairs(xb):
    w = xb.shape[1] // 2
    bits = pltpu.bitcast(xb.astype(jnp.float32), jnp.uint32)
    return (bits[:, :w] >> 16) | (bits[:, w:] & jnp.uint32(0xFFFF0000))


def _unpack_bf16_pairs(words):
    lo = pltpu.bitcast(words << 16, jnp.float32)
    hi = pltpu.bitcast(words & jnp.uint32(0xFFFF0000), jnp.float32)
    return jnp.concatenate([lo.astype(jnp.bfloat16), hi.astype(jnp.bfloat16)], axis=1)


def _split_proj(proj):
    a = ATTN_DIM
    q = proj[:, :a]
    k = proj[:, a:a + KV_DIM]
    v = proj[:, a + KV_DIM:a + 2 * KV_DIM]
    c0 = a + 2 * KV_DIM
    h_conv = proj[:, c0:c0 + CONV_DIM]
    gate_b = proj[:, c0 + CONV_DIM:c0 + 2 * CONV_DIM]
    gate_c = proj[:, c0 + 2 * CONV_DIM:c0 + 3 * CONV_DIM]
    return q, k, v, gate_c * h_conv, gate_b


def _mixer_tail(x, attn_n, conv_n, mod, wout_ref, fnorm_ref, wr_ref, br_ref, tri_ref, cnt_ref,
                x1_ref, h2_ref, ri_ref, rf_ref):
    gate_a, shift_f, scale_f = mod
    cat = jnp.concatenate([attn_n.astype(jnp.bfloat16), conv_n.astype(jnp.bfloat16)], axis=1)
    mix = jnp.dot(cat, wout_ref[...], preferred_element_type=jnp.float32)
    x1 = x + gate_a * mix
    x1_ref[...] = x1
    h2 = _rms(x1, fnorm_ref[...]) * (1.0 + scale_f) + shift_f
    h2b = h2.astype(jnp.bfloat16)
    h2_ref[...] = _pack_bf16_pairs(h2b)
    _route(h2b, wr_ref, br_ref, tri_ref, cnt_ref, ri_ref, rf_ref)


def _adaln_kernel(c_ref, w_ref, b_ref, o_ref):
    c = c_ref[...]
    a = (c * jax.nn.sigmoid(c)).astype(jnp.bfloat16)
    o_ref[...] = jnp.dot(a, w_ref[...].astype(jnp.bfloat16), preferred_element_type=jnp.float32) + b_ref[...]


def _adaln(c, w_mod, b_mod):
    n = c.shape[0]
    tn = 1536
    return pl.pallas_call(
        _adaln_kernel,
        grid=(w_mod.shape[1] // tn,),
        in_specs=[pl.BlockSpec((n, D_MODEL), lambda j: (0, 0)),
                  pl.BlockSpec((D_MODEL, tn), lambda j: (0, j)),
                  pl.BlockSpec((1, tn), lambda j: (0, j))],
        out_specs=pl.BlockSpec((n, tn), lambda j: (0, j)),
        out_shape=jax.ShapeDtypeStruct((n, w_mod.shape[1]), jnp.float32),
        compiler_params=pltpu.CompilerParams(dimension_semantics=("arbitrary",), vmem_limit_bytes=VMEM_LIMIT),
        name="adaln",
    )(c, w_mod, b_mod.reshape(1, -1))


def _prompt_mixer_kernel(sinks_ref, x_ref, mod_ref, anorm_ref, fnorm_ref, win_ref, cos_ref, sin_ref, bias_ref,
                         convw_ref, gna_ref, gnc_ref, wout_ref, wr_ref, br_ref, tri_ref,
                         x1_ref, h2_ref, ri_ref, rf_ref, cnt_out_ref, knew_ref, vnew_ref, unew_ref,
                         q_s, k_s, ksw_s, v_s, vsw_s, u_s, att_s, cnt_s):
    b = pl.program_id(0)
    t = pl.program_id(1)
    tm = MIXER_TILE
    nblk = tm // WINDOW

    @pl.when(jnp.logical_and(b == 0, t == 0))
    def _():
        cnt_s[...] = jnp.zeros_like(cnt_s)

    @pl.when(t == 0)
    def _():
        z = jnp.zeros((WINDOW, KV_DIM), jnp.bfloat16)
        k_s[0:WINDOW, :] = z
        ksw_s[0:WINDOW, :] = z
        v_s[0:WINDOW, :] = z
        vsw_s[0:WINDOW, :] = z
        u_s[0:8, :] = jnp.zeros((8, CONV_DIM), jnp.float32)

    x = x_ref[...]
    mod = mod_ref[...]
    h = _rms(x, anorm_ref[...]) * (1.0 + mod[1:2]) + mod[0:1]
    proj = jnp.dot(h.astype(jnp.bfloat16), win_ref[...], preferred_element_type=jnp.float32)
    q, k, v, u, gate_b = _split_proj(proj)
    cos = cos_ref[...]
    sin = sin_ref[...]
    q = _rope(q, cos, sin) * (HEAD_DIM ** -0.5)
    k = _rope(k, cos, sin)
    q_s[...] = q.astype(jnp.bfloat16)
    k_s[WINDOW:, :] = k.astype(jnp.bfloat16)
    ksw_s[WINDOW:, :] = _swap_halves(k).astype(jnp.bfloat16)
    v_s[WINDOW:, :] = v.astype(jnp.bfloat16)
    vsw_s[WINDOW:, :] = _swap_halves(v).astype(jnp.bfloat16)
    knew_ref[...] = k[tm - WINDOW:, :]
    vnew_ref[...] = v[tm - WINDOW:, :]

    u_s[8:, :] = u
    unew_ref[...] = u[tm - 8:, :]
    cw = convw_ref[...]
    conv = (u_s[6:6 + tm, :] * cw[0:1] + u_s[7:7 + tm, :] * cw[1:2] + u * cw[2:3]) * gate_b
    u_s[0:8, :] = u[tm - 8:, :]
    conv_n = _group_norm64(conv, gnc_ref[...])

    lo = _half_lane_mask(WINDOW)
    first = t == 0

    def block(i, carry):
        r0 = pl.multiple_of(i * WINDOW, WINDOW)
        bias = bias_ref[jnp.where(jnp.logical_and(first, i == 0), 0, 1)]
        qb = q_s[pl.ds(r0, WINDOW), :]
        keys = k_s[pl.ds(r0, 2 * WINDOW), :]
        keys_sw = ksw_s[pl.ds(r0, 2 * WINDOW), :]
        vals = v_s[pl.ds(r0, 2 * WINDOW), :]
        vals_sw = vsw_s[pl.ds(r0, 2 * WINDOW), :]
        outs = []
        for g in range(N_KV_HEADS):
            outs += _attend(qb, keys, keys_sw, vals, vals_sw, bias, sinks_ref, g)
        att_s[pl.ds(r0, WINDOW), :] = _merge_heads(outs, lo)
        return carry

    lax.fori_loop(0, nblk, block, 0, unroll=True)
    k_s[0:WINDOW, :] = k_s[tm:tm + WINDOW, :]
    ksw_s[0:WINDOW, :] = ksw_s[tm:tm + WINDOW, :]
    v_s[0:WINDOW, :] = v_s[tm:tm + WINDOW, :]
    vsw_s[0:WINDOW, :] = vsw_s[tm:tm + WINDOW, :]

    attn_n = att_s[...] * gna_ref[...]
    _mixer_tail(x, attn_n, conv_n, (mod[2:3], mod[3:4], mod[4:5]), wout_ref, fnorm_ref, wr_ref, br_ref, tri_ref,
                cnt_s, x1_ref, h2_ref, ri_ref, rf_ref)
    cnt_out_ref[...] = cnt_s[...]


def _prompt_mixer(x, mod, sinks, consts):
    bsz, seq, _ = x.shape
    tm = MIXER_TILE
    nt = seq // tm
    full = lambda shape: pl.BlockSpec(shape, lambda b, t, s: (0,) * len(shape))
    in_specs = [
        pl.BlockSpec((None, tm, D_MODEL), lambda b, t, s: (b, t, 0)),
        pl.BlockSpec((None, 6, D_MODEL), lambda b, t, s: (b, 0, 0)),
        full((1, D_MODEL)), full((1, D_MODEL)),
        full((D_MODEL, IN_PROJ_DIM)),
        pl.BlockSpec((tm, LANES), lambda b, t, s: (t, 0)),
        pl.BlockSpec((tm, LANES), lambda b, t, s: (t, 0)),
        full((2, WINDOW, 2 * WINDOW)),
        full((3, CONV_DIM)), full((1, ATTN_DIM)), full((1, CONV_DIM)),
        full((D_MODEL, D_MODEL)),
        full((ROUTER_ROWS, D_MODEL)), full((ROUTER_ROWS, LANES)),
        full((tm, tm)),
    ]
    out_shape = [
        jax.ShapeDtypeStruct((bsz * seq, D_MODEL), jnp.float32),
        jax.ShapeDtypeStruct((bsz * seq, D_MODEL // 2), jnp.uint32),
        jax.ShapeDtypeStruct((bsz * nt, 8, tm), jnp.int32),
        jax.ShapeDtypeStruct((bsz * nt, 8, tm), jnp.float32),
        jax.ShapeDtypeStruct((N_EXPERTS, LANES), jnp.float32),
        jax.ShapeDtypeStruct((bsz, WINDOW, KV_DIM), jnp.float32),
        jax.ShapeDtypeStruct((bsz, WINDOW, KV_DIM), jnp.float32),
        jax.ShapeDtypeStruct((bsz, 8, CONV_DIM), jnp.float32),
    ]
    out_specs = [
        pl.BlockSpec((tm, D_MODEL), lambda b, t, s: (b * nt + t, 0)),
        pl.BlockSpec((tm, D_MODEL // 2), lambda b, t, s: (b * nt + t, 0)),
        pl.BlockSpec((None, 8, tm), lambda b, t, s: (b * nt + t, 0, 0)),
        pl.BlockSpec((None, 8, tm), lambda b, t, s: (b * nt + t, 0, 0)),
        pl.BlockSpec((N_EXPERTS, LANES), lambda b, t, s: (0, 0)),
        pl.BlockSpec((None, WINDOW, KV_DIM), lambda b, t, s: (b, 0, 0)),
        pl.BlockSpec((None, WINDOW, KV_DIM), lambda b, t, s: (b, 0, 0)),
        pl.BlockSpec((None, 8, CONV_DIM), lambda b, t, s: (b, 0, 0)),
    ]
    scratch = [
        pltpu.VMEM((tm, ATTN_DIM), jnp.bfloat16),
        pltpu.VMEM((tm + WINDOW, KV_DIM), jnp.bfloat16), pltpu.VMEM((tm + WINDOW, KV_DIM), jnp.bfloat16),
        pltpu.VMEM((tm + WINDOW, KV_DIM), jnp.bfloat16), pltpu.VMEM((tm + WINDOW, KV_DIM), jnp.bfloat16),
        pltpu.VMEM((tm + 8, CONV_DIM), jnp.float32),
        pltpu.VMEM((tm, ATTN_DIM), jnp.float32),
        pltpu.VMEM((N_EXPERTS, LANES), jnp.float32),
    ]
    return pl.pallas_call(
        _prompt_mixer_kernel,
        grid_spec=pltpu.PrefetchScalarGridSpec(
            num_scalar_prefetch=1, grid=(bsz, nt), in_specs=in_specs, out_specs=out_specs, scratch_shapes=scratch),
        out_shape=out_shape,
        compiler_params=pltpu.CompilerParams(dimension_semantics=("arbitrary", "arbitrary"),
                                             vmem_limit_bytes=VMEM_LIMIT),
        name="prompt_mixer",
    )(sinks, x, mod, consts["anorm"], consts["fnorm"], consts["w_in"], consts["cos_p"], consts["sin_p"],
      consts["bias_p"], consts["conv_w"], consts["gn_attn"], consts["gn_conv"], consts["w_out"],
      consts["w_r"], consts["b_r"], consts["tri"])


def _sample_mixer_kernel(sinks_ref, x_ref, mod_ref, anorm_ref, fnorm_ref, win_ref, cos_ref, sin_ref, bias_ref,
                         convw_ref, gna_ref, gnc_ref, wout_ref, wr_ref, br_ref, tri_ref,
                         ck_ref, cv_ref, e1_ref, e2_ref, cnt_in_ref,
                         x1_ref, h2_ref, ri_ref, rf_ref, cnt_out_ref, knew_ref, vnew_ref, unew_ref,
                         q_s, k_s, v_s, att_s, cnt_s, *, dec):
    n = x_ref.shape[0]
    cb = SAMPLE_CHUNK
    rows = cb * dec
    pad_keys = SAMPLE_KEYS - cb * WINDOW - rows

    @pl.when(pl.program_id(0) == 0)
    def _():
        cnt_s[...] = cnt_in_ref[...]

    x = x_ref[...]
    mod = mod_ref[...]
    md = lambda j: mod[:, j * D_MODEL:(j + 1) * D_MODEL]
    h = _rms(x, anorm_ref[...]) * (1.0 + md(1)) + md(0)
    proj = jnp.dot(h.astype(jnp.bfloat16), win_ref[...], preferred_element_type=jnp.float32)
    q, k, v, u, gate_b = _split_proj(proj)
    cos = cos_ref[...]
    sin = sin_ref[...]
    q = _rope(q, cos, sin) * (HEAD_DIM ** -0.5)
    k = _rope(k, cos, sin)
    q_s[...] = q.astype(jnp.bfloat16)
    k_s[...] = k
    v_s[...] = v
    knew_ref[...] = k
    vnew_ref[...] = v
    unew_ref[...] = u

    pos = lax.broadcasted_iota(jnp.int32, (n, CONV_DIM), 0) % dec
    um1 = jnp.where(pos >= 1, pltpu.roll(u, 1, 0), 0.0) + e1_ref[...]
    um2 = jnp.where(pos >= 2, pltpu.roll(u, 2, 0), 0.0) + e2_ref[...]
    cw = convw_ref[...]
    conv = (um2 * cw[0:1] + um1 * cw[1:2] + u * cw[2:3]) * gate_b
    conv_n = _group_norm64(conv, gnc_ref[...])

    lo = _half_lane_mask(rows)
    bias = bias_ref[...]

    def chunk(c, carry):
        r0 = pl.multiple_of(c * rows, rows)
        c0 = pl.multiple_of(c * (cb * WINDOW), cb * WINDOW)
        zpad = jnp.zeros((pad_keys, KV_DIM), jnp.float32)
        kc = jnp.concatenate([ck_ref[pl.ds(c0, cb * WINDOW), :], k_s[pl.ds(r0, rows), :], zpad], axis=0)
        vc = jnp.concatenate([cv_ref[pl.ds(c0, cb * WINDOW), :], v_s[pl.ds(r0, rows), :], zpad], axis=0)
        keys, keys_sw = kc.astype(jnp.bfloat16), _swap_halves(kc).astype(jnp.bfloat16)
        vals, vals_sw = vc.astype(jnp.bfloat16), _swap_halves(vc).astype(jnp.bfloat16)
        qb = q_s[pl.ds(r0, rows), :]
        outs = []
        for g in range(N_KV_HEADS):
            outs += _attend(qb, keys, keys_sw, vals, vals_sw, bias, sinks_ref, g)
        att_s[pl.ds(r0, rows), :] = _merge_heads(outs, lo)
        return carry

    lax.fori_loop(0, n // rows, chunk, 0)
    attn_n = att_s[...] * gna_ref[...]
    _mixer_tail(x, attn_n, conv_n, (md(2), md(3), md(4)), wout_ref, fnorm_ref, wr_ref, br_ref, tri_ref,
                cnt_s, x1_ref, h2_ref, ri_ref, rf_ref)
    cnt_out_ref[...] = cnt_s[...]


def _sample_mixer(x, mod, sinks, consts, ck, cv, e1, e2, cnt_in, dec):
    n_s = x.shape[0]
    ts = SAMPLE_TILE
    seqs = ts // dec
    full = lambda a: pl.BlockSpec(a.shape, lambda i, s: (0,) * a.ndim)
    rows = lambda width: pl.BlockSpec((ts, width), lambda i, s: (i, 0))
    in_specs = [
        rows(D_MODEL), rows(6 * D_MODEL),
        full(consts["anorm"]), full(consts["fnorm"]), full(consts["w_in"]),
        rows(LANES), rows(LANES), full(consts["bias_s"]),
        full(consts["conv_w"]), full(consts["gn_attn"]), full(consts["gn_conv"]), full(consts["w_out"]),
        full(consts["w_r"]), full(consts["b_r"]),
        pl.BlockSpec((ts, ts), lambda i, s: (0, 0)),
        pl.BlockSpec((seqs * WINDOW, KV_DIM), lambda i, s: (i, 0)),
        pl.BlockSpec((seqs * WINDOW, KV_DIM), lambda i, s: (i, 0)),
        rows(CONV_DIM), rows(CONV_DIM), full(cnt_in),
    ]
    args = [x, mod, consts["anorm"], consts["fnorm"], consts["w_in"], consts["cos_s"], consts["sin_s"],
            consts["bias_s"], consts["conv_w"], consts["gn_attn"], consts["gn_conv"], consts["w_out"],
            consts["w_r"], consts["b_r"], consts["tri"], ck, cv, e1, e2, cnt_in]
    nt = n_s // ts
    out_shape = [
        jax.ShapeDtypeStruct((n_s, D_MODEL), jnp.float32),
        jax.ShapeDtypeStruct((n_s, D_MODEL // 2), jnp.uint32),
        jax.ShapeDtypeStruct((nt, 8, ts), jnp.int32),
        jax.ShapeDtypeStruct((nt, 8, ts), jnp.float32),
        jax.ShapeDtypeStruct((N_EXPERTS, LANES), jnp.float32),
        jax.ShapeDtypeStruct((n_s, KV_DIM), jnp.float32),
        jax.ShapeDtypeStruct((n_s, KV_DIM), jnp.float32),
        jax.ShapeDtypeStruct((n_s, CONV_DIM), jnp.float32),
    ]
    out_specs = [
        rows(D_MODEL),
        rows(D_MODEL // 2),
        pl.BlockSpec((None, 8, ts), lambda i, s: (i, 0, 0)),
        pl.BlockSpec((None, 8, ts), lambda i, s: (i, 0, 0)),
        pl.BlockSpec((N_EXPERTS, LANES), lambda i, s: (0, 0)),
        rows(KV_DIM), rows(KV_DIM), rows(CONV_DIM),
    ]
    scratch = [
        pltpu.VMEM((ts, ATTN_DIM), jnp.bfloat16),
        pltpu.VMEM((ts, KV_DIM), jnp.float32), pltpu.VMEM((ts, KV_DIM), jnp.float32),
        pltpu.VMEM((ts, ATTN_DIM), jnp.float32),
        pltpu.VMEM((N_EXPERTS, LANES), jnp.float32),
    ]
    return pl.pallas_call(
        functools.partial(_sample_mixer_kernel, dec=dec),
        grid_spec=pltpu.PrefetchScalarGridSpec(
            num_scalar_prefetch=1, grid=(nt,), in_specs=in_specs, out_specs=out_specs, scratch_shapes=scratch),
        out_shape=out_shape,
        compiler_params=pltpu.CompilerParams(dimension_semantics=("arbitrary",), vmem_limit_bytes=VMEM_LIMIT),
        name="sample_mixer",
    )(sinks, *args)


def _dispatch(idx_p, idx_s, h2_p, h2_s, n_rows):
    mesh = plsc.VectorSubcoreMesh(core_axis_name="core", subcore_axis_name="subcore")
    rows, window = SC_ROWS, SC_INDEX_WINDOW
    sub = window // rows

    width = h2_p.shape[1]

    @pl.kernel(out_type=jax.ShapeDtypeStruct((n_rows, width), h2_p.dtype), mesh=mesh, scratch_types=[])
    def scatter(hp_hbm, hs_hbm, ip_hbm, is_hbm, xs_hbm):
        def run(h_hbm, i_hbm):
            n_tok = h_hbm.shape[0]

            def body(indices, h_vmem, i_vmem):
                pltpu.sync_copy(h_vmem, xs_hbm.at[i_vmem.at[0, pl.ds(indices[1] * rows, rows)]])

            pltpu.emit_pipeline(
                body, grid=(2 * n_tok // window, sub),
                in_specs=[pl.BlockSpec((rows, width), lambda i, j: ((i * sub + j) % (n_tok // rows), 0)),
                          pl.BlockSpec((1, window), lambda i, j: (0, i))],
                out_specs=[], core_axis_name=("core", "subcore"),
                dimension_semantics=(pltpu.PARALLEL, pltpu.ARBITRARY), _explicit_indices=True,
            )(h_hbm, i_hbm)

        run(hp_hbm, ip_hbm)
        run(hs_hbm, is_hbm)

    return scatter(h2_p, h2_s, idx_p, idx_s)


def _experts_kernel(be_ref, nu_ref, nv_ref, xs_ref, wg_ref, wu_ref, wd_ref, y_ref, wg_s, wu_s, wd_s):
    j = pl.program_id(0)
    used = j < nu_ref[0]
    changed = jnp.logical_or(j == 0, be_ref[j] != be_ref[jnp.maximum(j - 1, 0)])

    @pl.when(jnp.logical_and(used, changed))
    def _():
        wg_s[...] = wg_ref[...].astype(jnp.bfloat16)
        wu_s[...] = wu_ref[...].astype(jnp.bfloat16)
        wd_s[...] = wd_ref[...].astype(jnp.bfloat16)

    @pl.when(used)
    def _():
        live = lax.broadcasted_iota(jnp.int32, (xs_ref.shape[0], 1), 0) < nv_ref[j]
        x = _unpack_bf16_pairs(jnp.where(live, xs_ref[...], jnp.uint32(0)))
        g = jnp.dot(x, wg_s[...], preferred_element_type=jnp.float32)
        u = jnp.dot(x, wu_s[...], preferred_element_type=jnp.float32)
        a = (g * jax.nn.sigmoid(g) * u).astype(jnp.bfloat16)
        y_ref[...] = jnp.dot(a, wd_s[...], preferred_element_type=jnp.float32).reshape(y_ref.shape)

    @pl.when(jnp.logical_not(used))
    def _():
        y_ref[...] = jnp.zeros_like(y_ref)


def _experts(block_e, n_used, n_valid, xs, w_gate, w_up, w_down):
    bm = EXPERT_BLOCK
    n_blocks = xs.shape[0] // bm
    row = lambda j, be, nu, nv: (jnp.minimum(j, nu[0] - 1), 0)
    wmap = lambda j, be, nu, nv: (be[j], 0, 0)
    return pl.pallas_call(
        _experts_kernel,
        grid_spec=pltpu.PrefetchScalarGridSpec(
            num_scalar_prefetch=3, grid=(n_blocks,),
            in_specs=[pl.BlockSpec((bm, D_MODEL // 2), row),
                      pl.BlockSpec((None, D_MODEL, EXPERT_HIDDEN), wmap),
                      pl.BlockSpec((None, D_MODEL, EXPERT_HIDDEN), wmap),
                      pl.BlockSpec((None, EXPERT_HIDDEN, D_MODEL), wmap)],
            out_specs=pl.BlockSpec((bm, 1, D_MODEL), lambda j, be, nu, nv: (j, 0, 0)),
            scratch_shapes=[pltpu.VMEM((D_MODEL, EXPERT_HIDDEN), jnp.bfloat16),
                            pltpu.VMEM((D_MODEL, EXPERT_HIDDEN), jnp.bfloat16),
                            pltpu.VMEM((EXPERT_HIDDEN, D_MODEL), jnp.bfloat16)]),
        out_shape=jax.ShapeDtypeStruct((xs.shape[0], 1, D_MODEL), jnp.float32),
        compiler_params=pltpu.CompilerParams(dimension_semantics=("arbitrary",), vmem_limit_bytes=VMEM_LIMIT),
        name="experts",
    )(block_e, n_used, n_valid, xs, w_gate, w_up, w_down)


def _combine_kernel(dest_ref, x1_ref, gate_ref, rf_ref, fin_ref, yb_ref, o_ref, ybuf, y_s, sem, *,
                    n_tok, per_token_gate):
    i = pl.program_id(0)
    n = pl.num_programs(0)
    tc = COMBINE_TILE

    def issue(step, slot):
        def one(r, carry):
            tok = step * tc + r
            for k in range(2):
                d = dest_ref[k * n_tok + tok]
                pltpu.make_async_copy(yb_ref.at[pl.ds(d, 1)], ybuf.at[slot, k, pl.ds(r, 1)], sem.at[slot]).start()
            return carry
        lax.fori_loop(0, tc, one, 0, unroll=8)

    slot = i % 2

    @pl.when(i == 0)
    def _():
        issue(0, 0)

    @pl.when(i + 1 < n)
    def _():
        issue(i + 1, 1 - slot)

    pltpu.make_async_copy(yb_ref.at[pl.ds(0, tc)], ybuf.at[slot, 0], sem.at[slot]).wait()
    pltpu.make_async_copy(yb_ref.at[pl.ds(0, tc)], ybuf.at[slot, 1], sem.at[slot]).wait()

    gates = jnp.transpose(rf_ref[...])
    y_s[0] = ybuf[slot, 0].reshape(tc, D_MODEL)
    y_s[1] = ybuf[slot, 1].reshape(tc, D_MODEL)
    y0 = y_s[0]
    y1 = y_s[1]
    moe = y0 * gates[:, 0:1] + y1 * gates[:, 1:2]
    gate_f = gate_ref[...] if per_token_gate else gate_ref[...][5:6]
    o_ref[...] = _rms(x1_ref[...] + gate_f * moe, fin_ref[...])


def _combine(dest_flat, x1, gate, rf, final_norm, yb, *, tiles_per_gate):
    n_tok = x1.shape[0]
    tc = COMBINE_TILE
    per_token = tiles_per_gate == 0
    if per_token:
        gate_spec = pl.BlockSpec((tc, D_MODEL), lambda i, d: (i, 0))
    else:
        gate_spec = pl.BlockSpec((None, 6, D_MODEL), lambda i, d: (i // tiles_per_gate, 0, 0))
    return pl.pallas_call(
        functools.partial(_combine_kernel, n_tok=n_tok, per_token_gate=per_token),
        grid_spec=pltpu.PrefetchScalarGridSpec(
            num_scalar_prefetch=1, grid=(n_tok // tc,),
            in_specs=[pl.BlockSpec((tc, D_MODEL), lambda i, d: (i, 0)),
                      gate_spec,
                      pl.BlockSpec((8, tc), lambda i, d: (0, i)),
                      pl.BlockSpec((1, D_MODEL), lambda i, d: (0, 0)),
                      pl.BlockSpec(memory_space=pl.ANY)],
            out_specs=pl.BlockSpec((tc, D_MODEL), lambda i, d: (i, 0)),
            scratch_shapes=[pltpu.VMEM((2, 2, tc, 1, D_MODEL), jnp.float32),
                            pltpu.VMEM((2, tc, D_MODEL), jnp.float32), pltpu.SemaphoreType.DMA((2,))]),
        out_shape=jax.ShapeDtypeStruct((n_tok, D_MODEL), jnp.float32),
        compiler_params=pltpu.CompilerParams(dimension_semantics=("arbitrary",), vmem_limit_bytes=VMEM_LIMIT),
        name="combine",
    )(dest_flat, x1, gate, rf, final_norm, yb)


def _rope_tables(pos):
    half = HEAD_DIM // 2
    inv_freq = np.power(np.float32(ROPE_THETA), -(np.arange(half, dtype=np.float32) / np.float32(half)))
    ang = (pos.astype(np.float32)[:, None] * inv_freq[None, :].astype(np.float32)).astype(np.float64)
    cos, sin = np.cos(ang).astype(np.float32), np.sin(ang).astype(np.float32)
    cos = np.concatenate([cos, cos, cos, cos], axis=1)
    sin = np.concatenate([-sin, sin, -sin, sin], axis=1)
    return cos, sin


def _prompt_bias():
    r = np.arange(WINDOW)[:, None]
    j = np.arange(2 * WINDOW)[None, :]
    band = (j > r) & (j <= r + WINDOW)
    later = np.where(band, 0.0, NEG).astype(np.float32)
    first = np.where(band & (j >= WINDOW), 0.0, NEG).astype(np.float32)
    return jnp.asarray(np.stack([first, later]))


def _sample_bias(dec):
    cb = SAMPLE_CHUNK
    qb = np.repeat(np.arange(cb), dec)[:, None]
    qi = np.tile(np.arange(dec), cb)[:, None]
    kb = np.concatenate([np.repeat(np.arange(cb), WINDOW), np.repeat(np.arange(cb), dec)])[None, :]
    kj = np.concatenate([np.tile(np.arange(WINDOW), cb), np.tile(np.arange(dec), cb)])[None, :]
    is_new = (np.arange(cb * WINDOW + cb * dec) >= cb * WINDOW)[None, :]
    ok = (kb == qb) & np.where(is_new, kj <= qi, kj > qi)
    ok = np.pad(ok, ((0, 0), (0, SAMPLE_KEYS - ok.shape[1])))
    return jnp.asarray(np.where(ok, 0.0, NEG).astype(np.float32))


def kernel(x_prompt, x_sample, cache_k, cache_v, state_conv, c_prompt, c_sample, attn_norm, ffn_norm, w_mod, b_mod, w_in, conv_w, attn_sinks, out_norm_attn, out_norm_conv, w_out, w_group, b_group, w_expert, b_expert, w_gate, w_up, w_down, final_norm):
    assert attn_norm.shape[0] == 1, "one layer"
    bsz, seq, _ = x_prompt.shape
    nseq, dec, _ = x_sample.shape
    m_prompt = bsz * seq
    n_s = nseq * dec
    m_total = m_prompt + n_s
    tm = MIXER_TILE
    assert seq % tm == 0 and n_s % SAMPLE_TILE == 0 and SAMPLE_TILE % (SAMPLE_CHUNK * dec) == 0
    assert cache_k.shape[2] == WINDOW and m_prompt % SAMPLE_TILE == 0 and m_total % COMBINE_TILE == 0
    bf = jnp.bfloat16

    w_r = jnp.zeros((ROUTER_ROWS, D_MODEL), jnp.float32)
    w_r = w_r.at[0:N_GROUPS].set(w_group[0].T).at[8:8 + N_EXPERTS].set(w_expert[0].T).astype(bf)
    b_r = jnp.zeros((ROUTER_ROWS,), jnp.float32).at[0:N_GROUPS].set(b_group[0]).at[8:8 + N_EXPERTS].set(b_expert[0])
    b_r = jnp.broadcast_to(b_r[:, None], (ROUTER_ROWS, LANES))
    cos_p, sin_p = _rope_tables(np.arange(seq))
    cos_s, sin_s = _rope_tables(PAST_LEN + np.arange(dec))
    tri = jnp.asarray(np.triu(np.ones((tm, tm), np.float32), 1)).astype(bf)
    consts = dict(
        anorm=attn_norm[0][None], fnorm=ffn_norm[0][None], w_in=w_in[0].astype(bf),
        cos_p=jnp.asarray(cos_p), sin_p=jnp.asarray(sin_p),
        cos_s=jnp.asarray(np.tile(cos_s, (nseq, 1))), sin_s=jnp.asarray(np.tile(sin_s, (nseq, 1))),
        bias_p=_prompt_bias(), bias_s=_sample_bias(dec), conv_w=conv_w[0],
        gn_attn=out_norm_attn[0][None], gn_conv=out_norm_conv[0][None], w_out=w_out[0].astype(bf),
        w_r=w_r, b_r=b_r, tri=tri)
    sinks = attn_sinks[0]

    mod = _adaln(jnp.concatenate([c_prompt, c_sample], axis=0), w_mod[0], b_mod[0])
    mod_p = mod[:bsz].reshape(bsz, 6, D_MODEL)
    mod_s = jnp.repeat(mod[bsz:], dec, axis=0)

    x1_p, h2_p, ri_p, rf_p, cnt_p, k_p, v_p, u_p = _prompt_mixer(x_prompt, mod_p, sinks, consts)
    st = state_conv[0]
    zrow = jnp.zeros((nseq, 1, CONV_DIM), jnp.float32)
    e1 = jnp.concatenate([st[:, 1:2]] + [zrow] * (dec - 1), axis=1).reshape(n_s, CONV_DIM)
    e2 = jnp.concatenate([st[:, 0:1], st[:, 1:2]] + [zrow] * (dec - 2), axis=1).reshape(n_s, CONV_DIM)
    ck = cache_k[0].reshape(nseq * WINDOW, KV_DIM)
    cv = cache_v[0].reshape(nseq * WINDOW, KV_DIM)
    x1_s, h2_s, ri_s, rf_s, cnt, k_s, v_s, u_s = _sample_mixer(
        x_sample.reshape(n_s, D_MODEL), mod_s, sinks, consts, ck, cv, e1, e2, cnt_p, dec)

    bm = EXPERT_BLOCK
    counts = cnt[:, 0].astype(jnp.int32)
    padded = (counts + bm - 1) // bm * bm
    pend = jnp.cumsum(padded)
    pstart = pend - padded
    n_blocks = -(-(2 * m_total + N_EXPERTS * (bm - 1)) // bm)
    n_used = (pend[-1] // bm).astype(jnp.int32)
    blk = jnp.arange(n_blocks, dtype=jnp.int32)
    eids = jnp.arange(N_EXPERTS, dtype=jnp.int32)
    block_e = jnp.sum((pend[None, :] <= (blk * bm)[:, None]).astype(jnp.int32), axis=1)
    last_e = jnp.max(jnp.where(padded > 0, eids, 0))
    block_e = jnp.where(blk < n_used, jnp.minimum(block_e, N_EXPERTS - 1), last_e)
    by_token = lambda a: jnp.transpose(a, (1, 0, 2)).reshape(8, -1)
    ri = jnp.concatenate([by_token(ri_p), by_token(ri_s)], axis=1)
    rf_flat = jnp.concatenate([by_token(rf_p), by_token(rf_s)], axis=1)
    start_of = jnp.sum(jnp.where(ri[0:2][..., None] == eids, pstart, 0), axis=-1)
    dest_flat = (start_of + ri[2:4]).reshape(2 * m_total)

    n_valid = jnp.clip(counts[block_e] - (blk * bm - pstart[block_e]), 0, bm)
    n_valid = jnp.where(blk < n_used, n_valid, 0).astype(jnp.int32)

    dest_p = jnp.concatenate([dest_flat[:m_prompt], dest_flat[m_total:m_total + m_prompt]])
    dest_s = jnp.concatenate([dest_flat[m_prompt:m_total], dest_flat[m_total + m_prompt:]])
    xs = _dispatch(dest_p[None], dest_s[None], h2_p, h2_s, n_blocks * bm)
    yb = _experts(block_e, n_used.reshape(1), n_valid, xs, w_gate[0], w_up[0], w_down[0])

    fin = final_norm[None]
    y_p = _combine(dest_p, x1_p, mod_p, rf_flat[:, :m_prompt], fin, yb, tiles_per_gate=seq // COMBINE_TILE)
    gate_s = mod_s[:, 5 * D_MODEL:]
    y_s = _combine(dest_s, x1_s, gate_s, rf_flat[:, m_prompt:], fin, yb, tiles_per_gate=0)

    keep = WINDOW
    new_k_p = k_p.reshape(1, bsz, keep, N_KV_HEADS, HEAD_DIM)
    new_v_p = v_p.reshape(1, bsz, keep, N_KV_HEADS, HEAD_DIM)
    new_u_p = u_p[:, 6:8][None]
    ksn = k_s.reshape(nseq, dec, N_KV_HEADS, HEAD_DIM)
    vsn = v_s.reshape(nseq, dec, N_KV_HEADS, HEAD_DIM)
    new_k_s = jnp.concatenate([cache_k[0][:, dec:], ksn], axis=1)[None]
    new_v_s = jnp.concatenate([cache_v[0][:, dec:], vsn], axis=1)[None]
    u_all = jnp.concatenate([st, u_s.reshape(nseq, dec, CONV_DIM)], axis=1)
    new_u_s = u_all[:, -2:][None]
    return (y_p.reshape(bsz, seq, D_MODEL), y_s.reshape(nseq, dec, D_MODEL),
            new_k_p, new_v_p, new_u_p, new_k_s, new_v_s, new_u_s)
```

```python
import functools

import jax
import jax.numpy as jnp
import numpy as np
from jax import lax
from jax.experimental import pallas as pl
from jax.experimental.pallas import tpu as pltpu
from jax.experimental.pallas import tpu_sc as plsc

D_MODEL = 1024
HEAD_DIM = 64
ATTN_DIM = 512
N_Q_HEADS = 8
N_KV_HEADS = 2
KV_DIM = 128
CONV_DIM = 512
WINDOW = 128
IN_PROJ_DIM = ATTN_DIM + 2 * KV_DIM + 3 * CONV_DIM
N_GROUPS = 4
EXPERTS_PER_GROUP = 8
N_EXPERTS = 32
EXPERT_HIDDEN = 512
RMS_EPS = 1e-5
ROPE_THETA = 10000.0
PAST_LEN = 16384

LANES = 128
ROUTER_ROWS = 48
NEG = -1e30

MIXER_TILE = 512
SAMPLE_TILE = 128
SAMPLE_CHUNK = 8
SAMPLE_KEYS = 1152
EXPERT_BLOCK = 512
COMBINE_TILE = 256
SC_ROWS = 64
SC_INDEX_WINDOW = 128
VMEM_LIMIT = 56 * 1024 * 1024


def _rms(x, gain):
    ms = jnp.mean(x * x, axis=-1, keepdims=True)
    return x * lax.rsqrt(ms + RMS_EPS) * gain


def _half_lane_mask(rows=1):
    return lax.broadcasted_iota(jnp.int32, (rows, LANES), 1) < HEAD_DIM


def _rope(x, cos, sin_signed):
    t, w = x.shape
    reps = w // LANES
    lane = lax.broadcasted_iota(jnp.int32, (t, w), 1)
    upper = (lane % HEAD_DIM) >= (HEAD_DIM // 2)
    partner = jnp.where(upper, pltpu.roll(x, HEAD_DIM // 2, 1), pltpu.roll(x, w - HEAD_DIM // 2, 1))
    if reps > 1:
        cos = jnp.concatenate([cos] * reps, axis=1)
        sin_signed = jnp.concatenate([sin_signed] * reps, axis=1)
    return x * cos + partner * sin_signed


def _swap_halves(x):
    return pltpu.roll(x, HEAD_DIM, 1)


def _group_norm64(x, gain):
    t, w = x.shape
    lo = _half_lane_mask(t)
    outs = []
    for c in range(w // LANES):
        xc = x[:, c * LANES:(c + 1) * LANES]
        sq = xc * xc
        s_lo = jnp.sum(jnp.where(lo, sq, 0.0), axis=-1, keepdims=True)
        s_hi = jnp.sum(jnp.where(lo, 0.0, sq), axis=-1, keepdims=True)
        r = jnp.where(lo, lax.rsqrt(s_lo * (1.0 / HEAD_DIM) + RMS_EPS), lax.rsqrt(s_hi * (1.0 / HEAD_DIM) + RMS_EPS))
        outs.append(xc * r)
    return jnp.concatenate(outs, axis=1) * gain


def _attend(q_rows, keys, keys_sw, vals, vals_sw, bias, sinks_ref, group):
    r = q_rows.shape[0]
    lo = _half_lane_mask(r)
    outs = []
    for parity in range(2):
        heads = [4 * group + parity, 4 * group + parity + 2]
        kk, vv = (keys, vals) if parity == group else (keys_sw, vals_sw)
        half = lo if parity == 0 else jnp.logical_not(lo)
        qs = []
        for h in heads:
            pair = q_rows[:, (h // 2) * LANES:(h // 2 + 1) * LANES]
            qs.append(jnp.where(half, pair, jnp.zeros_like(pair)))
        qcat = jnp.concatenate(qs, axis=0)
        s = lax.dot_general(qcat, kk, (((1,), (1,)), ((), ())), preferred_element_type=jnp.float32)
        ps, dens = [], []
        for i, h in enumerate(heads):
            sh = s[i * r:(i + 1) * r] + bias
            sink = sinks_ref[h]
            m = jnp.maximum(jnp.max(sh, axis=-1, keepdims=True), sink)
            p = jnp.exp(sh - m)
            dens.append(jnp.sum(p, axis=-1, keepdims=True) + jnp.exp(sink - m))
            ps.append(p.astype(jnp.bfloat16))
        o = jnp.dot(jnp.concatenate(ps, axis=0), vv, preferred_element_type=jnp.float32)
        for i, h in enumerate(heads):
            oh = o[i * r:(i + 1) * r] / dens[i]
            ms = jnp.sum(jnp.where(half, oh * oh, 0.0), axis=-1, keepdims=True) * (1.0 / HEAD_DIM)
            outs.append((h, oh * lax.rsqrt(ms + RMS_EPS)))
    outs.sort(key=lambda t: t[0])
    return [o for _, o in outs]


def _merge_heads(head_outs, lo):
    pairs = [jnp.where(lo, head_outs[2 * j], head_outs[2 * j + 1]) for j in range(4)]
    return jnp.concatenate(pairs, axis=1)


def _route(h2b, wr_ref, br_ref, tri_ref, cnt_ref, ri_ref, rf_ref):
    t = h2b.shape[0]
    lg = lax.dot_general(wr_ref[...], h2b, (((1,), (1,)), ((), ())), preferred_element_type=jnp.float32)
    lg = lg + br_ref[...][:, 0:1]
    r8 = lax.broadcasted_iota(jnp.int32, (8, t), 0)
    lgrp = jnp.where(r8 < N_GROUPS, lg[0:8], -jnp.inf)
    gmax = jnp.max(lgrp, axis=0, keepdims=True)
    grp = jnp.min(jnp.where(lgrp == gmax, r8, 8), axis=0, keepdims=True)
    pg_sel = 1.0 / jnp.sum(jnp.exp(lgrp - gmax), axis=0, keepdims=True)
    r32 = lax.broadcasted_iota(jnp.int32, (N_EXPERTS, t), 0)
    le = jnp.where((r32 // EXPERTS_PER_GROUP) == grp, lg[8:8 + N_EXPERTS], -jnp.inf)
    v1 = jnp.max(le, axis=0, keepdims=True)
    i1 = jnp.min(jnp.where(le == v1, r32, N_EXPERTS), axis=0, keepdims=True)
    le2 = jnp.where(r32 == i1, -jnp.inf, le)
    v2 = jnp.max(le2, axis=0, keepdims=True)
    i2 = jnp.min(jnp.where(le2 == v2, r32, N_EXPERTS), axis=0, keepdims=True)
    e21 = jnp.exp(v2 - v1)
    gate1 = pg_sel / (1.0 + e21)
    gate2 = pg_sel * e21 / (1.0 + e21)
    hot1 = r32 == i1
    hot2 = r32 == i2
    onehot = jnp.where(jnp.logical_or(hot1, hot2), 1.0, 0.0)
    before = jnp.dot(onehot.astype(jnp.bfloat16), tri_ref[...], preferred_element_type=jnp.float32)
    pos = before + cnt_ref[...][:, 0:1]
    rank1 = jnp.sum(jnp.where(hot1, pos, 0.0), axis=0, keepdims=True)
    rank2 = jnp.sum(jnp.where(hot2, pos, 0.0), axis=0, keepdims=True)
    cnt_ref[...] = cnt_ref[...] + jnp.sum(onehot, axis=1, keepdims=True)
    zi = jnp.zeros((4, t), jnp.int32)
    ri_ref[...] = jnp.concatenate([i1, i2, rank1.astype(jnp.int32), rank2.astype(jnp.int32), zi], axis=0)
    rf_ref[...] = jnp.concatenate([gate1, gate2, jnp.zeros((6, t), jnp.float32)], axis=0)


def _pack_bf16_pairs(xb):
    w = xb.shape[1] // 2
    bits = pltpu.bitcast(xb.astype(jnp.float32), jnp.uint32)
    return (bits[:, :w] >> 16) | (bits[:, w:] & jnp.uint32(0xFFFF0000))


def _unpack_bf16_pairs(words):
    lo = pltpu.bitcast(words << 16, jnp.float32)
    hi = pltpu.bitcast(words & jnp.uint32(0xFFFF0000), jnp.float32)
    return jnp.concatenate([lo.astype(jnp.bfloat16), hi.astype(jnp.bfloat16)], axis=1)


def _split_proj(proj):
    a = ATTN_DIM
    q = proj[:, :a]
    k = proj[:, a:a + KV_DIM]
    v = proj[:, a + KV_DIM:a + 2 * KV_DIM]
    c0 = a + 2 * KV_DIM
    h_conv = proj[:, c0:c0 + CONV_DIM]
    gate_b = proj[:, c0 + CONV_DIM:c0 + 2 * CONV_DIM]
    gate_c = proj[:, c0 + 2 * CONV_DIM:c0 + 3 * CONV_DIM]
    return q, k, v, gate_c * h_conv, gate_b


def _mixer_tail(x, attn_n, conv_n, mod, wout_ref, fnorm_ref, wr_ref, br_ref, tri_ref, cnt_ref,
                x1_ref, h2_ref, ri_ref, rf_ref):
    gate_a, shift_f, scale_f = mod
    cat = jnp.concatenate([attn_n.astype(jnp.bfloat16), conv_n.astype(jnp.bfloat16)], axis=1)
    mix = jnp.dot(cat, wout_ref[...], preferred_element_type=jnp.float32)
    x1 = x + gate_a * mix
    x1_ref[...] = x1
    h2 = _rms(x1, fnorm_ref[...] * (1.0 + scale_f)) + shift_f
    h2b = h2.astype(jnp.bfloat16)
    h2_ref[...] = _pack_bf16_pairs(h2b)
    _route(h2b, wr_ref, br_ref, tri_ref, cnt_ref, ri_ref, rf_ref)


def _adaln_kernel(c_ref, w_ref, b_ref, o_ref):
    c = c_ref[...]
    a = (c * jax.nn.sigmoid(c)).astype(jnp.bfloat16)
    o_ref[...] = jnp.dot(a, w_ref[...].astype(jnp.bfloat16), preferred_element_type=jnp.float32) + b_ref[...]


def _adaln(c, w_mod, b_mod):
    n = c.shape[0]
    tn = 1536
    return pl.pallas_call(
        _adaln_kernel,
        grid=(w_mod.shape[1] // tn,),
        in_specs=[pl.BlockSpec((n, D_MODEL), lambda j: (0, 0)),
                  pl.BlockSpec((D_MODEL, tn), lambda j: (0, j)),
                  pl.BlockSpec((1, tn), lambda j: (0, j))],
        out_specs=pl.BlockSpec((n, tn), lambda j: (0, j)),
        out_shape=jax.ShapeDtypeStruct((n, w_mod.shape[1]), jnp.float32),
        compiler_params=pltpu.CompilerParams(dimension_semantics=("arbitrary",), vmem_limit_bytes=VMEM_LIMIT),
        name="adaln",
    )(c, w_mod, b_mod.reshape(1, -1))


def _prompt_mixer_kernel(sinks_ref, x_ref, mod_ref, anorm_ref, fnorm_ref, win_ref, cos_ref, sin_ref, bias_ref,
                         convw_ref, gna_ref, gnc_ref, wout_ref, wr_ref, br_ref, tri_ref,
                         x1_ref, h2_ref, ri_ref, rf_ref, cnt_out_ref, knew_ref, vnew_ref, unew_ref,
                         q_s, k_s, ksw_s, v_s, vsw_s, u_s, att_s, cnt_s):
    b = pl.program_id(0)
    t = pl.program_id(1)
    tm = MIXER_TILE
    nblk = tm // WINDOW

    @pl.when(jnp.logical_and(b == 0, t == 0))
    def _():
        cnt_s[...] = jnp.zeros_like(cnt_s)

    @pl.when(t == 0)
    def _():
        z = jnp.zeros((WINDOW, KV_DIM), jnp.bfloat16)
        k_s[0:WINDOW, :] = z
        ksw_s[0:WINDOW, :] = z
        v_s[0:WINDOW, :] = z
        vsw_s[0:WINDOW, :] = z
        u_s[0:8, :] = jnp.zeros((8, CONV_DIM), jnp.float32)

    x = x_ref[...]
    mod = mod_ref[...]
    h = _rms(x, anorm_ref[...] * (1.0 + mod[1:2])) + mod[0:1]
    proj = jnp.dot(h.astype(jnp.bfloat16), win_ref[...], preferred_element_type=jnp.float32)
    q, k, v, u, gate_b = _split_proj(proj)
    cos = cos_ref[...]
    sin = sin_ref[...]
    q = _rope(q, cos, sin) * (HEAD_DIM ** -0.5)
    k = _rope(k, cos, sin)
    q_s[...] = q.astype(jnp.bfloat16)
    k_s[WINDOW:, :] = k.astype(jnp.bfloat16)
    ksw_s[WINDOW:, :] = _swap_halves(k).astype(jnp.bfloat16)
    v_s[WINDOW:, :] = v.astype(jnp.bfloat16)
    vsw_s[WINDOW:, :] = _swap_halves(v).astype(jnp.bfloat16)
    knew_ref[...] = k[tm - WINDOW:, :]
    vnew_ref[...] = v[tm - WINDOW:, :]

    u_s[8:, :] = u
    unew_ref[...] = u[tm - 8:, :]
    cw = convw_ref[...]
    conv = (u_s[6:6 + tm, :] * cw[0:1] + u_s[7:7 + tm, :] * cw[1:2] + u * cw[2:3]) * gate_b
    u_s[0:8, :] = u[tm - 8:, :]
    conv_n = _group_norm64(conv, gnc_ref[...])

    lo = _half_lane_mask(WINDOW)
    first = t == 0

    def block(i, carry):
        r0 = pl.multiple_of(i * WINDOW, WINDOW)
        bias = bias_ref[jnp.where(jnp.logical_and(first, i == 0), 0, 1)]
        qb = q_s[pl.ds(r0, WINDOW), :]
        keys = k_s[pl.ds(r0, 2 * WINDOW), :]
        keys_sw = ksw_s[pl.ds(r0, 2 * WINDOW), :]
        vals = v_s[pl.ds(r0, 2 * WINDOW), :]
        vals_sw = vsw_s[pl.ds(r0, 2 * WINDOW), :]
        outs = []
        for g in range(N_KV_HEADS):
            outs += _attend(qb, keys, keys_sw, vals, vals_sw, bias, sinks_ref, g)
        att_s[pl.ds(r0, WINDOW), :] = _merge_heads(outs, lo)
        return carry

    lax.fori_loop(0, nblk, block, 0, unroll=True)
    k_s[0:WINDOW, :] = k_s[tm:tm + WINDOW, :]
    ksw_s[0:WINDOW, :] = ksw_s[tm:tm + WINDOW, :]
    v_s[0:WINDOW, :] = v_s[tm:tm + WINDOW, :]
    vsw_s[0:WINDOW, :] = vsw_s[tm:tm + WINDOW, :]

    attn_n = att_s[...] * gna_ref[...]
    _mixer_tail(x, attn_n, conv_n, (mod[2:3], mod[3:4], mod[4:5]), wout_ref, fnorm_ref, wr_ref, br_ref, tri_ref,
                cnt_s, x1_ref, h2_ref, ri_ref, rf_ref)
    cnt_out_ref[...] = cnt_s[...]


def _prompt_mixer(x, mod, sinks, consts):
    bsz, seq, _ = x.shape
    tm = MIXER_TILE
    nt = seq // tm
    full = lambda shape: pl.BlockSpec(shape, lambda b, t, s: (0,) * len(shape))
    in_specs = [
        pl.BlockSpec((None, tm, D_MODEL), lambda b, t, s: (b, t, 0)),
        pl.BlockSpec((None, 6, D_MODEL), lambda b, t, s: (b, 0, 0)),
        full((1, D_MODEL)), full((1, D_MODEL)),
        full((D_MODEL, IN_PROJ_DIM)),
        pl.BlockSpec((tm, LANES), lambda b, t, s: (t, 0)),
        pl.BlockSpec((tm, LANES), lambda b, t, s: (t, 0)),
        full((2, WINDOW, 2 * WINDOW)),
        full((3, CONV_DIM)), full((1, ATTN_DIM)), full((1, CONV_DIM)),
        full((D_MODEL, D_MODEL)),
        full((ROUTER_ROWS, D_MODEL)), full((ROUTER_ROWS, LANES)),
        full((tm, tm)),
    ]
    out_shape = [
        jax.ShapeDtypeStruct((bsz * seq, D_MODEL), jnp.float32),
        jax.ShapeDtypeStruct((bsz * seq, D_MODEL // 2), jnp.uint32),
        jax.ShapeDtypeStruct((bsz * nt, 8, tm), jnp.int32),
        jax.ShapeDtypeStruct((bsz * nt, 8, tm), jnp.float32),
        jax.ShapeDtypeStruct((N_EXPERTS, LANES), jnp.float32),
        jax.ShapeDtypeStruct((bsz, WINDOW, KV_DIM), jnp.float32),
        jax.ShapeDtypeStruct((bsz, WINDOW, KV_DIM), jnp.float32),
        jax.ShapeDtypeStruct((bsz, 8, CONV_DIM), jnp.float32),
    ]
    out_specs = [
        pl.BlockSpec((tm, D_MODEL), lambda b, t, s: (b * nt + t, 0)),
        pl.BlockSpec((tm, D_MODEL // 2), lambda b, t, s: (b * nt + t, 0)),
        pl.BlockSpec((None, 8, tm), lambda b, t, s: (b * nt + t, 0, 0)),
        pl.BlockSpec((None, 8, tm), lambda b, t, s: (b * nt + t, 0, 0)),
        pl.BlockSpec((N_EXPERTS, LANES), lambda b, t, s: (0, 0)),
        pl.BlockSpec((None, WINDOW, KV_DIM), lambda b, t, s: (b, 0, 0)),
        pl.BlockSpec((None, WINDOW, KV_DIM), lambda b, t, s: (b, 0, 0)),
        pl.BlockSpec((None, 8, CONV_DIM), lambda b, t, s: (b, 0, 0)),
    ]
    scratch = [
        pltpu.VMEM((tm, ATTN_DIM), jnp.bfloat16),
        pltpu.VMEM((tm + WINDOW, KV_DIM), jnp.bfloat16), pltpu.VMEM((tm + WINDOW, KV_DIM), jnp.bfloat16),
        pltpu.VMEM((tm + WINDOW, KV_DIM), jnp.bfloat16), pltpu.VMEM((tm + WINDOW, KV_DIM), jnp.bfloat16),
        pltpu.VMEM((tm + 8, CONV_DIM), jnp.float32),
        pltpu.VMEM((tm, ATTN_DIM), jnp.float32),
        pltpu.VMEM((N_EXPERTS, LANES), jnp.float32),
    ]
    return pl.pallas_call(
        _prompt_mixer_kernel,
        grid_spec=pltpu.PrefetchScalarGridSpec(
            num_scalar_prefetch=1, grid=(bsz, nt), in_specs=in_specs, out_specs=out_specs, scratch_shapes=scratch),
        out_shape=out_shape,
        compiler_params=pltpu.CompilerParams(dimension_semantics=("arbitrary", "arbitrary"),
                                             vmem_limit_bytes=VMEM_LIMIT),
        name="prompt_mixer",
    )(sinks, x, mod, consts["anorm"], consts["fnorm"], consts["w_in"], consts["cos_p"], consts["sin_p"],
      consts["bias_p"], consts["conv_w"], consts["gn_attn"], consts["gn_conv"], consts["w_out"],
      consts["w_r"], consts["b_r"], consts["tri"])


def _sample_mixer_kernel(sinks_ref, x_ref, mod_ref, anorm_ref, fnorm_ref, win_ref, cos_ref, sin_ref, bias_ref,
                         convw_ref, gna_ref, gnc_ref, wout_ref, wr_ref, br_ref, tri_ref,
                         ck_ref, cv_ref, e1_ref, e2_ref, cnt_in_ref,
                         x1_ref, h2_ref, ri_ref, rf_ref, cnt_out_ref, knew_ref, vnew_ref, unew_ref,
                         q_s, k_s, v_s, att_s, cnt_s, *, dec):
    n = x_ref.shape[0]
    cb = SAMPLE_CHUNK
    rows = cb * dec
    pad_keys = SAMPLE_KEYS - cb * WINDOW - rows

    @pl.when(pl.program_id(0) == 0)
    def _():
        cnt_s[...] = cnt_in_ref[...]

    x = x_ref[...]
    mod = mod_ref[...]
    md = lambda j: mod[:, j * D_MODEL:(j + 1) * D_MODEL]
    h = _rms(x, anorm_ref[...]) * (1.0 + md(1)) + md(0)
    proj = jnp.dot(h.astype(jnp.bfloat16), win_ref[...], preferred_element_type=jnp.float32)
    q, k, v, u, gate_b = _split_proj(proj)
    cos = cos_ref[...]
    sin = sin_ref[...]
    q = _rope(q, cos, sin) * (HEAD_DIM ** -0.5)
    k = _rope(k, cos, sin)
    q_s[...] = q.astype(jnp.bfloat16)
    k_s[...] = k
    v_s[...] = v
    knew_ref[...] = k
    vnew_ref[...] = v
    unew_ref[...] = u

    pos = lax.broadcasted_iota(jnp.int32, (n, CONV_DIM), 0) % dec
    um1 = jnp.where(pos >= 1, pltpu.roll(u, 1, 0), 0.0) + e1_ref[...]
    um2 = jnp.where(pos >= 2, pltpu.roll(u, 2, 0), 0.0) + e2_ref[...]
    cw = convw_ref[...]
    conv = (um2 * cw[0:1] + um1 * cw[1:2] + u * cw[2:3]) * gate_b
    conv_n = _group_norm64(conv, gnc_ref[...])

    lo = _half_lane_mask(rows)
    bias = bias_ref[...]

    def chunk(c, carry):
        r0 = pl.multiple_of(c * rows, rows)
        c0 = pl.multiple_of(c * (cb * WINDOW), cb * WINDOW)
        zpad = jnp.zeros((pad_keys, KV_DIM), jnp.float32)
        kc = jnp.concatenate([ck_ref[pl.ds(c0, cb * WINDOW), :], k_s[pl.ds(r0, rows), :], zpad], axis=0)
        vc = jnp.concatenate([cv_ref[pl.ds(c0, cb * WINDOW), :], v_s[pl.ds(r0, rows), :], zpad], axis=0)
        keys, keys_sw = kc.astype(jnp.bfloat16), _swap_halves(kc).astype(jnp.bfloat16)
        vals, vals_sw = vc.astype(jnp.bfloat16), _swap_halves(vc).astype(jnp.bfloat16)
        qb = q_s[pl.ds(r0, rows), :]
        outs = []
        for g in range(N_KV_HEADS):
            outs += _attend(qb, keys, keys_sw, vals, vals_sw, bias, sinks_ref, g)
        att_s[pl.ds(r0, rows), :] = _merge_heads(outs, lo)
        return carry

    lax.fori_loop(0, n // rows, chunk, 0)
    attn_n = att_s[...] * gna_ref[...]
    _mixer_tail(x, attn_n, conv_n, (md(2), md(3), md(4)), wout_ref, fnorm_ref, wr_ref, br_ref, tri_ref,
                cnt_s, x1_ref, h2_ref, ri_ref, rf_ref)
    cnt_out_ref[...] = cnt_s[...]


def _sample_mixer(x, mod, sinks, consts, ck, cv, e1, e2, cnt_in, dec):
    n_s = x.shape[0]
    ts = SAMPLE_TILE
    seqs = ts // dec
    full = lambda a: pl.BlockSpec(a.shape, lambda i, s: (0,) * a.ndim)
    rows = lambda width: pl.BlockSpec((ts, width), lambda i, s: (i, 0))
    in_specs = [
        rows(D_MODEL), rows(6 * D_MODEL),
        full(consts["anorm"]), full(consts["fnorm"]), full(consts["w_in"]),
        rows(LANES), rows(LANES), full(consts["bias_s"]),
        full(consts["conv_w"]), full(consts["gn_attn"]), full(consts["gn_conv"]), full(consts["w_out"]),
        full(consts["w_r"]), full(consts["b_r"]),
        pl.BlockSpec((ts, ts), lambda i, s: (0, 0)),
        pl.BlockSpec((seqs * WINDOW, KV_DIM), lambda i, s: (i, 0)),
        pl.BlockSpec((seqs * WINDOW, KV_DIM), lambda i, s: (i, 0)),
        rows(CONV_DIM), rows(CONV_DIM), full(cnt_in),
    ]
    args = [x, mod, consts["anorm"], consts["fnorm"], consts["w_in"], consts["cos_s"], consts["sin_s"],
            consts["bias_s"], consts["conv_w"], consts["gn_attn"], consts["gn_conv"], consts["w_out"],
            consts["w_r"], consts["b_r"], consts["tri"], ck, cv, e1, e2, cnt_in]
    nt = n_s // ts
    out_shape = [
        jax.ShapeDtypeStruct((n_s, D_MODEL), jnp.float32),
        jax.ShapeDtypeStruct((n_s, D_MODEL // 2), jnp.uint32),
        jax.ShapeDtypeStruct((nt, 8, ts), jnp.int32),
        jax.ShapeDtypeStruct((nt, 8, ts), jnp.float32),
        jax.ShapeDtypeStruct((N_EXPERTS, LANES), jnp.float32),
        jax.ShapeDtypeStruct((n_s, KV_DIM), jnp.float32),
        jax.ShapeDtypeStruct((n_s, KV_DIM), jnp.float32),
        jax.ShapeDtypeStruct((n_s, CONV_DIM), jnp.float32),
    ]
    out_specs = [
        rows(D_MODEL),
        rows(D_MODEL // 2),
        pl.BlockSpec((None, 8, ts), lambda i, s: (i, 0, 0)),
        pl.BlockSpec((None, 8, ts), lambda i, s: (i, 0, 0)),
        pl.BlockSpec((N_EXPERTS, LANES), lambda i, s: (0, 0)),
        rows(KV_DIM), rows(KV_DIM), rows(CONV_DIM),
    ]
    scratch = [
        pltpu.VMEM((ts, ATTN_DIM), jnp.bfloat16),
        pltpu.VMEM((ts, KV_DIM), jnp.float32), pltpu.VMEM((ts, KV_DIM), jnp.float32),
        pltpu.VMEM((ts, ATTN_DIM), jnp.float32),
        pltpu.VMEM((N_EXPERTS, LANES), jnp.float32),
    ]
    return pl.pallas_call(
        functools.partial(_sample_mixer_kernel, dec=dec),
        grid_spec=pltpu.PrefetchScalarGridSpec(
            num_scalar_prefetch=1, grid=(nt,), in_specs=in_specs, out_specs=out_specs, scratch_shapes=scratch),
        out_shape=out_shape,
        compiler_params=pltpu.CompilerParams(dimension_semantics=("arbitrary",), vmem_limit_bytes=VMEM_LIMIT),
        name="sample_mixer",
    )(sinks, *args)


def _dispatch(idx_p, idx_s, h2_p, h2_s, n_rows):
    mesh = plsc.VectorSubcoreMesh(core_axis_name="core", subcore_axis_name="subcore")
    rows, window = SC_ROWS, SC_INDEX_WINDOW
    sub = window // rows

    width = h2_p.shape[1]

    @pl.kernel(out_type=jax.ShapeDtypeStruct((n_rows, width), h2_p.dtype), mesh=mesh, scratch_types=[])
    def scatter(hp_hbm, hs_hbm, ip_hbm, is_hbm, xs_hbm):
        def run(h_hbm, i_hbm):
            n_tok = h_hbm.shape[0]

            def body(indices, h_vmem, i_vmem):
                pltpu.sync_copy(h_vmem, xs_hbm.at[i_vmem.at[0, pl.ds(indices[1] * rows, rows)]])

            pltpu.emit_pipeline(
                body, grid=(2 * n_tok // window, sub),
                in_specs=[pl.BlockSpec((rows, width), lambda i, j: ((i * sub + j) % (n_tok // rows), 0)),
                          pl.BlockSpec((1, window), lambda i, j: (0, i))],
                out_specs=[], core_axis_name=("core", "subcore"),
                dimension_semantics=(pltpu.PARALLEL, pltpu.ARBITRARY), _explicit_indices=True,
            )(h_hbm, i_hbm)

        run(hp_hbm, ip_hbm)
        run(hs_hbm, is_hbm)

    return scatter(h2_p, h2_s, idx_p, idx_s)


def _experts_kernel(be_ref, nu_ref, nv_ref, nxt_ref, par_ref, xs_ref, wg_ref, wu_ref, wd_ref, y_ref,
                    wg_s, wu_s, wd_s, wg_f, wu_f, wd_f, sem):
    j = pl.program_id(0)
    used = j < nu_ref[0]
    changed = jnp.logical_or(j == 0, be_ref[j] != be_ref[jnp.maximum(j - 1, 0)])

    def weight_copies(e, slot):
        return [pltpu.make_async_copy(src.at[e], dst.at[slot], sem.at[slot])
                for src, dst in ((wg_ref, wg_f), (wu_ref, wu_f), (wd_ref, wd_f))]

    @pl.when(jnp.logical_and(used, changed))
    def _():
        slot = par_ref[j]

        @pl.when(j == 0)
        def _():
            for c in weight_copies(be_ref[0], slot):
                c.start()

        for c in weight_copies(be_ref[j], slot):
            c.wait()

        @pl.when(nxt_ref[j] >= 0)
        def _():
            for c in weight_copies(nxt_ref[j], 1 - slot):
                c.start()

        wg_s[...] = wg_f[slot].astype(jnp.bfloat16)
        wu_s[...] = wu_f[slot].astype(jnp.bfloat16)
        wd_s[...] = wd_f[slot].astype(jnp.bfloat16)

    @pl.when(used)
    def _():
        live = lax.broadcasted_iota(jnp.int32, (xs_ref.shape[0], 1), 0) < nv_ref[j]
        x = _unpack_bf16_pairs(jnp.where(live, xs_ref[...], jnp.uint32(0)))
        g = jnp.dot(x, wg_s[...], preferred_element_type=jnp.float32)
        u = jnp.dot(x, wu_s[...], preferred_element_type=jnp.float32)
        a = (g * jax.nn.sigmoid(g) * u).astype(jnp.bfloat16)
        y_ref[...] = jnp.dot(a, wd_s[...], preferred_element_type=jnp.float32).reshape(y_ref.shape)

    @pl.when(jnp.logical_not(used))
    def _():
        y_ref[...] = jnp.zeros_like(y_ref)


def _experts(block_e, n_used, n_valid, next_e, parity, xs, w_gate, w_up, w_down):
    bm = EXPERT_BLOCK
    n_blocks = xs.shape[0] // bm
    hbm = pl.BlockSpec(memory_space=pl.ANY)
    return pl.pallas_call(
        _experts_kernel,
        grid_spec=pltpu.PrefetchScalarGridSpec(
            num_scalar_prefetch=5, grid=(n_blocks,),
            in_specs=[pl.BlockSpec((bm, D_MODEL // 2), lambda j, be, nu, *_: (jnp.minimum(j, nu[0] - 1), 0)),
                      hbm, hbm, hbm],
            out_specs=pl.BlockSpec((bm, 1, D_MODEL), lambda j, *_: (j, 0, 0)),
            scratch_shapes=[pltpu.VMEM((D_MODEL, EXPERT_HIDDEN), jnp.bfloat16),
                            pltpu.VMEM((D_MODEL, EXPERT_HIDDEN), jnp.bfloat16),
                            pltpu.VMEM((EXPERT_HIDDEN, D_MODEL), jnp.bfloat16),
                            pltpu.VMEM((2, D_MODEL, EXPERT_HIDDEN), jnp.float32),
                            pltpu.VMEM((2, D_MODEL, EXPERT_HIDDEN), jnp.float32),
                            pltpu.VMEM((2, EXPERT_HIDDEN, D_MODEL), jnp.float32),
                            pltpu.SemaphoreType.DMA((2,))]),
        out_shape=jax.ShapeDtypeStruct((xs.shape[0], 1, D_MODEL), jnp.float32),
        compiler_params=pltpu.CompilerParams(dimension_semantics=("arbitrary",), vmem_limit_bytes=VMEM_LIMIT),
        name="experts",
    )(block_e, n_used, n_valid, next_e, parity, xs, w_gate, w_up, w_down)


def _combine_kernel(dest_ref, x1_ref, gate_ref, rf_ref, fin_ref, yb_ref, o_ref, ybuf, y_s, sem, *,
                    n_tok, per_token_gate):
    i = pl.program_id(0)
    n = pl.num_programs(0)
    tc = COMBINE_TILE

    def issue(step, slot):
        def one(r, carry):
            tok = step * tc + r
            for k in range(2):
                d = dest_ref[k * n_tok + tok]
                pltpu.make_async_copy(yb_ref.at[pl.ds(d, 1)], ybuf.at[slot, k, pl.ds(r, 1)], sem.at[slot]).start()
            return carry
        lax.fori_loop(0, tc, one, 0, unroll=8)

    slot = i % 2

    @pl.when(i == 0)
    def _():
        issue(0, 0)

    @pl.when(i + 1 < n)
    def _():
        issue(i + 1, 1 - slot)

    pltpu.make_async_copy(yb_ref.at[pl.ds(0, tc)], ybuf.at[slot, 0], sem.at[slot]).wait()
    pltpu.make_async_copy(yb_ref.at[pl.ds(0, tc)], ybuf.at[slot, 1], sem.at[slot]).wait()

    gates = jnp.transpose(rf_ref[...])
    y_s[0] = ybuf[slot, 0].reshape(tc, D_MODEL)
    y_s[1] = ybuf[slot, 1].reshape(tc, D_MODEL)
    y0 = y_s[0]
    y1 = y_s[1]
    moe = y0 * gates[:, 0:1] + y1 * gates[:, 1:2]
    gate_f = gate_ref[...] if per_token_gate else gate_ref[...][5:6]
    o_ref[...] = _rms(x1_ref[...] + gate_f * moe, fin_ref[...])


def _combine(dest_flat, x1, gate, rf, final_norm, yb, *, tiles_per_gate):
    n_tok = x1.shape[0]
    tc = COMBINE_TILE
    per_token = tiles_per_gate == 0
    if per_token:
        gate_spec = pl.BlockSpec((tc, D_MODEL), lambda i, d: (i, 0))
    else:
        gate_spec = pl.BlockSpec((None, 6, D_MODEL), lambda i, d: (i // tiles_per_gate, 0, 0))
    return pl.pallas_call(
        functools.partial(_combine_kernel, n_tok=n_tok, per_token_gate=per_token),
        grid_spec=pltpu.PrefetchScalarGridSpec(
            num_scalar_prefetch=1, grid=(n_tok // tc,),
            in_specs=[pl.BlockSpec((tc, D_MODEL), lambda i, d: (i, 0)),
                      gate_spec,
                      pl.BlockSpec((8, tc), lambda i, d: (0, i)),
                      pl.BlockSpec((1, D_MODEL), lambda i, d: (0, 0)),
                      pl.BlockSpec(memory_space=pl.ANY)],
            out_specs=pl.BlockSpec((tc, D_MODEL), lambda i, d: (i, 0)),
            scratch_shapes=[pltpu.VMEM((2, 2, tc, 1, D_MODEL), jnp.float32),
                            pltpu.VMEM((2, tc, D_MODEL), jnp.float32), pltpu.SemaphoreType.DMA((2,))]),
        out_shape=jax.ShapeDtypeStruct((n_tok, D_MODEL), jnp.float32),
        compiler_params=pltpu.CompilerParams(dimension_semantics=("arbitrary",), vmem_limit_bytes=VMEM_LIMIT),
        name="combine",
    )(dest_flat, x1, gate, rf, final_norm, yb)


def _rope_tables(pos):
    half = HEAD_DIM // 2
    inv_freq = np.power(np.float32(ROPE_THETA), -(np.arange(half, dtype=np.float32) / np.float32(half)))
    ang = (pos.astype(np.float32)[:, None] * inv_freq[None, :].astype(np.float32)).astype(np.float64)
    cos, sin = np.cos(ang).astype(np.float32), np.sin(ang).astype(np.float32)
    cos = np.concatenate([cos, cos, cos, cos], axis=1)
    sin = np.concatenate([-sin, sin, -sin, sin], axis=1)
    return cos, sin


def _prompt_bias():
    r = np.arange(WINDOW)[:, None]
    j = np.arange(2 * WINDOW)[None, :]
    band = (j > r) & (j <= r + WINDOW)
    later = np.where(band, 0.0, NEG).astype(np.float32)
    first = np.where(band & (j >= WINDOW), 0.0, NEG).astype(np.float32)
    return jnp.asarray(np.stack([first, later]))


def _sample_bias(dec):
    cb = SAMPLE_CHUNK
    qb = np.repeat(np.arange(cb), dec)[:, None]
    qi = np.tile(np.arange(dec), cb)[:, None]
    kb = np.concatenate([np.repeat(np.arange(cb), WINDOW), np.repeat(np.arange(cb), dec)])[None, :]
    kj = np.concatenate([np.tile(np.arange(WINDOW), cb), np.tile(np.arange(dec), cb)])[None, :]
    is_new = (np.arange(cb * WINDOW + cb * dec) >= cb * WINDOW)[None, :]
    ok = (kb == qb) & np.where(is_new, kj <= qi, kj > qi)
    ok = np.pad(ok, ((0, 0), (0, SAMPLE_KEYS - ok.shape[1])))
    return jnp.asarray(np.where(ok, 0.0, NEG).astype(np.float32))


def kernel(x_prompt, x_sample, cache_k, cache_v, state_conv, c_prompt, c_sample, attn_norm, ffn_norm, w_mod, b_mod, w_in, conv_w, attn_sinks, out_norm_attn, out_norm_conv, w_out, w_group, b_group, w_expert, b_expert, w_gate, w_up, w_down, final_norm):
    assert attn_norm.shape[0] == 1, "one layer"
    bsz, seq, _ = x_prompt.shape
    nseq, dec, _ = x_sample.shape
    m_prompt = bsz * seq
    n_s = nseq * dec
    m_total = m_prompt + n_s
    tm = MIXER_TILE
    assert seq % tm == 0 and n_s % SAMPLE_TILE == 0 and SAMPLE_TILE % (SAMPLE_CHUNK * dec) == 0
    assert cache_k.shape[2] == WINDOW and m_prompt % SAMPLE_TILE == 0 and m_total % COMBINE_TILE == 0
    bf = jnp.bfloat16

    w_r = jnp.zeros((ROUTER_ROWS, D_MODEL), jnp.float32)
    w_r = w_r.at[0:N_GROUPS].set(w_group[0].T).at[8:8 + N_EXPERTS].set(w_expert[0].T).astype(bf)
    b_r = jnp.zeros((ROUTER_ROWS,), jnp.float32).at[0:N_GROUPS].set(b_group[0]).at[8:8 + N_EXPERTS].set(b_expert[0])
    b_r = jnp.broadcast_to(b_r[:, None], (ROUTER_ROWS, LANES))
    cos_p, sin_p = _rope_tables(np.arange(seq))
    cos_s, sin_s = _rope_tables(PAST_LEN + np.arange(dec))
    tri = jnp.asarray(np.triu(np.ones((tm, tm), np.float32), 1)).astype(bf)
    consts = dict(
        anorm=attn_norm[0][None], fnorm=ffn_norm[0][None], w_in=w_in[0].astype(bf),
        cos_p=jnp.asarray(cos_p), sin_p=jnp.asarray(sin_p),
        cos_s=jnp.asarray(np.tile(cos_s, (nseq, 1))), sin_s=jnp.asarray(np.tile(sin_s, (nseq, 1))),
        bias_p=_prompt_bias(), bias_s=_sample_bias(dec), conv_w=conv_w[0],
        gn_attn=out_norm_attn[0][None], gn_conv=out_norm_conv[0][None], w_out=w_out[0].astype(bf),
        w_r=w_r, b_r=b_r, tri=tri)
    sinks = attn_sinks[0]

    mod = _adaln(jnp.concatenate([c_prompt, c_sample], axis=0), w_mod[0], b_mod[0])
    mod_p = mod[:bsz].reshape(bsz, 6, D_MODEL)
    mod_s = jnp.repeat(mod[bsz:], dec, axis=0)

    x1_p, h2_p, ri_p, rf_p, cnt_p, k_p, v_p, u_p = _prompt_mixer(x_prompt, mod_p, sinks, consts)
    st = state_conv[0]
    zrow = jnp.zeros((nseq, 1, CONV_DIM), jnp.float32)
    e1 = jnp.concatenate([st[:, 1:2]] + [zrow] * (dec - 1), axis=1).reshape(n_s, CONV_DIM)
    e2 = jnp.concatenate([st[:, 0:1], st[:, 1:2]] + [zrow] * (dec - 2), axis=1).reshape(n_s, CONV_DIM)
    ck = cache_k[0].reshape(nseq * WINDOW, KV_DIM)
    cv = cache_v[0].reshape(nseq * WINDOW, KV_DIM)
    x1_s, h2_s, ri_s, rf_s, cnt, k_s, v_s, u_s = _sample_mixer(
        x_sample.reshape(n_s, D_MODEL), mod_s, sinks, consts, ck, cv, e1, e2, cnt_p, dec)

    bm = EXPERT_BLOCK
    counts = cnt[:, 0].astype(jnp.int32)
    padded = (counts + bm - 1) // bm * bm
    pend = jnp.cumsum(padded)
    pstart = pend - padded
    n_blocks = -(-(2 * m_total + N_EXPERTS * (bm - 1)) // bm)
    n_used = (pend[-1] // bm).astype(jnp.int32)
    blk = jnp.arange(n_blocks, dtype=jnp.int32)
    eids = jnp.arange(N_EXPERTS, dtype=jnp.int32)
    block_e = jnp.sum((pend[None, :] <= (blk * bm)[:, None]).astype(jnp.int32), axis=1)
    last_e = jnp.max(jnp.where(padded > 0, eids, 0))
    block_e = jnp.where(blk < n_used, jnp.minimum(block_e, N_EXPERTS - 1), last_e)
    by_token = lambda a: jnp.transpose(a, (1, 0, 2)).reshape(8, -1)
    ri = jnp.concatenate([by_token(ri_p), by_token(ri_s)], axis=1)
    rf_flat = jnp.concatenate([by_token(rf_p), by_token(rf_s)], axis=1)
    start_of = jnp.sum(jnp.where(ri[0:2][..., None] == eids, pstart, 0), axis=-1)
    dest_flat = (start_of + ri[2:4]).reshape(2 * m_total)

    of_block = lambda table: jnp.sum(jnp.where(block_e[:, None] == eids, table, 0), axis=1)
    n_valid = jnp.clip(of_block(counts) - (blk * bm - of_block(pstart)), 0, bm)
    n_valid = jnp.where(blk < n_used, n_valid, 0).astype(jnp.int32)
    present = padded > 0
    later = jnp.logical_and(present[None, :], eids[None, :] > eids[:, None])
    next_of = jnp.min(jnp.where(later, eids[None, :], N_EXPERTS), axis=1)
    next_of = jnp.where(next_of == N_EXPERTS, -1, next_of)
    next_e = of_block(next_of).astype(jnp.int32)
    parity = of_block((jnp.cumsum(present.astype(jnp.int32)) - 1) % 2).astype(jnp.int32)

    dest_p = jnp.concatenate([dest_flat[:m_prompt], dest_flat[m_total:m_total + m_prompt]])
    dest_s = jnp.concatenate([dest_flat[m_prompt:m_total], dest_flat[m_total + m_prompt:]])
    xs = _dispatch(dest_p[None], dest_s[None], h2_p, h2_s, n_blocks * bm)
    yb = _experts(block_e, n_used.reshape(1), n_valid, next_e, parity, xs, w_gate[0], w_up[0], w_down[0])

    fin = final_norm[None]
    y_p = _combine(dest_p, x1_p, mod_p, rf_flat[:, :m_prompt], fin, yb, tiles_per_gate=seq // COMBINE_TILE)
    gate_s = mod_s[:, 5 * D_MODEL:]
    y_s = _combine(dest_s, x1_s, gate_s, rf_flat[:, m_prompt:], fin, yb, tiles_per_gate=0)

    keep = WINDOW
    new_k_p = k_p.reshape(1, bsz, keep, N_KV_HEADS, HEAD_DIM)
    new_v_p = v_p.reshape(1, bsz, keep, N_KV_HEADS, HEAD_DIM)
    new_u_p = u_p[:, 6:8][None]
    ksn = k_s.reshape(nseq, dec, N_KV_HEADS, HEAD_DIM)
    vsn = v_s.reshape(nseq, dec, N_KV_HEADS, HEAD_DIM)
    new_k_s = jnp.concatenate([cache_k[0][:, dec:], ksn], axis=1)[None]
    new_v_s = jnp.concatenate([cache_v[0][:, dec:], vsn], axis=1)[None]
    u_all = jnp.concatenate([st, u_s.reshape(nseq, dec, CONV_DIM)], axis=1)
    new_u_s = u_all[:, -2:][None]
    return (y_p.reshape(bsz, seq, D_MODEL), y_s.reshape(nseq, dec, D_MODEL),
            new_k_p, new_v_p, new_u_p, new_k_s, new_v_s, new_u_s)
```

```python
import functools

import jax
import jax.numpy as jnp
import numpy as np
from jax import lax
from jax.experimental import pallas as pl
from jax.experimental.pallas import tpu as pltpu
from jax.experimental.pallas import tpu_sc as plsc

D_MODEL = 1024
HEAD_DIM = 64
ATTN_DIM = 512
N_Q_HEADS = 8
N_KV_HEADS = 2
KV_DIM = 128
CONV_DIM = 512
WINDOW = 128
IN_PROJ_DIM = ATTN_DIM + 2 * KV_DIM + 3 * CONV_DIM
N_GROUPS = 4
EXPERTS_PER_GROUP = 8
N_EXPERTS = 32
EXPERT_HIDDEN = 512
RMS_EPS = 1e-5
ROPE_THETA = 10000.0
PAST_LEN = 16384

LANES = 128
ROUTER_ROWS = 48
NEG = -1e30

MIXER_TILE = 512
SAMPLE_TILE = 128
SAMPLE_CHUNK = 8
SAMPLE_KEYS = 1152
EXPERT_BLOCK = 512
COMBINE_TILE = 256
SC_ROWS = 64
SC_INDEX_WINDOW = 128
VMEM_LIMIT = 56 * 1024 * 1024


def _rms(x, gain):
    ms = jnp.mean(x * x, axis=-1, keepdims=True)
    return x * lax.rsqrt(ms + RMS_EPS) * gain


def _half_lane_mask(rows=1):
    return lax.broadcasted_iota(jnp.int32, (rows, LANES), 1) < HEAD_DIM


def _rope(x, cos, sin_signed):
    t, w = x.shape
    reps = w // LANES
    lane = lax.broadcasted_iota(jnp.int32, (t, w), 1)
    upper = (lane % HEAD_DIM) >= (HEAD_DIM // 2)
    partner = jnp.where(upper, pltpu.roll(x, HEAD_DIM // 2, 1), pltpu.roll(x, w - HEAD_DIM // 2, 1))
    if reps > 1:
        cos = jnp.concatenate([cos] * reps, axis=1)
        sin_signed = jnp.concatenate([sin_signed] * reps, axis=1)
    return x * cos + partner * sin_signed


def _swap_halves(x):
    return pltpu.roll(x, HEAD_DIM, 1)


def _group_norm64(x, gain):
    t, w = x.shape
    lo = _half_lane_mask(t)
    outs = []
    for c in range(w // LANES):
        xc = x[:, c * LANES:(c + 1) * LANES]
        sq = xc * xc
        s_lo = jnp.sum(jnp.where(lo, sq, 0.0), axis=-1, keepdims=True)
        s_hi = jnp.sum(jnp.where(lo, 0.0, sq), axis=-1, keepdims=True)
        r = jnp.where(lo, lax.rsqrt(s_lo * (1.0 / HEAD_DIM) + RMS_EPS), lax.rsqrt(s_hi * (1.0 / HEAD_DIM) + RMS_EPS))
        outs.append(xc * r)
    return jnp.concatenate(outs, axis=1) * gain


def _attend(q_rows, keys, keys_sw, vals, vals_sw, bias, sinks_ref, group):
    r = q_rows.shape[0]
    lo = _half_lane_mask(r)
    outs = []
    for parity in range(2):
        heads = [4 * group + parity, 4 * group + parity + 2]
        kk, vv = (keys, vals) if parity == group else (keys_sw, vals_sw)
        half = lo if parity == 0 else jnp.logical_not(lo)
        qs = []
        for h in heads:
            pair = q_rows[:, (h // 2) * LANES:(h // 2 + 1) * LANES]
            qs.append(jnp.where(half, pair, jnp.zeros_like(pair)))
        qcat = jnp.concatenate(qs, axis=0)
        s = lax.dot_general(qcat, kk, (((1,), (1,)), ((), ())), preferred_element_type=jnp.float32)
        ps, dens = [], []
        for i, h in enumerate(heads):
            sh = s[i * r:(i + 1) * r] + bias
            sink = sinks_ref[h]
            m = jnp.maximum(jnp.max(sh, axis=-1, keepdims=True), sink)
            p = jnp.exp(sh - m)
            dens.append(jnp.sum(p, axis=-1, keepdims=True) + jnp.exp(sink - m))
            ps.append(p.astype(jnp.bfloat16))
        o = jnp.dot(jnp.concatenate(ps, axis=0), vv, preferred_element_type=jnp.float32)
        for i, h in enumerate(heads):
            oh = o[i * r:(i + 1) * r] / dens[i]
            ms = jnp.sum(jnp.where(half, oh * oh, 0.0), axis=-1, keepdims=True) * (1.0 / HEAD_DIM)
            outs.append((h, oh * lax.rsqrt(ms + RMS_EPS)))
    outs.sort(key=lambda t: t[0])
    return [o for _, o in outs]


def _merge_heads(head_outs, lo):
    pairs = [jnp.where(lo, head_outs[2 * j], head_outs[2 * j + 1]) for j in range(4)]
    return jnp.concatenate(pairs, axis=1)


def _route(h2b, wr_ref, br_ref, tri_ref, cnt_ref, ri_ref, rf_ref):
    t = h2b.shape[0]
    lg = lax.dot_general(wr_ref[...], h2b, (((1,), (1,)), ((), ())), preferred_element_type=jnp.float32)
    lg = lg + br_ref[...][:, 0:1]
    r8 = lax.broadcasted_iota(jnp.int32, (8, t), 0)
    lgrp = jnp.where(r8 < N_GROUPS, lg[0:8], -jnp.inf)
    gmax = jnp.max(lgrp, axis=0, keepdims=True)
    grp = jnp.min(jnp.where(lgrp == gmax, r8, 8), axis=0, keepdims=True)
    pg_sel = 1.0 / jnp.sum(jnp.exp(lgrp - gmax), axis=0, keepdims=True)
    r32 = lax.broadcasted_iota(jnp.int32, (N_EXPERTS, t), 0)
    le = jnp.where((r32 // EXPERTS_PER_GROUP) == grp, lg[8:8 + N_EXPERTS], -jnp.inf)
    v1 = jnp.max(le, axis=0, keepdims=True)
    i1 = jnp.min(jnp.where(le == v1, r32, N_EXPERTS), axis=0, keepdims=True)
    le2 = jnp.where(r32 == i1, -jnp.inf, le)
    v2 = jnp.max(le2, axis=0, keepdims=True)
    i2 = jnp.min(jnp.where(le2 == v2, r32, N_EXPERTS), axis=0, keepdims=True)
    e21 = jnp.exp(v2 - v1)
    gate1 = pg_sel / (1.0 + e21)
    gate2 = pg_sel * e21 / (1.0 + e21)
    hot1 = r32 == i1
    hot2 = r32 == i2
    onehot = jnp.where(jnp.logical_or(hot1, hot2), 1.0, 0.0)
    before = jnp.dot(onehot.astype(jnp.bfloat16), tri_ref[...], preferred_element_type=jnp.float32)
    pos = before + cnt_ref[...][:, 0:1]
    rank1 = jnp.sum(jnp.where(hot1, pos, 0.0), axis=0, keepdims=True)
    rank2 = jnp.sum(jnp.where(hot2, pos, 0.0), axis=0, keepdims=True)
    cnt_ref[...] = cnt_ref[...] + jnp.sum(onehot, axis=1, keepdims=True)
    zi = jnp.zeros((4, t), jnp.int32)
    ri_ref[...] = jnp.concatenate([i1, i2, rank1.astype(jnp.int32), rank2.astype(jnp.int32), zi], axis=0)
    rf_ref[...] = jnp.concatenate([gate1, gate2, jnp.zeros((6, t), jnp.float32)], axis=0)


def _pack_bf16_pairs(xb):
    w = xb.shape[1] // 2
    bits = pltpu.bitcast(xb.astype(jnp.float32), jnp.uint32)
    return (bits[:, :w] >> 16) | (bits[:, w:] & jnp.uint32(0xFFFF0000))


def _unpack_bf16_pairs(words):
    lo = pltpu.bitcast(words << 16, jnp.float32)
    hi = pltpu.bitcast(words & jnp.uint32(0xFFFF0000), jnp.float32)
    return jnp.concatenate([lo.astype(jnp.bfloat16), hi.astype(jnp.bfloat16)], axis=1)


def _split_proj(proj):
    a = ATTN_DIM
    q = proj[:, :a]
    k = proj[:, a:a + KV_DIM]
    v = proj[:, a + KV_DIM:a + 2 * KV_DIM]
    c0 = a + 2 * KV_DIM
    h_conv = proj[:, c0:c0 + CONV_DIM]
    gate_b = proj[:, c0 + CONV_DIM:c0 + 2 * CONV_DIM]
    gate_c = proj[:, c0 + 2 * CONV_DIM:c0 + 3 * CONV_DIM]
    return q, k, v, gate_c * h_conv, gate_b


def _mixer_tail(x, attn_n, conv_n, mod, wout_ref, fnorm_ref, wr_ref, br_ref, tri_ref, cnt_ref,
                x1_ref, h2_ref, ri_ref, rf_ref):
    gate_a, shift_f, scale_f = mod
    cat = jnp.concatenate([attn_n.astype(jnp.bfloat16), conv_n.astype(jnp.bfloat16)], axis=1)
    mix = jnp.dot(cat, wout_ref[...], preferred_element_type=jnp.float32)
    x1 = x + gate_a * mix
    x1_ref[...] = x1
    h2 = _rms(x1, fnorm_ref[...] * (1.0 + scale_f)) + shift_f
    h2b = h2.astype(jnp.bfloat16)
    h2_ref[...] = _pack_bf16_pairs(h2b)
    _route(h2b, wr_ref, br_ref, tri_ref, cnt_ref, ri_ref, rf_ref)


def _adaln_kernel(c_ref, w_ref, b_ref, o_ref):
    c = c_ref[...]
    a = (c * jax.nn.sigmoid(c)).astype(jnp.bfloat16)
    o_ref[...] = jnp.dot(a, w_ref[...].astype(jnp.bfloat16), preferred_element_type=jnp.float32) + b_ref[...]


def _adaln(c, w_mod, b_mod):
    n = c.shape[0]
    tn = 1536
    return pl.pallas_call(
        _adaln_kernel,
        grid=(w_mod.shape[1] // tn,),
        in_specs=[pl.BlockSpec((n, D_MODEL), lambda j: (0, 0)),
                  pl.BlockSpec((D_MODEL, tn), lambda j: (0, j)),
                  pl.BlockSpec((1, tn), lambda j: (0, j))],
        out_specs=pl.BlockSpec((n, tn), lambda j: (0, j)),
        out_shape=jax.ShapeDtypeStruct((n, w_mod.shape[1]), jnp.float32),
        compiler_params=pltpu.CompilerParams(dimension_semantics=("arbitrary",), vmem_limit_bytes=VMEM_LIMIT),
        name="adaln",
    )(c, w_mod, b_mod.reshape(1, -1))


def _prompt_mixer_kernel(sinks_ref, x_ref, mod_ref, anorm_ref, fnorm_ref, win_ref, cos_ref, sin_ref, bias_ref,
                         convw_ref, gna_ref, gnc_ref, wout_ref, wr_ref, br_ref, tri_ref,
                         x1_ref, h2_ref, ri_ref, rf_ref, cnt_out_ref, knew_ref, vnew_ref, unew_ref,
                         q_s, k_s, ksw_s, v_s, vsw_s, u_s, att_s, cnt_s):
    b = pl.program_id(0)
    t = pl.program_id(1)
    tm = MIXER_TILE
    nblk = tm // WINDOW

    @pl.when(jnp.logical_and(b == 0, t == 0))
    def _():
        cnt_s[...] = jnp.zeros_like(cnt_s)

    @pl.when(t == 0)
    def _():
        z = jnp.zeros((WINDOW, KV_DIM), jnp.bfloat16)
        k_s[0:WINDOW, :] = z
        ksw_s[0:WINDOW, :] = z
        v_s[0:WINDOW, :] = z
        vsw_s[0:WINDOW, :] = z
        u_s[0:8, :] = jnp.zeros((8, CONV_DIM), jnp.float32)

    x = x_ref[...]
    mod = mod_ref[...]
    h = _rms(x, anorm_ref[...] * (1.0 + mod[1:2])) + mod[0:1]
    proj = jnp.dot(h.astype(jnp.bfloat16), win_ref[...], preferred_element_type=jnp.float32)
    q, k, v, u, gate_b = _split_proj(proj)
    cos = cos_ref[...]
    sin = sin_ref[...]
    q = _rope(q, cos, sin) * (HEAD_DIM ** -0.5)
    k = _rope(k, cos, sin)
    q_s[...] = q.astype(jnp.bfloat16)
    k_s[WINDOW:, :] = k.astype(jnp.bfloat16)
    ksw_s[WINDOW:, :] = _swap_halves(k).astype(jnp.bfloat16)
    v_s[WINDOW:, :] = v.astype(jnp.bfloat16)
    vsw_s[WINDOW:, :] = _swap_halves(v).astype(jnp.bfloat16)
    knew_ref[...] = k[tm - WINDOW:, :]
    vnew_ref[...] = v[tm - WINDOW:, :]

    u_s[8:, :] = u
    unew_ref[...] = u[tm - 8:, :]
    cw = convw_ref[...]
    conv = (u_s[6:6 + tm, :] * cw[0:1] + u_s[7:7 + tm, :] * cw[1:2] + u * cw[2:3]) * gate_b
    u_s[0:8, :] = u[tm - 8:, :]
    conv_n = _group_norm64(conv, gnc_ref[...])

    lo = _half_lane_mask(WINDOW)
    first = t == 0

    def block(i, carry):
        r0 = pl.multiple_of(i * WINDOW, WINDOW)
        bias = bias_ref[jnp.where(jnp.logical_and(first, i == 0), 0, 1)]
        qb = q_s[pl.ds(r0, WINDOW), :]
        keys = k_s[pl.ds(r0, 2 * WINDOW), :]
        keys_sw = ksw_s[pl.ds(r0, 2 * WINDOW), :]
        vals = v_s[pl.ds(r0, 2 * WINDOW), :]
        vals_sw = vsw_s[pl.ds(r0, 2 * WINDOW), :]
        outs = []
        for g in range(N_KV_HEADS):
            outs += _attend(qb, keys, keys_sw, vals, vals_sw, bias, sinks_ref, g)
        att_s[pl.ds(r0, WINDOW), :] = _merge_heads(outs, lo)
        return carry

    lax.fori_loop(0, nblk, block, 0, unroll=True)
    k_s[0:WINDOW, :] = k_s[tm:tm + WINDOW, :]
    ksw_s[0:WINDOW, :] = ksw_s[tm:tm + WINDOW, :]
    v_s[0:WINDOW, :] = v_s[tm:tm + WINDOW, :]
    vsw_s[0:WINDOW, :] = vsw_s[tm:tm + WINDOW, :]

    attn_n = att_s[...] * gna_ref[...]
    _mixer_tail(x, attn_n, conv_n, (mod[2:3], mod[3:4], mod[4:5]), wout_ref, fnorm_ref, wr_ref, br_ref, tri_ref,
                cnt_s, x1_ref, h2_ref, ri_ref, rf_ref)
    cnt_out_ref[...] = cnt_s[...]


def _prompt_mixer(x, mod, sinks, consts):
    bsz, seq, _ = x.shape
    tm = MIXER_TILE
    nt = seq // tm
    full = lambda shape: pl.BlockSpec(shape, lambda b, t, s: (0,) * len(shape))
    in_specs = [
        pl.BlockSpec((None, tm, D_MODEL), lambda b, t, s: (b, t, 0)),
        pl.BlockSpec((None, 6, D_MODEL), lambda b, t, s: (b, 0, 0)),
        full((1, D_MODEL)), full((1, D_MODEL)),
        full((D_MODEL, IN_PROJ_DIM)),
        pl.BlockSpec((tm, LANES), lambda b, t, s: (t, 0)),
        pl.BlockSpec((tm, LANES), lambda b, t, s: (t, 0)),
        full((2, WINDOW, 2 * WINDOW)),
        full((3, CONV_DIM)), full((1, ATTN_DIM)), full((1, CONV_DIM)),
        full((D_MODEL, D_MODEL)),
        full((ROUTER_ROWS, D_MODEL)), full((ROUTER_ROWS, LANES)),
        full((tm, tm)),
    ]
    out_shape = [
        jax.ShapeDtypeStruct((bsz * seq, D_MODEL), jnp.float32),
        jax.ShapeDtypeStruct((bsz * seq, D_MODEL // 2), jnp.uint32),
        jax.ShapeDtypeStruct((bsz * nt, 8, tm), jnp.int32),
        jax.ShapeDtypeStruct((bsz * nt, 8, tm), jnp.float32),
        jax.ShapeDtypeStruct((N_EXPERTS, LANES), jnp.float32),
        jax.ShapeDtypeStruct((bsz, WINDOW, KV_DIM), jnp.float32),
        jax.ShapeDtypeStruct((bsz, WINDOW, KV_DIM), jnp.float32),
        jax.ShapeDtypeStruct((bsz, 8, CONV_DIM), jnp.float32),
    ]
    out_specs = [
        pl.BlockSpec((tm, D_MODEL), lambda b, t, s: (b * nt + t, 0)),
        pl.BlockSpec((tm, D_MODEL // 2), lambda b, t, s: (b * nt + t, 0)),
        pl.BlockSpec((None, 8, tm), lambda b, t, s: (b * nt + t, 0, 0)),
        pl.BlockSpec((None, 8, tm), lambda b, t, s: (b * nt + t, 0, 0)),
        pl.BlockSpec((N_EXPERTS, LANES), lambda b, t, s: (0, 0)),
        pl.BlockSpec((None, WINDOW, KV_DIM), lambda b, t, s: (b, 0, 0)),
        pl.BlockSpec((None, WINDOW, KV_DIM), lambda b, t, s: (b, 0, 0)),
        pl.BlockSpec((None, 8, CONV_DIM), lambda b, t, s: (b, 0, 0)),
    ]
    scratch = [
        pltpu.VMEM((tm, ATTN_DIM), jnp.bfloat16),
        pltpu.VMEM((tm + WINDOW, KV_DIM), jnp.bfloat16), pltpu.VMEM((tm + WINDOW, KV_DIM), jnp.bfloat16),
        pltpu.VMEM((tm + WINDOW, KV_DIM), jnp.bfloat16), pltpu.VMEM((tm + WINDOW, KV_DIM), jnp.bfloat16),
        pltpu.VMEM((tm + 8, CONV_DIM), jnp.float32),
        pltpu.VMEM((tm, ATTN_DIM), jnp.float32),
        pltpu.VMEM((N_EXPERTS, LANES), jnp.float32),
    ]
    return pl.pallas_call(
        _prompt_mixer_kernel,
        grid_spec=pltpu.PrefetchScalarGridSpec(
            num_scalar_prefetch=1, grid=(bsz, nt), in_specs=in_specs, out_specs=out_specs, scratch_shapes=scratch),
        out_shape=out_shape,
        compiler_params=pltpu.CompilerParams(dimension_semantics=("arbitrary", "arbitrary"),
                                             vmem_limit_bytes=VMEM_LIMIT),
        name="prompt_mixer",
    )(sinks, x, mod, consts["anorm"], consts["fnorm"], consts["w_in"], consts["cos_p"], consts["sin_p"],
      consts["bias_p"], consts["conv_w"], consts["gn_attn"], consts["gn_conv"], consts["w_out"],
      consts["w_r"], consts["b_r"], consts["tri"])


def _sample_mixer_kernel(sinks_ref, x_ref, mod_ref, anorm_ref, fnorm_ref, win_ref, cos_ref, sin_ref, bias_ref,
                         convw_ref, gna_ref, gnc_ref, wout_ref, wr_ref, br_ref, tri_ref,
                         ck_ref, cv_ref, e1_ref, e2_ref, cnt_in_ref,
                         x1_ref, h2_ref, ri_ref, rf_ref, cnt_out_ref, knew_ref, vnew_ref, unew_ref,
                         q_s, k_s, v_s, att_s, cnt_s, *, dec):
    n = x_ref.shape[0]
    cb = SAMPLE_CHUNK
    rows = cb * dec
    pad_keys = SAMPLE_KEYS - cb * WINDOW - rows

    @pl.when(pl.program_id(0) == 0)
    def _():
        cnt_s[...] = cnt_in_ref[...]

    x = x_ref[...]
    mod = mod_ref[...]
    md = lambda j: mod[:, j * D_MODEL:(j + 1) * D_MODEL]
    h = _rms(x, anorm_ref[...]) * (1.0 + md(1)) + md(0)
    proj = jnp.dot(h.astype(jnp.bfloat16), win_ref[...], preferred_element_type=jnp.float32)
    q, k, v, u, gate_b = _split_proj(proj)
    cos = cos_ref[...]
    sin = sin_ref[...]
    q = _rope(q, cos, sin) * (HEAD_DIM ** -0.5)
    k = _rope(k, cos, sin)
    q_s[...] = q.astype(jnp.bfloat16)
    k_s[...] = k
    v_s[...] = v
    knew_ref[...] = k
    vnew_ref[...] = v
    unew_ref[...] = u

    pos = lax.broadcasted_iota(jnp.int32, (n, CONV_DIM), 0) % dec
    um1 = jnp.where(pos >= 1, pltpu.roll(u, 1, 0), 0.0) + e1_ref[...]
    um2 = jnp.where(pos >= 2, pltpu.roll(u, 2, 0), 0.0) + e2_ref[...]
    cw = convw_ref[...]
    conv = (um2 * cw[0:1] + um1 * cw[1:2] + u * cw[2:3]) * gate_b
    conv_n = _group_norm64(conv, gnc_ref[...])

    lo = _half_lane_mask(rows)
    bias = bias_ref[...]

    def chunk(c, carry):
        r0 = pl.multiple_of(c * rows, rows)
        c0 = pl.multiple_of(c * (cb * WINDOW), cb * WINDOW)
        zpad = jnp.zeros((pad_keys, KV_DIM), jnp.float32)
        kc = jnp.concatenate([ck_ref[pl.ds(c0, cb * WINDOW), :], k_s[pl.ds(r0, rows), :], zpad], axis=0)
        vc = jnp.concatenate([cv_ref[pl.ds(c0, cb * WINDOW), :], v_s[pl.ds(r0, rows), :], zpad], axis=0)
        keys, keys_sw = kc.astype(jnp.bfloat16), _swap_halves(kc).astype(jnp.bfloat16)
        vals, vals_sw = vc.astype(jnp.bfloat16), _swap_halves(vc).astype(jnp.bfloat16)
        qb = q_s[pl.ds(r0, rows), :]
        outs = []
        for g in range(N_KV_HEADS):
            outs += _attend(qb, keys, keys_sw, vals, vals_sw, bias, sinks_ref, g)
        att_s[pl.ds(r0, rows), :] = _merge_heads(outs, lo)
        return carry

    lax.fori_loop(0, n // rows, chunk, 0)
    attn_n = att_s[...] * gna_ref[...]
    _mixer_tail(x, attn_n, conv_n, (md(2), md(3), md(4)), wout_ref, fnorm_ref, wr_ref, br_ref, tri_ref,
                cnt_s, x1_ref, h2_ref, ri_ref, rf_ref)
    cnt_out_ref[...] = cnt_s[...]


def _sample_mixer(x, mod, sinks, consts, ck, cv, e1, e2, cnt_in, dec):
    n_s = x.shape[0]
    ts = SAMPLE_TILE
    seqs = ts // dec
    full = lambda a: pl.BlockSpec(a.shape, lambda i, s: (0,) * a.ndim)
    rows = lambda width: pl.BlockSpec((ts, width), lambda i, s: (i, 0))
    in_specs = [
        rows(D_MODEL), rows(6 * D_MODEL),
        full(consts["anorm"]), full(consts["fnorm"]), full(consts["w_in"]),
        rows(LANES), rows(LANES), full(consts["bias_s"]),
        full(consts["conv_w"]), full(consts["gn_attn"]), full(consts["gn_conv"]), full(consts["w_out"]),
        full(consts["w_r"]), full(consts["b_r"]),
        pl.BlockSpec((ts, ts), lambda i, s: (0, 0)),
        pl.BlockSpec((seqs * WINDOW, KV_DIM), lambda i, s: (i, 0)),
        pl.BlockSpec((seqs * WINDOW, KV_DIM), lambda i, s: (i, 0)),
        rows(CONV_DIM), rows(CONV_DIM), full(cnt_in),
    ]
    args = [x, mod, consts["anorm"], consts["fnorm"], consts["w_in"], consts["cos_s"], consts["sin_s"],
            consts["bias_s"], consts["conv_w"], consts["gn_attn"], consts["gn_conv"], consts["w_out"],
            consts["w_r"], consts["b_r"], consts["tri"], ck, cv, e1, e2, cnt_in]
    nt = n_s // ts
    out_shape = [
        jax.ShapeDtypeStruct((n_s, D_MODEL), jnp.float32),
        jax.ShapeDtypeStruct((n_s, D_MODEL // 2), jnp.uint32),
        jax.ShapeDtypeStruct((nt, 8, ts), jnp.int32),
        jax.ShapeDtypeStruct((nt, 8, ts), jnp.float32),
        jax.ShapeDtypeStruct((N_EXPERTS, LANES), jnp.float32),
        jax.ShapeDtypeStruct((n_s, KV_DIM), jnp.float32),
        jax.ShapeDtypeStruct((n_s, KV_DIM), jnp.float32),
        jax.ShapeDtypeStruct((n_s, CONV_DIM), jnp.float32),
    ]
    out_specs = [
        rows(D_MODEL),
        rows(D_MODEL // 2),
        pl.BlockSpec((None, 8, ts), lambda i, s: (i, 0, 0)),
        pl.BlockSpec((None, 8, ts), lambda i, s: (i, 0, 0)),
        pl.BlockSpec((N_EXPERTS, LANES), lambda i, s: (0, 0)),
        rows(KV_DIM), rows(KV_DIM), rows(CONV_DIM),
    ]
    scratch = [
        pltpu.VMEM((ts, ATTN_DIM), jnp.bfloat16),
        pltpu.VMEM((ts, KV_DIM), jnp.float32), pltpu.VMEM((ts, KV_DIM), jnp.float32),
        pltpu.VMEM((ts, ATTN_DIM), jnp.float32),
        pltpu.VMEM((N_EXPERTS, LANES), jnp.float32),
    ]
    return pl.pallas_call(
        functools.partial(_sample_mixer_kernel, dec=dec),
        grid_spec=pltpu.PrefetchScalarGridSpec(
            num_scalar_prefetch=1, grid=(nt,), in_specs=in_specs, out_specs=out_specs, scratch_shapes=scratch),
        out_shape=out_shape,
        compiler_params=pltpu.CompilerParams(dimension_semantics=("arbitrary",), vmem_limit_bytes=VMEM_LIMIT),
        name="sample_mixer",
    )(sinks, *args)


def _dispatch(idx, h2, n_rows):
    mesh = plsc.VectorSubcoreMesh(core_axis_name="core", subcore_axis_name="subcore")
    rows, window = SC_ROWS, SC_INDEX_WINDOW
    sub = window // rows
    n_tok, width = h2.shape
    assert n_tok % rows == 0 and (2 * n_tok) % window == 0

    @pl.kernel(out_type=jax.ShapeDtypeStruct((n_rows, width), h2.dtype), mesh=mesh, scratch_types=[])
    def scatter(h_hbm, i_hbm, xs_hbm):
        def body(indices, h_vmem, i_vmem):
            pltpu.sync_copy(h_vmem, xs_hbm.at[i_vmem.at[0, pl.ds(indices[1] * rows, rows)]])

        pltpu.emit_pipeline(
            body, grid=(2 * n_tok // window, sub),
            in_specs=[pl.BlockSpec((rows, width), lambda i, j: ((i * sub + j) % (n_tok // rows), 0)),
                      pl.BlockSpec((1, window), lambda i, j: (0, i))],
            out_specs=[], core_axis_name=("core", "subcore"),
            dimension_semantics=(pltpu.PARALLEL, pltpu.ARBITRARY), _explicit_indices=True,
        )(h_hbm, i_hbm)

    return scatter(h2, idx)


def _scatter_rows_kernel(dest_ref, h_ref, xs_in_ref, xs_ref, sem):
    del xs_in_ref
    n_tok = h_ref.shape[0]

    def issue(i, carry):
        for k in range(2):
            d = dest_ref[k * n_tok + i]
            pltpu.make_async_copy(h_ref.at[pl.ds(i, 1)], xs_ref.at[pl.ds(d, 1)], sem.at[0]).start()
        return carry

    lax.fori_loop(0, n_tok, issue, 0, unroll=8)
    for _ in range(2):
        pltpu.make_async_copy(h_ref, xs_ref.at[pl.ds(0, n_tok)], sem.at[0]).wait()


def _scatter_rows(dest, h2, xs):
    return pl.pallas_call(
        _scatter_rows_kernel,
        grid_spec=pltpu.PrefetchScalarGridSpec(
            num_scalar_prefetch=1, grid=(1,),
            in_specs=[pl.BlockSpec(h2.shape, lambda i, d: (0, 0)), pl.BlockSpec(memory_space=pl.ANY)],
            out_specs=pl.BlockSpec(memory_space=pl.ANY),
            scratch_shapes=[pltpu.SemaphoreType.DMA((1,))]),
        out_shape=jax.ShapeDtypeStruct(xs.shape, xs.dtype),
        input_output_aliases={2: 0},
        compiler_params=pltpu.CompilerParams(dimension_semantics=("arbitrary",)),
        name="scatter_rows",
    )(dest, h2, xs)


def _experts_kernel(be_ref, nu_ref, nv_ref, nxt_ref, par_ref, phys_ref, xs_ref, wg_ref, wu_ref, wd_ref, y_ref,
                    wg_s, wu_s, wd_s, wg_f, wu_f, wd_f, sem):
    del phys_ref
    j = pl.program_id(0)
    used = j < nu_ref[0]
    changed = jnp.logical_or(j == 0, be_ref[j] != be_ref[jnp.maximum(j - 1, 0)])

    def weight_copies(e, slot):
        return [pltpu.make_async_copy(src.at[e], dst.at[slot], sem.at[slot])
                for src, dst in ((wg_ref, wg_f), (wu_ref, wu_f), (wd_ref, wd_f))]

    @pl.when(jnp.logical_and(used, changed))
    def _():
        slot = par_ref[j]

        @pl.when(j == 0)
        def _():
            for c in weight_copies(be_ref[0], slot):
                c.start()

        for c in weight_copies(be_ref[j], slot):
            c.wait()

        @pl.when(nxt_ref[j] >= 0)
        def _():
            for c in weight_copies(nxt_ref[j], 1 - slot):
                c.start()

        wg_s[...] = wg_f[slot].astype(jnp.bfloat16)
        wu_s[...] = wu_f[slot].astype(jnp.bfloat16)
        wd_s[...] = wd_f[slot].astype(jnp.bfloat16)

    @pl.when(used)
    def _():
        live = lax.broadcasted_iota(jnp.int32, (xs_ref.shape[0], 1), 0) < nv_ref[j]
        x = _unpack_bf16_pairs(jnp.where(live, xs_ref[...], jnp.uint32(0)))
        g = jnp.dot(x, wg_s[...], preferred_element_type=jnp.float32)
        u = jnp.dot(x, wu_s[...], preferred_element_type=jnp.float32)
        a = (g * jax.nn.sigmoid(g) * u).astype(jnp.bfloat16)
        y_ref[...] = jnp.dot(a, wd_s[...], preferred_element_type=jnp.float32).reshape(y_ref.shape)


def _experts(block_e, n_used, n_valid, next_e, parity, phys, xs, w_gate, w_up, w_down):
    bm = EXPERT_BLOCK
    n_blocks = xs.shape[0] // bm
    hbm = pl.BlockSpec(memory_space=pl.ANY)
    return pl.pallas_call(
        _experts_kernel,
        grid_spec=pltpu.PrefetchScalarGridSpec(
            num_scalar_prefetch=6, grid=(n_blocks,),
            in_specs=[pl.BlockSpec((bm, D_MODEL // 2), lambda j, be, nu, nv, nx, pa, ph: (ph[j], 0)),
                      hbm, hbm, hbm],
            out_specs=pl.BlockSpec((bm, 1, D_MODEL), lambda j, be, nu, nv, nx, pa, ph: (ph[j], 0, 0)),
            scratch_shapes=[pltpu.VMEM((D_MODEL, EXPERT_HIDDEN), jnp.bfloat16),
                            pltpu.VMEM((D_MODEL, EXPERT_HIDDEN), jnp.bfloat16),
                            pltpu.VMEM((EXPERT_HIDDEN, D_MODEL), jnp.bfloat16),
                            pltpu.VMEM((2, D_MODEL, EXPERT_HIDDEN), jnp.float32),
                            pltpu.VMEM((2, D_MODEL, EXPERT_HIDDEN), jnp.float32),
                            pltpu.VMEM((2, EXPERT_HIDDEN, D_MODEL), jnp.float32),
                            pltpu.SemaphoreType.DMA((2,))]),
        out_shape=jax.ShapeDtypeStruct((xs.shape[0], 1, D_MODEL), jnp.float32),
        compiler_params=pltpu.CompilerParams(dimension_semantics=("arbitrary",), vmem_limit_bytes=VMEM_LIMIT),
        name="experts",
    )(block_e, n_used, n_valid, next_e, parity, phys, xs, w_gate, w_up, w_down)


def _combine_kernel(dest_ref, x1_ref, gate_ref, rf_ref, fin_ref, yb_ref, o_ref, ybuf, y_s, sem, *,
                    n_tok, per_token_gate):
    i = pl.program_id(0)
    n = pl.num_programs(0)
    tc = COMBINE_TILE

    def issue(step, slot):
        def one(r, carry):
            tok = step * tc + r
            for k in range(2):
                d = dest_ref[k * n_tok + tok]
                pltpu.make_async_copy(yb_ref.at[pl.ds(d, 1)], ybuf.at[slot, k, pl.ds(r, 1)], sem.at[slot]).start()
            return carry
        lax.fori_loop(0, tc, one, 0, unroll=8)

    slot = i % 2

    @pl.when(i == 0)
    def _():
        issue(0, 0)

    @pl.when(i + 1 < n)
    def _():
        issue(i + 1, 1 - slot)

    pltpu.make_async_copy(yb_ref.at[pl.ds(0, tc)], ybuf.at[slot, 0], sem.at[slot]).wait()
    pltpu.make_async_copy(yb_ref.at[pl.ds(0, tc)], ybuf.at[slot, 1], sem.at[slot]).wait()

    gates = jnp.transpose(rf_ref[...])
    y_s[0] = ybuf[slot, 0].reshape(tc, D_MODEL)
    y_s[1] = ybuf[slot, 1].reshape(tc, D_MODEL)
    y0 = y_s[0]
    y1 = y_s[1]
    moe = y0 * gates[:, 0:1] + y1 * gates[:, 1:2]
    gate_f = gate_ref[...] if per_token_gate else gate_ref[...][5:6]
    o_ref[...] = _rms(x1_ref[...] + gate_f * moe, fin_ref[...])


def _combine(dest_flat, x1, gate, rf, final_norm, yb, *, tiles_per_gate):
    n_tok = x1.shape[0]
    tc = COMBINE_TILE
    per_token = tiles_per_gate == 0
    if per_token:
        gate_spec = pl.BlockSpec((tc, D_MODEL), lambda i, d: (i, 0))
    else:
        gate_spec = pl.BlockSpec((None, 6, D_MODEL), lambda i, d: (i // tiles_per_gate, 0, 0))
    return pl.pallas_call(
        functools.partial(_combine_kernel, n_tok=n_tok, per_token_gate=per_token),
        grid_spec=pltpu.PrefetchScalarGridSpec(
            num_scalar_prefetch=1, grid=(n_tok // tc,),
            in_specs=[pl.BlockSpec((tc, D_MODEL), lambda i, d: (i, 0)),
                      gate_spec,
                      pl.BlockSpec((8, tc), lambda i, d: (0, i)),
                      pl.BlockSpec((1, D_MODEL), lambda i, d: (0, 0)),
                      pl.BlockSpec(memory_space=pl.ANY)],
            out_specs=pl.BlockSpec((tc, D_MODEL), lambda i, d: (i, 0)),
            scratch_shapes=[pltpu.VMEM((2, 2, tc, 1, D_MODEL), jnp.float32),
                            pltpu.VMEM((2, tc, D_MODEL), jnp.float32), pltpu.SemaphoreType.DMA((2,))]),
        out_shape=jax.ShapeDtypeStruct((n_tok, D_MODEL), jnp.float32),
        compiler_params=pltpu.CompilerParams(dimension_semantics=("arbitrary",), vmem_limit_bytes=VMEM_LIMIT),
        name="combine",
    )(dest_flat, x1, gate, rf, final_norm, yb)


def _rope_tables(pos):
    half = HEAD_DIM // 2
    inv_freq = np.power(np.float32(ROPE_THETA), -(np.arange(half, dtype=np.float32) / np.float32(half)))
    ang = (pos.astype(np.float32)[:, None] * inv_freq[None, :].astype(np.float32)).astype(np.float64)
    cos, sin = np.cos(ang).astype(np.float32), np.sin(ang).astype(np.float32)
    cos = np.concatenate([cos, cos, cos, cos], axis=1)
    sin = np.concatenate([-sin, sin, -sin, sin], axis=1)
    return cos, sin


def _prompt_bias():
    r = np.arange(WINDOW)[:, None]
    j = np.arange(2 * WINDOW)[None, :]
    band = (j > r) & (j <= r + WINDOW)
    later = np.where(band, 0.0, NEG).astype(np.float32)
    first = np.where(band & (j >= WINDOW), 0.0, NEG).astype(np.float32)
    return jnp.asarray(np.stack([first, later]))


def _sample_bias(dec):
    cb = SAMPLE_CHUNK
    qb = np.repeat(np.arange(cb), dec)[:, None]
    qi = np.tile(np.arange(dec), cb)[:, None]
    kb = np.concatenate([np.repeat(np.arange(cb), WINDOW), np.repeat(np.arange(cb), dec)])[None, :]
    kj = np.concatenate([np.tile(np.arange(WINDOW), cb), np.tile(np.arange(dec), cb)])[None, :]
    is_new = (np.arange(cb * WINDOW + cb * dec) >= cb * WINDOW)[None, :]
    ok = (kb == qb) & np.where(is_new, kj <= qi, kj > qi)
    ok = np.pad(ok, ((0, 0), (0, SAMPLE_KEYS - ok.shape[1])))
    return jnp.asarray(np.where(ok, 0.0, NEG).astype(np.float32))


def kernel(x_prompt, x_sample, cache_k, cache_v, state_conv, c_prompt, c_sample, attn_norm, ffn_norm, w_mod, b_mod, w_in, conv_w, attn_sinks, out_norm_attn, out_norm_conv, w_out, w_group, b_group, w_expert, b_expert, w_gate, w_up, w_down, final_norm):
    assert attn_norm.shape[0] == 1, "one layer"
    bsz, seq, _ = x_prompt.shape
    nseq, dec, _ = x_sample.shape
    m_prompt = bsz * seq
    n_s = nseq * dec
    m_total = m_prompt + n_s
    tm = MIXER_TILE
    assert seq % tm == 0 and n_s % SAMPLE_TILE == 0 and SAMPLE_TILE % (SAMPLE_CHUNK * dec) == 0
    assert cache_k.shape[2] == WINDOW and m_prompt % SAMPLE_TILE == 0 and m_total % COMBINE_TILE == 0
    bf = jnp.bfloat16

    w_r = jnp.zeros((ROUTER_ROWS, D_MODEL), jnp.float32)
    w_r = w_r.at[0:N_GROUPS].set(w_group[0].T).at[8:8 + N_EXPERTS].set(w_expert[0].T).astype(bf)
    b_r = jnp.zeros((ROUTER_ROWS,), jnp.float32).at[0:N_GROUPS].set(b_group[0]).at[8:8 + N_EXPERTS].set(b_expert[0])
    b_r = jnp.broadcast_to(b_r[:, None], (ROUTER_ROWS, LANES))
    cos_p, sin_p = _rope_tables(np.arange(seq))
    cos_s, sin_s = _rope_tables(PAST_LEN + np.arange(dec))
    tri = jnp.asarray(np.triu(np.ones((tm, tm), np.float32), 1)).astype(bf)
    consts = dict(
        anorm=attn_norm[0][None], fnorm=ffn_norm[0][None], w_in=w_in[0].astype(bf),
        cos_p=jnp.asarray(cos_p), sin_p=jnp.asarray(sin_p),
        cos_s=jnp.asarray(np.tile(cos_s, (nseq, 1))), sin_s=jnp.asarray(np.tile(sin_s, (nseq, 1))),
        bias_p=_prompt_bias(), bias_s=_sample_bias(dec), conv_w=conv_w[0],
        gn_attn=out_norm_attn[0][None], gn_conv=out_norm_conv[0][None], w_out=w_out[0].astype(bf),
        w_r=w_r, b_r=b_r, tri=tri)
    sinks = attn_sinks[0]

    mod = _adaln(jnp.concatenate([c_prompt, c_sample], axis=0), w_mod[0], b_mod[0])
    mod_p = mod[:bsz].reshape(bsz, 6, D_MODEL)
    mod_s = jnp.repeat(mod[bsz:], dec, axis=0)

    bm = EXPERT_BLOCK
    eids = jnp.arange(N_EXPERTS, dtype=jnp.int32)
    lookup = lambda table, e: jnp.sum(jnp.where(e[..., None] == eids, table, 0), axis=-1)
    by_token = lambda a: jnp.transpose(a, (1, 0, 2)).reshape(8, -1)
    ceil_bm = lambda c: (c + bm - 1) // bm * bm
    x1_p, h2_p, ri_p, rf_p, cnt_p, k_p, v_p, u_p = _prompt_mixer(x_prompt, mod_p, sinks, consts)
    cnt_p = cnt_p[:, 0].astype(jnp.int32)
    padded_p = ceil_bm(cnt_p)
    pstart_p = jnp.cumsum(padded_p) - padded_p
    nb_p = -(-(2 * m_prompt + N_EXPERTS * (bm - 1)) // bm)
    nb_o = -(-(2 * n_s + N_EXPERTS * (bm - 1)) // bm)
    n_blocks = nb_p + nb_o
    ri_p = by_token(ri_p)
    dest_p = (lookup(pstart_p, ri_p[0:2]) + ri_p[2:4]).reshape(2 * m_prompt)
    xs = _dispatch(dest_p[None], h2_p, n_blocks * bm)

    st = state_conv[0]
    zrow = jnp.zeros((nseq, 1, CONV_DIM), jnp.float32)
    e1 = jnp.concatenate([st[:, 1:2]] + [zrow] * (dec - 1), axis=1).reshape(n_s, CONV_DIM)
    e2 = jnp.concatenate([st[:, 0:1], st[:, 1:2]] + [zrow] * (dec - 2), axis=1).reshape(n_s, CONV_DIM)
    ck = cache_k[0].reshape(nseq * WINDOW, KV_DIM)
    cv = cache_v[0].reshape(nseq * WINDOW, KV_DIM)
    x1_s, h2_s, ri_s, rf_s, cnt_s, k_s, v_s, u_s = _sample_mixer(
        x_sample.reshape(n_s, D_MODEL), mod_s, sinks, consts, ck, cv, e1, e2,
        jnp.zeros((N_EXPERTS, LANES), jnp.float32), dec)

    cnt_s = cnt_s[:, 0].astype(jnp.int32)
    fit = jnp.minimum(cnt_s, padded_p - cnt_p)
    over = cnt_s - fit
    padded_o = ceil_bm(over)
    ostart = nb_p * bm + jnp.cumsum(padded_o) - padded_o
    ri_s = by_token(ri_s)
    e_s, r_s = ri_s[0:2], ri_s[2:4]
    dest_s = jnp.where(r_s < lookup(fit, e_s), lookup(pstart_p + cnt_p, e_s) + r_s, lookup(ostart - fit, e_s) + r_s)
    dest_s = dest_s.reshape(2 * n_s)
    xs = _scatter_rows(dest_s, h2_s, xs)

    np_e, no_e = padded_p // bm, padded_o // bm
    lend = jnp.cumsum(np_e + no_e)
    lstart = lend - np_e - no_e
    n_used = lend[-1].astype(jnp.int32)
    blk = jnp.arange(n_blocks, dtype=jnp.int32)
    present = np_e + no_e > 0
    last_e = jnp.max(jnp.where(present, eids, 0))
    block_e = jnp.sum((lend[None, :] <= blk[:, None]).astype(jnp.int32), axis=1)
    block_e = jnp.where(blk < n_used, jnp.minimum(block_e, N_EXPERTS - 1), last_e)
    of_block = lambda table: lookup(table, block_e)
    within = blk - of_block(lstart)
    in_prompt = within < of_block(np_e)
    phys = jnp.where(in_prompt, of_block(pstart_p) // bm + within, of_block(ostart) // bm + within - of_block(np_e))
    n_valid = jnp.where(in_prompt, of_block(cnt_p + fit) - within * bm, of_block(over) - (within - of_block(np_e)) * bm)
    n_valid = jnp.where(blk < n_used, jnp.clip(n_valid, 0, bm), 0).astype(jnp.int32)
    phys = jnp.where(blk < n_used, phys, jnp.sum(jnp.where(blk == n_used - 1, phys, 0))).astype(jnp.int32)
    later = jnp.logical_and(present[None, :], eids[None, :] > eids[:, None])
    next_of = jnp.min(jnp.where(later, eids[None, :], N_EXPERTS), axis=1)
    next_of = jnp.where(next_of == N_EXPERTS, -1, next_of)
    next_e = of_block(next_of).astype(jnp.int32)
    parity = of_block((jnp.cumsum(present.astype(jnp.int32)) - 1) % 2).astype(jnp.int32)
    yb = _experts(block_e, n_used.reshape(1), n_valid, next_e, parity, phys, xs, w_gate[0], w_up[0], w_down[0])

    fin = final_norm[None]
    y_p = _combine(dest_p, x1_p, mod_p, by_token(rf_p), fin, yb, tiles_per_gate=seq // COMBINE_TILE)
    gate_s = mod_s[:, 5 * D_MODEL:]
    y_s = _combine(dest_s, x1_s, gate_s, by_token(rf_s), fin, yb, tiles_per_gate=0)

    keep = WINDOW
    new_k_p = k_p.reshape(1, bsz, keep, N_KV_HEADS, HEAD_DIM)
    new_v_p = v_p.reshape(1, bsz, keep, N_KV_HEADS, HEAD_DIM)
    new_u_p = u_p[:, 6:8][None]
    ksn = k_s.reshape(nseq, dec, N_KV_HEADS, HEAD_DIM)
    vsn = v_s.reshape(nseq, dec, N_KV_HEADS, HEAD_DIM)
    new_k_s = jnp.concatenate([cache_k[0][:, dec:], ksn], axis=1)[None]
    new_v_s = jnp.concatenate([cache_v[0][:, dec:], vsn], axis=1)[None]
    u_all = jnp.concatenate([st, u_s.reshape(nseq, dec, CONV_DIM)], axis=1)
    new_u_s = u_all[:, -2:][None]
    return (y_p.reshape(bsz, seq, D_MODEL), y_s.reshape(nseq, dec, D_MODEL),
            new_k_p, new_v_p, new_u_p, new_k_s, new_v_s, new_u_s)
```

```python
import functools

import jax
import jax.numpy as jnp
import numpy as np
from jax import lax
from jax.experimental import pallas as pl
from jax.experimental.pallas import tpu as pltpu
from jax.experimental.pallas import tpu_sc as plsc

D_MODEL = 1024
HEAD_DIM = 64
ATTN_DIM = 512
N_Q_HEADS = 8
N_KV_HEADS = 2
KV_DIM = 128
CONV_DIM = 512
WINDOW = 128
IN_PROJ_DIM = ATTN_DIM + 2 * KV_DIM + 3 * CONV_DIM
N_GROUPS = 4
EXPERTS_PER_GROUP = 8
N_EXPERTS = 32
EXPERT_HIDDEN = 512
RMS_EPS = 1e-5
ROPE_THETA = 10000.0
PAST_LEN = 16384

LANES = 128
ROUTER_ROWS = 48
NEG = -1e30

MIXER_TILE = 512
SAMPLE_TILE = 128
SAMPLE_CHUNK = 8
SAMPLE_KEYS = 1152
EXPERT_BLOCK = 512
COMBINE_TILE = 256
SC_ROWS = 64
SC_INDEX_WINDOW = 128
VMEM_LIMIT = 56 * 1024 * 1024


def _rms(x, gain):
    ms = jnp.mean(x * x, axis=-1, keepdims=True)
    return x * lax.rsqrt(ms + RMS_EPS) * gain


def _half_lane_mask(rows=1):
    return lax.broadcasted_iota(jnp.int32, (rows, LANES), 1) < HEAD_DIM


def _rope(x, cos, sin_signed):
    t, w = x.shape
    reps = w // LANES
    lane = lax.broadcasted_iota(jnp.int32, (t, w), 1)
    upper = (lane % HEAD_DIM) >= (HEAD_DIM // 2)
    partner = jnp.where(upper, pltpu.roll(x, HEAD_DIM // 2, 1), pltpu.roll(x, w - HEAD_DIM // 2, 1))
    if reps > 1:
        cos = jnp.concatenate([cos] * reps, axis=1)
        sin_signed = jnp.concatenate([sin_signed] * reps, axis=1)
    return x * cos + partner * sin_signed


def _swap_halves(x):
    return pltpu.roll(x, HEAD_DIM, 1)


def _group_norm64(x, gain):
    t, w = x.shape
    lo = _half_lane_mask(t)
    outs = []
    for c in range(w // LANES):
        xc = x[:, c * LANES:(c + 1) * LANES]
        sq = xc * xc
        s_lo = jnp.sum(jnp.where(lo, sq, 0.0), axis=-1, keepdims=True)
        s_hi = jnp.sum(jnp.where(lo, 0.0, sq), axis=-1, keepdims=True)
        r = jnp.where(lo, lax.rsqrt(s_lo * (1.0 / HEAD_DIM) + RMS_EPS), lax.rsqrt(s_hi * (1.0 / HEAD_DIM) + RMS_EPS))
        outs.append(xc * r)
    return jnp.concatenate(outs, axis=1) * gain


def _attend(q_rows, keys, keys_sw, vals, vals_sw, bias, sinks_ref, group):
    r = q_rows.shape[0]
    lo = _half_lane_mask(r)
    outs = []
    for parity in range(2):
        heads = [4 * group + parity, 4 * group + parity + 2]
        kk, vv = (keys, vals) if parity == group else (keys_sw, vals_sw)
        half = lo if parity == 0 else jnp.logical_not(lo)
        qs = []
        for h in heads:
            pair = q_rows[:, (h // 2) * LANES:(h // 2 + 1) * LANES]
            qs.append(jnp.where(half, pair, jnp.zeros_like(pair)))
        qcat = jnp.concatenate(qs, axis=0)
        s = lax.dot_general(qcat, kk, (((1,), (1,)), ((), ())), preferred_element_type=jnp.float32)
        ps, dens = [], []
        for i, h in enumerate(heads):
            sh = s[i * r:(i + 1) * r] + bias
            sink = sinks_ref[h]
            m = jnp.maximum(jnp.max(sh, axis=-1, keepdims=True), sink)
            p = jnp.exp(sh - m)
            dens.append(jnp.sum(p, axis=-1, keepdims=True) + jnp.exp(sink - m))
            ps.append(p.astype(jnp.bfloat16))
        o = jnp.dot(jnp.concatenate(ps, axis=0), vv, preferred_element_type=jnp.float32)
        for i, h in enumerate(heads):
            oh = o[i * r:(i + 1) * r] / dens[i]
            ms = jnp.sum(jnp.where(half, oh * oh, 0.0), axis=-1, keepdims=True) * (1.0 / HEAD_DIM)
            outs.append((h, oh * lax.rsqrt(ms + RMS_EPS)))
    outs.sort(key=lambda t: t[0])
    return [o for _, o in outs]


def _merge_heads(head_outs, lo):
    pairs = [jnp.where(lo, head_outs[2 * j], head_outs[2 * j + 1]) for j in range(4)]
    return jnp.concatenate(pairs, axis=1)


def _route(h2b, wr_ref, br_ref, tri_ref, cnt_ref, ri_ref, rf_ref):
    t = h2b.shape[0]
    lg = lax.dot_general(wr_ref[...], h2b, (((1,), (1,)), ((), ())), preferred_element_type=jnp.float32)
    lg = lg + br_ref[...][:, 0:1]
    r8 = lax.broadcasted_iota(jnp.int32, (8, t), 0)
    lgrp = jnp.where(r8 < N_GROUPS, lg[0:8], -jnp.inf)
    gmax = jnp.max(lgrp, axis=0, keepdims=True)
    grp = jnp.min(jnp.where(lgrp == gmax, r8, 8), axis=0, keepdims=True)
    pg_sel = 1.0 / jnp.sum(jnp.exp(lgrp - gmax), axis=0, keepdims=True)
    r32 = lax.broadcasted_iota(jnp.int32, (N_EXPERTS, t), 0)
    le = jnp.where((r32 // EXPERTS_PER_GROUP) == grp, lg[8:8 + N_EXPERTS], -jnp.inf)
    v1 = jnp.max(le, axis=0, keepdims=True)
    i1 = jnp.min(jnp.where(le == v1, r32, N_EXPERTS), axis=0, keepdims=True)
    le2 = jnp.where(r32 == i1, -jnp.inf, le)
    v2 = jnp.max(le2, axis=0, keepdims=True)
    i2 = jnp.min(jnp.where(le2 == v2, r32, N_EXPERTS), axis=0, keepdims=True)
    e21 = jnp.exp(v2 - v1)
    gate1 = pg_sel / (1.0 + e21)
    gate2 = pg_sel * e21 / (1.0 + e21)
    hot1 = r32 == i1
    hot2 = r32 == i2
    onehot = jnp.where(jnp.logical_or(hot1, hot2), 1.0, 0.0)
    before = jnp.dot(onehot.astype(jnp.bfloat16), tri_ref[...], preferred_element_type=jnp.float32)
    pos = before + cnt_ref[...][:, 0:1]
    rank1 = jnp.sum(jnp.where(hot1, pos, 0.0), axis=0, keepdims=True)
    rank2 = jnp.sum(jnp.where(hot2, pos, 0.0), axis=0, keepdims=True)
    cnt_ref[...] = cnt_ref[...] + jnp.sum(onehot, axis=1, keepdims=True)
    zi = jnp.zeros((4, t), jnp.int32)
    ri_ref[...] = jnp.concatenate([i1, i2, rank1.astype(jnp.int32), rank2.astype(jnp.int32), zi], axis=0)
    rf_ref[...] = jnp.concatenate([gate1, gate2, jnp.zeros((6, t), jnp.float32)], axis=0)


def _pack_bf16_pairs(xb):
    w = xb.shape[1] // 2
    bits = pltpu.bitcast(xb.astype(jnp.float32), jnp.uint32)
    return (bits[:, :w] >> 16) | (bits[:, w:] & jnp.uint32(0xFFFF0000))


def _unpack_bf16_pairs(words):
    lo = pltpu.bitcast(words << 16, jnp.float32)
    hi = pltpu.bitcast(words & jnp.uint32(0xFFFF0000), jnp.float32)
    return jnp.concatenate([lo.astype(jnp.bfloat16), hi.astype(jnp.bfloat16)], axis=1)


def _split_proj(proj):
    a = ATTN_DIM
    q = proj[:, :a]
    k = proj[:, a:a + KV_DIM]
    v = proj[:, a + KV_DIM:a + 2 * KV_DIM]
    c0 = a + 2 * KV_DIM
    h_conv = proj[:, c0:c0 + CONV_DIM]
    gate_b = proj[:, c0 + CONV_DIM:c0 + 2 * CONV_DIM]
    gate_c = proj[:, c0 + 2 * CONV_DIM:c0 + 3 * CONV_DIM]
    return q, k, v, gate_c * h_conv, gate_b


def _mixer_tail(x, attn_n, conv_n, mod, wout_ref, fnorm_ref, wr_ref, br_ref, tri_ref, cnt_ref,
                x1_ref, h2_ref, ri_ref, rf_ref):
    gate_a, shift_f, scale_f = mod
    cat = jnp.concatenate([attn_n.astype(jnp.bfloat16), conv_n.astype(jnp.bfloat16)], axis=1)
    mix = jnp.dot(cat, wout_ref[...], preferred_element_type=jnp.float32)
    x1 = x + gate_a * mix
    x1_ref[...] = x1
    h2 = _rms(x1, fnorm_ref[...] * (1.0 + scale_f)) + shift_f
    h2b = h2.astype(jnp.bfloat16)
    h2_ref[...] = _pack_bf16_pairs(h2b)
    _route(h2b, wr_ref, br_ref, tri_ref, cnt_ref, ri_ref, rf_ref)


def _adaln_kernel(c_ref, w_ref, b_ref, o_ref):
    c = c_ref[...]
    a = (c * jax.nn.sigmoid(c)).astype(jnp.bfloat16)
    o_ref[...] = jnp.dot(a, w_ref[...].astype(jnp.bfloat16), preferred_element_type=jnp.float32) + b_ref[...]


def _adaln(c, w_mod, b_mod):
    n = c.shape[0]
    tn = 1536
    return pl.pallas_call(
        _adaln_kernel,
        grid=(w_mod.shape[1] // tn,),
        in_specs=[pl.BlockSpec((n, D_MODEL), lambda j: (0, 0)),
                  pl.BlockSpec((D_MODEL, tn), lambda j: (0, j)),
                  pl.BlockSpec((1, tn), lambda j: (0, j))],
        out_specs=pl.BlockSpec((n, tn), lambda j: (0, j)),
        out_shape=jax.ShapeDtypeStruct((n, w_mod.shape[1]), jnp.float32),
        compiler_params=pltpu.CompilerParams(dimension_semantics=("arbitrary",), vmem_limit_bytes=VMEM_LIMIT),
        name="adaln",
    )(c, w_mod, b_mod.reshape(1, -1))


def _prompt_mixer_kernel(sinks_ref, x_ref, mod_ref, anorm_ref, fnorm_ref, win_ref, cos_ref, sin_ref, bias_ref,
                         convw_ref, gna_ref, gnc_ref, wout_ref, wr_ref, br_ref, tri_ref,
                         x1_ref, h2_ref, ri_ref, rf_ref, cnt_out_ref, knew_ref, vnew_ref, unew_ref,
                         q_s, k_s, ksw_s, v_s, vsw_s, u_s, att_s, cnt_s):
    b = pl.program_id(0)
    t = pl.program_id(1)
    tm = MIXER_TILE
    nblk = tm // WINDOW

    @pl.when(jnp.logical_and(b == 0, t == 0))
    def _():
        cnt_s[...] = jnp.zeros_like(cnt_s)

    @pl.when(t == 0)
    def _():
        z = jnp.zeros((WINDOW, KV_DIM), jnp.bfloat16)
        k_s[0:WINDOW, :] = z
        ksw_s[0:WINDOW, :] = z
        v_s[0:WINDOW, :] = z
        vsw_s[0:WINDOW, :] = z
        u_s[0:8, :] = jnp.zeros((8, CONV_DIM), jnp.float32)

    x = x_ref[...]
    mod = mod_ref[...]
    h = _rms(x, anorm_ref[...] * (1.0 + mod[1:2])) + mod[0:1]
    proj = jnp.dot(h.astype(jnp.bfloat16), win_ref[...], preferred_element_type=jnp.float32)
    q, k, v, u, gate_b = _split_proj(proj)
    cos = cos_ref[...]
    sin = sin_ref[...]
    q = _rope(q, cos, sin) * (HEAD_DIM ** -0.5)
    k = _rope(k, cos, sin)
    q_s[...] = q.astype(jnp.bfloat16)
    k_s[WINDOW:, :] = k.astype(jnp.bfloat16)
    ksw_s[WINDOW:, :] = _swap_halves(k).astype(jnp.bfloat16)
    v_s[WINDOW:, :] = v.astype(jnp.bfloat16)
    vsw_s[WINDOW:, :] = _swap_halves(v).astype(jnp.bfloat16)
    knew_ref[...] = k[tm - WINDOW:, :]
    vnew_ref[...] = v[tm - WINDOW:, :]

    u_s[8:, :] = u
    unew_ref[...] = u[tm - 8:, :]
    cw = convw_ref[...]
    conv = (u_s[6:6 + tm, :] * cw[0:1] + u_s[7:7 + tm, :] * cw[1:2] + u * cw[2:3]) * gate_b
    u_s[0:8, :] = u[tm - 8:, :]
    conv_n = _group_norm64(conv, gnc_ref[...])

    lo = _half_lane_mask(WINDOW)
    first = t == 0

    def block(i, carry):
        r0 = pl.multiple_of(i * WINDOW, WINDOW)
        bias = bias_ref[jnp.where(jnp.logical_and(first, i == 0), 0, 1)]
        qb = q_s[pl.ds(r0, WINDOW), :]
        keys = k_s[pl.ds(r0, 2 * WINDOW), :]
        keys_sw = ksw_s[pl.ds(r0, 2 * WINDOW), :]
        vals = v_s[pl.ds(r0, 2 * WINDOW), :]
        vals_sw = vsw_s[pl.ds(r0, 2 * WINDOW), :]
        outs = []
        for g in range(N_KV_HEADS):
            outs += _attend(qb, keys, keys_sw, vals, vals_sw, bias, sinks_ref, g)
        att_s[pl.ds(r0, WINDOW), :] = _merge_heads(outs, lo)
        return carry

    lax.fori_loop(0, nblk, block, 0, unroll=True)
    k_s[0:WINDOW, :] = k_s[tm:tm + WINDOW, :]
    ksw_s[0:WINDOW, :] = ksw_s[tm:tm + WINDOW, :]
    v_s[0:WINDOW, :] = v_s[tm:tm + WINDOW, :]
    vsw_s[0:WINDOW, :] = vsw_s[tm:tm + WINDOW, :]

    attn_n = att_s[...] * gna_ref[...]
    _mixer_tail(x, attn_n, conv_n, (mod[2:3], mod[3:4], mod[4:5]), wout_ref, fnorm_ref, wr_ref, br_ref, tri_ref,
                cnt_s, x1_ref, h2_ref, ri_ref, rf_ref)
    cnt_out_ref[...] = cnt_s[...]


def _prompt_mixer(x, mod, sinks, consts):
    bsz, seq, _ = x.shape
    tm = MIXER_TILE
    nt = seq // tm
    full = lambda shape: pl.BlockSpec(shape, lambda b, t, s: (0,) * len(shape))
    in_specs = [
        pl.BlockSpec((None, tm, D_MODEL), lambda b, t, s: (b, t, 0)),
        pl.BlockSpec((None, 6, D_MODEL), lambda b, t, s: (b, 0, 0)),
        full((1, D_MODEL)), full((1, D_MODEL)),
        full((D_MODEL, IN_PROJ_DIM)),
        pl.BlockSpec((tm, LANES), lambda b, t, s: (t, 0)),
        pl.BlockSpec((tm, LANES), lambda b, t, s: (t, 0)),
        full((2, WINDOW, 2 * WINDOW)),
        full((3, CONV_DIM)), full((1, ATTN_DIM)), full((1, CONV_DIM)),
        full((D_MODEL, D_MODEL)),
        full((ROUTER_ROWS, D_MODEL)), full((ROUTER_ROWS, LANES)),
        full((tm, tm)),
    ]
    out_shape = [
        jax.ShapeDtypeStruct((bsz * seq, D_MODEL), jnp.float32),
        jax.ShapeDtypeStruct((bsz * seq, D_MODEL // 2), jnp.uint32),
        jax.ShapeDtypeStruct((bsz * nt, 8, tm), jnp.int32),
        jax.ShapeDtypeStruct((bsz * nt, 8, tm), jnp.float32),
        jax.ShapeDtypeStruct((N_EXPERTS, LANES), jnp.float32),
        jax.ShapeDtypeStruct((bsz, WINDOW, KV_DIM), jnp.float32),
        jax.ShapeDtypeStruct((bsz, WINDOW, KV_DIM), jnp.float32),
        jax.ShapeDtypeStruct((bsz, 8, CONV_DIM), jnp.float32),
    ]
    out_specs = [
        pl.BlockSpec((tm, D_MODEL), lambda b, t, s: (b * nt + t, 0)),
        pl.BlockSpec((tm, D_MODEL // 2), lambda b, t, s: (b * nt + t, 0)),
        pl.BlockSpec((None, 8, tm), lambda b, t, s: (b * nt + t, 0, 0)),
        pl.BlockSpec((None, 8, tm), lambda b, t, s: (b * nt + t, 0, 0)),
        pl.BlockSpec((N_EXPERTS, LANES), lambda b, t, s: (0, 0)),
        pl.BlockSpec((None, WINDOW, KV_DIM), lambda b, t, s: (b, 0, 0)),
        pl.BlockSpec((None, WINDOW, KV_DIM), lambda b, t, s: (b, 0, 0)),
        pl.BlockSpec((None, 8, CONV_DIM), lambda b, t, s: (b, 0, 0)),
    ]
    scratch = [
        pltpu.VMEM((tm, ATTN_DIM), jnp.bfloat16),
        pltpu.VMEM((tm + WINDOW, KV_DIM), jnp.bfloat16), pltpu.VMEM((tm + WINDOW, KV_DIM), jnp.bfloat16),
        pltpu.VMEM((tm + WINDOW, KV_DIM), jnp.bfloat16), pltpu.VMEM((tm + WINDOW, KV_DIM), jnp.bfloat16),
        pltpu.VMEM((tm + 8, CONV_DIM), jnp.float32),
        pltpu.VMEM((tm, ATTN_DIM), jnp.float32),
        pltpu.VMEM((N_EXPERTS, LANES), jnp.float32),
    ]
    return pl.pallas_call(
        _prompt_mixer_kernel,
        grid_spec=pltpu.PrefetchScalarGridSpec(
            num_scalar_prefetch=1, grid=(bsz, nt), in_specs=in_specs, out_specs=out_specs, scratch_shapes=scratch),
        out_shape=out_shape,
        compiler_params=pltpu.CompilerParams(dimension_semantics=("arbitrary", "arbitrary"),
                                             vmem_limit_bytes=VMEM_LIMIT),
        name="prompt_mixer",
    )(sinks, x, mod, consts["anorm"], consts["fnorm"], consts["w_in"], consts["cos_p"], consts["sin_p"],
      consts["bias_p"], consts["conv_w"], consts["gn_attn"], consts["gn_conv"], consts["w_out"],
      consts["w_r"], consts["b_r"], consts["tri"])


def _sample_mixer_kernel(sinks_ref, x_ref, mod_ref, anorm_ref, fnorm_ref, win_ref, cos_ref, sin_ref, bias_ref,
                         convw_ref, gna_ref, gnc_ref, wout_ref, wr_ref, br_ref, tri_ref,
                         ck_ref, cv_ref, e1_ref, e2_ref, cnt_in_ref,
                         x1_ref, h2_ref, ri_ref, rf_ref, cnt_out_ref, knew_ref, vnew_ref, unew_ref,
                         q_s, k_s, v_s, att_s, cnt_s, *, dec):
    n = x_ref.shape[0]
    cb = SAMPLE_CHUNK
    rows = cb * dec
    pad_keys = SAMPLE_KEYS - cb * WINDOW - rows

    @pl.when(pl.program_id(0) == 0)
    def _():
        cnt_s[...] = cnt_in_ref[...]

    x = x_ref[...]
    mod = mod_ref[...]
    md = lambda j: mod[:, j * D_MODEL:(j + 1) * D_MODEL]
    h = _rms(x, anorm_ref[...]) * (1.0 + md(1)) + md(0)
    proj = jnp.dot(h.astype(jnp.bfloat16), win_ref[...], preferred_element_type=jnp.float32)
    q, k, v, u, gate_b = _split_proj(proj)
    cos = cos_ref[...]
    sin = sin_ref[...]
    q = _rope(q, cos, sin) * (HEAD_DIM ** -0.5)
    k = _rope(k, cos, sin)
    q_s[...] = q.astype(jnp.bfloat16)
    k_s[...] = k
    v_s[...] = v
    knew_ref[...] = k
    vnew_ref[...] = v
    unew_ref[...] = u

    pos = lax.broadcasted_iota(jnp.int32, (n, CONV_DIM), 0) % dec
    um1 = jnp.where(pos >= 1, pltpu.roll(u, 1, 0), 0.0) + e1_ref[...]
    um2 = jnp.where(pos >= 2, pltpu.roll(u, 2, 0), 0.0) + e2_ref[...]
    cw = convw_ref[...]
    conv = (um2 * cw[0:1] + um1 * cw[1:2] + u * cw[2:3]) * gate_b
    conv_n = _group_norm64(conv, gnc_ref[...])

    lo = _half_lane_mask(rows)
    bias = bias_ref[...]

    def chunk(c, carry):
        r0 = pl.multiple_of(c * rows, rows)
        c0 = pl.multiple_of(c * (cb * WINDOW), cb * WINDOW)
        zpad = jnp.zeros((pad_keys, KV_DIM), jnp.float32)
        kc = jnp.concatenate([ck_ref[pl.ds(c0, cb * WINDOW), :], k_s[pl.ds(r0, rows), :], zpad], axis=0)
        vc = jnp.concatenate([cv_ref[pl.ds(c0, cb * WINDOW), :], v_s[pl.ds(r0, rows), :], zpad], axis=0)
        keys, keys_sw = kc.astype(jnp.bfloat16), _swap_halves(kc).astype(jnp.bfloat16)
        vals, vals_sw = vc.astype(jnp.bfloat16), _swap_halves(vc).astype(jnp.bfloat16)
        qb = q_s[pl.ds(r0, rows), :]
        outs = []
        for g in range(N_KV_HEADS):
            outs += _attend(qb, keys, keys_sw, vals, vals_sw, bias, sinks_ref, g)
        att_s[pl.ds(r0, rows), :] = _merge_heads(outs, lo)
        return carry

    lax.fori_loop(0, n // rows, chunk, 0)
    attn_n = att_s[...] * gna_ref[...]
    _mixer_tail(x, attn_n, conv_n, (md(2), md(3), md(4)), wout_ref, fnorm_ref, wr_ref, br_ref, tri_ref,
                cnt_s, x1_ref, h2_ref, ri_ref, rf_ref)
    cnt_out_ref[...] = cnt_s[...]


def _sample_mixer(x, mod, sinks, consts, ck, cv, e1, e2, cnt_in, dec):
    n_s = x.shape[0]
    ts = SAMPLE_TILE
    seqs = ts // dec
    full = lambda a: pl.BlockSpec(a.shape, lambda i, s: (0,) * a.ndim)
    rows = lambda width: pl.BlockSpec((ts, width), lambda i, s: (i, 0))
    in_specs = [
        rows(D_MODEL), rows(6 * D_MODEL),
        full(consts["anorm"]), full(consts["fnorm"]), full(consts["w_in"]),
        rows(LANES), rows(LANES), full(consts["bias_s"]),
        full(consts["conv_w"]), full(consts["gn_attn"]), full(consts["gn_conv"]), full(consts["w_out"]),
        full(consts["w_r"]), full(consts["b_r"]),
        pl.BlockSpec((ts, ts), lambda i, s: (0, 0)),
        pl.BlockSpec((seqs * WINDOW, KV_DIM), lambda i, s: (i, 0)),
        pl.BlockSpec((seqs * WINDOW, KV_DIM), lambda i, s: (i, 0)),
        rows(CONV_DIM), rows(CONV_DIM), full(cnt_in),
    ]
    args = [x, mod, consts["anorm"], consts["fnorm"], consts["w_in"], consts["cos_s"], consts["sin_s"],
            consts["bias_s"], consts["conv_w"], consts["gn_attn"], consts["gn_conv"], consts["w_out"],
            consts["w_r"], consts["b_r"], consts["tri"], ck, cv, e1, e2, cnt_in]
    nt = n_s // ts
    out_shape = [
        jax.ShapeDtypeStruct((n_s, D_MODEL), jnp.float32),
        jax.ShapeDtypeStruct((n_s, D_MODEL // 2), jnp.uint32),
        jax.ShapeDtypeStruct((nt, 8, ts), jnp.int32),
        jax.ShapeDtypeStruct((nt, 8, ts), jnp.float32),
        jax.ShapeDtypeStruct((N_EXPERTS, LANES), jnp.float32),
        jax.ShapeDtypeStruct((n_s, KV_DIM), jnp.float32),
        jax.ShapeDtypeStruct((n_s, KV_DIM), jnp.float32),
        jax.ShapeDtypeStruct((n_s, CONV_DIM), jnp.float32),
    ]
    out_specs = [
        rows(D_MODEL),
        rows(D_MODEL // 2),
        pl.BlockSpec((None, 8, ts), lambda i, s: (i, 0, 0)),
        pl.BlockSpec((None, 8, ts), lambda i, s: (i, 0, 0)),
        pl.BlockSpec((N_EXPERTS, LANES), lambda i, s: (0, 0)),
        rows(KV_DIM), rows(KV_DIM), rows(CONV_DIM),
    ]
    scratch = [
        pltpu.VMEM((ts, ATTN_DIM), jnp.bfloat16),
        pltpu.VMEM((ts, KV_DIM), jnp.float32), pltpu.VMEM((ts, KV_DIM), jnp.float32),
        pltpu.VMEM((ts, ATTN_DIM), jnp.float32),
        pltpu.VMEM((N_EXPERTS, LANES), jnp.float32),
    ]
    return pl.pallas_call(
        functools.partial(_sample_mixer_kernel, dec=dec),
        grid_spec=pltpu.PrefetchScalarGridSpec(
            num_scalar_prefetch=1, grid=(nt,), in_specs=in_specs, out_specs=out_specs, scratch_shapes=scratch),
        out_shape=out_shape,
        compiler_params=pltpu.CompilerParams(dimension_semantics=("arbitrary",), vmem_limit_bytes=VMEM_LIMIT),
        name="sample_mixer",
    )(sinks, *args)


def _dispatch(idx, h2, n_rows):
    mesh = plsc.VectorSubcoreMesh(core_axis_name="core", subcore_axis_name="subcore")
    rows, window = SC_ROWS, SC_INDEX_WINDOW
    sub = window // rows
    n_tok, width = h2.shape
    assert n_tok % rows == 0 and (2 * n_tok) % window == 0

    moved = 3 * n_tok * width * h2.dtype.itemsize
    @pl.kernel(out_type=jax.ShapeDtypeStruct((n_rows, width), h2.dtype), mesh=mesh, scratch_types=[],
               cost_estimate=pl.CostEstimate(flops=0, transcendentals=0, bytes_accessed=moved))
    def scatter(h_hbm, i_hbm, xs_hbm):
        def body(indices, h_vmem, i_vmem):
            pltpu.sync_copy(h_vmem, xs_hbm.at[i_vmem.at[0, pl.ds(indices[1] * rows, rows)]])

        pltpu.emit_pipeline(
            body, grid=(2 * n_tok // window, sub),
            in_specs=[pl.BlockSpec((rows, width), lambda i, j: ((i * sub + j) % (n_tok // rows), 0)),
                      pl.BlockSpec((1, window), lambda i, j: (0, i))],
            out_specs=[], core_axis_name=("core", "subcore"),
            dimension_semantics=(pltpu.PARALLEL, pltpu.ARBITRARY), _explicit_indices=True,
        )(h_hbm, i_hbm)

    return scatter(h2, idx)


def _scatter_rows_kernel(dest_ref, h_ref, xs_in_ref, xs_ref, sem):
    del xs_in_ref
    n_tok = h_ref.shape[0]

    def issue(i, carry):
        for k in range(2):
            d = dest_ref[k * n_tok + i]
            pltpu.make_async_copy(h_ref.at[pl.ds(i, 1)], xs_ref.at[pl.ds(d, 1)], sem.at[0]).start(priority=k)
        return carry

    lax.fori_loop(0, n_tok, issue, 0, unroll=8)
    for _ in range(2):
        pltpu.make_async_copy(h_ref, xs_ref.at[pl.ds(0, n_tok)], sem.at[0]).wait()


def _scatter_rows(dest, h2, xs):
    return pl.pallas_call(
        _scatter_rows_kernel,
        grid_spec=pltpu.PrefetchScalarGridSpec(
            num_scalar_prefetch=1, grid=(1,),
            in_specs=[pl.BlockSpec(h2.shape, lambda i, d: (0, 0)), pl.BlockSpec(memory_space=pl.ANY)],
            out_specs=pl.BlockSpec(memory_space=pl.ANY),
            scratch_shapes=[pltpu.SemaphoreType.DMA((1,))]),
        out_shape=jax.ShapeDtypeStruct(xs.shape, xs.dtype),
        input_output_aliases={2: 0},
        compiler_params=pltpu.CompilerParams(dimension_semantics=("arbitrary",)),
        name="scatter_rows",
    )(dest, h2, xs)


def _experts_kernel(be_ref, nu_ref, nv_ref, nxt_ref, par_ref, phys_ref, xs_ref, wg_ref, wu_ref, wd_ref, y_ref,
                    wg_s, wu_s, wd_s, wg_f, wu_f, wd_f, sem):
    del phys_ref
    j = pl.program_id(0)
    used = j < nu_ref[0]
    changed = jnp.logical_or(j == 0, be_ref[j] != be_ref[jnp.maximum(j - 1, 0)])

    def weight_copies(e, slot):
        return [pltpu.make_async_copy(src.at[e], dst.at[slot], sem.at[slot])
                for src, dst in ((wg_ref, wg_f), (wu_ref, wu_f), (wd_ref, wd_f))]

    @pl.when(jnp.logical_and(used, changed))
    def _():
        slot = par_ref[j]

        @pl.when(j == 0)
        def _():
            for c in weight_copies(be_ref[0], slot):
                c.start()

        for c in weight_copies(be_ref[j], slot):
            c.wait()

        @pl.when(nxt_ref[j] >= 0)
        def _():
            for c in weight_copies(nxt_ref[j], 1 - slot):
                c.start()

        wg_s[...] = wg_f[slot].astype(jnp.bfloat16)
        wu_s[...] = wu_f[slot].astype(jnp.bfloat16)
        wd_s[...] = wd_f[slot].astype(jnp.bfloat16)

    @pl.when(used)
    def _():
        live = lax.broadcasted_iota(jnp.int32, (xs_ref.shape[0], 1), 0) < nv_ref[j]
        x = _unpack_bf16_pairs(jnp.where(live, xs_ref[...], jnp.uint32(0)))
        g = jnp.dot(x, wg_s[...], preferred_element_type=jnp.float32)
        u = jnp.dot(x, wu_s[...], preferred_element_type=jnp.float32)
        a = (g * jax.nn.sigmoid(g) * u).astype(jnp.bfloat16)
        y_ref[...] = jnp.dot(a, wd_s[...], preferred_element_type=jnp.float32).reshape(y_ref.shape)


def _experts(block_e, n_used, n_valid, next_e, parity, phys, xs, w_gate, w_up, w_down):
    bm = EXPERT_BLOCK
    n_blocks = xs.shape[0] // bm
    hbm = pl.BlockSpec(memory_space=pl.ANY)
    return pl.pallas_call(
        _experts_kernel,
        grid_spec=pltpu.PrefetchScalarGridSpec(
            num_scalar_prefetch=6, grid=(n_blocks,),
            in_specs=[pl.BlockSpec((bm, D_MODEL // 2), lambda j, be, nu, nv, nx, pa, ph: (ph[j], 0)),
                      hbm, hbm, hbm],
            out_specs=pl.BlockSpec((bm, 1, D_MODEL), lambda j, be, nu, nv, nx, pa, ph: (ph[j], 0, 0)),
            scratch_shapes=[pltpu.VMEM((D_MODEL, EXPERT_HIDDEN), jnp.bfloat16),
                            pltpu.VMEM((D_MODEL, EXPERT_HIDDEN), jnp.bfloat16),
                            pltpu.VMEM((EXPERT_HIDDEN, D_MODEL), jnp.bfloat16),
                            pltpu.VMEM((2, D_MODEL, EXPERT_HIDDEN), jnp.float32),
                            pltpu.VMEM((2, D_MODEL, EXPERT_HIDDEN), jnp.float32),
                            pltpu.VMEM((2, EXPERT_HIDDEN, D_MODEL), jnp.float32),
                            pltpu.SemaphoreType.DMA((2,))]),
        out_shape=jax.ShapeDtypeStruct((xs.shape[0], 1, D_MODEL), jnp.float32),
        compiler_params=pltpu.CompilerParams(dimension_semantics=("arbitrary",), vmem_limit_bytes=VMEM_LIMIT),
        name="experts",
    )(block_e, n_used, n_valid, next_e, parity, phys, xs, w_gate, w_up, w_down)


def _combine_kernel(dest_ref, x1_ref, gate_ref, rf_ref, fin_ref, yb_ref, o_ref, buf_a, buf_b, y_s, sem, *,
                    n_tok, per_token_gate):
    g = pl.program_id(0)
    n = pl.num_programs(0)
    tc = COMBINE_TILE

    def issue(half_index, buf, s):
        base = half_index * tc
        for r in range(tc):
            for k in range(2):
                d = dest_ref[k * n_tok + base + r]
                pltpu.make_async_copy(yb_ref.at[pl.ds(d, 1)], buf.at[k, pl.ds(r, 1)], sem.at[s]).start(priority=k)

    def wait(buf, s):
        for k in range(2):
            pltpu.make_async_copy(yb_ref.at[pl.ds(0, tc)], buf.at[k], sem.at[s]).wait()

    def combine(buf, rows):
        gates = jnp.transpose(rf_ref[:, rows])
        y_s[0] = buf[0].reshape(tc, D_MODEL)
        y_s[1] = buf[1].reshape(tc, D_MODEL)
        moe = y_s[0] * gates[:, 0:1] + y_s[1] * gates[:, 1:2]
        gate_f = gate_ref[rows, :] if per_token_gate else gate_ref[...][5:6]
        o_ref[rows, :] = _rms(x1_ref[rows, :] + gate_f * moe, fin_ref[...])

    @pl.when(g == 0)
    def _():
        issue(0, buf_a, 0)

    wait(buf_a, 0)
    issue(2 * g + 1, buf_b, 1)
    combine(buf_a, slice(0, tc))
    wait(buf_b, 1)
    issue(2 * jnp.minimum(g + 1, n - 1), buf_a, 0)
    combine(buf_b, slice(tc, 2 * tc))

    @pl.when(g == n - 1)
    def _():
        wait(buf_a, 0)


def _combine(dest_flat, x1, gate, rf, final_norm, yb, *, tiles_per_gate):
    n_tok = x1.shape[0]
    tc = COMBINE_TILE
    assert n_tok % (2 * tc) == 0
    per_token = tiles_per_gate == 0
    if per_token:
        gate_spec = pl.BlockSpec((2 * tc, D_MODEL), lambda i, d: (i, 0))
    else:
        gate_spec = pl.BlockSpec((None, 6, D_MODEL), lambda i, d: (i // tiles_per_gate, 0, 0))
    return pl.pallas_call(
        functools.partial(_combine_kernel, n_tok=n_tok, per_token_gate=per_token),
        grid_spec=pltpu.PrefetchScalarGridSpec(
            num_scalar_prefetch=1, grid=(n_tok // (2 * tc),),
            in_specs=[pl.BlockSpec((2 * tc, D_MODEL), lambda i, d: (i, 0)),
                      gate_spec,
                      pl.BlockSpec((8, 2 * tc), lambda i, d: (0, i)),
                      pl.BlockSpec((1, D_MODEL), lambda i, d: (0, 0)),
                      pl.BlockSpec(memory_space=pl.ANY)],
            out_specs=pl.BlockSpec((2 * tc, D_MODEL), lambda i, d: (i, 0)),
            scratch_shapes=[pltpu.VMEM((2, tc, 1, D_MODEL), jnp.float32),
                            pltpu.VMEM((2, tc, 1, D_MODEL), jnp.float32),
                            pltpu.VMEM((2, tc, D_MODEL), jnp.float32), pltpu.SemaphoreType.DMA((2,))]),
        out_shape=jax.ShapeDtypeStruct((n_tok, D_MODEL), jnp.float32),
        compiler_params=pltpu.CompilerParams(dimension_semantics=("arbitrary",), vmem_limit_bytes=VMEM_LIMIT),
        name="combine",
    )(dest_flat, x1, gate, rf, final_norm, yb)


def _rope_tables(pos):
    half = HEAD_DIM // 2
    inv_freq = np.power(np.float32(ROPE_THETA), -(np.arange(half, dtype=np.float32) / np.float32(half)))
    ang = (pos.astype(np.float32)[:, None] * inv_freq[None, :].astype(np.float32)).astype(np.float64)
    cos, sin = np.cos(ang).astype(np.float32), np.sin(ang).astype(np.float32)
    cos = np.concatenate([cos, cos, cos, cos], axis=1)
    sin = np.concatenate([-sin, sin, -sin, sin], axis=1)
    return cos, sin


def _prompt_bias():
    r = np.arange(WINDOW)[:, None]
    j = np.arange(2 * WINDOW)[None, :]
    band = (j > r) & (j <= r + WINDOW)
    later = np.where(band, 0.0, NEG).astype(np.float32)
    first = np.where(band & (j >= WINDOW), 0.0, NEG).astype(np.float32)
    return jnp.asarray(np.stack([first, later]))


def _sample_bias(dec):
    cb = SAMPLE_CHUNK
    qb = np.repeat(np.arange(cb), dec)[:, None]
    qi = np.tile(np.arange(dec), cb)[:, None]
    kb = np.concatenate([np.repeat(np.arange(cb), WINDOW), np.repeat(np.arange(cb), dec)])[None, :]
    kj = np.concatenate([np.tile(np.arange(WINDOW), cb), np.tile(np.arange(dec), cb)])[None, :]
    is_new = (np.arange(cb * WINDOW + cb * dec) >= cb * WINDOW)[None, :]
    ok = (kb == qb) & np.where(is_new, kj <= qi, kj > qi)
    ok = np.pad(ok, ((0, 0), (0, SAMPLE_KEYS - ok.shape[1])))
    return jnp.asarray(np.where(ok, 0.0, NEG).astype(np.float32))


def kernel(x_prompt, x_sample, cache_k, cache_v, state_conv, c_prompt, c_sample, attn_norm, ffn_norm, w_mod, b_mod, w_in, conv_w, attn_sinks, out_norm_attn, out_norm_conv, w_out, w_group, b_group, w_expert, b_expert, w_gate, w_up, w_down, final_norm):
    assert attn_norm.shape[0] == 1, "one layer"
    bsz, seq, _ = x_prompt.shape
    nseq, dec, _ = x_sample.shape
    m_prompt = bsz * seq
    n_s = nseq * dec
    m_total = m_prompt + n_s
    tm = MIXER_TILE
    assert seq % tm == 0 and n_s % SAMPLE_TILE == 0 and SAMPLE_TILE % (SAMPLE_CHUNK * dec) == 0
    assert cache_k.shape[2] == WINDOW and m_prompt % SAMPLE_TILE == 0 and m_total % COMBINE_TILE == 0
    bf = jnp.bfloat16

    w_r = jnp.zeros((ROUTER_ROWS, D_MODEL), jnp.float32)
    w_r = w_r.at[0:N_GROUPS].set(w_group[0].T).at[8:8 + N_EXPERTS].set(w_expert[0].T).astype(bf)
    b_r = jnp.zeros((ROUTER_ROWS,), jnp.float32).at[0:N_GROUPS].set(b_group[0]).at[8:8 + N_EXPERTS].set(b_expert[0])
    b_r = jnp.broadcast_to(b_r[:, None], (ROUTER_ROWS, LANES))
    cos_p, sin_p = _rope_tables(np.arange(seq))
    cos_s, sin_s = _rope_tables(PAST_LEN + np.arange(dec))
    tri = jnp.asarray(np.triu(np.ones((tm, tm), np.float32), 1)).astype(bf)
    consts = dict(
        anorm=attn_norm[0][None], fnorm=ffn_norm[0][None], w_in=w_in[0].astype(bf),
        cos_p=jnp.asarray(cos_p), sin_p=jnp.asarray(sin_p),
        cos_s=jnp.asarray(np.tile(cos_s, (nseq, 1))), sin_s=jnp.asarray(np.tile(sin_s, (nseq, 1))),
        bias_p=_prompt_bias(), bias_s=_sample_bias(dec), conv_w=conv_w[0],
        gn_attn=out_norm_attn[0][None], gn_conv=out_norm_conv[0][None], w_out=w_out[0].astype(bf),
        w_r=w_r, b_r=b_r, tri=tri)
    sinks = attn_sinks[0]

    mod = _adaln(jnp.concatenate([c_prompt, c_sample], axis=0), w_mod[0], b_mod[0])
    mod_p = mod[:bsz].reshape(bsz, 6, D_MODEL)
    mod_s = jnp.repeat(mod[bsz:], dec, axis=0)

    bm = EXPERT_BLOCK
    eids = jnp.arange(N_EXPERTS, dtype=jnp.int32)
    lookup = lambda table, e: jnp.sum(jnp.where(e[..., None] == eids, table, 0), axis=-1)
    by_token = lambda a: jnp.transpose(a, (1, 0, 2)).reshape(8, -1)
    ceil_bm = lambda c: (c + bm - 1) // bm * bm
    x1_p, h2_p, ri_p, rf_p, cnt_p, k_p, v_p, u_p = _prompt_mixer(x_prompt, mod_p, sinks, consts)
    cnt_p = cnt_p[:, 0].astype(jnp.int32)
    padded_p = ceil_bm(cnt_p)
    pstart_p = jnp.cumsum(padded_p) - padded_p
    nb_p = -(-(2 * m_prompt + N_EXPERTS * (bm - 1)) // bm)
    nb_o = -(-(2 * n_s + N_EXPERTS * (bm - 1)) // bm)
    n_blocks = nb_p + nb_o
    ri_p = by_token(ri_p)
    dest_p = (lookup(pstart_p, ri_p[0:2]) + ri_p[2:4]).reshape(2 * m_prompt)
    xs = _dispatch(dest_p[None], h2_p, n_blocks * bm)

    st = state_conv[0]
    zrow = jnp.zeros((nseq, 1, CONV_DIM), jnp.float32)
    e1 = jnp.concatenate([st[:, 1:2]] + [zrow] * (dec - 1), axis=1).reshape(n_s, CONV_DIM)
    e2 = jnp.concatenate([st[:, 0:1], st[:, 1:2]] + [zrow] * (dec - 2), axis=1).reshape(n_s, CONV_DIM)
    ck = cache_k[0].reshape(nseq * WINDOW, KV_DIM)
    cv = cache_v[0].reshape(nseq * WINDOW, KV_DIM)
    x1_s, h2_s, ri_s, rf_s, cnt_s, k_s, v_s, u_s = _sample_mixer(
        x_sample.reshape(n_s, D_MODEL), mod_s, sinks, consts, ck, cv, e1, e2,
        jnp.zeros((N_EXPERTS, LANES), jnp.float32), dec)

    cnt_s = cnt_s[:, 0].astype(jnp.int32)
    fit = jnp.minimum(cnt_s, padded_p - cnt_p)
    over = cnt_s - fit
    padded_o = ceil_bm(over)
    ostart = nb_p * bm + jnp.cumsum(padded_o) - padded_o
    ri_s = by_token(ri_s)
    e_s, r_s = ri_s[0:2], ri_s[2:4]
    dest_s = jnp.where(r_s < lookup(fit, e_s), lookup(pstart_p + cnt_p, e_s) + r_s, lookup(ostart - fit, e_s) + r_s)
    dest_s = dest_s.reshape(2 * n_s)
    xs = _scatter_rows(dest_s, h2_s, xs)

    np_e, no_e = padded_p // bm, padded_o // bm
    lend = jnp.cumsum(np_e + no_e)
    lstart = lend - np_e - no_e
    n_used = lend[-1].astype(jnp.int32)
    blk = jnp.arange(n_blocks, dtype=jnp.int32)
    present = np_e + no_e > 0
    last_e = jnp.max(jnp.where(present, eids, 0))
    block_e = jnp.sum((lend[None, :] <= blk[:, None]).astype(jnp.int32), axis=1)
    block_e = jnp.where(blk < n_used, jnp.minimum(block_e, N_EXPERTS - 1), last_e)
    of_block = lambda table: lookup(table, block_e)
    within = blk - of_block(lstart)
    in_prompt = within < of_block(np_e)
    phys = jnp.where(in_prompt, of_block(pstart_p) // bm + within, of_block(ostart) // bm + within - of_block(np_e))
    n_valid = jnp.where(in_prompt, of_block(cnt_p + fit) - within * bm, of_block(over) - (within - of_block(np_e)) * bm)
    n_valid = jnp.where(blk < n_used, jnp.clip(n_valid, 0, bm), 0).astype(jnp.int32)
    phys = jnp.where(blk < n_used, phys, jnp.sum(jnp.where(blk == n_used - 1, phys, 0))).astype(jnp.int32)
    later = jnp.logical_and(present[None, :], eids[None, :] > eids[:, None])
    next_of = jnp.min(jnp.where(later, eids[None, :], N_EXPERTS), axis=1)
    next_of = jnp.where(next_of == N_EXPERTS, -1, next_of)
    next_e = of_block(next_of).astype(jnp.int32)
    parity = of_block((jnp.cumsum(present.astype(jnp.int32)) - 1) % 2).astype(jnp.int32)
    yb = _experts(block_e, n_used.reshape(1), n_valid, next_e, parity, phys, xs, w_gate[0], w_up[0], w_down[0])

    fin = final_norm[None]
    y_p = _combine(dest_p, x1_p, mod_p, by_token(rf_p), fin, yb, tiles_per_gate=seq // (2 * COMBINE_TILE))
    gate_s = mod_s[:, 5 * D_MODEL:]
    y_s = _combine(dest_s, x1_s, gate_s, by_token(rf_s), fin, yb, tiles_per_gate=0)

    keep = WINDOW
    new_k_p = k_p.reshape(1, bsz, keep, N_KV_HEADS, HEAD_DIM)
    new_v_p = v_p.reshape(1, bsz, keep, N_KV_HEADS, HEAD_DIM)
    new_u_p = u_p[:, 6:8][None]
    ksn = k_s.reshape(nseq, dec, N_KV_HEADS, HEAD_DIM)
    vsn = v_s.reshape(nseq, dec, N_KV_HEADS, HEAD_DIM)
    new_k_s = jnp.concatenate([cache_k[0][:, dec:], ksn], axis=1)[None]
    new_v_s = jnp.concatenate([cache_v[0][:, dec:], vsn], axis=1)[None]
    u_all = jnp.concatenate([st, u_s.reshape(nseq, dec, CONV_DIM)], axis=1)
    new_u_s = u_all[:, -2:][None]
    return (y_p.reshape(bsz, seq, D_MODEL), y_s.reshape(nseq, dec, D_MODEL),
            new_k_p, new_v_p, new_u_p, new_k_s, new_v_s, new_u_s)
```

```python
import functools

import jax
import jax.numpy as jnp
import numpy as np
from jax import lax
from jax.experimental import pallas as pl
from jax.experimental.pallas import tpu as pltpu
from jax.experimental.pallas import tpu_sc as plsc

D_MODEL = 1024
HEAD_DIM = 64
ATTN_DIM = 512
N_Q_HEADS = 8
N_KV_HEADS = 2
KV_DIM = 128
CONV_DIM = 512
WINDOW = 128
IN_PROJ_DIM = ATTN_DIM + 2 * KV_DIM + 3 * CONV_DIM
N_GROUPS = 4
EXPERTS_PER_GROUP = 8
N_EXPERTS = 32
EXPERT_HIDDEN = 512
RMS_EPS = 1e-5
ROPE_THETA = 10000.0
PAST_LEN = 16384

LANES = 128
ROUTER_ROWS = 48
NEG = -1e30

MIXER_TILE = 512
SAMPLE_TILE = 128
SAMPLE_CHUNK = 8
SAMPLE_KEYS = 1152
EXPERT_BLOCK = 512
COMBINE_TILE = 256
SC_ROWS = 64
SC_INDEX_WINDOW = 128
VMEM_LIMIT = 56 * 1024 * 1024


def _rms(x, gain):
    ms = jnp.mean(x * x, axis=-1, keepdims=True)
    return x * lax.rsqrt(ms + RMS_EPS) * gain


def _half_lane_mask(rows=1):
    return lax.broadcasted_iota(jnp.int32, (rows, LANES), 1) < HEAD_DIM


def _rope(x, cos, sin_signed):
    t, w = x.shape
    reps = w // LANES
    lane = lax.broadcasted_iota(jnp.int32, (t, w), 1)
    upper = (lane % HEAD_DIM) >= (HEAD_DIM // 2)
    partner = jnp.where(upper, pltpu.roll(x, HEAD_DIM // 2, 1), pltpu.roll(x, w - HEAD_DIM // 2, 1))
    if reps > 1:
        cos = jnp.concatenate([cos] * reps, axis=1)
        sin_signed = jnp.concatenate([sin_signed] * reps, axis=1)
    return x * cos + partner * sin_signed


def _swap_halves(x):
    return pltpu.roll(x, HEAD_DIM, 1)


def _group_norm64(x, gain):
    t, w = x.shape
    lo = _half_lane_mask(t)
    outs = []
    for c in range(w // LANES):
        xc = x[:, c * LANES:(c + 1) * LANES]
        sq = xc * xc
        s_lo = jnp.sum(jnp.where(lo, sq, 0.0), axis=-1, keepdims=True)
        s_hi = jnp.sum(jnp.where(lo, 0.0, sq), axis=-1, keepdims=True)
        r = jnp.where(lo, lax.rsqrt(s_lo * (1.0 / HEAD_DIM) + RMS_EPS), lax.rsqrt(s_hi * (1.0 / HEAD_DIM) + RMS_EPS))
        outs.append(xc * r)
    return jnp.concatenate(outs, axis=1) * gain


def _attend(q_rows, keys, keys_sw, vals, vals_sw, bias, sinks_ref, group):
    r = q_rows.shape[0]
    lo = _half_lane_mask(r)
    outs = []
    for parity in range(2):
        heads = [4 * group + parity, 4 * group + parity + 2]
        kk, vv = (keys, vals) if parity == group else (keys_sw, vals_sw)
        half = lo if parity == 0 else jnp.logical_not(lo)
        qs = []
        for h in heads:
            pair = q_rows[:, (h // 2) * LANES:(h // 2 + 1) * LANES]
            qs.append(jnp.where(half, pair, jnp.zeros_like(pair)))
        qcat = jnp.concatenate(qs, axis=0)
        s = lax.dot_general(qcat, kk, (((1,), (1,)), ((), ())), preferred_element_type=jnp.float32)
        ps, dens = [], []
        for i, h in enumerate(heads):
            sh = s[i * r:(i + 1) * r] + bias
            sink = sinks_ref[h]
            m = jnp.maximum(jnp.max(sh, axis=-1, keepdims=True), sink)
            p = jnp.exp(sh - m)
            dens.append(jnp.sum(p, axis=-1, keepdims=True) + jnp.exp(sink - m))
            ps.append(p.astype(jnp.bfloat16))
        o = jnp.dot(jnp.concatenate(ps, axis=0), vv, preferred_element_type=jnp.float32)
        for i, h in enumerate(heads):
            oh = o[i * r:(i + 1) * r] / dens[i]
            ms = jnp.sum(jnp.where(half, oh * oh, 0.0), axis=-1, keepdims=True) * (1.0 / HEAD_DIM)
            outs.append((h, oh * lax.rsqrt(ms + RMS_EPS)))
    outs.sort(key=lambda t: t[0])
    return [o for _, o in outs]


def _merge_heads(head_outs, lo):
    pairs = [jnp.where(lo, head_outs[2 * j], head_outs[2 * j + 1]) for j in range(4)]
    return jnp.concatenate(pairs, axis=1)


def _route(h2b, wr_ref, br_ref, tri_ref, cnt_ref, ri_ref, rf_ref):
    t = h2b.shape[0]
    lg = lax.dot_general(wr_ref[...], h2b, (((1,), (1,)), ((), ())), preferred_element_type=jnp.float32)
    lg = lg + br_ref[...][:, 0:1]
    r8 = lax.broadcasted_iota(jnp.int32, (8, t), 0)
    lgrp = jnp.where(r8 < N_GROUPS, lg[0:8], -jnp.inf)
    gmax = jnp.max(lgrp, axis=0, keepdims=True)
    grp = jnp.min(jnp.where(lgrp == gmax, r8, 8), axis=0, keepdims=True)
    pg_sel = 1.0 / jnp.sum(jnp.exp(lgrp - gmax), axis=0, keepdims=True)
    r32 = lax.broadcasted_iota(jnp.int32, (N_EXPERTS, t), 0)
    le = jnp.where((r32 // EXPERTS_PER_GROUP) == grp, lg[8:8 + N_EXPERTS], -jnp.inf)
    v1 = jnp.max(le, axis=0, keepdims=True)
    i1 = jnp.min(jnp.where(le == v1, r32, N_EXPERTS), axis=0, keepdims=True)
    le2 = jnp.where(r32 == i1, -jnp.inf, le)
    v2 = jnp.max(le2, axis=0, keepdims=True)
    i2 = jnp.min(jnp.where(le2 == v2, r32, N_EXPERTS), axis=0, keepdims=True)
    e21 = jnp.exp(v2 - v1)
    gate1 = pg_sel / (1.0 + e21)
    gate2 = pg_sel * e21 / (1.0 + e21)
    hot1 = r32 == i1
    hot2 = r32 == i2
    onehot = jnp.where(jnp.logical_or(hot1, hot2), 1.0, 0.0)
    before = jnp.dot(onehot.astype(jnp.bfloat16), tri_ref[...], preferred_element_type=jnp.float32)
    pos = before + cnt_ref[...][:, 0:1]
    rank1 = jnp.sum(jnp.where(hot1, pos, 0.0), axis=0, keepdims=True)
    rank2 = jnp.sum(jnp.where(hot2, pos, 0.0), axis=0, keepdims=True)
    cnt_ref[...] = cnt_ref[...] + jnp.sum(onehot, axis=1, keepdims=True)
    zi = jnp.zeros((4, t), jnp.int32)
    ri_ref[...] = jnp.concatenate([i1, i2, rank1.astype(jnp.int32), rank2.astype(jnp.int32), zi], axis=0)
    rf_ref[...] = jnp.concatenate([gate1, gate2, jnp.zeros((6, t), jnp.float32)], axis=0)


def _pack_bf16_pairs(xb):
    w = xb.shape[1] // 2
    bits = pltpu.bitcast(xb.astype(jnp.float32), jnp.uint32)
    return (bits[:, :w] >> 16) | (bits[:, w:] & jnp.uint32(0xFFFF0000))


def _unpack_pairs_f32(words):
    lo = pltpu.bitcast(words << 16, jnp.float32)
    hi = pltpu.bitcast(words & jnp.uint32(0xFFFF0000), jnp.float32)
    return jnp.concatenate([lo, hi], axis=1)


def _unpack_bf16_pairs(words):
    return _unpack_pairs_f32(words).astype(jnp.bfloat16)


def _split_proj(proj):
    a = ATTN_DIM
    q = proj[:, :a]
    k = proj[:, a:a + KV_DIM]
    v = proj[:, a + KV_DIM:a + 2 * KV_DIM]
    c0 = a + 2 * KV_DIM
    h_conv = proj[:, c0:c0 + CONV_DIM]
    gate_b = proj[:, c0 + CONV_DIM:c0 + 2 * CONV_DIM]
    gate_c = proj[:, c0 + 2 * CONV_DIM:c0 + 3 * CONV_DIM]
    return q, k, v, gate_c * h_conv, gate_b


def _mixer_tail(x, attn_n, conv_n, mod, wout_ref, fnorm_ref, wr_ref, br_ref, tri_ref, cnt_ref,
                x1_ref, h2_ref, ri_ref, rf_ref):
    gate_a, shift_f, scale_f = mod
    cat = jnp.concatenate([attn_n.astype(jnp.bfloat16), conv_n.astype(jnp.bfloat16)], axis=1)
    mix = jnp.dot(cat, wout_ref[...], preferred_element_type=jnp.float32)
    x1 = x + gate_a * mix
    x1_ref[...] = x1
    h2 = _rms(x1, fnorm_ref[...] * (1.0 + scale_f)) + shift_f
    h2b = h2.astype(jnp.bfloat16)
    h2_ref[...] = _pack_bf16_pairs(h2b)
    _route(h2b, wr_ref, br_ref, tri_ref, cnt_ref, ri_ref, rf_ref)


def _adaln_kernel(c_ref, w_ref, b_ref, o_ref):
    c = c_ref[...]
    a = (c * jax.nn.sigmoid(c)).astype(jnp.bfloat16)
    o_ref[...] = jnp.dot(a, w_ref[...].astype(jnp.bfloat16), preferred_element_type=jnp.float32) + b_ref[...]


def _adaln(c, w_mod, b_mod):
    n = c.shape[0]
    tn = 1536
    return pl.pallas_call(
        _adaln_kernel,
        grid=(w_mod.shape[1] // tn,),
        in_specs=[pl.BlockSpec((n, D_MODEL), lambda j: (0, 0)),
                  pl.BlockSpec((D_MODEL, tn), lambda j: (0, j)),
                  pl.BlockSpec((1, tn), lambda j: (0, j))],
        out_specs=pl.BlockSpec((n, tn), lambda j: (0, j)),
        out_shape=jax.ShapeDtypeStruct((n, w_mod.shape[1]), jnp.float32),
        compiler_params=pltpu.CompilerParams(dimension_semantics=("arbitrary",), vmem_limit_bytes=VMEM_LIMIT),
        name="adaln",
    )(c, w_mod, b_mod.reshape(1, -1))


def _prompt_mixer_kernel(sinks_ref, x_ref, mod_ref, anorm_ref, fnorm_ref, win_ref, cos_ref, sin_ref, bias_ref,
                         convw_ref, gna_ref, gnc_ref, wout_ref, wr_ref, br_ref, tri_ref,
                         x1_ref, h2_ref, ri_ref, rf_ref, cnt_out_ref, knew_ref, vnew_ref, unew_ref,
                         q_s, k_s, ksw_s, v_s, vsw_s, u_s, att_s, cnt_s):
    b = pl.program_id(0)
    t = pl.program_id(1)
    tm = MIXER_TILE
    nblk = tm // WINDOW

    @pl.when(jnp.logical_and(b == 0, t == 0))
    def _():
        cnt_s[...] = jnp.zeros_like(cnt_s)

    @pl.when(t == 0)
    def _():
        z = jnp.zeros((WINDOW, KV_DIM), jnp.bfloat16)
        k_s[0:WINDOW, :] = z
        ksw_s[0:WINDOW, :] = z
        v_s[0:WINDOW, :] = z
        vsw_s[0:WINDOW, :] = z
        u_s[0:8, :] = jnp.zeros((8, CONV_DIM), jnp.float32)

    x = x_ref[...]
    mod = mod_ref[...]
    h = _rms(x, anorm_ref[...] * (1.0 + mod[1:2])) + mod[0:1]
    proj = jnp.dot(h.astype(jnp.bfloat16), win_ref[...], preferred_element_type=jnp.float32)
    q, k, v, u, gate_b = _split_proj(proj)
    cos = cos_ref[...]
    sin = sin_ref[...]
    q = _rope(q, cos, sin) * (HEAD_DIM ** -0.5)
    k = _rope(k, cos, sin)
    q_s[...] = q.astype(jnp.bfloat16)
    k_s[WINDOW:, :] = k.astype(jnp.bfloat16)
    ksw_s[WINDOW:, :] = _swap_halves(k).astype(jnp.bfloat16)
    v_s[WINDOW:, :] = v.astype(jnp.bfloat16)
    vsw_s[WINDOW:, :] = _swap_halves(v).astype(jnp.bfloat16)
    knew_ref[...] = k[tm - WINDOW:, :]
    vnew_ref[...] = v[tm - WINDOW:, :]

    u_s[8:, :] = u
    unew_ref[...] = u[tm - 8:, :]
    cw = convw_ref[...]
    conv = (u_s[6:6 + tm, :] * cw[0:1] + u_s[7:7 + tm, :] * cw[1:2] + u * cw[2:3]) * gate_b
    u_s[0:8, :] = u[tm - 8:, :]
    conv_n = _group_norm64(conv, gnc_ref[...])

    lo = _half_lane_mask(WINDOW)
    first = t == 0

    def block(i, carry):
        r0 = pl.multiple_of(i * WINDOW, WINDOW)
        bias = bias_ref[jnp.where(jnp.logical_and(first, i == 0), 0, 1)]
        qb = q_s[pl.ds(r0, WINDOW), :]
        keys = k_s[pl.ds(r0, 2 * WINDOW), :]
        keys_sw = ksw_s[pl.ds(r0, 2 * WINDOW), :]
        vals = v_s[pl.ds(r0, 2 * WINDOW), :]
        vals_sw = vsw_s[pl.ds(r0, 2 * WINDOW), :]
        outs = []
        for g in range(N_KV_HEADS):
            outs += _attend(qb, keys, keys_sw, vals, vals_sw, bias, sinks_ref, g)
        att_s[pl.ds(r0, WINDOW), :] = _merge_heads(outs, lo)
        return carry

    lax.fori_loop(0, nblk, block, 0, unroll=True)
    k_s[0:WINDOW, :] = k_s[tm:tm + WINDOW, :]
    ksw_s[0:WINDOW, :] = ksw_s[tm:tm + WINDOW, :]
    v_s[0:WINDOW, :] = v_s[tm:tm + WINDOW, :]
    vsw_s[0:WINDOW, :] = vsw_s[tm:tm + WINDOW, :]

    attn_n = att_s[...] * gna_ref[...]
    _mixer_tail(x, attn_n, conv_n, (mod[2:3], mod[3:4], mod[4:5]), wout_ref, fnorm_ref, wr_ref, br_ref, tri_ref,
                cnt_s, x1_ref, h2_ref, ri_ref, rf_ref)
    cnt_out_ref[...] = cnt_s[...]


def _prompt_mixer(x, mod, sinks, consts):
    bsz, seq, _ = x.shape
    tm = MIXER_TILE
    nt = seq // tm
    full = lambda shape: pl.BlockSpec(shape, lambda b, t, s: (0,) * len(shape))
    in_specs = [
        pl.BlockSpec((None, tm, D_MODEL), lambda b, t, s: (b, t, 0)),
        pl.BlockSpec((None, 6, D_MODEL), lambda b, t, s: (b, 0, 0)),
        full((1, D_MODEL)), full((1, D_MODEL)),
        full((D_MODEL, IN_PROJ_DIM)),
        pl.BlockSpec((tm, LANES), lambda b, t, s: (t, 0)),
        pl.BlockSpec((tm, LANES), lambda b, t, s: (t, 0)),
        full((2, WINDOW, 2 * WINDOW)),
        full((3, CONV_DIM)), full((1, ATTN_DIM)), full((1, CONV_DIM)),
        full((D_MODEL, D_MODEL)),
        full((ROUTER_ROWS, D_MODEL)), full((ROUTER_ROWS, LANES)),
        full((tm, tm)),
    ]
    out_shape = [
        jax.ShapeDtypeStruct((bsz * seq, D_MODEL), jnp.float32),
        jax.ShapeDtypeStruct((bsz * seq, D_MODEL // 2), jnp.uint32),
        jax.ShapeDtypeStruct((bsz * nt, 8, tm), jnp.int32),
        jax.ShapeDtypeStruct((bsz * nt, 8, tm), jnp.float32),
        jax.ShapeDtypeStruct((N_EXPERTS, LANES), jnp.float32),
        jax.ShapeDtypeStruct((bsz, WINDOW, KV_DIM), jnp.float32),
        jax.ShapeDtypeStruct((bsz, WINDOW, KV_DIM), jnp.float32),
        jax.ShapeDtypeStruct((bsz, 8, CONV_DIM), jnp.float32),
    ]
    out_specs = [
        pl.BlockSpec((tm, D_MODEL), lambda b, t, s: (b * nt + t, 0)),
        pl.BlockSpec((tm, D_MODEL // 2), lambda b, t, s: (b * nt + t, 0)),
        pl.BlockSpec((None, 8, tm), lambda b, t, s: (b * nt + t, 0, 0)),
        pl.BlockSpec((None, 8, tm), lambda b, t, s: (b * nt + t, 0, 0)),
        pl.BlockSpec((N_EXPERTS, LANES), lambda b, t, s: (0, 0)),
        pl.BlockSpec((None, WINDOW, KV_DIM), lambda b, t, s: (b, 0, 0)),
        pl.BlockSpec((None, WINDOW, KV_DIM), lambda b, t, s: (b, 0, 0)),
        pl.BlockSpec((None, 8, CONV_DIM), lambda b, t, s: (b, 0, 0)),
    ]
    scratch = [
        pltpu.VMEM((tm, ATTN_DIM), jnp.bfloat16),
        pltpu.VMEM((tm + WINDOW, KV_DIM), jnp.bfloat16), pltpu.VMEM((tm + WINDOW, KV_DIM), jnp.bfloat16),
        pltpu.VMEM((tm + WINDOW, KV_DIM), jnp.bfloat16), pltpu.VMEM((tm + WINDOW, KV_DIM), jnp.bfloat16),
        pltpu.VMEM((tm + 8, CONV_DIM), jnp.float32),
        pltpu.VMEM((tm, ATTN_DIM), jnp.float32),
        pltpu.VMEM((N_EXPERTS, LANES), jnp.float32),
    ]
    return pl.pallas_call(
        _prompt_mixer_kernel,
        grid_spec=pltpu.PrefetchScalarGridSpec(
            num_scalar_prefetch=1, grid=(bsz, nt), in_specs=in_specs, out_specs=out_specs, scratch_shapes=scratch),
        out_shape=out_shape,
        compiler_params=pltpu.CompilerParams(dimension_semantics=("arbitrary", "arbitrary"),
                                             vmem_limit_bytes=VMEM_LIMIT),
        name="prompt_mixer",
    )(sinks, x, mod, consts["anorm"], consts["fnorm"], consts["w_in"], consts["cos_p"], consts["sin_p"],
      consts["bias_p"], consts["conv_w"], consts["gn_attn"], consts["gn_conv"], consts["w_out"],
      consts["w_r"], consts["b_r"], consts["tri"])


def _sample_mixer_kernel(sinks_ref, x_ref, mod_ref, anorm_ref, fnorm_ref, win_ref, cos_ref, sin_ref, bias_ref,
                         convw_ref, gna_ref, gnc_ref, wout_ref, wr_ref, br_ref, tri_ref,
                         ck_ref, cv_ref, e1_ref, e2_ref, cnt_in_ref,
                         x1_ref, h2_ref, ri_ref, rf_ref, cnt_out_ref, knew_ref, vnew_ref, unew_ref,
                         q_s, k_s, v_s, att_s, cnt_s, *, dec):
    n = x_ref.shape[0]
    cb = SAMPLE_CHUNK
    rows = cb * dec
    pad_keys = SAMPLE_KEYS - cb * WINDOW - rows

    @pl.when(pl.program_id(0) == 0)
    def _():
        cnt_s[...] = cnt_in_ref[...]

    x = x_ref[...]
    mod = mod_ref[...]
    md = lambda j: mod[:, j * D_MODEL:(j + 1) * D_MODEL]
    h = _rms(x, anorm_ref[...]) * (1.0 + md(1)) + md(0)
    proj = jnp.dot(h.astype(jnp.bfloat16), win_ref[...], preferred_element_type=jnp.float32)
    q, k, v, u, gate_b = _split_proj(proj)
    cos = cos_ref[...]
    sin = sin_ref[...]
    q = _rope(q, cos, sin) * (HEAD_DIM ** -0.5)
    k = _rope(k, cos, sin)
    q_s[...] = q.astype(jnp.bfloat16)
    k_s[...] = k
    v_s[...] = v
    for b in range(n // dec):
        lo, hi = b * WINDOW, (b + 1) * WINDOW
        knew_ref[lo:hi - dec, :] = ck_ref[lo + dec:hi, :]
        vnew_ref[lo:hi - dec, :] = cv_ref[lo + dec:hi, :]
        knew_ref[hi - dec:hi, :] = k[b * dec:(b + 1) * dec, :]
        vnew_ref[hi - dec:hi, :] = v[b * dec:(b + 1) * dec, :]
    unew_ref[...] = u

    pos = lax.broadcasted_iota(jnp.int32, (n, CONV_DIM), 0) % dec
    um1 = jnp.where(pos >= 1, pltpu.roll(u, 1, 0), 0.0) + e1_ref[...]
    um2 = jnp.where(pos >= 2, pltpu.roll(u, 2, 0), 0.0) + e2_ref[...]
    cw = convw_ref[...]
    conv = (um2 * cw[0:1] + um1 * cw[1:2] + u * cw[2:3]) * gate_b
    conv_n = _group_norm64(conv, gnc_ref[...])

    lo = _half_lane_mask(rows)
    bias = bias_ref[...]

    def chunk(c, carry):
        r0 = pl.multiple_of(c * rows, rows)
        c0 = pl.multiple_of(c * (cb * WINDOW), cb * WINDOW)
        zpad = jnp.zeros((pad_keys, KV_DIM), jnp.float32)
        kc = jnp.concatenate([ck_ref[pl.ds(c0, cb * WINDOW), :], k_s[pl.ds(r0, rows), :], zpad], axis=0)
        vc = jnp.concatenate([cv_ref[pl.ds(c0, cb * WINDOW), :], v_s[pl.ds(r0, rows), :], zpad], axis=0)
        keys, keys_sw = kc.astype(jnp.bfloat16), _swap_halves(kc).astype(jnp.bfloat16)
        vals, vals_sw = vc.astype(jnp.bfloat16), _swap_halves(vc).astype(jnp.bfloat16)
        qb = q_s[pl.ds(r0, rows), :]
        outs = []
        for g in range(N_KV_HEADS):
            outs += _attend(qb, keys, keys_sw, vals, vals_sw, bias, sinks_ref, g)
        att_s[pl.ds(r0, rows), :] = _merge_heads(outs, lo)
        return carry

    lax.fori_loop(0, n // rows, chunk, 0)
    attn_n = att_s[...] * gna_ref[...]
    _mixer_tail(x, attn_n, conv_n, (md(2), md(3), md(4)), wout_ref, fnorm_ref, wr_ref, br_ref, tri_ref,
                cnt_s, x1_ref, h2_ref, ri_ref, rf_ref)
    cnt_out_ref[...] = cnt_s[...]


def _sample_mixer(x, mod, sinks, consts, ck, cv, e1, e2, cnt_in, dec):
    n_s = x.shape[0]
    ts = SAMPLE_TILE
    seqs = ts // dec
    full = lambda a: pl.BlockSpec(a.shape, lambda i, s: (0,) * a.ndim)
    rows = lambda width: pl.BlockSpec((ts, width), lambda i, s: (i, 0))
    in_specs = [
        rows(D_MODEL), rows(6 * D_MODEL),
        full(consts["anorm"]), full(consts["fnorm"]), full(consts["w_in"]),
        rows(LANES), rows(LANES), full(consts["bias_s"]),
        full(consts["conv_w"]), full(consts["gn_attn"]), full(consts["gn_conv"]), full(consts["w_out"]),
        full(consts["w_r"]), full(consts["b_r"]),
        pl.BlockSpec((ts, ts), lambda i, s: (0, 0)),
        pl.BlockSpec((seqs * WINDOW, KV_DIM), lambda i, s: (i, 0)),
        pl.BlockSpec((seqs * WINDOW, KV_DIM), lambda i, s: (i, 0)),
        rows(CONV_DIM), rows(CONV_DIM), full(cnt_in),
    ]
    args = [x, mod, consts["anorm"], consts["fnorm"], consts["w_in"], consts["cos_s"], consts["sin_s"],
            consts["bias_s"], consts["conv_w"], consts["gn_attn"], consts["gn_conv"], consts["w_out"],
            consts["w_r"], consts["b_r"], consts["tri"], ck, cv, e1, e2, cnt_in]
    nt = n_s // ts
    out_shape = [
        jax.ShapeDtypeStruct((n_s, D_MODEL), jnp.float32),
        jax.ShapeDtypeStruct((n_s, D_MODEL // 2), jnp.uint32),
        jax.ShapeDtypeStruct((nt, 8, ts), jnp.int32),
        jax.ShapeDtypeStruct((nt, 8, ts), jnp.float32),
        jax.ShapeDtypeStruct((N_EXPERTS, LANES), jnp.float32),
        jax.ShapeDtypeStruct(ck.shape, jnp.float32),
        jax.ShapeDtypeStruct(cv.shape, jnp.float32),
        jax.ShapeDtypeStruct((n_s, CONV_DIM), jnp.float32),
    ]
    out_specs = [
        rows(D_MODEL),
        rows(D_MODEL // 2),
        pl.BlockSpec((None, 8, ts), lambda i, s: (i, 0, 0)),
        pl.BlockSpec((None, 8, ts), lambda i, s: (i, 0, 0)),
        pl.BlockSpec((N_EXPERTS, LANES), lambda i, s: (0, 0)),
        pl.BlockSpec((seqs * WINDOW, KV_DIM), lambda i, s: (i, 0)),
        pl.BlockSpec((seqs * WINDOW, KV_DIM), lambda i, s: (i, 0)),
        rows(CONV_DIM),
    ]
    scratch = [
        pltpu.VMEM((ts, ATTN_DIM), jnp.bfloat16),
        pltpu.VMEM((ts, KV_DIM), jnp.float32), pltpu.VMEM((ts, KV_DIM), jnp.float32),
        pltpu.VMEM((ts, ATTN_DIM), jnp.float32),
        pltpu.VMEM((N_EXPERTS, LANES), jnp.float32),
    ]
    return pl.pallas_call(
        functools.partial(_sample_mixer_kernel, dec=dec),
        grid_spec=pltpu.PrefetchScalarGridSpec(
            num_scalar_prefetch=1, grid=(nt,), in_specs=in_specs, out_specs=out_specs, scratch_shapes=scratch),
        out_shape=out_shape,
        compiler_params=pltpu.CompilerParams(dimension_semantics=("arbitrary",), vmem_limit_bytes=VMEM_LIMIT),
        name="sample_mixer",
    )(sinks, *args)


def _dispatch(idx, h2, n_rows):
    mesh = plsc.VectorSubcoreMesh(core_axis_name="core", subcore_axis_name="subcore")
    rows, window = SC_ROWS, SC_INDEX_WINDOW
    sub = window // rows
    n_tok, width = h2.shape
    assert n_tok % rows == 0 and (2 * n_tok) % window == 0

    moved = 3 * n_tok * width * h2.dtype.itemsize
    @pl.kernel(out_type=jax.ShapeDtypeStruct((n_rows, width), h2.dtype), mesh=mesh, scratch_types=[],
               cost_estimate=pl.CostEstimate(flops=0, transcendentals=0, bytes_accessed=moved))
    def scatter(h_hbm, i_hbm, xs_hbm):
        def body(indices, h_vmem, i_vmem):
            pltpu.sync_copy(h_vmem, xs_hbm.at[i_vmem.at[0, pl.ds(indices[1] * rows, rows)]])

        pltpu.emit_pipeline(
            body, grid=(2 * n_tok // window, sub),
            in_specs=[pl.BlockSpec((rows, width), lambda i, j: ((i * sub + j) % (n_tok // rows), 0)),
                      pl.BlockSpec((1, window), lambda i, j: (0, i))],
            out_specs=[], core_axis_name=("core", "subcore"),
            dimension_semantics=(pltpu.PARALLEL, pltpu.ARBITRARY), _explicit_indices=True,
        )(h_hbm, i_hbm)

    return scatter(h2, idx)


def _scatter_rows_kernel(dest_ref, h_ref, xs_in_ref, xs_ref, sem):
    del xs_in_ref
    n_tok = h_ref.shape[0]

    def issue(i, carry):
        for k in range(2):
            d = dest_ref[k * n_tok + i]
            pltpu.make_async_copy(h_ref.at[pl.ds(i, 1)], xs_ref.at[pl.ds(d, 1)], sem.at[0]).start(priority=k)
        return carry

    lax.fori_loop(0, n_tok, issue, 0, unroll=8)
    for _ in range(2):
        pltpu.make_async_copy(h_ref, xs_ref.at[pl.ds(0, n_tok)], sem.at[0]).wait()


def _scatter_rows(dest, h2, xs):
    return pl.pallas_call(
        _scatter_rows_kernel,
        grid_spec=pltpu.PrefetchScalarGridSpec(
            num_scalar_prefetch=1, grid=(1,),
            in_specs=[pl.BlockSpec(h2.shape, lambda i, d: (0, 0)), pl.BlockSpec(memory_space=pl.ANY)],
            out_specs=pl.BlockSpec(memory_space=pl.ANY),
            scratch_shapes=[pltpu.SemaphoreType.DMA((1,))]),
        out_shape=jax.ShapeDtypeStruct(xs.shape, xs.dtype),
        input_output_aliases={2: 0},
        compiler_params=pltpu.CompilerParams(dimension_semantics=("arbitrary",)),
        name="scatter_rows",
    )(dest, h2, xs)


def _experts_kernel(be_ref, nu_ref, nv_ref, nxt_ref, par_ref, phys_ref, xs_ref, wg_ref, wu_ref, wd_ref, y_ref,
                    wg_s, wu_s, wd_s, wg_f, wu_f, wd_f, sem):
    del phys_ref
    j = pl.program_id(0)
    used = j < nu_ref[0]
    changed = jnp.logical_or(j == 0, be_ref[j] != be_ref[jnp.maximum(j - 1, 0)])

    def weight_copies(e, slot):
        return [pltpu.make_async_copy(src.at[e], dst.at[slot], sem.at[slot])
                for src, dst in ((wg_ref, wg_f), (wu_ref, wu_f), (wd_ref, wd_f))]

    @pl.when(jnp.logical_and(used, changed))
    def _():
        slot = par_ref[j]

        @pl.when(j == 0)
        def _():
            for c in weight_copies(be_ref[0], slot):
                c.start()

        for c in weight_copies(be_ref[j], slot):
            c.wait()

        @pl.when(nxt_ref[j] >= 0)
        def _():
            for c in weight_copies(nxt_ref[j], 1 - slot):
                c.start()

        wg_s[...] = wg_f[slot].astype(jnp.bfloat16)
        wu_s[...] = wu_f[slot].astype(jnp.bfloat16)
        wd_s[...] = wd_f[slot].astype(jnp.bfloat16)

    @pl.when(used)
    def _():
        live = lax.broadcasted_iota(jnp.int32, (xs_ref.shape[0], 1), 0) < nv_ref[j]
        x = _unpack_bf16_pairs(jnp.where(live, xs_ref[...], jnp.uint32(0)))
        g = jnp.dot(x, wg_s[...], preferred_element_type=jnp.float32)
        u = jnp.dot(x, wu_s[...], preferred_element_type=jnp.float32)
        a = (g * jax.nn.sigmoid(g) * u).astype(jnp.bfloat16)
        y = jnp.dot(a, wd_s[...], preferred_element_type=jnp.float32)
        y_ref[...] = _pack_bf16_pairs(y.astype(jnp.bfloat16)).reshape(y_ref.shape)


def _experts(block_e, n_used, n_valid, next_e, parity, phys, xs, w_gate, w_up, w_down):
    bm = EXPERT_BLOCK
    n_blocks = xs.shape[0] // bm
    hbm = pl.BlockSpec(memory_space=pl.ANY)
    return pl.pallas_call(
        _experts_kernel,
        grid_spec=pltpu.PrefetchScalarGridSpec(
            num_scalar_prefetch=6, grid=(n_blocks,),
            in_specs=[pl.BlockSpec((bm, D_MODEL // 2), lambda j, be, nu, nv, nx, pa, ph: (ph[j], 0)),
                      hbm, hbm, hbm],
            out_specs=pl.BlockSpec((bm, 1, D_MODEL // 2), lambda j, be, nu, nv, nx, pa, ph: (ph[j], 0, 0)),
            scratch_shapes=[pltpu.VMEM((D_MODEL, EXPERT_HIDDEN), jnp.bfloat16),
                            pltpu.VMEM((D_MODEL, EXPERT_HIDDEN), jnp.bfloat16),
                            pltpu.VMEM((EXPERT_HIDDEN, D_MODEL), jnp.bfloat16),
                            pltpu.VMEM((2, D_MODEL, EXPERT_HIDDEN), jnp.float32),
                            pltpu.VMEM((2, D_MODEL, EXPERT_HIDDEN), jnp.float32),
                            pltpu.VMEM((2, EXPERT_HIDDEN, D_MODEL), jnp.float32),
                            pltpu.SemaphoreType.DMA((2,))]),
        out_shape=jax.ShapeDtypeStruct((xs.shape[0], 1, D_MODEL // 2), jnp.uint32),
        compiler_params=pltpu.CompilerParams(dimension_semantics=("arbitrary",), vmem_limit_bytes=VMEM_LIMIT),
        name="experts",
    )(block_e, n_used, n_valid, next_e, parity, phys, xs, w_gate, w_up, w_down)


def _combine_kernel(dest_ref, x1_ref, gate_ref, rf_ref, fin_ref, yb_ref, o_ref, buf_a, buf_b, y_s, sem, *,
                    n_tok, per_token_gate):
    g = pl.program_id(0)
    n = pl.num_programs(0)
    tc = COMBINE_TILE

    def issue(half_index, buf, s):
        base = half_index * tc
        for r in range(tc):
            for k in range(2):
                d = dest_ref[k * n_tok + base + r]
                pltpu.make_async_copy(yb_ref.at[pl.ds(d, 1)], buf.at[k, pl.ds(r, 1)], sem.at[s]).start(priority=k)

    def wait(buf, s):
        for k in range(2):
            pltpu.make_async_copy(yb_ref.at[pl.ds(0, tc)], buf.at[k], sem.at[s]).wait()

    def combine(buf, rows):
        gates = jnp.transpose(rf_ref[:, rows])
        y_s[0] = buf[0].reshape(tc, D_MODEL // 2)
        y_s[1] = buf[1].reshape(tc, D_MODEL // 2)
        moe = _unpack_pairs_f32(y_s[0]) * gates[:, 0:1] + _unpack_pairs_f32(y_s[1]) * gates[:, 1:2]
        gate_f = gate_ref[rows, :] if per_token_gate else gate_ref[...][5:6]
        o_ref[rows, :] = _rms(x1_ref[rows, :] + gate_f * moe, fin_ref[...])

    @pl.when(g == 0)
    def _():
        issue(0, buf_a, 0)

    wait(buf_a, 0)
    issue(2 * g + 1, buf_b, 1)
    combine(buf_a, slice(0, tc))
    wait(buf_b, 1)
    issue(2 * jnp.minimum(g + 1, n - 1), buf_a, 0)
    combine(buf_b, slice(tc, 2 * tc))

    @pl.when(g == n - 1)
    def _():
        wait(buf_a, 0)


def _combine(dest_flat, x1, gate, rf, final_norm, yb, *, tiles_per_gate):
    n_tok = x1.shape[0]
    tc = COMBINE_TILE
    assert n_tok % (2 * tc) == 0
    per_token = tiles_per_gate == 0
    if per_token:
        gate_spec = pl.BlockSpec((2 * tc, D_MODEL), lambda i, d: (i, 0))
    else:
        gate_spec = pl.BlockSpec((None, 6, D_MODEL), lambda i, d: (i // tiles_per_gate, 0, 0))
    return pl.pallas_call(
        functools.partial(_combine_kernel, n_tok=n_tok, per_token_gate=per_token),
        grid_spec=pltpu.PrefetchScalarGridSpec(
            num_scalar_prefetch=1, grid=(n_tok // (2 * tc),),
            in_specs=[pl.BlockSpec((2 * tc, D_MODEL), lambda i, d: (i, 0)),
                      gate_spec,
                      pl.BlockSpec((8, 2 * tc), lambda i, d: (0, i)),
                      pl.BlockSpec((1, D_MODEL), lambda i, d: (0, 0)),
                      pl.BlockSpec(memory_space=pl.ANY)],
            out_specs=pl.BlockSpec((2 * tc, D_MODEL), lambda i, d: (i, 0)),
            scratch_shapes=[pltpu.VMEM((2, tc, 1, D_MODEL // 2), jnp.uint32),
                            pltpu.VMEM((2, tc, 1, D_MODEL // 2), jnp.uint32),
                            pltpu.VMEM((2, tc, D_MODEL // 2), jnp.uint32), pltpu.SemaphoreType.DMA((2,))]),
        out_shape=jax.ShapeDtypeStruct((n_tok, D_MODEL), jnp.float32),
        compiler_params=pltpu.CompilerParams(dimension_semantics=("arbitrary",), vmem_limit_bytes=VMEM_LIMIT),
        name="combine",
    )(dest_flat, x1, gate, rf, final_norm, yb)


def _rope_tables(pos):
    half = HEAD_DIM // 2
    inv_freq = np.power(np.float32(ROPE_THETA), -(np.arange(half, dtype=np.float32) / np.float32(half)))
    ang = (pos.astype(np.float32)[:, None] * inv_freq[None, :].astype(np.float32)).astype(np.float64)
    cos, sin = np.cos(ang).astype(np.float32), np.sin(ang).astype(np.float32)
    cos = np.concatenate([cos, cos, cos, cos], axis=1)
    sin = np.concatenate([-sin, sin, -sin, sin], axis=1)
    return cos, sin


def _prompt_bias():
    r = np.arange(WINDOW)[:, None]
    j = np.arange(2 * WINDOW)[None, :]
    band = (j > r) & (j <= r + WINDOW)
    later = np.where(band, 0.0, NEG).astype(np.float32)
    first = np.where(band & (j >= WINDOW), 0.0, NEG).astype(np.float32)
    return jnp.asarray(np.stack([first, later]))


def _sample_bias(dec):
    cb = SAMPLE_CHUNK
    qb = np.repeat(np.arange(cb), dec)[:, None]
    qi = np.tile(np.arange(dec), cb)[:, None]
    kb = np.concatenate([np.repeat(np.arange(cb), WINDOW), np.repeat(np.arange(cb), dec)])[None, :]
    kj = np.concatenate([np.tile(np.arange(WINDOW), cb), np.tile(np.arange(dec), cb)])[None, :]
    is_new = (np.arange(cb * WINDOW + cb * dec) >= cb * WINDOW)[None, :]
    ok = (kb == qb) & np.where(is_new, kj <= qi, kj > qi)
    ok = np.pad(ok, ((0, 0), (0, SAMPLE_KEYS - ok.shape[1])))
    return jnp.asarray(np.where(ok, 0.0, NEG).astype(np.float32))


def kernel(x_prompt, x_sample, cache_k, cache_v, state_conv, c_prompt, c_sample, attn_norm, ffn_norm, w_mod, b_mod, w_in, conv_w, attn_sinks, out_norm_attn, out_norm_conv, w_out, w_group, b_group, w_expert, b_expert, w_gate, w_up, w_down, final_norm):
    assert attn_norm.shape[0] == 1, "one layer"
    bsz, seq, _ = x_prompt.shape
    nseq, dec, _ = x_sample.shape
    m_prompt = bsz * seq
    n_s = nseq * dec
    m_total = m_prompt + n_s
    tm = MIXER_TILE
    assert seq % tm == 0 and n_s % SAMPLE_TILE == 0 and SAMPLE_TILE % (SAMPLE_CHUNK * dec) == 0
    assert cache_k.shape[2] == WINDOW and m_prompt % SAMPLE_TILE == 0 and m_total % COMBINE_TILE == 0
    bf = jnp.bfloat16

    w_r = jnp.zeros((ROUTER_ROWS, D_MODEL), jnp.float32)
    w_r = w_r.at[0:N_GROUPS].set(w_group[0].T).at[8:8 + N_EXPERTS].set(w_expert[0].T).astype(bf)
    b_r = jnp.zeros((ROUTER_ROWS,), jnp.float32).at[0:N_GROUPS].set(b_group[0]).at[8:8 + N_EXPERTS].set(b_expert[0])
    b_r = jnp.broadcast_to(b_r[:, None], (ROUTER_ROWS, LANES))
    cos_p, sin_p = _rope_tables(np.arange(seq))
    cos_s, sin_s = _rope_tables(PAST_LEN + np.arange(dec))
    tri = jnp.asarray(np.triu(np.ones((tm, tm), np.float32), 1)).astype(bf)
    consts = dict(
        anorm=attn_norm[0][None], fnorm=ffn_norm[0][None], w_in=w_in[0].astype(bf),
        cos_p=jnp.asarray(cos_p), sin_p=jnp.asarray(sin_p),
        cos_s=jnp.asarray(np.tile(cos_s, (nseq, 1))), sin_s=jnp.asarray(np.tile(sin_s, (nseq, 1))),
        bias_p=_prompt_bias(), bias_s=_sample_bias(dec), conv_w=conv_w[0],
        gn_attn=out_norm_attn[0][None], gn_conv=out_norm_conv[0][None], w_out=w_out[0].astype(bf),
        w_r=w_r, b_r=b_r, tri=tri)
    sinks = attn_sinks[0]

    mod = _adaln(jnp.concatenate([c_prompt, c_sample], axis=0), w_mod[0], b_mod[0])
    mod_p = mod[:bsz].reshape(bsz, 6, D_MODEL)
    mod_s = jnp.repeat(mod[bsz:], dec, axis=0)

    bm = EXPERT_BLOCK
    eids = jnp.arange(N_EXPERTS, dtype=jnp.int32)
    lookup = lambda table, e: jnp.sum(jnp.where(e[..., None] == eids, table, 0), axis=-1)
    by_token = lambda a: jnp.transpose(a, (1, 0, 2)).reshape(8, -1)
    ceil_bm = lambda c: (c + bm - 1) // bm * bm
    x1_p, h2_p, ri_p, rf_p, cnt_p, k_p, v_p, u_p = _prompt_mixer(x_prompt, mod_p, sinks, consts)
    cnt_p = cnt_p[:, 0].astype(jnp.int32)
    padded_p = ceil_bm(cnt_p)
    pstart_p = jnp.cumsum(padded_p) - padded_p
    nb_p = -(-(2 * m_prompt + N_EXPERTS * (bm - 1)) // bm)
    nb_o = -(-(2 * n_s + N_EXPERTS * (bm - 1)) // bm)
    n_blocks = nb_p + nb_o
    ri_p = by_token(ri_p)
    dest_p = (lookup(pstart_p, ri_p[0:2]) + ri_p[2:4]).reshape(2 * m_prompt)
    xs = _dispatch(dest_p[None], h2_p, n_blocks * bm)

    st = state_conv[0]
    zrow = jnp.zeros((nseq, 1, CONV_DIM), jnp.float32)
    e1 = jnp.concatenate([st[:, 1:2]] + [zrow] * (dec - 1), axis=1).reshape(n_s, CONV_DIM)
    e2 = jnp.concatenate([st[:, 0:1], st[:, 1:2]] + [zrow] * (dec - 2), axis=1).reshape(n_s, CONV_DIM)
    ck = cache_k[0].reshape(nseq * WINDOW, KV_DIM)
    cv = cache_v[0].reshape(nseq * WINDOW, KV_DIM)
    x1_s, h2_s, ri_s, rf_s, cnt_s, k_s, v_s, u_s = _sample_mixer(
        x_sample.reshape(n_s, D_MODEL), mod_s, sinks, consts, ck, cv, e1, e2,
        jnp.zeros((N_EXPERTS, LANES), jnp.float32), dec)

    cnt_s = cnt_s[:, 0].astype(jnp.int32)
    fit = jnp.minimum(cnt_s, padded_p - cnt_p)
    over = cnt_s - fit
    padded_o = ceil_bm(over)
    ostart = nb_p * bm + jnp.cumsum(padded_o) - padded_o
    ri_s = by_token(ri_s)
    e_s, r_s = ri_s[0:2], ri_s[2:4]
    dest_s = jnp.where(r_s < lookup(fit, e_s), lookup(pstart_p + cnt_p, e_s) + r_s, lookup(ostart - fit, e_s) + r_s)
    dest_s = dest_s.reshape(2 * n_s)
    xs = _scatter_rows(dest_s, h2_s, xs)

    np_e, no_e = padded_p // bm, padded_o // bm
    lend = jnp.cumsum(np_e + no_e)
    lstart = lend - np_e - no_e
    n_used = lend[-1].astype(jnp.int32)
    blk = jnp.arange(n_blocks, dtype=jnp.int32)
    present = np_e + no_e > 0
    last_e = jnp.max(jnp.where(present, eids, 0))
    block_e = jnp.sum((lend[None, :] <= blk[:, None]).astype(jnp.int32), axis=1)
    block_e = jnp.where(blk < n_used, jnp.minimum(block_e, N_EXPERTS - 1), last_e)
    of_block = lambda table: lookup(table, block_e)
    within = blk - of_block(lstart)
    in_prompt = within < of_block(np_e)
    phys = jnp.where(in_prompt, of_block(pstart_p) // bm + within, of_block(ostart) // bm + within - of_block(np_e))
    n_valid = jnp.where(in_prompt, of_block(cnt_p + fit) - within * bm, of_block(over) - (within - of_block(np_e)) * bm)
    n_valid = jnp.where(blk < n_used, jnp.clip(n_valid, 0, bm), 0).astype(jnp.int32)
    phys = jnp.where(blk < n_used, phys, jnp.sum(jnp.where(blk == n_used - 1, phys, 0))).astype(jnp.int32)
    later = jnp.logical_and(present[None, :], eids[None, :] > eids[:, None])
    next_of = jnp.min(jnp.where(later, eids[None, :], N_EXPERTS), axis=1)
    next_of = jnp.where(next_of == N_EXPERTS, -1, next_of)
    next_e = of_block(next_of).astype(jnp.int32)
    parity = of_block((jnp.cumsum(present.astype(jnp.int32)) - 1) % 2).astype(jnp.int32)
    yb = _experts(block_e, n_used.reshape(1), n_valid, next_e, parity, phys, xs, w_gate[0], w_up[0], w_down[0])

    fin = final_norm[None]
    y_p = _combine(dest_p, x1_p, mod_p, by_token(rf_p), fin, yb, tiles_per_gate=seq // (2 * COMBINE_TILE))
    gate_s = mod_s[:, 5 * D_MODEL:]
    y_s = _combine(dest_s, x1_s, gate_s, by_token(rf_s), fin, yb, tiles_per_gate=0)

    keep = WINDOW
    new_k_p = k_p.reshape(1, bsz, keep, N_KV_HEADS, HEAD_DIM)
    new_v_p = v_p.reshape(1, bsz, keep, N_KV_HEADS, HEAD_DIM)
    new_u_p = u_p[:, 6:8][None]
    new_k_s = k_s.reshape(1, nseq, keep, N_KV_HEADS, HEAD_DIM)
    new_v_s = v_s.reshape(1, nseq, keep, N_KV_HEADS, HEAD_DIM)
    u_all = jnp.concatenate([st, u_s.reshape(nseq, dec, CONV_DIM)], axis=1)
    new_u_s = u_all[:, -2:][None]
    return (y_p.reshape(bsz, seq, D_MODEL), y_s.reshape(nseq, dec, D_MODEL),
            new_k_p, new_v_p, new_u_p, new_k_s, new_v_s, new_u_s)
```

```python
import functools

import jax
import jax.numpy as jnp
import numpy as np
from jax import lax
from jax.experimental import pallas as pl
from jax.experimental.pallas import tpu as pltpu
from jax.experimental.pallas import tpu_sc as plsc

D_MODEL = 1024
HEAD_DIM = 64
ATTN_DIM = 512
N_Q_HEADS = 8
N_KV_HEADS = 2
KV_DIM = 128
CONV_DIM = 512
WINDOW = 128
IN_PROJ_DIM = ATTN_DIM + 2 * KV_DIM + 3 * CONV_DIM
N_GROUPS = 4
EXPERTS_PER_GROUP = 8
N_EXPERTS = 32
EXPERT_HIDDEN = 512
RMS_EPS = 1e-5
ROPE_THETA = 10000.0
PAST_LEN = 16384

LANES = 128
ROUTER_ROWS = 48
NEG = -1e30

MIXER_TILE = 512
SAMPLE_TILE = 128
SAMPLE_CHUNK = 8
SAMPLE_KEYS = 1152
EXPERT_BLOCK = 512
COMBINE_TILE = 512
SC_ROWS = 64
SC_INDEX_WINDOW = 128
VMEM_LIMIT = 56 * 1024 * 1024


def _rms(x, gain):
    ms = jnp.mean(x * x, axis=-1, keepdims=True)
    return x * lax.rsqrt(ms + RMS_EPS) * gain


def _half_lane_mask(rows=1):
    return lax.broadcasted_iota(jnp.int32, (rows, LANES), 1) < HEAD_DIM


def _rope(x, cos, sin_signed):
    t, w = x.shape
    reps = w // LANES
    lane = lax.broadcasted_iota(jnp.int32, (t, w), 1)
    upper = (lane % HEAD_DIM) >= (HEAD_DIM // 2)
    partner = jnp.where(upper, pltpu.roll(x, HEAD_DIM // 2, 1), pltpu.roll(x, w - HEAD_DIM // 2, 1))
    if reps > 1:
        cos = jnp.concatenate([cos] * reps, axis=1)
        sin_signed = jnp.concatenate([sin_signed] * reps, axis=1)
    return x * cos + partner * sin_signed


def _swap_halves(x):
    return pltpu.roll(x, HEAD_DIM, 1)


def _group_norm64(x, gain):
    t, w = x.shape
    lo = _half_lane_mask(t)
    outs = []
    for c in range(w // LANES):
        xc = x[:, c * LANES:(c + 1) * LANES]
        sq = xc * xc
        s_lo = jnp.sum(jnp.where(lo, sq, 0.0), axis=-1, keepdims=True)
        s_hi = jnp.sum(jnp.where(lo, 0.0, sq), axis=-1, keepdims=True)
        r = jnp.where(lo, lax.rsqrt(s_lo * (1.0 / HEAD_DIM) + RMS_EPS), lax.rsqrt(s_hi * (1.0 / HEAD_DIM) + RMS_EPS))
        outs.append(xc * r)
    return jnp.concatenate(outs, axis=1) * gain


def _attend(q_rows, keys, keys_sw, vals, vals_sw, bias, sinks_ref, group):
    r = q_rows.shape[0]
    lo = _half_lane_mask(r)
    outs = []
    for parity in range(2):
        heads = [4 * group + parity, 4 * group + parity + 2]
        kk, vv = (keys, vals) if parity == group else (keys_sw, vals_sw)
        half = lo if parity == 0 else jnp.logical_not(lo)
        qs = []
        for h in heads:
            pair = q_rows[:, (h // 2) * LANES:(h // 2 + 1) * LANES]
            qs.append(jnp.where(half, pair, jnp.zeros_like(pair)))
        qcat = jnp.concatenate(qs, axis=0)
        s = lax.dot_general(qcat, kk, (((1,), (1,)), ((), ())), preferred_element_type=jnp.float32)
        ps, dens = [], []
        for i, h in enumerate(heads):
            sh = s[i * r:(i + 1) * r] + bias
            sink = sinks_ref[h]
            m = jnp.maximum(jnp.max(sh, axis=-1, keepdims=True), sink)
            p = jnp.exp(sh - m)
            dens.append(jnp.sum(p, axis=-1, keepdims=True) + jnp.exp(sink - m))
            ps.append(p.astype(jnp.bfloat16))
        o = jnp.dot(jnp.concatenate(ps, axis=0), vv, preferred_element_type=jnp.float32)
        for i, h in enumerate(heads):
            oh = o[i * r:(i + 1) * r] / dens[i]
            ms = jnp.sum(jnp.where(half, oh * oh, 0.0), axis=-1, keepdims=True) * (1.0 / HEAD_DIM)
            outs.append((h, oh * lax.rsqrt(ms + RMS_EPS)))
    outs.sort(key=lambda t: t[0])
    return [o for _, o in outs]


def _merge_heads(head_outs, lo):
    pairs = [jnp.where(lo, head_outs[2 * j], head_outs[2 * j + 1]) for j in range(4)]
    return jnp.concatenate(pairs, axis=1)


def _route(h2b, wr_ref, br_ref, tri_ref, cnt_ref, ri_ref, rf_ref):
    t = h2b.shape[0]
    lg = lax.dot_general(wr_ref[...], h2b, (((1,), (1,)), ((), ())), preferred_element_type=jnp.float32)
    lg = lg + br_ref[...][:, 0:1]
    r8 = lax.broadcasted_iota(jnp.int32, (8, t), 0)
    lgrp = jnp.where(r8 < N_GROUPS, lg[0:8], -jnp.inf)
    gmax = jnp.max(lgrp, axis=0, keepdims=True)
    grp = jnp.min(jnp.where(lgrp == gmax, r8, 8), axis=0, keepdims=True)
    pg_sel = 1.0 / jnp.sum(jnp.exp(lgrp - gmax), axis=0, keepdims=True)
    r32 = lax.broadcasted_iota(jnp.int32, (N_EXPERTS, t), 0)
    le = jnp.where((r32 // EXPERTS_PER_GROUP) == grp, lg[8:8 + N_EXPERTS], -jnp.inf)
    v1 = jnp.max(le, axis=0, keepdims=True)
    i1 = jnp.min(jnp.where(le == v1, r32, N_EXPERTS), axis=0, keepdims=True)
    le2 = jnp.where(r32 == i1, -jnp.inf, le)
    v2 = jnp.max(le2, axis=0, keepdims=True)
    i2 = jnp.min(jnp.where(le2 == v2, r32, N_EXPERTS), axis=0, keepdims=True)
    e21 = jnp.exp(v2 - v1)
    gate1 = pg_sel / (1.0 + e21)
    gate2 = pg_sel * e21 / (1.0 + e21)
    hot1 = r32 == i1
    hot2 = r32 == i2
    onehot = jnp.where(jnp.logical_or(hot1, hot2), 1.0, 0.0)
    before = jnp.dot(onehot.astype(jnp.bfloat16), tri_ref[...], preferred_element_type=jnp.float32)
    pos = before + cnt_ref[...][:, 0:1]
    rank1 = jnp.sum(jnp.where(hot1, pos, 0.0), axis=0, keepdims=True)
    rank2 = jnp.sum(jnp.where(hot2, pos, 0.0), axis=0, keepdims=True)
    cnt_ref[...] = cnt_ref[...] + jnp.sum(onehot, axis=1, keepdims=True)
    zi = jnp.zeros((4, t), jnp.int32)
    ri_ref[...] = jnp.concatenate([i1, i2, rank1.astype(jnp.int32), rank2.astype(jnp.int32), zi], axis=0)
    rf_ref[...] = jnp.concatenate([gate1, gate2, jnp.zeros((6, t), jnp.float32)], axis=0)


def _pack_bf16_pairs(xb):
    w = xb.shape[1] // 2
    bits = pltpu.bitcast(xb.astype(jnp.float32), jnp.uint32)
    return (bits[:, :w] >> 16) | (bits[:, w:] & jnp.uint32(0xFFFF0000))


def _unpack_pairs_f32(words):
    lo = pltpu.bitcast(words << 16, jnp.float32)
    hi = pltpu.bitcast(words & jnp.uint32(0xFFFF0000), jnp.float32)
    return jnp.concatenate([lo, hi], axis=1)


def _unpack_bf16_pairs(words):
    return _unpack_pairs_f32(words).astype(jnp.bfloat16)


def _split_proj(proj):
    a = ATTN_DIM
    q = proj[:, :a]
    k = proj[:, a:a + KV_DIM]
    v = proj[:, a + KV_DIM:a + 2 * KV_DIM]
    c0 = a + 2 * KV_DIM
    h_conv = proj[:, c0:c0 + CONV_DIM]
    gate_b = proj[:, c0 + CONV_DIM:c0 + 2 * CONV_DIM]
    gate_c = proj[:, c0 + 2 * CONV_DIM:c0 + 3 * CONV_DIM]
    return q, k, v, gate_c * h_conv, gate_b


def _mixer_tail(x, attn_n, conv_n, mod, wout_ref, fnorm_ref, wr_ref, br_ref, tri_ref, cnt_ref,
                x1_ref, h2_ref, ri_ref, rf_ref):
    gate_a, shift_f, scale_f = mod
    cat = jnp.concatenate([attn_n.astype(jnp.bfloat16), conv_n.astype(jnp.bfloat16)], axis=1)
    mix = jnp.dot(cat, wout_ref[...], preferred_element_type=jnp.float32)
    x1 = x + gate_a * mix
    x1_ref[...] = x1
    h2 = _rms(x1, fnorm_ref[...] * (1.0 + scale_f)) + shift_f
    h2b = h2.astype(jnp.bfloat16)
    h2_ref[...] = _pack_bf16_pairs(h2b)
    _route(h2b, wr_ref, br_ref, tri_ref, cnt_ref, ri_ref, rf_ref)


def _adaln_kernel(c_ref, w_ref, b_ref, o_ref):
    c = c_ref[...]
    a = (c * jax.nn.sigmoid(c)).astype(jnp.bfloat16)
    o_ref[...] = jnp.dot(a, w_ref[...].astype(jnp.bfloat16), preferred_element_type=jnp.float32) + b_ref[...]


def _adaln(c, w_mod, b_mod):
    n = c.shape[0]
    tn = 1536
    return pl.pallas_call(
        _adaln_kernel,
        grid=(w_mod.shape[1] // tn,),
        in_specs=[pl.BlockSpec((n, D_MODEL), lambda j: (0, 0)),
                  pl.BlockSpec((D_MODEL, tn), lambda j: (0, j)),
                  pl.BlockSpec((1, tn), lambda j: (0, j))],
        out_specs=pl.BlockSpec((n, tn), lambda j: (0, j)),
        out_shape=jax.ShapeDtypeStruct((n, w_mod.shape[1]), jnp.float32),
        compiler_params=pltpu.CompilerParams(dimension_semantics=("arbitrary",), vmem_limit_bytes=VMEM_LIMIT),
        name="adaln",
    )(c, w_mod, b_mod.reshape(1, -1))


def _prompt_mixer_kernel(sinks_ref, x_ref, mod_ref, anorm_ref, fnorm_ref, win_ref, cos_ref, sin_ref, bias_ref,
                         convw_ref, gna_ref, gnc_ref, wout_ref, wr_ref, br_ref, tri_ref,
                         x1_ref, h2_ref, ri_ref, rf_ref, cnt_out_ref, knew_ref, vnew_ref, unew_ref,
                         q_s, k_s, ksw_s, v_s, vsw_s, u_s, att_s, cnt_s):
    b = pl.program_id(0)
    t = pl.program_id(1)
    tm = MIXER_TILE
    nblk = tm // WINDOW

    @pl.when(jnp.logical_and(b == 0, t == 0))
    def _():
        cnt_s[...] = jnp.zeros_like(cnt_s)

    @pl.when(t == 0)
    def _():
        z = jnp.zeros((WINDOW, KV_DIM), jnp.bfloat16)
        k_s[0:WINDOW, :] = z
        ksw_s[0:WINDOW, :] = z
        v_s[0:WINDOW, :] = z
        vsw_s[0:WINDOW, :] = z
        u_s[0:8, :] = jnp.zeros((8, CONV_DIM), jnp.float32)

    x = x_ref[...]
    mod = mod_ref[...]
    h = _rms(x, anorm_ref[...] * (1.0 + mod[1:2])) + mod[0:1]
    proj = jnp.dot(h.astype(jnp.bfloat16), win_ref[...], preferred_element_type=jnp.float32)
    q, k, v, u, gate_b = _split_proj(proj)
    cos = cos_ref[...]
    sin = sin_ref[...]
    q = _rope(q, cos, sin) * (HEAD_DIM ** -0.5)
    k = _rope(k, cos, sin)
    q_s[...] = q.astype(jnp.bfloat16)
    k_s[WINDOW:, :] = k.astype(jnp.bfloat16)
    ksw_s[WINDOW:, :] = _swap_halves(k).astype(jnp.bfloat16)
    v_s[WINDOW:, :] = v.astype(jnp.bfloat16)
    vsw_s[WINDOW:, :] = _swap_halves(v).astype(jnp.bfloat16)
    knew_ref[...] = k[tm - WINDOW:, :]
    vnew_ref[...] = v[tm - WINDOW:, :]

    u_s[8:, :] = u
    unew_ref[...] = u[tm - 8:, :]
    cw = convw_ref[...]
    conv = (u_s[6:6 + tm, :] * cw[0:1] + u_s[7:7 + tm, :] * cw[1:2] + u * cw[2:3]) * gate_b
    u_s[0:8, :] = u[tm - 8:, :]
    conv_n = _group_norm64(conv, gnc_ref[...])

    lo = _half_lane_mask(WINDOW)
    first = t == 0

    def block(i, carry):
        r0 = pl.multiple_of(i * WINDOW, WINDOW)
        bias = bias_ref[jnp.where(jnp.logical_and(first, i == 0), 0, 1)]
        qb = q_s[pl.ds(r0, WINDOW), :]
        keys = k_s[pl.ds(r0, 2 * WINDOW), :]
        keys_sw = ksw_s[pl.ds(r0, 2 * WINDOW), :]
        vals = v_s[pl.ds(r0, 2 * WINDOW), :]
        vals_sw = vsw_s[pl.ds(r0, 2 * WINDOW), :]
        outs = []
        for g in range(N_KV_HEADS):
            outs += _attend(qb, keys, keys_sw, vals, vals_sw, bias, sinks_ref, g)
        att_s[pl.ds(r0, WINDOW), :] = _merge_heads(outs, lo)
        return carry

    lax.fori_loop(0, nblk, block, 0, unroll=True)
    k_s[0:WINDOW, :] = k_s[tm:tm + WINDOW, :]
    ksw_s[0:WINDOW, :] = ksw_s[tm:tm + WINDOW, :]
    v_s[0:WINDOW, :] = v_s[tm:tm + WINDOW, :]
    vsw_s[0:WINDOW, :] = vsw_s[tm:tm + WINDOW, :]

    attn_n = att_s[...] * gna_ref[...]
    _mixer_tail(x, attn_n, conv_n, (mod[2:3], mod[3:4], mod[4:5]), wout_ref, fnorm_ref, wr_ref, br_ref, tri_ref,
                cnt_s, x1_ref, h2_ref, ri_ref, rf_ref)
    cnt_out_ref[...] = cnt_s[...]


def _prompt_mixer(x, mod, sinks, consts):
    bsz, seq, _ = x.shape
    tm = MIXER_TILE
    nt = seq // tm
    full = lambda shape: pl.BlockSpec(shape, lambda b, t, s: (0,) * len(shape))
    in_specs = [
        pl.BlockSpec((None, tm, D_MODEL), lambda b, t, s: (b, t, 0)),
        pl.BlockSpec((None, 6, D_MODEL), lambda b, t, s: (b, 0, 0)),
        full((1, D_MODEL)), full((1, D_MODEL)),
        full((D_MODEL, IN_PROJ_DIM)),
        pl.BlockSpec((tm, LANES), lambda b, t, s: (t, 0)),
        pl.BlockSpec((tm, LANES), lambda b, t, s: (t, 0)),
        full((2, WINDOW, 2 * WINDOW)),
        full((3, CONV_DIM)), full((1, ATTN_DIM)), full((1, CONV_DIM)),
        full((D_MODEL, D_MODEL)),
        full((ROUTER_ROWS, D_MODEL)), full((ROUTER_ROWS, LANES)),
        full((tm, tm)),
    ]
    out_shape = [
        jax.ShapeDtypeStruct((bsz * seq, D_MODEL), jnp.float32),
        jax.ShapeDtypeStruct((bsz * seq, D_MODEL // 2), jnp.uint32),
        jax.ShapeDtypeStruct((bsz * nt, 8, tm), jnp.int32),
        jax.ShapeDtypeStruct((bsz * nt, 8, tm), jnp.float32),
        jax.ShapeDtypeStruct((N_EXPERTS, LANES), jnp.float32),
        jax.ShapeDtypeStruct((bsz, WINDOW, KV_DIM), jnp.float32),
        jax.ShapeDtypeStruct((bsz, WINDOW, KV_DIM), jnp.float32),
        jax.ShapeDtypeStruct((bsz, 8, CONV_DIM), jnp.float32),
    ]
    out_specs = [
        pl.BlockSpec((tm, D_MODEL), lambda b, t, s: (b * nt + t, 0)),
        pl.BlockSpec((tm, D_MODEL // 2), lambda b, t, s: (b * nt + t, 0)),
        pl.BlockSpec((None, 8, tm), lambda b, t, s: (b * nt + t, 0, 0)),
        pl.BlockSpec((None, 8, tm), lambda b, t, s: (b * nt + t, 0, 0)),
        pl.BlockSpec((N_EXPERTS, LANES), lambda b, t, s: (0, 0)),
        pl.BlockSpec((None, WINDOW, KV_DIM), lambda b, t, s: (b, 0, 0)),
        pl.BlockSpec((None, WINDOW, KV_DIM), lambda b, t, s: (b, 0, 0)),
        pl.BlockSpec((None, 8, CONV_DIM), lambda b, t, s: (b, 0, 0)),
    ]
    scratch = [
        pltpu.VMEM((tm, ATTN_DIM), jnp.bfloat16),
        pltpu.VMEM((tm + WINDOW, KV_DIM), jnp.bfloat16), pltpu.VMEM((tm + WINDOW, KV_DIM), jnp.bfloat16),
        pltpu.VMEM((tm + WINDOW, KV_DIM), jnp.bfloat16), pltpu.VMEM((tm + WINDOW, KV_DIM), jnp.bfloat16),
        pltpu.VMEM((tm + 8, CONV_DIM), jnp.float32),
        pltpu.VMEM((tm, ATTN_DIM), jnp.float32),
        pltpu.VMEM((N_EXPERTS, LANES), jnp.float32),
    ]
    return pl.pallas_call(
        _prompt_mixer_kernel,
        grid_spec=pltpu.PrefetchScalarGridSpec(
            num_scalar_prefetch=1, grid=(bsz, nt), in_specs=in_specs, out_specs=out_specs, scratch_shapes=scratch),
        out_shape=out_shape,
        compiler_params=pltpu.CompilerParams(dimension_semantics=("arbitrary", "arbitrary"),
                                             vmem_limit_bytes=VMEM_LIMIT),
        name="prompt_mixer",
    )(sinks, x, mod, consts["anorm"], consts["fnorm"], consts["w_in"], consts["cos_p"], consts["sin_p"],
      consts["bias_p"], consts["conv_w"], consts["gn_attn"], consts["gn_conv"], consts["w_out"],
      consts["w_r"], consts["b_r"], consts["tri"])


def _sample_mixer_kernel(sinks_ref, x_ref, mod_ref, anorm_ref, fnorm_ref, win_ref, cos_ref, sin_ref, bias_ref,
                         convw_ref, gna_ref, gnc_ref, wout_ref, wr_ref, br_ref, tri_ref,
                         ck_ref, cv_ref, e1_ref, e2_ref, cnt_in_ref,
                         x1_ref, h2_ref, ri_ref, rf_ref, cnt_out_ref, knew_ref, vnew_ref, unew_ref,
                         q_s, k_s, v_s, att_s, cnt_s, *, dec):
    n = x_ref.shape[0]
    cb = SAMPLE_CHUNK
    rows = cb * dec
    pad_keys = SAMPLE_KEYS - cb * WINDOW - rows

    @pl.when(pl.program_id(0) == 0)
    def _():
        cnt_s[...] = cnt_in_ref[...]

    x = x_ref[...]
    mod = mod_ref[...]
    md = lambda j: mod[:, j * D_MODEL:(j + 1) * D_MODEL]
    h = _rms(x, anorm_ref[...]) * (1.0 + md(1)) + md(0)
    proj = jnp.dot(h.astype(jnp.bfloat16), win_ref[...], preferred_element_type=jnp.float32)
    q, k, v, u, gate_b = _split_proj(proj)
    cos = cos_ref[...]
    sin = sin_ref[...]
    q = _rope(q, cos, sin) * (HEAD_DIM ** -0.5)
    k = _rope(k, cos, sin)
    q_s[...] = q.astype(jnp.bfloat16)
    k_s[...] = k
    v_s[...] = v
    knew_ref[...] = k
    vnew_ref[...] = v
    unew_ref[...] = u

    pos = lax.broadcasted_iota(jnp.int32, (n, CONV_DIM), 0) % dec
    um1 = jnp.where(pos >= 1, pltpu.roll(u, 1, 0), 0.0) + e1_ref[...]
    um2 = jnp.where(pos >= 2, pltpu.roll(u, 2, 0), 0.0) + e2_ref[...]
    cw = convw_ref[...]
    conv = (um2 * cw[0:1] + um1 * cw[1:2] + u * cw[2:3]) * gate_b
    conv_n = _group_norm64(conv, gnc_ref[...])

    lo = _half_lane_mask(rows)
    bias = bias_ref[...]

    def chunk(c, carry):
        r0 = pl.multiple_of(c * rows, rows)
        c0 = pl.multiple_of(c * (cb * WINDOW), cb * WINDOW)
        zpad = jnp.zeros((pad_keys, KV_DIM), jnp.float32)
        kc = jnp.concatenate([ck_ref[pl.ds(c0, cb * WINDOW), :], k_s[pl.ds(r0, rows), :], zpad], axis=0)
        vc = jnp.concatenate([cv_ref[pl.ds(c0, cb * WINDOW), :], v_s[pl.ds(r0, rows), :], zpad], axis=0)
        keys, keys_sw = kc.astype(jnp.bfloat16), _swap_halves(kc).astype(jnp.bfloat16)
        vals, vals_sw = vc.astype(jnp.bfloat16), _swap_halves(vc).astype(jnp.bfloat16)
        qb = q_s[pl.ds(r0, rows), :]
        outs = []
        for g in range(N_KV_HEADS):
            outs += _attend(qb, keys, keys_sw, vals, vals_sw, bias, sinks_ref, g)
        att_s[pl.ds(r0, rows), :] = _merge_heads(outs, lo)
        return carry

    lax.fori_loop(0, n // rows, chunk, 0)
    attn_n = att_s[...] * gna_ref[...]
    _mixer_tail(x, attn_n, conv_n, (md(2), md(3), md(4)), wout_ref, fnorm_ref, wr_ref, br_ref, tri_ref,
                cnt_s, x1_ref, h2_ref, ri_ref, rf_ref)
    cnt_out_ref[...] = cnt_s[...]


def _sample_mixer(x, mod, sinks, consts, ck, cv, e1, e2, cnt_in, dec):
    n_s = x.shape[0]
    ts = SAMPLE_TILE
    seqs = ts // dec
    full = lambda a: pl.BlockSpec(a.shape, lambda i, s: (0,) * a.ndim)
    rows = lambda width: pl.BlockSpec((ts, width), lambda i, s: (i, 0))
    in_specs = [
        rows(D_MODEL), rows(6 * D_MODEL),
        full(consts["anorm"]), full(consts["fnorm"]), full(consts["w_in"]),
        rows(LANES), rows(LANES), full(consts["bias_s"]),
        full(consts["conv_w"]), full(consts["gn_attn"]), full(consts["gn_conv"]), full(consts["w_out"]),
        full(consts["w_r"]), full(consts["b_r"]),
        pl.BlockSpec((ts, ts), lambda i, s: (0, 0)),
        pl.BlockSpec((seqs * WINDOW, KV_DIM), lambda i, s: (i, 0)),
        pl.BlockSpec((seqs * WINDOW, KV_DIM), lambda i, s: (i, 0)),
        rows(CONV_DIM), rows(CONV_DIM), full(cnt_in),
    ]
    args = [x, mod, consts["anorm"], consts["fnorm"], consts["w_in"], consts["cos_s"], consts["sin_s"],
            consts["bias_s"], consts["conv_w"], consts["gn_attn"], consts["gn_conv"], consts["w_out"],
            consts["w_r"], consts["b_r"], consts["tri"], ck, cv, e1, e2, cnt_in]
    nt = n_s // ts
    out_shape = [
        jax.ShapeDtypeStruct((n_s, D_MODEL), jnp.float32),
        jax.ShapeDtypeStruct((n_s, D_MODEL // 2), jnp.uint32),
        jax.ShapeDtypeStruct((nt, 8, ts), jnp.int32),
        jax.ShapeDtypeStruct((nt, 8, ts), jnp.float32),
        jax.ShapeDtypeStruct((N_EXPERTS, LANES), jnp.float32),
        jax.ShapeDtypeStruct((n_s, KV_DIM), jnp.float32),
        jax.ShapeDtypeStruct((n_s, KV_DIM), jnp.float32),
        jax.ShapeDtypeStruct((n_s, CONV_DIM), jnp.float32),
    ]
    out_specs = [
        rows(D_MODEL),
        rows(D_MODEL // 2),
        pl.BlockSpec((None, 8, ts), lambda i, s: (i, 0, 0)),
        pl.BlockSpec((None, 8, ts), lambda i, s: (i, 0, 0)),
        pl.BlockSpec((N_EXPERTS, LANES), lambda i, s: (0, 0)),
        rows(KV_DIM), rows(KV_DIM), rows(CONV_DIM),
    ]
    scratch = [
        pltpu.VMEM((ts, ATTN_DIM), jnp.bfloat16),
        pltpu.VMEM((ts, KV_DIM), jnp.float32), pltpu.VMEM((ts, KV_DIM), jnp.float32),
        pltpu.VMEM((ts, ATTN_DIM), jnp.float32),
        pltpu.VMEM((N_EXPERTS, LANES), jnp.float32),
    ]
    return pl.pallas_call(
        functools.partial(_sample_mixer_kernel, dec=dec),
        grid_spec=pltpu.PrefetchScalarGridSpec(
            num_scalar_prefetch=1, grid=(nt,), in_specs=in_specs, out_specs=out_specs, scratch_shapes=scratch),
        out_shape=out_shape,
        compiler_params=pltpu.CompilerParams(dimension_semantics=("arbitrary",), vmem_limit_bytes=VMEM_LIMIT),
        name="sample_mixer",
    )(sinks, *args)


def _dispatch(idx, h2, n_rows):
    mesh = plsc.VectorSubcoreMesh(core_axis_name="core", subcore_axis_name="subcore")
    rows, window = SC_ROWS, SC_INDEX_WINDOW
    sub = window // rows
    n_tok, width = h2.shape
    assert n_tok % rows == 0 and (2 * n_tok) % window == 0

    moved = 3 * n_tok * width * h2.dtype.itemsize
    @pl.kernel(out_type=jax.ShapeDtypeStruct((n_rows, width), h2.dtype), mesh=mesh, scratch_types=[],
               cost_estimate=pl.CostEstimate(flops=0, transcendentals=0, bytes_accessed=moved))
    def scatter(h_hbm, i_hbm, xs_hbm):
        def body(indices, h_vmem, i_vmem):
            pltpu.sync_copy(h_vmem, xs_hbm.at[i_vmem.at[0, pl.ds(indices[1] * rows, rows)]])

        pltpu.emit_pipeline(
            body, grid=(2 * n_tok // window, sub),
            in_specs=[pl.BlockSpec((rows, width), lambda i, j: ((i * sub + j) % (n_tok // rows), 0)),
                      pl.BlockSpec((1, window), lambda i, j: (0, i))],
            out_specs=[], core_axis_name=("core", "subcore"),
            dimension_semantics=(pltpu.PARALLEL, pltpu.ARBITRARY), _explicit_indices=True,
        )(h_hbm, i_hbm)

    return scatter(h2, idx)


def _scatter_rows_kernel(dest_ref, h_ref, xs_in_ref, xs_ref, sem):
    del xs_in_ref
    n_tok = h_ref.shape[0]

    def issue(i, carry):
        for k in range(2):
            d = dest_ref[k * n_tok + i]
            pltpu.make_async_copy(h_ref.at[pl.ds(i, 1)], xs_ref.at[pl.ds(d, 1)], sem.at[0]).start(priority=k)
        return carry

    lax.fori_loop(0, n_tok, issue, 0, unroll=8)
    for _ in range(2):
        pltpu.make_async_copy(h_ref, xs_ref.at[pl.ds(0, n_tok)], sem.at[0]).wait()


def _scatter_rows(dest, h2, xs):
    return pl.pallas_call(
        _scatter_rows_kernel,
        grid_spec=pltpu.PrefetchScalarGridSpec(
            num_scalar_prefetch=1, grid=(1,),
            in_specs=[pl.BlockSpec(h2.shape, lambda i, d: (0, 0)), pl.BlockSpec(memory_space=pl.ANY)],
            out_specs=pl.BlockSpec(memory_space=pl.ANY),
            scratch_shapes=[pltpu.SemaphoreType.DMA((1,))]),
        out_shape=jax.ShapeDtypeStruct(xs.shape, xs.dtype),
        input_output_aliases={2: 0},
        compiler_params=pltpu.CompilerParams(dimension_semantics=("arbitrary",)),
        name="scatter_rows",
    )(dest, h2, xs)


def _experts_kernel(be_ref, nu_ref, nv_ref, nxt_ref, par_ref, phys_ref, xs_ref, wg_ref, wu_ref, wd_ref, y_ref,
                    wg_s, wu_s, wd_s, wg_f, wu_f, wd_f, sem):
    del phys_ref
    j = pl.program_id(0)
    used = j < nu_ref[0]
    changed = jnp.logical_or(j == 0, be_ref[j] != be_ref[jnp.maximum(j - 1, 0)])

    def weight_copies(e, slot):
        return [pltpu.make_async_copy(src.at[e], dst.at[slot], sem.at[slot])
                for src, dst in ((wg_ref, wg_f), (wu_ref, wu_f), (wd_ref, wd_f))]

    @pl.when(jnp.logical_and(used, changed))
    def _():
        slot = par_ref[j]

        @pl.when(j == 0)
        def _():
            for c in weight_copies(be_ref[0], slot):
                c.start()

        for c in weight_copies(be_ref[j], slot):
            c.wait()

        @pl.when(nxt_ref[j] >= 0)
        def _():
            for c in weight_copies(nxt_ref[j], 1 - slot):
                c.start()

        wg_s[...] = wg_f[slot].astype(jnp.bfloat16)
        wu_s[...] = wu_f[slot].astype(jnp.bfloat16)
        wd_s[...] = wd_f[slot].astype(jnp.bfloat16)

    @pl.when(used)
    def _():
        live = lax.broadcasted_iota(jnp.int32, (xs_ref.shape[0], 1), 0) < nv_ref[j]
        x = _unpack_bf16_pairs(jnp.where(live, xs_ref[...], jnp.uint32(0)))
        g = jnp.dot(x, wg_s[...], preferred_element_type=jnp.float32)
        u = jnp.dot(x, wu_s[...], preferred_element_type=jnp.float32)
        a = (g * jax.nn.sigmoid(g) * u).astype(jnp.bfloat16)
        y = jnp.dot(a, wd_s[...], preferred_element_type=jnp.float32)
        y_ref[...] = _pack_bf16_pairs(y.astype(jnp.bfloat16))


def _experts(block_e, n_used, n_valid, next_e, parity, phys, xs, w_gate, w_up, w_down):
    bm = EXPERT_BLOCK
    n_blocks = xs.shape[0] // bm
    hbm = pl.BlockSpec(memory_space=pl.ANY)
    return pl.pallas_call(
        _experts_kernel,
        grid_spec=pltpu.PrefetchScalarGridSpec(
            num_scalar_prefetch=6, grid=(n_blocks,),
            in_specs=[pl.BlockSpec((bm, D_MODEL // 2), lambda j, be, nu, nv, nx, pa, ph: (ph[j], 0)),
                      hbm, hbm, hbm],
            out_specs=pl.BlockSpec((bm, D_MODEL // 2), lambda j, be, nu, nv, nx, pa, ph: (ph[j], 0)),
            scratch_shapes=[pltpu.VMEM((D_MODEL, EXPERT_HIDDEN), jnp.bfloat16),
                            pltpu.VMEM((D_MODEL, EXPERT_HIDDEN), jnp.bfloat16),
                            pltpu.VMEM((EXPERT_HIDDEN, D_MODEL), jnp.bfloat16),
                            pltpu.VMEM((2, D_MODEL, EXPERT_HIDDEN), jnp.float32),
                            pltpu.VMEM((2, D_MODEL, EXPERT_HIDDEN), jnp.float32),
                            pltpu.VMEM((2, EXPERT_HIDDEN, D_MODEL), jnp.float32),
                            pltpu.SemaphoreType.DMA((2,))]),
        out_shape=jax.ShapeDtypeStruct((xs.shape[0], D_MODEL // 2), jnp.uint32),
        compiler_params=pltpu.CompilerParams(dimension_semantics=("arbitrary",), vmem_limit_bytes=VMEM_LIMIT),
        name="experts",
    )(block_e, n_used, n_valid, next_e, parity, phys, xs, w_gate, w_up, w_down)


def _gather_rows(idx, yb):
    mesh = plsc.VectorSubcoreMesh(core_axis_name="core", subcore_axis_name="subcore")
    rows, window = SC_ROWS, SC_INDEX_WINDOW
    sub = window // rows
    n = idx.shape[1]
    width = yb.shape[1]
    assert n % window == 0

    @pl.kernel(out_type=jax.ShapeDtypeStruct((n, width), yb.dtype), mesh=mesh, scratch_types=[])
    def gather(y_hbm, i_hbm, o_hbm):
        def body(indices, i_vmem, o_vmem):
            pltpu.sync_copy(y_hbm.at[i_vmem.at[0, pl.ds(indices[1] * rows, rows)]], o_vmem)

        pltpu.emit_pipeline(
            body, grid=(n // window, sub),
            in_specs=[pl.BlockSpec((1, window), lambda i, j: (0, i))],
            out_specs=[pl.BlockSpec((rows, width), lambda i, j: (i * sub + j, 0))],
            core_axis_name=("core", "subcore"),
            dimension_semantics=(pltpu.PARALLEL, pltpu.ARBITRARY), _explicit_indices=True,
        )(i_hbm, o_hbm)

    return gather(yb, idx)


def _combine_kernel(x1_ref, gate_ref, rf_ref, fin_ref, y0_ref, y1_ref, *out_refs, per_token_gate):
    o_ref = out_refs[-1]
    gates = jnp.transpose(rf_ref[...])
    moe = _unpack_pairs_f32(y0_ref[...]) * gates[:, 0:1] + _unpack_pairs_f32(y1_ref[...]) * gates[:, 1:2]
    gate_f = gate_ref[...] if per_token_gate else gate_ref[...][5:6]
    o_ref[...] = _rms(x1_ref[...] + gate_f * moe, fin_ref[...])


def _combine(x1, gate, rf, final_norm, ys, out_prev, first_tile, n_tiles, *, per_token_gate):
    tc = COMBINE_TILE
    gate_spec = (pl.BlockSpec((tc, D_MODEL), lambda i: (first_tile + i, 0)) if per_token_gate
                 else pl.BlockSpec((6, D_MODEL), lambda i: (0, 0)))
    args = [x1, gate, rf, final_norm, ys, ys]
    in_specs = [pl.BlockSpec((tc, D_MODEL), lambda i: (first_tile + i, 0)),
                gate_spec,
                pl.BlockSpec((8, tc), lambda i: (0, first_tile + i)),
                pl.BlockSpec((1, D_MODEL), lambda i: (0, 0)),
                pl.BlockSpec((tc, D_MODEL // 2), lambda i: (i, 0)),
                pl.BlockSpec((tc, D_MODEL // 2), lambda i: (n_tiles + i, 0))]
    aliases = {}
    if out_prev is not None:
        args.append(out_prev)
        in_specs.append(pl.BlockSpec(memory_space=pl.ANY))
        aliases = {len(args) - 1: 0}
    return pl.pallas_call(
        functools.partial(_combine_kernel, per_token_gate=per_token_gate),
        grid=(n_tiles,), in_specs=in_specs,
        out_specs=pl.BlockSpec((tc, D_MODEL), lambda i: (first_tile + i, 0)),
        out_shape=jax.ShapeDtypeStruct(x1.shape, jnp.float32),
        input_output_aliases=aliases,
        compiler_params=pltpu.CompilerParams(dimension_semantics=("arbitrary",), vmem_limit_bytes=VMEM_LIMIT),
        name="combine",
    )(*args)


def _rope_tables(pos):
    half = HEAD_DIM // 2
    inv_freq = np.power(np.float32(ROPE_THETA), -(np.arange(half, dtype=np.float32) / np.float32(half)))
    ang = (pos.astype(np.float32)[:, None] * inv_freq[None, :].astype(np.float32)).astype(np.float64)
    cos, sin = np.cos(ang).astype(np.float32), np.sin(ang).astype(np.float32)
    cos = np.concatenate([cos, cos, cos, cos], axis=1)
    sin = np.concatenate([-sin, sin, -sin, sin], axis=1)
    return cos, sin


def _prompt_bias():
    r = np.arange(WINDOW)[:, None]
    j = np.arange(2 * WINDOW)[None, :]
    band = (j > r) & (j <= r + WINDOW)
    later = np.where(band, 0.0, NEG).astype(np.float32)
    first = np.where(band & (j >= WINDOW), 0.0, NEG).astype(np.float32)
    return jnp.asarray(np.stack([first, later]))


def _sample_bias(dec):
    cb = SAMPLE_CHUNK
    qb = np.repeat(np.arange(cb), dec)[:, None]
    qi = np.tile(np.arange(dec), cb)[:, None]
    kb = np.concatenate([np.repeat(np.arange(cb), WINDOW), np.repeat(np.arange(cb), dec)])[None, :]
    kj = np.concatenate([np.tile(np.arange(WINDOW), cb), np.tile(np.arange(dec), cb)])[None, :]
    is_new = (np.arange(cb * WINDOW + cb * dec) >= cb * WINDOW)[None, :]
    ok = (kb == qb) & np.where(is_new, kj <= qi, kj > qi)
    ok = np.pad(ok, ((0, 0), (0, SAMPLE_KEYS - ok.shape[1])))
    return jnp.asarray(np.where(ok, 0.0, NEG).astype(np.float32))


def kernel(x_prompt, x_sample, cache_k, cache_v, state_conv, c_prompt, c_sample, attn_norm, ffn_norm, w_mod, b_mod, w_in, conv_w, attn_sinks, out_norm_attn, out_norm_conv, w_out, w_group, b_group, w_expert, b_expert, w_gate, w_up, w_down, final_norm):
    assert attn_norm.shape[0] == 1, "one layer"
    bsz, seq, _ = x_prompt.shape
    nseq, dec, _ = x_sample.shape
    m_prompt = bsz * seq
    n_s = nseq * dec
    m_total = m_prompt + n_s
    tm = MIXER_TILE
    assert seq % tm == 0 and n_s % SAMPLE_TILE == 0 and SAMPLE_TILE % (SAMPLE_CHUNK * dec) == 0
    assert cache_k.shape[2] == WINDOW and m_prompt % SAMPLE_TILE == 0 and m_total % COMBINE_TILE == 0
    bf = jnp.bfloat16

    w_r = jnp.zeros((ROUTER_ROWS, D_MODEL), jnp.float32)
    w_r = w_r.at[0:N_GROUPS].set(w_group[0].T).at[8:8 + N_EXPERTS].set(w_expert[0].T).astype(bf)
    b_r = jnp.zeros((ROUTER_ROWS,), jnp.float32).at[0:N_GROUPS].set(b_group[0]).at[8:8 + N_EXPERTS].set(b_expert[0])
    b_r = jnp.broadcast_to(b_r[:, None], (ROUTER_ROWS, LANES))
    cos_p, sin_p = _rope_tables(np.arange(seq))
    cos_s, sin_s = _rope_tables(PAST_LEN + np.arange(dec))
    tri = jnp.asarray(np.triu(np.ones((tm, tm), np.float32), 1)).astype(bf)
    consts = dict(
        anorm=attn_norm[0][None], fnorm=ffn_norm[0][None], w_in=w_in[0].astype(bf),
        cos_p=jnp.asarray(cos_p), sin_p=jnp.asarray(sin_p),
        cos_s=jnp.asarray(np.tile(cos_s, (nseq, 1))), sin_s=jnp.asarray(np.tile(sin_s, (nseq, 1))),
        bias_p=_prompt_bias(), bias_s=_sample_bias(dec), conv_w=conv_w[0],
        gn_attn=out_norm_attn[0][None], gn_conv=out_norm_conv[0][None], w_out=w_out[0].astype(bf),
        w_r=w_r, b_r=b_r, tri=tri)
    sinks = attn_sinks[0]

    mod = _adaln(jnp.concatenate([c_prompt, c_sample], axis=0), w_mod[0], b_mod[0])
    mod_p = mod[:bsz].reshape(bsz, 6, D_MODEL)
    mod_s = jnp.repeat(mod[bsz:], dec, axis=0)

    bm = EXPERT_BLOCK
    eids = jnp.arange(N_EXPERTS, dtype=jnp.int32)
    lookup = lambda table, e: jnp.sum(jnp.where(e[..., None] == eids, table, 0), axis=-1)
    by_token = lambda a: jnp.transpose(a, (1, 0, 2)).reshape(8, -1)
    ceil_bm = lambda c: (c + bm - 1) // bm * bm
    x1_p, h2_p, ri_p, rf_p, cnt_p, k_p, v_p, u_p = _prompt_mixer(x_prompt, mod_p, sinks, consts)
    cnt_p = cnt_p[:, 0].astype(jnp.int32)
    padded_p = ceil_bm(cnt_p)
    pstart_p = jnp.cumsum(padded_p) - padded_p
    nb_p = -(-(2 * m_prompt + N_EXPERTS * (bm - 1)) // bm)
    nb_o = -(-(2 * n_s + N_EXPERTS * (bm - 1)) // bm)
    n_blocks = nb_p + nb_o
    ri_p = by_token(ri_p)
    dest_p = (lookup(pstart_p, ri_p[0:2]) + ri_p[2:4]).reshape(2 * m_prompt)
    xs = _dispatch(dest_p[None], h2_p, n_blocks * bm)

    st = state_conv[0]
    zrow = jnp.zeros((nseq, 1, CONV_DIM), jnp.float32)
    e1 = jnp.concatenate([st[:, 1:2]] + [zrow] * (dec - 1), axis=1).reshape(n_s, CONV_DIM)
    e2 = jnp.concatenate([st[:, 0:1], st[:, 1:2]] + [zrow] * (dec - 2), axis=1).reshape(n_s, CONV_DIM)
    ck = cache_k[0].reshape(nseq * WINDOW, KV_DIM)
    cv = cache_v[0].reshape(nseq * WINDOW, KV_DIM)
    x1_s, h2_s, ri_s, rf_s, cnt_s, k_s, v_s, u_s = _sample_mixer(
        x_sample.reshape(n_s, D_MODEL), mod_s, sinks, consts, ck, cv, e1, e2,
        jnp.zeros((N_EXPERTS, LANES), jnp.float32), dec)

    cnt_s = cnt_s[:, 0].astype(jnp.int32)
    fit = jnp.minimum(cnt_s, padded_p - cnt_p)
    over = cnt_s - fit
    padded_o = ceil_bm(over)
    ostart = nb_p * bm + jnp.cumsum(padded_o) - padded_o
    ri_s = by_token(ri_s)
    e_s, r_s = ri_s[0:2], ri_s[2:4]
    dest_s = jnp.where(r_s < lookup(fit, e_s), lookup(pstart_p + cnt_p, e_s) + r_s, lookup(ostart - fit, e_s) + r_s)
    dest_s = dest_s.reshape(2 * n_s)
    xs = _scatter_rows(dest_s, h2_s, xs)

    np_e, no_e = padded_p // bm, padded_o // bm
    lend = jnp.cumsum(np_e + no_e)
    lstart = lend - np_e - no_e
    n_used = lend[-1].astype(jnp.int32)
    blk = jnp.arange(n_blocks, dtype=jnp.int32)
    present = np_e + no_e > 0
    last_e = jnp.max(jnp.where(present, eids, 0))
    block_e = jnp.sum((lend[None, :] <= blk[:, None]).astype(jnp.int32), axis=1)
    block_e = jnp.where(blk < n_used, jnp.minimum(block_e, N_EXPERTS - 1), last_e)
    of_block = lambda table: lookup(table, block_e)
    within = blk - of_block(lstart)
    in_prompt = within < of_block(np_e)
    phys = jnp.where(in_prompt, of_block(pstart_p) // bm + within, of_block(ostart) // bm + within - of_block(np_e))
    n_valid = jnp.where(in_prompt, of_block(cnt_p + fit) - within * bm, of_block(over) - (within - of_block(np_e)) * bm)
    n_valid = jnp.where(blk < n_used, jnp.clip(n_valid, 0, bm), 0).astype(jnp.int32)
    phys = jnp.where(blk < n_used, phys, jnp.sum(jnp.where(blk == n_used - 1, phys, 0))).astype(jnp.int32)
    later = jnp.logical_and(present[None, :], eids[None, :] > eids[:, None])
    next_of = jnp.min(jnp.where(later, eids[None, :], N_EXPERTS), axis=1)
    next_of = jnp.where(next_of == N_EXPERTS, -1, next_of)
    next_e = of_block(next_of).astype(jnp.int32)
    parity = of_block((jnp.cumsum(present.astype(jnp.int32)) - 1) % 2).astype(jnp.int32)
    yb = _experts(block_e, n_used.reshape(1), n_valid, next_e, parity, phys, xs, w_gate[0], w_up[0], w_down[0])

    fin = final_norm[None]
    tc = COMBINE_TILE
    rf_p, rf_s = by_token(rf_p), by_token(rf_s)
    y_p = None
    for c in range(bsz):
        lo, hi = c * seq, (c + 1) * seq
        idx = jnp.concatenate([dest_p[lo:hi], dest_p[m_prompt + lo:m_prompt + hi]])[None]
        y_p = _combine(x1_p, mod_p[c], rf_p, fin, _gather_rows(idx, yb), y_p, c * (seq // tc), seq // tc,
                       per_token_gate=False)
    gate_s = mod_s[:, 5 * D_MODEL:]
    y_s = _combine(x1_s, gate_s, rf_s, fin, _gather_rows(dest_s[None], yb), None, 0, n_s // tc, per_token_gate=True)

    keep = WINDOW
    new_k_p = k_p.reshape(1, bsz, keep, N_KV_HEADS, HEAD_DIM)
    new_v_p = v_p.reshape(1, bsz, keep, N_KV_HEADS, HEAD_DIM)
    new_u_p = u_p[:, 6:8][None]
    ksn = k_s.reshape(nseq, dec, N_KV_HEADS, HEAD_DIM)
    vsn = v_s.reshape(nseq, dec, N_KV_HEADS, HEAD_DIM)
    new_k_s = jnp.concatenate([cache_k[0][:, dec:], ksn], axis=1)[None]
    new_v_s = jnp.concatenate([cache_v[0][:, dec:], vsn], axis=1)[None]
    u_all = jnp.concatenate([st, u_s.reshape(nseq, dec, CONV_DIM)], axis=1)
    new_u_s = u_all[:, -2:][None]
    return (y_p.reshape(bsz, seq, D_MODEL), y_s.reshape(nseq, dec, D_MODEL),
            new_k_p, new_v_p, new_u_p, new_k_s, new_v_s, new_u_s)
```

```python
import functools

import jax
import jax.numpy as jnp
import numpy as np
from jax import lax
from jax.experimental import pallas as pl
from jax.experimental.pallas import tpu as pltpu
from jax.experimental.pallas import tpu_sc as plsc

D_MODEL = 1024
HEAD_DIM = 64
ATTN_DIM = 512
N_Q_HEADS = 8
N_KV_HEADS = 2
KV_DIM = 128
CONV_DIM = 512
WINDOW = 128
IN_PROJ_DIM = ATTN_DIM + 2 * KV_DIM + 3 * CONV_DIM
N_GROUPS = 4
EXPERTS_PER_GROUP = 8
N_EXPERTS = 32
EXPERT_HIDDEN = 512
RMS_EPS = 1e-5
ROPE_THETA = 10000.0
PAST_LEN = 16384

LANES = 128
ROUTER_ROWS = 48
NEG = -1e30

MIXER_TILE = 512
SAMPLE_TILE = 128
SAMPLE_CHUNK = 8
SAMPLE_KEYS = 1152
EXPERT_BLOCK = 512
COMBINE_TILE = 512
COMBINE_CHUNK = 4096
SC_ROWS = 64
SC_INDEX_WINDOW = 128
VMEM_LIMIT = 56 * 1024 * 1024


def _rms(x, gain):
    ms = jnp.mean(x * x, axis=-1, keepdims=True)
    return x * lax.rsqrt(ms + RMS_EPS) * gain


def _half_lane_mask(rows=1):
    return lax.broadcasted_iota(jnp.int32, (rows, LANES), 1) < HEAD_DIM


def _rope(x, cos, sin_signed):
    t, w = x.shape
    reps = w // LANES
    lane = lax.broadcasted_iota(jnp.int32, (t, w), 1)
    upper = (lane % HEAD_DIM) >= (HEAD_DIM // 2)
    partner = jnp.where(upper, pltpu.roll(x, HEAD_DIM // 2, 1), pltpu.roll(x, w - HEAD_DIM // 2, 1))
    if reps > 1:
        cos = jnp.concatenate([cos] * reps, axis=1)
        sin_signed = jnp.concatenate([sin_signed] * reps, axis=1)
    return x * cos + partner * sin_signed


def _swap_halves(x):
    return pltpu.roll(x, HEAD_DIM, 1)


def _group_norm64(x, gain):
    t, w = x.shape
    lo = _half_lane_mask(t)
    outs = []
    for c in range(w // LANES):
        xc = x[:, c * LANES:(c + 1) * LANES]
        sq = xc * xc
        s_lo = jnp.sum(jnp.where(lo, sq, 0.0), axis=-1, keepdims=True)
        s_hi = jnp.sum(jnp.where(lo, 0.0, sq), axis=-1, keepdims=True)
        r = jnp.where(lo, lax.rsqrt(s_lo * (1.0 / HEAD_DIM) + RMS_EPS), lax.rsqrt(s_hi * (1.0 / HEAD_DIM) + RMS_EPS))
        outs.append(xc * r)
    return jnp.concatenate(outs, axis=1) * gain


def _attend(q_rows, keys, keys_sw, vals, vals_sw, bias, sinks_ref, group):
    r = q_rows.shape[0]
    lo = _half_lane_mask(r)
    outs = []
    for parity in range(2):
        heads = [4 * group + parity, 4 * group + parity + 2]
        kk, vv = (keys, vals) if parity == group else (keys_sw, vals_sw)
        half = lo if parity == 0 else jnp.logical_not(lo)
        qs = []
        for h in heads:
            pair = q_rows[:, (h // 2) * LANES:(h // 2 + 1) * LANES]
            qs.append(jnp.where(half, pair, jnp.zeros_like(pair)))
        qcat = jnp.concatenate(qs, axis=0)
        s = lax.dot_general(qcat, kk, (((1,), (1,)), ((), ())), preferred_element_type=jnp.float32)
        ps, dens = [], []
        for i, h in enumerate(heads):
            sh = s[i * r:(i + 1) * r] + bias
            sink = sinks_ref[h]
            m = jnp.maximum(jnp.max(sh, axis=-1, keepdims=True), sink)
            p = jnp.exp(sh - m)
            dens.append(jnp.sum(p, axis=-1, keepdims=True) + jnp.exp(sink - m))
            ps.append(p.astype(jnp.bfloat16))
        o = jnp.dot(jnp.concatenate(ps, axis=0), vv, preferred_element_type=jnp.float32)
        for i, h in enumerate(heads):
            oh = o[i * r:(i + 1) * r] / dens[i]
            ms = jnp.sum(jnp.where(half, oh * oh, 0.0), axis=-1, keepdims=True) * (1.0 / HEAD_DIM)
            outs.append((h, oh * lax.rsqrt(ms + RMS_EPS)))
    outs.sort(key=lambda t: t[0])
    return [o for _, o in outs]


def _merge_heads(head_outs, lo):
    pairs = [jnp.where(lo, head_outs[2 * j], head_outs[2 * j + 1]) for j in range(4)]
    return jnp.concatenate(pairs, axis=1)


def _route(h2b, wr_ref, br_ref, tri_ref, cnt_ref, ri_ref, rf_ref):
    t = h2b.shape[0]
    lg = lax.dot_general(wr_ref[...], h2b, (((1,), (1,)), ((), ())), preferred_element_type=jnp.float32)
    lg = lg + br_ref[...][:, 0:1]
    r8 = lax.broadcasted_iota(jnp.int32, (8, t), 0)
    lgrp = jnp.where(r8 < N_GROUPS, lg[0:8], -jnp.inf)
    gmax = jnp.max(lgrp, axis=0, keepdims=True)
    grp = jnp.min(jnp.where(lgrp == gmax, r8, 8), axis=0, keepdims=True)
    pg_sel = 1.0 / jnp.sum(jnp.exp(lgrp - gmax), axis=0, keepdims=True)
    r32 = lax.broadcasted_iota(jnp.int32, (N_EXPERTS, t), 0)
    le = jnp.where((r32 // EXPERTS_PER_GROUP) == grp, lg[8:8 + N_EXPERTS], -jnp.inf)
    v1 = jnp.max(le, axis=0, keepdims=True)
    i1 = jnp.min(jnp.where(le == v1, r32, N_EXPERTS), axis=0, keepdims=True)
    le2 = jnp.where(r32 == i1, -jnp.inf, le)
    v2 = jnp.max(le2, axis=0, keepdims=True)
    i2 = jnp.min(jnp.where(le2 == v2, r32, N_EXPERTS), axis=0, keepdims=True)
    e21 = jnp.exp(v2 - v1)
    gate1 = pg_sel / (1.0 + e21)
    gate2 = pg_sel * e21 / (1.0 + e21)
    hot1 = r32 == i1
    hot2 = r32 == i2
    onehot = jnp.where(jnp.logical_or(hot1, hot2), 1.0, 0.0)
    before = jnp.dot(onehot.astype(jnp.bfloat16), tri_ref[...], preferred_element_type=jnp.float32)
    pos = before + cnt_ref[...][:, 0:1]
    rank1 = jnp.sum(jnp.where(hot1, pos, 0.0), axis=0, keepdims=True)
    rank2 = jnp.sum(jnp.where(hot2, pos, 0.0), axis=0, keepdims=True)
    cnt_ref[...] = cnt_ref[...] + jnp.sum(onehot, axis=1, keepdims=True)
    zi = jnp.zeros((4, t), jnp.int32)
    ri_ref[...] = jnp.concatenate([i1, i2, rank1.astype(jnp.int32), rank2.astype(jnp.int32), zi], axis=0)
    rf_ref[...] = jnp.concatenate([gate1, gate2, jnp.zeros((6, t), jnp.float32)], axis=0)


def _pack_bf16_pairs(xb):
    w = xb.shape[1] // 2
    bits = pltpu.bitcast(xb.astype(jnp.float32), jnp.uint32)
    return (bits[:, :w] >> 16) | (bits[:, w:] & jnp.uint32(0xFFFF0000))


def _unpack_pairs_f32(words):
    lo = pltpu.bitcast(words << 16, jnp.float32)
    hi = pltpu.bitcast(words & jnp.uint32(0xFFFF0000), jnp.float32)
    return jnp.concatenate([lo, hi], axis=1)


def _unpack_bf16_pairs(words):
    return _unpack_pairs_f32(words).astype(jnp.bfloat16)


def _split_proj(proj):
    a = ATTN_DIM
    q = proj[:, :a]
    k = proj[:, a:a + KV_DIM]
    v = proj[:, a + KV_DIM:a + 2 * KV_DIM]
    c0 = a + 2 * KV_DIM
    h_conv = proj[:, c0:c0 + CONV_DIM]
    gate_b = proj[:, c0 + CONV_DIM:c0 + 2 * CONV_DIM]
    gate_c = proj[:, c0 + 2 * CONV_DIM:c0 + 3 * CONV_DIM]
    return q, k, v, gate_c * h_conv, gate_b


def _mixer_tail(x, attn_n, conv_n, mod, wout_ref, fnorm_ref, wr_ref, br_ref, tri_ref, cnt_ref,
                x1_ref, h2_ref, ri_ref, rf_ref):
    gate_a, shift_f, scale_f = mod
    cat = jnp.concatenate([attn_n.astype(jnp.bfloat16), conv_n.astype(jnp.bfloat16)], axis=1)
    mix = jnp.dot(cat, wout_ref[...], preferred_element_type=jnp.float32)
    x1 = x + gate_a * mix
    x1_ref[...] = x1
    h2 = _rms(x1, fnorm_ref[...] * (1.0 + scale_f)) + shift_f
    h2b = h2.astype(jnp.bfloat16)
    h2_ref[...] = _pack_bf16_pairs(h2b)
    _route(h2b, wr_ref, br_ref, tri_ref, cnt_ref, ri_ref, rf_ref)


def _adaln_kernel(c_ref, w_ref, b_ref, o_ref):
    c = c_ref[...]
    a = (c * jax.nn.sigmoid(c)).astype(jnp.bfloat16)
    o_ref[...] = jnp.dot(a, w_ref[...].astype(jnp.bfloat16), preferred_element_type=jnp.float32) + b_ref[...]


def _adaln(c, w_mod, b_mod):
    n = c.shape[0]
    tn = 1536
    return pl.pallas_call(
        _adaln_kernel,
        grid=(w_mod.shape[1] // tn,),
        in_specs=[pl.BlockSpec((n, D_MODEL), lambda j: (0, 0)),
                  pl.BlockSpec((D_MODEL, tn), lambda j: (0, j)),
                  pl.BlockSpec((1, tn), lambda j: (0, j))],
        out_specs=pl.BlockSpec((n, tn), lambda j: (0, j)),
        out_shape=jax.ShapeDtypeStruct((n, w_mod.shape[1]), jnp.float32),
        compiler_params=pltpu.CompilerParams(dimension_semantics=("arbitrary",), vmem_limit_bytes=VMEM_LIMIT),
        name="adaln",
    )(c, w_mod, b_mod.reshape(1, -1))


def _prompt_mixer_kernel(sinks_ref, x_ref, mod_ref, anorm_ref, fnorm_ref, win_ref, cos_ref, sin_ref, bias_ref,
                         convw_ref, gna_ref, gnc_ref, wout_ref, wr_ref, br_ref, tri_ref,
                         x1_ref, h2_ref, ri_ref, rf_ref, cnt_out_ref, knew_ref, vnew_ref, unew_ref,
                         q_s, k_s, ksw_s, v_s, vsw_s, u_s, att_s, cnt_s):
    b = pl.program_id(0)
    t = pl.program_id(1)
    tm = MIXER_TILE
    nblk = tm // WINDOW

    @pl.when(jnp.logical_and(b == 0, t == 0))
    def _():
        cnt_s[...] = jnp.zeros_like(cnt_s)

    @pl.when(t == 0)
    def _():
        z = jnp.zeros((WINDOW, KV_DIM), jnp.bfloat16)
        k_s[0:WINDOW, :] = z
        ksw_s[0:WINDOW, :] = z
        v_s[0:WINDOW, :] = z
        vsw_s[0:WINDOW, :] = z
        u_s[0:8, :] = jnp.zeros((8, CONV_DIM), jnp.float32)

    x = x_ref[...]
    mod = mod_ref[...]
    h = _rms(x, anorm_ref[...] * (1.0 + mod[1:2])) + mod[0:1]
    proj = jnp.dot(h.astype(jnp.bfloat16), win_ref[...], preferred_element_type=jnp.float32)
    q, k, v, u, gate_b = _split_proj(proj)
    cos = cos_ref[...]
    sin = sin_ref[...]
    q = _rope(q, cos, sin) * (HEAD_DIM ** -0.5)
    k = _rope(k, cos, sin)
    q_s[...] = q.astype(jnp.bfloat16)
    k_s[WINDOW:, :] = k.astype(jnp.bfloat16)
    ksw_s[WINDOW:, :] = _swap_halves(k).astype(jnp.bfloat16)
    v_s[WINDOW:, :] = v.astype(jnp.bfloat16)
    vsw_s[WINDOW:, :] = _swap_halves(v).astype(jnp.bfloat16)
    knew_ref[...] = k[tm - WINDOW:, :]
    vnew_ref[...] = v[tm - WINDOW:, :]

    u_s[8:, :] = u
    unew_ref[...] = u[tm - 8:, :]
    cw = convw_ref[...]
    conv = (u_s[6:6 + tm, :] * cw[0:1] + u_s[7:7 + tm, :] * cw[1:2] + u * cw[2:3]) * gate_b
    u_s[0:8, :] = u[tm - 8:, :]
    conv_n = _group_norm64(conv, gnc_ref[...])

    lo = _half_lane_mask(WINDOW)
    first = t == 0

    def block(i, carry):
        r0 = pl.multiple_of(i * WINDOW, WINDOW)
        bias = bias_ref[jnp.where(jnp.logical_and(first, i == 0), 0, 1)]
        qb = q_s[pl.ds(r0, WINDOW), :]
        keys = k_s[pl.ds(r0, 2 * WINDOW), :]
        keys_sw = ksw_s[pl.ds(r0, 2 * WINDOW), :]
        vals = v_s[pl.ds(r0, 2 * WINDOW), :]
        vals_sw = vsw_s[pl.ds(r0, 2 * WINDOW), :]
        outs = []
        for g in range(N_KV_HEADS):
            outs += _attend(qb, keys, keys_sw, vals, vals_sw, bias, sinks_ref, g)
        att_s[pl.ds(r0, WINDOW), :] = _merge_heads(outs, lo)
        return carry

    lax.fori_loop(0, nblk, block, 0, unroll=True)
    k_s[0:WINDOW, :] = k_s[tm:tm + WINDOW, :]
    ksw_s[0:WINDOW, :] = ksw_s[tm:tm + WINDOW, :]
    v_s[0:WINDOW, :] = v_s[tm:tm + WINDOW, :]
    vsw_s[0:WINDOW, :] = vsw_s[tm:tm + WINDOW, :]

    attn_n = att_s[...] * gna_ref[...]
    _mixer_tail(x, attn_n, conv_n, (mod[2:3], mod[3:4], mod[4:5]), wout_ref, fnorm_ref, wr_ref, br_ref, tri_ref,
                cnt_s, x1_ref, h2_ref, ri_ref, rf_ref)
    cnt_out_ref[...] = cnt_s[...]


def _prompt_mixer(x, mod, sinks, consts):
    bsz, seq, _ = x.shape
    tm = MIXER_TILE
    nt = seq // tm
    full = lambda shape: pl.BlockSpec(shape, lambda b, t, s: (0,) * len(shape))
    in_specs = [
        pl.BlockSpec((None, tm, D_MODEL), lambda b, t, s: (b, t, 0)),
        pl.BlockSpec((None, 6, D_MODEL), lambda b, t, s: (b, 0, 0)),
        full((1, D_MODEL)), full((1, D_MODEL)),
        full((D_MODEL, IN_PROJ_DIM)),
        pl.BlockSpec((tm, LANES), lambda b, t, s: (t, 0)),
        pl.BlockSpec((tm, LANES), lambda b, t, s: (t, 0)),
        full((2, WINDOW, 2 * WINDOW)),
        full((3, CONV_DIM)), full((1, ATTN_DIM)), full((1, CONV_DIM)),
        full((D_MODEL, D_MODEL)),
        full((ROUTER_ROWS, D_MODEL)), full((ROUTER_ROWS, LANES)),
        full((tm, tm)),
    ]
    out_shape = [
        jax.ShapeDtypeStruct((bsz * seq, D_MODEL), jnp.float32),
        jax.ShapeDtypeStruct((bsz * seq, D_MODEL // 2), jnp.uint32),
        jax.ShapeDtypeStruct((bsz * nt, 8, tm), jnp.int32),
        jax.ShapeDtypeStruct((bsz * nt, 8, tm), jnp.float32),
        jax.ShapeDtypeStruct((N_EXPERTS, LANES), jnp.float32),
        jax.ShapeDtypeStruct((bsz, WINDOW, KV_DIM), jnp.float32),
        jax.ShapeDtypeStruct((bsz, WINDOW, KV_DIM), jnp.float32),
        jax.ShapeDtypeStruct((bsz, 8, CONV_DIM), jnp.float32),
    ]
    out_specs = [
        pl.BlockSpec((tm, D_MODEL), lambda b, t, s: (b * nt + t, 0)),
        pl.BlockSpec((tm, D_MODEL // 2), lambda b, t, s: (b * nt + t, 0)),
        pl.BlockSpec((None, 8, tm), lambda b, t, s: (b * nt + t, 0, 0)),
        pl.BlockSpec((None, 8, tm), lambda b, t, s: (b * nt + t, 0, 0)),
        pl.BlockSpec((N_EXPERTS, LANES), lambda b, t, s: (0, 0)),
        pl.BlockSpec((None, WINDOW, KV_DIM), lambda b, t, s: (b, 0, 0)),
        pl.BlockSpec((None, WINDOW, KV_DIM), lambda b, t, s: (b, 0, 0)),
        pl.BlockSpec((None, 8, CONV_DIM), lambda b, t, s: (b, 0, 0)),
    ]
    scratch = [
        pltpu.VMEM((tm, ATTN_DIM), jnp.bfloat16),
        pltpu.VMEM((tm + WINDOW, KV_DIM), jnp.bfloat16), pltpu.VMEM((tm + WINDOW, KV_DIM), jnp.bfloat16),
        pltpu.VMEM((tm + WINDOW, KV_DIM), jnp.bfloat16), pltpu.VMEM((tm + WINDOW, KV_DIM), jnp.bfloat16),
        pltpu.VMEM((tm + 8, CONV_DIM), jnp.float32),
        pltpu.VMEM((tm, ATTN_DIM), jnp.float32),
        pltpu.VMEM((N_EXPERTS, LANES), jnp.float32),
    ]
    return pl.pallas_call(
        _prompt_mixer_kernel,
        grid_spec=pltpu.PrefetchScalarGridSpec(
            num_scalar_prefetch=1, grid=(bsz, nt), in_specs=in_specs, out_specs=out_specs, scratch_shapes=scratch),
        out_shape=out_shape,
        compiler_params=pltpu.CompilerParams(dimension_semantics=("arbitrary", "arbitrary"),
                                             vmem_limit_bytes=VMEM_LIMIT),
        name="prompt_mixer",
    )(sinks, x, mod, consts["anorm"], consts["fnorm"], consts["w_in"], consts["cos_p"], consts["sin_p"],
      consts["bias_p"], consts["conv_w"], consts["gn_attn"], consts["gn_conv"], consts["w_out"],
      consts["w_r"], consts["b_r"], consts["tri"])


def _sample_mixer_kernel(sinks_ref, x_ref, mod_ref, anorm_ref, fnorm_ref, win_ref, cos_ref, sin_ref, bias_ref,
                         convw_ref, gna_ref, gnc_ref, wout_ref, wr_ref, br_ref, tri_ref,
                         ck_ref, cv_ref, e1_ref, e2_ref, cnt_in_ref,
                         x1_ref, h2_ref, ri_ref, rf_ref, cnt_out_ref, knew_ref, vnew_ref, unew_ref,
                         q_s, k_s, v_s, att_s, cnt_s, *, dec):
    n = x_ref.shape[0]
    cb = SAMPLE_CHUNK
    rows = cb * dec
    pad_keys = SAMPLE_KEYS - cb * WINDOW - rows

    @pl.when(pl.program_id(0) == 0)
    def _():
        cnt_s[...] = cnt_in_ref[...]

    x = x_ref[...]
    mod = mod_ref[...]
    md = lambda j: mod[:, j * D_MODEL:(j + 1) * D_MODEL]
    h = _rms(x, anorm_ref[...]) * (1.0 + md(1)) + md(0)
    proj = jnp.dot(h.astype(jnp.bfloat16), win_ref[...], preferred_element_type=jnp.float32)
    q, k, v, u, gate_b = _split_proj(proj)
    cos = cos_ref[...]
    sin = sin_ref[...]
    q = _rope(q, cos, sin) * (HEAD_DIM ** -0.5)
    k = _rope(k, cos, sin)
    q_s[...] = q.astype(jnp.bfloat16)
    k_s[...] = k
    v_s[...] = v
    knew_ref[...] = k
    vnew_ref[...] = v
    unew_ref[...] = u

    pos = lax.broadcasted_iota(jnp.int32, (n, CONV_DIM), 0) % dec
    um1 = jnp.where(pos >= 1, pltpu.roll(u, 1, 0), 0.0) + e1_ref[...]
    um2 = jnp.where(pos >= 2, pltpu.roll(u, 2, 0), 0.0) + e2_ref[...]
    cw = convw_ref[...]
    conv = (um2 * cw[0:1] + um1 * cw[1:2] + u * cw[2:3]) * gate_b
    conv_n = _group_norm64(conv, gnc_ref[...])

    lo = _half_lane_mask(rows)
    bias = bias_ref[...]

    def chunk(c, carry):
        r0 = pl.multiple_of(c * rows, rows)
        c0 = pl.multiple_of(c * (cb * WINDOW), cb * WINDOW)
        zpad = jnp.zeros((pad_keys, KV_DIM), jnp.float32)
        kc = jnp.concatenate([ck_ref[pl.ds(c0, cb * WINDOW), :], k_s[pl.ds(r0, rows), :], zpad], axis=0)
        vc = jnp.concatenate([cv_ref[pl.ds(c0, cb * WINDOW), :], v_s[pl.ds(r0, rows), :], zpad], axis=0)
        keys, keys_sw = kc.astype(jnp.bfloat16), _swap_halves(kc).astype(jnp.bfloat16)
        vals, vals_sw = vc.astype(jnp.bfloat16), _swap_halves(vc).astype(jnp.bfloat16)
        qb = q_s[pl.ds(r0, rows), :]
        outs = []
        for g in range(N_KV_HEADS):
            outs += _attend(qb, keys, keys_sw, vals, vals_sw, bias, sinks_ref, g)
        att_s[pl.ds(r0, rows), :] = _merge_heads(outs, lo)
        return carry

    lax.fori_loop(0, n // rows, chunk, 0)
    attn_n = att_s[...] * gna_ref[...]
    _mixer_tail(x, attn_n, conv_n, (md(2), md(3), md(4)), wout_ref, fnorm_ref, wr_ref, br_ref, tri_ref,
                cnt_s, x1_ref, h2_ref, ri_ref, rf_ref)
    cnt_out_ref[...] = cnt_s[...]


def _sample_mixer(x, mod, sinks, consts, ck, cv, e1, e2, cnt_in, dec):
    n_s = x.shape[0]
    ts = SAMPLE_TILE
    seqs = ts // dec
    full = lambda a: pl.BlockSpec(a.shape, lambda i, s: (0,) * a.ndim)
    rows = lambda width: pl.BlockSpec((ts, width), lambda i, s: (i, 0))
    in_specs = [
        rows(D_MODEL), rows(6 * D_MODEL),
        full(consts["anorm"]), full(consts["fnorm"]), full(consts["w_in"]),
        rows(LANES), rows(LANES), full(consts["bias_s"]),
        full(consts["conv_w"]), full(consts["gn_attn"]), full(consts["gn_conv"]), full(consts["w_out"]),
        full(consts["w_r"]), full(consts["b_r"]),
        pl.BlockSpec((ts, ts), lambda i, s: (0, 0)),
        pl.BlockSpec((seqs * WINDOW, KV_DIM), lambda i, s: (i, 0)),
        pl.BlockSpec((seqs * WINDOW, KV_DIM), lambda i, s: (i, 0)),
        rows(CONV_DIM), rows(CONV_DIM), full(cnt_in),
    ]
    args = [x, mod, consts["anorm"], consts["fnorm"], consts["w_in"], consts["cos_s"], consts["sin_s"],
            consts["bias_s"], consts["conv_w"], consts["gn_attn"], consts["gn_conv"], consts["w_out"],
            consts["w_r"], consts["b_r"], consts["tri"], ck, cv, e1, e2, cnt_in]
    nt = n_s // ts
    out_shape = [
        jax.ShapeDtypeStruct((n_s, D_MODEL), jnp.float32),
        jax.ShapeDtypeStruct((n_s, D_MODEL // 2), jnp.uint32),
        jax.ShapeDtypeStruct((nt, 8, ts), jnp.int32),
        jax.ShapeDtypeStruct((nt, 8, ts), jnp.float32),
        jax.ShapeDtypeStruct((N_EXPERTS, LANES), jnp.float32),
        jax.ShapeDtypeStruct((n_s, KV_DIM), jnp.float32),
        jax.ShapeDtypeStruct((n_s, KV_DIM), jnp.float32),
        jax.ShapeDtypeStruct((n_s, CONV_DIM), jnp.float32),
    ]
    out_specs = [
        rows(D_MODEL),
        rows(D_MODEL // 2),
        pl.BlockSpec((None, 8, ts), lambda i, s: (i, 0, 0)),
        pl.BlockSpec((None, 8, ts), lambda i, s: (i, 0, 0)),
        pl.BlockSpec((N_EXPERTS, LANES), lambda i, s: (0, 0)),
        rows(KV_DIM), rows(KV_DIM), rows(CONV_DIM),
    ]
    scratch = [
        pltpu.VMEM((ts, ATTN_DIM), jnp.bfloat16),
        pltpu.VMEM((ts, KV_DIM), jnp.float32), pltpu.VMEM((ts, KV_DIM), jnp.float32),
        pltpu.VMEM((ts, ATTN_DIM), jnp.float32),
        pltpu.VMEM((N_EXPERTS, LANES), jnp.float32),
    ]
    return pl.pallas_call(
        functools.partial(_sample_mixer_kernel, dec=dec),
        grid_spec=pltpu.PrefetchScalarGridSpec(
            num_scalar_prefetch=1, grid=(nt,), in_specs=in_specs, out_specs=out_specs, scratch_shapes=scratch),
        out_shape=out_shape,
        compiler_params=pltpu.CompilerParams(dimension_semantics=("arbitrary",), vmem_limit_bytes=VMEM_LIMIT),
        name="sample_mixer",
    )(sinks, *args)


def _dispatch(idx, h2, n_rows):
    mesh = plsc.VectorSubcoreMesh(core_axis_name="core", subcore_axis_name="subcore")
    rows, window = SC_ROWS, SC_INDEX_WINDOW
    sub = window // rows
    n_tok, width = h2.shape
    assert n_tok % rows == 0 and (2 * n_tok) % window == 0

    moved = 3 * n_tok * width * h2.dtype.itemsize
    @pl.kernel(out_type=jax.ShapeDtypeStruct((n_rows, width), h2.dtype), mesh=mesh, scratch_types=[],
               cost_estimate=pl.CostEstimate(flops=0, transcendentals=0, bytes_accessed=moved))
    def scatter(h_hbm, i_hbm, xs_hbm):
        def body(indices, h_vmem, i_vmem):
            pltpu.sync_copy(h_vmem, xs_hbm.at[i_vmem.at[0, pl.ds(indices[1] * rows, rows)]])

        pltpu.emit_pipeline(
            body, grid=(2 * n_tok // window, sub),
            in_specs=[pl.BlockSpec((rows, width), lambda i, j: ((i * sub + j) % (n_tok // rows), 0)),
                      pl.BlockSpec((1, window), lambda i, j: (0, i))],
            out_specs=[], core_axis_name=("core", "subcore"),
            dimension_semantics=(pltpu.PARALLEL, pltpu.ARBITRARY), _explicit_indices=True,
        )(h_hbm, i_hbm)

    return scatter(h2, idx)


def _scatter_rows_kernel(dest_ref, h_ref, xs_in_ref, xs_ref, sem):
    del xs_in_ref
    n_tok = h_ref.shape[0]

    def issue(i, carry):
        for k in range(2):
            d = dest_ref[k * n_tok + i]
            pltpu.make_async_copy(h_ref.at[pl.ds(i, 1)], xs_ref.at[pl.ds(d, 1)], sem.at[0]).start(priority=k)
        return carry

    lax.fori_loop(0, n_tok, issue, 0, unroll=8)
    for _ in range(2):
        pltpu.make_async_copy(h_ref, xs_ref.at[pl.ds(0, n_tok)], sem.at[0]).wait()


def _scatter_rows(dest, h2, xs):
    return pl.pallas_call(
        _scatter_rows_kernel,
        grid_spec=pltpu.PrefetchScalarGridSpec(
            num_scalar_prefetch=1, grid=(1,),
            in_specs=[pl.BlockSpec(h2.shape, lambda i, d: (0, 0)), pl.BlockSpec(memory_space=pl.ANY)],
            out_specs=pl.BlockSpec(memory_space=pl.ANY),
            scratch_shapes=[pltpu.SemaphoreType.DMA((1,))]),
        out_shape=jax.ShapeDtypeStruct(xs.shape, xs.dtype),
        input_output_aliases={2: 0},
        compiler_params=pltpu.CompilerParams(dimension_semantics=("arbitrary",)),
        name="scatter_rows",
    )(dest, h2, xs)


def _experts_kernel(be_ref, nu_ref, nv_ref, nxt_ref, par_ref, phys_ref, xs_ref, wg_ref, wu_ref, wd_ref, y_ref,
                    wg_s, wu_s, wd_s, wg_f, wu_f, wd_f, sem):
    del phys_ref
    j = pl.program_id(0)
    used = j < nu_ref[0]
    changed = jnp.logical_or(j == 0, be_ref[j] != be_ref[jnp.maximum(j - 1, 0)])

    def weight_copies(e, slot):
        return [pltpu.make_async_copy(src.at[e], dst.at[slot], sem.at[slot])
                for src, dst in ((wg_ref, wg_f), (wu_ref, wu_f), (wd_ref, wd_f))]

    @pl.when(jnp.logical_and(used, changed))
    def _():
        slot = par_ref[j]

        @pl.when(j == 0)
        def _():
            for c in weight_copies(be_ref[0], slot):
                c.start()

        for c in weight_copies(be_ref[j], slot):
            c.wait()

        @pl.when(nxt_ref[j] >= 0)
        def _():
            for c in weight_copies(nxt_ref[j], 1 - slot):
                c.start()

        wg_s[...] = wg_f[slot].astype(jnp.bfloat16)
        wu_s[...] = wu_f[slot].astype(jnp.bfloat16)
        wd_s[...] = wd_f[slot].astype(jnp.bfloat16)

    @pl.when(used)
    def _():
        live = lax.broadcasted_iota(jnp.int32, (xs_ref.shape[0], 1), 0) < nv_ref[j]
        x = _unpack_bf16_pairs(jnp.where(live, xs_ref[...], jnp.uint32(0)))
        g = jnp.dot(x, wg_s[...], preferred_element_type=jnp.float32)
        u = jnp.dot(x, wu_s[...], preferred_element_type=jnp.float32)
        a = (g * jax.nn.sigmoid(g) * u).astype(jnp.bfloat16)
        y = jnp.dot(a, wd_s[...], preferred_element_type=jnp.float32)
        y_ref[...] = _pack_bf16_pairs(y.astype(jnp.bfloat16))


def _experts(block_e, n_used, n_valid, next_e, parity, phys, xs, w_gate, w_up, w_down):
    bm = EXPERT_BLOCK
    n_blocks = xs.shape[0] // bm
    hbm = pl.BlockSpec(memory_space=pl.ANY)
    return pl.pallas_call(
        _experts_kernel,
        grid_spec=pltpu.PrefetchScalarGridSpec(
            num_scalar_prefetch=6, grid=(n_blocks,),
            in_specs=[pl.BlockSpec((bm, D_MODEL // 2), lambda j, be, nu, nv, nx, pa, ph: (ph[j], 0)),
                      hbm, hbm, hbm],
            out_specs=pl.BlockSpec((bm, D_MODEL // 2), lambda j, be, nu, nv, nx, pa, ph: (ph[j], 0)),
            scratch_shapes=[pltpu.VMEM((D_MODEL, EXPERT_HIDDEN), jnp.bfloat16),
                            pltpu.VMEM((D_MODEL, EXPERT_HIDDEN), jnp.bfloat16),
                            pltpu.VMEM((EXPERT_HIDDEN, D_MODEL), jnp.bfloat16),
                            pltpu.VMEM((2, D_MODEL, EXPERT_HIDDEN), jnp.float32),
                            pltpu.VMEM((2, D_MODEL, EXPERT_HIDDEN), jnp.float32),
                            pltpu.VMEM((2, EXPERT_HIDDEN, D_MODEL), jnp.float32),
                            pltpu.SemaphoreType.DMA((2,))]),
        out_shape=jax.ShapeDtypeStruct((xs.shape[0], D_MODEL // 2), jnp.uint32),
        compiler_params=pltpu.CompilerParams(dimension_semantics=("arbitrary",), vmem_limit_bytes=VMEM_LIMIT),
        name="experts",
    )(block_e, n_used, n_valid, next_e, parity, phys, xs, w_gate, w_up, w_down)


def _gather_rows(idx, yb):
    mesh = plsc.VectorSubcoreMesh(core_axis_name="core", subcore_axis_name="subcore")
    rows, window = SC_ROWS, SC_INDEX_WINDOW
    sub = window // rows
    n = idx.shape[1]
    width = yb.shape[1]
    assert n % window == 0

    @pl.kernel(out_type=jax.ShapeDtypeStruct((n, width), yb.dtype), mesh=mesh, scratch_types=[])
    def gather(y_hbm, i_hbm, o_hbm):
        def body(indices, i_vmem, o_vmem):
            pltpu.sync_copy(y_hbm.at[i_vmem.at[0, pl.ds(indices[1] * rows, rows)]], o_vmem)

        pltpu.emit_pipeline(
            body, grid=(n // window, sub),
            in_specs=[pl.BlockSpec((1, window), lambda i, j: (0, i))],
            out_specs=[pl.BlockSpec((rows, width), lambda i, j: (i * sub + j, 0))],
            core_axis_name=("core", "subcore"),
            dimension_semantics=(pltpu.PARALLEL, pltpu.ARBITRARY), _explicit_indices=True,
        )(i_hbm, o_hbm)

    return gather(yb, idx)


def _combine_kernel(x1_ref, gate_ref, rf_ref, fin_ref, y0_ref, y1_ref, *out_refs, per_token_gate):
    o_ref = out_refs[-1]
    gates = jnp.transpose(rf_ref[...])
    moe = _unpack_pairs_f32(y0_ref[...]) * gates[:, 0:1] + _unpack_pairs_f32(y1_ref[...]) * gates[:, 1:2]
    gate_f = gate_ref[...] if per_token_gate else gate_ref[...][5:6]
    o_ref[...] = _rms(x1_ref[...] + gate_f * moe, fin_ref[...])


def _combine(x1, gate, rf, final_norm, ys, out_prev, first_tile, n_tiles, *, per_token_gate):
    tc = COMBINE_TILE
    gate_spec = (pl.BlockSpec((tc, D_MODEL), lambda i: (first_tile + i, 0)) if per_token_gate
                 else pl.BlockSpec((6, D_MODEL), lambda i: (0, 0)))
    args = [x1, gate, rf, final_norm, ys, ys]
    in_specs = [pl.BlockSpec((tc, D_MODEL), lambda i: (first_tile + i, 0)),
                gate_spec,
                pl.BlockSpec((8, tc), lambda i: (0, first_tile + i)),
                pl.BlockSpec((1, D_MODEL), lambda i: (0, 0)),
                pl.BlockSpec((tc, D_MODEL // 2), lambda i: (i, 0)),
                pl.BlockSpec((tc, D_MODEL // 2), lambda i: (n_tiles + i, 0))]
    aliases = {}
    if out_prev is not None:
        args.append(out_prev)
        in_specs.append(pl.BlockSpec(memory_space=pl.ANY))
        aliases = {len(args) - 1: 0}
    return pl.pallas_call(
        functools.partial(_combine_kernel, per_token_gate=per_token_gate),
        grid=(n_tiles,), in_specs=in_specs,
        out_specs=pl.BlockSpec((tc, D_MODEL), lambda i: (first_tile + i, 0)),
        out_shape=jax.ShapeDtypeStruct(x1.shape, jnp.float32),
        input_output_aliases=aliases,
        compiler_params=pltpu.CompilerParams(dimension_semantics=("arbitrary",), vmem_limit_bytes=VMEM_LIMIT),
        name="combine",
    )(*args)


def _rope_tables(pos):
    half = HEAD_DIM // 2
    inv_freq = np.power(np.float32(ROPE_THETA), -(np.arange(half, dtype=np.float32) / np.float32(half)))
    ang = (pos.astype(np.float32)[:, None] * inv_freq[None, :].astype(np.float32)).astype(np.float64)
    cos, sin = np.cos(ang).astype(np.float32), np.sin(ang).astype(np.float32)
    cos = np.concatenate([cos, cos, cos, cos], axis=1)
    sin = np.concatenate([-sin, sin, -sin, sin], axis=1)
    return cos, sin


def _prompt_bias():
    r = np.arange(WINDOW)[:, None]
    j = np.arange(2 * WINDOW)[None, :]
    band = (j > r) & (j <= r + WINDOW)
    later = np.where(band, 0.0, NEG).astype(np.float32)
    first = np.where(band & (j >= WINDOW), 0.0, NEG).astype(np.float32)
    return jnp.asarray(np.stack([first, later]))


def _sample_bias(dec):
    cb = SAMPLE_CHUNK
    qb = np.repeat(np.arange(cb), dec)[:, None]
    qi = np.tile(np.arange(dec), cb)[:, None]
    kb = np.concatenate([np.repeat(np.arange(cb), WINDOW), np.repeat(np.arange(cb), dec)])[None, :]
    kj = np.concatenate([np.tile(np.arange(WINDOW), cb), np.tile(np.arange(dec), cb)])[None, :]
    is_new = (np.arange(cb * WINDOW + cb * dec) >= cb * WINDOW)[None, :]
    ok = (kb == qb) & np.where(is_new, kj <= qi, kj > qi)
    ok = np.pad(ok, ((0, 0), (0, SAMPLE_KEYS - ok.shape[1])))
    return jnp.asarray(np.where(ok, 0.0, NEG).astype(np.float32))


def kernel(x_prompt, x_sample, cache_k, cache_v, state_conv, c_prompt, c_sample, attn_norm, ffn_norm, w_mod, b_mod, w_in, conv_w, attn_sinks, out_norm_attn, out_norm_conv, w_out, w_group, b_group, w_expert, b_expert, w_gate, w_up, w_down, final_norm):
    assert attn_norm.shape[0] == 1, "one layer"
    bsz, seq, _ = x_prompt.shape
    nseq, dec, _ = x_sample.shape
    m_prompt = bsz * seq
    n_s = nseq * dec
    m_total = m_prompt + n_s
    tm = MIXER_TILE
    assert seq % tm == 0 and n_s % SAMPLE_TILE == 0 and SAMPLE_TILE % (SAMPLE_CHUNK * dec) == 0
    assert cache_k.shape[2] == WINDOW and m_prompt % SAMPLE_TILE == 0 and m_total % COMBINE_TILE == 0
    bf = jnp.bfloat16

    w_r = jnp.zeros((ROUTER_ROWS, D_MODEL), jnp.float32)
    w_r = w_r.at[0:N_GROUPS].set(w_group[0].T).at[8:8 + N_EXPERTS].set(w_expert[0].T).astype(bf)
    b_r = jnp.zeros((ROUTER_ROWS,), jnp.float32).at[0:N_GROUPS].set(b_group[0]).at[8:8 + N_EXPERTS].set(b_expert[0])
    b_r = jnp.broadcast_to(b_r[:, None], (ROUTER_ROWS, LANES))
    cos_p, sin_p = _rope_tables(np.arange(seq))
    cos_s, sin_s = _rope_tables(PAST_LEN + np.arange(dec))
    tri = jnp.asarray(np.triu(np.ones((tm, tm), np.float32), 1)).astype(bf)
    consts = dict(
        anorm=attn_norm[0][None], fnorm=ffn_norm[0][None], w_in=w_in[0].astype(bf),
        cos_p=jnp.asarray(cos_p), sin_p=jnp.asarray(sin_p),
        cos_s=jnp.asarray(np.tile(cos_s, (nseq, 1))), sin_s=jnp.asarray(np.tile(sin_s, (nseq, 1))),
        bias_p=_prompt_bias(), bias_s=_sample_bias(dec), conv_w=conv_w[0],
        gn_attn=out_norm_attn[0][None], gn_conv=out_norm_conv[0][None], w_out=w_out[0].astype(bf),
        w_r=w_r, b_r=b_r, tri=tri)
    sinks = attn_sinks[0]

    mod = _adaln(jnp.concatenate([c_prompt, c_sample], axis=0), w_mod[0], b_mod[0])
    mod_p = mod[:bsz].reshape(bsz, 6, D_MODEL)
    mod_s = jnp.repeat(mod[bsz:], dec, axis=0)

    bm = EXPERT_BLOCK
    eids = jnp.arange(N_EXPERTS, dtype=jnp.int32)
    lookup = lambda table, e: jnp.sum(jnp.where(e[..., None] == eids, table, 0), axis=-1)
    by_token = lambda a: jnp.transpose(a, (1, 0, 2)).reshape(8, -1)
    ceil_bm = lambda c: (c + bm - 1) // bm * bm
    x1_p, h2_p, ri_p, rf_p, cnt_p, k_p, v_p, u_p = _prompt_mixer(x_prompt, mod_p, sinks, consts)
    cnt_p = cnt_p[:, 0].astype(jnp.int32)
    padded_p = ceil_bm(cnt_p)
    pstart_p = jnp.cumsum(padded_p) - padded_p
    nb_p = -(-(2 * m_prompt + N_EXPERTS * (bm - 1)) // bm)
    nb_o = -(-(2 * n_s + N_EXPERTS * (bm - 1)) // bm)
    n_blocks = nb_p + nb_o
    ri_p = by_token(ri_p)
    dest_p = (lookup(pstart_p, ri_p[0:2]) + ri_p[2:4]).reshape(2 * m_prompt)
    xs = _dispatch(dest_p[None], h2_p, n_blocks * bm)

    st = state_conv[0]
    zrow = jnp.zeros((nseq, 1, CONV_DIM), jnp.float32)
    e1 = jnp.concatenate([st[:, 1:2]] + [zrow] * (dec - 1), axis=1).reshape(n_s, CONV_DIM)
    e2 = jnp.concatenate([st[:, 0:1], st[:, 1:2]] + [zrow] * (dec - 2), axis=1).reshape(n_s, CONV_DIM)
    ck = cache_k[0].reshape(nseq * WINDOW, KV_DIM)
    cv = cache_v[0].reshape(nseq * WINDOW, KV_DIM)
    x1_s, h2_s, ri_s, rf_s, cnt_s, k_s, v_s, u_s = _sample_mixer(
        x_sample.reshape(n_s, D_MODEL), mod_s, sinks, consts, ck, cv, e1, e2,
        jnp.zeros((N_EXPERTS, LANES), jnp.float32), dec)

    cnt_s = cnt_s[:, 0].astype(jnp.int32)
    fit = jnp.minimum(cnt_s, padded_p - cnt_p)
    over = cnt_s - fit
    padded_o = ceil_bm(over)
    ostart = nb_p * bm + jnp.cumsum(padded_o) - padded_o
    ri_s = by_token(ri_s)
    e_s, r_s = ri_s[0:2], ri_s[2:4]
    dest_s = jnp.where(r_s < lookup(fit, e_s), lookup(pstart_p + cnt_p, e_s) + r_s, lookup(ostart - fit, e_s) + r_s)
    dest_s = dest_s.reshape(2 * n_s)

    ksn = k_s.reshape(nseq, dec, N_KV_HEADS, HEAD_DIM)
    vsn = v_s.reshape(nseq, dec, N_KV_HEADS, HEAD_DIM)
    new_k_s = jnp.concatenate([cache_k[0][:, dec:], ksn], axis=1)[None]
    new_v_s = jnp.concatenate([cache_v[0][:, dec:], vsn], axis=1)[None]
    new_k_s, new_v_s, dest_s = lax.optimization_barrier((new_k_s, new_v_s, dest_s))
    xs = _scatter_rows(dest_s, h2_s, xs)

    np_e, no_e = padded_p // bm, padded_o // bm
    lend = jnp.cumsum(np_e + no_e)
    lstart = lend - np_e - no_e
    n_used = lend[-1].astype(jnp.int32)
    blk = jnp.arange(n_blocks, dtype=jnp.int32)
    present = np_e + no_e > 0
    last_e = jnp.max(jnp.where(present, eids, 0))
    block_e = jnp.sum((lend[None, :] <= blk[:, None]).astype(jnp.int32), axis=1)
    block_e = jnp.where(blk < n_used, jnp.minimum(block_e, N_EXPERTS - 1), last_e)
    of_block = lambda table: lookup(table, block_e)
    within = blk - of_block(lstart)
    in_prompt = within < of_block(np_e)
    phys = jnp.where(in_prompt, of_block(pstart_p) // bm + within, of_block(ostart) // bm + within - of_block(np_e))
    n_valid = jnp.where(in_prompt, of_block(cnt_p + fit) - within * bm, of_block(over) - (within - of_block(np_e)) * bm)
    n_valid = jnp.where(blk < n_used, jnp.clip(n_valid, 0, bm), 0).astype(jnp.int32)
    phys = jnp.where(blk < n_used, phys, jnp.sum(jnp.where(blk == n_used - 1, phys, 0))).astype(jnp.int32)
    later = jnp.logical_and(present[None, :], eids[None, :] > eids[:, None])
    next_of = jnp.min(jnp.where(later, eids[None, :], N_EXPERTS), axis=1)
    next_of = jnp.where(next_of == N_EXPERTS, -1, next_of)
    next_e = of_block(next_of).astype(jnp.int32)
    parity = of_block((jnp.cumsum(present.astype(jnp.int32)) - 1) % 2).astype(jnp.int32)
    yb = _experts(block_e, n_used.reshape(1), n_valid, next_e, parity, phys, xs, w_gate[0], w_up[0], w_down[0])

    fin = final_norm[None]
    tc = COMBINE_TILE
    rf_p, rf_s = by_token(rf_p), by_token(rf_s)
    y_p = None
    chunk = min(COMBINE_CHUNK, seq)
    assert seq % chunk == 0 and chunk % tc == 0
    for c in range(m_prompt // chunk):
        lo, hi = c * chunk, (c + 1) * chunk
        idx = jnp.concatenate([dest_p[lo:hi], dest_p[m_prompt + lo:m_prompt + hi]])[None]
        y_p = _combine(x1_p, mod_p[lo // seq], rf_p, fin, _gather_rows(idx, yb), y_p, lo // tc, chunk // tc,
                       per_token_gate=False)
    gate_s = mod_s[:, 5 * D_MODEL:]
    y_s = _combine(x1_s, gate_s, rf_s, fin, _gather_rows(dest_s[None], yb), None, 0, n_s // tc, per_token_gate=True)

    keep = WINDOW
    new_k_p = k_p.reshape(1, bsz, keep, N_KV_HEADS, HEAD_DIM)
    new_v_p = v_p.reshape(1, bsz, keep, N_KV_HEADS, HEAD_DIM)
    new_u_p = u_p[:, 6:8][None]
    u_all = jnp.concatenate([st, u_s.reshape(nseq, dec, CONV_DIM)], axis=1)
    new_u_s = u_all[:, -2:][None]
    return (y_p.reshape(bsz, seq, D_MODEL), y_s.reshape(nseq, dec, D_MODEL),
            new_k_p, new_v_p, new_u_p, new_k_s, new_v_s, new_u_s)
```

```python
import functools

import jax
import jax.numpy as jnp
import numpy as np
from jax import lax
from jax.experimental import pallas as pl
from jax.experimental.pallas import tpu as pltpu
from jax.experimental.pallas import tpu_sc as plsc

D_MODEL = 1024
HEAD_DIM = 64
ATTN_DIM = 512
N_Q_HEADS = 8
N_KV_HEADS = 2
KV_DIM = 128
CONV_DIM = 512
WINDOW = 128
IN_PROJ_DIM = ATTN_DIM + 2 * KV_DIM + 3 * CONV_DIM
N_GROUPS = 4
EXPERTS_PER_GROUP = 8
N_EXPERTS = 32
EXPERT_HIDDEN = 512
RMS_EPS = 1e-5
ROPE_THETA = 10000.0
PAST_LEN = 16384

LANES = 128
ROUTER_ROWS = 48
NEG = -1e30

MIXER_TILE = 512
SAMPLE_TILE = 128
SAMPLE_CHUNK = 8
SAMPLE_KEYS = 1152
EXPERT_BLOCK = 512
COMBINE_TILE = 512
COMBINE_CHUNK = 4096
SC_ROWS = 64
SC_INDEX_WINDOW = 128
VMEM_LIMIT = 56 * 1024 * 1024


def _rms(x, gain):
    ms = jnp.mean(x * x, axis=-1, keepdims=True)
    return x * lax.rsqrt(ms + RMS_EPS) * gain


def _half_lane_mask(rows=1):
    return lax.broadcasted_iota(jnp.int32, (rows, LANES), 1) < HEAD_DIM


def _rope(x, cos, sin_signed):
    t, w = x.shape
    reps = w // LANES
    lane = lax.broadcasted_iota(jnp.int32, (t, w), 1)
    upper = (lane % HEAD_DIM) >= (HEAD_DIM // 2)
    partner = jnp.where(upper, pltpu.roll(x, HEAD_DIM // 2, 1), pltpu.roll(x, w - HEAD_DIM // 2, 1))
    if reps > 1:
        cos = jnp.concatenate([cos] * reps, axis=1)
        sin_signed = jnp.concatenate([sin_signed] * reps, axis=1)
    return x * cos + partner * sin_signed


def _swap_halves(x):
    return pltpu.roll(x, HEAD_DIM, 1)


def _group_norm64(x, gain):
    t, w = x.shape
    lo = _half_lane_mask(t)
    outs = []
    for c in range(w // LANES):
        xc = x[:, c * LANES:(c + 1) * LANES]
        sq = xc * xc
        s_lo = jnp.sum(jnp.where(lo, sq, 0.0), axis=-1, keepdims=True)
        s_hi = jnp.sum(jnp.where(lo, 0.0, sq), axis=-1, keepdims=True)
        r = jnp.where(lo, lax.rsqrt(s_lo * (1.0 / HEAD_DIM) + RMS_EPS), lax.rsqrt(s_hi * (1.0 / HEAD_DIM) + RMS_EPS))
        outs.append(xc * r)
    return jnp.concatenate(outs, axis=1) * gain


def _attend(q_rows, keys, keys_sw, vals, vals_sw, bias, sinks_ref, group):
    r = q_rows.shape[0]
    lo = _half_lane_mask(r)
    outs = []
    for parity in range(2):
        heads = [4 * group + parity, 4 * group + parity + 2]
        kk, vv = (keys, vals) if parity == group else (keys_sw, vals_sw)
        half = lo if parity == 0 else jnp.logical_not(lo)
        qs = []
        for h in heads:
            pair = q_rows[:, (h // 2) * LANES:(h // 2 + 1) * LANES]
            qs.append(jnp.where(half, pair, jnp.zeros_like(pair)))
        qcat = jnp.concatenate(qs, axis=0)
        s = lax.dot_general(qcat, kk, (((1,), (1,)), ((), ())), preferred_element_type=jnp.float32)
        ps, dens = [], []
        for i, h in enumerate(heads):
            sh = s[i * r:(i + 1) * r] + bias
            sink = sinks_ref[h]
            m = jnp.maximum(jnp.max(sh, axis=-1, keepdims=True), sink)
            p = jnp.exp(sh - m)
            dens.append(jnp.sum(p, axis=-1, keepdims=True) + jnp.exp(sink - m))
            ps.append(p.astype(jnp.bfloat16))
        o = jnp.dot(jnp.concatenate(ps, axis=0), vv, preferred_element_type=jnp.float32)
        for i, h in enumerate(heads):
            oh = o[i * r:(i + 1) * r] / dens[i]
            ms = jnp.sum(jnp.where(half, oh * oh, 0.0), axis=-1, keepdims=True) * (1.0 / HEAD_DIM)
            outs.append((h, oh * lax.rsqrt(ms + RMS_EPS)))
    outs.sort(key=lambda t: t[0])
    return [o for _, o in outs]


def _merge_heads(head_outs, lo):
    pairs = [jnp.where(lo, head_outs[2 * j], head_outs[2 * j + 1]) for j in range(4)]
    return jnp.concatenate(pairs, axis=1)


def _route(h2b, wr_ref, br_ref, tri_ref, cnt_ref, ri_ref, rf_ref):
    t = h2b.shape[0]
    lg = lax.dot_general(wr_ref[...], h2b, (((1,), (1,)), ((), ())), preferred_element_type=jnp.float32)
    lg = lg + br_ref[...][:, 0:1]
    r8 = lax.broadcasted_iota(jnp.int32, (8, t), 0)
    lgrp = jnp.where(r8 < N_GROUPS, lg[0:8], -jnp.inf)
    gmax = jnp.max(lgrp, axis=0, keepdims=True)
    grp = jnp.min(jnp.where(lgrp == gmax, r8, 8), axis=0, keepdims=True)
    pg_sel = 1.0 / jnp.sum(jnp.exp(lgrp - gmax), axis=0, keepdims=True)
    r32 = lax.broadcasted_iota(jnp.int32, (N_EXPERTS, t), 0)
    le = jnp.where((r32 // EXPERTS_PER_GROUP) == grp, lg[8:8 + N_EXPERTS], -jnp.inf)
    v1 = jnp.max(le, axis=0, keepdims=True)
    i1 = jnp.min(jnp.where(le == v1, r32, N_EXPERTS), axis=0, keepdims=True)
    le2 = jnp.where(r32 == i1, -jnp.inf, le)
    v2 = jnp.max(le2, axis=0, keepdims=True)
    i2 = jnp.min(jnp.where(le2 == v2, r32, N_EXPERTS), axis=0, keepdims=True)
    e21 = jnp.exp(v2 - v1)
    gate1 = pg_sel / (1.0 + e21)
    gate2 = pg_sel * e21 / (1.0 + e21)
    hot1 = r32 == i1
    hot2 = r32 == i2
    onehot = jnp.where(jnp.logical_or(hot1, hot2), 1.0, 0.0)
    before = jnp.dot(onehot.astype(jnp.bfloat16), tri_ref[...], preferred_element_type=jnp.float32)
    pos = before + cnt_ref[...][:, 0:1]
    rank1 = jnp.sum(jnp.where(hot1, pos, 0.0), axis=0, keepdims=True)
    rank2 = jnp.sum(jnp.where(hot2, pos, 0.0), axis=0, keepdims=True)
    cnt_ref[...] = cnt_ref[...] + jnp.sum(onehot, axis=1, keepdims=True)
    zi = jnp.zeros((4, t), jnp.int32)
    ri_ref[...] = jnp.concatenate([i1, i2, rank1.astype(jnp.int32), rank2.astype(jnp.int32), zi], axis=0)
    rf_ref[...] = jnp.concatenate([gate1, gate2, jnp.zeros((6, t), jnp.float32)], axis=0)


def _pack_bf16_pairs(xb):
    w = xb.shape[1] // 2
    bits = pltpu.bitcast(xb.astype(jnp.float32), jnp.uint32)
    return (bits[:, :w] >> 16) | (bits[:, w:] & jnp.uint32(0xFFFF0000))


def _unpack_pairs_f32(words):
    lo = pltpu.bitcast(words << 16, jnp.float32)
    hi = pltpu.bitcast(words & jnp.uint32(0xFFFF0000), jnp.float32)
    return jnp.concatenate([lo, hi], axis=1)


def _unpack_bf16_pairs(words):
    return _unpack_pairs_f32(words).astype(jnp.bfloat16)


def _split_proj(proj):
    a = ATTN_DIM
    q = proj[:, :a]
    k = proj[:, a:a + KV_DIM]
    v = proj[:, a + KV_DIM:a + 2 * KV_DIM]
    c0 = a + 2 * KV_DIM
    h_conv = proj[:, c0:c0 + CONV_DIM]
    gate_b = proj[:, c0 + CONV_DIM:c0 + 2 * CONV_DIM]
    gate_c = proj[:, c0 + 2 * CONV_DIM:c0 + 3 * CONV_DIM]
    return q, k, v, gate_c * h_conv, gate_b


def _mixer_tail(x, attn_n, conv_n, mod, wout_ref, fnorm_ref, wr_ref, br_ref, tri_ref, cnt_ref,
                x1_ref, h2_ref, ri_ref, rf_ref):
    gate_a, shift_f, scale_f = mod
    cat = jnp.concatenate([attn_n.astype(jnp.bfloat16), conv_n.astype(jnp.bfloat16)], axis=1)
    mix = jnp.dot(cat, wout_ref[...], preferred_element_type=jnp.float32)
    x1 = x + gate_a * mix
    x1_ref[...] = x1
    h2 = _rms(x1, fnorm_ref[...] * (1.0 + scale_f)) + shift_f
    h2b = h2.astype(jnp.bfloat16)
    h2_ref[...] = _pack_bf16_pairs(h2b)
    _route(h2b, wr_ref, br_ref, tri_ref, cnt_ref, ri_ref, rf_ref)


def _adaln_kernel(c_ref, w_ref, b_ref, o_ref):
    c = c_ref[...]
    a = (c * jax.nn.sigmoid(c)).astype(jnp.bfloat16)
    o_ref[...] = jnp.dot(a, w_ref[...].astype(jnp.bfloat16), preferred_element_type=jnp.float32) + b_ref[...]


def _adaln(c, w_mod, b_mod):
    n = c.shape[0]
    tn = 1536
    return pl.pallas_call(
        _adaln_kernel,
        grid=(w_mod.shape[1] // tn,),
        in_specs=[pl.BlockSpec((n, D_MODEL), lambda j: (0, 0)),
                  pl.BlockSpec((D_MODEL, tn), lambda j: (0, j)),
                  pl.BlockSpec((1, tn), lambda j: (0, j))],
        out_specs=pl.BlockSpec((n, tn), lambda j: (0, j)),
        out_shape=jax.ShapeDtypeStruct((n, w_mod.shape[1]), jnp.float32),
        compiler_params=pltpu.CompilerParams(dimension_semantics=("arbitrary",), vmem_limit_bytes=VMEM_LIMIT),
        name="adaln",
    )(c, w_mod, b_mod.reshape(1, -1))


def _prompt_mixer_kernel(sinks_ref, x_ref, mod_ref, anorm_ref, fnorm_ref, win_ref, cos_ref, sin_ref, bias_ref,
                         convw_ref, gna_ref, gnc_ref, wout_ref, wr_ref, br_ref, tri_ref,
                         x1_ref, h2_ref, ri_ref, rf_ref, cnt_out_ref, knew_ref, vnew_ref, unew_ref,
                         q_s, k_s, ksw_s, v_s, vsw_s, u_s, att_s, cnt_s):
    b = pl.program_id(0)
    t = pl.program_id(1)
    tm = MIXER_TILE
    nblk = tm // WINDOW

    @pl.when(jnp.logical_and(b == 0, t == 0))
    def _():
        cnt_s[...] = jnp.zeros_like(cnt_s)

    @pl.when(t == 0)
    def _():
        z = jnp.zeros((WINDOW, KV_DIM), jnp.bfloat16)
        k_s[0:WINDOW, :] = z
        ksw_s[0:WINDOW, :] = z
        v_s[0:WINDOW, :] = z
        vsw_s[0:WINDOW, :] = z
        u_s[0:8, :] = jnp.zeros((8, CONV_DIM), jnp.float32)

    x = x_ref[...]
    mod = mod_ref[...]
    h = _rms(x, anorm_ref[...] * (1.0 + mod[1:2])) + mod[0:1]
    proj = jnp.dot(h.astype(jnp.bfloat16), win_ref[...], preferred_element_type=jnp.float32)
    q, k, v, u, gate_b = _split_proj(proj)
    cos = cos_ref[...]
    sin = sin_ref[...]
    q = _rope(q, cos, sin) * (HEAD_DIM ** -0.5)
    k = _rope(k, cos, sin)
    q_s[...] = q.astype(jnp.bfloat16)
    k_s[WINDOW:, :] = k.astype(jnp.bfloat16)
    ksw_s[WINDOW:, :] = _swap_halves(k).astype(jnp.bfloat16)
    v_s[WINDOW:, :] = v.astype(jnp.bfloat16)
    vsw_s[WINDOW:, :] = _swap_halves(v).astype(jnp.bfloat16)
    knew_ref[...] = k[tm - WINDOW:, :]
    vnew_ref[...] = v[tm - WINDOW:, :]

    u_s[8:, :] = u
    unew_ref[...] = u[tm - 8:, :]
    cw = convw_ref[...]
    conv = (u_s[6:6 + tm, :] * cw[0:1] + u_s[7:7 + tm, :] * cw[1:2] + u * cw[2:3]) * gate_b
    u_s[0:8, :] = u[tm - 8:, :]
    conv_n = _group_norm64(conv, gnc_ref[...])

    lo = _half_lane_mask(WINDOW)
    first = t == 0

    def block(i, carry):
        r0 = pl.multiple_of(i * WINDOW, WINDOW)
        bias = bias_ref[jnp.where(jnp.logical_and(first, i == 0), 0, 1)]
        qb = q_s[pl.ds(r0, WINDOW), :]
        keys = k_s[pl.ds(r0, 2 * WINDOW), :]
        keys_sw = ksw_s[pl.ds(r0, 2 * WINDOW), :]
        vals = v_s[pl.ds(r0, 2 * WINDOW), :]
        vals_sw = vsw_s[pl.ds(r0, 2 * WINDOW), :]
        outs = []
        for g in range(N_KV_HEADS):
            outs += _attend(qb, keys, keys_sw, vals, vals_sw, bias, sinks_ref, g)
        att_s[pl.ds(r0, WINDOW), :] = _merge_heads(outs, lo)
        return carry

    lax.fori_loop(0, nblk, block, 0, unroll=True)
    k_s[0:WINDOW, :] = k_s[tm:tm + WINDOW, :]
    ksw_s[0:WINDOW, :] = ksw_s[tm:tm + WINDOW, :]
    v_s[0:WINDOW, :] = v_s[tm:tm + WINDOW, :]
    vsw_s[0:WINDOW, :] = vsw_s[tm:tm + WINDOW, :]

    attn_n = att_s[...] * gna_ref[...]
    _mixer_tail(x, attn_n, conv_n, (mod[2:3], mod[3:4], mod[4:5]), wout_ref, fnorm_ref, wr_ref, br_ref, tri_ref,
                cnt_s, x1_ref, h2_ref, ri_ref, rf_ref)
    cnt_out_ref[...] = cnt_s[...]


def _prompt_mixer(x, mod, sinks, consts):
    bsz, seq, _ = x.shape
    tm = MIXER_TILE
    nt = seq // tm
    full = lambda shape: pl.BlockSpec(shape, lambda b, t, s: (0,) * len(shape))
    in_specs = [
        pl.BlockSpec((None, tm, D_MODEL), lambda b, t, s: (b, t, 0)),
        pl.BlockSpec((None, 6, D_MODEL), lambda b, t, s: (b, 0, 0)),
        full((1, D_MODEL)), full((1, D_MODEL)),
        full((D_MODEL, IN_PROJ_DIM)),
        pl.BlockSpec((tm, LANES), lambda b, t, s: (t, 0)),
        pl.BlockSpec((tm, LANES), lambda b, t, s: (t, 0)),
        full((2, WINDOW, 2 * WINDOW)),
        full((3, CONV_DIM)), full((1, ATTN_DIM)), full((1, CONV_DIM)),
        full((D_MODEL, D_MODEL)),
        full((ROUTER_ROWS, D_MODEL)), full((ROUTER_ROWS, LANES)),
        full((tm, tm)),
    ]
    out_shape = [
        jax.ShapeDtypeStruct((bsz * seq, D_MODEL), jnp.float32),
        jax.ShapeDtypeStruct((bsz * seq, D_MODEL // 2), jnp.uint32),
        jax.ShapeDtypeStruct((bsz * nt, 8, tm), jnp.int32),
        jax.ShapeDtypeStruct((bsz * nt, 8, tm), jnp.float32),
        jax.ShapeDtypeStruct((N_EXPERTS, LANES), jnp.float32),
        jax.ShapeDtypeStruct((bsz, WINDOW, KV_DIM), jnp.float32),
        jax.ShapeDtypeStruct((bsz, WINDOW, KV_DIM), jnp.float32),
        jax.ShapeDtypeStruct((bsz, 8, CONV_DIM), jnp.float32),
    ]
    out_specs = [
        pl.BlockSpec((tm, D_MODEL), lambda b, t, s: (b * nt + t, 0)),
        pl.BlockSpec((tm, D_MODEL // 2), lambda b, t, s: (b * nt + t, 0)),
        pl.BlockSpec((None, 8, tm), lambda b, t, s: (b * nt + t, 0, 0)),
        pl.BlockSpec((None, 8, tm), lambda b, t, s: (b * nt + t, 0, 0)),
        pl.BlockSpec((N_EXPERTS, LANES), lambda b, t, s: (0, 0)),
        pl.BlockSpec((None, WINDOW, KV_DIM), lambda b, t, s: (b, 0, 0)),
        pl.BlockSpec((None, WINDOW, KV_DIM), lambda b, t, s: (b, 0, 0)),
        pl.BlockSpec((None, 8, CONV_DIM), lambda b, t, s: (b, 0, 0)),
    ]
    scratch = [
        pltpu.VMEM((tm, ATTN_DIM), jnp.bfloat16),
        pltpu.VMEM((tm + WINDOW, KV_DIM), jnp.bfloat16), pltpu.VMEM((tm + WINDOW, KV_DIM), jnp.bfloat16),
        pltpu.VMEM((tm + WINDOW, KV_DIM), jnp.bfloat16), pltpu.VMEM((tm + WINDOW, KV_DIM), jnp.bfloat16),
        pltpu.VMEM((tm + 8, CONV_DIM), jnp.float32),
        pltpu.VMEM((tm, ATTN_DIM), jnp.float32),
        pltpu.VMEM((N_EXPERTS, LANES), jnp.float32),
    ]
    return pl.pallas_call(
        _prompt_mixer_kernel,
        grid_spec=pltpu.PrefetchScalarGridSpec(
            num_scalar_prefetch=1, grid=(bsz, nt), in_specs=in_specs, out_specs=out_specs, scratch_shapes=scratch),
        out_shape=out_shape,
        compiler_params=pltpu.CompilerParams(dimension_semantics=("arbitrary", "arbitrary"),
                                             vmem_limit_bytes=VMEM_LIMIT),
        name="prompt_mixer",
    )(sinks, x, mod, consts["anorm"], consts["fnorm"], consts["w_in"], consts["cos_p"], consts["sin_p"],
      consts["bias_p"], consts["conv_w"], consts["gn_attn"], consts["gn_conv"], consts["w_out"],
      consts["w_r"], consts["b_r"], consts["tri"])


def _sample_mixer_kernel(sinks_ref, x_ref, mod_ref, anorm_ref, fnorm_ref, win_ref, cos_ref, sin_ref, bias_ref,
                         convw_ref, gna_ref, gnc_ref, wout_ref, wr_ref, br_ref, tri_ref,
                         ck_ref, cv_ref, e1_ref, e2_ref, cnt_in_ref,
                         x1_ref, h2_ref, ri_ref, rf_ref, cnt_out_ref, knew_ref, vnew_ref, unew_ref,
                         q_s, k_s, v_s, att_s, cnt_s, *, dec):
    n = x_ref.shape[0]
    cb = SAMPLE_CHUNK
    rows = cb * dec
    pad_keys = SAMPLE_KEYS - cb * WINDOW - rows

    @pl.when(pl.program_id(0) == 0)
    def _():
        cnt_s[...] = cnt_in_ref[...]

    x = x_ref[...]
    mod = mod_ref[...]
    md = lambda j: mod[:, j * D_MODEL:(j + 1) * D_MODEL]
    h = _rms(x, anorm_ref[...]) * (1.0 + md(1)) + md(0)
    proj = jnp.dot(h.astype(jnp.bfloat16), win_ref[...], preferred_element_type=jnp.float32)
    q, k, v, u, gate_b = _split_proj(proj)
    cos = cos_ref[...]
    sin = sin_ref[...]
    q = _rope(q, cos, sin) * (HEAD_DIM ** -0.5)
    k = _rope(k, cos, sin)
    q_s[...] = q.astype(jnp.bfloat16)
    k_s[...] = k
    v_s[...] = v
    knew_ref[...] = k
    vnew_ref[...] = v
    unew_ref[...] = u

    pos = lax.broadcasted_iota(jnp.int32, (n, CONV_DIM), 0) % dec
    um1 = jnp.where(pos >= 1, pltpu.roll(u, 1, 0), 0.0) + e1_ref[...]
    um2 = jnp.where(pos >= 2, pltpu.roll(u, 2, 0), 0.0) + e2_ref[...]
    cw = convw_ref[...]
    conv = (um2 * cw[0:1] + um1 * cw[1:2] + u * cw[2:3]) * gate_b
    conv_n = _group_norm64(conv, gnc_ref[...])

    lo = _half_lane_mask(rows)
    bias = bias_ref[...]

    def chunk(c, carry):
        r0 = pl.multiple_of(c * rows, rows)
        c0 = pl.multiple_of(c * (cb * WINDOW), cb * WINDOW)
        zpad = jnp.zeros((pad_keys, KV_DIM), jnp.float32)
        kc = jnp.concatenate([ck_ref[pl.ds(c0, cb * WINDOW), :], k_s[pl.ds(r0, rows), :], zpad], axis=0)
        vc = jnp.concatenate([cv_ref[pl.ds(c0, cb * WINDOW), :], v_s[pl.ds(r0, rows), :], zpad], axis=0)
        keys, keys_sw = kc.astype(jnp.bfloat16), _swap_halves(kc).astype(jnp.bfloat16)
        vals, vals_sw = vc.astype(jnp.bfloat16), _swap_halves(vc).astype(jnp.bfloat16)
        qb = q_s[pl.ds(r0, rows), :]
        outs = []
        for g in range(N_KV_HEADS):
            outs += _attend(qb, keys, keys_sw, vals, vals_sw, bias, sinks_ref, g)
        att_s[pl.ds(r0, rows), :] = _merge_heads(outs, lo)
        return carry

    lax.fori_loop(0, n // rows, chunk, 0)
    attn_n = att_s[...] * gna_ref[...]
    _mixer_tail(x, attn_n, conv_n, (md(2), md(3), md(4)), wout_ref, fnorm_ref, wr_ref, br_ref, tri_ref,
                cnt_s, x1_ref, h2_ref, ri_ref, rf_ref)
    cnt_out_ref[...] = cnt_s[...]


def _sample_mixer(x, mod, sinks, consts, ck, cv, e1, e2, cnt_in, dec):
    n_s = x.shape[0]
    ts = SAMPLE_TILE
    seqs = ts // dec
    full = lambda a: pl.BlockSpec(a.shape, lambda i, s: (0,) * a.ndim)
    rows = lambda width: pl.BlockSpec((ts, width), lambda i, s: (i, 0))
    in_specs = [
        rows(D_MODEL), rows(6 * D_MODEL),
        full(consts["anorm"]), full(consts["fnorm"]), full(consts["w_in"]),
        rows(LANES), rows(LANES), full(consts["bias_s"]),
        full(consts["conv_w"]), full(consts["gn_attn"]), full(consts["gn_conv"]), full(consts["w_out"]),
        full(consts["w_r"]), full(consts["b_r"]),
        pl.BlockSpec((ts, ts), lambda i, s: (0, 0)),
        pl.BlockSpec((seqs * WINDOW, KV_DIM), lambda i, s: (i, 0)),
        pl.BlockSpec((seqs * WINDOW, KV_DIM), lambda i, s: (i, 0)),
        rows(CONV_DIM), rows(CONV_DIM), full(cnt_in),
    ]
    args = [x, mod, consts["anorm"], consts["fnorm"], consts["w_in"], consts["cos_s"], consts["sin_s"],
            consts["bias_s"], consts["conv_w"], consts["gn_attn"], consts["gn_conv"], consts["w_out"],
            consts["w_r"], consts["b_r"], consts["tri"], ck, cv, e1, e2, cnt_in]
    nt = n_s // ts
    out_shape = [
        jax.ShapeDtypeStruct((n_s, D_MODEL), jnp.float32),
        jax.ShapeDtypeStruct((n_s, D_MODEL // 2), jnp.uint32),
        jax.ShapeDtypeStruct((nt, 8, ts), jnp.int32),
        jax.ShapeDtypeStruct((nt, 8, ts), jnp.float32),
        jax.ShapeDtypeStruct((N_EXPERTS, LANES), jnp.float32),
        jax.ShapeDtypeStruct((n_s, KV_DIM), jnp.float32),
        jax.ShapeDtypeStruct((n_s, KV_DIM), jnp.float32),
        jax.ShapeDtypeStruct((n_s, CONV_DIM), jnp.float32),
    ]
    out_specs = [
        rows(D_MODEL),
        rows(D_MODEL // 2),
        pl.BlockSpec((None, 8, ts), lambda i, s: (i, 0, 0)),
        pl.BlockSpec((None, 8, ts), lambda i, s: (i, 0, 0)),
        pl.BlockSpec((N_EXPERTS, LANES), lambda i, s: (0, 0)),
        rows(KV_DIM), rows(KV_DIM), rows(CONV_DIM),
    ]
    scratch = [
        pltpu.VMEM((ts, ATTN_DIM), jnp.bfloat16),
        pltpu.VMEM((ts, KV_DIM), jnp.float32), pltpu.VMEM((ts, KV_DIM), jnp.float32),
        pltpu.VMEM((ts, ATTN_DIM), jnp.float32),
        pltpu.VMEM((N_EXPERTS, LANES), jnp.float32),
    ]
    return pl.pallas_call(
        functools.partial(_sample_mixer_kernel, dec=dec),
        grid_spec=pltpu.PrefetchScalarGridSpec(
            num_scalar_prefetch=1, grid=(nt,), in_specs=in_specs, out_specs=out_specs, scratch_shapes=scratch),
        out_shape=out_shape,
        compiler_params=pltpu.CompilerParams(dimension_semantics=("arbitrary",), vmem_limit_bytes=VMEM_LIMIT),
        name="sample_mixer",
    )(sinks, *args)


def _dispatch(idx, h2, n_rows):
    mesh = plsc.VectorSubcoreMesh(core_axis_name="core", subcore_axis_name="subcore")
    rows, window = SC_ROWS, SC_INDEX_WINDOW
    sub = window // rows
    n_tok, width = h2.shape
    assert n_tok % rows == 0 and (2 * n_tok) % window == 0

    moved = 3 * n_tok * width * h2.dtype.itemsize
    @pl.kernel(out_type=jax.ShapeDtypeStruct((n_rows, width), h2.dtype), mesh=mesh, scratch_types=[],
               cost_estimate=pl.CostEstimate(flops=0, transcendentals=0, bytes_accessed=moved))
    def scatter(h_hbm, i_hbm, xs_hbm):
        def body(indices, h_vmem, i_vmem):
            pltpu.sync_copy(h_vmem, xs_hbm.at[i_vmem.at[0, pl.ds(indices[1] * rows, rows)]])

        pltpu.emit_pipeline(
            body, grid=(2 * n_tok // window, sub),
            in_specs=[pl.BlockSpec((rows, width), lambda i, j: ((i * sub + j) % (n_tok // rows), 0)),
                      pl.BlockSpec((1, window), lambda i, j: (0, i))],
            out_specs=[], core_axis_name=("core", "subcore"),
            dimension_semantics=(pltpu.PARALLEL, pltpu.ARBITRARY), _explicit_indices=True,
        )(h_hbm, i_hbm)

    return scatter(h2, idx)


def _scatter_rows_kernel(dest_ref, h_ref, xs_in_ref, xs_ref, sem):
    del xs_in_ref
    n_tok = h_ref.shape[0]

    def issue(i, carry):
        for k in range(2):
            d = dest_ref[k * n_tok + i]
            pltpu.make_async_copy(h_ref.at[pl.ds(i, 1)], xs_ref.at[pl.ds(d, 1)], sem.at[0]).start(priority=k)
        return carry

    lax.fori_loop(0, n_tok, issue, 0, unroll=8)
    for _ in range(2):
        pltpu.make_async_copy(h_ref, xs_ref.at[pl.ds(0, n_tok)], sem.at[0]).wait()


def _scatter_rows(dest, h2, xs):
    return pl.pallas_call(
        _scatter_rows_kernel,
        grid_spec=pltpu.PrefetchScalarGridSpec(
            num_scalar_prefetch=1, grid=(1,),
            in_specs=[pl.BlockSpec(h2.shape, lambda i, d: (0, 0)), pl.BlockSpec(memory_space=pl.ANY)],
            out_specs=pl.BlockSpec(memory_space=pl.ANY),
            scratch_shapes=[pltpu.SemaphoreType.DMA((1,))]),
        out_shape=jax.ShapeDtypeStruct(xs.shape, xs.dtype),
        input_output_aliases={2: 0},
        compiler_params=pltpu.CompilerParams(dimension_semantics=("arbitrary",)),
        name="scatter_rows",
    )(dest, h2, xs)


def _experts_kernel(be_ref, nu_ref, nv_ref, nxt_ref, par_ref, phys_ref, xs_ref, wg_ref, wu_ref, wd_ref, y_ref,
                    wg_s, wu_s, wd_s, wg_f, wu_f, wd_f, sem):
    del phys_ref
    j = pl.program_id(0)
    used = j < nu_ref[0]
    changed = jnp.logical_or(j == 0, be_ref[j] != be_ref[jnp.maximum(j - 1, 0)])

    def weight_copies(e, slot):
        return [pltpu.make_async_copy(src.at[e], dst.at[slot], sem.at[slot])
                for src, dst in ((wg_ref, wg_f), (wu_ref, wu_f), (wd_ref, wd_f))]

    @pl.when(jnp.logical_and(used, changed))
    def _():
        slot = par_ref[j]

        @pl.when(j == 0)
        def _():
            for c in weight_copies(be_ref[0], slot):
                c.start()

        for c in weight_copies(be_ref[j], slot):
            c.wait()

        @pl.when(nxt_ref[j] >= 0)
        def _():
            for c in weight_copies(nxt_ref[j], 1 - slot):
                c.start()

        wg_s[...] = wg_f[slot].astype(jnp.bfloat16)
        wu_s[...] = wu_f[slot].astype(jnp.bfloat16)
        wd_s[...] = wd_f[slot].astype(jnp.bfloat16)

    @pl.when(used)
    def _():
        live = lax.broadcasted_iota(jnp.int32, (xs_ref.shape[0], 1), 0) < nv_ref[j]
        x = _unpack_bf16_pairs(jnp.where(live, xs_ref[...], jnp.uint32(0)))
        g = jnp.dot(x, wg_s[...], preferred_element_type=jnp.float32)
        u = jnp.dot(x, wu_s[...], preferred_element_type=jnp.float32)
        a = (g * jax.nn.sigmoid(g) * u).astype(jnp.bfloat16)
        y = jnp.dot(a, wd_s[...], preferred_element_type=jnp.float32)
        y_ref[...] = _pack_bf16_pairs(y.astype(jnp.bfloat16))


def _experts(block_e, n_used, n_valid, next_e, parity, phys, xs, w_gate, w_up, w_down):
    bm = EXPERT_BLOCK
    n_blocks = xs.shape[0] // bm
    hbm = pl.BlockSpec(memory_space=pl.ANY)
    return pl.pallas_call(
        _experts_kernel,
        grid_spec=pltpu.PrefetchScalarGridSpec(
            num_scalar_prefetch=6, grid=(n_blocks,),
            in_specs=[pl.BlockSpec((bm, D_MODEL // 2), lambda j, be, nu, nv, nx, pa, ph: (ph[j], 0)),
                      hbm, hbm, hbm],
            out_specs=pl.BlockSpec((bm, D_MODEL // 2), lambda j, be, nu, nv, nx, pa, ph: (ph[j], 0)),
            scratch_shapes=[pltpu.VMEM((D_MODEL, EXPERT_HIDDEN), jnp.bfloat16),
                            pltpu.VMEM((D_MODEL, EXPERT_HIDDEN), jnp.bfloat16),
                            pltpu.VMEM((EXPERT_HIDDEN, D_MODEL), jnp.bfloat16),
                            pltpu.VMEM((2, D_MODEL, EXPERT_HIDDEN), jnp.float32),
                            pltpu.VMEM((2, D_MODEL, EXPERT_HIDDEN), jnp.float32),
                            pltpu.VMEM((2, EXPERT_HIDDEN, D_MODEL), jnp.float32),
                            pltpu.SemaphoreType.DMA((2,))]),
        out_shape=jax.ShapeDtypeStruct((xs.shape[0], D_MODEL // 2), jnp.uint32),
        compiler_params=pltpu.CompilerParams(dimension_semantics=("arbitrary",), vmem_limit_bytes=VMEM_LIMIT),
        name="experts",
    )(block_e, n_used, n_valid, next_e, parity, phys, xs, w_gate, w_up, w_down)


def _gather_rows(idx, yb):
    mesh = plsc.VectorSubcoreMesh(core_axis_name="core", subcore_axis_name="subcore")
    rows, window = SC_ROWS, SC_INDEX_WINDOW
    sub = window // rows
    n = idx.shape[1]
    width = yb.shape[1]
    assert n % window == 0

    @pl.kernel(out_type=jax.ShapeDtypeStruct((n, width), yb.dtype), mesh=mesh, scratch_types=[])
    def gather(y_hbm, i_hbm, o_hbm):
        def body(indices, i_vmem, o_vmem):
            pltpu.sync_copy(y_hbm.at[i_vmem.at[0, pl.ds(indices[1] * rows, rows)]], o_vmem)

        pltpu.emit_pipeline(
            body, grid=(n // window, sub),
            in_specs=[pl.BlockSpec((1, window), lambda i, j: (0, i))],
            out_specs=[pl.BlockSpec((rows, width), lambda i, j: (i * sub + j, 0))],
            core_axis_name=("core", "subcore"),
            dimension_semantics=(pltpu.PARALLEL, pltpu.ARBITRARY), _explicit_indices=True,
        )(i_hbm, o_hbm)

    return gather(yb, idx)


def _combine_kernel(x1_ref, gate_ref, rf_ref, fin_ref, y0_ref, y1_ref, *out_refs, per_token_gate):
    o_ref = out_refs[-1]
    gates = jnp.transpose(rf_ref[...])
    moe = _unpack_pairs_f32(y0_ref[...]) * gates[:, 0:1] + _unpack_pairs_f32(y1_ref[...]) * gates[:, 1:2]
    gate_f = gate_ref[...] if per_token_gate else gate_ref[...][5:6]
    o_ref[...] = _rms(x1_ref[...] + gate_f * moe, fin_ref[...])


def _combine(x1, gate, rf, final_norm, ys, out_prev, first_tile, n_tiles, *, per_token_gate):
    tc = COMBINE_TILE
    gate_spec = (pl.BlockSpec((tc, D_MODEL), lambda i: (first_tile + i, 0)) if per_token_gate
                 else pl.BlockSpec((6, D_MODEL), lambda i: (0, 0)))
    args = [x1, gate, rf, final_norm, ys, ys]
    in_specs = [pl.BlockSpec((tc, D_MODEL), lambda i: (first_tile + i, 0)),
                gate_spec,
                pl.BlockSpec((8, tc), lambda i: (0, first_tile + i)),
                pl.BlockSpec((1, D_MODEL), lambda i: (0, 0)),
                pl.BlockSpec((tc, D_MODEL // 2), lambda i: (i, 0)),
                pl.BlockSpec((tc, D_MODEL // 2), lambda i: (n_tiles + i, 0))]
    aliases = {}
    if out_prev is not None:
        args.append(out_prev)
        in_specs.append(pl.BlockSpec(memory_space=pl.ANY))
        aliases = {len(args) - 1: 0}
    return pl.pallas_call(
        functools.partial(_combine_kernel, per_token_gate=per_token_gate),
        grid=(n_tiles,), in_specs=in_specs,
        out_specs=pl.BlockSpec((tc, D_MODEL), lambda i: (first_tile + i, 0)),
        out_shape=jax.ShapeDtypeStruct(x1.shape, jnp.float32),
        input_output_aliases=aliases,
        compiler_params=pltpu.CompilerParams(dimension_semantics=("arbitrary",), vmem_limit_bytes=VMEM_LIMIT),
        name="combine",
    )(*args)


def _rope_tables(pos):
    half = HEAD_DIM // 2
    inv_freq = np.power(np.float32(ROPE_THETA), -(np.arange(half, dtype=np.float32) / np.float32(half)))
    ang = (pos.astype(np.float32)[:, None] * inv_freq[None, :].astype(np.float32)).astype(np.float64)
    cos, sin = np.cos(ang).astype(np.float32), np.sin(ang).astype(np.float32)
    cos = np.concatenate([cos, cos, cos, cos], axis=1)
    sin = np.concatenate([-sin, sin, -sin, sin], axis=1)
    return cos, sin


def _prompt_bias():
    r = np.arange(WINDOW)[:, None]
    j = np.arange(2 * WINDOW)[None, :]
    band = (j > r) & (j <= r + WINDOW)
    later = np.where(band, 0.0, NEG).astype(np.float32)
    first = np.where(band & (j >= WINDOW), 0.0, NEG).astype(np.float32)
    return jnp.asarray(np.stack([first, later]))


def _sample_bias(dec):
    cb = SAMPLE_CHUNK
    qb = np.repeat(np.arange(cb), dec)[:, None]
    qi = np.tile(np.arange(dec), cb)[:, None]
    kb = np.concatenate([np.repeat(np.arange(cb), WINDOW), np.repeat(np.arange(cb), dec)])[None, :]
    kj = np.concatenate([np.tile(np.arange(WINDOW), cb), np.tile(np.arange(dec), cb)])[None, :]
    is_new = (np.arange(cb * WINDOW + cb * dec) >= cb * WINDOW)[None, :]
    ok = (kb == qb) & np.where(is_new, kj <= qi, kj > qi)
    ok = np.pad(ok, ((0, 0), (0, SAMPLE_KEYS - ok.shape[1])))
    return jnp.asarray(np.where(ok, 0.0, NEG).astype(np.float32))


def kernel(x_prompt, x_sample, cache_k, cache_v, state_conv, c_prompt, c_sample, attn_norm, ffn_norm, w_mod, b_mod, w_in, conv_w, attn_sinks, out_norm_attn, out_norm_conv, w_out, w_group, b_group, w_expert, b_expert, w_gate, w_up, w_down, final_norm):
    assert attn_norm.shape[0] == 1, "one layer"
    bsz, seq, _ = x_prompt.shape
    nseq, dec, _ = x_sample.shape
    m_prompt = bsz * seq
    n_s = nseq * dec
    m_total = m_prompt + n_s
    tm = MIXER_TILE
    assert seq % tm == 0 and n_s % SAMPLE_TILE == 0 and SAMPLE_TILE % (SAMPLE_CHUNK * dec) == 0
    assert cache_k.shape[2] == WINDOW and m_prompt % SAMPLE_TILE == 0 and m_total % COMBINE_TILE == 0
    bf = jnp.bfloat16

    w_r = jnp.zeros((ROUTER_ROWS, D_MODEL), jnp.float32)
    w_r = w_r.at[0:N_GROUPS].set(w_group[0].T).at[8:8 + N_EXPERTS].set(w_expert[0].T).astype(bf)
    b_r = jnp.zeros((ROUTER_ROWS,), jnp.float32).at[0:N_GROUPS].set(b_group[0]).at[8:8 + N_EXPERTS].set(b_expert[0])
    b_r = jnp.broadcast_to(b_r[:, None], (ROUTER_ROWS, LANES))
    cos_p, sin_p = _rope_tables(np.arange(seq))
    cos_s, sin_s = _rope_tables(PAST_LEN + np.arange(dec))
    tri = jnp.asarray(np.triu(np.ones((tm, tm), np.float32), 1)).astype(bf)
    consts = dict(
        anorm=attn_norm[0][None], fnorm=ffn_norm[0][None], w_in=w_in[0].astype(bf),
        cos_p=jnp.asarray(cos_p), sin_p=jnp.asarray(sin_p),
        cos_s=jnp.asarray(np.tile(cos_s, (nseq, 1))), sin_s=jnp.asarray(np.tile(sin_s, (nseq, 1))),
        bias_p=_prompt_bias(), bias_s=_sample_bias(dec), conv_w=conv_w[0],
        gn_attn=out_norm_attn[0][None], gn_conv=out_norm_conv[0][None], w_out=w_out[0].astype(bf),
        w_r=w_r, b_r=b_r, tri=tri)
    sinks = attn_sinks[0]

    mod = _adaln(jnp.concatenate([c_prompt, c_sample], axis=0), w_mod[0], b_mod[0])
    mod_p = mod[:bsz].reshape(bsz, 6, D_MODEL)
    mod_s = jnp.repeat(mod[bsz:], dec, axis=0)

    bm = EXPERT_BLOCK
    eids = jnp.arange(N_EXPERTS, dtype=jnp.int32)
    lookup = lambda table, e: jnp.sum(jnp.where(e[..., None] == eids, table, 0), axis=-1)
    by_token = lambda a: jnp.transpose(a, (1, 0, 2)).reshape(8, -1)
    ceil_bm = lambda c: (c + bm - 1) // bm * bm
    x1_p, h2_p, ri_p, rf_p, cnt_p, k_p, v_p, u_p = _prompt_mixer(x_prompt, mod_p, sinks, consts)
    cnt_p = cnt_p[:, 0].astype(jnp.int32)
    padded_p = ceil_bm(cnt_p)
    pstart_p = jnp.cumsum(padded_p) - padded_p
    nb_p = -(-(2 * m_prompt + N_EXPERTS * (bm - 1)) // bm)
    nb_o = -(-(2 * n_s + N_EXPERTS * (bm - 1)) // bm)
    n_blocks = nb_p + nb_o
    ri_p = by_token(ri_p)
    dest_p = (lookup(pstart_p, ri_p[0:2]) + ri_p[2:4]).reshape(2 * m_prompt)

    st = state_conv[0]
    zrow = jnp.zeros((nseq, 1, CONV_DIM), jnp.float32)
    e1 = jnp.concatenate([st[:, 1:2]] + [zrow] * (dec - 1), axis=1).reshape(n_s, CONV_DIM)
    e2 = jnp.concatenate([st[:, 0:1], st[:, 1:2]] + [zrow] * (dec - 2), axis=1).reshape(n_s, CONV_DIM)
    dest_p, e1 = lax.optimization_barrier((dest_p, e1))
    xs = _dispatch(dest_p[None], h2_p, n_blocks * bm)
    ck = cache_k[0].reshape(nseq * WINDOW, KV_DIM)
    cv = cache_v[0].reshape(nseq * WINDOW, KV_DIM)
    x1_s, h2_s, ri_s, rf_s, cnt_s, k_s, v_s, u_s = _sample_mixer(
        x_sample.reshape(n_s, D_MODEL), mod_s, sinks, consts, ck, cv, e1, e2,
        jnp.zeros((N_EXPERTS, LANES), jnp.float32), dec)

    cnt_s = cnt_s[:, 0].astype(jnp.int32)
    fit = jnp.minimum(cnt_s, padded_p - cnt_p)
    over = cnt_s - fit
    padded_o = ceil_bm(over)
    ostart = nb_p * bm + jnp.cumsum(padded_o) - padded_o
    ri_s = by_token(ri_s)
    e_s, r_s = ri_s[0:2], ri_s[2:4]
    dest_s = jnp.where(r_s < lookup(fit, e_s), lookup(pstart_p + cnt_p, e_s) + r_s, lookup(ostart - fit, e_s) + r_s)
    dest_s = dest_s.reshape(2 * n_s)

    ksn = k_s.reshape(nseq, dec, N_KV_HEADS, HEAD_DIM)
    vsn = v_s.reshape(nseq, dec, N_KV_HEADS, HEAD_DIM)
    new_k_s = jnp.concatenate([cache_k[0][:, dec:], ksn], axis=1)[None]
    new_v_s = jnp.concatenate([cache_v[0][:, dec:], vsn], axis=1)[None]
    new_k_s, new_v_s, dest_s = lax.optimization_barrier((new_k_s, new_v_s, dest_s))
    xs = _scatter_rows(dest_s, h2_s, xs)

    np_e, no_e = padded_p // bm, padded_o // bm
    lend = jnp.cumsum(np_e + no_e)
    lstart = lend - np_e - no_e
    n_used = lend[-1].astype(jnp.int32)
    blk = jnp.arange(n_blocks, dtype=jnp.int32)
    present = np_e + no_e > 0
    last_e = jnp.max(jnp.where(present, eids, 0))
    block_e = jnp.sum((lend[None, :] <= blk[:, None]).astype(jnp.int32), axis=1)
    block_e = jnp.where(blk < n_used, jnp.minimum(block_e, N_EXPERTS - 1), last_e)
    of_block = lambda table: lookup(table, block_e)
    within = blk - of_block(lstart)
    in_prompt = within < of_block(np_e)
    phys = jnp.where(in_prompt, of_block(pstart_p) // bm + within, of_block(ostart) // bm + within - of_block(np_e))
    n_valid = jnp.where(in_prompt, of_block(cnt_p + fit) - within * bm, of_block(over) - (within - of_block(np_e)) * bm)
    n_valid = jnp.where(blk < n_used, jnp.clip(n_valid, 0, bm), 0).astype(jnp.int32)
    phys = jnp.where(blk < n_used, phys, jnp.sum(jnp.where(blk == n_used - 1, phys, 0))).astype(jnp.int32)
    later = jnp.logical_and(present[None, :], eids[None, :] > eids[:, None])
    next_of = jnp.min(jnp.where(later, eids[None, :], N_EXPERTS), axis=1)
    next_of = jnp.where(next_of == N_EXPERTS, -1, next_of)
    next_e = of_block(next_of).astype(jnp.int32)
    parity = of_block((jnp.cumsum(present.astype(jnp.int32)) - 1) % 2).astype(jnp.int32)
    yb = _experts(block_e, n_used.reshape(1), n_valid, next_e, parity, phys, xs, w_gate[0], w_up[0], w_down[0])

    fin = final_norm[None]
    tc = COMBINE_TILE
    rf_p, rf_s = by_token(rf_p), by_token(rf_s)
    y_p = None
    chunk = min(COMBINE_CHUNK, seq)
    assert seq % chunk == 0 and chunk % tc == 0
    for c in range(m_prompt // chunk):
        lo, hi = c * chunk, (c + 1) * chunk
        idx = jnp.concatenate([dest_p[lo:hi], dest_p[m_prompt + lo:m_prompt + hi]])[None]
        y_p = _combine(x1_p, mod_p[lo // seq], rf_p, fin, _gather_rows(idx, yb), y_p, lo // tc, chunk // tc,
                       per_token_gate=False)
    gate_s = mod_s[:, 5 * D_MODEL:]
    y_s = _combine(x1_s, gate_s, rf_s, fin, _gather_rows(dest_s[None], yb), None, 0, n_s // tc, per_token_gate=True)

    keep = WINDOW
    new_k_p = k_p.reshape(1, bsz, keep, N_KV_HEADS, HEAD_DIM)
    new_v_p = v_p.reshape(1, bsz, keep, N_KV_HEADS, HEAD_DIM)
    new_u_p = u_p[:, 6:8][None]
    u_all = jnp.concatenate([st, u_s.reshape(nseq, dec, CONV_DIM)], axis=1)
    new_u_s = u_all[:, -2:][None]
    return (y_p.reshape(bsz, seq, D_MODEL), y_s.reshape(nseq, dec, D_MODEL),
            new_k_p, new_v_p, new_u_p, new_k_s, new_v_s, new_u_s)
```

```python
import functools

import jax
import jax.numpy as jnp
import numpy as np
from jax import lax
from jax.experimental import pallas as pl
from jax.experimental.pallas import tpu as pltpu
from jax.experimental.pallas import tpu_sc as plsc

D_MODEL = 1024
HEAD_DIM = 64
ATTN_DIM = 512
N_Q_HEADS = 8
N_KV_HEADS = 2
KV_DIM = 128
CONV_DIM = 512
WINDOW = 128
IN_PROJ_DIM = ATTN_DIM + 2 * KV_DIM + 3 * CONV_DIM
N_GROUPS = 4
EXPERTS_PER_GROUP = 8
N_EXPERTS = 32
EXPERT_HIDDEN = 512
RMS_EPS = 1e-5
ROPE_THETA = 10000.0
PAST_LEN = 16384

LANES = 128
ROUTER_ROWS = 48
NEG = -1e30

MIXER_TILE = 512
SAMPLE_TILE = 128
SAMPLE_CHUNK = 8
SAMPLE_KEYS = 1152
EXPERT_BLOCK = 512
COMBINE_TILE = 512
COMBINE_CHUNK = 4096
SC_ROWS = 64
SC_INDEX_WINDOW = 128
VMEM_LIMIT = 56 * 1024 * 1024


def _rms(x, gain):
    ms = jnp.mean(x * x, axis=-1, keepdims=True)
    return x * lax.rsqrt(ms + RMS_EPS) * gain


def _half_lane_mask(rows=1):
    return lax.broadcasted_iota(jnp.int32, (rows, LANES), 1) < HEAD_DIM


def _rope(x, cos, sin_signed):
    t, w = x.shape
    reps = w // LANES
    lane = lax.broadcasted_iota(jnp.int32, (t, w), 1)
    upper = (lane % HEAD_DIM) >= (HEAD_DIM // 2)
    partner = jnp.where(upper, pltpu.roll(x, HEAD_DIM // 2, 1), pltpu.roll(x, w - HEAD_DIM // 2, 1))
    if reps > 1:
        cos = jnp.concatenate([cos] * reps, axis=1)
        sin_signed = jnp.concatenate([sin_signed] * reps, axis=1)
    return x * cos + partner * sin_signed


def _swap_halves(x):
    return pltpu.roll(x, HEAD_DIM, 1)


def _group_norm64(x, gain):
    t, w = x.shape
    lo = _half_lane_mask(t)
    outs = []
    for c in range(w // LANES):
        xc = x[:, c * LANES:(c + 1) * LANES]
        sq = xc * xc
        s_lo = jnp.sum(jnp.where(lo, sq, 0.0), axis=-1, keepdims=True)
        s_hi = jnp.sum(jnp.where(lo, 0.0, sq), axis=-1, keepdims=True)
        r = jnp.where(lo, lax.rsqrt(s_lo * (1.0 / HEAD_DIM) + RMS_EPS), lax.rsqrt(s_hi * (1.0 / HEAD_DIM) + RMS_EPS))
        outs.append(xc * r)
    return jnp.concatenate(outs, axis=1) * gain


def _attend(q_rows, keys, keys_sw, vals, vals_sw, bias, sinks_ref, group):
    r = q_rows.shape[0]
    lo = _half_lane_mask(r)
    outs = []
    for parity in range(2):
        heads = [4 * group + parity, 4 * group + parity + 2]
        kk, vv = (keys, vals) if parity == group else (keys_sw, vals_sw)
        half = lo if parity == 0 else jnp.logical_not(lo)
        qs = []
        for h in heads:
            pair = q_rows[:, (h // 2) * LANES:(h // 2 + 1) * LANES]
            qs.append(jnp.where(half, pair, jnp.zeros_like(pair)))
        qcat = jnp.concatenate(qs, axis=0)
        s = lax.dot_general(qcat, kk, (((1,), (1,)), ((), ())), preferred_element_type=jnp.float32)
        ps, dens = [], []
        for i, h in enumerate(heads):
            sh = s[i * r:(i + 1) * r] + bias
            sink = sinks_ref[h]
            m = jnp.maximum(jnp.max(sh, axis=-1, keepdims=True), sink)
            p = jnp.exp(sh - m)
            dens.append(jnp.sum(p, axis=-1, keepdims=True) + jnp.exp(sink - m))
            ps.append(p.astype(jnp.bfloat16))
        o = jnp.dot(jnp.concatenate(ps, axis=0), vv, preferred_element_type=jnp.float32)
        for i, h in enumerate(heads):
            oh = o[i * r:(i + 1) * r] / dens[i]
            ms = jnp.sum(jnp.where(half, oh * oh, 0.0), axis=-1, keepdims=True) * (1.0 / HEAD_DIM)
            outs.append((h, oh * lax.rsqrt(ms + RMS_EPS)))
    outs.sort(key=lambda t: t[0])
    return [o for _, o in outs]


def _merge_heads(head_outs, lo):
    pairs = [jnp.where(lo, head_outs[2 * j], head_outs[2 * j + 1]) for j in range(4)]
    return jnp.concatenate(pairs, axis=1)


def _route(h2b, wr_ref, br_ref, tri_ref, cnt_ref, ri_ref, rf_ref):
    t = h2b.shape[0]
    lg = lax.dot_general(wr_ref[...], h2b, (((1,), (1,)), ((), ())), preferred_element_type=jnp.float32)
    lg = lg + br_ref[...][:, 0:1]
    r8 = lax.broadcasted_iota(jnp.int32, (8, t), 0)
    lgrp = jnp.where(r8 < N_GROUPS, lg[0:8], -jnp.inf)
    gmax = jnp.max(lgrp, axis=0, keepdims=True)
    grp = jnp.min(jnp.where(lgrp == gmax, r8, 8), axis=0, keepdims=True)
    pg_sel = 1.0 / jnp.sum(jnp.exp(lgrp - gmax), axis=0, keepdims=True)
    r32 = lax.broadcasted_iota(jnp.int32, (N_EXPERTS, t), 0)
    le = jnp.where((r32 // EXPERTS_PER_GROUP) == grp, lg[8:8 + N_EXPERTS], -jnp.inf)
    v1 = jnp.max(le, axis=0, keepdims=True)
    i1 = jnp.min(jnp.where(le == v1, r32, N_EXPERTS), axis=0, keepdims=True)
    le2 = jnp.where(r32 == i1, -jnp.inf, le)
    v2 = jnp.max(le2, axis=0, keepdims=True)
    i2 = jnp.min(jnp.where(le2 == v2, r32, N_EXPERTS), axis=0, keepdims=True)
    e21 = jnp.exp(v2 - v1)
    gate1 = pg_sel / (1.0 + e21)
    gate2 = pg_sel * e21 / (1.0 + e21)
    hot1 = r32 == i1
    hot2 = r32 == i2
    onehot = jnp.where(jnp.logical_or(hot1, hot2), 1.0, 0.0)
    before = jnp.dot(onehot.astype(jnp.bfloat16), tri_ref[...], preferred_element_type=jnp.float32)
    pos = before + cnt_ref[...][:, 0:1]
    rank1 = jnp.sum(jnp.where(hot1, pos, 0.0), axis=0, keepdims=True)
    rank2 = jnp.sum(jnp.where(hot2, pos, 0.0), axis=0, keepdims=True)
    cnt_ref[...] = cnt_ref[...] + jnp.sum(onehot, axis=1, keepdims=True)
    zi = jnp.zeros((4, t), jnp.int32)
    ri_ref[...] = jnp.concatenate([i1, i2, rank1.astype(jnp.int32), rank2.astype(jnp.int32), zi], axis=0)
    rf_ref[...] = jnp.concatenate([gate1, gate2, jnp.zeros((6, t), jnp.float32)], axis=0)


def _pack_bf16_pairs(xb):
    w = xb.shape[1] // 2
    bits = pltpu.bitcast(xb.astype(jnp.float32), jnp.uint32)
    return (bits[:, :w] >> 16) | (bits[:, w:] & jnp.uint32(0xFFFF0000))


def _unpack_pairs_f32(words):
    lo = pltpu.bitcast(words << 16, jnp.float32)
    hi = pltpu.bitcast(words & jnp.uint32(0xFFFF0000), jnp.float32)
    return jnp.concatenate([lo, hi], axis=1)


def _unpack_bf16_pairs(words):
    return _unpack_pairs_f32(words).astype(jnp.bfloat16)


def _split_proj(proj):
    a = ATTN_DIM
    q = proj[:, :a]
    k = proj[:, a:a + KV_DIM]
    v = proj[:, a + KV_DIM:a + 2 * KV_DIM]
    c0 = a + 2 * KV_DIM
    h_conv = proj[:, c0:c0 + CONV_DIM]
    gate_b = proj[:, c0 + CONV_DIM:c0 + 2 * CONV_DIM]
    gate_c = proj[:, c0 + 2 * CONV_DIM:c0 + 3 * CONV_DIM]
    return q, k, v, gate_c * h_conv, gate_b


def _mixer_tail(x, attn_n, conv_n, mod, wout_ref, fnorm_ref, wr_ref, br_ref, tri_ref, cnt_ref,
                x1_ref, h2_ref, ri_ref, rf_ref):
    gate_a, shift_f, scale_f = mod
    cat = jnp.concatenate([attn_n.astype(jnp.bfloat16), conv_n.astype(jnp.bfloat16)], axis=1)
    mix = jnp.dot(cat, wout_ref[...], preferred_element_type=jnp.float32)
    x1 = x + gate_a * mix
    x1_ref[...] = x1
    h2 = _rms(x1, fnorm_ref[...] * (1.0 + scale_f)) + shift_f
    h2b = h2.astype(jnp.bfloat16)
    h2_ref[...] = _pack_bf16_pairs(h2b)
    _route(h2b, wr_ref, br_ref, tri_ref, cnt_ref, ri_ref, rf_ref)


def _adaln_kernel(c_ref, w_ref, b_ref, o_ref):
    c = c_ref[...]
    a = (c * jax.nn.sigmoid(c)).astype(jnp.bfloat16)
    o_ref[...] = jnp.dot(a, w_ref[...].astype(jnp.bfloat16), preferred_element_type=jnp.float32) + b_ref[...]


def _adaln(c, w_mod, b_mod):
    n = c.shape[0]
    tn = 1536
    return pl.pallas_call(
        _adaln_kernel,
        grid=(w_mod.shape[1] // tn,),
        in_specs=[pl.BlockSpec((n, D_MODEL), lambda j: (0, 0)),
                  pl.BlockSpec((D_MODEL, tn), lambda j: (0, j)),
                  pl.BlockSpec((1, tn), lambda j: (0, j))],
        out_specs=pl.BlockSpec((n, tn), lambda j: (0, j)),
        out_shape=jax.ShapeDtypeStruct((n, w_mod.shape[1]), jnp.float32),
        compiler_params=pltpu.CompilerParams(dimension_semantics=("arbitrary",), vmem_limit_bytes=VMEM_LIMIT),
        name="adaln",
    )(c, w_mod, b_mod.reshape(1, -1))


def _prompt_mixer_kernel(sinks_ref, x_ref, mod_ref, anorm_ref, fnorm_ref, win_ref, cos_ref, sin_ref, bias_ref,
                         convw_ref, gna_ref, gnc_ref, wout_ref, wr_ref, br_ref, tri_ref,
                         x1_ref, h2_ref, ri_ref, rf_ref, cnt_out_ref, knew_ref, vnew_ref, unew_ref,
                         q_s, k_s, ksw_s, v_s, vsw_s, u_s, att_s, cnt_s):
    b = pl.program_id(0)
    t = pl.program_id(1)
    tm = MIXER_TILE
    nblk = tm // WINDOW

    @pl.when(jnp.logical_and(b == 0, t == 0))
    def _():
        cnt_s[...] = jnp.zeros_like(cnt_s)

    @pl.when(t == 0)
    def _():
        z = jnp.zeros((WINDOW, KV_DIM), jnp.bfloat16)
        k_s[0:WINDOW, :] = z
        ksw_s[0:WINDOW, :] = z
        v_s[0:WINDOW, :] = z
        vsw_s[0:WINDOW, :] = z
        u_s[0:8, :] = jnp.zeros((8, CONV_DIM), jnp.float32)

    x = x_ref[...]
    mod = mod_ref[...]
    h = _rms(x, anorm_ref[...] * (1.0 + mod[1:2])) + mod[0:1]
    proj = jnp.dot(h.astype(jnp.bfloat16), win_ref[...], preferred_element_type=jnp.float32)
    q, k, v, u, gate_b = _split_proj(proj)
    cos = cos_ref[...]
    sin = sin_ref[...]
    q = _rope(q, cos, sin) * (HEAD_DIM ** -0.5)
    k = _rope(k, cos, sin)
    q_s[...] = q.astype(jnp.bfloat16)
    k_s[WINDOW:, :] = k.astype(jnp.bfloat16)
    ksw_s[WINDOW:, :] = _swap_halves(k).astype(jnp.bfloat16)
    v_s[WINDOW:, :] = v.astype(jnp.bfloat16)
    vsw_s[WINDOW:, :] = _swap_halves(v).astype(jnp.bfloat16)
    knew_ref[...] = k[tm - WINDOW:, :]
    vnew_ref[...] = v[tm - WINDOW:, :]

    u_s[8:, :] = u
    unew_ref[...] = u[tm - 8:, :]
    cw = convw_ref[...]
    conv = (u_s[6:6 + tm, :] * cw[0:1] + u_s[7:7 + tm, :] * cw[1:2] + u * cw[2:3]) * gate_b
    u_s[0:8, :] = u[tm - 8:, :]
    conv_n = _group_norm64(conv, gnc_ref[...])

    lo = _half_lane_mask(WINDOW)
    first = t == 0

    def block(i, carry):
        r0 = pl.multiple_of(i * WINDOW, WINDOW)
        bias = bias_ref[jnp.where(jnp.logical_and(first, i == 0), 0, 1)]
        qb = q_s[pl.ds(r0, WINDOW), :]
        keys = k_s[pl.ds(r0, 2 * WINDOW), :]
        keys_sw = ksw_s[pl.ds(r0, 2 * WINDOW), :]
        vals = v_s[pl.ds(r0, 2 * WINDOW), :]
        vals_sw = vsw_s[pl.ds(r0, 2 * WINDOW), :]
        outs = []
        for g in range(N_KV_HEADS):
            outs += _attend(qb, keys, keys_sw, vals, vals_sw, bias, sinks_ref, g)
        att_s[pl.ds(r0, WINDOW), :] = _merge_heads(outs, lo)
        return carry

    lax.fori_loop(0, nblk, block, 0, unroll=True)
    k_s[0:WINDOW, :] = k_s[tm:tm + WINDOW, :]
    ksw_s[0:WINDOW, :] = ksw_s[tm:tm + WINDOW, :]
    v_s[0:WINDOW, :] = v_s[tm:tm + WINDOW, :]
    vsw_s[0:WINDOW, :] = vsw_s[tm:tm + WINDOW, :]

    attn_n = att_s[...] * gna_ref[...]
    _mixer_tail(x, attn_n, conv_n, (mod[2:3], mod[3:4], mod[4:5]), wout_ref, fnorm_ref, wr_ref, br_ref, tri_ref,
                cnt_s, x1_ref, h2_ref, ri_ref, rf_ref)
    cnt_out_ref[...] = cnt_s[...]


def _prompt_mixer(x, mod, sinks, consts):
    bsz, seq, _ = x.shape
    tm = MIXER_TILE
    nt = seq // tm
    full = lambda shape: pl.BlockSpec(shape, lambda b, t, s: (0,) * len(shape))
    in_specs = [
        pl.BlockSpec((None, tm, D_MODEL), lambda b, t, s: (b, t, 0)),
        pl.BlockSpec((None, 6, D_MODEL), lambda b, t, s: (b, 0, 0)),
        full((1, D_MODEL)), full((1, D_MODEL)),
        full((D_MODEL, IN_PROJ_DIM)),
        pl.BlockSpec((tm, LANES), lambda b, t, s: (t, 0)),
        pl.BlockSpec((tm, LANES), lambda b, t, s: (t, 0)),
        full((2, WINDOW, 2 * WINDOW)),
        full((3, CONV_DIM)), full((1, ATTN_DIM)), full((1, CONV_DIM)),
        full((D_MODEL, D_MODEL)),
        full((ROUTER_ROWS, D_MODEL)), full((ROUTER_ROWS, LANES)),
        full((tm, tm)),
    ]
    out_shape = [
        jax.ShapeDtypeStruct((bsz * seq, D_MODEL), jnp.float32),
        jax.ShapeDtypeStruct((bsz * seq, D_MODEL // 2), jnp.uint32),
        jax.ShapeDtypeStruct((bsz * nt, 8, tm), jnp.int32),
        jax.ShapeDtypeStruct((bsz * nt, 8, tm), jnp.float32),
        jax.ShapeDtypeStruct((N_EXPERTS, LANES), jnp.float32),
        jax.ShapeDtypeStruct((bsz, WINDOW, KV_DIM), jnp.float32),
        jax.ShapeDtypeStruct((bsz, WINDOW, KV_DIM), jnp.float32),
        jax.ShapeDtypeStruct((bsz, 8, CONV_DIM), jnp.float32),
    ]
    out_specs = [
        pl.BlockSpec((tm, D_MODEL), lambda b, t, s: (b * nt + t, 0)),
        pl.BlockSpec((tm, D_MODEL // 2), lambda b, t, s: (b * nt + t, 0)),
        pl.BlockSpec((None, 8, tm), lambda b, t, s: (b * nt + t, 0, 0)),
        pl.BlockSpec((None, 8, tm), lambda b, t, s: (b * nt + t, 0, 0)),
        pl.BlockSpec((N_EXPERTS, LANES), lambda b, t, s: (0, 0)),
        pl.BlockSpec((None, WINDOW, KV_DIM), lambda b, t, s: (b, 0, 0)),
        pl.BlockSpec((None, WINDOW, KV_DIM), lambda b, t, s: (b, 0, 0)),
        pl.BlockSpec((None, 8, CONV_DIM), lambda b, t, s: (b, 0, 0)),
    ]
    scratch = [
        pltpu.VMEM((tm, ATTN_DIM), jnp.bfloat16),
        pltpu.VMEM((tm + WINDOW, KV_DIM), jnp.bfloat16), pltpu.VMEM((tm + WINDOW, KV_DIM), jnp.bfloat16),
        pltpu.VMEM((tm + WINDOW, KV_DIM), jnp.bfloat16), pltpu.VMEM((tm + WINDOW, KV_DIM), jnp.bfloat16),
        pltpu.VMEM((tm + 8, CONV_DIM), jnp.float32),
        pltpu.VMEM((tm, ATTN_DIM), jnp.float32),
        pltpu.VMEM((N_EXPERTS, LANES), jnp.float32),
    ]
    return pl.pallas_call(
        _prompt_mixer_kernel,
        grid_spec=pltpu.PrefetchScalarGridSpec(
            num_scalar_prefetch=1, grid=(bsz, nt), in_specs=in_specs, out_specs=out_specs, scratch_shapes=scratch),
        out_shape=out_shape,
        compiler_params=pltpu.CompilerParams(dimension_semantics=("arbitrary", "arbitrary"),
                                             vmem_limit_bytes=VMEM_LIMIT),
        name="prompt_mixer",
    )(sinks, x, mod, consts["anorm"], consts["fnorm"], consts["w_in"], consts["cos_p"], consts["sin_p"],
      consts["bias_p"], consts["conv_w"], consts["gn_attn"], consts["gn_conv"], consts["w_out"],
      consts["w_r"], consts["b_r"], consts["tri"])


def _sample_mixer_kernel(sinks_ref, x_ref, mod_ref, anorm_ref, fnorm_ref, win_ref, cos_ref, sin_ref, bias_ref,
                         convw_ref, gna_ref, gnc_ref, wout_ref, wr_ref, br_ref, tri_ref,
                         ck_ref, cv_ref, e1_ref, e2_ref, cnt_in_ref,
                         x1_ref, h2_ref, ri_ref, rf_ref, cnt_out_ref, knew_ref, vnew_ref, unew_ref,
                         q_s, k_s, v_s, att_s, cnt_s, *, dec):
    n = x_ref.shape[0]
    cb = SAMPLE_CHUNK
    rows = cb * dec
    pad_keys = SAMPLE_KEYS - cb * WINDOW - rows

    @pl.when(pl.program_id(0) == 0)
    def _():
        cnt_s[...] = cnt_in_ref[...]

    x = x_ref[...]
    mod = jnp.repeat(mod_ref[...], dec, axis=0)
    md = lambda j: mod[:, j * D_MODEL:(j + 1) * D_MODEL]
    h = _rms(x, anorm_ref[...]) * (1.0 + md(1)) + md(0)
    proj = jnp.dot(h.astype(jnp.bfloat16), win_ref[...], preferred_element_type=jnp.float32)
    q, k, v, u, gate_b = _split_proj(proj)
    cos = cos_ref[...]
    sin = sin_ref[...]
    q = _rope(q, cos, sin) * (HEAD_DIM ** -0.5)
    k = _rope(k, cos, sin)
    q_s[...] = q.astype(jnp.bfloat16)
    k_s[...] = k
    v_s[...] = v
    knew_ref[...] = k
    vnew_ref[...] = v
    unew_ref[...] = u

    pos = lax.broadcasted_iota(jnp.int32, (n, CONV_DIM), 0) % dec
    um1 = jnp.where(pos >= 1, pltpu.roll(u, 1, 0), 0.0) + e1_ref[...]
    um2 = jnp.where(pos >= 2, pltpu.roll(u, 2, 0), 0.0) + e2_ref[...]
    cw = convw_ref[...]
    conv = (um2 * cw[0:1] + um1 * cw[1:2] + u * cw[2:3]) * gate_b
    conv_n = _group_norm64(conv, gnc_ref[...])

    lo = _half_lane_mask(rows)
    bias = bias_ref[...]

    def chunk(c, carry):
        r0 = pl.multiple_of(c * rows, rows)
        c0 = pl.multiple_of(c * (cb * WINDOW), cb * WINDOW)
        zpad = jnp.zeros((pad_keys, KV_DIM), jnp.float32)
        kc = jnp.concatenate([ck_ref[pl.ds(c0, cb * WINDOW), :], k_s[pl.ds(r0, rows), :], zpad], axis=0)
        vc = jnp.concatenate([cv_ref[pl.ds(c0, cb * WINDOW), :], v_s[pl.ds(r0, rows), :], zpad], axis=0)
        keys, keys_sw = kc.astype(jnp.bfloat16), _swap_halves(kc).astype(jnp.bfloat16)
        vals, vals_sw = vc.astype(jnp.bfloat16), _swap_halves(vc).astype(jnp.bfloat16)
        qb = q_s[pl.ds(r0, rows), :]
        outs = []
        for g in range(N_KV_HEADS):
            outs += _attend(qb, keys, keys_sw, vals, vals_sw, bias, sinks_ref, g)
        att_s[pl.ds(r0, rows), :] = _merge_heads(outs, lo)
        return carry

    lax.fori_loop(0, n // rows, chunk, 0, unroll=True)
    attn_n = att_s[...] * gna_ref[...]
    _mixer_tail(x, attn_n, conv_n, (md(2), md(3), md(4)), wout_ref, fnorm_ref, wr_ref, br_ref, tri_ref,
                cnt_s, x1_ref, h2_ref, ri_ref, rf_ref)
    cnt_out_ref[...] = cnt_s[...]


def _sample_mixer(x, mod, sinks, consts, ck, cv, e1, e2, cnt_in, dec):
    n_s = x.shape[0]
    ts = SAMPLE_TILE
    seqs = ts // dec
    full = lambda a: pl.BlockSpec(a.shape, lambda i, s: (0,) * a.ndim)
    rows = lambda width: pl.BlockSpec((ts, width), lambda i, s: (i, 0))
    in_specs = [
        rows(D_MODEL), pl.BlockSpec((seqs, 6 * D_MODEL), lambda i, s: (i, 0)),
        full(consts["anorm"]), full(consts["fnorm"]), full(consts["w_in"]),
        rows(LANES), rows(LANES), full(consts["bias_s"]),
        full(consts["conv_w"]), full(consts["gn_attn"]), full(consts["gn_conv"]), full(consts["w_out"]),
        full(consts["w_r"]), full(consts["b_r"]),
        pl.BlockSpec((ts, ts), lambda i, s: (0, 0)),
        pl.BlockSpec((seqs * WINDOW, KV_DIM), lambda i, s: (i, 0)),
        pl.BlockSpec((seqs * WINDOW, KV_DIM), lambda i, s: (i, 0)),
        rows(CONV_DIM), rows(CONV_DIM), full(cnt_in),
    ]
    args = [x, mod, consts["anorm"], consts["fnorm"], consts["w_in"], consts["cos_s"], consts["sin_s"],
            consts["bias_s"], consts["conv_w"], consts["gn_attn"], consts["gn_conv"], consts["w_out"],
            consts["w_r"], consts["b_r"], consts["tri"], ck, cv, e1, e2, cnt_in]
    nt = n_s // ts
    out_shape = [
        jax.ShapeDtypeStruct((n_s, D_MODEL), jnp.float32),
        jax.ShapeDtypeStruct((n_s, D_MODEL // 2), jnp.uint32),
        jax.ShapeDtypeStruct((nt, 8, ts), jnp.int32),
        jax.ShapeDtypeStruct((nt, 8, ts), jnp.float32),
        jax.ShapeDtypeStruct((N_EXPERTS, LANES), jnp.float32),
        jax.ShapeDtypeStruct((n_s, KV_DIM), jnp.float32),
        jax.ShapeDtypeStruct((n_s, KV_DIM), jnp.float32),
        jax.ShapeDtypeStruct((n_s, CONV_DIM), jnp.float32),
    ]
    out_specs = [
        rows(D_MODEL),
        rows(D_MODEL // 2),
        pl.BlockSpec((None, 8, ts), lambda i, s: (i, 0, 0)),
        pl.BlockSpec((None, 8, ts), lambda i, s: (i, 0, 0)),
        pl.BlockSpec((N_EXPERTS, LANES), lambda i, s: (0, 0)),
        rows(KV_DIM), rows(KV_DIM), rows(CONV_DIM),
    ]
    scratch = [
        pltpu.VMEM((ts, ATTN_DIM), jnp.bfloat16),
        pltpu.VMEM((ts, KV_DIM), jnp.float32), pltpu.VMEM((ts, KV_DIM), jnp.float32),
        pltpu.VMEM((ts, ATTN_DIM), jnp.float32),
        pltpu.VMEM((N_EXPERTS, LANES), jnp.float32),
    ]
    return pl.pallas_call(
        functools.partial(_sample_mixer_kernel, dec=dec),
        grid_spec=pltpu.PrefetchScalarGridSpec(
            num_scalar_prefetch=1, grid=(nt,), in_specs=in_specs, out_specs=out_specs, scratch_shapes=scratch),
        out_shape=out_shape,
        compiler_params=pltpu.CompilerParams(dimension_semantics=("arbitrary",), vmem_limit_bytes=VMEM_LIMIT),
        name="sample_mixer",
    )(sinks, *args)


def _dispatch(idx, h2, n_rows):
    mesh = plsc.VectorSubcoreMesh(core_axis_name="core", subcore_axis_name="subcore")
    rows, window = SC_ROWS, SC_INDEX_WINDOW
    sub = window // rows
    n_tok, width = h2.shape
    assert n_tok % rows == 0 and (2 * n_tok) % window == 0

    moved = 3 * n_tok * width * h2.dtype.itemsize
    @pl.kernel(out_type=jax.ShapeDtypeStruct((n_rows, width), h2.dtype), mesh=mesh, scratch_types=[],
               cost_estimate=pl.CostEstimate(flops=0, transcendentals=0, bytes_accessed=moved))
    def scatter(h_hbm, i_hbm, xs_hbm):
        def body(indices, h_vmem, i_vmem):
            pltpu.sync_copy(h_vmem, xs_hbm.at[i_vmem.at[0, pl.ds(indices[1] * rows, rows)]])

        pltpu.emit_pipeline(
            body, grid=(2 * n_tok // window, sub),
            in_specs=[pl.BlockSpec((rows, width), lambda i, j: ((i * sub + j) % (n_tok // rows), 0)),
                      pl.BlockSpec((1, window), lambda i, j: (0, i))],
            out_specs=[], core_axis_name=("core", "subcore"),
            dimension_semantics=(pltpu.PARALLEL, pltpu.ARBITRARY), _explicit_indices=True,
        )(h_hbm, i_hbm)

    return scatter(h2, idx)


def _scatter_rows_kernel(dest_ref, h_ref, xs_in_ref, xs_ref, sem):
    del xs_in_ref
    n_tok = h_ref.shape[0]

    def issue(i, carry):
        for k in range(2):
            d = dest_ref[k * n_tok + i]
            pltpu.make_async_copy(h_ref.at[pl.ds(i, 1)], xs_ref.at[pl.ds(d, 1)], sem.at[0]).start(priority=k)
        return carry

    lax.fori_loop(0, n_tok, issue, 0, unroll=8)
    for _ in range(2):
        pltpu.make_async_copy(h_ref, xs_ref.at[pl.ds(0, n_tok)], sem.at[0]).wait()


def _scatter_rows(dest, h2, xs):
    return pl.pallas_call(
        _scatter_rows_kernel,
        grid_spec=pltpu.PrefetchScalarGridSpec(
            num_scalar_prefetch=1, grid=(1,),
            in_specs=[pl.BlockSpec(h2.shape, lambda i, d: (0, 0)), pl.BlockSpec(memory_space=pl.ANY)],
            out_specs=pl.BlockSpec(memory_space=pl.ANY),
            scratch_shapes=[pltpu.SemaphoreType.DMA((1,))]),
        out_shape=jax.ShapeDtypeStruct(xs.shape, xs.dtype),
        input_output_aliases={2: 0},
        compiler_params=pltpu.CompilerParams(dimension_semantics=("arbitrary",)),
        name="scatter_rows",
    )(dest, h2, xs)


def _experts_kernel(be_ref, nu_ref, nv_ref, nxt_ref, par_ref, phys_ref, xs_ref, wg_ref, wu_ref, wd_ref, y_ref,
                    wg_s, wu_s, wd_s, wg_f, wu_f, wd_f, sem):
    del phys_ref
    j = pl.program_id(0)
    used = j < nu_ref[0]
    changed = jnp.logical_or(j == 0, be_ref[j] != be_ref[jnp.maximum(j - 1, 0)])

    def weight_copies(e, slot):
        return [pltpu.make_async_copy(src.at[e], dst.at[slot], sem.at[slot])
                for src, dst in ((wg_ref, wg_f), (wu_ref, wu_f), (wd_ref, wd_f))]

    @pl.when(jnp.logical_and(used, changed))
    def _():
        slot = par_ref[j]

        @pl.when(j == 0)
        def _():
            for c in weight_copies(be_ref[0], slot):
                c.start()

        for c in weight_copies(be_ref[j], slot):
            c.wait()

        @pl.when(nxt_ref[j] >= 0)
        def _():
            for c in weight_copies(nxt_ref[j], 1 - slot):
                c.start()

        wg_s[...] = wg_f[slot].astype(jnp.bfloat16)
        wu_s[...] = wu_f[slot].astype(jnp.bfloat16)
        wd_s[...] = wd_f[slot].astype(jnp.bfloat16)

    @pl.when(used)
    def _():
        live = lax.broadcasted_iota(jnp.int32, (xs_ref.shape[0], 1), 0) < nv_ref[j]
        x = _unpack_bf16_pairs(jnp.where(live, xs_ref[...], jnp.uint32(0)))
        g = jnp.dot(x, wg_s[...], preferred_element_type=jnp.float32)
        u = jnp.dot(x, wu_s[...], preferred_element_type=jnp.float32)
        a = (g * jax.nn.sigmoid(g) * u).astype(jnp.bfloat16)
        y = jnp.dot(a, wd_s[...], preferred_element_type=jnp.float32)
        y_ref[...] = _pack_bf16_pairs(y.astype(jnp.bfloat16))


def _experts(block_e, n_used, n_valid, next_e, parity, phys, xs, w_gate, w_up, w_down):
    bm = EXPERT_BLOCK
    n_blocks = xs.shape[0] // bm
    hbm = pl.BlockSpec(memory_space=pl.ANY)
    return pl.pallas_call(
        _experts_kernel,
        grid_spec=pltpu.PrefetchScalarGridSpec(
            num_scalar_prefetch=6, grid=(n_blocks,),
            in_specs=[pl.BlockSpec((bm, D_MODEL // 2), lambda j, be, nu, nv, nx, pa, ph: (ph[j], 0)),
                      hbm, hbm, hbm],
            out_specs=pl.BlockSpec((bm, D_MODEL // 2), lambda j, be, nu, nv, nx, pa, ph: (ph[j], 0)),
            scratch_shapes=[pltpu.VMEM((D_MODEL, EXPERT_HIDDEN), jnp.bfloat16),
                            pltpu.VMEM((D_MODEL, EXPERT_HIDDEN), jnp.bfloat16),
                            pltpu.VMEM((EXPERT_HIDDEN, D_MODEL), jnp.bfloat16),
                            pltpu.VMEM((2, D_MODEL, EXPERT_HIDDEN), jnp.float32),
                            pltpu.VMEM((2, D_MODEL, EXPERT_HIDDEN), jnp.float32),
                            pltpu.VMEM((2, EXPERT_HIDDEN, D_MODEL), jnp.float32),
                            pltpu.SemaphoreType.DMA((2,))]),
        out_shape=jax.ShapeDtypeStruct((xs.shape[0], D_MODEL // 2), jnp.uint32),
        compiler_params=pltpu.CompilerParams(dimension_semantics=("arbitrary",), vmem_limit_bytes=VMEM_LIMIT),
        name="experts",
    )(block_e, n_used, n_valid, next_e, parity, phys, xs, w_gate, w_up, w_down)


def _gather_rows(idx, yb):
    mesh = plsc.VectorSubcoreMesh(core_axis_name="core", subcore_axis_name="subcore")
    rows, window = SC_ROWS, SC_INDEX_WINDOW
    sub = window // rows
    n = idx.shape[1]
    width = yb.shape[1]
    assert n % window == 0

    @pl.kernel(out_type=jax.ShapeDtypeStruct((n, width), yb.dtype), mesh=mesh, scratch_types=[])
    def gather(y_hbm, i_hbm, o_hbm):
        def body(indices, i_vmem, o_vmem):
            pltpu.sync_copy(y_hbm.at[i_vmem.at[0, pl.ds(indices[1] * rows, rows)]], o_vmem)

        pltpu.emit_pipeline(
            body, grid=(n // window, sub),
            in_specs=[pl.BlockSpec((1, window), lambda i, j: (0, i))],
            out_specs=[pl.BlockSpec((rows, width), lambda i, j: (i * sub + j, 0))],
            core_axis_name=("core", "subcore"),
            dimension_semantics=(pltpu.PARALLEL, pltpu.ARBITRARY), _explicit_indices=True,
        )(i_hbm, o_hbm)

    return gather(yb, idx)


def _combine_kernel(x1_ref, gate_ref, rf_ref, fin_ref, y0_ref, y1_ref, *out_refs, per_token_gate):
    o_ref = out_refs[-1]
    gates = jnp.transpose(rf_ref[...])
    moe = _unpack_pairs_f32(y0_ref[...]) * gates[:, 0:1] + _unpack_pairs_f32(y1_ref[...]) * gates[:, 1:2]
    gate_f = gate_ref[...] if per_token_gate else gate_ref[...][5:6]
    o_ref[...] = _rms(x1_ref[...] + gate_f * moe, fin_ref[...])


def _combine(x1, gate, rf, final_norm, ys, out_prev, first_tile, n_tiles, *, per_token_gate):
    tc = COMBINE_TILE
    gate_spec = (pl.BlockSpec((tc, D_MODEL), lambda i: (first_tile + i, 0)) if per_token_gate
                 else pl.BlockSpec((6, D_MODEL), lambda i: (0, 0)))
    args = [x1, gate, rf, final_norm, ys, ys]
    in_specs = [pl.BlockSpec((tc, D_MODEL), lambda i: (first_tile + i, 0)),
                gate_spec,
                pl.BlockSpec((8, tc), lambda i: (0, first_tile + i)),
                pl.BlockSpec((1, D_MODEL), lambda i: (0, 0)),
                pl.BlockSpec((tc, D_MODEL // 2), lambda i: (i, 0)),
                pl.BlockSpec((tc, D_MODEL // 2), lambda i: (n_tiles + i, 0))]
    aliases = {}
    if out_prev is not None:
        args.append(out_prev)
        in_specs.append(pl.BlockSpec(memory_space=pl.ANY))
        aliases = {len(args) - 1: 0}
    return pl.pallas_call(
        functools.partial(_combine_kernel, per_token_gate=per_token_gate),
        grid=(n_tiles,), in_specs=in_specs,
        out_specs=pl.BlockSpec((tc, D_MODEL), lambda i: (first_tile + i, 0)),
        out_shape=jax.ShapeDtypeStruct(x1.shape, jnp.float32),
        input_output_aliases=aliases,
        compiler_params=pltpu.CompilerParams(dimension_semantics=("arbitrary",), vmem_limit_bytes=VMEM_LIMIT),
        name="combine",
    )(*args)


def _rope_tables(pos):
    half = HEAD_DIM // 2
    inv_freq = np.power(np.float32(ROPE_THETA), -(np.arange(half, dtype=np.float32) / np.float32(half)))
    ang = (pos.astype(np.float32)[:, None] * inv_freq[None, :].astype(np.float32)).astype(np.float64)
    cos, sin = np.cos(ang).astype(np.float32), np.sin(ang).astype(np.float32)
    cos = np.concatenate([cos, cos, cos, cos], axis=1)
    sin = np.concatenate([-sin, sin, -sin, sin], axis=1)
    return cos, sin


def _prompt_bias():
    r = np.arange(WINDOW)[:, None]
    j = np.arange(2 * WINDOW)[None, :]
    band = (j > r) & (j <= r + WINDOW)
    later = np.where(band, 0.0, NEG).astype(np.float32)
    first = np.where(band & (j >= WINDOW), 0.0, NEG).astype(np.float32)
    return jnp.asarray(np.stack([first, later]))


def _sample_bias(dec):
    cb = SAMPLE_CHUNK
    qb = np.repeat(np.arange(cb), dec)[:, None]
    qi = np.tile(np.arange(dec), cb)[:, None]
    kb = np.concatenate([np.repeat(np.arange(cb), WINDOW), np.repeat(np.arange(cb), dec)])[None, :]
    kj = np.concatenate([np.tile(np.arange(WINDOW), cb), np.tile(np.arange(dec), cb)])[None, :]
    is_new = (np.arange(cb * WINDOW + cb * dec) >= cb * WINDOW)[None, :]
    ok = (kb == qb) & np.where(is_new, kj <= qi, kj > qi)
    ok = np.pad(ok, ((0, 0), (0, SAMPLE_KEYS - ok.shape[1])))
    return jnp.asarray(np.where(ok, 0.0, NEG).astype(np.float32))


def kernel(x_prompt, x_sample, cache_k, cache_v, state_conv, c_prompt, c_sample, attn_norm, ffn_norm, w_mod, b_mod, w_in, conv_w, attn_sinks, out_norm_attn, out_norm_conv, w_out, w_group, b_group, w_expert, b_expert, w_gate, w_up, w_down, final_norm):
    assert attn_norm.shape[0] == 1, "one layer"
    bsz, seq, _ = x_prompt.shape
    nseq, dec, _ = x_sample.shape
    m_prompt = bsz * seq
    n_s = nseq * dec
    m_total = m_prompt + n_s
    tm = MIXER_TILE
    assert seq % tm == 0 and n_s % SAMPLE_TILE == 0 and SAMPLE_TILE % (SAMPLE_CHUNK * dec) == 0
    assert cache_k.shape[2] == WINDOW and m_prompt % SAMPLE_TILE == 0 and m_total % COMBINE_TILE == 0
    bf = jnp.bfloat16

    w_r = jnp.zeros((ROUTER_ROWS, D_MODEL), jnp.float32)
    w_r = w_r.at[0:N_GROUPS].set(w_group[0].T).at[8:8 + N_EXPERTS].set(w_expert[0].T).astype(bf)
    b_r = jnp.zeros((ROUTER_ROWS,), jnp.float32).at[0:N_GROUPS].set(b_group[0]).at[8:8 + N_EXPERTS].set(b_expert[0])
    b_r = jnp.broadcast_to(b_r[:, None], (ROUTER_ROWS, LANES))
    cos_p, sin_p = _rope_tables(np.arange(seq))
    cos_s, sin_s = _rope_tables(PAST_LEN + np.arange(dec))
    tri = jnp.asarray(np.triu(np.ones((tm, tm), np.float32), 1)).astype(bf)
    consts = dict(
        anorm=attn_norm[0][None], fnorm=ffn_norm[0][None], w_in=w_in[0].astype(bf),
        cos_p=jnp.asarray(cos_p), sin_p=jnp.asarray(sin_p),
        cos_s=jnp.asarray(np.tile(cos_s, (nseq, 1))), sin_s=jnp.asarray(np.tile(sin_s, (nseq, 1))),
        bias_p=_prompt_bias(), bias_s=_sample_bias(dec), conv_w=conv_w[0],
        gn_attn=out_norm_attn[0][None], gn_conv=out_norm_conv[0][None], w_out=w_out[0].astype(bf),
        w_r=w_r, b_r=b_r, tri=tri)
    sinks = attn_sinks[0]

    mod = _adaln(jnp.concatenate([c_prompt, c_sample], axis=0), w_mod[0], b_mod[0])
    mod_p = mod[:bsz].reshape(bsz, 6, D_MODEL)
    mod_s = mod[bsz:]

    bm = EXPERT_BLOCK
    eids = jnp.arange(N_EXPERTS, dtype=jnp.int32)
    lookup = lambda table, e: jnp.sum(jnp.where(e[..., None] == eids, table, 0), axis=-1)
    by_token = lambda a: jnp.transpose(a, (1, 0, 2)).reshape(8, -1)
    ceil_bm = lambda c: (c + bm - 1) // bm * bm
    x1_p, h2_p, ri_p, rf_p, cnt_p, k_p, v_p, u_p = _prompt_mixer(x_prompt, mod_p, sinks, consts)
    cnt_p = cnt_p[:, 0].astype(jnp.int32)
    padded_p = ceil_bm(cnt_p)
    pstart_p = jnp.cumsum(padded_p) - padded_p
    nb_p = -(-(2 * m_prompt + N_EXPERTS * (bm - 1)) // bm)
    nb_o = -(-(2 * n_s + N_EXPERTS * (bm - 1)) // bm)
    n_blocks = nb_p + nb_o
    ri_p = by_token(ri_p)
    dest_p = (lookup(pstart_p, ri_p[0:2]) + ri_p[2:4]).reshape(2 * m_prompt)

    st = state_conv[0]
    zrow = jnp.zeros((nseq, 1, CONV_DIM), jnp.float32)
    e1 = jnp.concatenate([st[:, 1:2]] + [zrow] * (dec - 1), axis=1).reshape(n_s, CONV_DIM)
    e2 = jnp.concatenate([st[:, 0:1], st[:, 1:2]] + [zrow] * (dec - 2), axis=1).reshape(n_s, CONV_DIM)
    dest_p, e1 = lax.optimization_barrier((dest_p, e1))
    xs = _dispatch(dest_p[None], h2_p, n_blocks * bm)
    ck = cache_k[0].reshape(nseq * WINDOW, KV_DIM)
    cv = cache_v[0].reshape(nseq * WINDOW, KV_DIM)
    x1_s, h2_s, ri_s, rf_s, cnt_s, k_s, v_s, u_s = _sample_mixer(
        x_sample.reshape(n_s, D_MODEL), mod_s, sinks, consts, ck, cv, e1, e2,
        jnp.zeros((N_EXPERTS, LANES), jnp.float32), dec)

    cnt_s = cnt_s[:, 0].astype(jnp.int32)
    fit = jnp.minimum(cnt_s, padded_p - cnt_p)
    over = cnt_s - fit
    padded_o = ceil_bm(over)
    ostart = nb_p * bm + jnp.cumsum(padded_o) - padded_o
    ri_s = by_token(ri_s)
    e_s, r_s = ri_s[0:2], ri_s[2:4]
    dest_s = jnp.where(r_s < lookup(fit, e_s), lookup(pstart_p + cnt_p, e_s) + r_s, lookup(ostart - fit, e_s) + r_s)
    dest_s = dest_s.reshape(2 * n_s)

    ksn = k_s.reshape(nseq, dec, N_KV_HEADS, HEAD_DIM)
    vsn = v_s.reshape(nseq, dec, N_KV_HEADS, HEAD_DIM)
    new_k_s = jnp.concatenate([cache_k[0][:, dec:], ksn], axis=1)[None]
    new_v_s = jnp.concatenate([cache_v[0][:, dec:], vsn], axis=1)[None]
    new_k_s, new_v_s, dest_s = lax.optimization_barrier((new_k_s, new_v_s, dest_s))
    xs = _scatter_rows(dest_s, h2_s, xs)

    np_e, no_e = padded_p // bm, padded_o // bm
    lend = jnp.cumsum(np_e + no_e)
    lstart = lend - np_e - no_e
    n_used = lend[-1].astype(jnp.int32)
    blk = jnp.arange(n_blocks, dtype=jnp.int32)
    present = np_e + no_e > 0
    last_e = jnp.max(jnp.where(present, eids, 0))
    block_e = jnp.sum((lend[None, :] <= blk[:, None]).astype(jnp.int32), axis=1)
    block_e = jnp.where(blk < n_used, jnp.minimum(block_e, N_EXPERTS - 1), last_e)
    of_block = lambda table: lookup(table, block_e)
    within = blk - of_block(lstart)
    in_prompt = within < of_block(np_e)
    phys = jnp.where(in_prompt, of_block(pstart_p) // bm + within, of_block(ostart) // bm + within - of_block(np_e))
    n_valid = jnp.where(in_prompt, of_block(cnt_p + fit) - within * bm, of_block(over) - (within - of_block(np_e)) * bm)
    n_valid = jnp.where(blk < n_used, jnp.clip(n_valid, 0, bm), 0).astype(jnp.int32)
    phys = jnp.where(blk < n_used, phys, jnp.sum(jnp.where(blk == n_used - 1, phys, 0))).astype(jnp.int32)
    later = jnp.logical_and(present[None, :], eids[None, :] > eids[:, None])
    next_of = jnp.min(jnp.where(later, eids[None, :], N_EXPERTS), axis=1)
    next_of = jnp.where(next_of == N_EXPERTS, -1, next_of)
    next_e = of_block(next_of).astype(jnp.int32)
    parity = of_block((jnp.cumsum(present.astype(jnp.int32)) - 1) % 2).astype(jnp.int32)
    yb = _experts(block_e, n_used.reshape(1), n_valid, next_e, parity, phys, xs, w_gate[0], w_up[0], w_down[0])

    fin = final_norm[None]
    tc = COMBINE_TILE
    rf_p, rf_s = by_token(rf_p), by_token(rf_s)
    y_p = None
    chunk = min(COMBINE_CHUNK, seq)
    assert seq % chunk == 0 and chunk % tc == 0
    for c in range(m_prompt // chunk):
        lo, hi = c * chunk, (c + 1) * chunk
        idx = jnp.concatenate([dest_p[lo:hi], dest_p[m_prompt + lo:m_prompt + hi]])[None]
        y_p = _combine(x1_p, mod_p[lo // seq], rf_p, fin, _gather_rows(idx, yb), y_p, lo // tc, chunk // tc,
                       per_token_gate=False)
    gate_s = jnp.repeat(mod_s[:, 5 * D_MODEL:], dec, axis=0)
    y_s = _combine(x1_s, gate_s, rf_s, fin, _gather_rows(dest_s[None], yb), None, 0, n_s // tc, per_token_gate=True)

    keep = WINDOW
    new_k_p = k_p.reshape(1, bsz, keep, N_KV_HEADS, HEAD_DIM)
    new_v_p = v_p.reshape(1, bsz, keep, N_KV_HEADS, HEAD_DIM)
    new_u_p = u_p[:, 6:8][None]
    u_all = jnp.concatenate([st, u_s.reshape(nseq, dec, CONV_DIM)], axis=1)
    new_u_s = u_all[:, -2:][None]
    return (y_p.reshape(bsz, seq, D_MODEL), y_s.reshape(nseq, dec, D_MODEL),
            new_k_p, new_v_p, new_u_p, new_k_s, new_v_s, new_u_s)
```

```python
import functools

import jax
import jax.numpy as jnp
import numpy as np
from jax import lax
from jax.experimental import pallas as pl
from jax.experimental.pallas import tpu as pltpu
from jax.experimental.pallas import tpu_sc as plsc

D_MODEL = 1024
HEAD_DIM = 64
ATTN_DIM = 512
N_Q_HEADS = 8
N_KV_HEADS = 2
KV_DIM = 128
CONV_DIM = 512
WINDOW = 128
IN_PROJ_DIM = ATTN_DIM + 2 * KV_DIM + 3 * CONV_DIM
N_GROUPS = 4
EXPERTS_PER_GROUP = 8
N_EXPERTS = 32
EXPERT_HIDDEN = 512
RMS_EPS = 1e-5
ROPE_THETA = 10000.0
PAST_LEN = 16384

LANES = 128
ROUTER_ROWS = 48
NEG = -1e30

MIXER_TILE = 512
SAMPLE_TILE = 128
SAMPLE_CHUNK = 8
SAMPLE_KEYS = 1152
EXPERT_BLOCK = 512
COMBINE_TILE = 512
COMBINE_CHUNK = 4096
SC_ROWS = 64
SC_INDEX_WINDOW = 128
VMEM_LIMIT = 56 * 1024 * 1024


def _rms(x, gain):
    ms = jnp.mean(x * x, axis=-1, keepdims=True)
    return x * lax.rsqrt(ms + RMS_EPS) * gain


def _half_lane_mask(rows=1):
    return lax.broadcasted_iota(jnp.int32, (rows, LANES), 1) < HEAD_DIM


def _rope(x, cos, sin_signed):
    t, w = x.shape
    reps = w // LANES
    lane = lax.broadcasted_iota(jnp.int32, (t, w), 1)
    upper = (lane % HEAD_DIM) >= (HEAD_DIM // 2)
    partner = jnp.where(upper, pltpu.roll(x, HEAD_DIM // 2, 1), pltpu.roll(x, w - HEAD_DIM // 2, 1))
    if reps > 1:
        cos = jnp.concatenate([cos] * reps, axis=1)
        sin_signed = jnp.concatenate([sin_signed] * reps, axis=1)
    return x * cos + partner * sin_signed


def _swap_halves(x):
    return pltpu.roll(x, HEAD_DIM, 1)


def _group_norm64(x, gain):
    t, w = x.shape
    lo = _half_lane_mask(t)
    outs = []
    for c in range(w // LANES):
        xc = x[:, c * LANES:(c + 1) * LANES]
        sq = xc * xc
        s_lo = jnp.sum(jnp.where(lo, sq, 0.0), axis=-1, keepdims=True)
        s_hi = jnp.sum(jnp.where(lo, 0.0, sq), axis=-1, keepdims=True)
        r = jnp.where(lo, lax.rsqrt(s_lo * (1.0 / HEAD_DIM) + RMS_EPS), lax.rsqrt(s_hi * (1.0 / HEAD_DIM) + RMS_EPS))
        outs.append(xc * r)
    return jnp.concatenate(outs, axis=1) * gain


def _attend(q_rows, keys, keys_sw, vals, vals_sw, bias, sinks_ref, group):
    r = q_rows.shape[0]
    lo = _half_lane_mask(r)
    outs = []
    for parity in range(2):
        heads = [4 * group + parity, 4 * group + parity + 2]
        kk, vv = (keys, vals) if parity == group else (keys_sw, vals_sw)
        half = lo if parity == 0 else jnp.logical_not(lo)
        qs = []
        for h in heads:
            pair = q_rows[:, (h // 2) * LANES:(h // 2 + 1) * LANES]
            qs.append(jnp.where(half, pair, jnp.zeros_like(pair)))
        qcat = jnp.concatenate(qs, axis=0)
        s = lax.dot_general(qcat, kk, (((1,), (1,)), ((), ())), preferred_element_type=jnp.float32)
        ps, dens = [], []
        for i, h in enumerate(heads):
            sh = s[i * r:(i + 1) * r] + bias
            sink = sinks_ref[h]
            m = jnp.maximum(jnp.max(sh, axis=-1, keepdims=True), sink)
            p = jnp.exp(sh - m)
            dens.append(jnp.sum(p, axis=-1, keepdims=True) + jnp.exp(sink - m))
            ps.append(p.astype(jnp.bfloat16))
        o = jnp.dot(jnp.concatenate(ps, axis=0), vv, preferred_element_type=jnp.float32)
        for i, h in enumerate(heads):
            oh = o[i * r:(i + 1) * r] / dens[i]
            ms = jnp.sum(jnp.where(half, oh * oh, 0.0), axis=-1, keepdims=True) * (1.0 / HEAD_DIM)
            outs.append((h, oh * lax.rsqrt(ms + RMS_EPS)))
    outs.sort(key=lambda t: t[0])
    return [o for _, o in outs]


def _merge_heads(head_outs, lo):
    pairs = [jnp.where(lo, head_outs[2 * j], head_outs[2 * j + 1]) for j in range(4)]
    return jnp.concatenate(pairs, axis=1)


def _route(h2b, wr_ref, br_ref, tri_ref, cnt_ref, ri_ref, rf_ref):
    t = h2b.shape[0]
    lg = lax.dot_general(wr_ref[...], h2b, (((1,), (1,)), ((), ())), preferred_element_type=jnp.float32)
    lg = lg + br_ref[...][:, 0:1]
    r8 = lax.broadcasted_iota(jnp.int32, (8, t), 0)
    lgrp = jnp.where(r8 < N_GROUPS, lg[0:8], -jnp.inf)
    gmax = jnp.max(lgrp, axis=0, keepdims=True)
    grp = jnp.min(jnp.where(lgrp == gmax, r8, 8), axis=0, keepdims=True)
    pg_sel = 1.0 / jnp.sum(jnp.exp(lgrp - gmax), axis=0, keepdims=True)
    r32 = lax.broadcasted_iota(jnp.int32, (N_EXPERTS, t), 0)
    le = jnp.where((r32 // EXPERTS_PER_GROUP) == grp, lg[8:8 + N_EXPERTS], -jnp.inf)
    v1 = jnp.max(le, axis=0, keepdims=True)
    i1 = jnp.min(jnp.where(le == v1, r32, N_EXPERTS), axis=0, keepdims=True)
    le2 = jnp.where(r32 == i1, -jnp.inf, le)
    v2 = jnp.max(le2, axis=0, keepdims=True)
    i2 = jnp.min(jnp.where(le2 == v2, r32, N_EXPERTS), axis=0, keepdims=True)
    e21 = jnp.exp(v2 - v1)
    gate1 = pg_sel / (1.0 + e21)
    gate2 = pg_sel * e21 / (1.0 + e21)
    hot1 = r32 == i1
    hot2 = r32 == i2
    onehot = jnp.where(jnp.logical_or(hot1, hot2), 1.0, 0.0)
    before = jnp.dot(onehot.astype(jnp.bfloat16), tri_ref[...], preferred_element_type=jnp.float32)
    pos = before + cnt_ref[...][:, 0:1]
    rank1 = jnp.sum(jnp.where(hot1, pos, 0.0), axis=0, keepdims=True)
    rank2 = jnp.sum(jnp.where(hot2, pos, 0.0), axis=0, keepdims=True)
    cnt_ref[...] = cnt_ref[...] + jnp.sum(onehot, axis=1, keepdims=True)
    zi = jnp.zeros((4, t), jnp.int32)
    ri_ref[...] = jnp.concatenate([i1, i2, rank1.astype(jnp.int32), rank2.astype(jnp.int32), zi], axis=0)
    rf_ref[...] = jnp.concatenate([gate1, gate2, jnp.zeros((6, t), jnp.float32)], axis=0)


def _pack_bf16_pairs(xb):
    w = xb.shape[1] // 2
    bits = pltpu.bitcast(xb.astype(jnp.float32), jnp.uint32)
    return (bits[:, :w] >> 16) | (bits[:, w:] & jnp.uint32(0xFFFF0000))


def _unpack_pairs_f32(words):
    lo = pltpu.bitcast(words << 16, jnp.float32)
    hi = pltpu.bitcast(words & jnp.uint32(0xFFFF0000), jnp.float32)
    return jnp.concatenate([lo, hi], axis=1)


def _unpack_bf16_pairs(words):
    return _unpack_pairs_f32(words).astype(jnp.bfloat16)


def _split_proj(proj):
    a = ATTN_DIM
    q = proj[:, :a]
    k = proj[:, a:a + KV_DIM]
    v = proj[:, a + KV_DIM:a + 2 * KV_DIM]
    c0 = a + 2 * KV_DIM
    h_conv = proj[:, c0:c0 + CONV_DIM]
    gate_b = proj[:, c0 + CONV_DIM:c0 + 2 * CONV_DIM]
    gate_c = proj[:, c0 + 2 * CONV_DIM:c0 + 3 * CONV_DIM]
    return q, k, v, gate_c * h_conv, gate_b


def _mixer_tail(x, attn_n, conv_n, mod, wout_ref, fnorm_ref, wr_ref, br_ref, tri_ref, cnt_ref,
                x1_ref, h2_ref, ri_ref, rf_ref):
    gate_a, shift_f, scale_f = mod
    cat = jnp.concatenate([attn_n.astype(jnp.bfloat16), conv_n.astype(jnp.bfloat16)], axis=1)
    mix = jnp.dot(cat, wout_ref[...], preferred_element_type=jnp.float32)
    x1 = x + gate_a * mix
    x1_ref[...] = x1
    h2 = _rms(x1, fnorm_ref[...] * (1.0 + scale_f)) + shift_f
    h2b = h2.astype(jnp.bfloat16)
    h2_ref[...] = _pack_bf16_pairs(h2b)
    _route(h2b, wr_ref, br_ref, tri_ref, cnt_ref, ri_ref, rf_ref)


def _adaln_kernel(c_ref, w_ref, b_ref, o_ref):
    c = c_ref[...]
    a = (c * jax.nn.sigmoid(c)).astype(jnp.bfloat16)
    o_ref[...] = jnp.dot(a, w_ref[...].astype(jnp.bfloat16), preferred_element_type=jnp.float32) + b_ref[...]


def _adaln(c, w_mod, b_mod):
    n = c.shape[0]
    tn = 1536
    return pl.pallas_call(
        _adaln_kernel,
        grid=(w_mod.shape[1] // tn,),
        in_specs=[pl.BlockSpec((n, D_MODEL), lambda j: (0, 0)),
                  pl.BlockSpec((D_MODEL, tn), lambda j: (0, j)),
                  pl.BlockSpec((1, tn), lambda j: (0, j))],
        out_specs=pl.BlockSpec((n, tn), lambda j: (0, j)),
        out_shape=jax.ShapeDtypeStruct((n, w_mod.shape[1]), jnp.float32),
        compiler_params=pltpu.CompilerParams(dimension_semantics=("arbitrary",), vmem_limit_bytes=VMEM_LIMIT),
        name="adaln",
    )(c, w_mod, b_mod.reshape(1, -1))


def _prompt_mixer_kernel(sinks_ref, x_ref, mod_ref, anorm_ref, fnorm_ref, win_ref, cos_ref, sin_ref, bias_ref,
                         convw_ref, gna_ref, gnc_ref, wout_ref, wr_ref, br_ref, tri_ref,
                         x1_ref, h2_ref, ri_ref, rf_ref, cnt_out_ref, knew_ref, vnew_ref, unew_ref,
                         q_s, k_s, ksw_s, v_s, vsw_s, u_s, att_s, cnt_s):
    b = pl.program_id(0)
    t = pl.program_id(1)
    tm = MIXER_TILE
    nblk = tm // WINDOW

    @pl.when(jnp.logical_and(b == 0, t == 0))
    def _():
        cnt_s[...] = jnp.zeros_like(cnt_s)

    @pl.when(t == 0)
    def _():
        z = jnp.zeros((WINDOW, KV_DIM), jnp.bfloat16)
        k_s[0:WINDOW, :] = z
        ksw_s[0:WINDOW, :] = z
        v_s[0:WINDOW, :] = z
        vsw_s[0:WINDOW, :] = z
        u_s[0:8, :] = jnp.zeros((8, CONV_DIM), jnp.float32)

    x = x_ref[...]
    mod = mod_ref[...]
    h = _rms(x, anorm_ref[...] * (1.0 + mod[1:2])) + mod[0:1]
    proj = jnp.dot(h.astype(jnp.bfloat16), win_ref[...], preferred_element_type=jnp.float32)
    q, k, v, u, gate_b = _split_proj(proj)
    cos = cos_ref[...]
    sin = sin_ref[...]
    q = _rope(q, cos, sin) * (HEAD_DIM ** -0.5)
    k = _rope(k, cos, sin)
    q_s[...] = q.astype(jnp.bfloat16)
    k_s[WINDOW:, :] = k.astype(jnp.bfloat16)
    ksw_s[WINDOW:, :] = _swap_halves(k).astype(jnp.bfloat16)
    v_s[WINDOW:, :] = v.astype(jnp.bfloat16)
    vsw_s[WINDOW:, :] = _swap_halves(v).astype(jnp.bfloat16)
    knew_ref[...] = k[tm - WINDOW:, :]
    vnew_ref[...] = v[tm - WINDOW:, :]

    u_s[8:, :] = u
    unew_ref[...] = u[tm - 8:, :]
    cw = convw_ref[...]
    conv = (u_s[6:6 + tm, :] * cw[0:1] + u_s[7:7 + tm, :] * cw[1:2] + u * cw[2:3]) * gate_b
    u_s[0:8, :] = u[tm - 8:, :]
    conv_n = _group_norm64(conv, gnc_ref[...])

    lo = _half_lane_mask(WINDOW)
    first = t == 0

    def block(i, carry):
        r0 = pl.multiple_of(i * WINDOW, WINDOW)
        bias = bias_ref[jnp.where(jnp.logical_and(first, i == 0), 0, 1)]
        qb = q_s[pl.ds(r0, WINDOW), :]
        keys = k_s[pl.ds(r0, 2 * WINDOW), :]
        keys_sw = ksw_s[pl.ds(r0, 2 * WINDOW), :]
        vals = v_s[pl.ds(r0, 2 * WINDOW), :]
        vals_sw = vsw_s[pl.ds(r0, 2 * WINDOW), :]
        outs = []
        for g in range(N_KV_HEADS):
            outs += _attend(qb, keys, keys_sw, vals, vals_sw, bias, sinks_ref, g)
        att_s[pl.ds(r0, WINDOW), :] = _merge_heads(outs, lo)
        return carry

    lax.fori_loop(0, nblk, block, 0, unroll=True)
    k_s[0:WINDOW, :] = k_s[tm:tm + WINDOW, :]
    ksw_s[0:WINDOW, :] = ksw_s[tm:tm + WINDOW, :]
    v_s[0:WINDOW, :] = v_s[tm:tm + WINDOW, :]
    vsw_s[0:WINDOW, :] = vsw_s[tm:tm + WINDOW, :]

    attn_n = att_s[...] * gna_ref[...]
    _mixer_tail(x, attn_n, conv_n, (mod[2:3], mod[3:4], mod[4:5]), wout_ref, fnorm_ref, wr_ref, br_ref, tri_ref,
                cnt_s, x1_ref, h2_ref, ri_ref, rf_ref)
    cnt_out_ref[...] = cnt_s[...]


def _prompt_mixer(x, mod, sinks, consts):
    bsz, seq, _ = x.shape
    tm = MIXER_TILE
    nt = seq // tm
    full = lambda shape: pl.BlockSpec(shape, lambda b, t, s: (0,) * len(shape))
    in_specs = [
        pl.BlockSpec((None, tm, D_MODEL), lambda b, t, s: (b, t, 0)),
        pl.BlockSpec((None, 6, D_MODEL), lambda b, t, s: (b, 0, 0)),
        full((1, D_MODEL)), full((1, D_MODEL)),
        full((D_MODEL, IN_PROJ_DIM)),
        pl.BlockSpec((tm, LANES), lambda b, t, s: (t, 0)),
        pl.BlockSpec((tm, LANES), lambda b, t, s: (t, 0)),
        full((2, WINDOW, 2 * WINDOW)),
        full((3, CONV_DIM)), full((1, ATTN_DIM)), full((1, CONV_DIM)),
        full((D_MODEL, D_MODEL)),
        full((ROUTER_ROWS, D_MODEL)), full((ROUTER_ROWS, LANES)),
        full((tm, tm)),
    ]
    out_shape = [
        jax.ShapeDtypeStruct((bsz * seq, D_MODEL), jnp.float32),
        jax.ShapeDtypeStruct((bsz * seq, D_MODEL // 2), jnp.uint32),
        jax.ShapeDtypeStruct((bsz * nt, 8, tm), jnp.int32),
        jax.ShapeDtypeStruct((bsz * nt, 8, tm), jnp.float32),
        jax.ShapeDtypeStruct((N_EXPERTS, LANES), jnp.float32),
        jax.ShapeDtypeStruct((bsz, WINDOW, KV_DIM), jnp.float32),
        jax.ShapeDtypeStruct((bsz, WINDOW, KV_DIM), jnp.float32),
        jax.ShapeDtypeStruct((bsz, 8, CONV_DIM), jnp.float32),
    ]
    out_specs = [
        pl.BlockSpec((tm, D_MODEL), lambda b, t, s: (b * nt + t, 0)),
        pl.BlockSpec((tm, D_MODEL // 2), lambda b, t, s: (b * nt + t, 0)),
        pl.BlockSpec((None, 8, tm), lambda b, t, s: (b * nt + t, 0, 0)),
        pl.BlockSpec((None, 8, tm), lambda b, t, s: (b * nt + t, 0, 0)),
        pl.BlockSpec((N_EXPERTS, LANES), lambda b, t, s: (0, 0)),
        pl.BlockSpec((None, WINDOW, KV_DIM), lambda b, t, s: (b, 0, 0)),
        pl.BlockSpec((None, WINDOW, KV_DIM), lambda b, t, s: (b, 0, 0)),
        pl.BlockSpec((None, 8, CONV_DIM), lambda b, t, s: (b, 0, 0)),
    ]
    scratch = [
        pltpu.VMEM((tm, ATTN_DIM), jnp.bfloat16),
        pltpu.VMEM((tm + WINDOW, KV_DIM), jnp.bfloat16), pltpu.VMEM((tm + WINDOW, KV_DIM), jnp.bfloat16),
        pltpu.VMEM((tm + WINDOW, KV_DIM), jnp.bfloat16), pltpu.VMEM((tm + WINDOW, KV_DIM), jnp.bfloat16),
        pltpu.VMEM((tm + 8, CONV_DIM), jnp.float32),
        pltpu.VMEM((tm, ATTN_DIM), jnp.float32),
        pltpu.VMEM((N_EXPERTS, LANES), jnp.float32),
    ]
    return pl.pallas_call(
        _prompt_mixer_kernel,
        grid_spec=pltpu.PrefetchScalarGridSpec(
            num_scalar_prefetch=1, grid=(bsz, nt), in_specs=in_specs, out_specs=out_specs, scratch_shapes=scratch),
        out_shape=out_shape,
        compiler_params=pltpu.CompilerParams(dimension_semantics=("arbitrary", "arbitrary"),
                                             vmem_limit_bytes=VMEM_LIMIT),
        name="prompt_mixer",
    )(sinks, x, mod, consts["anorm"], consts["fnorm"], consts["w_in"], consts["cos_p"], consts["sin_p"],
      consts["bias_p"], consts["conv_w"], consts["gn_attn"], consts["gn_conv"], consts["w_out"],
      consts["w_r"], consts["b_r"], consts["tri"])


def _sample_mixer_kernel(sinks_ref, x_ref, mod_ref, anorm_ref, fnorm_ref, win_ref, cos_ref, sin_ref, bias_ref,
                         convw_ref, gna_ref, gnc_ref, wout_ref, wr_ref, br_ref, tri_ref,
                         ck_ref, cv_ref, e1_ref, e2_ref, cnt_in_ref,
                         x1_ref, h2_ref, ri_ref, rf_ref, cnt_out_ref, knew_ref, vnew_ref, unew_ref,
                         q_s, k_s, v_s, att_s, cnt_s, *, dec):
    n = x_ref.shape[0]
    cb = SAMPLE_CHUNK
    rows = cb * dec
    pad_keys = SAMPLE_KEYS - cb * WINDOW - rows

    @pl.when(pl.program_id(0) == 0)
    def _():
        cnt_s[...] = cnt_in_ref[...]

    x = x_ref[...]
    mod = jnp.repeat(mod_ref[...], dec, axis=0)
    md = lambda j: mod[:, j * D_MODEL:(j + 1) * D_MODEL]
    h = _rms(x, anorm_ref[...]) * (1.0 + md(1)) + md(0)
    proj = jnp.dot(h.astype(jnp.bfloat16), win_ref[...], preferred_element_type=jnp.float32)
    q, k, v, u, gate_b = _split_proj(proj)
    cos = cos_ref[...]
    sin = sin_ref[...]
    q = _rope(q, cos, sin) * (HEAD_DIM ** -0.5)
    k = _rope(k, cos, sin)
    q_s[...] = q.astype(jnp.bfloat16)
    k_s[...] = k
    v_s[...] = v
    unew_ref[...] = u
    per_seq = WINDOW * N_KV_HEADS
    for b in range(n // dec):
        lo, hi, keep = b * per_seq, (b + 1) * per_seq, (WINDOW - dec) * N_KV_HEADS
        knew_ref[lo:lo + keep, :] = ck_ref[lo + dec * N_KV_HEADS:hi, :]
        vnew_ref[lo:lo + keep, :] = cv_ref[lo + dec * N_KV_HEADS:hi, :]
        for g in range(N_KV_HEADS):
            head = slice(g * HEAD_DIM, (g + 1) * HEAD_DIM)
            knew_ref[pl.ds(lo + keep + g, dec, stride=N_KV_HEADS), :] = k[b * dec:(b + 1) * dec, head]
            vnew_ref[pl.ds(lo + keep + g, dec, stride=N_KV_HEADS), :] = v[b * dec:(b + 1) * dec, head]

    pos = lax.broadcasted_iota(jnp.int32, (n, CONV_DIM), 0) % dec
    um1 = jnp.where(pos >= 1, pltpu.roll(u, 1, 0), 0.0) + e1_ref[...]
    um2 = jnp.where(pos >= 2, pltpu.roll(u, 2, 0), 0.0) + e2_ref[...]
    cw = convw_ref[...]
    conv = (um2 * cw[0:1] + um1 * cw[1:2] + u * cw[2:3]) * gate_b
    conv_n = _group_norm64(conv, gnc_ref[...])

    lo = _half_lane_mask(rows)
    bias = bias_ref[...]

    def cache_rows(ref, c):
        c0 = c * (cb * WINDOW * N_KV_HEADS)
        even = ref[pl.ds(c0, cb * WINDOW, stride=N_KV_HEADS), :]
        odd = ref[pl.ds(c0 + 1, cb * WINDOW, stride=N_KV_HEADS), :]
        return jnp.concatenate([even, odd], axis=1), jnp.concatenate([odd, even], axis=1)

    zpad = jnp.zeros((pad_keys, KV_DIM), jnp.float32)
    for c in range(n // rows):
        r0 = c * rows
        kcache, kcache_sw = cache_rows(ck_ref, c)
        vcache, vcache_sw = cache_rows(cv_ref, c)
        knew, vnew = k_s[r0:r0 + rows, :], v_s[r0:r0 + rows, :]
        keys = jnp.concatenate([kcache, knew, zpad], axis=0).astype(jnp.bfloat16)
        keys_sw = jnp.concatenate([kcache_sw, _swap_halves(knew), zpad], axis=0).astype(jnp.bfloat16)
        vals = jnp.concatenate([vcache, vnew, zpad], axis=0).astype(jnp.bfloat16)
        vals_sw = jnp.concatenate([vcache_sw, _swap_halves(vnew), zpad], axis=0).astype(jnp.bfloat16)
        qb = q_s[r0:r0 + rows, :]
        outs = []
        for g in range(N_KV_HEADS):
            outs += _attend(qb, keys, keys_sw, vals, vals_sw, bias, sinks_ref, g)
        att_s[r0:r0 + rows, :] = _merge_heads(outs, lo)

    attn_n = att_s[...] * gna_ref[...]
    _mixer_tail(x, attn_n, conv_n, (md(2), md(3), md(4)), wout_ref, fnorm_ref, wr_ref, br_ref, tri_ref,
                cnt_s, x1_ref, h2_ref, ri_ref, rf_ref)
    cnt_out_ref[...] = cnt_s[...]


def _sample_mixer(x, mod, sinks, consts, ck, cv, e1, e2, cnt_in, dec):
    n_s = x.shape[0]
    ts = SAMPLE_TILE
    seqs = ts // dec
    full = lambda a: pl.BlockSpec(a.shape, lambda i, s: (0,) * a.ndim)
    rows = lambda width: pl.BlockSpec((ts, width), lambda i, s: (i, 0))
    in_specs = [
        rows(D_MODEL), pl.BlockSpec((seqs, 6 * D_MODEL), lambda i, s: (i, 0)),
        full(consts["anorm"]), full(consts["fnorm"]), full(consts["w_in"]),
        rows(LANES), rows(LANES), full(consts["bias_s"]),
        full(consts["conv_w"]), full(consts["gn_attn"]), full(consts["gn_conv"]), full(consts["w_out"]),
        full(consts["w_r"]), full(consts["b_r"]),
        pl.BlockSpec((ts, ts), lambda i, s: (0, 0)),
        pl.BlockSpec((seqs * WINDOW * N_KV_HEADS, HEAD_DIM), lambda i, s: (i, 0)),
        pl.BlockSpec((seqs * WINDOW * N_KV_HEADS, HEAD_DIM), lambda i, s: (i, 0)),
        rows(CONV_DIM), rows(CONV_DIM), full(cnt_in),
    ]
    args = [x, mod, consts["anorm"], consts["fnorm"], consts["w_in"], consts["cos_s"], consts["sin_s"],
            consts["bias_s"], consts["conv_w"], consts["gn_attn"], consts["gn_conv"], consts["w_out"],
            consts["w_r"], consts["b_r"], consts["tri"], ck, cv, e1, e2, cnt_in]
    nt = n_s // ts
    out_shape = [
        jax.ShapeDtypeStruct((n_s, D_MODEL), jnp.float32),
        jax.ShapeDtypeStruct((n_s, D_MODEL // 2), jnp.uint32),
        jax.ShapeDtypeStruct((nt, 8, ts), jnp.int32),
        jax.ShapeDtypeStruct((nt, 8, ts), jnp.float32),
        jax.ShapeDtypeStruct((N_EXPERTS, LANES), jnp.float32),
        jax.ShapeDtypeStruct(ck.shape, jnp.float32),
        jax.ShapeDtypeStruct(cv.shape, jnp.float32),
        jax.ShapeDtypeStruct((n_s, CONV_DIM), jnp.float32),
    ]
    out_specs = [
        rows(D_MODEL),
        rows(D_MODEL // 2),
        pl.BlockSpec((None, 8, ts), lambda i, s: (i, 0, 0)),
        pl.BlockSpec((None, 8, ts), lambda i, s: (i, 0, 0)),
        pl.BlockSpec((N_EXPERTS, LANES), lambda i, s: (0, 0)),
        pl.BlockSpec((seqs * WINDOW * N_KV_HEADS, HEAD_DIM), lambda i, s: (i, 0)),
        pl.BlockSpec((seqs * WINDOW * N_KV_HEADS, HEAD_DIM), lambda i, s: (i, 0)),
        rows(CONV_DIM),
    ]
    scratch = [
        pltpu.VMEM((ts, ATTN_DIM), jnp.bfloat16),
        pltpu.VMEM((ts, KV_DIM), jnp.float32), pltpu.VMEM((ts, KV_DIM), jnp.float32),
        pltpu.VMEM((ts, ATTN_DIM), jnp.float32),
        pltpu.VMEM((N_EXPERTS, LANES), jnp.float32),
    ]
    return pl.pallas_call(
        functools.partial(_sample_mixer_kernel, dec=dec),
        grid_spec=pltpu.PrefetchScalarGridSpec(
            num_scalar_prefetch=1, grid=(nt,), in_specs=in_specs, out_specs=out_specs, scratch_shapes=scratch),
        out_shape=out_shape,
        compiler_params=pltpu.CompilerParams(dimension_semantics=("arbitrary",), vmem_limit_bytes=VMEM_LIMIT),
        name="sample_mixer",
    )(sinks, *args)


def _dispatch(idx, h2, n_rows):
    mesh = plsc.VectorSubcoreMesh(core_axis_name="core", subcore_axis_name="subcore")
    rows, window = SC_ROWS, SC_INDEX_WINDOW
    sub = window // rows
    n_tok, width = h2.shape
    assert n_tok % rows == 0 and (2 * n_tok) % window == 0

    moved = 3 * n_tok * width * h2.dtype.itemsize
    @pl.kernel(out_type=jax.ShapeDtypeStruct((n_rows, width), h2.dtype), mesh=mesh, scratch_types=[],
               cost_estimate=pl.CostEstimate(flops=0, transcendentals=0, bytes_accessed=moved))
    def scatter(h_hbm, i_hbm, xs_hbm):
        def body(indices, h_vmem, i_vmem):
            pltpu.sync_copy(h_vmem, xs_hbm.at[i_vmem.at[0, pl.ds(indices[1] * rows, rows)]])

        pltpu.emit_pipeline(
            body, grid=(2 * n_tok // window, sub),
            in_specs=[pl.BlockSpec((rows, width), lambda i, j: ((i * sub + j) % (n_tok // rows), 0)),
                      pl.BlockSpec((1, window), lambda i, j: (0, i))],
            out_specs=[], core_axis_name=("core", "subcore"),
            dimension_semantics=(pltpu.PARALLEL, pltpu.ARBITRARY), _explicit_indices=True,
        )(h_hbm, i_hbm)

    return scatter(h2, idx)


def _scatter_rows_kernel(dest_ref, h_ref, xs_in_ref, xs_ref, sem):
    del xs_in_ref
    n_tok = h_ref.shape[0]

    def issue(i, carry):
        for k in range(2):
            d = dest_ref[k * n_tok + i]
            pltpu.make_async_copy(h_ref.at[pl.ds(i, 1)], xs_ref.at[pl.ds(d, 1)], sem.at[0]).start(priority=k)
        return carry

    lax.fori_loop(0, n_tok, issue, 0, unroll=8)
    for _ in range(2):
        pltpu.make_async_copy(h_ref, xs_ref.at[pl.ds(0, n_tok)], sem.at[0]).wait()


def _scatter_rows(dest, h2, xs):
    return pl.pallas_call(
        _scatter_rows_kernel,
        grid_spec=pltpu.PrefetchScalarGridSpec(
            num_scalar_prefetch=1, grid=(1,),
            in_specs=[pl.BlockSpec(h2.shape, lambda i, d: (0, 0)), pl.BlockSpec(memory_space=pl.ANY)],
            out_specs=pl.BlockSpec(memory_space=pl.ANY),
            scratch_shapes=[pltpu.SemaphoreType.DMA((1,))]),
        out_shape=jax.ShapeDtypeStruct(xs.shape, xs.dtype),
        input_output_aliases={2: 0},
        compiler_params=pltpu.CompilerParams(dimension_semantics=("arbitrary",)),
        name="scatter_rows",
    )(dest, h2, xs)


def _experts_kernel(be_ref, nu_ref, nv_ref, nxt_ref, par_ref, phys_ref, xs_ref, wg_ref, wu_ref, wd_ref, y_ref,
                    wg_s, wu_s, wd_s, wg_f, wu_f, wd_f, sem):
    del phys_ref
    j = pl.program_id(0)
    used = j < nu_ref[0]
    changed = jnp.logical_or(j == 0, be_ref[j] != be_ref[jnp.maximum(j - 1, 0)])

    def weight_copies(e, slot):
        return [pltpu.make_async_copy(src.at[e], dst.at[slot], sem.at[slot])
                for src, dst in ((wg_ref, wg_f), (wu_ref, wu_f), (wd_ref, wd_f))]

    @pl.when(jnp.logical_and(used, changed))
    def _():
        slot = par_ref[j]

        @pl.when(j == 0)
        def _():
            for c in weight_copies(be_ref[0], slot):
                c.start()

        for c in weight_copies(be_ref[j], slot):
            c.wait()

        @pl.when(nxt_ref[j] >= 0)
        def _():
            for c in weight_copies(nxt_ref[j], 1 - slot):
                c.start()

        wg_s[...] = wg_f[slot].astype(jnp.bfloat16)
        wu_s[...] = wu_f[slot].astype(jnp.bfloat16)
        wd_s[...] = wd_f[slot].astype(jnp.bfloat16)

    @pl.when(used)
    def _():
        live = lax.broadcasted_iota(jnp.int32, (xs_ref.shape[0], 1), 0) < nv_ref[j]
        x = _unpack_bf16_pairs(jnp.where(live, xs_ref[...], jnp.uint32(0)))
        g = jnp.dot(x, wg_s[...], preferred_element_type=jnp.float32)
        u = jnp.dot(x, wu_s[...], preferred_element_type=jnp.float32)
        a = (g * jax.nn.sigmoid(g) * u).astype(jnp.bfloat16)
        y = jnp.dot(a, wd_s[...], preferred_element_type=jnp.float32)
        y_ref[...] = _pack_bf16_pairs(y.astype(jnp.bfloat16))


def _experts(block_e, n_used, n_valid, next_e, parity, phys, xs, w_gate, w_up, w_down):
    bm = EXPERT_BLOCK
    n_blocks = xs.shape[0] // bm
    hbm = pl.BlockSpec(memory_space=pl.ANY)
    return pl.pallas_call(
        _experts_kernel,
        grid_spec=pltpu.PrefetchScalarGridSpec(
            num_scalar_prefetch=6, grid=(n_blocks,),
            in_specs=[pl.BlockSpec((bm, D_MODEL // 2), lambda j, be, nu, nv, nx, pa, ph: (ph[j], 0)),
                      hbm, hbm, hbm],
            out_specs=pl.BlockSpec((bm, D_MODEL // 2), lambda j, be, nu, nv, nx, pa, ph: (ph[j], 0)),
            scratch_shapes=[pltpu.VMEM((D_MODEL, EXPERT_HIDDEN), jnp.bfloat16),
                            pltpu.VMEM((D_MODEL, EXPERT_HIDDEN), jnp.bfloat16),
                            pltpu.VMEM((EXPERT_HIDDEN, D_MODEL), jnp.bfloat16),
                            pltpu.VMEM((2, D_MODEL, EXPERT_HIDDEN), jnp.float32),
                            pltpu.VMEM((2, D_MODEL, EXPERT_HIDDEN), jnp.float32),
                            pltpu.VMEM((2, EXPERT_HIDDEN, D_MODEL), jnp.float32),
                            pltpu.SemaphoreType.DMA((2,))]),
        out_shape=jax.ShapeDtypeStruct((xs.shape[0], D_MODEL // 2), jnp.uint32),
        compiler_params=pltpu.CompilerParams(dimension_semantics=("arbitrary",), vmem_limit_bytes=VMEM_LIMIT),
        name="experts",
    )(block_e, n_used, n_valid, next_e, parity, phys, xs, w_gate, w_up, w_down)


def _gather_rows(idx, yb):
    mesh = plsc.VectorSubcoreMesh(core_axis_name="core", subcore_axis_name="subcore")
    rows, window = SC_ROWS, SC_INDEX_WINDOW
    sub = window // rows
    n = idx.shape[1]
    width = yb.shape[1]
    assert n % window == 0

    @pl.kernel(out_type=jax.ShapeDtypeStruct((n, width), yb.dtype), mesh=mesh, scratch_types=[])
    def gather(y_hbm, i_hbm, o_hbm):
        def body(indices, i_vmem, o_vmem):
            pltpu.sync_copy(y_hbm.at[i_vmem.at[0, pl.ds(indices[1] * rows, rows)]], o_vmem)

        pltpu.emit_pipeline(
            body, grid=(n // window, sub),
            in_specs=[pl.BlockSpec((1, window), lambda i, j: (0, i))],
            out_specs=[pl.BlockSpec((rows, width), lambda i, j: (i * sub + j, 0))],
            core_axis_name=("core", "subcore"),
            dimension_semantics=(pltpu.PARALLEL, pltpu.ARBITRARY), _explicit_indices=True,
        )(i_hbm, o_hbm)

    return gather(yb, idx)


def _combine_kernel(x1_ref, gate_ref, rf_ref, fin_ref, y0_ref, y1_ref, *out_refs, per_token_gate):
    o_ref = out_refs[-1]
    gates = jnp.transpose(rf_ref[...])
    moe = _unpack_pairs_f32(y0_ref[...]) * gates[:, 0:1] + _unpack_pairs_f32(y1_ref[...]) * gates[:, 1:2]
    gate_f = gate_ref[...] if per_token_gate else gate_ref[...][5:6]
    o_ref[...] = _rms(x1_ref[...] + gate_f * moe, fin_ref[...])


def _combine(x1, gate, rf, final_norm, ys, out_prev, first_tile, n_tiles, *, per_token_gate):
    tc = COMBINE_TILE
    gate_spec = (pl.BlockSpec((tc, D_MODEL), lambda i: (first_tile + i, 0)) if per_token_gate
                 else pl.BlockSpec((6, D_MODEL), lambda i: (0, 0)))
    args = [x1, gate, rf, final_norm, ys, ys]
    in_specs = [pl.BlockSpec((tc, D_MODEL), lambda i: (first_tile + i, 0)),
                gate_spec,
                pl.BlockSpec((8, tc), lambda i: (0, first_tile + i)),
                pl.BlockSpec((1, D_MODEL), lambda i: (0, 0)),
                pl.BlockSpec((tc, D_MODEL // 2), lambda i: (i, 0)),
                pl.BlockSpec((tc, D_MODEL // 2), lambda i: (n_tiles + i, 0))]
    aliases = {}
    if out_prev is not None:
        args.append(out_prev)
        in_specs.append(pl.BlockSpec(memory_space=pl.ANY))
        aliases = {len(args) - 1: 0}
    return pl.pallas_call(
        functools.partial(_combine_kernel, per_token_gate=per_token_gate),
        grid=(n_tiles,), in_specs=in_specs,
        out_specs=pl.BlockSpec((tc, D_MODEL), lambda i: (first_tile + i, 0)),
        out_shape=jax.ShapeDtypeStruct(x1.shape, jnp.float32),
        input_output_aliases=aliases,
        compiler_params=pltpu.CompilerParams(dimension_semantics=("arbitrary",), vmem_limit_bytes=VMEM_LIMIT),
        name="combine",
    )(*args)


def _rope_tables(pos):
    half = HEAD_DIM // 2
    inv_freq = np.power(np.float32(ROPE_THETA), -(np.arange(half, dtype=np.float32) / np.float32(half)))
    ang = (pos.astype(np.float32)[:, None] * inv_freq[None, :].astype(np.float32)).astype(np.float64)
    cos, sin = np.cos(ang).astype(np.float32), np.sin(ang).astype(np.float32)
    cos = np.concatenate([cos, cos, cos, cos], axis=1)
    sin = np.concatenate([-sin, sin, -sin, sin], axis=1)
    return cos, sin


def _prompt_bias():
    r = np.arange(WINDOW)[:, None]
    j = np.arange(2 * WINDOW)[None, :]
    band = (j > r) & (j <= r + WINDOW)
    later = np.where(band, 0.0, NEG).astype(np.float32)
    first = np.where(band & (j >= WINDOW), 0.0, NEG).astype(np.float32)
    return jnp.asarray(np.stack([first, later]))


def _sample_bias(dec):
    cb = SAMPLE_CHUNK
    qb = np.repeat(np.arange(cb), dec)[:, None]
    qi = np.tile(np.arange(dec), cb)[:, None]
    kb = np.concatenate([np.repeat(np.arange(cb), WINDOW), np.repeat(np.arange(cb), dec)])[None, :]
    kj = np.concatenate([np.tile(np.arange(WINDOW), cb), np.tile(np.arange(dec), cb)])[None, :]
    is_new = (np.arange(cb * WINDOW + cb * dec) >= cb * WINDOW)[None, :]
    ok = (kb == qb) & np.where(is_new, kj <= qi, kj > qi)
    ok = np.pad(ok, ((0, 0), (0, SAMPLE_KEYS - ok.shape[1])))
    return jnp.asarray(np.where(ok, 0.0, NEG).astype(np.float32))


def kernel(x_prompt, x_sample, cache_k, cache_v, state_conv, c_prompt, c_sample, attn_norm, ffn_norm, w_mod, b_mod, w_in, conv_w, attn_sinks, out_norm_attn, out_norm_conv, w_out, w_group, b_group, w_expert, b_expert, w_gate, w_up, w_down, final_norm):
    assert attn_norm.shape[0] == 1, "one layer"
    bsz, seq, _ = x_prompt.shape
    nseq, dec, _ = x_sample.shape
    m_prompt = bsz * seq
    n_s = nseq * dec
    m_total = m_prompt + n_s
    tm = MIXER_TILE
    assert seq % tm == 0 and n_s % SAMPLE_TILE == 0 and SAMPLE_TILE % (SAMPLE_CHUNK * dec) == 0
    assert cache_k.shape[2] == WINDOW and m_prompt % SAMPLE_TILE == 0 and m_total % COMBINE_TILE == 0
    bf = jnp.bfloat16

    w_r = jnp.zeros((ROUTER_ROWS, D_MODEL), jnp.float32)
    w_r = w_r.at[0:N_GROUPS].set(w_group[0].T).at[8:8 + N_EXPERTS].set(w_expert[0].T).astype(bf)
    b_r = jnp.zeros((ROUTER_ROWS,), jnp.float32).at[0:N_GROUPS].set(b_group[0]).at[8:8 + N_EXPERTS].set(b_expert[0])
    b_r = jnp.broadcast_to(b_r[:, None], (ROUTER_ROWS, LANES))
    cos_p, sin_p = _rope_tables(np.arange(seq))
    cos_s, sin_s = _rope_tables(PAST_LEN + np.arange(dec))
    tri = jnp.asarray(np.triu(np.ones((tm, tm), np.float32), 1)).astype(bf)
    consts = dict(
        anorm=attn_norm[0][None], fnorm=ffn_norm[0][None], w_in=w_in[0].astype(bf),
        cos_p=jnp.asarray(cos_p), sin_p=jnp.asarray(sin_p),
        cos_s=jnp.asarray(np.tile(cos_s, (nseq, 1))), sin_s=jnp.asarray(np.tile(sin_s, (nseq, 1))),
        bias_p=_prompt_bias(), bias_s=_sample_bias(dec), conv_w=conv_w[0],
        gn_attn=out_norm_attn[0][None], gn_conv=out_norm_conv[0][None], w_out=w_out[0].astype(bf),
        w_r=w_r, b_r=b_r, tri=tri)
    sinks = attn_sinks[0]

    mod = _adaln(jnp.concatenate([c_prompt, c_sample], axis=0), w_mod[0], b_mod[0])
    mod_p = mod[:bsz].reshape(bsz, 6, D_MODEL)
    mod_s = mod[bsz:]

    bm = EXPERT_BLOCK
    eids = jnp.arange(N_EXPERTS, dtype=jnp.int32)
    lookup = lambda table, e: jnp.sum(jnp.where(e[..., None] == eids, table, 0), axis=-1)
    by_token = lambda a: jnp.transpose(a, (1, 0, 2)).reshape(8, -1)
    ceil_bm = lambda c: (c + bm - 1) // bm * bm
    x1_p, h2_p, ri_p, rf_p, cnt_p, k_p, v_p, u_p = _prompt_mixer(x_prompt, mod_p, sinks, consts)
    cnt_p = cnt_p[:, 0].astype(jnp.int32)
    padded_p = ceil_bm(cnt_p)
    pstart_p = jnp.cumsum(padded_p) - padded_p
    nb_p = -(-(2 * m_prompt + N_EXPERTS * (bm - 1)) // bm)
    nb_o = -(-(2 * n_s + N_EXPERTS * (bm - 1)) // bm)
    n_blocks = nb_p + nb_o
    ri_p = by_token(ri_p)
    dest_p = (lookup(pstart_p, ri_p[0:2]) + ri_p[2:4]).reshape(2 * m_prompt)

    st = state_conv[0]
    zrow = jnp.zeros((nseq, 1, CONV_DIM), jnp.float32)
    e1 = jnp.concatenate([st[:, 1:2]] + [zrow] * (dec - 1), axis=1).reshape(n_s, CONV_DIM)
    e2 = jnp.concatenate([st[:, 0:1], st[:, 1:2]] + [zrow] * (dec - 2), axis=1).reshape(n_s, CONV_DIM)
    dest_p, e1 = lax.optimization_barrier((dest_p, e1))
    xs = _dispatch(dest_p[None], h2_p, n_blocks * bm)
    ck = cache_k[0].reshape(nseq * WINDOW * N_KV_HEADS, HEAD_DIM)
    cv = cache_v[0].reshape(nseq * WINDOW * N_KV_HEADS, HEAD_DIM)
    x1_s, h2_s, ri_s, rf_s, cnt_s, k_s, v_s, u_s = _sample_mixer(
        x_sample.reshape(n_s, D_MODEL), mod_s, sinks, consts, ck, cv, e1, e2,
        jnp.zeros((N_EXPERTS, LANES), jnp.float32), dec)

    cnt_s = cnt_s[:, 0].astype(jnp.int32)
    fit = jnp.minimum(cnt_s, padded_p - cnt_p)
    over = cnt_s - fit
    padded_o = ceil_bm(over)
    ostart = nb_p * bm + jnp.cumsum(padded_o) - padded_o
    ri_s = by_token(ri_s)
    e_s, r_s = ri_s[0:2], ri_s[2:4]
    dest_s = jnp.where(r_s < lookup(fit, e_s), lookup(pstart_p + cnt_p, e_s) + r_s, lookup(ostart - fit, e_s) + r_s)
    dest_s = dest_s.reshape(2 * n_s)

    new_k_s = k_s.reshape(1, nseq, WINDOW, N_KV_HEADS, HEAD_DIM)
    new_v_s = v_s.reshape(1, nseq, WINDOW, N_KV_HEADS, HEAD_DIM)
    xs = _scatter_rows(dest_s, h2_s, xs)

    np_e, no_e = padded_p // bm, padded_o // bm
    lend = jnp.cumsum(np_e + no_e)
    lstart = lend - np_e - no_e
    n_used = lend[-1].astype(jnp.int32)
    blk = jnp.arange(n_blocks, dtype=jnp.int32)
    present = np_e + no_e > 0
    last_e = jnp.max(jnp.where(present, eids, 0))
    block_e = jnp.sum((lend[None, :] <= blk[:, None]).astype(jnp.int32), axis=1)
    block_e = jnp.where(blk < n_used, jnp.minimum(block_e, N_EXPERTS - 1), last_e)
    of_block = lambda table: lookup(table, block_e)
    within = blk - of_block(lstart)
    in_prompt = within < of_block(np_e)
    phys = jnp.where(in_prompt, of_block(pstart_p) // bm + within, of_block(ostart) // bm + within - of_block(np_e))
    n_valid = jnp.where(in_prompt, of_block(cnt_p + fit) - within * bm, of_block(over) - (within - of_block(np_e)) * bm)
    n_valid = jnp.where(blk < n_used, jnp.clip(n_valid, 0, bm), 0).astype(jnp.int32)
    phys = jnp.where(blk < n_used, phys, jnp.sum(jnp.where(blk == n_used - 1, phys, 0))).astype(jnp.int32)
    later = jnp.logical_and(present[None, :], eids[None, :] > eids[:, None])
    next_of = jnp.min(jnp.where(later, eids[None, :], N_EXPERTS), axis=1)
    next_of = jnp.where(next_of == N_EXPERTS, -1, next_of)
    next_e = of_block(next_of).astype(jnp.int32)
    parity = of_block((jnp.cumsum(present.astype(jnp.int32)) - 1) % 2).astype(jnp.int32)
    yb = _experts(block_e, n_used.reshape(1), n_valid, next_e, parity, phys, xs, w_gate[0], w_up[0], w_down[0])

    fin = final_norm[None]
    tc = COMBINE_TILE
    rf_p, rf_s = by_token(rf_p), by_token(rf_s)
    y_p = None
    chunk = min(COMBINE_CHUNK, seq)
    assert seq % chunk == 0 and chunk % tc == 0
    for c in range(m_prompt // chunk):
        lo, hi = c * chunk, (c + 1) * chunk
        idx = jnp.concatenate([dest_p[lo:hi], dest_p[m_prompt + lo:m_prompt + hi]])[None]
        y_p = _combine(x1_p, mod_p[lo // seq], rf_p, fin, _gather_rows(idx, yb), y_p, lo // tc, chunk // tc,
                       per_token_gate=False)
    gate_s = jnp.repeat(mod_s[:, 5 * D_MODEL:], dec, axis=0)
    y_s = _combine(x1_s, gate_s, rf_s, fin, _gather_rows(dest_s[None], yb), None, 0, n_s // tc, per_token_gate=True)

    keep = WINDOW
    new_k_p = k_p.reshape(1, bsz, keep, N_KV_HEADS, HEAD_DIM)
    new_v_p = v_p.reshape(1, bsz, keep, N_KV_HEADS, HEAD_DIM)
    new_u_p = u_p[:, 6:8][None]
    u_all = jnp.concatenate([st, u_s.reshape(nseq, dec, CONV_DIM)], axis=1)
    new_u_s = u_all[:, -2:][None]
    return (y_p.reshape(bsz, seq, D_MODEL), y_s.reshape(nseq, dec, D_MODEL),
            new_k_p, new_v_p, new_u_p, new_k_s, new_v_s, new_u_s)
```

```python
import functools

import jax
import jax.numpy as jnp
import numpy as np
from jax import lax
from jax.experimental import pallas as pl
from jax.experimental.pallas import tpu as pltpu
from jax.experimental.pallas import tpu_sc as plsc

D_MODEL = 1024
HEAD_DIM = 64
ATTN_DIM = 512
N_Q_HEADS = 8
N_KV_HEADS = 2
KV_DIM = 128
CONV_DIM = 512
WINDOW = 128
IN_PROJ_DIM = ATTN_DIM + 2 * KV_DIM + 3 * CONV_DIM
N_GROUPS = 4
EXPERTS_PER_GROUP = 8
N_EXPERTS = 32
EXPERT_HIDDEN = 512
RMS_EPS = 1e-5
ROPE_THETA = 10000.0
PAST_LEN = 16384

LANES = 128
ROUTER_ROWS = 48
NEG = -1e30

MIXER_TILE = 512
SAMPLE_TILE = 128
SAMPLE_CHUNK = 8
SAMPLE_KEYS = 1152
EXPERT_BLOCK = 512
COMBINE_TILE = 512
COMBINE_CHUNK = 8192
SC_ROWS = 64
SC_INDEX_WINDOW = 128
VMEM_LIMIT = 56 * 1024 * 1024


def _rms(x, gain):
    ms = jnp.mean(x * x, axis=-1, keepdims=True)
    return x * lax.rsqrt(ms + RMS_EPS) * gain


def _half_lane_mask(rows=1):
    return lax.broadcasted_iota(jnp.int32, (rows, LANES), 1) < HEAD_DIM


def _rope(x, cos, sin_signed):
    t, w = x.shape
    reps = w // LANES
    lane = lax.broadcasted_iota(jnp.int32, (t, w), 1)
    upper = (lane % HEAD_DIM) >= (HEAD_DIM // 2)
    partner = jnp.where(upper, pltpu.roll(x, HEAD_DIM // 2, 1), pltpu.roll(x, w - HEAD_DIM // 2, 1))
    if reps > 1:
        cos = jnp.concatenate([cos] * reps, axis=1)
        sin_signed = jnp.concatenate([sin_signed] * reps, axis=1)
    return x * cos + partner * sin_signed


def _swap_halves(x):
    return pltpu.roll(x, HEAD_DIM, 1)


def _group_norm64(x, gain):
    t, w = x.shape
    lo = _half_lane_mask(t)
    outs = []
    for c in range(w // LANES):
        xc = x[:, c * LANES:(c + 1) * LANES]
        sq = xc * xc
        s_lo = jnp.sum(jnp.where(lo, sq, 0.0), axis=-1, keepdims=True)
        s_hi = jnp.sum(jnp.where(lo, 0.0, sq), axis=-1, keepdims=True)
        r = jnp.where(lo, lax.rsqrt(s_lo * (1.0 / HEAD_DIM) + RMS_EPS), lax.rsqrt(s_hi * (1.0 / HEAD_DIM) + RMS_EPS))
        outs.append(xc * r)
    return jnp.concatenate(outs, axis=1) * gain


def _attend(q_rows, keys, keys_sw, vals, vals_sw, bias, sinks_ref, group, transposed=False):
    r = q_rows.shape[0]
    lo = _half_lane_mask(r)
    outs = []
    for parity in range(2):
        heads = [4 * group + parity, 4 * group + parity + 2]
        kk, vv = (keys, vals) if parity == group else (keys_sw, vals_sw)
        half = lo if parity == 0 else jnp.logical_not(lo)
        qs = []
        for h in heads:
            pair = q_rows[:, (h // 2) * LANES:(h // 2 + 1) * LANES]
            qs.append(jnp.where(half, pair, jnp.zeros_like(pair)))
        qcat = jnp.concatenate(qs, axis=0)
        s = (jnp.dot(qcat, kk, preferred_element_type=jnp.float32) if transposed else
             lax.dot_general(qcat, kk, (((1,), (1,)), ((), ())), preferred_element_type=jnp.float32))
        ps, dens = [], []
        for i, h in enumerate(heads):
            sh = s[i * r:(i + 1) * r] + bias
            sink = sinks_ref[h]
            m = jnp.maximum(jnp.max(sh, axis=-1, keepdims=True), sink)
            p = jnp.exp(sh - m)
            dens.append(jnp.sum(p, axis=-1, keepdims=True) + jnp.exp(sink - m))
            ps.append(p.astype(jnp.bfloat16))
        pcat = jnp.concatenate(ps, axis=0)
        o = (lax.dot_general(pcat, vv, (((1,), (1,)), ((), ())), preferred_element_type=jnp.float32) if transposed
             else jnp.dot(pcat, vv, preferred_element_type=jnp.float32))
        for i, h in enumerate(heads):
            oh = o[i * r:(i + 1) * r] / dens[i]
            ms = jnp.sum(jnp.where(half, oh * oh, 0.0), axis=-1, keepdims=True) * (1.0 / HEAD_DIM)
            outs.append((h, oh * lax.rsqrt(ms + RMS_EPS)))
    outs.sort(key=lambda t: t[0])
    return [o for _, o in outs]


def _merge_heads(head_outs, lo):
    pairs = [jnp.where(lo, head_outs[2 * j], head_outs[2 * j + 1]) for j in range(4)]
    return jnp.concatenate(pairs, axis=1)


def _route(h2b, wr_ref, br_ref, tri_ref, cnt_ref, ri_ref, rf_ref):
    t = h2b.shape[0]
    lg = lax.dot_general(wr_ref[...], h2b, (((1,), (1,)), ((), ())), preferred_element_type=jnp.float32)
    lg = lg + br_ref[...][:, 0:1]
    r8 = lax.broadcasted_iota(jnp.int32, (8, t), 0)
    lgrp = jnp.where(r8 < N_GROUPS, lg[0:8], -jnp.inf)
    gmax = jnp.max(lgrp, axis=0, keepdims=True)
    grp = jnp.min(jnp.where(lgrp == gmax, r8, 8), axis=0, keepdims=True)
    pg_sel = 1.0 / jnp.sum(jnp.exp(lgrp - gmax), axis=0, keepdims=True)
    r32 = lax.broadcasted_iota(jnp.int32, (N_EXPERTS, t), 0)
    le = jnp.where((r32 // EXPERTS_PER_GROUP) == grp, lg[8:8 + N_EXPERTS], -jnp.inf)
    v1 = jnp.max(le, axis=0, keepdims=True)
    i1 = jnp.min(jnp.where(le == v1, r32, N_EXPERTS), axis=0, keepdims=True)
    le2 = jnp.where(r32 == i1, -jnp.inf, le)
    v2 = jnp.max(le2, axis=0, keepdims=True)
    i2 = jnp.min(jnp.where(le2 == v2, r32, N_EXPERTS), axis=0, keepdims=True)
    e21 = jnp.exp(v2 - v1)
    gate1 = pg_sel / (1.0 + e21)
    gate2 = pg_sel * e21 / (1.0 + e21)
    hot1 = r32 == i1
    hot2 = r32 == i2
    onehot = jnp.where(jnp.logical_or(hot1, hot2), 1.0, 0.0)
    before = jnp.dot(onehot.astype(jnp.bfloat16), tri_ref[...], preferred_element_type=jnp.float32)
    pos = before + cnt_ref[...][:, 0:1]
    rank1 = jnp.sum(jnp.where(hot1, pos, 0.0), axis=0, keepdims=True)
    rank2 = jnp.sum(jnp.where(hot2, pos, 0.0), axis=0, keepdims=True)
    cnt_ref[...] = cnt_ref[...] + jnp.sum(onehot, axis=1, keepdims=True)
    zi = jnp.zeros((4, t), jnp.int32)
    ri_ref[...] = jnp.concatenate([i1, i2, rank1.astype(jnp.int32), rank2.astype(jnp.int32), zi], axis=0)
    rf_ref[...] = jnp.concatenate([gate1, gate2, jnp.zeros((6, t), jnp.float32)], axis=0)


def _pack_bf16_pairs(xb):
    w = xb.shape[1] // 2
    bits = pltpu.bitcast(xb.astype(jnp.float32), jnp.uint32)
    return (bits[:, :w] >> 16) | (bits[:, w:] & jnp.uint32(0xFFFF0000))


def _unpack_pairs_f32(words):
    lo = pltpu.bitcast(words << 16, jnp.float32)
    hi = pltpu.bitcast(words & jnp.uint32(0xFFFF0000), jnp.float32)
    return jnp.concatenate([lo, hi], axis=1)


def _unpack_bf16_pairs(words):
    return _unpack_pairs_f32(words).astype(jnp.bfloat16)


def _split_proj(proj):
    a = ATTN_DIM
    q = proj[:, :a]
    k = proj[:, a:a + KV_DIM]
    v = proj[:, a + KV_DIM:a + 2 * KV_DIM]
    c0 = a + 2 * KV_DIM
    h_conv = proj[:, c0:c0 + CONV_DIM]
    gate_b = proj[:, c0 + CONV_DIM:c0 + 2 * CONV_DIM]
    gate_c = proj[:, c0 + 2 * CONV_DIM:c0 + 3 * CONV_DIM]
    return q, k, v, gate_c * h_conv, gate_b


def _mixer_tail(x, attn_n, conv_n, mod, wout_ref, fnorm_ref, wr_ref, br_ref, tri_ref, cnt_ref,
                x1_ref, h2_ref, ri_ref, rf_ref):
    gate_a, shift_f, scale_f = mod
    cat = jnp.concatenate([attn_n.astype(jnp.bfloat16), conv_n.astype(jnp.bfloat16)], axis=1)
    mix = jnp.dot(cat, wout_ref[...], preferred_element_type=jnp.float32)
    x1 = x + gate_a * mix
    x1_ref[...] = x1
    h2 = _rms(x1, fnorm_ref[...] * (1.0 + scale_f)) + shift_f
    h2b = h2.astype(jnp.bfloat16)
    h2_ref[...] = _pack_bf16_pairs(h2b)
    _route(h2b, wr_ref, br_ref, tri_ref, cnt_ref, ri_ref, rf_ref)


def _adaln_kernel(c_ref, w_ref, b_ref, o_ref):
    c = c_ref[...]
    a = (c * jax.nn.sigmoid(c)).astype(jnp.bfloat16)
    o_ref[...] = jnp.dot(a, w_ref[...].astype(jnp.bfloat16), preferred_element_type=jnp.float32) + b_ref[...]


def _adaln(c, w_mod, b_mod):
    n = c.shape[0]
    tn = 1536
    return pl.pallas_call(
        _adaln_kernel,
        grid=(w_mod.shape[1] // tn,),
        in_specs=[pl.BlockSpec((n, D_MODEL), lambda j: (0, 0)),
                  pl.BlockSpec((D_MODEL, tn), lambda j: (0, j)),
                  pl.BlockSpec((1, tn), lambda j: (0, j))],
        out_specs=pl.BlockSpec((n, tn), lambda j: (0, j)),
        out_shape=jax.ShapeDtypeStruct((n, w_mod.shape[1]), jnp.float32),
        compiler_params=pltpu.CompilerParams(dimension_semantics=("arbitrary",), vmem_limit_bytes=VMEM_LIMIT),
        name="adaln",
    )(c, w_mod, b_mod.reshape(1, -1))


def _prompt_mixer_kernel(sinks_ref, x_ref, mod_ref, anorm_ref, fnorm_ref, win_ref, cos_ref, sin_ref, bias_ref,
                         convw_ref, gna_ref, gnc_ref, wout_ref, wr_ref, br_ref, tri_ref,
                         x1_ref, h2_ref, ri_ref, rf_ref, cnt_out_ref, knew_ref, vnew_ref, unew_ref,
                         q_s, k_s, ksw_s, v_s, vsw_s, u_s, att_s, cnt_s):
    b = pl.program_id(0)
    t = pl.program_id(1)
    tm = MIXER_TILE
    nblk = tm // WINDOW

    @pl.when(jnp.logical_and(b == 0, t == 0))
    def _():
        cnt_s[...] = jnp.zeros_like(cnt_s)

    @pl.when(t == 0)
    def _():
        z = jnp.zeros((WINDOW, KV_DIM), jnp.bfloat16)
        k_s[0:WINDOW, :] = z
        ksw_s[0:WINDOW, :] = z
        v_s[0:WINDOW, :] = z
        vsw_s[0:WINDOW, :] = z
        u_s[0:8, :] = jnp.zeros((8, CONV_DIM), jnp.float32)

    x = x_ref[...]
    mod = mod_ref[...]
    h = _rms(x, anorm_ref[...] * (1.0 + mod[1:2])) + mod[0:1]
    proj = jnp.dot(h.astype(jnp.bfloat16), win_ref[...], preferred_element_type=jnp.float32)
    q, k, v, u, gate_b = _split_proj(proj)
    cos = cos_ref[...]
    sin = sin_ref[...]
    q = _rope(q, cos, sin) * (HEAD_DIM ** -0.5)
    k = _rope(k, cos, sin)
    q_s[...] = q.astype(jnp.bfloat16)
    k_s[WINDOW:, :] = k.astype(jnp.bfloat16)
    ksw_s[WINDOW:, :] = _swap_halves(k).astype(jnp.bfloat16)
    v_s[WINDOW:, :] = v.astype(jnp.bfloat16)
    vsw_s[WINDOW:, :] = _swap_halves(v).astype(jnp.bfloat16)
    knew_ref[...] = k[tm - WINDOW:, :]
    vnew_ref[...] = v[tm - WINDOW:, :]

    u_s[8:, :] = u
    unew_ref[...] = u[tm - 8:, :]
    cw = convw_ref[...]
    conv = (u_s[6:6 + tm, :] * cw[0:1] + u_s[7:7 + tm, :] * cw[1:2] + u * cw[2:3]) * gate_b
    u_s[0:8, :] = u[tm - 8:, :]
    conv_n = _group_norm64(conv, gnc_ref[...])

    lo = _half_lane_mask(WINDOW)
    first = t == 0

    def block(i, carry):
        r0 = pl.multiple_of(i * WINDOW, WINDOW)
        bias = bias_ref[jnp.where(jnp.logical_and(first, i == 0), 0, 1)]
        qb = q_s[pl.ds(r0, WINDOW), :]
        keys = k_s[pl.ds(r0, 2 * WINDOW), :]
        keys_sw = ksw_s[pl.ds(r0, 2 * WINDOW), :]
        vals = v_s[pl.ds(r0, 2 * WINDOW), :]
        vals_sw = vsw_s[pl.ds(r0, 2 * WINDOW), :]
        outs = []
        for g in range(N_KV_HEADS):
            outs += _attend(qb, keys, keys_sw, vals, vals_sw, bias, sinks_ref, g)
        att_s[pl.ds(r0, WINDOW), :] = _merge_heads(outs, lo)
        return carry

    lax.fori_loop(0, nblk, block, 0, unroll=True)
    k_s[0:WINDOW, :] = k_s[tm:tm + WINDOW, :]
    ksw_s[0:WINDOW, :] = ksw_s[tm:tm + WINDOW, :]
    v_s[0:WINDOW, :] = v_s[tm:tm + WINDOW, :]
    vsw_s[0:WINDOW, :] = vsw_s[tm:tm + WINDOW, :]

    attn_n = att_s[...] * gna_ref[...]
    _mixer_tail(x, attn_n, conv_n, (mod[2:3], mod[3:4], mod[4:5]), wout_ref, fnorm_ref, wr_ref, br_ref, tri_ref,
                cnt_s, x1_ref, h2_ref, ri_ref, rf_ref)
    cnt_out_ref[...] = cnt_s[...]


def _prompt_mixer(x, mod, sinks, consts):
    bsz, seq, _ = x.shape
    tm = MIXER_TILE
    nt = seq // tm
    full = lambda shape: pl.BlockSpec(shape, lambda b, t, s: (0,) * len(shape))
    in_specs = [
        pl.BlockSpec((None, tm, D_MODEL), lambda b, t, s: (b, t, 0)),
        pl.BlockSpec((None, 6, D_MODEL), lambda b, t, s: (b, 0, 0)),
        full((1, D_MODEL)), full((1, D_MODEL)),
        full((D_MODEL, IN_PROJ_DIM)),
        pl.BlockSpec((tm, LANES), lambda b, t, s: (t, 0)),
        pl.BlockSpec((tm, LANES), lambda b, t, s: (t, 0)),
        full((2, WINDOW, 2 * WINDOW)),
        full((3, CONV_DIM)), full((1, ATTN_DIM)), full((1, CONV_DIM)),
        full((D_MODEL, D_MODEL)),
        full((ROUTER_ROWS, D_MODEL)), full((ROUTER_ROWS, LANES)),
        full((tm, tm)),
    ]
    out_shape = [
        jax.ShapeDtypeStruct((bsz * seq, D_MODEL), jnp.float32),
        jax.ShapeDtypeStruct((bsz * seq, D_MODEL // 2), jnp.uint32),
        jax.ShapeDtypeStruct((bsz * nt, 8, tm), jnp.int32),
        jax.ShapeDtypeStruct((bsz * nt, 8, tm), jnp.float32),
        jax.ShapeDtypeStruct((N_EXPERTS, LANES), jnp.float32),
        jax.ShapeDtypeStruct((bsz, WINDOW, KV_DIM), jnp.float32),
        jax.ShapeDtypeStruct((bsz, WINDOW, KV_DIM), jnp.float32),
        jax.ShapeDtypeStruct((bsz, 8, CONV_DIM), jnp.float32),
    ]
    out_specs = [
        pl.BlockSpec((tm, D_MODEL), lambda b, t, s: (b * nt + t, 0)),
        pl.BlockSpec((tm, D_MODEL // 2), lambda b, t, s: (b * nt + t, 0)),
        pl.BlockSpec((None, 8, tm), lambda b, t, s: (b * nt + t, 0, 0)),
        pl.BlockSpec((None, 8, tm), lambda b, t, s: (b * nt + t, 0, 0)),
        pl.BlockSpec((N_EXPERTS, LANES), lambda b, t, s: (0, 0)),
        pl.BlockSpec((None, WINDOW, KV_DIM), lambda b, t, s: (b, 0, 0)),
        pl.BlockSpec((None, WINDOW, KV_DIM), lambda b, t, s: (b, 0, 0)),
        pl.BlockSpec((None, 8, CONV_DIM), lambda b, t, s: (b, 0, 0)),
    ]
    scratch = [
        pltpu.VMEM((tm, ATTN_DIM), jnp.bfloat16),
        pltpu.VMEM((tm + WINDOW, KV_DIM), jnp.bfloat16), pltpu.VMEM((tm + WINDOW, KV_DIM), jnp.bfloat16),
        pltpu.VMEM((tm + WINDOW, KV_DIM), jnp.bfloat16), pltpu.VMEM((tm + WINDOW, KV_DIM), jnp.bfloat16),
        pltpu.VMEM((tm + 8, CONV_DIM), jnp.float32),
        pltpu.VMEM((tm, ATTN_DIM), jnp.float32),
        pltpu.VMEM((N_EXPERTS, LANES), jnp.float32),
    ]
    return pl.pallas_call(
        _prompt_mixer_kernel,
        grid_spec=pltpu.PrefetchScalarGridSpec(
            num_scalar_prefetch=1, grid=(bsz, nt), in_specs=in_specs, out_specs=out_specs, scratch_shapes=scratch),
        out_shape=out_shape,
        compiler_params=pltpu.CompilerParams(dimension_semantics=("arbitrary", "arbitrary"),
                                             vmem_limit_bytes=VMEM_LIMIT),
        name="prompt_mixer",
    )(sinks, x, mod, consts["anorm"], consts["fnorm"], consts["w_in"], consts["cos_p"], consts["sin_p"],
      consts["bias_p"], consts["conv_w"], consts["gn_attn"], consts["gn_conv"], consts["w_out"],
      consts["w_r"], consts["b_r"], consts["tri"])


def _sample_mixer_kernel(sinks_ref, x_ref, mod_ref, anorm_ref, fnorm_ref, win_ref, cos_ref, sin_ref, bias_ref,
                         convw_ref, gna_ref, gnc_ref, wout_ref, wr_ref, br_ref, tri_ref,
                         ck_ref, cv_ref, e1_ref, e2_ref, cnt_in_ref,
                         x1_ref, h2_ref, ri_ref, rf_ref, cnt_out_ref, knew_ref, vnew_ref, unew_ref,
                         q_s, k_s, v_s, att_s, cnt_s, *, dec):
    n = x_ref.shape[0]
    cb = SAMPLE_CHUNK
    rows = cb * dec

    @pl.when(pl.program_id(0) == 0)
    def _():
        cnt_s[...] = cnt_in_ref[...]

    x = x_ref[...]
    mod = jnp.repeat(mod_ref[...], dec, axis=0)
    md = lambda j: mod[:, j * D_MODEL:(j + 1) * D_MODEL]
    h = _rms(x, anorm_ref[...]) * (1.0 + md(1)) + md(0)
    proj = jnp.dot(h.astype(jnp.bfloat16), win_ref[...], preferred_element_type=jnp.float32)
    q, k, v, u, gate_b = _split_proj(proj)
    cos = cos_ref[...]
    sin = sin_ref[...]
    q = _rope(q, cos, sin) * (HEAD_DIM ** -0.5)
    k = _rope(k, cos, sin)
    q_s[...] = q.astype(jnp.bfloat16)
    k_s[...] = k
    v_s[...] = v
    unew_ref[...] = u

    pos = lax.broadcasted_iota(jnp.int32, (n, CONV_DIM), 0) % dec
    um1 = jnp.where(pos >= 1, pltpu.roll(u, 1, 0), 0.0) + e1_ref[...]
    um2 = jnp.where(pos >= 2, pltpu.roll(u, 2, 0), 0.0) + e2_ref[...]
    cw = convw_ref[...]
    conv = (um2 * cw[0:1] + um1 * cw[1:2] + u * cw[2:3]) * gate_b
    conv_n = _group_norm64(conv, gnc_ref[...])

    lo = _half_lane_mask(rows)
    bias = bias_ref[...]

    half = KV_DIM // 2
    lane = lax.broadcasted_iota(jnp.int32, (KV_DIM, WINDOW), 1)
    swap_rows = lambda t: jnp.concatenate([t[half:], t[:half]], axis=0)

    for c in range(n // rows):
        r0 = c * rows
        qb = q_s[r0:r0 + rows, :]
        cats = []
        for ref, new_rows, out_ref in ((ck_ref, k_s[r0:r0 + rows, :], knew_ref), (cv_ref, v_s[r0:r0 + rows, :], vnew_ref)):
            tiles = [ref[(c * cb + b) * KV_DIM:(c * cb + b + 1) * KV_DIM, :] for b in range(cb)]
            new_t = jnp.concatenate([jnp.transpose(new_rows), jnp.zeros((KV_DIM, WINDOW - rows), jnp.float32)], axis=1)
            cat = jnp.concatenate(tiles + [new_t], axis=1)
            cats.append((cat.astype(jnp.bfloat16), swap_rows(cat).astype(jnp.bfloat16)))
            for b in range(cb):
                kept = pltpu.roll(tiles[b], WINDOW - dec, 1)
                fresh = pltpu.roll(new_t, (WINDOW - dec - b * dec) % WINDOW, 1)
                out_ref[(c * cb + b) * KV_DIM:(c * cb + b + 1) * KV_DIM, :] = jnp.where(lane < WINDOW - dec, kept, fresh)
        (keys, keys_sw), (vals, vals_sw) = cats
        outs = []
        for g in range(N_KV_HEADS):
            outs += _attend(qb, keys, keys_sw, vals, vals_sw, bias, sinks_ref, g, transposed=True)
        att_s[r0:r0 + rows, :] = _merge_heads(outs, lo)

    attn_n = att_s[...] * gna_ref[...]
    _mixer_tail(x, attn_n, conv_n, (md(2), md(3), md(4)), wout_ref, fnorm_ref, wr_ref, br_ref, tri_ref,
                cnt_s, x1_ref, h2_ref, ri_ref, rf_ref)
    cnt_out_ref[...] = cnt_s[...]


def _sample_mixer(x, mod, sinks, consts, ck, cv, e1, e2, cnt_in, dec):
    n_s = x.shape[0]
    ts = SAMPLE_TILE
    seqs = ts // dec
    full = lambda a: pl.BlockSpec(a.shape, lambda i, s: (0,) * a.ndim)
    rows = lambda width: pl.BlockSpec((ts, width), lambda i, s: (i, 0))
    in_specs = [
        rows(D_MODEL), pl.BlockSpec((seqs, 6 * D_MODEL), lambda i, s: (i, 0)),
        full(consts["anorm"]), full(consts["fnorm"]), full(consts["w_in"]),
        rows(LANES), rows(LANES), full(consts["bias_s"]),
        full(consts["conv_w"]), full(consts["gn_attn"]), full(consts["gn_conv"]), full(consts["w_out"]),
        full(consts["w_r"]), full(consts["b_r"]),
        pl.BlockSpec((ts, ts), lambda i, s: (0, 0)),
        pl.BlockSpec((seqs * KV_DIM, WINDOW), lambda i, s: (i, 0)),
        pl.BlockSpec((seqs * KV_DIM, WINDOW), lambda i, s: (i, 0)),
        rows(CONV_DIM), rows(CONV_DIM), full(cnt_in),
    ]
    args = [x, mod, consts["anorm"], consts["fnorm"], consts["w_in"], consts["cos_s"], consts["sin_s"],
            consts["bias_s"], consts["conv_w"], consts["gn_attn"], consts["gn_conv"], consts["w_out"],
            consts["w_r"], consts["b_r"], consts["tri"], ck, cv, e1, e2, cnt_in]
    nt = n_s // ts
    out_shape = [
        jax.ShapeDtypeStruct((n_s, D_MODEL), jnp.float32),
        jax.ShapeDtypeStruct((n_s, D_MODEL // 2), jnp.uint32),
        jax.ShapeDtypeStruct((nt, 8, ts), jnp.int32),
        jax.ShapeDtypeStruct((nt, 8, ts), jnp.float32),
        jax.ShapeDtypeStruct((N_EXPERTS, LANES), jnp.float32),
        jax.ShapeDtypeStruct(ck.shape, jnp.float32),
        jax.ShapeDtypeStruct(cv.shape, jnp.float32),
        jax.ShapeDtypeStruct((n_s, CONV_DIM), jnp.float32),
    ]
    out_specs = [
        rows(D_MODEL),
        rows(D_MODEL // 2),
        pl.BlockSpec((None, 8, ts), lambda i, s: (i, 0, 0)),
        pl.BlockSpec((None, 8, ts), lambda i, s: (i, 0, 0)),
        pl.BlockSpec((N_EXPERTS, LANES), lambda i, s: (0, 0)),
        pl.BlockSpec((seqs * KV_DIM, WINDOW), lambda i, s: (i, 0)),
        pl.BlockSpec((seqs * KV_DIM, WINDOW), lambda i, s: (i, 0)),
        rows(CONV_DIM),
    ]
    scratch = [
        pltpu.VMEM((ts, ATTN_DIM), jnp.bfloat16),
        pltpu.VMEM((ts, KV_DIM), jnp.float32), pltpu.VMEM((ts, KV_DIM), jnp.float32),
        pltpu.VMEM((ts, ATTN_DIM), jnp.float32),
        pltpu.VMEM((N_EXPERTS, LANES), jnp.float32),
    ]
    return pl.pallas_call(
        functools.partial(_sample_mixer_kernel, dec=dec),
        grid_spec=pltpu.PrefetchScalarGridSpec(
            num_scalar_prefetch=1, grid=(nt,), in_specs=in_specs, out_specs=out_specs, scratch_shapes=scratch),
        out_shape=out_shape,
        compiler_params=pltpu.CompilerParams(dimension_semantics=("arbitrary",), vmem_limit_bytes=VMEM_LIMIT),
        name="sample_mixer",
    )(sinks, *args)


def _dispatch(idx, h2, n_rows):
    mesh = plsc.VectorSubcoreMesh(core_axis_name="core", subcore_axis_name="subcore")
    rows, window = SC_ROWS, SC_INDEX_WINDOW
    sub = window // rows
    n_tok, width = h2.shape
    assert n_tok % rows == 0 and (2 * n_tok) % window == 0

    moved = 3 * n_tok * width * h2.dtype.itemsize
    @pl.kernel(out_type=jax.ShapeDtypeStruct((n_rows, width), h2.dtype), mesh=mesh, scratch_types=[],
               cost_estimate=pl.CostEstimate(flops=0, transcendentals=0, bytes_accessed=moved))
    def scatter(h_hbm, i_hbm, xs_hbm):
        def body(indices, h_vmem, i_vmem):
            pltpu.sync_copy(h_vmem, xs_hbm.at[i_vmem.at[0, pl.ds(indices[1] * rows, rows)]])

        pltpu.emit_pipeline(
            body, grid=(2 * n_tok // window, sub),
            in_specs=[pl.BlockSpec((rows, width), lambda i, j: ((i * sub + j) % (n_tok // rows), 0)),
                      pl.BlockSpec((1, window), lambda i, j: (0, i))],
            out_specs=[], core_axis_name=("core", "subcore"),
            dimension_semantics=(pltpu.PARALLEL, pltpu.ARBITRARY), _explicit_indices=True,
        )(h_hbm, i_hbm)

    return scatter(h2, idx)


def _scatter_rows_kernel(dest_ref, h_ref, xs_in_ref, xs_ref, sem):
    del xs_in_ref
    n_tok = h_ref.shape[0]

    def issue(i, carry):
        for k in range(2):
            d = dest_ref[k * n_tok + i]
            pltpu.make_async_copy(h_ref.at[pl.ds(i, 1)], xs_ref.at[pl.ds(d, 1)], sem.at[0]).start(priority=k)
        return carry

    lax.fori_loop(0, n_tok, issue, 0, unroll=8)
    for _ in range(2):
        pltpu.make_async_copy(h_ref, xs_ref.at[pl.ds(0, n_tok)], sem.at[0]).wait()


def _scatter_rows(dest, h2, xs):
    return pl.pallas_call(
        _scatter_rows_kernel,
        grid_spec=pltpu.PrefetchScalarGridSpec(
            num_scalar_prefetch=1, grid=(1,),
            in_specs=[pl.BlockSpec(h2.shape, lambda i, d: (0, 0)), pl.BlockSpec(memory_space=pl.ANY)],
            out_specs=pl.BlockSpec(memory_space=pl.ANY),
            scratch_shapes=[pltpu.SemaphoreType.DMA((1,))]),
        out_shape=jax.ShapeDtypeStruct(xs.shape, xs.dtype),
        input_output_aliases={2: 0},
        compiler_params=pltpu.CompilerParams(dimension_semantics=("arbitrary",)),
        name="scatter_rows",
    )(dest, h2, xs)


def _experts_kernel(be_ref, nu_ref, nv_ref, nxt_ref, par_ref, phys_ref, xs_ref, wg_ref, wu_ref, wd_ref, y_ref,
                    wg_s, wu_s, wd_s, wg_f, wu_f, wd_f, sem):
    del phys_ref
    j = pl.program_id(0)
    used = j < nu_ref[0]
    changed = jnp.logical_or(j == 0, be_ref[j] != be_ref[jnp.maximum(j - 1, 0)])

    def weight_copies(e, slot):
        return [pltpu.make_async_copy(src.at[e], dst.at[slot], sem.at[slot])
                for src, dst in ((wg_ref, wg_f), (wu_ref, wu_f), (wd_ref, wd_f))]

    @pl.when(jnp.logical_and(used, changed))
    def _():
        slot = par_ref[j]

        @pl.when(j == 0)
        def _():
            for c in weight_copies(be_ref[0], slot):
                c.start()

        for c in weight_copies(be_ref[j], slot):
            c.wait()

        @pl.when(nxt_ref[j] >= 0)
        def _():
            for c in weight_copies(nxt_ref[j], 1 - slot):
                c.start()

        wg_s[...] = wg_f[slot].astype(jnp.bfloat16)
        wu_s[...] = wu_f[slot].astype(jnp.bfloat16)
        wd_s[...] = wd_f[slot].astype(jnp.bfloat16)

    @pl.when(used)
    def _():
        live = lax.broadcasted_iota(jnp.int32, (xs_ref.shape[0], 1), 0) < nv_ref[j]
        x = _unpack_bf16_pairs(jnp.where(live, xs_ref[...], jnp.uint32(0)))
        g = jnp.dot(x, wg_s[...], preferred_element_type=jnp.float32)
        u = jnp.dot(x, wu_s[...], preferred_element_type=jnp.float32)
        a = (g * jax.nn.sigmoid(g) * u).astype(jnp.bfloat16)
        y = jnp.dot(a, wd_s[...], preferred_element_type=jnp.float32)
        y_ref[...] = _pack_bf16_pairs(y.astype(jnp.bfloat16))


def _experts(block_e, n_used, n_valid, next_e, parity, phys, xs, w_gate, w_up, w_down):
    bm = EXPERT_BLOCK
    n_blocks = xs.shape[0] // bm
    hbm = pl.BlockSpec(memory_space=pl.ANY)
    return pl.pallas_call(
        _experts_kernel,
        grid_spec=pltpu.PrefetchScalarGridSpec(
            num_scalar_prefetch=6, grid=(n_blocks,),
            in_specs=[pl.BlockSpec((bm, D_MODEL // 2), lambda j, be, nu, nv, nx, pa, ph: (ph[j], 0)),
                      hbm, hbm, hbm],
            out_specs=pl.BlockSpec((bm, D_MODEL // 2), lambda j, be, nu, nv, nx, pa, ph: (ph[j], 0)),
            scratch_shapes=[pltpu.VMEM((D_MODEL, EXPERT_HIDDEN), jnp.bfloat16),
                            pltpu.VMEM((D_MODEL, EXPERT_HIDDEN), jnp.bfloat16),
                            pltpu.VMEM((EXPERT_HIDDEN, D_MODEL), jnp.bfloat16),
                            pltpu.VMEM((2, D_MODEL, EXPERT_HIDDEN), jnp.float32),
                            pltpu.VMEM((2, D_MODEL, EXPERT_HIDDEN), jnp.float32),
                            pltpu.VMEM((2, EXPERT_HIDDEN, D_MODEL), jnp.float32),
                            pltpu.SemaphoreType.DMA((2,))]),
        out_shape=jax.ShapeDtypeStruct((xs.shape[0], D_MODEL // 2), jnp.uint32),
        compiler_params=pltpu.CompilerParams(dimension_semantics=("arbitrary",), vmem_limit_bytes=VMEM_LIMIT),
        name="experts",
    )(block_e, n_used, n_valid, next_e, parity, phys, xs, w_gate, w_up, w_down)


def _gather_rows(idx, yb):
    mesh = plsc.VectorSubcoreMesh(core_axis_name="core", subcore_axis_name="subcore")
    rows, window = SC_ROWS, SC_INDEX_WINDOW
    sub = window // rows
    n = idx.shape[1]
    width = yb.shape[1]
    assert n % window == 0

    @pl.kernel(out_type=jax.ShapeDtypeStruct((n, width), yb.dtype), mesh=mesh, scratch_types=[])
    def gather(y_hbm, i_hbm, o_hbm):
        def body(indices, i_vmem, o_vmem):
            pltpu.sync_copy(y_hbm.at[i_vmem.at[0, pl.ds(indices[1] * rows, rows)]], o_vmem)

        pltpu.emit_pipeline(
            body, grid=(n // window, sub),
            in_specs=[pl.BlockSpec((1, window), lambda i, j: (0, i))],
            out_specs=[pl.BlockSpec((rows, width), lambda i, j: (i * sub + j, 0))],
            core_axis_name=("core", "subcore"),
            dimension_semantics=(pltpu.PARALLEL, pltpu.ARBITRARY), _explicit_indices=True,
        )(i_hbm, o_hbm)

    return gather(yb, idx)


def _combine_kernel(x1_ref, gate_ref, rf_ref, fin_ref, y0_ref, y1_ref, *out_refs, per_token_gate):
    o_ref = out_refs[-1]
    gates = jnp.transpose(rf_ref[...])
    moe = _unpack_pairs_f32(y0_ref[...]) * gates[:, 0:1] + _unpack_pairs_f32(y1_ref[...]) * gates[:, 1:2]
    gate_f = gate_ref[...] if per_token_gate else gate_ref[...][5:6]
    o_ref[...] = _rms(x1_ref[...] + gate_f * moe, fin_ref[...])


def _combine(x1, gate, rf, final_norm, ys, out_prev, first_tile, n_tiles, *, per_token_gate):
    tc = COMBINE_TILE
    gate_spec = (pl.BlockSpec((tc, D_MODEL), lambda i: (first_tile + i, 0)) if per_token_gate
                 else pl.BlockSpec((6, D_MODEL), lambda i: (0, 0)))
    args = [x1, gate, rf, final_norm, ys, ys]
    in_specs = [pl.BlockSpec((tc, D_MODEL), lambda i: (first_tile + i, 0)),
                gate_spec,
                pl.BlockSpec((8, tc), lambda i: (0, first_tile + i)),
                pl.BlockSpec((1, D_MODEL), lambda i: (0, 0)),
                pl.BlockSpec((tc, D_MODEL // 2), lambda i: (i, 0)),
                pl.BlockSpec((tc, D_MODEL // 2), lambda i: (n_tiles + i, 0))]
    aliases = {}
    if out_prev is not None:
        args.append(out_prev)
        in_specs.append(pl.BlockSpec(memory_space=pl.ANY))
        aliases = {len(args) - 1: 0}
    return pl.pallas_call(
        functools.partial(_combine_kernel, per_token_gate=per_token_gate),
        grid=(n_tiles,), in_specs=in_specs,
        out_specs=pl.BlockSpec((tc, D_MODEL), lambda i: (first_tile + i, 0)),
        out_shape=jax.ShapeDtypeStruct(x1.shape, jnp.float32),
        input_output_aliases=aliases,
        compiler_params=pltpu.CompilerParams(dimension_semantics=("arbitrary",), vmem_limit_bytes=VMEM_LIMIT),
        name="combine",
    )(*args)


def _rope_tables(pos):
    half = HEAD_DIM // 2
    inv_freq = np.power(np.float32(ROPE_THETA), -(np.arange(half, dtype=np.float32) / np.float32(half)))
    ang = (pos.astype(np.float32)[:, None] * inv_freq[None, :].astype(np.float32)).astype(np.float64)
    cos, sin = np.cos(ang).astype(np.float32), np.sin(ang).astype(np.float32)
    cos = np.concatenate([cos, cos, cos, cos], axis=1)
    sin = np.concatenate([-sin, sin, -sin, sin], axis=1)
    return cos, sin


def _prompt_bias():
    r = np.arange(WINDOW)[:, None]
    j = np.arange(2 * WINDOW)[None, :]
    band = (j > r) & (j <= r + WINDOW)
    later = np.where(band, 0.0, NEG).astype(np.float32)
    first = np.where(band & (j >= WINDOW), 0.0, NEG).astype(np.float32)
    return jnp.asarray(np.stack([first, later]))


def _sample_bias(dec):
    cb = SAMPLE_CHUNK
    qb = np.repeat(np.arange(cb), dec)[:, None]
    qi = np.tile(np.arange(dec), cb)[:, None]
    kb = np.concatenate([np.repeat(np.arange(cb), WINDOW), np.repeat(np.arange(cb), dec)])[None, :]
    kj = np.concatenate([np.tile(np.arange(WINDOW), cb), np.tile(np.arange(dec), cb)])[None, :]
    is_new = (np.arange(cb * WINDOW + cb * dec) >= cb * WINDOW)[None, :]
    ok = (kb == qb) & np.where(is_new, kj <= qi, kj > qi)
    ok = np.pad(ok, ((0, 0), (0, SAMPLE_KEYS - ok.shape[1])))
    return jnp.asarray(np.where(ok, 0.0, NEG).astype(np.float32))


def kernel(x_prompt, x_sample, cache_k, cache_v, state_conv, c_prompt, c_sample, attn_norm, ffn_norm, w_mod, b_mod, w_in, conv_w, attn_sinks, out_norm_attn, out_norm_conv, w_out, w_group, b_group, w_expert, b_expert, w_gate, w_up, w_down, final_norm):
    assert attn_norm.shape[0] == 1, "one layer"
    bsz, seq, _ = x_prompt.shape
    nseq, dec, _ = x_sample.shape
    m_prompt = bsz * seq
    n_s = nseq * dec
    m_total = m_prompt + n_s
    tm = MIXER_TILE
    assert seq % tm == 0 and n_s % SAMPLE_TILE == 0 and SAMPLE_TILE % (SAMPLE_CHUNK * dec) == 0
    assert cache_k.shape[2] == WINDOW and m_prompt % SAMPLE_TILE == 0 and m_total % COMBINE_TILE == 0
    bf = jnp.bfloat16

    w_r = jnp.zeros((ROUTER_ROWS, D_MODEL), jnp.float32)
    w_r = w_r.at[0:N_GROUPS].set(w_group[0].T).at[8:8 + N_EXPERTS].set(w_expert[0].T).astype(bf)
    b_r = jnp.zeros((ROUTER_ROWS,), jnp.float32).at[0:N_GROUPS].set(b_group[0]).at[8:8 + N_EXPERTS].set(b_expert[0])
    b_r = jnp.broadcast_to(b_r[:, None], (ROUTER_ROWS, LANES))
    cos_p, sin_p = _rope_tables(np.arange(seq))
    cos_s, sin_s = _rope_tables(PAST_LEN + np.arange(dec))
    tri = jnp.asarray(np.triu(np.ones((tm, tm), np.float32), 1)).astype(bf)
    consts = dict(
        anorm=attn_norm[0][None], fnorm=ffn_norm[0][None], w_in=w_in[0].astype(bf),
        cos_p=jnp.asarray(cos_p), sin_p=jnp.asarray(sin_p),
        cos_s=jnp.asarray(np.tile(cos_s, (nseq, 1))), sin_s=jnp.asarray(np.tile(sin_s, (nseq, 1))),
        bias_p=_prompt_bias(), bias_s=_sample_bias(dec), conv_w=conv_w[0],
        gn_attn=out_norm_attn[0][None], gn_conv=out_norm_conv[0][None], w_out=w_out[0].astype(bf),
        w_r=w_r, b_r=b_r, tri=tri)
    sinks = attn_sinks[0]

    mod = _adaln(jnp.concatenate([c_prompt, c_sample], axis=0), w_mod[0], b_mod[0])
    mod_p = mod[:bsz].reshape(bsz, 6, D_MODEL)
    mod_s = mod[bsz:]

    bm = EXPERT_BLOCK
    eids = jnp.arange(N_EXPERTS, dtype=jnp.int32)
    lookup = lambda table, e: jnp.sum(jnp.where(e[..., None] == eids, table, 0), axis=-1)
    by_token = lambda a: jnp.transpose(a, (1, 0, 2)).reshape(8, -1)
    ceil_bm = lambda c: (c + bm - 1) // bm * bm
    x1_p, h2_p, ri_p, rf_p, cnt_p, k_p, v_p, u_p = _prompt_mixer(x_prompt, mod_p, sinks, consts)
    cnt_p = cnt_p[:, 0].astype(jnp.int32)
    padded_p = ceil_bm(cnt_p)
    pstart_p = jnp.cumsum(padded_p) - padded_p
    nb_p = -(-(2 * m_prompt + N_EXPERTS * (bm - 1)) // bm)
    nb_o = -(-(2 * n_s + N_EXPERTS * (bm - 1)) // bm)
    n_blocks = nb_p + nb_o
    ri_p = by_token(ri_p)
    dest_p = (lookup(pstart_p, ri_p[0:2]) + ri_p[2:4]).reshape(2 * m_prompt)

    st = state_conv[0]
    zrow = jnp.zeros((nseq, 1, CONV_DIM), jnp.float32)
    e1 = jnp.concatenate([st[:, 1:2]] + [zrow] * (dec - 1), axis=1).reshape(n_s, CONV_DIM)
    e2 = jnp.concatenate([st[:, 0:1], st[:, 1:2]] + [zrow] * (dec - 2), axis=1).reshape(n_s, CONV_DIM)
    dest_p, e1 = lax.optimization_barrier((dest_p, e1))
    xs = _dispatch(dest_p[None], h2_p, n_blocks * bm)
    ck = jnp.transpose(cache_k[0], (0, 2, 3, 1)).reshape(nseq * KV_DIM, WINDOW)
    cv = jnp.transpose(cache_v[0], (0, 2, 3, 1)).reshape(nseq * KV_DIM, WINDOW)
    x1_s, h2_s, ri_s, rf_s, cnt_s, k_s, v_s, u_s = _sample_mixer(
        x_sample.reshape(n_s, D_MODEL), mod_s, sinks, consts, ck, cv, e1, e2,
        jnp.zeros((N_EXPERTS, LANES), jnp.float32), dec)

    cnt_s = cnt_s[:, 0].astype(jnp.int32)
    fit = jnp.minimum(cnt_s, padded_p - cnt_p)
    over = cnt_s - fit
    padded_o = ceil_bm(over)
    ostart = nb_p * bm + jnp.cumsum(padded_o) - padded_o
    ri_s = by_token(ri_s)
    e_s, r_s = ri_s[0:2], ri_s[2:4]
    dest_s = jnp.where(r_s < lookup(fit, e_s), lookup(pstart_p + cnt_p, e_s) + r_s, lookup(ostart - fit, e_s) + r_s)
    dest_s = dest_s.reshape(2 * n_s)

    from_t = lambda a: jnp.transpose(a.reshape(nseq, N_KV_HEADS, HEAD_DIM, WINDOW), (0, 3, 1, 2))[None]
    new_k_s, new_v_s = from_t(k_s), from_t(v_s)
    xs = _scatter_rows(dest_s, h2_s, xs)

    np_e, no_e = padded_p // bm, padded_o // bm
    lend = jnp.cumsum(np_e + no_e)
    lstart = lend - np_e - no_e
    n_used = lend[-1].astype(jnp.int32)
    blk = jnp.arange(n_blocks, dtype=jnp.int32)
    present = np_e + no_e > 0
    last_e = jnp.max(jnp.where(present, eids, 0))
    block_e = jnp.sum((lend[None, :] <= blk[:, None]).astype(jnp.int32), axis=1)
    block_e = jnp.where(blk < n_used, jnp.minimum(block_e, N_EXPERTS - 1), last_e)
    of_block = lambda table: lookup(table, block_e)
    within = blk - of_block(lstart)
    in_prompt = within < of_block(np_e)
    phys = jnp.where(in_prompt, of_block(pstart_p) // bm + within, of_block(ostart) // bm + within - of_block(np_e))
    n_valid = jnp.where(in_prompt, of_block(cnt_p + fit) - within * bm, of_block(over) - (within - of_block(np_e)) * bm)
    n_valid = jnp.where(blk < n_used, jnp.clip(n_valid, 0, bm), 0).astype(jnp.int32)
    phys = jnp.where(blk < n_used, phys, jnp.sum(jnp.where(blk == n_used - 1, phys, 0))).astype(jnp.int32)
    later = jnp.logical_and(present[None, :], eids[None, :] > eids[:, None])
    next_of = jnp.min(jnp.where(later, eids[None, :], N_EXPERTS), axis=1)
    next_of = jnp.where(next_of == N_EXPERTS, -1, next_of)
    next_e = of_block(next_of).astype(jnp.int32)
    parity = of_block((jnp.cumsum(present.astype(jnp.int32)) - 1) % 2).astype(jnp.int32)
    yb = _experts(block_e, n_used.reshape(1), n_valid, next_e, parity, phys, xs, w_gate[0], w_up[0], w_down[0])

    fin = final_norm[None]
    tc = COMBINE_TILE
    rf_p, rf_s = by_token(rf_p), by_token(rf_s)
    y_p = None
    chunk = min(COMBINE_CHUNK, seq)
    assert seq % chunk == 0 and chunk % tc == 0
    for c in range(m_prompt // chunk):
        lo, hi = c * chunk, (c + 1) * chunk
        idx = jnp.concatenate([dest_p[lo:hi], dest_p[m_prompt + lo:m_prompt + hi]])[None]
        y_p = _combine(x1_p, mod_p[lo // seq], rf_p, fin, _gather_rows(idx, yb), y_p, lo // tc, chunk // tc,
                       per_token_gate=False)
    gate_s = jnp.repeat(mod_s[:, 5 * D_MODEL:], dec, axis=0)
    y_s = _combine(x1_s, gate_s, rf_s, fin, _gather_rows(dest_s[None], yb), None, 0, n_s // tc, per_token_gate=True)

    keep = WINDOW
    new_k_p = k_p.reshape(1, bsz, keep, N_KV_HEADS, HEAD_DIM)
    new_v_p = v_p.reshape(1, bsz, keep, N_KV_HEADS, HEAD_DIM)
    new_u_p = u_p[:, 6:8][None]
    u_all = jnp.concatenate([st, u_s.reshape(nseq, dec, CONV_DIM)], axis=1)
    new_u_s = u_all[:, -2:][None]
    return (y_p.reshape(bsz, seq, D_MODEL), y_s.reshape(nseq, dec, D_MODEL),
            new_k_p, new_v_p, new_u_p, new_k_s, new_v_s, new_u_s)
```

```python
import functools

import jax
import jax.numpy as jnp
import numpy as np
from jax import lax
from jax.experimental import pallas as pl
from jax.experimental.pallas import tpu as pltpu
from jax.experimental.pallas import tpu_sc as plsc

D_MODEL = 1024
HEAD_DIM = 64
ATTN_DIM = 512
N_Q_HEADS = 8
N_KV_HEADS = 2
KV_DIM = 128
CONV_DIM = 512
WINDOW = 128
IN_PROJ_DIM = ATTN_DIM + 2 * KV_DIM + 3 * CONV_DIM
N_GROUPS = 4
EXPERTS_PER_GROUP = 8
N_EXPERTS = 32
EXPERT_HIDDEN = 512
RMS_EPS = 1e-5
ROPE_THETA = 10000.0
PAST_LEN = 16384

LANES = 128
ROUTER_ROWS = 48
NEG = -1e30

MIXER_TILE = 512
SAMPLE_TILE = 128
SAMPLE_CHUNK = 8
SAMPLE_KEYS = 1152
EXPERT_BLOCK = 512
COMBINE_TILE = 512
COMBINE_CHUNK = 4096
SC_ROWS = 64
SC_INDEX_WINDOW = 128
VMEM_LIMIT = 56 * 1024 * 1024


def _rms(x, gain):
    ms = jnp.mean(x * x, axis=-1, keepdims=True)
    return x * lax.rsqrt(ms + RMS_EPS) * gain


def _half_lane_mask(rows=1):
    return lax.broadcasted_iota(jnp.int32, (rows, LANES), 1) < HEAD_DIM


def _rope(x, cos, sin_signed):
    t, w = x.shape
    reps = w // LANES
    lane = lax.broadcasted_iota(jnp.int32, (t, w), 1)
    upper = (lane % HEAD_DIM) >= (HEAD_DIM // 2)
    partner = jnp.where(upper, pltpu.roll(x, HEAD_DIM // 2, 1), pltpu.roll(x, w - HEAD_DIM // 2, 1))
    if reps > 1:
        cos = jnp.concatenate([cos] * reps, axis=1)
        sin_signed = jnp.concatenate([sin_signed] * reps, axis=1)
    return x * cos + partner * sin_signed


def _swap_halves(x):
    return pltpu.roll(x, HEAD_DIM, 1)


def _group_norm64(x, gain):
    t, w = x.shape
    lo = _half_lane_mask(t)
    outs = []
    for c in range(w // LANES):
        xc = x[:, c * LANES:(c + 1) * LANES]
        sq = xc * xc
        s_lo = jnp.sum(jnp.where(lo, sq, 0.0), axis=-1, keepdims=True)
        s_hi = jnp.sum(jnp.where(lo, 0.0, sq), axis=-1, keepdims=True)
        r = jnp.where(lo, lax.rsqrt(s_lo * (1.0 / HEAD_DIM) + RMS_EPS), lax.rsqrt(s_hi * (1.0 / HEAD_DIM) + RMS_EPS))
        outs.append(xc * r)
    return jnp.concatenate(outs, axis=1) * gain


def _attend(q_rows, keys, keys_sw, vals, vals_sw, bias, sinks_ref, group, transposed=False):
    r = q_rows.shape[0]
    lo = _half_lane_mask(r)
    outs = []
    for parity in range(2):
        heads = [4 * group + parity, 4 * group + parity + 2]
        kk, vv = (keys, vals) if parity == group else (keys_sw, vals_sw)
        half = lo if parity == 0 else jnp.logical_not(lo)
        qs = []
        for h in heads:
            pair = q_rows[:, (h // 2) * LANES:(h // 2 + 1) * LANES]
            qs.append(jnp.where(half, pair, jnp.zeros_like(pair)))
        qcat = jnp.concatenate(qs, axis=0)
        s = (jnp.dot(qcat, kk, preferred_element_type=jnp.float32) if transposed else
             lax.dot_general(qcat, kk, (((1,), (1,)), ((), ())), preferred_element_type=jnp.float32))
        ps, dens = [], []
        for i, h in enumerate(heads):
            sh = s[i * r:(i + 1) * r] + bias
            sink = sinks_ref[h]
            m = jnp.maximum(jnp.max(sh, axis=-1, keepdims=True), sink)
            p = jnp.exp(sh - m)
            dens.append(jnp.sum(p, axis=-1, keepdims=True) + jnp.exp(sink - m))
            ps.append(p.astype(jnp.bfloat16))
        pcat = jnp.concatenate(ps, axis=0)
        o = (lax.dot_general(pcat, vv, (((1,), (1,)), ((), ())), preferred_element_type=jnp.float32) if transposed
             else jnp.dot(pcat, vv, preferred_element_type=jnp.float32))
        for i, h in enumerate(heads):
            oh = o[i * r:(i + 1) * r] / dens[i]
            ms = jnp.sum(jnp.where(half, oh * oh, 0.0), axis=-1, keepdims=True) * (1.0 / HEAD_DIM)
            outs.append((h, oh * lax.rsqrt(ms + RMS_EPS)))
    outs.sort(key=lambda t: t[0])
    return [o for _, o in outs]


def _merge_heads(head_outs, lo):
    pairs = [jnp.where(lo, head_outs[2 * j], head_outs[2 * j + 1]) for j in range(4)]
    return jnp.concatenate(pairs, axis=1)


def _route(h2b, wr_ref, br_ref, tri_ref, cnt_ref, ri_ref, rf_ref):
    t = h2b.shape[0]
    lg = lax.dot_general(wr_ref[...], h2b, (((1,), (1,)), ((), ())), preferred_element_type=jnp.float32)
    lg = lg + br_ref[...][:, 0:1]
    r8 = lax.broadcasted_iota(jnp.int32, (8, t), 0)
    lgrp = jnp.where(r8 < N_GROUPS, lg[0:8], -jnp.inf)
    gmax = jnp.max(lgrp, axis=0, keepdims=True)
    grp = jnp.min(jnp.where(lgrp == gmax, r8, 8), axis=0, keepdims=True)
    pg_sel = 1.0 / jnp.sum(jnp.exp(lgrp - gmax), axis=0, keepdims=True)
    r32 = lax.broadcasted_iota(jnp.int32, (N_EXPERTS, t), 0)
    le = jnp.where((r32 // EXPERTS_PER_GROUP) == grp, lg[8:8 + N_EXPERTS], -jnp.inf)
    v1 = jnp.max(le, axis=0, keepdims=True)
    i1 = jnp.min(jnp.where(le == v1, r32, N_EXPERTS), axis=0, keepdims=True)
    le2 = jnp.where(r32 == i1, -jnp.inf, le)
    v2 = jnp.max(le2, axis=0, keepdims=True)
    i2 = jnp.min(jnp.where(le2 == v2, r32, N_EXPERTS), axis=0, keepdims=True)
    e21 = jnp.exp(v2 - v1)
    gate1 = pg_sel / (1.0 + e21)
    gate2 = pg_sel * e21 / (1.0 + e21)
    hot1 = r32 == i1
    hot2 = r32 == i2
    onehot = jnp.where(jnp.logical_or(hot1, hot2), 1.0, 0.0)
    before = jnp.dot(onehot.astype(jnp.bfloat16), tri_ref[...], preferred_element_type=jnp.float32)
    pos = before + cnt_ref[...][:, 0:1]
    rank1 = jnp.sum(jnp.where(hot1, pos, 0.0), axis=0, keepdims=True)
    rank2 = jnp.sum(jnp.where(hot2, pos, 0.0), axis=0, keepdims=True)
    cnt_ref[...] = cnt_ref[...] + jnp.sum(onehot, axis=1, keepdims=True)
    zi = jnp.zeros((4, t), jnp.int32)
    ri_ref[...] = jnp.concatenate([i1, i2, rank1.astype(jnp.int32), rank2.astype(jnp.int32), zi], axis=0)
    rf_ref[...] = jnp.concatenate([gate1, gate2, jnp.zeros((6, t), jnp.float32)], axis=0)


def _pack_bf16_pairs(xb):
    w = xb.shape[1] // 2
    bits = pltpu.bitcast(xb.astype(jnp.float32), jnp.uint32)
    return (bits[:, :w] >> 16) | (bits[:, w:] & jnp.uint32(0xFFFF0000))


def _unpack_pairs_f32(words):
    lo = pltpu.bitcast(words << 16, jnp.float32)
    hi = pltpu.bitcast(words & jnp.uint32(0xFFFF0000), jnp.float32)
    return jnp.concatenate([lo, hi], axis=1)


def _unpack_bf16_pairs(words):
    return _unpack_pairs_f32(words).astype(jnp.bfloat16)


def _split_proj(proj):
    a = ATTN_DIM
    q = proj[:, :a]
    k = proj[:, a:a + KV_DIM]
    v = proj[:, a + KV_DIM:a + 2 * KV_DIM]
    c0 = a + 2 * KV_DIM
    h_conv = proj[:, c0:c0 + CONV_DIM]
    gate_b = proj[:, c0 + CONV_DIM:c0 + 2 * CONV_DIM]
    gate_c = proj[:, c0 + 2 * CONV_DIM:c0 + 3 * CONV_DIM]
    return q, k, v, gate_c * h_conv, gate_b


def _mixer_tail(x, attn_n, conv_n, mod, wout_ref, fnorm_ref, wr_ref, br_ref, tri_ref, cnt_ref,
                x1_ref, h2_ref, ri_ref, rf_ref):
    gate_a, shift_f, scale_f = mod
    cat = jnp.concatenate([attn_n.astype(jnp.bfloat16), conv_n.astype(jnp.bfloat16)], axis=1)
    mix = jnp.dot(cat, wout_ref[...], preferred_element_type=jnp.float32)
    x1 = x + gate_a * mix
    x1_ref[...] = x1
    h2 = _rms(x1, fnorm_ref[...] * (1.0 + scale_f)) + shift_f
    h2b = h2.astype(jnp.bfloat16)
    h2_ref[...] = _pack_bf16_pairs(h2b)
    _route(h2b, wr_ref, br_ref, tri_ref, cnt_ref, ri_ref, rf_ref)


def _adaln_kernel(c_ref, w_ref, b_ref, o_ref):
    c = c_ref[...]
    a = (c * jax.nn.sigmoid(c)).astype(jnp.bfloat16)
    o_ref[...] = jnp.dot(a, w_ref[...].astype(jnp.bfloat16), preferred_element_type=jnp.float32) + b_ref[...]


def _adaln(c, w_mod, b_mod):
    n = c.shape[0]
    tn = 1536
    return pl.pallas_call(
        _adaln_kernel,
        grid=(w_mod.shape[1] // tn,),
        in_specs=[pl.BlockSpec((n, D_MODEL), lambda j: (0, 0)),
                  pl.BlockSpec((D_MODEL, tn), lambda j: (0, j)),
                  pl.BlockSpec((1, tn), lambda j: (0, j))],
        out_specs=pl.BlockSpec((n, tn), lambda j: (0, j)),
        out_shape=jax.ShapeDtypeStruct((n, w_mod.shape[1]), jnp.float32),
        compiler_params=pltpu.CompilerParams(dimension_semantics=("arbitrary",), vmem_limit_bytes=VMEM_LIMIT),
        name="adaln",
    )(c, w_mod, b_mod.reshape(1, -1))


def _prompt_mixer_kernel(sinks_ref, x_ref, mod_ref, anorm_ref, fnorm_ref, win_ref, cos_ref, sin_ref, bias_ref,
                         convw_ref, gna_ref, gnc_ref, wout_ref, wr_ref, br_ref, tri_ref,
                         x1_ref, h2_ref, ri_ref, rf_ref, cnt_out_ref, knew_ref, vnew_ref, unew_ref,
                         q_s, k_s, ksw_s, v_s, vsw_s, u_s, att_s, cnt_s):
    b = pl.program_id(0)
    t = pl.program_id(1)
    tm = MIXER_TILE
    nblk = tm // WINDOW

    @pl.when(jnp.logical_and(b == 0, t == 0))
    def _():
        cnt_s[...] = jnp.zeros_like(cnt_s)

    @pl.when(t == 0)
    def _():
        z = jnp.zeros((WINDOW, KV_DIM), jnp.bfloat16)
        k_s[0:WINDOW, :] = z
        ksw_s[0:WINDOW, :] = z
        v_s[0:WINDOW, :] = z
        vsw_s[0:WINDOW, :] = z
        u_s[0:8, :] = jnp.zeros((8, CONV_DIM), jnp.float32)

    x = x_ref[...]
    mod = mod_ref[...]
    h = _rms(x, anorm_ref[...] * (1.0 + mod[1:2])) + mod[0:1]
    proj = jnp.dot(h.astype(jnp.bfloat16), win_ref[...], preferred_element_type=jnp.float32)
    q, k, v, u, gate_b = _split_proj(proj)
    cos = cos_ref[...]
    sin = sin_ref[...]
    q = _rope(q, cos, sin) * (HEAD_DIM ** -0.5)
    k = _rope(k, cos, sin)
    q_s[...] = q.astype(jnp.bfloat16)
    k_s[WINDOW:, :] = k.astype(jnp.bfloat16)
    ksw_s[WINDOW:, :] = _swap_halves(k).astype(jnp.bfloat16)
    v_s[WINDOW:, :] = v.astype(jnp.bfloat16)
    vsw_s[WINDOW:, :] = _swap_halves(v).astype(jnp.bfloat16)
    knew_ref[...] = k[tm - WINDOW:, :]
    vnew_ref[...] = v[tm - WINDOW:, :]

    u_s[8:, :] = u
    unew_ref[...] = u[tm - 8:, :]
    cw = convw_ref[...]
    conv = (u_s[6:6 + tm, :] * cw[0:1] + u_s[7:7 + tm, :] * cw[1:2] + u * cw[2:3]) * gate_b
    u_s[0:8, :] = u[tm - 8:, :]
    conv_n = _group_norm64(conv, gnc_ref[...])

    lo = _half_lane_mask(WINDOW)
    first = t == 0

    def block(i, carry):
        r0 = pl.multiple_of(i * WINDOW, WINDOW)
        bias = bias_ref[jnp.where(jnp.logical_and(first, i == 0), 0, 1)]
        qb = q_s[pl.ds(r0, WINDOW), :]
        keys = k_s[pl.ds(r0, 2 * WINDOW), :]
        keys_sw = ksw_s[pl.ds(r0, 2 * WINDOW), :]
        vals = v_s[pl.ds(r0, 2 * WINDOW), :]
        vals_sw = vsw_s[pl.ds(r0, 2 * WINDOW), :]
        outs = []
        for g in range(N_KV_HEADS):
            outs += _attend(qb, keys, keys_sw, vals, vals_sw, bias, sinks_ref, g)
        att_s[pl.ds(r0, WINDOW), :] = _merge_heads(outs, lo)
        return carry

    lax.fori_loop(0, nblk, block, 0, unroll=True)
    k_s[0:WINDOW, :] = k_s[tm:tm + WINDOW, :]
    ksw_s[0:WINDOW, :] = ksw_s[tm:tm + WINDOW, :]
    v_s[0:WINDOW, :] = v_s[tm:tm + WINDOW, :]
    vsw_s[0:WINDOW, :] = vsw_s[tm:tm + WINDOW, :]

    attn_n = att_s[...] * gna_ref[...]
    _mixer_tail(x, attn_n, conv_n, (mod[2:3], mod[3:4], mod[4:5]), wout_ref, fnorm_ref, wr_ref, br_ref, tri_ref,
                cnt_s, x1_ref, h2_ref, ri_ref, rf_ref)
    cnt_out_ref[...] = cnt_s[...]


def _prompt_mixer(x, mod, sinks, consts):
    bsz, seq, _ = x.shape
    tm = MIXER_TILE
    nt = seq // tm
    full = lambda shape: pl.BlockSpec(shape, lambda b, t, s: (0,) * len(shape))
    in_specs = [
        pl.BlockSpec((None, tm, D_MODEL), lambda b, t, s: (b, t, 0)),
        pl.BlockSpec((None, 6, D_MODEL), lambda b, t, s: (b, 0, 0)),
        full((1, D_MODEL)), full((1, D_MODEL)),
        full((D_MODEL, IN_PROJ_DIM)),
        pl.BlockSpec((tm, LANES), lambda b, t, s: (t, 0)),
        pl.BlockSpec((tm, LANES), lambda b, t, s: (t, 0)),
        full((2, WINDOW, 2 * WINDOW)),
        full((3, CONV_DIM)), full((1, ATTN_DIM)), full((1, CONV_DIM)),
        full((D_MODEL, D_MODEL)),
        full((ROUTER_ROWS, D_MODEL)), full((ROUTER_ROWS, LANES)),
        full((tm, tm)),
    ]
    out_shape = [
        jax.ShapeDtypeStruct((bsz * seq, D_MODEL), jnp.float32),
        jax.ShapeDtypeStruct((bsz * seq, D_MODEL // 2), jnp.uint32),
        jax.ShapeDtypeStruct((bsz * nt, 8, tm), jnp.int32),
        jax.ShapeDtypeStruct((bsz * nt, 8, tm), jnp.float32),
        jax.ShapeDtypeStruct((N_EXPERTS, LANES), jnp.float32),
        jax.ShapeDtypeStruct((bsz, WINDOW, KV_DIM), jnp.float32),
        jax.ShapeDtypeStruct((bsz, WINDOW, KV_DIM), jnp.float32),
        jax.ShapeDtypeStruct((bsz, 8, CONV_DIM), jnp.float32),
    ]
    out_specs = [
        pl.BlockSpec((tm, D_MODEL), lambda b, t, s: (b * nt + t, 0)),
        pl.BlockSpec((tm, D_MODEL // 2), lambda b, t, s: (b * nt + t, 0)),
        pl.BlockSpec((None, 8, tm), lambda b, t, s: (b * nt + t, 0, 0)),
        pl.BlockSpec((None, 8, tm), lambda b, t, s: (b * nt + t, 0, 0)),
        pl.BlockSpec((N_EXPERTS, LANES), lambda b, t, s: (0, 0)),
        pl.BlockSpec((None, WINDOW, KV_DIM), lambda b, t, s: (b, 0, 0)),
        pl.BlockSpec((None, WINDOW, KV_DIM), lambda b, t, s: (b, 0, 0)),
        pl.BlockSpec((None, 8, CONV_DIM), lambda b, t, s: (b, 0, 0)),
    ]
    scratch = [
        pltpu.VMEM((tm, ATTN_DIM), jnp.bfloat16),
        pltpu.VMEM((tm + WINDOW, KV_DIM), jnp.bfloat16), pltpu.VMEM((tm + WINDOW, KV_DIM), jnp.bfloat16),
        pltpu.VMEM((tm + WINDOW, KV_DIM), jnp.bfloat16), pltpu.VMEM((tm + WINDOW, KV_DIM), jnp.bfloat16),
        pltpu.VMEM((tm + 8, CONV_DIM), jnp.float32),
        pltpu.VMEM((tm, ATTN_DIM), jnp.float32),
        pltpu.VMEM((N_EXPERTS, LANES), jnp.float32),
    ]
    return pl.pallas_call(
        _prompt_mixer_kernel,
        grid_spec=pltpu.PrefetchScalarGridSpec(
            num_scalar_prefetch=1, grid=(bsz, nt), in_specs=in_specs, out_specs=out_specs, scratch_shapes=scratch),
        out_shape=out_shape,
        compiler_params=pltpu.CompilerParams(dimension_semantics=("arbitrary", "arbitrary"),
                                             vmem_limit_bytes=VMEM_LIMIT),
        name="prompt_mixer",
    )(sinks, x, mod, consts["anorm"], consts["fnorm"], consts["w_in"], consts["cos_p"], consts["sin_p"],
      consts["bias_p"], consts["conv_w"], consts["gn_attn"], consts["gn_conv"], consts["w_out"],
      consts["w_r"], consts["b_r"], consts["tri"])


def _sample_mixer_kernel(sinks_ref, x_ref, mod_ref, anorm_ref, fnorm_ref, win_ref, cos_ref, sin_ref, bias_ref,
                         convw_ref, gna_ref, gnc_ref, wout_ref, wr_ref, br_ref, tri_ref,
                         ck_ref, cv_ref, e1_ref, e2_ref, cnt_in_ref,
                         x1_ref, h2_ref, ri_ref, rf_ref, cnt_out_ref, knew_ref, vnew_ref, unew_ref,
                         q_s, k_s, v_s, att_s, cnt_s, *, dec):
    n = x_ref.shape[0]
    cb = SAMPLE_CHUNK
    rows = cb * dec

    @pl.when(pl.program_id(0) == 0)
    def _():
        cnt_s[...] = cnt_in_ref[...]

    x = x_ref[...]
    mod = jnp.repeat(mod_ref[...], dec, axis=0)
    md = lambda j: mod[:, j * D_MODEL:(j + 1) * D_MODEL]
    h = _rms(x, anorm_ref[...]) * (1.0 + md(1)) + md(0)
    proj = jnp.dot(h.astype(jnp.bfloat16), win_ref[...], preferred_element_type=jnp.float32)
    q, k, v, u, gate_b = _split_proj(proj)
    cos = cos_ref[...]
    sin = sin_ref[...]
    q = _rope(q, cos, sin) * (HEAD_DIM ** -0.5)
    k = _rope(k, cos, sin)
    q_s[...] = q.astype(jnp.bfloat16)
    k_s[...] = k
    v_s[...] = v
    unew_ref[...] = u

    pos = lax.broadcasted_iota(jnp.int32, (n, CONV_DIM), 0) % dec
    um1 = jnp.where(pos >= 1, pltpu.roll(u, 1, 0), 0.0) + e1_ref[...]
    um2 = jnp.where(pos >= 2, pltpu.roll(u, 2, 0), 0.0) + e2_ref[...]
    cw = convw_ref[...]
    conv = (um2 * cw[0:1] + um1 * cw[1:2] + u * cw[2:3]) * gate_b
    conv_n = _group_norm64(conv, gnc_ref[...])

    lo = _half_lane_mask(rows)
    bias = bias_ref[...]

    half = KV_DIM // 2
    lane = lax.broadcasted_iota(jnp.int32, (KV_DIM, WINDOW), 1)
    swap_rows = lambda t: jnp.concatenate([t[half:], t[:half]], axis=0)

    for c in range(n // rows):
        r0 = c * rows
        qb = q_s[r0:r0 + rows, :]
        cats = []
        for ref, new_rows, out_ref in ((ck_ref, k_s[r0:r0 + rows, :], knew_ref), (cv_ref, v_s[r0:r0 + rows, :], vnew_ref)):
            tiles = [ref[(c * cb + b) * KV_DIM:(c * cb + b + 1) * KV_DIM, :] for b in range(cb)]
            new_t = jnp.concatenate([jnp.transpose(new_rows), jnp.zeros((KV_DIM, WINDOW - rows), jnp.float32)], axis=1)
            cat = jnp.concatenate(tiles + [new_t], axis=1)
            cats.append((cat.astype(jnp.bfloat16), swap_rows(cat).astype(jnp.bfloat16)))
            for b in range(cb):
                kept = pltpu.roll(tiles[b], WINDOW - dec, 1)
                fresh = pltpu.roll(new_t, (WINDOW - dec - b * dec) % WINDOW, 1)
                out_ref[(c * cb + b) * KV_DIM:(c * cb + b + 1) * KV_DIM, :] = jnp.where(lane < WINDOW - dec, kept, fresh)
        (keys, keys_sw), (vals, vals_sw) = cats
        outs = []
        for g in range(N_KV_HEADS):
            outs += _attend(qb, keys, keys_sw, vals, vals_sw, bias, sinks_ref, g, transposed=True)
        att_s[r0:r0 + rows, :] = _merge_heads(outs, lo)

    attn_n = att_s[...] * gna_ref[...]
    _mixer_tail(x, attn_n, conv_n, (md(2), md(3), md(4)), wout_ref, fnorm_ref, wr_ref, br_ref, tri_ref,
                cnt_s, x1_ref, h2_ref, ri_ref, rf_ref)
    cnt_out_ref[...] = cnt_s[...]


def _sample_mixer(x, mod, sinks, consts, ck, cv, e1, e2, cnt_in, dec):
    n_s = x.shape[0]
    ts = SAMPLE_TILE
    seqs = ts // dec
    full = lambda a: pl.BlockSpec(a.shape, lambda i, s: (0,) * a.ndim)
    rows = lambda width: pl.BlockSpec((ts, width), lambda i, s: (i, 0))
    in_specs = [
        rows(D_MODEL), pl.BlockSpec((seqs, 6 * D_MODEL), lambda i, s: (i, 0)),
        full(consts["anorm"]), full(consts["fnorm"]), full(consts["w_in"]),
        rows(LANES), rows(LANES), full(consts["bias_s"]),
        full(consts["conv_w"]), full(consts["gn_attn"]), full(consts["gn_conv"]), full(consts["w_out"]),
        full(consts["w_r"]), full(consts["b_r"]),
        pl.BlockSpec((ts, ts), lambda i, s: (0, 0)),
        pl.BlockSpec((seqs * KV_DIM, WINDOW), lambda i, s: (i, 0)),
        pl.BlockSpec((seqs * KV_DIM, WINDOW), lambda i, s: (i, 0)),
        rows(CONV_DIM), rows(CONV_DIM), full(cnt_in),
    ]
    args = [x, mod, consts["anorm"], consts["fnorm"], consts["w_in"], consts["cos_s"], consts["sin_s"],
            consts["bias_s"], consts["conv_w"], consts["gn_attn"], consts["gn_conv"], consts["w_out"],
            consts["w_r"], consts["b_r"], consts["tri"], ck, cv, e1, e2, cnt_in]
    nt = n_s // ts
    out_shape = [
        jax.ShapeDtypeStruct((n_s, D_MODEL), jnp.float32),
        jax.ShapeDtypeStruct((n_s, D_MODEL // 2), jnp.uint32),
        jax.ShapeDtypeStruct((nt, 8, ts), jnp.int32),
        jax.ShapeDtypeStruct((nt, 8, ts), jnp.float32),
        jax.ShapeDtypeStruct((N_EXPERTS, LANES), jnp.float32),
        jax.ShapeDtypeStruct(ck.shape, jnp.float32),
        jax.ShapeDtypeStruct(cv.shape, jnp.float32),
        jax.ShapeDtypeStruct((n_s, CONV_DIM), jnp.float32),
    ]
    out_specs = [
        rows(D_MODEL),
        rows(D_MODEL // 2),
        pl.BlockSpec((None, 8, ts), lambda i, s: (i, 0, 0)),
        pl.BlockSpec((None, 8, ts), lambda i, s: (i, 0, 0)),
        pl.BlockSpec((N_EXPERTS, LANES), lambda i, s: (0, 0)),
        pl.BlockSpec((seqs * KV_DIM, WINDOW), lambda i, s: (i, 0)),
        pl.BlockSpec((seqs * KV_DIM, WINDOW), lambda i, s: (i, 0)),
        rows(CONV_DIM),
    ]
    scratch = [
        pltpu.VMEM((ts, ATTN_DIM), jnp.bfloat16),
        pltpu.VMEM((ts, KV_DIM), jnp.float32), pltpu.VMEM((ts, KV_DIM), jnp.float32),
        pltpu.VMEM((ts, ATTN_DIM), jnp.float32),
        pltpu.VMEM((N_EXPERTS, LANES), jnp.float32),
    ]
    return pl.pallas_call(
        functools.partial(_sample_mixer_kernel, dec=dec),
        grid_spec=pltpu.PrefetchScalarGridSpec(
            num_scalar_prefetch=1, grid=(nt,), in_specs=in_specs, out_specs=out_specs, scratch_shapes=scratch),
        out_shape=out_shape,
        compiler_params=pltpu.CompilerParams(dimension_semantics=("arbitrary",), vmem_limit_bytes=VMEM_LIMIT),
        name="sample_mixer",
    )(sinks, *args)


def _dispatch(idx, h2, n_rows):
    mesh = plsc.VectorSubcoreMesh(core_axis_name="core", subcore_axis_name="subcore")
    rows, window = SC_ROWS, SC_INDEX_WINDOW
    sub = window // rows
    n_tok, width = h2.shape
    assert n_tok % rows == 0 and (2 * n_tok) % window == 0

    moved = 3 * n_tok * width * h2.dtype.itemsize
    @pl.kernel(out_type=jax.ShapeDtypeStruct((n_rows, width), h2.dtype), mesh=mesh, scratch_types=[],
               cost_estimate=pl.CostEstimate(flops=0, transcendentals=0, bytes_accessed=moved))
    def scatter(h_hbm, i_hbm, xs_hbm):
        def body(indices, h_vmem, i_vmem):
            pltpu.sync_copy(h_vmem, xs_hbm.at[i_vmem.at[0, pl.ds(indices[1] * rows, rows)]])

        pltpu.emit_pipeline(
            body, grid=(2 * n_tok // window, sub),
            in_specs=[pl.BlockSpec((rows, width), lambda i, j: ((i * sub + j) % (n_tok // rows), 0)),
                      pl.BlockSpec((1, window), lambda i, j: (0, i))],
            out_specs=[], core_axis_name=("core", "subcore"),
            dimension_semantics=(pltpu.PARALLEL, pltpu.ARBITRARY), _explicit_indices=True,
        )(h_hbm, i_hbm)

    return scatter(h2, idx)


def _scatter_rows_kernel(dest_ref, h_ref, xs_in_ref, xs_ref, sem):
    del xs_in_ref
    n_tok = h_ref.shape[0]

    def issue(i, carry):
        for k in range(2):
            d = dest_ref[k * n_tok + i]
            pltpu.make_async_copy(h_ref.at[pl.ds(i, 1)], xs_ref.at[pl.ds(d, 1)], sem.at[0]).start(priority=k)
        return carry

    lax.fori_loop(0, n_tok, issue, 0, unroll=8)
    for _ in range(2):
        pltpu.make_async_copy(h_ref, xs_ref.at[pl.ds(0, n_tok)], sem.at[0]).wait()


def _scatter_rows(dest, h2, xs):
    return pl.pallas_call(
        _scatter_rows_kernel,
        grid_spec=pltpu.PrefetchScalarGridSpec(
            num_scalar_prefetch=1, grid=(1,),
            in_specs=[pl.BlockSpec(h2.shape, lambda i, d: (0, 0)), pl.BlockSpec(memory_space=pl.ANY)],
            out_specs=pl.BlockSpec(memory_space=pl.ANY),
            scratch_shapes=[pltpu.SemaphoreType.DMA((1,))]),
        out_shape=jax.ShapeDtypeStruct(xs.shape, xs.dtype),
        input_output_aliases={2: 0},
        compiler_params=pltpu.CompilerParams(dimension_semantics=("arbitrary",)),
        name="scatter_rows",
    )(dest, h2, xs)


def _experts_kernel(be_ref, nu_ref, nv_ref, nxt_ref, par_ref, phys_ref, xs_ref, wg_ref, wu_ref, wd_ref, y_ref,
                    wg_s, wu_s, wd_s, wg_f, wu_f, wd_f, sem):
    del phys_ref
    j = pl.program_id(0)
    used = j < nu_ref[0]
    changed = jnp.logical_or(j == 0, be_ref[j] != be_ref[jnp.maximum(j - 1, 0)])

    def weight_copies(e, slot):
        return [pltpu.make_async_copy(src.at[e], dst.at[slot], sem.at[slot])
                for src, dst in ((wg_ref, wg_f), (wu_ref, wu_f), (wd_ref, wd_f))]

    @pl.when(jnp.logical_and(used, changed))
    def _():
        slot = par_ref[j]

        @pl.when(j == 0)
        def _():
            for c in weight_copies(be_ref[0], slot):
                c.start()

        for c in weight_copies(be_ref[j], slot):
            c.wait()

        @pl.when(nxt_ref[j] >= 0)
        def _():
            for c in weight_copies(nxt_ref[j], 1 - slot):
                c.start()

        wg_s[...] = wg_f[slot].astype(jnp.bfloat16)
        wu_s[...] = wu_f[slot].astype(jnp.bfloat16)
        wd_s[...] = wd_f[slot].astype(jnp.bfloat16)

    @pl.when(used)
    def _():
        hb = xs_ref.shape[0] // 2
        row = lax.broadcasted_iota(jnp.int32, (hb, 1), 0)

        def load(h):
            live = row + h * hb < nv_ref[j]
            return _unpack_bf16_pairs(jnp.where(live, xs_ref[h * hb:(h + 1) * hb, :], jnp.uint32(0)))

        def swiglu(g, u):
            return (g * jax.nn.sigmoid(g) * u).astype(jnp.bfloat16)

        x0 = load(0)
        g0 = jnp.dot(x0, wg_s[...], preferred_element_type=jnp.float32)
        u0 = jnp.dot(x0, wu_s[...], preferred_element_type=jnp.float32)
        x1 = load(1)
        g1 = jnp.dot(x1, wg_s[...], preferred_element_type=jnp.float32)
        a0 = swiglu(g0, u0)
        u1 = jnp.dot(x1, wu_s[...], preferred_element_type=jnp.float32)
        y0 = jnp.dot(a0, wd_s[...], preferred_element_type=jnp.float32)
        a1 = swiglu(g1, u1)
        y_ref[0:hb, :] = _pack_bf16_pairs(y0.astype(jnp.bfloat16))
        y1 = jnp.dot(a1, wd_s[...], preferred_element_type=jnp.float32)
        y_ref[hb:, :] = _pack_bf16_pairs(y1.astype(jnp.bfloat16))


def _experts(block_e, n_used, n_valid, next_e, parity, phys, xs, w_gate, w_up, w_down):
    bm = EXPERT_BLOCK
    n_blocks = xs.shape[0] // bm
    hbm = pl.BlockSpec(memory_space=pl.ANY)
    return pl.pallas_call(
        _experts_kernel,
        grid_spec=pltpu.PrefetchScalarGridSpec(
            num_scalar_prefetch=6, grid=(n_blocks,),
            in_specs=[pl.BlockSpec((bm, D_MODEL // 2), lambda j, be, nu, nv, nx, pa, ph: (ph[j], 0)),
                      hbm, hbm, hbm],
            out_specs=pl.BlockSpec((bm, D_MODEL // 2), lambda j, be, nu, nv, nx, pa, ph: (ph[j], 0)),
            scratch_shapes=[pltpu.VMEM((D_MODEL, EXPERT_HIDDEN), jnp.bfloat16),
                            pltpu.VMEM((D_MODEL, EXPERT_HIDDEN), jnp.bfloat16),
                            pltpu.VMEM((EXPERT_HIDDEN, D_MODEL), jnp.bfloat16),
                            pltpu.VMEM((2, D_MODEL, EXPERT_HIDDEN), jnp.float32),
                            pltpu.VMEM((2, D_MODEL, EXPERT_HIDDEN), jnp.float32),
                            pltpu.VMEM((2, EXPERT_HIDDEN, D_MODEL), jnp.float32),
                            pltpu.SemaphoreType.DMA((2,))]),
        out_shape=jax.ShapeDtypeStruct((xs.shape[0], D_MODEL // 2), jnp.uint32),
        compiler_params=pltpu.CompilerParams(dimension_semantics=("arbitrary",), vmem_limit_bytes=VMEM_LIMIT),
        name="experts",
    )(block_e, n_used, n_valid, next_e, parity, phys, xs, w_gate, w_up, w_down)


def _gather_rows(idx, yb):
    mesh = plsc.VectorSubcoreMesh(core_axis_name="core", subcore_axis_name="subcore")
    rows, window = SC_ROWS, SC_INDEX_WINDOW
    sub = window // rows
    n = idx.shape[1]
    width = yb.shape[1]
    assert n % window == 0

    @pl.kernel(out_type=jax.ShapeDtypeStruct((n, width), yb.dtype), mesh=mesh, scratch_types=[])
    def gather(y_hbm, i_hbm, o_hbm):
        def body(indices, i_vmem, o_vmem):
            pltpu.sync_copy(y_hbm.at[i_vmem.at[0, pl.ds(indices[1] * rows, rows)]], o_vmem)

        pltpu.emit_pipeline(
            body, grid=(n // window, sub),
            in_specs=[pl.BlockSpec((1, window), lambda i, j: (0, i))],
            out_specs=[pl.BlockSpec((rows, width), lambda i, j: (i * sub + j, 0))],
            core_axis_name=("core", "subcore"),
            dimension_semantics=(pltpu.PARALLEL, pltpu.ARBITRARY), _explicit_indices=True,
        )(i_hbm, o_hbm)

    return gather(yb, idx)


def _combine_kernel(x1_ref, gate_ref, rf_ref, fin_ref, y0_ref, y1_ref, *out_refs, per_token_gate):
    o_ref = out_refs[-1]
    gates = jnp.transpose(rf_ref[...])
    moe = _unpack_pairs_f32(y0_ref[...]) * gates[:, 0:1] + _unpack_pairs_f32(y1_ref[...]) * gates[:, 1:2]
    gate_f = gate_ref[...] if per_token_gate else gate_ref[...][5:6]
    o_ref[...] = _rms(x1_ref[...] + gate_f * moe, fin_ref[...])


def _combine(x1, gate, rf, final_norm, ys, out_prev, first_tile, n_tiles, *, per_token_gate):
    tc = COMBINE_TILE
    gate_spec = (pl.BlockSpec((tc, D_MODEL), lambda i: (first_tile + i, 0)) if per_token_gate
                 else pl.BlockSpec((6, D_MODEL), lambda i: (0, 0)))
    args = [x1, gate, rf, final_norm, ys, ys]
    in_specs = [pl.BlockSpec((tc, D_MODEL), lambda i: (first_tile + i, 0)),
                gate_spec,
                pl.BlockSpec((8, tc), lambda i: (0, first_tile + i)),
                pl.BlockSpec((1, D_MODEL), lambda i: (0, 0)),
                pl.BlockSpec((tc, D_MODEL // 2), lambda i: (i, 0)),
                pl.BlockSpec((tc, D_MODEL // 2), lambda i: (n_tiles + i, 0))]
    aliases = {}
    if out_prev is not None:
        args.append(out_prev)
        in_specs.append(pl.BlockSpec(memory_space=pl.ANY))
        aliases = {len(args) - 1: 0}
    return pl.pallas_call(
        functools.partial(_combine_kernel, per_token_gate=per_token_gate),
        grid=(n_tiles,), in_specs=in_specs,
        out_specs=pl.BlockSpec((tc, D_MODEL), lambda i: (first_tile + i, 0)),
        out_shape=jax.ShapeDtypeStruct(x1.shape, jnp.float32),
        input_output_aliases=aliases,
        compiler_params=pltpu.CompilerParams(dimension_semantics=("arbitrary",), vmem_limit_bytes=VMEM_LIMIT),
        name="combine",
    )(*args)


def _rope_tables(pos):
    half = HEAD_DIM // 2
    inv_freq = np.power(np.float32(ROPE_THETA), -(np.arange(half, dtype=np.float32) / np.float32(half)))
    ang = (pos.astype(np.float32)[:, None] * inv_freq[None, :].astype(np.float32)).astype(np.float64)
    cos, sin = np.cos(ang).astype(np.float32), np.sin(ang).astype(np.float32)
    cos = np.concatenate([cos, cos, cos, cos], axis=1)
    sin = np.concatenate([-sin, sin, -sin, sin], axis=1)
    return cos, sin


def _prompt_bias():
    r = np.arange(WINDOW)[:, None]
    j = np.arange(2 * WINDOW)[None, :]
    band = (j > r) & (j <= r + WINDOW)
    later = np.where(band, 0.0, NEG).astype(np.float32)
    first = np.where(band & (j >= WINDOW), 0.0, NEG).astype(np.float32)
    return jnp.asarray(np.stack([first, later]))


def _sample_bias(dec):
    cb = SAMPLE_CHUNK
    qb = np.repeat(np.arange(cb), dec)[:, None]
    qi = np.tile(np.arange(dec), cb)[:, None]
    kb = np.concatenate([np.repeat(np.arange(cb), WINDOW), np.repeat(np.arange(cb), dec)])[None, :]
    kj = np.concatenate([np.tile(np.arange(WINDOW), cb), np.tile(np.arange(dec), cb)])[None, :]
    is_new = (np.arange(cb * WINDOW + cb * dec) >= cb * WINDOW)[None, :]
    ok = (kb == qb) & np.where(is_new, kj <= qi, kj > qi)
    ok = np.pad(ok, ((0, 0), (0, SAMPLE_KEYS - ok.shape[1])))
    return jnp.asarray(np.where(ok, 0.0, NEG).astype(np.float32))


def kernel(x_prompt, x_sample, cache_k, cache_v, state_conv, c_prompt, c_sample, attn_norm, ffn_norm, w_mod, b_mod, w_in, conv_w, attn_sinks, out_norm_attn, out_norm_conv, w_out, w_group, b_group, w_expert, b_expert, w_gate, w_up, w_down, final_norm):
    assert attn_norm.shape[0] == 1, "one layer"
    bsz, seq, _ = x_prompt.shape
    nseq, dec, _ = x_sample.shape
    m_prompt = bsz * seq
    n_s = nseq * dec
    m_total = m_prompt + n_s
    tm = MIXER_TILE
    assert seq % tm == 0 and n_s % SAMPLE_TILE == 0 and SAMPLE_TILE % (SAMPLE_CHUNK * dec) == 0
    assert cache_k.shape[2] == WINDOW and m_prompt % SAMPLE_TILE == 0 and m_total % COMBINE_TILE == 0
    bf = jnp.bfloat16

    w_r = jnp.zeros((ROUTER_ROWS, D_MODEL), jnp.float32)
    w_r = w_r.at[0:N_GROUPS].set(w_group[0].T).at[8:8 + N_EXPERTS].set(w_expert[0].T).astype(bf)
    b_r = jnp.zeros((ROUTER_ROWS,), jnp.float32).at[0:N_GROUPS].set(b_group[0]).at[8:8 + N_EXPERTS].set(b_expert[0])
    b_r = jnp.broadcast_to(b_r[:, None], (ROUTER_ROWS, LANES))
    cos_p, sin_p = _rope_tables(np.arange(seq))
    cos_s, sin_s = _rope_tables(PAST_LEN + np.arange(dec))
    tri = jnp.asarray(np.triu(np.ones((tm, tm), np.float32), 1)).astype(bf)
    consts = dict(
        anorm=attn_norm[0][None], fnorm=ffn_norm[0][None], w_in=w_in[0].astype(bf),
        cos_p=jnp.asarray(cos_p), sin_p=jnp.asarray(sin_p),
        cos_s=jnp.asarray(np.tile(cos_s, (nseq, 1))), sin_s=jnp.asarray(np.tile(sin_s, (nseq, 1))),
        bias_p=_prompt_bias(), bias_s=_sample_bias(dec), conv_w=conv_w[0],
        gn_attn=out_norm_attn[0][None], gn_conv=out_norm_conv[0][None], w_out=w_out[0].astype(bf),
        w_r=w_r, b_r=b_r, tri=tri)
    sinks = attn_sinks[0]

    mod = _adaln(jnp.concatenate([c_prompt, c_sample], axis=0), w_mod[0], b_mod[0])
    mod_p = mod[:bsz].reshape(bsz, 6, D_MODEL)
    mod_s = mod[bsz:]

    bm = EXPERT_BLOCK
    eids = jnp.arange(N_EXPERTS, dtype=jnp.int32)
    lookup = lambda table, e: jnp.sum(jnp.where(e[..., None] == eids, table, 0), axis=-1)
    by_token = lambda a: jnp.transpose(a, (1, 0, 2)).reshape(8, -1)
    ceil_bm = lambda c: (c + bm - 1) // bm * bm
    x1_p, h2_p, ri_p, rf_p, cnt_p, k_p, v_p, u_p = _prompt_mixer(x_prompt, mod_p, sinks, consts)
    cnt_p = cnt_p[:, 0].astype(jnp.int32)
    padded_p = ceil_bm(cnt_p)
    pstart_p = jnp.cumsum(padded_p) - padded_p
    nb_p = -(-(2 * m_prompt + N_EXPERTS * (bm - 1)) // bm)
    nb_o = -(-(2 * n_s + N_EXPERTS * (bm - 1)) // bm)
    n_blocks = nb_p + nb_o
    ri_p = by_token(ri_p)
    dest_p = (lookup(pstart_p, ri_p[0:2]) + ri_p[2:4]).reshape(2 * m_prompt)

    st = state_conv[0]
    zrow = jnp.zeros((nseq, 1, CONV_DIM), jnp.float32)
    e1 = jnp.concatenate([st[:, 1:2]] + [zrow] * (dec - 1), axis=1).reshape(n_s, CONV_DIM)
    e2 = jnp.concatenate([st[:, 0:1], st[:, 1:2]] + [zrow] * (dec - 2), axis=1).reshape(n_s, CONV_DIM)
    dest_p, e1 = lax.optimization_barrier((dest_p, e1))
    xs = _dispatch(dest_p[None], h2_p, n_blocks * bm)
    ck = jnp.transpose(cache_k[0], (0, 2, 3, 1)).reshape(nseq * KV_DIM, WINDOW)
    cv = jnp.transpose(cache_v[0], (0, 2, 3, 1)).reshape(nseq * KV_DIM, WINDOW)
    x1_s, h2_s, ri_s, rf_s, cnt_s, k_s, v_s, u_s = _sample_mixer(
        x_sample.reshape(n_s, D_MODEL), mod_s, sinks, consts, ck, cv, e1, e2,
        jnp.zeros((N_EXPERTS, LANES), jnp.float32), dec)

    cnt_s = cnt_s[:, 0].astype(jnp.int32)
    fit = jnp.minimum(cnt_s, padded_p - cnt_p)
    over = cnt_s - fit
    padded_o = ceil_bm(over)
    ostart = nb_p * bm + jnp.cumsum(padded_o) - padded_o
    ri_s = by_token(ri_s)
    e_s, r_s = ri_s[0:2], ri_s[2:4]
    dest_s = jnp.where(r_s < lookup(fit, e_s), lookup(pstart_p + cnt_p, e_s) + r_s, lookup(ostart - fit, e_s) + r_s)
    dest_s = dest_s.reshape(2 * n_s)

    from_t = lambda a: jnp.transpose(a.reshape(nseq, N_KV_HEADS, HEAD_DIM, WINDOW), (0, 3, 1, 2))[None]
    new_k_s, new_v_s = from_t(k_s), from_t(v_s)
    xs = _scatter_rows(dest_s, h2_s, xs)

    np_e, no_e = padded_p // bm, padded_o // bm
    lend = jnp.cumsum(np_e + no_e)
    lstart = lend - np_e - no_e
    n_used = lend[-1].astype(jnp.int32)
    blk = jnp.arange(n_blocks, dtype=jnp.int32)
    present = np_e + no_e > 0
    last_e = jnp.max(jnp.where(present, eids, 0))
    block_e = jnp.sum((lend[None, :] <= blk[:, None]).astype(jnp.int32), axis=1)
    block_e = jnp.where(blk < n_used, jnp.minimum(block_e, N_EXPERTS - 1), last_e)
    of_block = lambda table: lookup(table, block_e)
    within = blk - of_block(lstart)
    in_prompt = within < of_block(np_e)
    phys = jnp.where(in_prompt, of_block(pstart_p) // bm + within, of_block(ostart) // bm + within - of_block(np_e))
    n_valid = jnp.where(in_prompt, of_block(cnt_p + fit) - within * bm, of_block(over) - (within - of_block(np_e)) * bm)
    n_valid = jnp.where(blk < n_used, jnp.clip(n_valid, 0, bm), 0).astype(jnp.int32)
    phys = jnp.where(blk < n_used, phys, jnp.sum(jnp.where(blk == n_used - 1, phys, 0))).astype(jnp.int32)
    later = jnp.logical_and(present[None, :], eids[None, :] > eids[:, None])
    next_of = jnp.min(jnp.where(later, eids[None, :], N_EXPERTS), axis=1)
    next_of = jnp.where(next_of == N_EXPERTS, -1, next_of)
    next_e = of_block(next_of).astype(jnp.int32)
    parity = of_block((jnp.cumsum(present.astype(jnp.int32)) - 1) % 2).astype(jnp.int32)
    yb = _experts(block_e, n_used.reshape(1), n_valid, next_e, parity, phys, xs, w_gate[0], w_up[0], w_down[0])

    fin = final_norm[None]
    tc = COMBINE_TILE
    rf_p, rf_s = by_token(rf_p), by_token(rf_s)
    y_p = None
    chunk = min(COMBINE_CHUNK, seq)
    assert seq % chunk == 0 and chunk % tc == 0
    for c in range(m_prompt // chunk):
        lo, hi = c * chunk, (c + 1) * chunk
        idx = jnp.concatenate([dest_p[lo:hi], dest_p[m_prompt + lo:m_prompt + hi]])[None]
        y_p = _combine(x1_p, mod_p[lo // seq], rf_p, fin, _gather_rows(idx, yb), y_p, lo // tc, chunk // tc,
                       per_token_gate=False)
    gate_s = jnp.repeat(mod_s[:, 5 * D_MODEL:], dec, axis=0)
    y_s = _combine(x1_s, gate_s, rf_s, fin, _gather_rows(dest_s[None], yb), None, 0, n_s // tc, per_token_gate=True)

    keep = WINDOW
    new_k_p = k_p.reshape(1, bsz, keep, N_KV_HEADS, HEAD_DIM)
    new_v_p = v_p.reshape(1, bsz, keep, N_KV_HEADS, HEAD_DIM)
    new_u_p = u_p[:, 6:8][None]
    u_all = jnp.concatenate([st, u_s.reshape(nseq, dec, CONV_DIM)], axis=1)
    new_u_s = u_all[:, -2:][None]
    return (y_p.reshape(bsz, seq, D_MODEL), y_s.reshape(nseq, dec, D_MODEL),
            new_k_p, new_v_p, new_u_p, new_k_s, new_v_s, new_u_s)
```

```python
import functools

import jax
import jax.numpy as jnp
import numpy as np
from jax import lax
from jax.experimental import pallas as pl
from jax.experimental.pallas import tpu as pltpu
from jax.experimental.pallas import tpu_sc as plsc

D_MODEL = 1024
HEAD_DIM = 64
ATTN_DIM = 512
N_Q_HEADS = 8
N_KV_HEADS = 2
KV_DIM = 128
CONV_DIM = 512
WINDOW = 128
IN_PROJ_DIM = ATTN_DIM + 2 * KV_DIM + 3 * CONV_DIM
N_GROUPS = 4
EXPERTS_PER_GROUP = 8
N_EXPERTS = 32
EXPERT_HIDDEN = 512
RMS_EPS = 1e-5
ROPE_THETA = 10000.0
PAST_LEN = 16384

LANES = 128
ROUTER_ROWS = 48
NEG = -1e30

MIXER_TILE = 512
SAMPLE_TILE = 128
SAMPLE_CHUNK = 8
SAMPLE_KEYS = 1152
EXPERT_BLOCK = 512
COMBINE_TILE = 512
COMBINE_CHUNK = 4096
SC_ROWS = 32
SC_INDEX_WINDOW = 128
VMEM_LIMIT = 56 * 1024 * 1024


def _rms(x, gain):
    ms = jnp.mean(x * x, axis=-1, keepdims=True)
    return x * lax.rsqrt(ms + RMS_EPS) * gain


def _half_lane_mask(rows=1):
    return lax.broadcasted_iota(jnp.int32, (rows, LANES), 1) < HEAD_DIM


def _rope(x, cos, sin_signed):
    t, w = x.shape
    reps = w // LANES
    lane = lax.broadcasted_iota(jnp.int32, (t, w), 1)
    upper = (lane % HEAD_DIM) >= (HEAD_DIM // 2)
    partner = jnp.where(upper, pltpu.roll(x, HEAD_DIM // 2, 1), pltpu.roll(x, w - HEAD_DIM // 2, 1))
    if reps > 1:
        cos = jnp.concatenate([cos] * reps, axis=1)
        sin_signed = jnp.concatenate([sin_signed] * reps, axis=1)
    return x * cos + partner * sin_signed


def _swap_halves(x):
    return pltpu.roll(x, HEAD_DIM, 1)


def _group_norm64(x, gain):
    t, w = x.shape
    lo = _half_lane_mask(t)
    outs = []
    for c in range(w // LANES):
        xc = x[:, c * LANES:(c + 1) * LANES]
        sq = xc * xc
        s_lo = jnp.sum(jnp.where(lo, sq, 0.0), axis=-1, keepdims=True)
        s_hi = jnp.sum(jnp.where(lo, 0.0, sq), axis=-1, keepdims=True)
        r = jnp.where(lo, lax.rsqrt(s_lo * (1.0 / HEAD_DIM) + RMS_EPS), lax.rsqrt(s_hi * (1.0 / HEAD_DIM) + RMS_EPS))
        outs.append(xc * r)
    return jnp.concatenate(outs, axis=1) * gain


def _attend(q_rows, keys, keys_sw, vals, vals_sw, bias, sinks_ref, group, transposed=False):
    r = q_rows.shape[0]
    lo = _half_lane_mask(r)
    outs = []
    for parity in range(2):
        heads = [4 * group + parity, 4 * group + parity + 2]
        kk, vv = (keys, vals) if parity == group else (keys_sw, vals_sw)
        half = lo if parity == 0 else jnp.logical_not(lo)
        qs = []
        for h in heads:
            pair = q_rows[:, (h // 2) * LANES:(h // 2 + 1) * LANES]
            qs.append(jnp.where(half, pair, jnp.zeros_like(pair)))
        qcat = jnp.concatenate(qs, axis=0)
        s = (jnp.dot(qcat, kk, preferred_element_type=jnp.float32) if transposed else
             lax.dot_general(qcat, kk, (((1,), (1,)), ((), ())), preferred_element_type=jnp.float32))
        ps, dens = [], []
        for i, h in enumerate(heads):
            sh = s[i * r:(i + 1) * r] + bias
            sink = sinks_ref[h]
            m = jnp.maximum(jnp.max(sh, axis=-1, keepdims=True), sink)
            p = jnp.exp(sh - m)
            dens.append(jnp.sum(p, axis=-1, keepdims=True) + jnp.exp(sink - m))
            ps.append(p.astype(jnp.bfloat16))
        pcat = jnp.concatenate(ps, axis=0)
        o = (lax.dot_general(pcat, vv, (((1,), (1,)), ((), ())), preferred_element_type=jnp.float32) if transposed
             else jnp.dot(pcat, vv, preferred_element_type=jnp.float32))
        for i, h in enumerate(heads):
            oh = o[i * r:(i + 1) * r] / dens[i]
            ms = jnp.sum(jnp.where(half, oh * oh, 0.0), axis=-1, keepdims=True) * (1.0 / HEAD_DIM)
            outs.append((h, oh * lax.rsqrt(ms + RMS_EPS)))
    outs.sort(key=lambda t: t[0])
    return [o for _, o in outs]


def _merge_heads(head_outs, lo):
    pairs = [jnp.where(lo, head_outs[2 * j], head_outs[2 * j + 1]) for j in range(4)]
    return jnp.concatenate(pairs, axis=1)


def _route(h2b, wr_ref, br_ref, tri_ref, cnt_ref, ri_ref, rf_ref):
    t = h2b.shape[0]
    lg = lax.dot_general(wr_ref[...], h2b, (((1,), (1,)), ((), ())), preferred_element_type=jnp.float32)
    lg = lg + br_ref[...][:, 0:1]
    r8 = lax.broadcasted_iota(jnp.int32, (8, t), 0)
    lgrp = jnp.where(r8 < N_GROUPS, lg[0:8], -jnp.inf)
    gmax = jnp.max(lgrp, axis=0, keepdims=True)
    grp = jnp.min(jnp.where(lgrp == gmax, r8, 8), axis=0, keepdims=True)
    pg_sel = 1.0 / jnp.sum(jnp.exp(lgrp - gmax), axis=0, keepdims=True)
    r32 = lax.broadcasted_iota(jnp.int32, (N_EXPERTS, t), 0)
    le = jnp.where((r32 // EXPERTS_PER_GROUP) == grp, lg[8:8 + N_EXPERTS], -jnp.inf)
    v1 = jnp.max(le, axis=0, keepdims=True)
    i1 = jnp.min(jnp.where(le == v1, r32, N_EXPERTS), axis=0, keepdims=True)
    le2 = jnp.where(r32 == i1, -jnp.inf, le)
    v2 = jnp.max(le2, axis=0, keepdims=True)
    i2 = jnp.min(jnp.where(le2 == v2, r32, N_EXPERTS), axis=0, keepdims=True)
    e21 = jnp.exp(v2 - v1)
    gate1 = pg_sel / (1.0 + e21)
    gate2 = pg_sel * e21 / (1.0 + e21)
    hot1 = r32 == i1
    hot2 = r32 == i2
    onehot = jnp.where(jnp.logical_or(hot1, hot2), 1.0, 0.0)
    before = jnp.dot(onehot.astype(jnp.bfloat16), tri_ref[...], preferred_element_type=jnp.float32)
    pos = before + cnt_ref[...][:, 0:1]
    rank1 = jnp.sum(jnp.where(hot1, pos, 0.0), axis=0, keepdims=True)
    rank2 = jnp.sum(jnp.where(hot2, pos, 0.0), axis=0, keepdims=True)
    cnt_ref[...] = cnt_ref[...] + jnp.sum(onehot, axis=1, keepdims=True)
    zi = jnp.zeros((4, t), jnp.int32)
    ri_ref[...] = jnp.concatenate([i1, i2, rank1.astype(jnp.int32), rank2.astype(jnp.int32), zi], axis=0)
    rf_ref[...] = jnp.concatenate([gate1, gate2, jnp.zeros((6, t), jnp.float32)], axis=0)


def _pack_bf16_pairs(xb):
    w = xb.shape[1] // 2
    bits = pltpu.bitcast(xb.astype(jnp.float32), jnp.uint32)
    return (bits[:, :w] >> 16) | (bits[:, w:] & jnp.uint32(0xFFFF0000))


def _unpack_pairs_f32(words):
    lo = pltpu.bitcast(words << 16, jnp.float32)
    hi = pltpu.bitcast(words & jnp.uint32(0xFFFF0000), jnp.float32)
    return jnp.concatenate([lo, hi], axis=1)


def _unpack_bf16_pairs(words):
    return _unpack_pairs_f32(words).astype(jnp.bfloat16)


def _split_proj(proj):
    a = ATTN_DIM
    q = proj[:, :a]
    k = proj[:, a:a + KV_DIM]
    v = proj[:, a + KV_DIM:a + 2 * KV_DIM]
    c0 = a + 2 * KV_DIM
    h_conv = proj[:, c0:c0 + CONV_DIM]
    gate_b = proj[:, c0 + CONV_DIM:c0 + 2 * CONV_DIM]
    gate_c = proj[:, c0 + 2 * CONV_DIM:c0 + 3 * CONV_DIM]
    return q, k, v, gate_c * h_conv, gate_b


def _mixer_tail(x, attn_n, conv_n, mod, wout_ref, fnorm_ref, wr_ref, br_ref, tri_ref, cnt_ref,
                x1_ref, h2_ref, ri_ref, rf_ref):
    gate_a, shift_f, scale_f = mod
    cat = jnp.concatenate([attn_n.astype(jnp.bfloat16), conv_n.astype(jnp.bfloat16)], axis=1)
    mix = jnp.dot(cat, wout_ref[...], preferred_element_type=jnp.float32)
    x1 = x + gate_a * mix
    x1_ref[...] = x1
    h2 = _rms(x1, fnorm_ref[...] * (1.0 + scale_f)) + shift_f
    h2b = h2.astype(jnp.bfloat16)
    h2_ref[...] = _pack_bf16_pairs(h2b)
    _route(h2b, wr_ref, br_ref, tri_ref, cnt_ref, ri_ref, rf_ref)


def _adaln_kernel(c_ref, w_ref, b_ref, o_ref):
    c = c_ref[...]
    a = (c * jax.nn.sigmoid(c)).astype(jnp.bfloat16)
    o_ref[...] = jnp.dot(a, w_ref[...].astype(jnp.bfloat16), preferred_element_type=jnp.float32) + b_ref[...]


def _adaln(c, w_mod, b_mod):
    n = c.shape[0]
    tn = 1536
    return pl.pallas_call(
        _adaln_kernel,
        grid=(w_mod.shape[1] // tn,),
        in_specs=[pl.BlockSpec((n, D_MODEL), lambda j: (0, 0)),
                  pl.BlockSpec((D_MODEL, tn), lambda j: (0, j)),
                  pl.BlockSpec((1, tn), lambda j: (0, j))],
        out_specs=pl.BlockSpec((n, tn), lambda j: (0, j)),
        out_shape=jax.ShapeDtypeStruct((n, w_mod.shape[1]), jnp.float32),
        compiler_params=pltpu.CompilerParams(dimension_semantics=("arbitrary",), vmem_limit_bytes=VMEM_LIMIT),
        name="adaln",
    )(c, w_mod, b_mod.reshape(1, -1))


def _prompt_mixer_kernel(sinks_ref, x_ref, mod_ref, anorm_ref, fnorm_ref, win_ref, cos_ref, sin_ref, bias_ref,
                         convw_ref, gna_ref, gnc_ref, wout_ref, wr_ref, br_ref, tri_ref,
                         x1_ref, h2_ref, ri_ref, rf_ref, cnt_out_ref, knew_ref, vnew_ref, unew_ref,
                         q_s, k_s, ksw_s, v_s, vsw_s, u_s, att_s, cnt_s):
    b = pl.program_id(0)
    t = pl.program_id(1)
    tm = MIXER_TILE
    nblk = tm // WINDOW

    @pl.when(jnp.logical_and(b == 0, t == 0))
    def _():
        cnt_s[...] = jnp.zeros_like(cnt_s)

    @pl.when(t == 0)
    def _():
        z = jnp.zeros((WINDOW, KV_DIM), jnp.bfloat16)
        k_s[0:WINDOW, :] = z
        ksw_s[0:WINDOW, :] = z
        v_s[0:WINDOW, :] = z
        vsw_s[0:WINDOW, :] = z
        u_s[0:8, :] = jnp.zeros((8, CONV_DIM), jnp.float32)

    x = x_ref[...]
    mod = mod_ref[...]
    h = _rms(x, anorm_ref[...] * (1.0 + mod[1:2])) + mod[0:1]
    proj = jnp.dot(h.astype(jnp.bfloat16), win_ref[...], preferred_element_type=jnp.float32)
    q, k, v, u, gate_b = _split_proj(proj)
    cos = cos_ref[...]
    sin = sin_ref[...]
    q = _rope(q, cos, sin) * (HEAD_DIM ** -0.5)
    k = _rope(k, cos, sin)
    q_s[...] = q.astype(jnp.bfloat16)
    k_s[WINDOW:, :] = k.astype(jnp.bfloat16)
    ksw_s[WINDOW:, :] = _swap_halves(k).astype(jnp.bfloat16)
    v_s[WINDOW:, :] = v.astype(jnp.bfloat16)
    vsw_s[WINDOW:, :] = _swap_halves(v).astype(jnp.bfloat16)
    knew_ref[...] = k[tm - WINDOW:, :]
    vnew_ref[...] = v[tm - WINDOW:, :]

    u_s[8:, :] = u
    unew_ref[...] = u[tm - 8:, :]
    cw = convw_ref[...]
    conv = (u_s[6:6 + tm, :] * cw[0:1] + u_s[7:7 + tm, :] * cw[1:2] + u * cw[2:3]) * gate_b
    u_s[0:8, :] = u[tm - 8:, :]
    conv_n = _group_norm64(conv, gnc_ref[...])

    lo = _half_lane_mask(WINDOW)
    first = t == 0

    def block(i, carry):
        r0 = pl.multiple_of(i * WINDOW, WINDOW)
        bias = bias_ref[jnp.where(jnp.logical_and(first, i == 0), 0, 1)]
        qb = q_s[pl.ds(r0, WINDOW), :]
        keys = k_s[pl.ds(r0, 2 * WINDOW), :]
        keys_sw = ksw_s[pl.ds(r0, 2 * WINDOW), :]
        vals = v_s[pl.ds(r0, 2 * WINDOW), :]
        vals_sw = vsw_s[pl.ds(r0, 2 * WINDOW), :]
        outs = []
        for g in range(N_KV_HEADS):
            outs += _attend(qb, keys, keys_sw, vals, vals_sw, bias, sinks_ref, g)
        att_s[pl.ds(r0, WINDOW), :] = _merge_heads(outs, lo)
        return carry

    lax.fori_loop(0, nblk, block, 0, unroll=True)
    k_s[0:WINDOW, :] = k_s[tm:tm + WINDOW, :]
    ksw_s[0:WINDOW, :] = ksw_s[tm:tm + WINDOW, :]
    v_s[0:WINDOW, :] = v_s[tm:tm + WINDOW, :]
    vsw_s[0:WINDOW, :] = vsw_s[tm:tm + WINDOW, :]

    attn_n = att_s[...] * gna_ref[...]
    _mixer_tail(x, attn_n, conv_n, (mod[2:3], mod[3:4], mod[4:5]), wout_ref, fnorm_ref, wr_ref, br_ref, tri_ref,
                cnt_s, x1_ref, h2_ref, ri_ref, rf_ref)
    cnt_out_ref[...] = cnt_s[...]


def _prompt_mixer(x, mod, sinks, consts):
    bsz, seq, _ = x.shape
    tm = MIXER_TILE
    nt = seq // tm
    full = lambda shape: pl.BlockSpec(shape, lambda b, t, s: (0,) * len(shape))
    in_specs = [
        pl.BlockSpec((None, tm, D_MODEL), lambda b, t, s: (b, t, 0)),
        pl.BlockSpec((None, 6, D_MODEL), lambda b, t, s: (b, 0, 0)),
        full((1, D_MODEL)), full((1, D_MODEL)),
        full((D_MODEL, IN_PROJ_DIM)),
        pl.BlockSpec((tm, LANES), lambda b, t, s: (t, 0)),
        pl.BlockSpec((tm, LANES), lambda b, t, s: (t, 0)),
        full((2, WINDOW, 2 * WINDOW)),
        full((3, CONV_DIM)), full((1, ATTN_DIM)), full((1, CONV_DIM)),
        full((D_MODEL, D_MODEL)),
        full((ROUTER_ROWS, D_MODEL)), full((ROUTER_ROWS, LANES)),
        full((tm, tm)),
    ]
    out_shape = [
        jax.ShapeDtypeStruct((bsz * seq, D_MODEL), jnp.float32),
        jax.ShapeDtypeStruct((bsz * seq, D_MODEL // 2), jnp.uint32),
        jax.ShapeDtypeStruct((bsz * nt, 8, tm), jnp.int32),
        jax.ShapeDtypeStruct((bsz * nt, 8, tm), jnp.float32),
        jax.ShapeDtypeStruct((N_EXPERTS, LANES), jnp.float32),
        jax.ShapeDtypeStruct((bsz, WINDOW, KV_DIM), jnp.float32),
        jax.ShapeDtypeStruct((bsz, WINDOW, KV_DIM), jnp.float32),
        jax.ShapeDtypeStruct((bsz, 8, CONV_DIM), jnp.float32),
    ]
    out_specs = [
        pl.BlockSpec((tm, D_MODEL), lambda b, t, s: (b * nt + t, 0)),
        pl.BlockSpec((tm, D_MODEL // 2), lambda b, t, s: (b * nt + t, 0)),
        pl.BlockSpec((None, 8, tm), lambda b, t, s: (b * nt + t, 0, 0)),
        pl.BlockSpec((None, 8, tm), lambda b, t, s: (b * nt + t, 0, 0)),
        pl.BlockSpec((N_EXPERTS, LANES), lambda b, t, s: (0, 0)),
        pl.BlockSpec((None, WINDOW, KV_DIM), lambda b, t, s: (b, 0, 0)),
        pl.BlockSpec((None, WINDOW, KV_DIM), lambda b, t, s: (b, 0, 0)),
        pl.BlockSpec((None, 8, CONV_DIM), lambda b, t, s: (b, 0, 0)),
    ]
    scratch = [
        pltpu.VMEM((tm, ATTN_DIM), jnp.bfloat16),
        pltpu.VMEM((tm + WINDOW, KV_DIM), jnp.bfloat16), pltpu.VMEM((tm + WINDOW, KV_DIM), jnp.bfloat16),
        pltpu.VMEM((tm + WINDOW, KV_DIM), jnp.bfloat16), pltpu.VMEM((tm + WINDOW, KV_DIM), jnp.bfloat16),
        pltpu.VMEM((tm + 8, CONV_DIM), jnp.float32),
        pltpu.VMEM((tm, ATTN_DIM), jnp.float32),
        pltpu.VMEM((N_EXPERTS, LANES), jnp.float32),
    ]
    return pl.pallas_call(
        _prompt_mixer_kernel,
        grid_spec=pltpu.PrefetchScalarGridSpec(
            num_scalar_prefetch=1, grid=(bsz, nt), in_specs=in_specs, out_specs=out_specs, scratch_shapes=scratch),
        out_shape=out_shape,
        compiler_params=pltpu.CompilerParams(dimension_semantics=("arbitrary", "arbitrary"),
                                             vmem_limit_bytes=VMEM_LIMIT),
        name="prompt_mixer",
    )(sinks, x, mod, consts["anorm"], consts["fnorm"], consts["w_in"], consts["cos_p"], consts["sin_p"],
      consts["bias_p"], consts["conv_w"], consts["gn_attn"], consts["gn_conv"], consts["w_out"],
      consts["w_r"], consts["b_r"], consts["tri"])


def _sample_mixer_kernel(sinks_ref, x_ref, mod_ref, anorm_ref, fnorm_ref, win_ref, cos_ref, sin_ref, bias_ref,
                         convw_ref, gna_ref, gnc_ref, wout_ref, wr_ref, br_ref, tri_ref,
                         ck_ref, cv_ref, e1_ref, e2_ref, cnt_in_ref,
                         x1_ref, h2_ref, ri_ref, rf_ref, cnt_out_ref, knew_ref, vnew_ref, unew_ref,
                         q_s, k_s, v_s, att_s, cnt_s, *, dec):
    n = x_ref.shape[0]
    cb = SAMPLE_CHUNK
    rows = cb * dec

    @pl.when(pl.program_id(0) == 0)
    def _():
        cnt_s[...] = cnt_in_ref[...]

    x = x_ref[...]
    mod = jnp.repeat(mod_ref[...], dec, axis=0)
    md = lambda j: mod[:, j * D_MODEL:(j + 1) * D_MODEL]
    h = _rms(x, anorm_ref[...]) * (1.0 + md(1)) + md(0)
    proj = jnp.dot(h.astype(jnp.bfloat16), win_ref[...], preferred_element_type=jnp.float32)
    q, k, v, u, gate_b = _split_proj(proj)
    cos = cos_ref[...]
    sin = sin_ref[...]
    q = _rope(q, cos, sin) * (HEAD_DIM ** -0.5)
    k = _rope(k, cos, sin)
    q_s[...] = q.astype(jnp.bfloat16)
    k_s[...] = k
    v_s[...] = v
    unew_ref[...] = u

    pos = lax.broadcasted_iota(jnp.int32, (n, CONV_DIM), 0) % dec
    um1 = jnp.where(pos >= 1, pltpu.roll(u, 1, 0), 0.0) + e1_ref[...]
    um2 = jnp.where(pos >= 2, pltpu.roll(u, 2, 0), 0.0) + e2_ref[...]
    cw = convw_ref[...]
    conv = (um2 * cw[0:1] + um1 * cw[1:2] + u * cw[2:3]) * gate_b
    conv_n = _group_norm64(conv, gnc_ref[...])

    lo = _half_lane_mask(rows)
    bias = bias_ref[...]

    half = KV_DIM // 2
    lane = lax.broadcasted_iota(jnp.int32, (KV_DIM, WINDOW), 1)
    swap_rows = lambda t: jnp.concatenate([t[half:], t[:half]], axis=0)

    for c in range(n // rows):
        r0 = c * rows
        qb = q_s[r0:r0 + rows, :]
        cats = []
        for ref, new_rows, out_ref in ((ck_ref, k_s[r0:r0 + rows, :], knew_ref), (cv_ref, v_s[r0:r0 + rows, :], vnew_ref)):
            tiles = [ref[(c * cb + b) * KV_DIM:(c * cb + b + 1) * KV_DIM, :] for b in range(cb)]
            new_t = jnp.concatenate([jnp.transpose(new_rows), jnp.zeros((KV_DIM, WINDOW - rows), jnp.float32)], axis=1)
            cat = jnp.concatenate(tiles + [new_t], axis=1)
            cats.append((cat.astype(jnp.bfloat16), swap_rows(cat).astype(jnp.bfloat16)))
            for b in range(cb):
                kept = pltpu.roll(tiles[b], WINDOW - dec, 1)
                fresh = pltpu.roll(new_t, (WINDOW - dec - b * dec) % WINDOW, 1)
                out_ref[(c * cb + b) * KV_DIM:(c * cb + b + 1) * KV_DIM, :] = jnp.where(lane < WINDOW - dec, kept, fresh)
        (keys, keys_sw), (vals, vals_sw) = cats
        outs = []
        for g in range(N_KV_HEADS):
            outs += _attend(qb, keys, keys_sw, vals, vals_sw, bias, sinks_ref, g, transposed=True)
        att_s[r0:r0 + rows, :] = _merge_heads(outs, lo)

    attn_n = att_s[...] * gna_ref[...]
    _mixer_tail(x, attn_n, conv_n, (md(2), md(3), md(4)), wout_ref, fnorm_ref, wr_ref, br_ref, tri_ref,
                cnt_s, x1_ref, h2_ref, ri_ref, rf_ref)
    cnt_out_ref[...] = cnt_s[...]


def _sample_mixer(x, mod, sinks, consts, ck, cv, e1, e2, cnt_in, dec):
    n_s = x.shape[0]
    ts = SAMPLE_TILE
    seqs = ts // dec
    full = lambda a: pl.BlockSpec(a.shape, lambda i, s: (0,) * a.ndim)
    rows = lambda width: pl.BlockSpec((ts, width), lambda i, s: (i, 0))
    in_specs = [
        rows(D_MODEL), pl.BlockSpec((seqs, 6 * D_MODEL), lambda i, s: (i, 0)),
        full(consts["anorm"]), full(consts["fnorm"]), full(consts["w_in"]),
        rows(LANES), rows(LANES), full(consts["bias_s"]),
        full(consts["conv_w"]), full(consts["gn_attn"]), full(consts["gn_conv"]), full(consts["w_out"]),
        full(consts["w_r"]), full(consts["b_r"]),
        pl.BlockSpec((ts, ts), lambda i, s: (0, 0)),
        pl.BlockSpec((seqs * KV_DIM, WINDOW), lambda i, s: (i, 0)),
        pl.BlockSpec((seqs * KV_DIM, WINDOW), lambda i, s: (i, 0)),
        rows(CONV_DIM), rows(CONV_DIM), full(cnt_in),
    ]
    args = [x, mod, consts["anorm"], consts["fnorm"], consts["w_in"], consts["cos_s"], consts["sin_s"],
            consts["bias_s"], consts["conv_w"], consts["gn_attn"], consts["gn_conv"], consts["w_out"],
            consts["w_r"], consts["b_r"], consts["tri"], ck, cv, e1, e2, cnt_in]
    nt = n_s // ts
    out_shape = [
        jax.ShapeDtypeStruct((n_s, D_MODEL), jnp.float32),
        jax.ShapeDtypeStruct((n_s, D_MODEL // 2), jnp.uint32),
        jax.ShapeDtypeStruct((nt, 8, ts), jnp.int32),
        jax.ShapeDtypeStruct((nt, 8, ts), jnp.float32),
        jax.ShapeDtypeStruct((N_EXPERTS, LANES), jnp.float32),
        jax.ShapeDtypeStruct(ck.shape, jnp.float32),
        jax.ShapeDtypeStruct(cv.shape, jnp.float32),
        jax.ShapeDtypeStruct((n_s, CONV_DIM), jnp.float32),
    ]
    out_specs = [
        rows(D_MODEL),
        rows(D_MODEL // 2),
        pl.BlockSpec((None, 8, ts), lambda i, s: (i, 0, 0)),
        pl.BlockSpec((None, 8, ts), lambda i, s: (i, 0, 0)),
        pl.BlockSpec((N_EXPERTS, LANES), lambda i, s: (0, 0)),
        pl.BlockSpec((seqs * KV_DIM, WINDOW), lambda i, s: (i, 0)),
        pl.BlockSpec((seqs * KV_DIM, WINDOW), lambda i, s: (i, 0)),
        rows(CONV_DIM),
    ]
    scratch = [
        pltpu.VMEM((ts, ATTN_DIM), jnp.bfloat16),
        pltpu.VMEM((ts, KV_DIM), jnp.float32), pltpu.VMEM((ts, KV_DIM), jnp.float32),
        pltpu.VMEM((ts, ATTN_DIM), jnp.float32),
        pltpu.VMEM((N_EXPERTS, LANES), jnp.float32),
    ]
    return pl.pallas_call(
        functools.partial(_sample_mixer_kernel, dec=dec),
        grid_spec=pltpu.PrefetchScalarGridSpec(
            num_scalar_prefetch=1, grid=(nt,), in_specs=in_specs, out_specs=out_specs, scratch_shapes=scratch),
        out_shape=out_shape,
        compiler_params=pltpu.CompilerParams(dimension_semantics=("arbitrary",), vmem_limit_bytes=VMEM_LIMIT),
        name="sample_mixer",
    )(sinks, *args)


def _dispatch(idx, h2, n_rows):
    mesh = plsc.VectorSubcoreMesh(core_axis_name="core", subcore_axis_name="subcore")
    rows, window = SC_ROWS, SC_INDEX_WINDOW
    sub = window // rows
    n_tok, width = h2.shape
    assert n_tok % rows == 0 and (2 * n_tok) % window == 0

    moved = 3 * n_tok * width * h2.dtype.itemsize
    @pl.kernel(out_type=jax.ShapeDtypeStruct((n_rows, width), h2.dtype), mesh=mesh, scratch_types=[],
               cost_estimate=pl.CostEstimate(flops=0, transcendentals=0, bytes_accessed=moved))
    def scatter(h_hbm, i_hbm, xs_hbm):
        def body(indices, h_vmem, i_vmem):
            pltpu.sync_copy(h_vmem, xs_hbm.at[i_vmem.at[0, pl.ds(indices[1] * rows, rows)]])

        pltpu.emit_pipeline(
            body, grid=(2 * n_tok // window, sub),
            in_specs=[pl.BlockSpec((rows, width), lambda i, j: ((i * sub + j) % (n_tok // rows), 0)),
                      pl.BlockSpec((1, window), lambda i, j: (0, i))],
            out_specs=[], core_axis_name=("core", "subcore"),
            dimension_semantics=(pltpu.PARALLEL, pltpu.ARBITRARY), _explicit_indices=True,
        )(h_hbm, i_hbm)

    return scatter(h2, idx)


def _scatter_rows_kernel(dest_ref, h_ref, xs_in_ref, xs_ref, sem):
    del xs_in_ref
    n_tok = h_ref.shape[0]

    def issue(i, carry):
        for k in range(2):
            d = dest_ref[k * n_tok + i]
            pltpu.make_async_copy(h_ref.at[pl.ds(i, 1)], xs_ref.at[pl.ds(d, 1)], sem.at[0]).start(priority=k)
        return carry

    lax.fori_loop(0, n_tok, issue, 0, unroll=8)
    for _ in range(2):
        pltpu.make_async_copy(h_ref, xs_ref.at[pl.ds(0, n_tok)], sem.at[0]).wait()


def _scatter_rows(dest, h2, xs):
    return pl.pallas_call(
        _scatter_rows_kernel,
        grid_spec=pltpu.PrefetchScalarGridSpec(
            num_scalar_prefetch=1, grid=(1,),
            in_specs=[pl.BlockSpec(h2.shape, lambda i, d: (0, 0)), pl.BlockSpec(memory_space=pl.ANY)],
            out_specs=pl.BlockSpec(memory_space=pl.ANY),
            scratch_shapes=[pltpu.SemaphoreType.DMA((1,))]),
        out_shape=jax.ShapeDtypeStruct(xs.shape, xs.dtype),
        input_output_aliases={2: 0},
        compiler_params=pltpu.CompilerParams(dimension_semantics=("arbitrary",)),
        name="scatter_rows",
    )(dest, h2, xs)


def _experts_kernel(be_ref, nu_ref, nv_ref, nxt_ref, par_ref, phys_ref, xs_ref, wg_ref, wu_ref, wd_ref, y_ref,
                    wg_s, wu_s, wd_s, wg_f, wu_f, wd_f, sem):
    del phys_ref
    j = pl.program_id(0)
    used = j < nu_ref[0]
    changed = jnp.logical_or(j == 0, be_ref[j] != be_ref[jnp.maximum(j - 1, 0)])

    def weight_copies(e, slot):
        return [pltpu.make_async_copy(src.at[e], dst.at[slot], sem.at[slot])
                for src, dst in ((wg_ref, wg_f), (wu_ref, wu_f), (wd_ref, wd_f))]

    @pl.when(jnp.logical_and(used, changed))
    def _():
        slot = par_ref[j]

        @pl.when(j == 0)
        def _():
            for c in weight_copies(be_ref[0], slot):
                c.start()

        for c in weight_copies(be_ref[j], slot):
            c.wait()

        @pl.when(nxt_ref[j] >= 0)
        def _():
            for c in weight_copies(nxt_ref[j], 1 - slot):
                c.start()

        wg_s[...] = wg_f[slot].astype(jnp.bfloat16)
        wu_s[...] = wu_f[slot].astype(jnp.bfloat16)
        wd_s[...] = wd_f[slot].astype(jnp.bfloat16)

    @pl.when(used)
    def _():
        hb = xs_ref.shape[0] // 2
        row = lax.broadcasted_iota(jnp.int32, (hb, 1), 0)

        def load(h):
            live = row + h * hb < nv_ref[j]
            return _unpack_bf16_pairs(jnp.where(live, xs_ref[h * hb:(h + 1) * hb, :], jnp.uint32(0)))

        def swiglu(g, u):
            return (g * jax.nn.sigmoid(g) * u).astype(jnp.bfloat16)

        x0 = load(0)
        g0 = jnp.dot(x0, wg_s[...], preferred_element_type=jnp.float32)
        u0 = jnp.dot(x0, wu_s[...], preferred_element_type=jnp.float32)
        x1 = load(1)
        g1 = jnp.dot(x1, wg_s[...], preferred_element_type=jnp.float32)
        a0 = swiglu(g0, u0)
        u1 = jnp.dot(x1, wu_s[...], preferred_element_type=jnp.float32)
        y0 = jnp.dot(a0, wd_s[...], preferred_element_type=jnp.float32)
        a1 = swiglu(g1, u1)
        y_ref[0:hb, :] = _pack_bf16_pairs(y0.astype(jnp.bfloat16))
        y1 = jnp.dot(a1, wd_s[...], preferred_element_type=jnp.float32)
        y_ref[hb:, :] = _pack_bf16_pairs(y1.astype(jnp.bfloat16))


def _experts(block_e, n_used, n_valid, next_e, parity, phys, xs, w_gate, w_up, w_down):
    bm = EXPERT_BLOCK
    n_blocks = xs.shape[0] // bm
    hbm = pl.BlockSpec(memory_space=pl.ANY)
    return pl.pallas_call(
        _experts_kernel,
        grid_spec=pltpu.PrefetchScalarGridSpec(
            num_scalar_prefetch=6, grid=(n_blocks,),
            in_specs=[pl.BlockSpec((bm, D_MODEL // 2), lambda j, be, nu, nv, nx, pa, ph: (ph[j], 0)),
                      hbm, hbm, hbm],
            out_specs=pl.BlockSpec((bm, D_MODEL // 2), lambda j, be, nu, nv, nx, pa, ph: (ph[j], 0)),
            scratch_shapes=[pltpu.VMEM((D_MODEL, EXPERT_HIDDEN), jnp.bfloat16),
                            pltpu.VMEM((D_MODEL, EXPERT_HIDDEN), jnp.bfloat16),
                            pltpu.VMEM((EXPERT_HIDDEN, D_MODEL), jnp.bfloat16),
                            pltpu.VMEM((2, D_MODEL, EXPERT_HIDDEN), jnp.float32),
                            pltpu.VMEM((2, D_MODEL, EXPERT_HIDDEN), jnp.float32),
                            pltpu.VMEM((2, EXPERT_HIDDEN, D_MODEL), jnp.float32),
                            pltpu.SemaphoreType.DMA((2,))]),
        out_shape=jax.ShapeDtypeStruct((xs.shape[0], D_MODEL // 2), jnp.uint32),
        compiler_params=pltpu.CompilerParams(dimension_semantics=("arbitrary",), vmem_limit_bytes=VMEM_LIMIT),
        name="experts",
    )(block_e, n_used, n_valid, next_e, parity, phys, xs, w_gate, w_up, w_down)


def _gather_rows(idx, yb):
    mesh = plsc.VectorSubcoreMesh(core_axis_name="core", subcore_axis_name="subcore")
    rows, window = SC_ROWS, SC_INDEX_WINDOW
    sub = window // rows
    n = idx.shape[1]
    width = yb.shape[1]
    assert n % window == 0

    @pl.kernel(out_type=jax.ShapeDtypeStruct((n, width), yb.dtype), mesh=mesh, scratch_types=[])
    def gather(y_hbm, i_hbm, o_hbm):
        def body(indices, i_vmem, o_vmem):
            pltpu.sync_copy(y_hbm.at[i_vmem.at[0, pl.ds(indices[1] * rows, rows)]], o_vmem)

        pltpu.emit_pipeline(
            body, grid=(n // window, sub),
            in_specs=[pl.BlockSpec((1, window), lambda i, j: (0, i))],
            out_specs=[pl.BlockSpec((rows, width), lambda i, j: (i * sub + j, 0))],
            core_axis_name=("core", "subcore"),
            dimension_semantics=(pltpu.PARALLEL, pltpu.ARBITRARY), _explicit_indices=True,
        )(i_hbm, o_hbm)

    return gather(yb, idx)


def _combine_kernel(x1_ref, gate_ref, rf_ref, fin_ref, y0_ref, y1_ref, *out_refs, per_token_gate):
    o_ref = out_refs[-1]
    gates = jnp.transpose(rf_ref[...])
    moe = _unpack_pairs_f32(y0_ref[...]) * gates[:, 0:1] + _unpack_pairs_f32(y1_ref[...]) * gates[:, 1:2]
    gate_f = gate_ref[...] if per_token_gate else gate_ref[...][5:6]
    o_ref[...] = _rms(x1_ref[...] + gate_f * moe, fin_ref[...])


def _combine(x1, gate, rf, final_norm, ys, out_prev, first_tile, n_tiles, *, per_token_gate):
    tc = COMBINE_TILE
    gate_spec = (pl.BlockSpec((tc, D_MODEL), lambda i: (first_tile + i, 0)) if per_token_gate
                 else pl.BlockSpec((6, D_MODEL), lambda i: (0, 0)))
    args = [x1, gate, rf, final_norm, ys, ys]
    in_specs = [pl.BlockSpec((tc, D_MODEL), lambda i: (first_tile + i, 0)),
                gate_spec,
                pl.BlockSpec((8, tc), lambda i: (0, first_tile + i)),
                pl.BlockSpec((1, D_MODEL), lambda i: (0, 0)),
                pl.BlockSpec((tc, D_MODEL // 2), lambda i: (i, 0)),
                pl.BlockSpec((tc, D_MODEL // 2), lambda i: (n_tiles + i, 0))]
    aliases = {}
    if out_prev is not None:
        args.append(out_prev)
        in_specs.append(pl.BlockSpec(memory_space=pl.ANY))
        aliases = {len(args) - 1: 0}
    return pl.pallas_call(
        functools.partial(_combine_kernel, per_token_gate=per_token_gate),
        grid=(n_tiles,), in_specs=in_specs,
        out_specs=pl.BlockSpec((tc, D_MODEL), lambda i: (first_tile + i, 0)),
        out_shape=jax.ShapeDtypeStruct(x1.shape, jnp.float32),
        input_output_aliases=aliases,
        compiler_params=pltpu.CompilerParams(dimension_semantics=("arbitrary",), vmem_limit_bytes=VMEM_LIMIT),
        name="combine",
    )(*args)


def _rope_tables(pos):
    half = HEAD_DIM // 2
    inv_freq = np.power(np.float32(ROPE_THETA), -(np.arange(half, dtype=np.float32) / np.float32(half)))
    ang = (pos.astype(np.float32)[:, None] * inv_freq[None, :].astype(np.float32)).astype(np.float64)
    cos, sin = np.cos(ang).astype(np.float32), np.sin(ang).astype(np.float32)
    cos = np.concatenate([cos, cos, cos, cos], axis=1)
    sin = np.concatenate([-sin, sin, -sin, sin], axis=1)
    return cos, sin


def _prompt_bias():
    r = np.arange(WINDOW)[:, None]
    j = np.arange(2 * WINDOW)[None, :]
    band = (j > r) & (j <= r + WINDOW)
    later = np.where(band, 0.0, NEG).astype(np.float32)
    first = np.where(band & (j >= WINDOW), 0.0, NEG).astype(np.float32)
    return jnp.asarray(np.stack([first, later]))


def _sample_bias(dec):
    cb = SAMPLE_CHUNK
    qb = np.repeat(np.arange(cb), dec)[:, None]
    qi = np.tile(np.arange(dec), cb)[:, None]
    kb = np.concatenate([np.repeat(np.arange(cb), WINDOW), np.repeat(np.arange(cb), dec)])[None, :]
    kj = np.concatenate([np.tile(np.arange(WINDOW), cb), np.tile(np.arange(dec), cb)])[None, :]
    is_new = (np.arange(cb * WINDOW + cb * dec) >= cb * WINDOW)[None, :]
    ok = (kb == qb) & np.where(is_new, kj <= qi, kj > qi)
    ok = np.pad(ok, ((0, 0), (0, SAMPLE_KEYS - ok.shape[1])))
    return jnp.asarray(np.where(ok, 0.0, NEG).astype(np.float32))


def kernel(x_prompt, x_sample, cache_k, cache_v, state_conv, c_prompt, c_sample, attn_norm, ffn_norm, w_mod, b_mod, w_in, conv_w, attn_sinks, out_norm_attn, out_norm_conv, w_out, w_group, b_group, w_expert, b_expert, w_gate, w_up, w_down, final_norm):
    assert attn_norm.shape[0] == 1, "one layer"
    bsz, seq, _ = x_prompt.shape
    nseq, dec, _ = x_sample.shape
    m_prompt = bsz * seq
    n_s = nseq * dec
    m_total = m_prompt + n_s
    tm = MIXER_TILE
    assert seq % tm == 0 and n_s % SAMPLE_TILE == 0 and SAMPLE_TILE % (SAMPLE_CHUNK * dec) == 0
    assert cache_k.shape[2] == WINDOW and m_prompt % SAMPLE_TILE == 0 and m_total % COMBINE_TILE == 0
    bf = jnp.bfloat16

    w_r = jnp.zeros((ROUTER_ROWS, D_MODEL), jnp.float32)
    w_r = w_r.at[0:N_GROUPS].set(w_group[0].T).at[8:8 + N_EXPERTS].set(w_expert[0].T).astype(bf)
    b_r = jnp.zeros((ROUTER_ROWS,), jnp.float32).at[0:N_GROUPS].set(b_group[0]).at[8:8 + N_EXPERTS].set(b_expert[0])
    b_r = jnp.broadcast_to(b_r[:, None], (ROUTER_ROWS, LANES))
    cos_p, sin_p = _rope_tables(np.arange(seq))
    cos_s, sin_s = _rope_tables(PAST_LEN + np.arange(dec))
    tri = jnp.asarray(np.triu(np.ones((tm, tm), np.float32), 1)).astype(bf)
    consts = dict(
        anorm=attn_norm[0][None], fnorm=ffn_norm[0][None], w_in=w_in[0].astype(bf),
        cos_p=jnp.asarray(cos_p), sin_p=jnp.asarray(sin_p),
        cos_s=jnp.asarray(np.tile(cos_s, (nseq, 1))), sin_s=jnp.asarray(np.tile(sin_s, (nseq, 1))),
        bias_p=_prompt_bias(), bias_s=_sample_bias(dec), conv_w=conv_w[0],
        gn_attn=out_norm_attn[0][None], gn_conv=out_norm_conv[0][None], w_out=w_out[0].astype(bf),
        w_r=w_r, b_r=b_r, tri=tri)
    sinks = attn_sinks[0]

    mod = _adaln(jnp.concatenate([c_prompt, c_sample], axis=0), w_mod[0], b_mod[0])
    mod_p = mod[:bsz].reshape(bsz, 6, D_MODEL)
    mod_s = mod[bsz:]

    bm = EXPERT_BLOCK
    eids = jnp.arange(N_EXPERTS, dtype=jnp.int32)
    lookup = lambda table, e: jnp.sum(jnp.where(e[..., None] == eids, table, 0), axis=-1)
    by_token = lambda a: jnp.transpose(a, (1, 0, 2)).reshape(8, -1)
    ceil_bm = lambda c: (c + bm - 1) // bm * bm
    x1_p, h2_p, ri_p, rf_p, cnt_p, k_p, v_p, u_p = _prompt_mixer(x_prompt, mod_p, sinks, consts)
    cnt_p = cnt_p[:, 0].astype(jnp.int32)
    padded_p = ceil_bm(cnt_p)
    pstart_p = jnp.cumsum(padded_p) - padded_p
    nb_p = -(-(2 * m_prompt + N_EXPERTS * (bm - 1)) // bm)
    nb_o = -(-(2 * n_s + N_EXPERTS * (bm - 1)) // bm)
    n_blocks = nb_p + nb_o
    ri_p = by_token(ri_p)
    dest_p = (lookup(pstart_p, ri_p[0:2]) + ri_p[2:4]).reshape(2 * m_prompt)

    st = state_conv[0]
    zrow = jnp.zeros((nseq, 1, CONV_DIM), jnp.float32)
    e1 = jnp.concatenate([st[:, 1:2]] + [zrow] * (dec - 1), axis=1).reshape(n_s, CONV_DIM)
    e2 = jnp.concatenate([st[:, 0:1], st[:, 1:2]] + [zrow] * (dec - 2), axis=1).reshape(n_s, CONV_DIM)
    dest_p, e1 = lax.optimization_barrier((dest_p, e1))
    xs = _dispatch(dest_p[None], h2_p, n_blocks * bm)
    ck = jnp.transpose(cache_k[0], (0, 2, 3, 1)).reshape(nseq * KV_DIM, WINDOW)
    cv = jnp.transpose(cache_v[0], (0, 2, 3, 1)).reshape(nseq * KV_DIM, WINDOW)
    x1_s, h2_s, ri_s, rf_s, cnt_s, k_s, v_s, u_s = _sample_mixer(
        x_sample.reshape(n_s, D_MODEL), mod_s, sinks, consts, ck, cv, e1, e2,
        jnp.zeros((N_EXPERTS, LANES), jnp.float32), dec)

    cnt_s = cnt_s[:, 0].astype(jnp.int32)
    fit = jnp.minimum(cnt_s, padded_p - cnt_p)
    over = cnt_s - fit
    padded_o = ceil_bm(over)
    ostart = nb_p * bm + jnp.cumsum(padded_o) - padded_o
    ri_s = by_token(ri_s)
    e_s, r_s = ri_s[0:2], ri_s[2:4]
    dest_s = jnp.where(r_s < lookup(fit, e_s), lookup(pstart_p + cnt_p, e_s) + r_s, lookup(ostart - fit, e_s) + r_s)
    dest_s = dest_s.reshape(2 * n_s)

    from_t = lambda a: jnp.transpose(a.reshape(nseq, N_KV_HEADS, HEAD_DIM, WINDOW), (0, 3, 1, 2))[None]
    new_k_s, new_v_s = from_t(k_s), from_t(v_s)
    xs = _scatter_rows(dest_s, h2_s, xs)

    np_e, no_e = padded_p // bm, padded_o // bm
    lend = jnp.cumsum(np_e + no_e)
    lstart = lend - np_e - no_e
    n_used = lend[-1].astype(jnp.int32)
    blk = jnp.arange(n_blocks, dtype=jnp.int32)
    present = np_e + no_e > 0
    last_e = jnp.max(jnp.where(present, eids, 0))
    block_e = jnp.sum((lend[None, :] <= blk[:, None]).astype(jnp.int32), axis=1)
    block_e = jnp.where(blk < n_used, jnp.minimum(block_e, N_EXPERTS - 1), last_e)
    of_block = lambda table: lookup(table, block_e)
    within = blk - of_block(lstart)
    in_prompt = within < of_block(np_e)
    phys = jnp.where(in_prompt, of_block(pstart_p) // bm + within, of_block(ostart) // bm + within - of_block(np_e))
    n_valid = jnp.where(in_prompt, of_block(cnt_p + fit) - within * bm, of_block(over) - (within - of_block(np_e)) * bm)
    n_valid = jnp.where(blk < n_used, jnp.clip(n_valid, 0, bm), 0).astype(jnp.int32)
    phys = jnp.where(blk < n_used, phys, jnp.sum(jnp.where(blk == n_used - 1, phys, 0))).astype(jnp.int32)
    later = jnp.logical_and(present[None, :], eids[None, :] > eids[:, None])
    next_of = jnp.min(jnp.where(later, eids[None, :], N_EXPERTS), axis=1)
    next_of = jnp.where(next_of == N_EXPERTS, -1, next_of)
    next_e = of_block(next_of).astype(jnp.int32)
    parity = of_block((jnp.cumsum(present.astype(jnp.int32)) - 1) % 2).astype(jnp.int32)
    yb = _experts(block_e, n_used.reshape(1), n_valid, next_e, parity, phys, xs, w_gate[0], w_up[0], w_down[0])

    fin = final_norm[None]
    tc = COMBINE_TILE
    rf_p, rf_s = by_token(rf_p), by_token(rf_s)
    y_p = None
    chunk = min(COMBINE_CHUNK, seq)
    assert seq % chunk == 0 and chunk % tc == 0
    for c in range(m_prompt // chunk):
        lo, hi = c * chunk, (c + 1) * chunk
        idx = jnp.concatenate([dest_p[lo:hi], dest_p[m_prompt + lo:m_prompt + hi]])[None]
        y_p = _combine(x1_p, mod_p[lo // seq], rf_p, fin, _gather_rows(idx, yb), y_p, lo // tc, chunk // tc,
                       per_token_gate=False)
    gate_s = jnp.repeat(mod_s[:, 5 * D_MODEL:], dec, axis=0)
    y_s = _combine(x1_s, gate_s, rf_s, fin, _gather_rows(dest_s[None], yb), None, 0, n_s // tc, per_token_gate=True)

    keep = WINDOW
    new_k_p = k_p.reshape(1, bsz, keep, N_KV_HEADS, HEAD_DIM)
    new_v_p = v_p.reshape(1, bsz, keep, N_KV_HEADS, HEAD_DIM)
    new_u_p = u_p[:, 6:8][None]
    u_all = jnp.concatenate([st, u_s.reshape(nseq, dec, CONV_DIM)], axis=1)
    new_u_s = u_all[:, -2:][None]
    return (y_p.reshape(bsz, seq, D_MODEL), y_s.reshape(nseq, dec, D_MODEL),
            new_k_p, new_v_p, new_u_p, new_k_s, new_v_s, new_u_s)
```

```python
import functools

import jax
import jax.numpy as jnp
import numpy as np
from jax import lax
from jax.experimental import pallas as pl
from jax.experimental.pallas import tpu as pltpu
from jax.experimental.pallas import tpu_sc as plsc

D_MODEL = 1024
HEAD_DIM = 64
ATTN_DIM = 512
N_Q_HEADS = 8
N_KV_HEADS = 2
KV_DIM = 128
CONV_DIM = 512
WINDOW = 128
IN_PROJ_DIM = ATTN_DIM + 2 * KV_DIM + 3 * CONV_DIM
N_GROUPS = 4
EXPERTS_PER_GROUP = 8
N_EXPERTS = 32
EXPERT_HIDDEN = 512
RMS_EPS = 1e-5
ROPE_THETA = 10000.0
PAST_LEN = 16384

LANES = 128
ROUTER_ROWS = 48
NEG = -1e30

MIXER_TILE = 512
SAMPLE_TILE = 128
SAMPLE_CHUNK = 8
SAMPLE_KEYS = (SAMPLE_CHUNK + 1) * WINDOW
EXPERT_BLOCK = 512
COMBINE_TILE = 512
COMBINE_CHUNK = 4096
SC_ROWS = 32
SC_INDEX_WINDOW = 128
VMEM_LIMIT = 56 * 1024 * 1024


def _rms(x, gain):
    ms = jnp.mean(x * x, axis=-1, keepdims=True)
    return x * lax.rsqrt(ms + RMS_EPS) * gain


def _half_lane_mask(rows=1):
    return lax.broadcasted_iota(jnp.int32, (rows, LANES), 1) < HEAD_DIM


def _rope(x, cos, sin_signed):
    t, w = x.shape
    reps = w // LANES
    lane = lax.broadcasted_iota(jnp.int32, (t, w), 1)
    upper = (lane % HEAD_DIM) >= (HEAD_DIM // 2)
    partner = jnp.where(upper, pltpu.roll(x, HEAD_DIM // 2, 1), pltpu.roll(x, w - HEAD_DIM // 2, 1))
    if reps > 1:
        cos = jnp.concatenate([cos] * reps, axis=1)
        sin_signed = jnp.concatenate([sin_signed] * reps, axis=1)
    return x * cos + partner * sin_signed


def _swap_halves(x):
    return pltpu.roll(x, HEAD_DIM, 1)


def _group_norm64(x, gain):
    t, w = x.shape
    lo = _half_lane_mask(t)
    outs = []
    for c in range(w // LANES):
        xc = x[:, c * LANES:(c + 1) * LANES]
        sq = xc * xc
        s_lo = jnp.sum(jnp.where(lo, sq, 0.0), axis=-1, keepdims=True)
        s_hi = jnp.sum(jnp.where(lo, 0.0, sq), axis=-1, keepdims=True)
        r = jnp.where(lo, lax.rsqrt(s_lo * (1.0 / HEAD_DIM) + RMS_EPS), lax.rsqrt(s_hi * (1.0 / HEAD_DIM) + RMS_EPS))
        outs.append(xc * r)
    return jnp.concatenate(outs, axis=1) * gain


def _attend(q_rows, keys, keys_sw, vals, vals_sw, bias, sinks_ref, group, transposed=False):
    r = q_rows.shape[0]
    lo = _half_lane_mask(r)
    outs = []
    for parity in range(2):
        heads = [4 * group + parity, 4 * group + parity + 2]
        kk, vv = (keys, vals) if parity == group else (keys_sw, vals_sw)
        half = lo if parity == 0 else jnp.logical_not(lo)
        qs = []
        for h in heads:
            pair = q_rows[:, (h // 2) * LANES:(h // 2 + 1) * LANES]
            qs.append(jnp.where(half, pair, jnp.zeros_like(pair)))
        qcat = jnp.concatenate(qs, axis=0)
        s = (jnp.dot(qcat, kk, preferred_element_type=jnp.float32) if transposed else
             lax.dot_general(qcat, kk, (((1,), (1,)), ((), ())), preferred_element_type=jnp.float32))
        ps, dens = [], []
        for i, h in enumerate(heads):
            sh = s[i * r:(i + 1) * r] + bias
            sink = sinks_ref[h]
            m = jnp.maximum(jnp.max(sh, axis=-1, keepdims=True), sink)
            p = jnp.exp(sh - m)
            dens.append(jnp.sum(p, axis=-1, keepdims=True) + jnp.exp(sink - m))
            ps.append(p.astype(jnp.bfloat16))
        pcat = jnp.concatenate(ps, axis=0)
        o = (lax.dot_general(pcat, vv, (((1,), (1,)), ((), ())), preferred_element_type=jnp.float32) if transposed
             else jnp.dot(pcat, vv, preferred_element_type=jnp.float32))
        for i, h in enumerate(heads):
            oh = o[i * r:(i + 1) * r] / dens[i]
            ms = jnp.sum(jnp.where(half, oh * oh, 0.0), axis=-1, keepdims=True) * (1.0 / HEAD_DIM)
            outs.append((h, oh * lax.rsqrt(ms + RMS_EPS)))
    outs.sort(key=lambda t: t[0])
    return [o for _, o in outs]


def _merge_heads(head_outs, lo):
    pairs = [jnp.where(lo, head_outs[2 * j], head_outs[2 * j + 1]) for j in range(4)]
    return jnp.concatenate(pairs, axis=1)


def _route(h2b, wr_ref, br_ref, tri_ref, cnt_ref, ri_ref, rf_ref):
    t = h2b.shape[0]
    lg = lax.dot_general(wr_ref[...], h2b, (((1,), (1,)), ((), ())), preferred_element_type=jnp.float32)
    lg = lg + br_ref[...][:, 0:1]
    r8 = lax.broadcasted_iota(jnp.int32, (8, t), 0)
    lgrp = jnp.where(r8 < N_GROUPS, lg[0:8], -jnp.inf)
    gmax = jnp.max(lgrp, axis=0, keepdims=True)
    grp = jnp.min(jnp.where(lgrp == gmax, r8, 8), axis=0, keepdims=True)
    pg_sel = 1.0 / jnp.sum(jnp.exp(lgrp - gmax), axis=0, keepdims=True)
    r32 = lax.broadcasted_iota(jnp.int32, (N_EXPERTS, t), 0)
    le = jnp.where((r32 // EXPERTS_PER_GROUP) == grp, lg[8:8 + N_EXPERTS], -jnp.inf)
    v1 = jnp.max(le, axis=0, keepdims=True)
    i1 = jnp.min(jnp.where(le == v1, r32, N_EXPERTS), axis=0, keepdims=True)
    le2 = jnp.where(r32 == i1, -jnp.inf, le)
    v2 = jnp.max(le2, axis=0, keepdims=True)
    i2 = jnp.min(jnp.where(le2 == v2, r32, N_EXPERTS), axis=0, keepdims=True)
    e21 = jnp.exp(v2 - v1)
    gate1 = pg_sel / (1.0 + e21)
    gate2 = pg_sel * e21 / (1.0 + e21)
    hot1 = r32 == i1
    hot2 = r32 == i2
    onehot = jnp.where(jnp.logical_or(hot1, hot2), 1.0, 0.0)
    before = jnp.dot(onehot.astype(jnp.bfloat16), tri_ref[...], preferred_element_type=jnp.float32)
    pos = before + cnt_ref[...][:, 0:1]
    rank1 = jnp.sum(jnp.where(hot1, pos, 0.0), axis=0, keepdims=True)
    rank2 = jnp.sum(jnp.where(hot2, pos, 0.0), axis=0, keepdims=True)
    cnt_ref[...] = cnt_ref[...] + jnp.sum(onehot, axis=1, keepdims=True)
    zi = jnp.zeros((4, t), jnp.int32)
    ri_ref[...] = jnp.concatenate([i1, i2, rank1.astype(jnp.int32), rank2.astype(jnp.int32), zi], axis=0)
    rf_ref[...] = jnp.concatenate([gate1, gate2, jnp.zeros((6, t), jnp.float32)], axis=0)


def _pack_bf16_pairs(xb):
    w = xb.shape[1] // 2
    bits = pltpu.bitcast(xb.astype(jnp.float32), jnp.uint32)
    return (bits[:, :w] >> 16) | (bits[:, w:] & jnp.uint32(0xFFFF0000))


def _unpack_pairs_f32(words):
    lo = pltpu.bitcast(words << 16, jnp.float32)
    hi = pltpu.bitcast(words & jnp.uint32(0xFFFF0000), jnp.float32)
    return jnp.concatenate([lo, hi], axis=1)


def _unpack_bf16_pairs(words):
    return _unpack_pairs_f32(words).astype(jnp.bfloat16)


def _split_proj(proj):
    a = ATTN_DIM
    q = proj[:, :a]
    k = proj[:, a:a + KV_DIM]
    v = proj[:, a + KV_DIM:a + 2 * KV_DIM]
    c0 = a + 2 * KV_DIM
    h_conv = proj[:, c0:c0 + CONV_DIM]
    gate_b = proj[:, c0 + CONV_DIM:c0 + 2 * CONV_DIM]
    gate_c = proj[:, c0 + 2 * CONV_DIM:c0 + 3 * CONV_DIM]
    return q, k, v, gate_c * h_conv, gate_b


def _mixer_tail(x, attn_n, conv_n, mod, wout_ref, fnorm_ref, wr_ref, br_ref, tri_ref, cnt_ref,
                x1_ref, h2_ref, ri_ref, rf_ref):
    gate_a, shift_f, scale_f = mod
    cat = jnp.concatenate([attn_n.astype(jnp.bfloat16), conv_n.astype(jnp.bfloat16)], axis=1)
    mix = jnp.dot(cat, wout_ref[...], preferred_element_type=jnp.float32)
    x1 = x + gate_a * mix
    x1_ref[...] = x1
    h2 = _rms(x1, fnorm_ref[...] * (1.0 + scale_f)) + shift_f
    h2b = h2.astype(jnp.bfloat16)
    h2_ref[...] = _pack_bf16_pairs(h2b)
    _route(h2b, wr_ref, br_ref, tri_ref, cnt_ref, ri_ref, rf_ref)


def _adaln_kernel(c_ref, w_ref, b_ref, o_ref):
    c = c_ref[...]
    a = (c * jax.nn.sigmoid(c)).astype(jnp.bfloat16)
    o_ref[...] = jnp.dot(a, w_ref[...].astype(jnp.bfloat16), preferred_element_type=jnp.float32) + b_ref[...]


def _adaln(c, w_mod, b_mod):
    n = c.shape[0]
    tn = 1536
    return pl.pallas_call(
        _adaln_kernel,
        grid=(w_mod.shape[1] // tn,),
        in_specs=[pl.BlockSpec((n, D_MODEL), lambda j: (0, 0)),
                  pl.BlockSpec((D_MODEL, tn), lambda j: (0, j)),
                  pl.BlockSpec((1, tn), lambda j: (0, j))],
        out_specs=pl.BlockSpec((n, tn), lambda j: (0, j)),
        out_shape=jax.ShapeDtypeStruct((n, w_mod.shape[1]), jnp.float32),
        compiler_params=pltpu.CompilerParams(dimension_semantics=("arbitrary",), vmem_limit_bytes=VMEM_LIMIT),
        name="adaln",
    )(c, w_mod, b_mod.reshape(1, -1))


def _prompt_mixer_kernel(sinks_ref, x_ref, mod_ref, anorm_ref, fnorm_ref, win_ref, cos_ref, sin_ref, bias_ref,
                         convw_ref, gna_ref, gnc_ref, wout_ref, wr_ref, br_ref, tri_ref,
                         x1_ref, h2_ref, ri_ref, rf_ref, cnt_out_ref, knew_ref, vnew_ref, unew_ref,
                         q_s, k_s, ksw_s, v_s, vsw_s, u_s, att_s, cnt_s):
    b = pl.program_id(0)
    t = pl.program_id(1)
    tm = MIXER_TILE
    nblk = tm // WINDOW

    @pl.when(jnp.logical_and(b == 0, t == 0))
    def _():
        cnt_s[...] = jnp.zeros_like(cnt_s)

    @pl.when(t == 0)
    def _():
        z = jnp.zeros((WINDOW, KV_DIM), jnp.bfloat16)
        k_s[0:WINDOW, :] = z
        ksw_s[0:WINDOW, :] = z
        v_s[0:WINDOW, :] = z
        vsw_s[0:WINDOW, :] = z
        u_s[0:8, :] = jnp.zeros((8, CONV_DIM), jnp.float32)

    x = x_ref[...]
    mod = mod_ref[...]
    h = _rms(x, anorm_ref[...] * (1.0 + mod[1:2])) + mod[0:1]
    proj = jnp.dot(h.astype(jnp.bfloat16), win_ref[...], preferred_element_type=jnp.float32)
    q, k, v, u, gate_b = _split_proj(proj)
    cos = cos_ref[...]
    sin = sin_ref[...]
    q = _rope(q, cos, sin) * (HEAD_DIM ** -0.5)
    k = _rope(k, cos, sin)
    q_s[...] = q.astype(jnp.bfloat16)
    k_s[WINDOW:, :] = k.astype(jnp.bfloat16)
    ksw_s[WINDOW:, :] = _swap_halves(k).astype(jnp.bfloat16)
    v_s[WINDOW:, :] = v.astype(jnp.bfloat16)
    vsw_s[WINDOW:, :] = _swap_halves(v).astype(jnp.bfloat16)
    knew_ref[...] = k[tm - WINDOW:, :]
    vnew_ref[...] = v[tm - WINDOW:, :]

    u_s[8:, :] = u
    unew_ref[...] = u[tm - 8:, :]
    cw = convw_ref[...]
    conv = (u_s[6:6 + tm, :] * cw[0:1] + u_s[7:7 + tm, :] * cw[1:2] + u * cw[2:3]) * gate_b
    u_s[0:8, :] = u[tm - 8:, :]
    conv_n = _group_norm64(conv, gnc_ref[...])

    lo = _half_lane_mask(WINDOW)
    first = t == 0

    def block(i, carry):
        r0 = pl.multiple_of(i * WINDOW, WINDOW)
        bias = bias_ref[jnp.where(jnp.logical_and(first, i == 0), 0, 1)]
        qb = q_s[pl.ds(r0, WINDOW), :]
        keys = k_s[pl.ds(r0, 2 * WINDOW), :]
        keys_sw = ksw_s[pl.ds(r0, 2 * WINDOW), :]
        vals = v_s[pl.ds(r0, 2 * WINDOW), :]
        vals_sw = vsw_s[pl.ds(r0, 2 * WINDOW), :]
        outs = []
        for g in range(N_KV_HEADS):
            outs += _attend(qb, keys, keys_sw, vals, vals_sw, bias, sinks_ref, g)
        att_s[pl.ds(r0, WINDOW), :] = _merge_heads(outs, lo)
        return carry

    lax.fori_loop(0, nblk, block, 0, unroll=True)
    k_s[0:WINDOW, :] = k_s[tm:tm + WINDOW, :]
    ksw_s[0:WINDOW, :] = ksw_s[tm:tm + WINDOW, :]
    v_s[0:WINDOW, :] = v_s[tm:tm + WINDOW, :]
    vsw_s[0:WINDOW, :] = vsw_s[tm:tm + WINDOW, :]

    attn_n = att_s[...] * gna_ref[...]
    _mixer_tail(x, attn_n, conv_n, (mod[2:3], mod[3:4], mod[4:5]), wout_ref, fnorm_ref, wr_ref, br_ref, tri_ref,
                cnt_s, x1_ref, h2_ref, ri_ref, rf_ref)
    cnt_out_ref[...] = cnt_s[...]


def _prompt_mixer(x, mod, sinks, consts):
    bsz, seq, _ = x.shape
    tm = MIXER_TILE
    nt = seq // tm
    full = lambda shape: pl.BlockSpec(shape, lambda b, t, s: (0,) * len(shape))
    in_specs = [
        pl.BlockSpec((None, tm, D_MODEL), lambda b, t, s: (b, t, 0)),
        pl.BlockSpec((None, 6, D_MODEL), lambda b, t, s: (b, 0, 0)),
        full((1, D_MODEL)), full((1, D_MODEL)),
        full((D_MODEL, IN_PROJ_DIM)),
        pl.BlockSpec((tm, LANES), lambda b, t, s: (t, 0)),
        pl.BlockSpec((tm, LANES), lambda b, t, s: (t, 0)),
        full((2, WINDOW, 2 * WINDOW)),
        full((3, CONV_DIM)), full((1, ATTN_DIM)), full((1, CONV_DIM)),
        full((D_MODEL, D_MODEL)),
        full((ROUTER_ROWS, D_MODEL)), full((ROUTER_ROWS, LANES)),
        full((tm, tm)),
    ]
    out_shape = [
        jax.ShapeDtypeStruct((bsz * seq, D_MODEL), jnp.float32),
        jax.ShapeDtypeStruct((bsz * seq, D_MODEL // 2), jnp.uint32),
        jax.ShapeDtypeStruct((bsz * nt, 8, tm), jnp.int32),
        jax.ShapeDtypeStruct((bsz * nt, 8, tm), jnp.float32),
        jax.ShapeDtypeStruct((N_EXPERTS, LANES), jnp.float32),
        jax.ShapeDtypeStruct((bsz, WINDOW, KV_DIM), jnp.float32),
        jax.ShapeDtypeStruct((bsz, WINDOW, KV_DIM), jnp.float32),
        jax.ShapeDtypeStruct((bsz, 8, CONV_DIM), jnp.float32),
    ]
    out_specs = [
        pl.BlockSpec((tm, D_MODEL), lambda b, t, s: (b * nt + t, 0)),
        pl.BlockSpec((tm, D_MODEL // 2), lambda b, t, s: (b * nt + t, 0)),
        pl.BlockSpec((None, 8, tm), lambda b, t, s: (b * nt + t, 0, 0)),
        pl.BlockSpec((None, 8, tm), lambda b, t, s: (b * nt + t, 0, 0)),
        pl.BlockSpec((N_EXPERTS, LANES), lambda b, t, s: (0, 0)),
        pl.BlockSpec((None, WINDOW, KV_DIM), lambda b, t, s: (b, 0, 0)),
        pl.BlockSpec((None, WINDOW, KV_DIM), lambda b, t, s: (b, 0, 0)),
        pl.BlockSpec((None, 8, CONV_DIM), lambda b, t, s: (b, 0, 0)),
    ]
    scratch = [
        pltpu.VMEM((tm, ATTN_DIM), jnp.bfloat16),
        pltpu.VMEM((tm + WINDOW, KV_DIM), jnp.bfloat16), pltpu.VMEM((tm + WINDOW, KV_DIM), jnp.bfloat16),
        pltpu.VMEM((tm + WINDOW, KV_DIM), jnp.bfloat16), pltpu.VMEM((tm + WINDOW, KV_DIM), jnp.bfloat16),
        pltpu.VMEM((tm + 8, CONV_DIM), jnp.float32),
        pltpu.VMEM((tm, ATTN_DIM), jnp.float32),
        pltpu.VMEM((N_EXPERTS, LANES), jnp.float32),
    ]
    return pl.pallas_call(
        _prompt_mixer_kernel,
        grid_spec=pltpu.PrefetchScalarGridSpec(
            num_scalar_prefetch=1, grid=(bsz, nt), in_specs=in_specs, out_specs=out_specs, scratch_shapes=scratch),
        out_shape=out_shape,
        compiler_params=pltpu.CompilerParams(dimension_semantics=("arbitrary", "arbitrary"),
                                             vmem_limit_bytes=VMEM_LIMIT),
        name="prompt_mixer",
    )(sinks, x, mod, consts["anorm"], consts["fnorm"], consts["w_in"], consts["cos_p"], consts["sin_p"],
      consts["bias_p"], consts["conv_w"], consts["gn_attn"], consts["gn_conv"], consts["w_out"],
      consts["w_r"], consts["b_r"], consts["tri"])


def _sample_mixer_kernel(sinks_ref, x_ref, mod_ref, anorm_ref, fnorm_ref, win_ref, cos_ref, sin_ref, bias_ref,
                         convw_ref, gna_ref, gnc_ref, wout_ref, wr_ref, br_ref, tri_ref,
                         ck_ref, cv_ref, e1_ref, e2_ref, cnt_in_ref,
                         x1_ref, h2_ref, ri_ref, rf_ref, cnt_out_ref, knew_ref, vnew_ref, unew_ref,
                         q_s, k_s, v_s, att_s, cnt_s, *, dec):
    n = x_ref.shape[0]
    cb = SAMPLE_CHUNK
    rows = cb * dec

    @pl.when(pl.program_id(0) == 0)
    def _():
        cnt_s[...] = cnt_in_ref[...]

    x = x_ref[...]
    mod = jnp.repeat(mod_ref[...], dec, axis=0)
    md = lambda j: mod[:, j * D_MODEL:(j + 1) * D_MODEL]
    h = _rms(x, anorm_ref[...]) * (1.0 + md(1)) + md(0)
    proj = jnp.dot(h.astype(jnp.bfloat16), win_ref[...], preferred_element_type=jnp.float32)
    q, k, v, u, gate_b = _split_proj(proj)
    cos = cos_ref[...]
    sin = sin_ref[...]
    q = _rope(q, cos, sin) * (HEAD_DIM ** -0.5)
    k = _rope(k, cos, sin)
    q_s[...] = q.astype(jnp.bfloat16)
    k_s[...] = k
    v_s[...] = v
    unew_ref[...] = u

    pos = lax.broadcasted_iota(jnp.int32, (n, CONV_DIM), 0) % dec
    um1 = jnp.where(pos >= 1, pltpu.roll(u, 1, 0), 0.0) + e1_ref[...]
    um2 = jnp.where(pos >= 2, pltpu.roll(u, 2, 0), 0.0) + e2_ref[...]
    cw = convw_ref[...]
    conv = (um2 * cw[0:1] + um1 * cw[1:2] + u * cw[2:3]) * gate_b
    conv_n = _group_norm64(conv, gnc_ref[...])

    lo = _half_lane_mask(rows)
    bias = bias_ref[...]

    half = KV_DIM // 2
    lane = lax.broadcasted_iota(jnp.int32, (KV_DIM, WINDOW), 1)
    swap_rows = lambda t: jnp.concatenate([t[half:], t[:half]], axis=0)

    for c in range(n // rows):
        r0 = c * rows
        qb = q_s[r0:r0 + rows, :]
        cats = []
        for ref, new_rows, out_ref in ((ck_ref, k_s[r0:r0 + rows, :], knew_ref), (cv_ref, v_s[r0:r0 + rows, :], vnew_ref)):
            tiles = [ref[(c * cb + b) * KV_DIM:(c * cb + b + 1) * KV_DIM, :] for b in range(cb)]
            new_t = jnp.concatenate([jnp.transpose(new_rows), jnp.zeros((KV_DIM, WINDOW - rows), jnp.float32)], axis=1)
            cat = jnp.concatenate(tiles + [new_t], axis=1)
            cats.append((cat.astype(jnp.bfloat16), swap_rows(cat).astype(jnp.bfloat16)))
            for b in range(cb):
                kept = pltpu.roll(tiles[b], WINDOW - dec, 1)
                fresh = pltpu.roll(new_t, (WINDOW - dec - b * dec) % WINDOW, 1)
                out_ref[(c * cb + b) * KV_DIM:(c * cb + b + 1) * KV_DIM, :] = jnp.where(lane < WINDOW - dec, kept, fresh)
        (keys, keys_sw), (vals, vals_sw) = cats
        outs = []
        for g in range(N_KV_HEADS):
            outs += _attend(qb, keys, keys_sw, vals, vals_sw, bias, sinks_ref, g, transposed=True)
        att_s[r0:r0 + rows, :] = _merge_heads(outs, lo)

    attn_n = att_s[...] * gna_ref[...]
    _mixer_tail(x, attn_n, conv_n, (md(2), md(3), md(4)), wout_ref, fnorm_ref, wr_ref, br_ref, tri_ref,
                cnt_s, x1_ref, h2_ref, ri_ref, rf_ref)
    cnt_out_ref[...] = cnt_s[...]


def _sample_mixer(x, mod, sinks, consts, ck, cv, e1, e2, cnt_in, dec):
    n_s = x.shape[0]
    ts = SAMPLE_TILE
    seqs = ts // dec
    full = lambda a: pl.BlockSpec(a.shape, lambda i, s: (0,) * a.ndim)
    rows = lambda width: pl.BlockSpec((ts, width), lambda i, s: (i, 0))
    in_specs = [
        rows(D_MODEL), pl.BlockSpec((seqs, 6 * D_MODEL), lambda i, s: (i, 0)),
        full(consts["anorm"]), full(consts["fnorm"]), full(consts["w_in"]),
        rows(LANES), rows(LANES), full(consts["bias_s"]),
        full(consts["conv_w"]), full(consts["gn_attn"]), full(consts["gn_conv"]), full(consts["w_out"]),
        full(consts["w_r"]), full(consts["b_r"]),
        pl.BlockSpec((ts, ts), lambda i, s: (0, 0)),
        pl.BlockSpec((seqs * KV_DIM, WINDOW), lambda i, s: (i, 0)),
        pl.BlockSpec((seqs * KV_DIM, WINDOW), lambda i, s: (i, 0)),
        rows(CONV_DIM), rows(CONV_DIM), full(cnt_in),
    ]
    args = [x, mod, consts["anorm"], consts["fnorm"], consts["w_in"], consts["cos_s"], consts["sin_s"],
            consts["bias_s"], consts["conv_w"], consts["gn_attn"], consts["gn_conv"], consts["w_out"],
            consts["w_r"], consts["b_r"], consts["tri"], ck, cv, e1, e2, cnt_in]
    nt = n_s // ts
    out_shape = [
        jax.ShapeDtypeStruct((n_s, D_MODEL), jnp.float32),
        jax.ShapeDtypeStruct((n_s, D_MODEL // 2), jnp.uint32),
        jax.ShapeDtypeStruct((nt, 8, ts), jnp.int32),
        jax.ShapeDtypeStruct((nt, 8, ts), jnp.float32),
        jax.ShapeDtypeStruct((N_EXPERTS, LANES), jnp.float32),
        jax.ShapeDtypeStruct(ck.shape, jnp.float32),
        jax.ShapeDtypeStruct(cv.shape, jnp.float32),
        jax.ShapeDtypeStruct((n_s, CONV_DIM), jnp.float32),
    ]
    out_specs = [
        rows(D_MODEL),
        rows(D_MODEL // 2),
        pl.BlockSpec((None, 8, ts), lambda i, s: (i, 0, 0)),
        pl.BlockSpec((None, 8, ts), lambda i, s: (i, 0, 0)),
        pl.BlockSpec((N_EXPERTS, LANES), lambda i, s: (0, 0)),
        pl.BlockSpec((seqs * KV_DIM, WINDOW), lambda i, s: (i, 0)),
        pl.BlockSpec((seqs * KV_DIM, WINDOW), lambda i, s: (i, 0)),
        rows(CONV_DIM),
    ]
    scratch = [
        pltpu.VMEM((ts, ATTN_DIM), jnp.bfloat16),
        pltpu.VMEM((ts, KV_DIM), jnp.float32), pltpu.VMEM((ts, KV_DIM), jnp.float32),
        pltpu.VMEM((ts, ATTN_DIM), jnp.float32),
        pltpu.VMEM((N_EXPERTS, LANES), jnp.float32),
    ]
    return pl.pallas_call(
        functools.partial(_sample_mixer_kernel, dec=dec),
        grid_spec=pltpu.PrefetchScalarGridSpec(
            num_scalar_prefetch=1, grid=(nt,), in_specs=in_specs, out_specs=out_specs, scratch_shapes=scratch),
        out_shape=out_shape,
        compiler_params=pltpu.CompilerParams(dimension_semantics=("arbitrary",), vmem_limit_bytes=VMEM_LIMIT),
        name="sample_mixer",
    )(sinks, *args)


def _dispatch(idx, h2, n_rows):
    mesh = plsc.VectorSubcoreMesh(core_axis_name="core", subcore_axis_name="subcore")
    rows, window = SC_ROWS, SC_INDEX_WINDOW
    sub = window // rows
    n_tok, width = h2.shape
    assert n_tok % rows == 0 and (2 * n_tok) % window == 0

    moved = 3 * n_tok * width * h2.dtype.itemsize
    @pl.kernel(out_type=jax.ShapeDtypeStruct((n_rows, width), h2.dtype), mesh=mesh, scratch_types=[],
               cost_estimate=pl.CostEstimate(flops=0, transcendentals=0, bytes_accessed=moved))
    def scatter(h_hbm, i_hbm, xs_hbm):
        def body(indices, h_vmem, i_vmem):
            pltpu.sync_copy(h_vmem, xs_hbm.at[i_vmem.at[0, pl.ds(indices[1] * rows, rows)]])

        pltpu.emit_pipeline(
            body, grid=(2 * n_tok // window, sub),
            in_specs=[pl.BlockSpec((rows, width), lambda i, j: ((i * sub + j) % (n_tok // rows), 0)),
                      pl.BlockSpec((1, window), lambda i, j: (0, i))],
            out_specs=[], core_axis_name=("core", "subcore"),
            dimension_semantics=(pltpu.PARALLEL, pltpu.ARBITRARY), _explicit_indices=True,
        )(h_hbm, i_hbm)

    return scatter(h2, idx)


def _scatter_rows_kernel(dest_ref, h_ref, xs_in_ref, xs_ref, sem):
    del xs_in_ref
    n_tok = h_ref.shape[0]

    def issue(i, carry):
        for k in range(2):
            d = dest_ref[k * n_tok + i]
            pltpu.make_async_copy(h_ref.at[pl.ds(i, 1)], xs_ref.at[pl.ds(d, 1)], sem.at[0]).start(priority=k)
        return carry

    lax.fori_loop(0, n_tok, issue, 0, unroll=8)
    for _ in range(2):
        pltpu.make_async_copy(h_ref, xs_ref.at[pl.ds(0, n_tok)], sem.at[0]).wait()


def _scatter_rows(dest, h2, xs):
    return pl.pallas_call(
        _scatter_rows_kernel,
        grid_spec=pltpu.PrefetchScalarGridSpec(
            num_scalar_prefetch=1, grid=(1,),
            in_specs=[pl.BlockSpec(h2.shape, lambda i, d: (0, 0)), pl.BlockSpec(memory_space=pl.ANY)],
            out_specs=pl.BlockSpec(memory_space=pl.ANY),
            scratch_shapes=[pltpu.SemaphoreType.DMA((1,))]),
        out_shape=jax.ShapeDtypeStruct(xs.shape, xs.dtype),
        input_output_aliases={2: 0},
        compiler_params=pltpu.CompilerParams(dimension_semantics=("arbitrary",)),
        name="scatter_rows",
    )(dest, h2, xs)


def _experts_kernel(be_ref, nu_ref, nv_ref, nxt_ref, par_ref, phys_ref, xs_ref, wg_ref, wu_ref, wd_ref, y_ref,
                    wg_s, wu_s, wd_s, wg_f, wu_f, wd_f, sem):
    del phys_ref
    j = pl.program_id(0)
    used = j < nu_ref[0]
    changed = jnp.logical_or(j == 0, be_ref[j] != be_ref[jnp.maximum(j - 1, 0)])

    def weight_copies(e, slot):
        return [pltpu.make_async_copy(src.at[e], dst.at[slot], sem.at[slot])
                for src, dst in ((wg_ref, wg_f), (wu_ref, wu_f), (wd_ref, wd_f))]

    @pl.when(jnp.logical_and(used, changed))
    def _():
        slot = par_ref[j]

        @pl.when(j == 0)
        def _():
            for c in weight_copies(be_ref[0], slot):
                c.start()

        for c in weight_copies(be_ref[j], slot):
            c.wait()

        @pl.when(nxt_ref[j] >= 0)
        def _():
            for c in weight_copies(nxt_ref[j], 1 - slot):
                c.start()

        wg_s[...] = wg_f[slot].astype(jnp.bfloat16)
        wu_s[...] = wu_f[slot].astype(jnp.bfloat16)
        wd_s[...] = wd_f[slot].astype(jnp.bfloat16)

    @pl.when(used)
    def _():
        hb = xs_ref.shape[0] // 2
        row = lax.broadcasted_iota(jnp.int32, (hb, 1), 0)

        def load(h):
            live = row + h * hb < nv_ref[j]
            return _unpack_bf16_pairs(jnp.where(live, xs_ref[h * hb:(h + 1) * hb, :], jnp.uint32(0)))

        def swiglu(g, u):
            return (g * jax.nn.sigmoid(g) * u).astype(jnp.bfloat16)

        x0 = load(0)
        g0 = jnp.dot(x0, wg_s[...], preferred_element_type=jnp.float32)
        u0 = jnp.dot(x0, wu_s[...], preferred_element_type=jnp.float32)
        x1 = load(1)
        g1 = jnp.dot(x1, wg_s[...], preferred_element_type=jnp.float32)
        a0 = swiglu(g0, u0)
        u1 = jnp.dot(x1, wu_s[...], preferred_element_type=jnp.float32)
        y0 = jnp.dot(a0, wd_s[...], preferred_element_type=jnp.float32)
        a1 = swiglu(g1, u1)
        y_ref[0:hb, :] = _pack_bf16_pairs(y0.astype(jnp.bfloat16))
        y1 = jnp.dot(a1, wd_s[...], preferred_element_type=jnp.float32)
        y_ref[hb:, :] = _pack_bf16_pairs(y1.astype(jnp.bfloat16))


def _experts(block_e, n_used, n_valid, next_e, parity, phys, xs, w_gate, w_up, w_down):
    bm = EXPERT_BLOCK
    n_blocks = xs.shape[0] // bm
    hbm = pl.BlockSpec(memory_space=pl.ANY)
    return pl.pallas_call(
        _experts_kernel,
        grid_spec=pltpu.PrefetchScalarGridSpec(
            num_scalar_prefetch=6, grid=(n_blocks,),
            in_specs=[pl.BlockSpec((bm, D_MODEL // 2), lambda j, be, nu, nv, nx, pa, ph: (ph[j], 0)),
                      hbm, hbm, hbm],
            out_specs=pl.BlockSpec((bm, D_MODEL // 2), lambda j, be, nu, nv, nx, pa, ph: (ph[j], 0)),
            scratch_shapes=[pltpu.VMEM((D_MODEL, EXPERT_HIDDEN), jnp.bfloat16),
                            pltpu.VMEM((D_MODEL, EXPERT_HIDDEN), jnp.bfloat16),
                            pltpu.VMEM((EXPERT_HIDDEN, D_MODEL), jnp.bfloat16),
                            pltpu.VMEM((2, D_MODEL, EXPERT_HIDDEN), jnp.float32),
                            pltpu.VMEM((2, D_MODEL, EXPERT_HIDDEN), jnp.float32),
                            pltpu.VMEM((2, EXPERT_HIDDEN, D_MODEL), jnp.float32),
                            pltpu.SemaphoreType.DMA((2,))]),
        out_shape=jax.ShapeDtypeStruct((xs.shape[0], D_MODEL // 2), jnp.uint32),
        compiler_params=pltpu.CompilerParams(dimension_semantics=("arbitrary",), vmem_limit_bytes=VMEM_LIMIT),
        name="experts",
    )(block_e, n_used, n_valid, next_e, parity, phys, xs, w_gate, w_up, w_down)


def _gather_rows(idx, yb):
    mesh = plsc.VectorSubcoreMesh(core_axis_name="core", subcore_axis_name="subcore")
    rows, window = SC_ROWS, SC_INDEX_WINDOW
    sub = window // rows
    n = idx.shape[1]
    width = yb.shape[1]
    assert n % window == 0

    @pl.kernel(out_type=jax.ShapeDtypeStruct((n, width), yb.dtype), mesh=mesh, scratch_types=[])
    def gather(y_hbm, i_hbm, o_hbm):
        def body(indices, i_vmem, o_vmem):
            pltpu.sync_copy(y_hbm.at[i_vmem.at[0, pl.ds(indices[1] * rows, rows)]], o_vmem)

        pltpu.emit_pipeline(
            body, grid=(n // window, sub),
            in_specs=[pl.BlockSpec((1, window), lambda i, j: (0, i))],
            out_specs=[pl.BlockSpec((rows, width), lambda i, j: (i * sub + j, 0))],
            core_axis_name=("core", "subcore"),
            dimension_semantics=(pltpu.PARALLEL, pltpu.ARBITRARY), _explicit_indices=True,
        )(i_hbm, o_hbm)

    return gather(yb, idx)


def _combine_kernel(x1_ref, gate_ref, rf_ref, fin_ref, y0_ref, y1_ref, *out_refs, per_token_gate):
    o_ref = out_refs[-1]
    gates = jnp.transpose(rf_ref[...])
    moe = _unpack_pairs_f32(y0_ref[...]) * gates[:, 0:1] + _unpack_pairs_f32(y1_ref[...]) * gates[:, 1:2]
    gate_f = gate_ref[...] if per_token_gate else gate_ref[...][5:6]
    o_ref[...] = _rms(x1_ref[...] + gate_f * moe, fin_ref[...])


def _combine(x1, gate, rf, final_norm, ys, out_prev, first_tile, n_tiles, *, per_token_gate):
    tc = COMBINE_TILE
    gate_spec = (pl.BlockSpec((tc, D_MODEL), lambda i: (first_tile + i, 0)) if per_token_gate
                 else pl.BlockSpec((6, D_MODEL), lambda i: (0, 0)))
    args = [x1, gate, rf, final_norm, ys, ys]
    in_specs = [pl.BlockSpec((tc, D_MODEL), lambda i: (first_tile + i, 0)),
                gate_spec,
                pl.BlockSpec((8, tc), lambda i: (0, first_tile + i)),
                pl.BlockSpec((1, D_MODEL), lambda i: (0, 0)),
                pl.BlockSpec((tc, D_MODEL // 2), lambda i: (i, 0)),
                pl.BlockSpec((tc, D_MODEL // 2), lambda i: (n_tiles + i, 0))]
    aliases = {}
    if out_prev is not None:
        args.append(out_prev)
        in_specs.append(pl.BlockSpec(memory_space=pl.ANY))
        aliases = {len(args) - 1: 0}
    return pl.pallas_call(
        functools.partial(_combine_kernel, per_token_gate=per_token_gate),
        grid=(n_tiles,), in_specs=in_specs,
        out_specs=pl.BlockSpec((tc, D_MODEL), lambda i: (first_tile + i, 0)),
        out_shape=jax.ShapeDtypeStruct(x1.shape, jnp.float32),
        input_output_aliases=aliases,
        compiler_params=pltpu.CompilerParams(dimension_semantics=("arbitrary",), vmem_limit_bytes=VMEM_LIMIT),
        name="combine",
    )(*args)


def _rope_tables(pos):
    half = HEAD_DIM // 2
    inv_freq = np.power(np.float32(ROPE_THETA), -(np.arange(half, dtype=np.float32) / np.float32(half)))
    ang = (pos.astype(np.float32)[:, None] * inv_freq[None, :].astype(np.float32)).astype(np.float64)
    cos, sin = np.cos(ang).astype(np.float32), np.sin(ang).astype(np.float32)
    cos = np.concatenate([cos, cos, cos, cos], axis=1)
    sin = np.concatenate([-sin, sin, -sin, sin], axis=1)
    return cos, sin


def _prompt_bias():
    r = np.arange(WINDOW)[:, None]
    j = np.arange(2 * WINDOW)[None, :]
    band = (j > r) & (j <= r + WINDOW)
    later = np.where(band, 0.0, NEG).astype(np.float32)
    first = np.where(band & (j >= WINDOW), 0.0, NEG).astype(np.float32)
    return jnp.asarray(np.stack([first, later]))


def _sample_bias(dec):
    cb = SAMPLE_CHUNK
    qb = np.repeat(np.arange(cb), dec)[:, None]
    qi = np.tile(np.arange(dec), cb)[:, None]
    kb = np.concatenate([np.repeat(np.arange(cb), WINDOW), np.repeat(np.arange(cb), dec)])[None, :]
    kj = np.concatenate([np.tile(np.arange(WINDOW), cb), np.tile(np.arange(dec), cb)])[None, :]
    is_new = (np.arange(cb * WINDOW + cb * dec) >= cb * WINDOW)[None, :]
    ok = (kb == qb) & np.where(is_new, kj <= qi, kj > qi)
    ok = np.pad(ok, ((0, 0), (0, SAMPLE_KEYS - ok.shape[1])))
    return jnp.asarray(np.where(ok, 0.0, NEG).astype(np.float32))


def kernel(x_prompt, x_sample, cache_k, cache_v, state_conv, c_prompt, c_sample, attn_norm, ffn_norm, w_mod, b_mod, w_in, conv_w, attn_sinks, out_norm_attn, out_norm_conv, w_out, w_group, b_group, w_expert, b_expert, w_gate, w_up, w_down, final_norm):
    assert attn_norm.shape[0] == 1, "one layer"
    bsz, seq, _ = x_prompt.shape
    nseq, dec, _ = x_sample.shape
    m_prompt = bsz * seq
    n_s = nseq * dec
    m_total = m_prompt + n_s
    tm = MIXER_TILE
    assert seq % tm == 0 and n_s % SAMPLE_TILE == 0 and SAMPLE_TILE % (SAMPLE_CHUNK * dec) == 0
    assert cache_k.shape[2] == WINDOW and m_prompt % SAMPLE_TILE == 0 and m_total % COMBINE_TILE == 0
    bf = jnp.bfloat16

    w_r = jnp.zeros((ROUTER_ROWS, D_MODEL), jnp.float32)
    w_r = w_r.at[0:N_GROUPS].set(w_group[0].T).at[8:8 + N_EXPERTS].set(w_expert[0].T).astype(bf)
    b_r = jnp.zeros((ROUTER_ROWS,), jnp.float32).at[0:N_GROUPS].set(b_group[0]).at[8:8 + N_EXPERTS].set(b_expert[0])
    b_r = jnp.broadcast_to(b_r[:, None], (ROUTER_ROWS, LANES))
    cos_p, sin_p = _rope_tables(np.arange(seq))
    cos_s, sin_s = _rope_tables(PAST_LEN + np.arange(dec))
    tri = jnp.asarray(np.triu(np.ones((tm, tm), np.float32), 1)).astype(bf)
    consts = dict(
        anorm=attn_norm[0][None], fnorm=ffn_norm[0][None], w_in=w_in[0].astype(bf),
        cos_p=jnp.asarray(cos_p), sin_p=jnp.asarray(sin_p),
        cos_s=jnp.asarray(np.tile(cos_s, (nseq, 1))), sin_s=jnp.asarray(np.tile(sin_s, (nseq, 1))),
        bias_p=_prompt_bias(), bias_s=_sample_bias(dec), conv_w=conv_w[0],
        gn_attn=out_norm_attn[0][None], gn_conv=out_norm_conv[0][None], w_out=w_out[0].astype(bf),
        w_r=w_r, b_r=b_r, tri=tri)
    sinks = attn_sinks[0]

    mod = _adaln(jnp.concatenate([c_prompt, c_sample], axis=0), w_mod[0], b_mod[0])
    mod_p = mod[:bsz].reshape(bsz, 6, D_MODEL)
    mod_s = mod[bsz:]

    bm = EXPERT_BLOCK
    eids = jnp.arange(N_EXPERTS, dtype=jnp.int32)
    lookup = lambda table, e: jnp.sum(jnp.where(e[..., None] == eids, table, 0), axis=-1)
    by_token = lambda a: jnp.transpose(a, (1, 0, 2)).reshape(8, -1)
    ceil_bm = lambda c: (c + bm - 1) // bm * bm
    x1_p, h2_p, ri_p, rf_p, cnt_p, k_p, v_p, u_p = _prompt_mixer(x_prompt, mod_p, sinks, consts)
    cnt_p = cnt_p[:, 0].astype(jnp.int32)
    padded_p = ceil_bm(cnt_p)
    pstart_p = jnp.cumsum(padded_p) - padded_p
    nb_p = -(-(2 * m_prompt + N_EXPERTS * (bm - 1)) // bm)
    nb_o = -(-(2 * n_s + N_EXPERTS * (bm - 1)) // bm)
    n_blocks = nb_p + nb_o
    ri_p = by_token(ri_p)
    dest_p = (lookup(pstart_p, ri_p[0:2]) + ri_p[2:4]).reshape(2 * m_prompt)

    st = state_conv[0]
    zrow = jnp.zeros((nseq, 1, CONV_DIM), jnp.float32)
    e1 = jnp.concatenate([st[:, 1:2]] + [zrow] * (dec - 1), axis=1).reshape(n_s, CONV_DIM)
    e2 = jnp.concatenate([st[:, 0:1], st[:, 1:2]] + [zrow] * (dec - 2), axis=1).reshape(n_s, CONV_DIM)
    dest_p, e1 = lax.optimization_barrier((dest_p, e1))
    xs = _dispatch(dest_p[None], h2_p, n_blocks * bm)
    ck = jnp.transpose(cache_k[0], (0, 2, 3, 1)).reshape(nseq * KV_DIM, WINDOW)
    cv = jnp.transpose(cache_v[0], (0, 2, 3, 1)).reshape(nseq * KV_DIM, WINDOW)
    x1_s, h2_s, ri_s, rf_s, cnt_s, k_s, v_s, u_s = _sample_mixer(
        x_sample.reshape(n_s, D_MODEL), mod_s, sinks, consts, ck, cv, e1, e2,
        jnp.zeros((N_EXPERTS, LANES), jnp.float32), dec)

    cnt_s = cnt_s[:, 0].astype(jnp.int32)
    fit = jnp.minimum(cnt_s, padded_p - cnt_p)
    over = cnt_s - fit
    padded_o = ceil_bm(over)
    ostart = nb_p * bm + jnp.cumsum(padded_o) - padded_o
    ri_s = by_token(ri_s)
    e_s, r_s = ri_s[0:2], ri_s[2:4]
    dest_s = jnp.where(r_s < lookup(fit, e_s), lookup(pstart_p + cnt_p, e_s) + r_s, lookup(ostart - fit, e_s) + r_s)
    dest_s = dest_s.reshape(2 * n_s)

    from_t = lambda a: jnp.transpose(a.reshape(nseq, N_KV_HEADS, HEAD_DIM, WINDOW), (0, 3, 1, 2))[None]
    new_k_s, new_v_s = from_t(k_s), from_t(v_s)
    xs = _scatter_rows(dest_s, h2_s, xs)

    np_e, no_e = padded_p // bm, padded_o // bm
    lend = jnp.cumsum(np_e + no_e)
    lstart = lend - np_e - no_e
    n_used = lend[-1].astype(jnp.int32)
    blk = jnp.arange(n_blocks, dtype=jnp.int32)
    present = np_e + no_e > 0
    last_e = jnp.max(jnp.where(present, eids, 0))
    block_e = jnp.sum((lend[None, :] <= blk[:, None]).astype(jnp.int32), axis=1)
    block_e = jnp.where(blk < n_used, jnp.minimum(block_e, N_EXPERTS - 1), last_e)
    of_block = lambda table: lookup(table, block_e)
    within = blk - of_block(lstart)
    in_prompt = within < of_block(np_e)
    phys = jnp.where(in_prompt, of_block(pstart_p) // bm + within, of_block(ostart) // bm + within - of_block(np_e))
    n_valid = jnp.where(in_prompt, of_block(cnt_p + fit) - within * bm, of_block(over) - (within - of_block(np_e)) * bm)
    n_valid = jnp.where(blk < n_used, jnp.clip(n_valid, 0, bm), 0).astype(jnp.int32)
    phys = jnp.where(blk < n_used, phys, jnp.sum(jnp.where(blk == n_used - 1, phys, 0))).astype(jnp.int32)
    later = jnp.logical_and(present[None, :], eids[None, :] > eids[:, None])
    next_of = jnp.min(jnp.where(later, eids[None, :], N_EXPERTS), axis=1)
    next_of = jnp.where(next_of == N_EXPERTS, -1, next_of)
    next_e = of_block(next_of).astype(jnp.int32)
    parity = of_block((jnp.cumsum(present.astype(jnp.int32)) - 1) % 2).astype(jnp.int32)
    yb = _experts(block_e, n_used.reshape(1), n_valid, next_e, parity, phys, xs, w_gate[0], w_up[0], w_down[0])

    fin = final_norm[None]
    tc = COMBINE_TILE
    rf_p, rf_s = by_token(rf_p), by_token(rf_s)
    y_p = None
    chunk = min(COMBINE_CHUNK, seq)
    assert seq % chunk == 0 and chunk % tc == 0
    head = chunk // 4 if (chunk // 4) % tc == 0 else chunk
    edges = sorted({0, head, *range(chunk, m_prompt + 1, chunk)})
    for lo, hi in zip(edges[:-1], edges[1:]):
        idx = jnp.concatenate([dest_p[lo:hi], dest_p[m_prompt + lo:m_prompt + hi]])[None]
        y_p = _combine(x1_p, mod_p[lo // seq], rf_p, fin, _gather_rows(idx, yb), y_p, lo // tc, (hi - lo) // tc,
                       per_token_gate=False)
    gate_s = jnp.repeat(mod_s[:, 5 * D_MODEL:], dec, axis=0)
    y_s = _combine(x1_s, gate_s, rf_s, fin, _gather_rows(dest_s[None], yb), None, 0, n_s // tc, per_token_gate=True)

    keep = WINDOW
    new_k_p = k_p.reshape(1, bsz, keep, N_KV_HEADS, HEAD_DIM)
    new_v_p = v_p.reshape(1, bsz, keep, N_KV_HEADS, HEAD_DIM)
    new_u_p = u_p[:, 6:8][None]
    u_all = jnp.concatenate([st, u_s.reshape(nseq, dec, CONV_DIM)], axis=1)
    new_u_s = u_all[:, -2:][None]
    return (y_p.reshape(bsz, seq, D_MODEL), y_s.reshape(nseq, dec, D_MODEL),
            new_k_p, new_v_p, new_u_p, new_k_s, new_v_s, new_u_s)
```

```python
import functools

import jax
import jax.numpy as jnp
import numpy as np
from jax import lax
from jax.experimental import pallas as pl
from jax.experimental.pallas import tpu as pltpu
from jax.experimental.pallas import tpu_sc as plsc

D_MODEL = 1024
HEAD_DIM = 64
ATTN_DIM = 512
N_Q_HEADS = 8
N_KV_HEADS = 2
KV_DIM = 128
CONV_DIM = 512
WINDOW = 128
IN_PROJ_DIM = ATTN_DIM + 2 * KV_DIM + 3 * CONV_DIM
N_GROUPS = 4
EXPERTS_PER_GROUP = 8
N_EXPERTS = 32
EXPERT_HIDDEN = 512
RMS_EPS = 1e-5
ROPE_THETA = 10000.0
PAST_LEN = 16384

LANES = 128
ROUTER_ROWS = 48
NEG = -1e30

MIXER_TILE = 512
SAMPLE_TILE = 128
SAMPLE_CHUNK = 8
SAMPLE_KEYS = (SAMPLE_CHUNK + 1) * WINDOW
EXPERT_BLOCK = 512
COMBINE_TILE = 512
COMBINE_CHUNK = 4096
SC_ROWS = 32
SC_INDEX_WINDOW = 128
VMEM_LIMIT = 56 * 1024 * 1024


def _rms(x, gain):
    ms = jnp.mean(x * x, axis=-1, keepdims=True)
    return x * lax.rsqrt(ms + RMS_EPS) * gain


def _half_lane_mask(rows=1):
    return lax.broadcasted_iota(jnp.int32, (rows, LANES), 1) < HEAD_DIM


def _rope(x, cos, sin_signed):
    t, w = x.shape
    reps = w // LANES
    lane = lax.broadcasted_iota(jnp.int32, (t, w), 1)
    upper = (lane % HEAD_DIM) >= (HEAD_DIM // 2)
    partner = jnp.where(upper, pltpu.roll(x, HEAD_DIM // 2, 1), pltpu.roll(x, w - HEAD_DIM // 2, 1))
    if reps > 1:
        cos = jnp.concatenate([cos] * reps, axis=1)
        sin_signed = jnp.concatenate([sin_signed] * reps, axis=1)
    return x * cos + partner * sin_signed


def _swap_halves(x):
    return pltpu.roll(x, HEAD_DIM, 1)


def _group_norm64(x, gain):
    t, w = x.shape
    lo = _half_lane_mask(t)
    outs = []
    for c in range(w // LANES):
        xc = x[:, c * LANES:(c + 1) * LANES]
        sq = xc * xc
        s_lo = jnp.sum(jnp.where(lo, sq, 0.0), axis=-1, keepdims=True)
        s_hi = jnp.sum(jnp.where(lo, 0.0, sq), axis=-1, keepdims=True)
        r = jnp.where(lo, lax.rsqrt(s_lo * (1.0 / HEAD_DIM) + RMS_EPS), lax.rsqrt(s_hi * (1.0 / HEAD_DIM) + RMS_EPS))
        outs.append(xc * r)
    return jnp.concatenate(outs, axis=1) * gain


def _attend(q_rows, keys, keys_sw, vals, vals_sw, bias, sinks_ref, group, transposed=False):
    r = q_rows.shape[0]
    lo = _half_lane_mask(r)
    outs = []
    for parity in range(2):
        heads = [4 * group + parity, 4 * group + parity + 2]
        kk, vv = (keys, vals) if parity == group else (keys_sw, vals_sw)
        half = lo if parity == 0 else jnp.logical_not(lo)
        qs = []
        for h in heads:
            pair = q_rows[:, (h // 2) * LANES:(h // 2 + 1) * LANES]
            qs.append(jnp.where(half, pair, jnp.zeros_like(pair)))
        qcat = jnp.concatenate(qs, axis=0)
        s = (jnp.dot(qcat, kk, preferred_element_type=jnp.float32) if transposed else
             lax.dot_general(qcat, kk, (((1,), (1,)), ((), ())), preferred_element_type=jnp.float32))
        ps, dens = [], []
        for i, h in enumerate(heads):
            sh = s[i * r:(i + 1) * r] + bias
            sink = sinks_ref[h]
            m = jnp.maximum(jnp.max(sh, axis=-1, keepdims=True), sink)
            p = jnp.exp(sh - m)
            dens.append(jnp.sum(p, axis=-1, keepdims=True) + jnp.exp(sink - m))
            ps.append(p.astype(jnp.bfloat16))
        pcat = jnp.concatenate(ps, axis=0)
        o = (lax.dot_general(pcat, vv, (((1,), (1,)), ((), ())), preferred_element_type=jnp.float32) if transposed
             else jnp.dot(pcat, vv, preferred_element_type=jnp.float32))
        for i, h in enumerate(heads):
            oh = o[i * r:(i + 1) * r] / dens[i]
            ms = jnp.sum(jnp.where(half, oh * oh, 0.0), axis=-1, keepdims=True) * (1.0 / HEAD_DIM)
            outs.append((h, oh * lax.rsqrt(ms + RMS_EPS)))
    outs.sort(key=lambda t: t[0])
    return [o for _, o in outs]


def _merge_heads(head_outs, lo):
    pairs = [jnp.where(lo, head_outs[2 * j], head_outs[2 * j + 1]) for j in range(4)]
    return jnp.concatenate(pairs, axis=1)


def _route(h2b, wr_ref, br_ref, tri_ref, cnt_ref, ri_ref, rf_ref):
    t = h2b.shape[0]
    lg = lax.dot_general(wr_ref[...], h2b, (((1,), (1,)), ((), ())), preferred_element_type=jnp.float32)
    lg = lg + br_ref[...][:, 0:1]
    r8 = lax.broadcasted_iota(jnp.int32, (8, t), 0)
    lgrp = jnp.where(r8 < N_GROUPS, lg[0:8], -jnp.inf)
    gmax = jnp.max(lgrp, axis=0, keepdims=True)
    grp = jnp.min(jnp.where(lgrp == gmax, r8, 8), axis=0, keepdims=True)
    pg_sel = 1.0 / jnp.sum(jnp.exp(lgrp - gmax), axis=0, keepdims=True)
    r32 = lax.broadcasted_iota(jnp.int32, (N_EXPERTS, t), 0)
    le = jnp.where((r32 // EXPERTS_PER_GROUP) == grp, lg[8:8 + N_EXPERTS], -jnp.inf)
    v1 = jnp.max(le, axis=0, keepdims=True)
    i1 = jnp.min(jnp.where(le == v1, r32, N_EXPERTS), axis=0, keepdims=True)
    le2 = jnp.where(r32 == i1, -jnp.inf, le)
    v2 = jnp.max(le2, axis=0, keepdims=True)
    i2 = jnp.min(jnp.where(le2 == v2, r32, N_EXPERTS), axis=0, keepdims=True)
    e21 = jnp.exp(v2 - v1)
    gate1 = pg_sel / (1.0 + e21)
    gate2 = pg_sel * e21 / (1.0 + e21)
    hot1 = r32 == i1
    hot2 = r32 == i2
    onehot = jnp.where(jnp.logical_or(hot1, hot2), 1.0, 0.0)
    before = jnp.dot(onehot.astype(jnp.bfloat16), tri_ref[...], preferred_element_type=jnp.float32)
    pos = before + cnt_ref[...][:, 0:1]
    rank1 = jnp.sum(jnp.where(hot1, pos, 0.0), axis=0, keepdims=True)
    rank2 = jnp.sum(jnp.where(hot2, pos, 0.0), axis=0, keepdims=True)
    cnt_ref[...] = cnt_ref[...] + jnp.sum(onehot, axis=1, keepdims=True)
    zi = jnp.zeros((4, t), jnp.int32)
    ri_ref[...] = jnp.concatenate([i1, i2, rank1.astype(jnp.int32), rank2.astype(jnp.int32), zi], axis=0)
    rf_ref[...] = jnp.concatenate([gate1, gate2, jnp.zeros((6, t), jnp.float32)], axis=0)


def _pack_bf16_pairs(xb):
    w = xb.shape[1] // 2
    bits = pltpu.bitcast(xb.astype(jnp.float32), jnp.uint32)
    return (bits[:, :w] >> 16) | (bits[:, w:] & jnp.uint32(0xFFFF0000))


def _unpack_pairs_f32(words):
    lo = pltpu.bitcast(words << 16, jnp.float32)
    hi = pltpu.bitcast(words & jnp.uint32(0xFFFF0000), jnp.float32)
    return jnp.concatenate([lo, hi], axis=1)


def _unpack_bf16_pairs(words):
    return _unpack_pairs_f32(words).astype(jnp.bfloat16)


def _split_proj(proj):
    a = ATTN_DIM
    q = proj[:, :a]
    k = proj[:, a:a + KV_DIM]
    v = proj[:, a + KV_DIM:a + 2 * KV_DIM]
    c0 = a + 2 * KV_DIM
    h_conv = proj[:, c0:c0 + CONV_DIM]
    gate_b = proj[:, c0 + CONV_DIM:c0 + 2 * CONV_DIM]
    gate_c = proj[:, c0 + 2 * CONV_DIM:c0 + 3 * CONV_DIM]
    return q, k, v, gate_c * h_conv, gate_b


def _mixer_tail(x, attn_n, conv_n, mod, wout_ref, fnorm_ref, wr_ref, br_ref, tri_ref, cnt_ref,
                x1_ref, h2_ref, ri_ref, rf_ref):
    gate_a, shift_f, scale_f = mod
    cat = jnp.concatenate([attn_n.astype(jnp.bfloat16), conv_n.astype(jnp.bfloat16)], axis=1)
    mix = jnp.dot(cat, wout_ref[...], preferred_element_type=jnp.float32)
    x1 = x + gate_a * mix
    x1_ref[...] = x1
    h2 = _rms(x1, fnorm_ref[...] * (1.0 + scale_f)) + shift_f
    h2b = h2.astype(jnp.bfloat16)
    h2_ref[...] = _pack_bf16_pairs(h2b)
    _route(h2b, wr_ref, br_ref, tri_ref, cnt_ref, ri_ref, rf_ref)


def _adaln_kernel(c_ref, w_ref, b_ref, o_ref):
    c = c_ref[...]
    a = (c * jax.nn.sigmoid(c)).astype(jnp.bfloat16)
    o_ref[...] = jnp.dot(a, w_ref[...].astype(jnp.bfloat16), preferred_element_type=jnp.float32) + b_ref[...]


def _adaln(c, w_mod, b_mod):
    n = c.shape[0]
    tn = 1536
    return pl.pallas_call(
        _adaln_kernel,
        grid=(w_mod.shape[1] // tn,),
        in_specs=[pl.BlockSpec((n, D_MODEL), lambda j: (0, 0)),
                  pl.BlockSpec((D_MODEL, tn), lambda j: (0, j)),
                  pl.BlockSpec((1, tn), lambda j: (0, j))],
        out_specs=pl.BlockSpec((n, tn), lambda j: (0, j)),
        out_shape=jax.ShapeDtypeStruct((n, w_mod.shape[1]), jnp.float32),
        compiler_params=pltpu.CompilerParams(dimension_semantics=("arbitrary",), vmem_limit_bytes=VMEM_LIMIT),
        name="adaln",
    )(c, w_mod, b_mod.reshape(1, -1))


def _prompt_mixer_kernel(sinks_ref, x_ref, mod_ref, anorm_ref, fnorm_ref, win_ref, cos_ref, sin_ref, bias_ref,
                         convw_ref, gna_ref, gnc_ref, wout_ref, wr_ref, br_ref, tri_ref,
                         x1_ref, h2_ref, ri_ref, rf_ref, cnt_out_ref, knew_ref, vnew_ref, unew_ref,
                         q_s, k_s, ksw_s, v_s, vsw_s, u_s, att_s, cnt_s):
    b = pl.program_id(0)
    t = pl.program_id(1)
    tm = MIXER_TILE
    nblk = tm // WINDOW

    @pl.when(jnp.logical_and(b == 0, t == 0))
    def _():
        cnt_s[...] = jnp.zeros_like(cnt_s)

    @pl.when(t == 0)
    def _():
        z = jnp.zeros((WINDOW, KV_DIM), jnp.bfloat16)
        k_s[0:WINDOW, :] = z
        ksw_s[0:WINDOW, :] = z
        v_s[0:WINDOW, :] = z
        vsw_s[0:WINDOW, :] = z
        u_s[0:8, :] = jnp.zeros((8, CONV_DIM), jnp.float32)

    x = x_ref[...]
    mod = mod_ref[...]
    h = _rms(x, anorm_ref[...] * (1.0 + mod[1:2])) + mod[0:1]
    proj = jnp.dot(h.astype(jnp.bfloat16), win_ref[...], preferred_element_type=jnp.float32)
    q, k, v, u, gate_b = _split_proj(proj)
    cos = cos_ref[...]
    sin = sin_ref[...]
    q = _rope(q, cos, sin) * (HEAD_DIM ** -0.5)
    k = _rope(k, cos, sin)
    q_s[...] = q.astype(jnp.bfloat16)
    k_s[WINDOW:, :] = k.astype(jnp.bfloat16)
    ksw_s[WINDOW:, :] = _swap_halves(k).astype(jnp.bfloat16)
    v_s[WINDOW:, :] = v.astype(jnp.bfloat16)
    vsw_s[WINDOW:, :] = _swap_halves(v).astype(jnp.bfloat16)
    knew_ref[...] = k[tm - WINDOW:, :]
    vnew_ref[...] = v[tm - WINDOW:, :]

    u_s[8:, :] = u
    unew_ref[...] = u[tm - 8:, :]
    cw = convw_ref[...]
    conv = (u_s[6:6 + tm, :] * cw[0:1] + u_s[7:7 + tm, :] * cw[1:2] + u * cw[2:3]) * gate_b
    u_s[0:8, :] = u[tm - 8:, :]
    conv_n = _group_norm64(conv, gnc_ref[...])

    lo = _half_lane_mask(WINDOW)
    first = t == 0

    def block(i, carry):
        r0 = pl.multiple_of(i * WINDOW, WINDOW)
        bias = bias_ref[jnp.where(jnp.logical_and(first, i == 0), 0, 1)]
        qb = q_s[pl.ds(r0, WINDOW), :]
        keys = k_s[pl.ds(r0, 2 * WINDOW), :]
        keys_sw = ksw_s[pl.ds(r0, 2 * WINDOW), :]
        vals = v_s[pl.ds(r0, 2 * WINDOW), :]
        vals_sw = vsw_s[pl.ds(r0, 2 * WINDOW), :]
        outs = []
        for g in range(N_KV_HEADS):
            outs += _attend(qb, keys, keys_sw, vals, vals_sw, bias, sinks_ref, g)
        att_s[pl.ds(r0, WINDOW), :] = _merge_heads(outs, lo)
        return carry

    lax.fori_loop(0, nblk, block, 0, unroll=True)
    k_s[0:WINDOW, :] = k_s[tm:tm + WINDOW, :]
    ksw_s[0:WINDOW, :] = ksw_s[tm:tm + WINDOW, :]
    v_s[0:WINDOW, :] = v_s[tm:tm + WINDOW, :]
    vsw_s[0:WINDOW, :] = vsw_s[tm:tm + WINDOW, :]

    attn_n = att_s[...] * gna_ref[...]
    _mixer_tail(x, attn_n, conv_n, (mod[2:3], mod[3:4], mod[4:5]), wout_ref, fnorm_ref, wr_ref, br_ref, tri_ref,
                cnt_s, x1_ref, h2_ref, ri_ref, rf_ref)
    cnt_out_ref[...] = cnt_s[...]


def _prompt_mixer(x, mod, sinks, consts):
    bsz, seq, _ = x.shape
    tm = MIXER_TILE
    nt = seq // tm
    full = lambda shape: pl.BlockSpec(shape, lambda b, t, s: (0,) * len(shape))
    in_specs = [
        pl.BlockSpec((None, tm, D_MODEL), lambda b, t, s: (b, t, 0)),
        pl.BlockSpec((None, 6, D_MODEL), lambda b, t, s: (b, 0, 0)),
        full((1, D_MODEL)), full((1, D_MODEL)),
        full((D_MODEL, IN_PROJ_DIM)),
        pl.BlockSpec((tm, LANES), lambda b, t, s: (t, 0)),
        pl.BlockSpec((tm, LANES), lambda b, t, s: (t, 0)),
        full((2, WINDOW, 2 * WINDOW)),
        full((3, CONV_DIM)), full((1, ATTN_DIM)), full((1, CONV_DIM)),
        full((D_MODEL, D_MODEL)),
        full((ROUTER_ROWS, D_MODEL)), full((ROUTER_ROWS, LANES)),
        full((tm, tm)),
    ]
    out_shape = [
        jax.ShapeDtypeStruct((bsz * seq, D_MODEL), jnp.float32),
        jax.ShapeDtypeStruct((bsz * seq, D_MODEL // 2), jnp.uint32),
        jax.ShapeDtypeStruct((bsz * nt, 8, tm), jnp.int32),
        jax.ShapeDtypeStruct((bsz * nt, 8, tm), jnp.float32),
        jax.ShapeDtypeStruct((N_EXPERTS, LANES), jnp.float32),
        jax.ShapeDtypeStruct((bsz, WINDOW, KV_DIM), jnp.float32),
        jax.ShapeDtypeStruct((bsz, WINDOW, KV_DIM), jnp.float32),
        jax.ShapeDtypeStruct((bsz, 8, CONV_DIM), jnp.float32),
    ]
    out_specs = [
        pl.BlockSpec((tm, D_MODEL), lambda b, t, s: (b * nt + t, 0)),
        pl.BlockSpec((tm, D_MODEL // 2), lambda b, t, s: (b * nt + t, 0)),
        pl.BlockSpec((None, 8, tm), lambda b, t, s: (b * nt + t, 0, 0)),
        pl.BlockSpec((None, 8, tm), lambda b, t, s: (b * nt + t, 0, 0)),
        pl.BlockSpec((N_EXPERTS, LANES), lambda b, t, s: (0, 0)),
        pl.BlockSpec((None, WINDOW, KV_DIM), lambda b, t, s: (b, 0, 0)),
        pl.BlockSpec((None, WINDOW, KV_DIM), lambda b, t, s: (b, 0, 0)),
        pl.BlockSpec((None, 8, CONV_DIM), lambda b, t, s: (b, 0, 0)),
    ]
    scratch = [
        pltpu.VMEM((tm, ATTN_DIM), jnp.bfloat16),
        pltpu.VMEM((tm + WINDOW, KV_DIM), jnp.bfloat16), pltpu.VMEM((tm + WINDOW, KV_DIM), jnp.bfloat16),
        pltpu.VMEM((tm + WINDOW, KV_DIM), jnp.bfloat16), pltpu.VMEM((tm + WINDOW, KV_DIM), jnp.bfloat16),
        pltpu.VMEM((tm + 8, CONV_DIM), jnp.float32),
        pltpu.VMEM((tm, ATTN_DIM), jnp.float32),
        pltpu.VMEM((N_EXPERTS, LANES), jnp.float32),
    ]
    return pl.pallas_call(
        _prompt_mixer_kernel,
        grid_spec=pltpu.PrefetchScalarGridSpec(
            num_scalar_prefetch=1, grid=(bsz, nt), in_specs=in_specs, out_specs=out_specs, scratch_shapes=scratch),
        out_shape=out_shape,
        compiler_params=pltpu.CompilerParams(dimension_semantics=("arbitrary", "arbitrary"),
                                             vmem_limit_bytes=VMEM_LIMIT),
        name="prompt_mixer",
    )(sinks, x, mod, consts["anorm"], consts["fnorm"], consts["w_in"], consts["cos_p"], consts["sin_p"],
      consts["bias_p"], consts["conv_w"], consts["gn_attn"], consts["gn_conv"], consts["w_out"],
      consts["w_r"], consts["b_r"], consts["tri"])


def _sample_mixer_kernel(sinks_ref, x_ref, mod_ref, anorm_ref, fnorm_ref, win_ref, cos_ref, sin_ref, bias_ref,
                         convw_ref, gna_ref, gnc_ref, wout_ref, wr_ref, br_ref, tri_ref,
                         ck_ref, cv_ref, e1_ref, e2_ref, cnt_in_ref,
                         x1_ref, h2_ref, ri_ref, rf_ref, cnt_out_ref, knew_ref, vnew_ref, unew_ref,
                         q_s, k_s, v_s, att_s, cnt_s, *, dec):
    n = x_ref.shape[0]
    cb = SAMPLE_CHUNK
    rows = cb * dec

    @pl.when(pl.program_id(0) == 0)
    def _():
        cnt_s[...] = cnt_in_ref[...]

    x = x_ref[...]
    mod = jnp.repeat(mod_ref[...], dec, axis=0)
    md = lambda j: mod[:, j * D_MODEL:(j + 1) * D_MODEL]
    h = _rms(x, anorm_ref[...]) * (1.0 + md(1)) + md(0)
    proj = jnp.dot(h.astype(jnp.bfloat16), win_ref[...], preferred_element_type=jnp.float32)
    q, k, v, u, gate_b = _split_proj(proj)
    cos = cos_ref[...]
    sin = sin_ref[...]
    q = _rope(q, cos, sin) * (HEAD_DIM ** -0.5)
    k = _rope(k, cos, sin)
    q_s[...] = q.astype(jnp.bfloat16)
    k_s[...] = k
    v_s[...] = v
    unew_ref[...] = u

    pos = lax.broadcasted_iota(jnp.int32, (n, CONV_DIM), 0) % dec
    um1 = jnp.where(pos >= 1, pltpu.roll(u, 1, 0), 0.0) + e1_ref[...]
    um2 = jnp.where(pos >= 2, pltpu.roll(u, 2, 0), 0.0) + e2_ref[...]
    cw = convw_ref[...]
    conv = (um2 * cw[0:1] + um1 * cw[1:2] + u * cw[2:3]) * gate_b
    conv_n = _group_norm64(conv, gnc_ref[...])

    lo = _half_lane_mask(rows)
    bias = bias_ref[...]

    half = KV_DIM // 2
    lane = lax.broadcasted_iota(jnp.int32, (KV_DIM, WINDOW), 1)
    swap_rows = lambda t: jnp.concatenate([t[half:], t[:half]], axis=0)

    for c in range(n // rows):
        r0 = c * rows
        qb = q_s[r0:r0 + rows, :]
        cats = []
        for ref, new_rows, out_ref in ((ck_ref, k_s[r0:r0 + rows, :], knew_ref), (cv_ref, v_s[r0:r0 + rows, :], vnew_ref)):
            tiles = [ref[(c * cb + b) * KV_DIM:(c * cb + b + 1) * KV_DIM, :] for b in range(cb)]
            new_t = jnp.concatenate([jnp.transpose(new_rows), jnp.zeros((KV_DIM, WINDOW - rows), jnp.float32)], axis=1)
            cat = jnp.concatenate(tiles + [new_t], axis=1)
            cats.append((cat.astype(jnp.bfloat16), swap_rows(cat).astype(jnp.bfloat16)))
            for b in range(cb):
                kept = pltpu.roll(tiles[b], WINDOW - dec, 1)
                fresh = pltpu.roll(new_t, (WINDOW - dec - b * dec) % WINDOW, 1)
                out_ref[(c * cb + b) * KV_DIM:(c * cb + b + 1) * KV_DIM, :] = jnp.where(lane < WINDOW - dec, kept, fresh)
        (keys, keys_sw), (vals, vals_sw) = cats
        outs = []
        for g in range(N_KV_HEADS):
            outs += _attend(qb, keys, keys_sw, vals, vals_sw, bias, sinks_ref, g, transposed=True)
        att_s[r0:r0 + rows, :] = _merge_heads(outs, lo)

    attn_n = att_s[...] * gna_ref[...]
    _mixer_tail(x, attn_n, conv_n, (md(2), md(3), md(4)), wout_ref, fnorm_ref, wr_ref, br_ref, tri_ref,
                cnt_s, x1_ref, h2_ref, ri_ref, rf_ref)
    cnt_out_ref[...] = cnt_s[...]


def _sample_mixer(x, mod, sinks, consts, ck, cv, e1, e2, cnt_in, dec):
    n_s = x.shape[0]
    ts = SAMPLE_TILE
    seqs = ts // dec
    full = lambda a: pl.BlockSpec(a.shape, lambda i, s: (0,) * a.ndim)
    rows = lambda width: pl.BlockSpec((ts, width), lambda i, s: (i, 0))
    in_specs = [
        rows(D_MODEL), pl.BlockSpec((seqs, 6 * D_MODEL), lambda i, s: (i, 0)),
        full(consts["anorm"]), full(consts["fnorm"]), full(consts["w_in"]),
        rows(LANES), rows(LANES), full(consts["bias_s"]),
        full(consts["conv_w"]), full(consts["gn_attn"]), full(consts["gn_conv"]), full(consts["w_out"]),
        full(consts["w_r"]), full(consts["b_r"]),
        pl.BlockSpec((ts, ts), lambda i, s: (0, 0)),
        pl.BlockSpec((seqs * KV_DIM, WINDOW), lambda i, s: (i, 0)),
        pl.BlockSpec((seqs * KV_DIM, WINDOW), lambda i, s: (i, 0)),
        rows(CONV_DIM), rows(CONV_DIM), full(cnt_in),
    ]
    args = [x, mod, consts["anorm"], consts["fnorm"], consts["w_in"], consts["cos_s"], consts["sin_s"],
            consts["bias_s"], consts["conv_w"], consts["gn_attn"], consts["gn_conv"], consts["w_out"],
            consts["w_r"], consts["b_r"], consts["tri"], ck, cv, e1, e2, cnt_in]
    nt = n_s // ts
    out_shape = [
        jax.ShapeDtypeStruct((n_s, D_MODEL), jnp.float32),
        jax.ShapeDtypeStruct((n_s, D_MODEL // 2), jnp.uint32),
        jax.ShapeDtypeStruct((nt, 8, ts), jnp.int32),
        jax.ShapeDtypeStruct((nt, 8, ts), jnp.float32),
        jax.ShapeDtypeStruct((N_EXPERTS, LANES), jnp.float32),
        jax.ShapeDtypeStruct(ck.shape, jnp.float32),
        jax.ShapeDtypeStruct(cv.shape, jnp.float32),
        jax.ShapeDtypeStruct((n_s, CONV_DIM), jnp.float32),
    ]
    out_specs = [
        rows(D_MODEL),
        rows(D_MODEL // 2),
        pl.BlockSpec((None, 8, ts), lambda i, s: (i, 0, 0)),
        pl.BlockSpec((None, 8, ts), lambda i, s: (i, 0, 0)),
        pl.BlockSpec((N_EXPERTS, LANES), lambda i, s: (0, 0)),
        pl.BlockSpec((seqs * KV_DIM, WINDOW), lambda i, s: (i, 0)),
        pl.BlockSpec((seqs * KV_DIM, WINDOW), lambda i, s: (i, 0)),
        rows(CONV_DIM),
    ]
    scratch = [
        pltpu.VMEM((ts, ATTN_DIM), jnp.bfloat16),
        pltpu.VMEM((ts, KV_DIM), jnp.float32), pltpu.VMEM((ts, KV_DIM), jnp.float32),
        pltpu.VMEM((ts, ATTN_DIM), jnp.float32),
        pltpu.VMEM((N_EXPERTS, LANES), jnp.float32),
    ]
    return pl.pallas_call(
        functools.partial(_sample_mixer_kernel, dec=dec),
        grid_spec=pltpu.PrefetchScalarGridSpec(
            num_scalar_prefetch=1, grid=(nt,), in_specs=in_specs, out_specs=out_specs, scratch_shapes=scratch),
        out_shape=out_shape,
        compiler_params=pltpu.CompilerParams(dimension_semantics=("arbitrary",), vmem_limit_bytes=VMEM_LIMIT),
        name="sample_mixer",
    )(sinks, *args)


def _dispatch(idx, h2, n_rows):
    mesh = plsc.VectorSubcoreMesh(core_axis_name="core", subcore_axis_name="subcore")
    rows, window = SC_ROWS, SC_INDEX_WINDOW
    sub = window // rows
    n_tok, width = h2.shape
    assert n_tok % rows == 0 and (2 * n_tok) % window == 0

    moved = 3 * n_tok * width * h2.dtype.itemsize
    @pl.kernel(out_type=jax.ShapeDtypeStruct((n_rows, width), h2.dtype), mesh=mesh, scratch_types=[],
               cost_estimate=pl.CostEstimate(flops=0, transcendentals=0, bytes_accessed=moved))
    def scatter(h_hbm, i_hbm, xs_hbm):
        def body(indices, h_vmem, i_vmem):
            pltpu.sync_copy(h_vmem, xs_hbm.at[i_vmem.at[0, pl.ds(indices[1] * rows, rows)]])

        pltpu.emit_pipeline(
            body, grid=(2 * n_tok // window, sub),
            in_specs=[pl.BlockSpec((rows, width), lambda i, j: ((i * sub + j) % (n_tok // rows), 0)),
                      pl.BlockSpec((1, window), lambda i, j: (0, i))],
            out_specs=[], core_axis_name=("core", "subcore"),
            dimension_semantics=(pltpu.PARALLEL, pltpu.ARBITRARY), _explicit_indices=True,
        )(h_hbm, i_hbm)

    return scatter(h2, idx)


def _scatter_rows_kernel(dest_ref, h_ref, xs_in_ref, xs_ref, sem):
    del xs_in_ref
    n_tok = h_ref.shape[0]

    def issue(i, carry):
        for k in range(2):
            d = dest_ref[k * n_tok + i]
            pltpu.make_async_copy(h_ref.at[pl.ds(i, 1)], xs_ref.at[pl.ds(d, 1)], sem.at[0]).start(priority=k)
        return carry

    lax.fori_loop(0, n_tok, issue, 0, unroll=8)
    for _ in range(2):
        pltpu.make_async_copy(h_ref, xs_ref.at[pl.ds(0, n_tok)], sem.at[0]).wait()


def _scatter_rows(dest, h2, xs):
    return pl.pallas_call(
        _scatter_rows_kernel,
        grid_spec=pltpu.PrefetchScalarGridSpec(
            num_scalar_prefetch=1, grid=(1,),
            in_specs=[pl.BlockSpec(h2.shape, lambda i, d: (0, 0)), pl.BlockSpec(memory_space=pl.ANY)],
            out_specs=pl.BlockSpec(memory_space=pl.ANY),
            scratch_shapes=[pltpu.SemaphoreType.DMA((1,))]),
        out_shape=jax.ShapeDtypeStruct(xs.shape, xs.dtype),
        input_output_aliases={2: 0},
        compiler_params=pltpu.CompilerParams(dimension_semantics=("arbitrary",)),
        name="scatter_rows",
    )(dest, h2, xs)


def _experts_kernel(be_ref, nu_ref, nv_ref, nxt_ref, par_ref, phys_ref, xs_ref, wg_ref, wu_ref, wd_ref, y_ref,
                    wg_s, wu_s, wd_s, wg_f, wu_f, wd_f, sem):
    del phys_ref
    j = pl.program_id(0)
    used = j < nu_ref[0]
    changed = jnp.logical_or(j == 0, be_ref[j] != be_ref[jnp.maximum(j - 1, 0)])

    def weight_copies(e, slot):
        return [pltpu.make_async_copy(src.at[e], dst.at[slot], sem.at[slot])
                for src, dst in ((wg_ref, wg_f), (wu_ref, wu_f), (wd_ref, wd_f))]

    @pl.when(jnp.logical_and(used, changed))
    def _():
        slot = par_ref[j]

        @pl.when(j == 0)
        def _():
            for c in weight_copies(be_ref[0], slot):
                c.start()

        for c in weight_copies(be_ref[j], slot):
            c.wait()

        @pl.when(nxt_ref[j] >= 0)
        def _():
            for c in weight_copies(nxt_ref[j], 1 - slot):
                c.start()

        wg_s[...] = wg_f[slot].astype(jnp.bfloat16)
        wu_s[...] = wu_f[slot].astype(jnp.bfloat16)
        wd_s[...] = wd_f[slot].astype(jnp.bfloat16)

    @pl.when(used)
    def _():
        hb = xs_ref.shape[0] // 2
        row = lax.broadcasted_iota(jnp.int32, (hb, 1), 0)

        def load(h):
            live = row + h * hb < nv_ref[j]
            return _unpack_bf16_pairs(jnp.where(live, xs_ref[h * hb:(h + 1) * hb, :], jnp.uint32(0)))

        def swiglu(g, u):
            return (g * jax.nn.sigmoid(g) * u).astype(jnp.bfloat16)

        x0 = load(0)
        g0 = jnp.dot(x0, wg_s[...], preferred_element_type=jnp.float32)
        u0 = jnp.dot(x0, wu_s[...], preferred_element_type=jnp.float32)
        x1 = load(1)
        g1 = jnp.dot(x1, wg_s[...], preferred_element_type=jnp.float32)
        a0 = swiglu(g0, u0)
        u1 = jnp.dot(x1, wu_s[...], preferred_element_type=jnp.float32)
        y0 = jnp.dot(a0, wd_s[...], preferred_element_type=jnp.float32)
        a1 = swiglu(g1, u1)
        y_ref[0:hb, :] = _pack_bf16_pairs(y0.astype(jnp.bfloat16))
        y1 = jnp.dot(a1, wd_s[...], preferred_element_type=jnp.float32)
        y_ref[hb:, :] = _pack_bf16_pairs(y1.astype(jnp.bfloat16))


def _experts(block_e, n_used, n_valid, next_e, parity, phys, xs, w_gate, w_up, w_down):
    bm = EXPERT_BLOCK
    n_blocks = xs.shape[0] // bm
    hbm = pl.BlockSpec(memory_space=pl.ANY)
    return pl.pallas_call(
        _experts_kernel,
        grid_spec=pltpu.PrefetchScalarGridSpec(
            num_scalar_prefetch=6, grid=(n_blocks,),
            in_specs=[pl.BlockSpec((bm, D_MODEL // 2), lambda j, be, nu, nv, nx, pa, ph: (ph[j], 0)),
                      hbm, hbm, hbm],
            out_specs=pl.BlockSpec((bm, D_MODEL // 2), lambda j, be, nu, nv, nx, pa, ph: (ph[j], 0)),
            scratch_shapes=[pltpu.VMEM((D_MODEL, EXPERT_HIDDEN), jnp.bfloat16),
                            pltpu.VMEM((D_MODEL, EXPERT_HIDDEN), jnp.bfloat16),
                            pltpu.VMEM((EXPERT_HIDDEN, D_MODEL), jnp.bfloat16),
                            pltpu.VMEM((2, D_MODEL, EXPERT_HIDDEN), jnp.float32),
                            pltpu.VMEM((2, D_MODEL, EXPERT_HIDDEN), jnp.float32),
                            pltpu.VMEM((2, EXPERT_HIDDEN, D_MODEL), jnp.float32),
                            pltpu.SemaphoreType.DMA((2,))]),
        out_shape=jax.ShapeDtypeStruct((xs.shape[0], D_MODEL // 2), jnp.uint32),
        compiler_params=pltpu.CompilerParams(dimension_semantics=("arbitrary",), vmem_limit_bytes=VMEM_LIMIT),
        name="experts",
    )(block_e, n_used, n_valid, next_e, parity, phys, xs, w_gate, w_up, w_down)


def _gather_rows(idx, yb):
    mesh = plsc.VectorSubcoreMesh(core_axis_name="core", subcore_axis_name="subcore")
    rows, window = SC_ROWS, SC_INDEX_WINDOW
    sub = window // rows
    n = idx.shape[1]
    width = yb.shape[1]
    assert n % window == 0

    @pl.kernel(out_type=jax.ShapeDtypeStruct((n, width), yb.dtype), mesh=mesh, scratch_types=[])
    def gather(y_hbm, i_hbm, o_hbm):
        def body(indices, i_vmem, o_vmem):
            pltpu.sync_copy(y_hbm.at[i_vmem.at[0, pl.ds(indices[1] * rows, rows)]], o_vmem)

        pltpu.emit_pipeline(
            body, grid=(n // window, sub),
            in_specs=[pl.BlockSpec((1, window), lambda i, j: (0, i))],
            out_specs=[pl.BlockSpec((rows, width), lambda i, j: (i * sub + j, 0))],
            core_axis_name=("core", "subcore"),
            dimension_semantics=(pltpu.PARALLEL, pltpu.ARBITRARY), _explicit_indices=True,
        )(i_hbm, o_hbm)

    return gather(yb, idx)


def _combine_kernel(idx_ref, x1_ref, gate_ref, rf_ref, fin_ref, y0_ref, yb_ref, *refs, per_token_gate):
    o_ref, buf_a, buf_b, sem = refs[-4:]
    g = pl.program_id(0)
    n = pl.num_programs(0)
    hc = COMBINE_TILE // 2

    def issue(half_index, buf, s):
        base = half_index * hc
        for r in range(hc):
            d = idx_ref[base + r]
            pltpu.make_async_copy(yb_ref.at[pl.ds(d, 1)], buf.at[pl.ds(r, 1)], sem.at[s]).start(priority=r % 2)

    def wait(buf, s):
        pltpu.make_async_copy(yb_ref.at[pl.ds(0, hc)], buf, sem.at[s]).wait()

    def combine(buf, rows):
        gates = jnp.transpose(rf_ref[:, rows])
        moe = _unpack_pairs_f32(y0_ref[rows, :]) * gates[:, 0:1] + _unpack_pairs_f32(buf[...]) * gates[:, 1:2]
        gate_f = gate_ref[rows, :] if per_token_gate else gate_ref[...][5:6]
        o_ref[rows, :] = _rms(x1_ref[rows, :] + gate_f * moe, fin_ref[...])

    @pl.when(g == 0)
    def _():
        issue(0, buf_a, 0)

    wait(buf_a, 0)
    issue(2 * g + 1, buf_b, 1)
    combine(buf_a, slice(0, hc))
    wait(buf_b, 1)
    issue(2 * jnp.minimum(g + 1, n - 1), buf_a, 0)
    combine(buf_b, slice(hc, 2 * hc))

    @pl.when(g == n - 1)
    def _():
        wait(buf_a, 0)


def _combine(idx1, x1, gate, rf, final_norm, y0, yb, out_prev, first_tile, n_tiles, *, per_token_gate):
    tc = COMBINE_TILE
    gate_spec = (pl.BlockSpec((tc, D_MODEL), lambda i, ix: (first_tile + i, 0)) if per_token_gate
                 else pl.BlockSpec((6, D_MODEL), lambda i, ix: (0, 0)))
    hbm = pl.BlockSpec(memory_space=pl.ANY)
    args = [idx1, x1, gate, rf, final_norm, y0, yb]
    in_specs = [pl.BlockSpec((tc, D_MODEL), lambda i, ix: (first_tile + i, 0)),
                gate_spec,
                pl.BlockSpec((8, tc), lambda i, ix: (0, first_tile + i)),
                pl.BlockSpec((1, D_MODEL), lambda i, ix: (0, 0)),
                pl.BlockSpec((tc, D_MODEL // 2), lambda i, ix: (i, 0)),
                hbm]
    aliases = {}
    if out_prev is not None:
        args.append(out_prev)
        in_specs.append(hbm)
        aliases = {len(args) - 1: 0}
    return pl.pallas_call(
        functools.partial(_combine_kernel, per_token_gate=per_token_gate),
        grid_spec=pltpu.PrefetchScalarGridSpec(
            num_scalar_prefetch=1, grid=(n_tiles,), in_specs=in_specs,
            out_specs=pl.BlockSpec((tc, D_MODEL), lambda i, ix: (first_tile + i, 0)),
            scratch_shapes=[pltpu.VMEM((tc // 2, D_MODEL // 2), jnp.uint32),
                            pltpu.VMEM((tc // 2, D_MODEL // 2), jnp.uint32),
                            pltpu.SemaphoreType.DMA((2,))]),
        out_shape=jax.ShapeDtypeStruct(x1.shape, jnp.float32),
        input_output_aliases=aliases,
        compiler_params=pltpu.CompilerParams(dimension_semantics=("arbitrary",), vmem_limit_bytes=VMEM_LIMIT),
        name="combine",
    )(*args)


def _rope_tables(pos):
    half = HEAD_DIM // 2
    inv_freq = np.power(np.float32(ROPE_THETA), -(np.arange(half, dtype=np.float32) / np.float32(half)))
    ang = (pos.astype(np.float32)[:, None] * inv_freq[None, :].astype(np.float32)).astype(np.float64)
    cos, sin = np.cos(ang).astype(np.float32), np.sin(ang).astype(np.float32)
    cos = np.concatenate([cos, cos, cos, cos], axis=1)
    sin = np.concatenate([-sin, sin, -sin, sin], axis=1)
    return cos, sin


def _prompt_bias():
    r = np.arange(WINDOW)[:, None]
    j = np.arange(2 * WINDOW)[None, :]
    band = (j > r) & (j <= r + WINDOW)
    later = np.where(band, 0.0, NEG).astype(np.float32)
    first = np.where(band & (j >= WINDOW), 0.0, NEG).astype(np.float32)
    return jnp.asarray(np.stack([first, later]))


def _sample_bias(dec):
    cb = SAMPLE_CHUNK
    qb = np.repeat(np.arange(cb), dec)[:, None]
    qi = np.tile(np.arange(dec), cb)[:, None]
    kb = np.concatenate([np.repeat(np.arange(cb), WINDOW), np.repeat(np.arange(cb), dec)])[None, :]
    kj = np.concatenate([np.tile(np.arange(WINDOW), cb), np.tile(np.arange(dec), cb)])[None, :]
    is_new = (np.arange(cb * WINDOW + cb * dec) >= cb * WINDOW)[None, :]
    ok = (kb == qb) & np.where(is_new, kj <= qi, kj > qi)
    ok = np.pad(ok, ((0, 0), (0, SAMPLE_KEYS - ok.shape[1])))
    return jnp.asarray(np.where(ok, 0.0, NEG).astype(np.float32))


def kernel(x_prompt, x_sample, cache_k, cache_v, state_conv, c_prompt, c_sample, attn_norm, ffn_norm, w_mod, b_mod, w_in, conv_w, attn_sinks, out_norm_attn, out_norm_conv, w_out, w_group, b_group, w_expert, b_expert, w_gate, w_up, w_down, final_norm):
    assert attn_norm.shape[0] == 1, "one layer"
    bsz, seq, _ = x_prompt.shape
    nseq, dec, _ = x_sample.shape
    m_prompt = bsz * seq
    n_s = nseq * dec
    m_total = m_prompt + n_s
    tm = MIXER_TILE
    assert seq % tm == 0 and n_s % SAMPLE_TILE == 0 and SAMPLE_TILE % (SAMPLE_CHUNK * dec) == 0
    assert cache_k.shape[2] == WINDOW and m_prompt % SAMPLE_TILE == 0 and m_total % COMBINE_TILE == 0
    bf = jnp.bfloat16

    w_r = jnp.zeros((ROUTER_ROWS, D_MODEL), jnp.float32)
    w_r = w_r.at[0:N_GROUPS].set(w_group[0].T).at[8:8 + N_EXPERTS].set(w_expert[0].T).astype(bf)
    b_r = jnp.zeros((ROUTER_ROWS,), jnp.float32).at[0:N_GROUPS].set(b_group[0]).at[8:8 + N_EXPERTS].set(b_expert[0])
    b_r = jnp.broadcast_to(b_r[:, None], (ROUTER_ROWS, LANES))
    cos_p, sin_p = _rope_tables(np.arange(seq))
    cos_s, sin_s = _rope_tables(PAST_LEN + np.arange(dec))
    tri = jnp.asarray(np.triu(np.ones((tm, tm), np.float32), 1)).astype(bf)
    consts = dict(
        anorm=attn_norm[0][None], fnorm=ffn_norm[0][None], w_in=w_in[0].astype(bf),
        cos_p=jnp.asarray(cos_p), sin_p=jnp.asarray(sin_p),
        cos_s=jnp.asarray(np.tile(cos_s, (nseq, 1))), sin_s=jnp.asarray(np.tile(sin_s, (nseq, 1))),
        bias_p=_prompt_bias(), bias_s=_sample_bias(dec), conv_w=conv_w[0],
        gn_attn=out_norm_attn[0][None], gn_conv=out_norm_conv[0][None], w_out=w_out[0].astype(bf),
        w_r=w_r, b_r=b_r, tri=tri)
    sinks = attn_sinks[0]

    mod = _adaln(jnp.concatenate([c_prompt, c_sample], axis=0), w_mod[0], b_mod[0])
    mod_p = mod[:bsz].reshape(bsz, 6, D_MODEL)
    mod_s = mod[bsz:]

    bm = EXPERT_BLOCK
    eids = jnp.arange(N_EXPERTS, dtype=jnp.int32)
    lookup = lambda table, e: jnp.sum(jnp.where(e[..., None] == eids, table, 0), axis=-1)
    by_token = lambda a: jnp.transpose(a, (1, 0, 2)).reshape(8, -1)
    ceil_bm = lambda c: (c + bm - 1) // bm * bm
    x1_p, h2_p, ri_p, rf_p, cnt_p, k_p, v_p, u_p = _prompt_mixer(x_prompt, mod_p, sinks, consts)
    cnt_p = cnt_p[:, 0].astype(jnp.int32)
    padded_p = ceil_bm(cnt_p)
    pstart_p = jnp.cumsum(padded_p) - padded_p
    nb_p = -(-(2 * m_prompt + N_EXPERTS * (bm - 1)) // bm)
    nb_o = -(-(2 * n_s + N_EXPERTS * (bm - 1)) // bm)
    n_blocks = nb_p + nb_o
    ri_p = by_token(ri_p)
    dest_p = (lookup(pstart_p, ri_p[0:2]) + ri_p[2:4]).reshape(2 * m_prompt)

    st = state_conv[0]
    zrow = jnp.zeros((nseq, 1, CONV_DIM), jnp.float32)
    e1 = jnp.concatenate([st[:, 1:2]] + [zrow] * (dec - 1), axis=1).reshape(n_s, CONV_DIM)
    e2 = jnp.concatenate([st[:, 0:1], st[:, 1:2]] + [zrow] * (dec - 2), axis=1).reshape(n_s, CONV_DIM)
    dest_p, e1 = lax.optimization_barrier((dest_p, e1))
    xs = _dispatch(dest_p[None], h2_p, n_blocks * bm)
    ck = jnp.transpose(cache_k[0], (0, 2, 3, 1)).reshape(nseq * KV_DIM, WINDOW)
    cv = jnp.transpose(cache_v[0], (0, 2, 3, 1)).reshape(nseq * KV_DIM, WINDOW)
    x1_s, h2_s, ri_s, rf_s, cnt_s, k_s, v_s, u_s = _sample_mixer(
        x_sample.reshape(n_s, D_MODEL), mod_s, sinks, consts, ck, cv, e1, e2,
        jnp.zeros((N_EXPERTS, LANES), jnp.float32), dec)

    cnt_s = cnt_s[:, 0].astype(jnp.int32)
    fit = jnp.minimum(cnt_s, padded_p - cnt_p)
    over = cnt_s - fit
    padded_o = ceil_bm(over)
    ostart = nb_p * bm + jnp.cumsum(padded_o) - padded_o
    ri_s = by_token(ri_s)
    e_s, r_s = ri_s[0:2], ri_s[2:4]
    dest_s = jnp.where(r_s < lookup(fit, e_s), lookup(pstart_p + cnt_p, e_s) + r_s, lookup(ostart - fit, e_s) + r_s)
    dest_s = dest_s.reshape(2 * n_s)

    from_t = lambda a: jnp.transpose(a.reshape(nseq, N_KV_HEADS, HEAD_DIM, WINDOW), (0, 3, 1, 2))[None]
    new_k_s, new_v_s = from_t(k_s), from_t(v_s)
    xs = _scatter_rows(dest_s, h2_s, xs)

    np_e, no_e = padded_p // bm, padded_o // bm
    lend = jnp.cumsum(np_e + no_e)
    lstart = lend - np_e - no_e
    n_used = lend[-1].astype(jnp.int32)
    blk = jnp.arange(n_blocks, dtype=jnp.int32)
    present = np_e + no_e > 0
    last_e = jnp.max(jnp.where(present, eids, 0))
    block_e = jnp.sum((lend[None, :] <= blk[:, None]).astype(jnp.int32), axis=1)
    block_e = jnp.where(blk < n_used, jnp.minimum(block_e, N_EXPERTS - 1), last_e)
    of_block = lambda table: lookup(table, block_e)
    within = blk - of_block(lstart)
    in_prompt = within < of_block(np_e)
    phys = jnp.where(in_prompt, of_block(pstart_p) // bm + within, of_block(ostart) // bm + within - of_block(np_e))
    n_valid = jnp.where(in_prompt, of_block(cnt_p + fit) - within * bm, of_block(over) - (within - of_block(np_e)) * bm)
    n_valid = jnp.where(blk < n_used, jnp.clip(n_valid, 0, bm), 0).astype(jnp.int32)
    phys = jnp.where(blk < n_used, phys, jnp.sum(jnp.where(blk == n_used - 1, phys, 0))).astype(jnp.int32)
    later = jnp.logical_and(present[None, :], eids[None, :] > eids[:, None])
    next_of = jnp.min(jnp.where(later, eids[None, :], N_EXPERTS), axis=1)
    next_of = jnp.where(next_of == N_EXPERTS, -1, next_of)
    next_e = of_block(next_of).astype(jnp.int32)
    parity = of_block((jnp.cumsum(present.astype(jnp.int32)) - 1) % 2).astype(jnp.int32)
    yb = _experts(block_e, n_used.reshape(1), n_valid, next_e, parity, phys, xs, w_gate[0], w_up[0], w_down[0])

    fin = final_norm[None]
    tc = COMBINE_TILE
    rf_p, rf_s = by_token(rf_p), by_token(rf_s)
    y_p = None
    chunk = min(COMBINE_CHUNK, seq)
    assert seq % chunk == 0 and chunk % tc == 0
    for c in range(m_prompt // chunk):
        lo, hi = c * chunk, (c + 1) * chunk
        y0 = _gather_rows(dest_p[lo:hi][None], yb)
        y_p = _combine(dest_p[m_prompt + lo:m_prompt + hi], x1_p, mod_p[lo // seq], rf_p, fin, y0, yb, y_p,
                       lo // tc, chunk // tc, per_token_gate=False)
    gate_s = jnp.repeat(mod_s[:, 5 * D_MODEL:], dec, axis=0)
    y_s = _combine(dest_s[n_s:], x1_s, gate_s, rf_s, fin, _gather_rows(dest_s[:n_s][None], yb), yb, None,
                   0, n_s // tc, per_token_gate=True)

    keep = WINDOW
    new_k_p = k_p.reshape(1, bsz, keep, N_KV_HEADS, HEAD_DIM)
    new_v_p = v_p.reshape(1, bsz, keep, N_KV_HEADS, HEAD_DIM)
    new_u_p = u_p[:, 6:8][None]
    u_all = jnp.concatenate([st, u_s.reshape(nseq, dec, CONV_DIM)], axis=1)
    new_u_s = u_all[:, -2:][None]
    return (y_p.reshape(bsz, seq, D_MODEL), y_s.reshape(nseq, dec, D_MODEL),
            new_k_p, new_v_p, new_u_p, new_k_s, new_v_s, new_u_s)
```

```python
import functools

import jax
import jax.numpy as jnp
import numpy as np
from jax import lax
from jax.experimental import pallas as pl
from jax.experimental.pallas import tpu as pltpu
from jax.experimental.pallas import tpu_sc as plsc

D_MODEL = 1024
HEAD_DIM = 64
ATTN_DIM = 512
N_Q_HEADS = 8
N_KV_HEADS = 2
KV_DIM = 128
CONV_DIM = 512
WINDOW = 128
IN_PROJ_DIM = ATTN_DIM + 2 * KV_DIM + 3 * CONV_DIM
N_GROUPS = 4
EXPERTS_PER_GROUP = 8
N_EXPERTS = 32
EXPERT_HIDDEN = 512
RMS_EPS = 1e-5
ROPE_THETA = 10000.0
PAST_LEN = 16384

LANES = 128
ROUTER_ROWS = 48
NEG = -1e30
LOG2_E = 1.4426950408889634
Q_SCALE = HEAD_DIM ** -0.5 * LOG2_E

MIXER_TILE = 512
SAMPLE_TILE = 128
SAMPLE_CHUNK = 8
SAMPLE_KEYS = (SAMPLE_CHUNK + 1) * WINDOW
EXPERT_BLOCK = 512
COMBINE_TILE = 512
COMBINE_CHUNK = 4096
SC_ROWS = 32
SC_INDEX_WINDOW = 128
VMEM_LIMIT = 56 * 1024 * 1024


def _rms(x, gain):
    ms = jnp.mean(x * x, axis=-1, keepdims=True)
    return x * lax.rsqrt(ms + RMS_EPS) * gain


def _half_lane_mask(rows=1):
    return lax.broadcasted_iota(jnp.int32, (rows, LANES), 1) < HEAD_DIM


def _rope(x, cos, sin_signed):
    t, w = x.shape
    reps = w // LANES
    lane = lax.broadcasted_iota(jnp.int32, (t, w), 1)
    upper = (lane % HEAD_DIM) >= (HEAD_DIM // 2)
    partner = jnp.where(upper, pltpu.roll(x, HEAD_DIM // 2, 1), pltpu.roll(x, w - HEAD_DIM // 2, 1))
    if reps > 1:
        cos = jnp.concatenate([cos] * reps, axis=1)
        sin_signed = jnp.concatenate([sin_signed] * reps, axis=1)
    return x * cos + partner * sin_signed


def _swap_halves(x):
    return pltpu.roll(x, HEAD_DIM, 1)


def _group_norm64(x, gain):
    t, w = x.shape
    lo = _half_lane_mask(t)
    outs = []
    for c in range(w // LANES):
        xc = x[:, c * LANES:(c + 1) * LANES]
        sq = xc * xc
        s_lo = jnp.sum(jnp.where(lo, sq, 0.0), axis=-1, keepdims=True)
        s_hi = jnp.sum(sq, axis=-1, keepdims=True) - s_lo
        r = jnp.where(lo, lax.rsqrt(s_lo * (1.0 / HEAD_DIM) + RMS_EPS), lax.rsqrt(s_hi * (1.0 / HEAD_DIM) + RMS_EPS))
        outs.append(xc * r)
    return jnp.concatenate(outs, axis=1) * gain


def _attend(q_rows, keys, keys_sw, vals, vals_sw, bias, sinks_ref, group, transposed=False):
    r = q_rows.shape[0]
    lo = _half_lane_mask(r)
    outs = []
    for parity in range(2):
        heads = [4 * group + parity, 4 * group + parity + 2]
        kk, vv = (keys, vals) if parity == group else (keys_sw, vals_sw)
        half = lo if parity == 0 else jnp.logical_not(lo)
        qs = []
        for h in heads:
            pair = q_rows[:, (h // 2) * LANES:(h // 2 + 1) * LANES]
            qs.append(jnp.where(half, pair, jnp.zeros_like(pair)))
        qcat = jnp.concatenate(qs, axis=0)
        s = (jnp.dot(qcat, kk, preferred_element_type=jnp.float32) if transposed else
             lax.dot_general(qcat, kk, (((1,), (1,)), ((), ())), preferred_element_type=jnp.float32))
        ps, dens = [], []
        for i, h in enumerate(heads):
            sh = s[i * r:(i + 1) * r] + bias
            sink = sinks_ref[h] * LOG2_E
            m = jnp.maximum(jnp.max(sh, axis=-1, keepdims=True), sink)
            p = jnp.exp2(sh - m)
            dens.append(jnp.sum(p, axis=-1, keepdims=True) + jnp.exp2(sink - m))
            ps.append(p.astype(jnp.bfloat16))
        pcat = jnp.concatenate(ps, axis=0)
        o = (lax.dot_general(pcat, vv, (((1,), (1,)), ((), ())), preferred_element_type=jnp.float32) if transposed
             else jnp.dot(pcat, vv, preferred_element_type=jnp.float32))
        for i, h in enumerate(heads):
            oh = o[i * r:(i + 1) * r] / dens[i]
            ms = jnp.sum(jnp.where(half, oh * oh, 0.0), axis=-1, keepdims=True) * (1.0 / HEAD_DIM)
            outs.append((h, oh * lax.rsqrt(ms + RMS_EPS)))
    outs.sort(key=lambda t: t[0])
    return [o for _, o in outs]


def _merge_heads(head_outs, lo):
    pairs = [jnp.where(lo, head_outs[2 * j], head_outs[2 * j + 1]) for j in range(4)]
    return jnp.concatenate(pairs, axis=1)


def _route(h2b, wr_ref, br_ref, tri_ref, cnt_ref, ri_ref, rf_ref):
    t = h2b.shape[0]
    lg = lax.dot_general(wr_ref[...], h2b, (((1,), (1,)), ((), ())), preferred_element_type=jnp.float32)
    lg = lg + br_ref[...][:, 0:1]
    r8 = lax.broadcasted_iota(jnp.int32, (8, t), 0)
    lgrp = jnp.where(r8 < N_GROUPS, lg[0:8], -jnp.inf)
    gmax = jnp.max(lgrp, axis=0, keepdims=True)
    grp = jnp.min(jnp.where(lgrp == gmax, r8, 8), axis=0, keepdims=True)
    pg_sel = 1.0 / jnp.sum(jnp.exp(lgrp - gmax), axis=0, keepdims=True)
    r32 = lax.broadcasted_iota(jnp.int32, (N_EXPERTS, t), 0)
    le = jnp.where((r32 // EXPERTS_PER_GROUP) == grp, lg[8:8 + N_EXPERTS], -jnp.inf)
    v1 = jnp.max(le, axis=0, keepdims=True)
    i1 = jnp.min(jnp.where(le == v1, r32, N_EXPERTS), axis=0, keepdims=True)
    le2 = jnp.where(r32 == i1, -jnp.inf, le)
    v2 = jnp.max(le2, axis=0, keepdims=True)
    i2 = jnp.min(jnp.where(le2 == v2, r32, N_EXPERTS), axis=0, keepdims=True)
    e21 = jnp.exp(v2 - v1)
    gate1 = pg_sel / (1.0 + e21)
    gate2 = pg_sel * e21 / (1.0 + e21)
    hot1 = r32 == i1
    hot2 = r32 == i2
    onehot = jnp.where(jnp.logical_or(hot1, hot2), 1.0, 0.0)
    before = jnp.dot(onehot.astype(jnp.bfloat16), tri_ref[...], preferred_element_type=jnp.float32)
    pos = before + cnt_ref[...][:, 0:1]
    rank1 = jnp.sum(jnp.where(hot1, pos, 0.0), axis=0, keepdims=True)
    rank2 = jnp.sum(jnp.where(hot2, pos, 0.0), axis=0, keepdims=True)
    cnt_ref[...] = cnt_ref[...] + jnp.sum(onehot, axis=1, keepdims=True)
    zi = jnp.zeros((4, t), jnp.int32)
    ri_ref[...] = jnp.concatenate([i1, i2, rank1.astype(jnp.int32), rank2.astype(jnp.int32), zi], axis=0)
    rf_ref[...] = jnp.concatenate([gate1, gate2, jnp.zeros((6, t), jnp.float32)], axis=0)


def _pack_bf16_pairs(xb):
    w = xb.shape[1] // 2
    bits = pltpu.bitcast(xb.astype(jnp.float32), jnp.uint32)
    return (bits[:, :w] >> 16) | (bits[:, w:] & jnp.uint32(0xFFFF0000))


def _unpack_pairs_f32(words):
    lo = pltpu.bitcast(words << 16, jnp.float32)
    hi = pltpu.bitcast(words & jnp.uint32(0xFFFF0000), jnp.float32)
    return jnp.concatenate([lo, hi], axis=1)


def _unpack_bf16_pairs(words):
    return _unpack_pairs_f32(words).astype(jnp.bfloat16)


def _split_proj(proj):
    a = ATTN_DIM
    q = proj[:, :a]
    k = proj[:, a:a + KV_DIM]
    v = proj[:, a + KV_DIM:a + 2 * KV_DIM]
    c0 = a + 2 * KV_DIM
    h_conv = proj[:, c0:c0 + CONV_DIM]
    gate_b = proj[:, c0 + CONV_DIM:c0 + 2 * CONV_DIM]
    gate_c = proj[:, c0 + 2 * CONV_DIM:c0 + 3 * CONV_DIM]
    return q, k, v, gate_c * h_conv, gate_b


def _mixer_tail(x, attn_n, conv_n, mod, wout_ref, fnorm_ref, wr_ref, br_ref, tri_ref, cnt_ref,
                x1_ref, h2_ref, ri_ref, rf_ref):
    gate_a, shift_f, scale_f = mod
    cat = jnp.concatenate([attn_n.astype(jnp.bfloat16), conv_n.astype(jnp.bfloat16)], axis=1)
    mix = jnp.dot(cat, wout_ref[...], preferred_element_type=jnp.float32)
    x1 = x + gate_a * mix
    x1_ref[...] = x1
    h2 = _rms(x1, fnorm_ref[...] * (1.0 + scale_f)) + shift_f
    h2b = h2.astype(jnp.bfloat16)
    h2_ref[...] = _pack_bf16_pairs(h2b)
    _route(h2b, wr_ref, br_ref, tri_ref, cnt_ref, ri_ref, rf_ref)


def _adaln_kernel(c_ref, w_ref, b_ref, o_ref):
    c = c_ref[...]
    a = (c * jax.nn.sigmoid(c)).astype(jnp.bfloat16)
    o_ref[...] = jnp.dot(a, w_ref[...].astype(jnp.bfloat16), preferred_element_type=jnp.float32) + b_ref[...]


def _adaln(c, w_mod, b_mod):
    n = c.shape[0]
    tn = 1536
    return pl.pallas_call(
        _adaln_kernel,
        grid=(w_mod.shape[1] // tn,),
        in_specs=[pl.BlockSpec((n, D_MODEL), lambda j: (0, 0)),
                  pl.BlockSpec((D_MODEL, tn), lambda j: (0, j)),
                  pl.BlockSpec((1, tn), lambda j: (0, j))],
        out_specs=pl.BlockSpec((n, tn), lambda j: (0, j)),
        out_shape=jax.ShapeDtypeStruct((n, w_mod.shape[1]), jnp.float32),
        compiler_params=pltpu.CompilerParams(dimension_semantics=("arbitrary",), vmem_limit_bytes=VMEM_LIMIT),
        name="adaln",
    )(c, w_mod, b_mod.reshape(1, -1))


def _prompt_mixer_kernel(sinks_ref, x_ref, mod_ref, anorm_ref, fnorm_ref, win_ref, cos_ref, sin_ref, bias_ref,
                         convw_ref, gna_ref, gnc_ref, wout_ref, wr_ref, br_ref, tri_ref,
                         x1_ref, h2_ref, ri_ref, rf_ref, cnt_out_ref, knew_ref, vnew_ref, unew_ref,
                         q_s, k_s, ksw_s, v_s, vsw_s, u_s, att_s, cnt_s):
    b = pl.program_id(0)
    t = pl.program_id(1)
    tm = MIXER_TILE
    nblk = tm // WINDOW

    @pl.when(jnp.logical_and(b == 0, t == 0))
    def _():
        cnt_s[...] = jnp.zeros_like(cnt_s)

    @pl.when(t == 0)
    def _():
        z = jnp.zeros((WINDOW, KV_DIM), jnp.bfloat16)
        k_s[0:WINDOW, :] = z
        ksw_s[0:WINDOW, :] = z
        v_s[0:WINDOW, :] = z
        vsw_s[0:WINDOW, :] = z
        u_s[0:8, :] = jnp.zeros((8, CONV_DIM), jnp.float32)

    x = x_ref[...]
    mod = mod_ref[...]
    h = _rms(x, anorm_ref[...] * (1.0 + mod[1:2])) + mod[0:1]
    proj = jnp.dot(h.astype(jnp.bfloat16), win_ref[...], preferred_element_type=jnp.float32)
    q, k, v, u, gate_b = _split_proj(proj)
    cos = cos_ref[...]
    sin = sin_ref[...]
    q = _rope(q, cos, sin) * Q_SCALE
    k = _rope(k, cos, sin)
    q_s[...] = q.astype(jnp.bfloat16)
    k_s[WINDOW:, :] = k.astype(jnp.bfloat16)
    ksw_s[WINDOW:, :] = _swap_halves(k).astype(jnp.bfloat16)
    v_s[WINDOW:, :] = v.astype(jnp.bfloat16)
    vsw_s[WINDOW:, :] = _swap_halves(v).astype(jnp.bfloat16)
    knew_ref[...] = k[tm - WINDOW:, :]
    vnew_ref[...] = v[tm - WINDOW:, :]

    u_s[8:, :] = u
    unew_ref[...] = u[tm - 8:, :]
    cw = convw_ref[...]
    conv = (u_s[6:6 + tm, :] * cw[0:1] + u_s[7:7 + tm, :] * cw[1:2] + u * cw[2:3]) * gate_b
    u_s[0:8, :] = u[tm - 8:, :]
    conv_n = _group_norm64(conv, gnc_ref[...])

    lo = _half_lane_mask(WINDOW)
    first = t == 0

    def block(i, carry):
        r0 = pl.multiple_of(i * WINDOW, WINDOW)
        bias = bias_ref[jnp.where(jnp.logical_and(first, i == 0), 0, 1)]
        qb = q_s[pl.ds(r0, WINDOW), :]
        keys = k_s[pl.ds(r0, 2 * WINDOW), :]
        keys_sw = ksw_s[pl.ds(r0, 2 * WINDOW), :]
        vals = v_s[pl.ds(r0, 2 * WINDOW), :]
        vals_sw = vsw_s[pl.ds(r0, 2 * WINDOW), :]
        outs = []
        for g in range(N_KV_HEADS):
            outs += _attend(qb, keys, keys_sw, vals, vals_sw, bias, sinks_ref, g)
        att_s[pl.ds(r0, WINDOW), :] = _merge_heads(outs, lo)
        return carry

    lax.fori_loop(0, nblk, block, 0, unroll=True)
    k_s[0:WINDOW, :] = k_s[tm:tm + WINDOW, :]
    ksw_s[0:WINDOW, :] = ksw_s[tm:tm + WINDOW, :]
    v_s[0:WINDOW, :] = v_s[tm:tm + WINDOW, :]
    vsw_s[0:WINDOW, :] = vsw_s[tm:tm + WINDOW, :]

    attn_n = att_s[...] * gna_ref[...]
    _mixer_tail(x, attn_n, conv_n, (mod[2:3], mod[3:4], mod[4:5]), wout_ref, fnorm_ref, wr_ref, br_ref, tri_ref,
                cnt_s, x1_ref, h2_ref, ri_ref, rf_ref)
    cnt_out_ref[...] = cnt_s[...]


def _prompt_mixer(x, mod, sinks, consts):
    bsz, seq, _ = x.shape
    tm = MIXER_TILE
    nt = seq // tm
    full = lambda shape: pl.BlockSpec(shape, lambda b, t, s: (0,) * len(shape))
    in_specs = [
        pl.BlockSpec((None, tm, D_MODEL), lambda b, t, s: (b, t, 0)),
        pl.BlockSpec((None, 6, D_MODEL), lambda b, t, s: (b, 0, 0)),
        full((1, D_MODEL)), full((1, D_MODEL)),
        full((D_MODEL, IN_PROJ_DIM)),
        pl.BlockSpec((tm, LANES), lambda b, t, s: (t, 0)),
        pl.BlockSpec((tm, LANES), lambda b, t, s: (t, 0)),
        full((2, WINDOW, 2 * WINDOW)),
        full((3, CONV_DIM)), full((1, ATTN_DIM)), full((1, CONV_DIM)),
        full((D_MODEL, D_MODEL)),
        full((ROUTER_ROWS, D_MODEL)), full((ROUTER_ROWS, LANES)),
        full((tm, tm)),
    ]
    out_shape = [
        jax.ShapeDtypeStruct((bsz * seq, D_MODEL), jnp.float32),
        jax.ShapeDtypeStruct((bsz * seq, D_MODEL // 2), jnp.uint32),
        jax.ShapeDtypeStruct((bsz * nt, 8, tm), jnp.int32),
        jax.ShapeDtypeStruct((bsz * nt, 8, tm), jnp.float32),
        jax.ShapeDtypeStruct((N_EXPERTS, LANES), jnp.float32),
        jax.ShapeDtypeStruct((bsz, WINDOW, KV_DIM), jnp.float32),
        jax.ShapeDtypeStruct((bsz, WINDOW, KV_DIM), jnp.float32),
        jax.ShapeDtypeStruct((bsz, 8, CONV_DIM), jnp.float32),
    ]
    out_specs = [
        pl.BlockSpec((tm, D_MODEL), lambda b, t, s: (b * nt + t, 0)),
        pl.BlockSpec((tm, D_MODEL // 2), lambda b, t, s: (b * nt + t, 0)),
        pl.BlockSpec((None, 8, tm), lambda b, t, s: (b * nt + t, 0, 0)),
        pl.BlockSpec((None, 8, tm), lambda b, t, s: (b * nt + t, 0, 0)),
        pl.BlockSpec((N_EXPERTS, LANES), lambda b, t, s: (0, 0)),
        pl.BlockSpec((None, WINDOW, KV_DIM), lambda b, t, s: (b, 0, 0)),
        pl.BlockSpec((None, WINDOW, KV_DIM), lambda b, t, s: (b, 0, 0)),
        pl.BlockSpec((None, 8, CONV_DIM), lambda b, t, s: (b, 0, 0)),
    ]
    scratch = [
        pltpu.VMEM((tm, ATTN_DIM), jnp.bfloat16),
        pltpu.VMEM((tm + WINDOW, KV_DIM), jnp.bfloat16), pltpu.VMEM((tm + WINDOW, KV_DIM), jnp.bfloat16),
        pltpu.VMEM((tm + WINDOW, KV_DIM), jnp.bfloat16), pltpu.VMEM((tm + WINDOW, KV_DIM), jnp.bfloat16),
        pltpu.VMEM((tm + 8, CONV_DIM), jnp.float32),
        pltpu.VMEM((tm, ATTN_DIM), jnp.float32),
        pltpu.VMEM((N_EXPERTS, LANES), jnp.float32),
    ]
    return pl.pallas_call(
        _prompt_mixer_kernel,
        grid_spec=pltpu.PrefetchScalarGridSpec(
            num_scalar_prefetch=1, grid=(bsz, nt), in_specs=in_specs, out_specs=out_specs, scratch_shapes=scratch),
        out_shape=out_shape,
        compiler_params=pltpu.CompilerParams(dimension_semantics=("arbitrary", "arbitrary"),
                                             vmem_limit_bytes=VMEM_LIMIT),
        name="prompt_mixer",
    )(sinks, x, mod, consts["anorm"], consts["fnorm"], consts["w_in"], consts["cos_p"], consts["sin_p"],
      consts["bias_p"], consts["conv_w"], consts["gn_attn"], consts["gn_conv"], consts["w_out"],
      consts["w_r"], consts["b_r"], consts["tri"])


def _sample_mixer_kernel(sinks_ref, x_ref, mod_ref, anorm_ref, fnorm_ref, win_ref, cos_ref, sin_ref, bias_ref,
                         convw_ref, gna_ref, gnc_ref, wout_ref, wr_ref, br_ref, tri_ref,
                         ck_ref, cv_ref, e1_ref, e2_ref, cnt_in_ref,
                         x1_ref, h2_ref, ri_ref, rf_ref, cnt_out_ref, knew_ref, vnew_ref, unew_ref,
                         q_s, k_s, v_s, att_s, cnt_s, *, dec):
    n = x_ref.shape[0]
    cb = SAMPLE_CHUNK
    rows = cb * dec

    @pl.when(pl.program_id(0) == 0)
    def _():
        cnt_s[...] = cnt_in_ref[...]

    x = x_ref[...]
    mod = jnp.repeat(mod_ref[...], dec, axis=0)
    md = lambda j: mod[:, j * D_MODEL:(j + 1) * D_MODEL]
    h = _rms(x, anorm_ref[...]) * (1.0 + md(1)) + md(0)
    proj = jnp.dot(h.astype(jnp.bfloat16), win_ref[...], preferred_element_type=jnp.float32)
    q, k, v, u, gate_b = _split_proj(proj)
    cos = cos_ref[...]
    sin = sin_ref[...]
    q = _rope(q, cos, sin) * Q_SCALE
    k = _rope(k, cos, sin)
    q_s[...] = q.astype(jnp.bfloat16)
    k_s[...] = k
    v_s[...] = v
    unew_ref[...] = u

    pos = lax.broadcasted_iota(jnp.int32, (n, CONV_DIM), 0) % dec
    um1 = jnp.where(pos >= 1, pltpu.roll(u, 1, 0), 0.0) + e1_ref[...]
    um2 = jnp.where(pos >= 2, pltpu.roll(u, 2, 0), 0.0) + e2_ref[...]
    cw = convw_ref[...]
    conv = (um2 * cw[0:1] + um1 * cw[1:2] + u * cw[2:3]) * gate_b
    conv_n = _group_norm64(conv, gnc_ref[...])

    lo = _half_lane_mask(rows)
    bias = bias_ref[...]

    half = KV_DIM // 2
    lane = lax.broadcasted_iota(jnp.int32, (KV_DIM, WINDOW), 1)
    swap_rows = lambda t: jnp.concatenate([t[half:], t[:half]], axis=0)

    for c in range(n // rows):
        r0 = c * rows
        qb = q_s[r0:r0 + rows, :]
        cats = []
        for ref, new_rows, out_ref in ((ck_ref, k_s[r0:r0 + rows, :], knew_ref), (cv_ref, v_s[r0:r0 + rows, :], vnew_ref)):
            tiles = [ref[(c * cb + b) * KV_DIM:(c * cb + b + 1) * KV_DIM, :] for b in range(cb)]
            new_t = jnp.concatenate([jnp.transpose(new_rows), jnp.zeros((KV_DIM, WINDOW - rows), jnp.float32)], axis=1)
            cat = jnp.concatenate(tiles + [new_t], axis=1)
            cats.append((cat.astype(jnp.bfloat16), swap_rows(cat).astype(jnp.bfloat16)))
            for b in range(cb):
                kept = pltpu.roll(tiles[b], WINDOW - dec, 1)
                fresh = pltpu.roll(new_t, (WINDOW - dec - b * dec) % WINDOW, 1)
                out_ref[(c * cb + b) * KV_DIM:(c * cb + b + 1) * KV_DIM, :] = jnp.where(lane < WINDOW - dec, kept, fresh)
        (keys, keys_sw), (vals, vals_sw) = cats
        outs = []
        for g in range(N_KV_HEADS):
            outs += _attend(qb, keys, keys_sw, vals, vals_sw, bias, sinks_ref, g, transposed=True)
        att_s[r0:r0 + rows, :] = _merge_heads(outs, lo)

    attn_n = att_s[...] * gna_ref[...]
    _mixer_tail(x, attn_n, conv_n, (md(2), md(3), md(4)), wout_ref, fnorm_ref, wr_ref, br_ref, tri_ref,
                cnt_s, x1_ref, h2_ref, ri_ref, rf_ref)
    cnt_out_ref[...] = cnt_s[...]


def _sample_mixer(x, mod, sinks, consts, ck, cv, e1, e2, cnt_in, dec):
    n_s = x.shape[0]
    ts = SAMPLE_TILE
    seqs = ts // dec
    full = lambda a: pl.BlockSpec(a.shape, lambda i, s: (0,) * a.ndim)
    rows = lambda width: pl.BlockSpec((ts, width), lambda i, s: (i, 0))
    in_specs = [
        rows(D_MODEL), pl.BlockSpec((seqs, 6 * D_MODEL), lambda i, s: (i, 0)),
        full(consts["anorm"]), full(consts["fnorm"]), full(consts["w_in"]),
        rows(LANES), rows(LANES), full(consts["bias_s"]),
        full(consts["conv_w"]), full(consts["gn_attn"]), full(consts["gn_conv"]), full(consts["w_out"]),
        full(consts["w_r"]), full(consts["b_r"]),
        pl.BlockSpec((ts, ts), lambda i, s: (0, 0)),
        pl.BlockSpec((seqs * KV_DIM, WINDOW), lambda i, s: (i, 0)),
        pl.BlockSpec((seqs * KV_DIM, WINDOW), lambda i, s: (i, 0)),
        rows(CONV_DIM), rows(CONV_DIM), full(cnt_in),
    ]
    args = [x, mod, consts["anorm"], consts["fnorm"], consts["w_in"], consts["cos_s"], consts["sin_s"],
            consts["bias_s"], consts["conv_w"], consts["gn_attn"], consts["gn_conv"], consts["w_out"],
            consts["w_r"], consts["b_r"], consts["tri"], ck, cv, e1, e2, cnt_in]
    nt = n_s // ts
    out_shape = [
        jax.ShapeDtypeStruct((n_s, D_MODEL), jnp.float32),
        jax.ShapeDtypeStruct((n_s, D_MODEL // 2), jnp.uint32),
        jax.ShapeDtypeStruct((nt, 8, ts), jnp.int32),
        jax.ShapeDtypeStruct((nt, 8, ts), jnp.float32),
        jax.ShapeDtypeStruct((N_EXPERTS, LANES), jnp.float32),
        jax.ShapeDtypeStruct(ck.shape, jnp.float32),
        jax.ShapeDtypeStruct(cv.shape, jnp.float32),
        jax.ShapeDtypeStruct((n_s, CONV_DIM), jnp.float32),
    ]
    out_specs = [
        rows(D_MODEL),
        rows(D_MODEL // 2),
        pl.BlockSpec((None, 8, ts), lambda i, s: (i, 0, 0)),
        pl.BlockSpec((None, 8, ts), lambda i, s: (i, 0, 0)),
        pl.BlockSpec((N_EXPERTS, LANES), lambda i, s: (0, 0)),
        pl.BlockSpec((seqs * KV_DIM, WINDOW), lambda i, s: (i, 0)),
        pl.BlockSpec((seqs * KV_DIM, WINDOW), lambda i, s: (i, 0)),
        rows(CONV_DIM),
    ]
    scratch = [
        pltpu.VMEM((ts, ATTN_DIM), jnp.bfloat16),
        pltpu.VMEM((ts, KV_DIM), jnp.float32), pltpu.VMEM((ts, KV_DIM), jnp.float32),
        pltpu.VMEM((ts, ATTN_DIM), jnp.float32),
        pltpu.VMEM((N_EXPERTS, LANES), jnp.float32),
    ]
    return pl.pallas_call(
        functools.partial(_sample_mixer_kernel, dec=dec),
        grid_spec=pltpu.PrefetchScalarGridSpec(
            num_scalar_prefetch=1, grid=(nt,), in_specs=in_specs, out_specs=out_specs, scratch_shapes=scratch),
        out_shape=out_shape,
        compiler_params=pltpu.CompilerParams(dimension_semantics=("arbitrary",), vmem_limit_bytes=VMEM_LIMIT),
        name="sample_mixer",
    )(sinks, *args)


def _dispatch(idx, h2, n_rows):
    mesh = plsc.VectorSubcoreMesh(core_axis_name="core", subcore_axis_name="subcore")
    rows, window = SC_ROWS, SC_INDEX_WINDOW
    sub = window // rows
    n_tok, width = h2.shape
    assert n_tok % rows == 0 and (2 * n_tok) % window == 0

    moved = 3 * n_tok * width * h2.dtype.itemsize
    @pl.kernel(out_type=jax.ShapeDtypeStruct((n_rows, width), h2.dtype), mesh=mesh, scratch_types=[],
               cost_estimate=pl.CostEstimate(flops=0, transcendentals=0, bytes_accessed=moved))
    def scatter(h_hbm, i_hbm, xs_hbm):
        def body(indices, h_vmem, i_vmem):
            pltpu.sync_copy(h_vmem, xs_hbm.at[i_vmem.at[0, pl.ds(indices[1] * rows, rows)]])

        pltpu.emit_pipeline(
            body, grid=(2 * n_tok // window, sub),
            in_specs=[pl.BlockSpec((rows, width), lambda i, j: ((i * sub + j) % (n_tok // rows), 0)),
                      pl.BlockSpec((1, window), lambda i, j: (0, i))],
            out_specs=[], core_axis_name=("core", "subcore"),
            dimension_semantics=(pltpu.PARALLEL, pltpu.ARBITRARY), _explicit_indices=True,
        )(h_hbm, i_hbm)

    return scatter(h2, idx)


def _scatter_rows_kernel(dest_ref, h_ref, xs_in_ref, xs_ref, sem):
    del xs_in_ref
    n_tok = h_ref.shape[0]

    def issue(i, carry):
        for k in range(2):
            d = dest_ref[k * n_tok + i]
            pltpu.make_async_copy(h_ref.at[pl.ds(i, 1)], xs_ref.at[pl.ds(d, 1)], sem.at[0]).start(priority=k)
        return carry

    lax.fori_loop(0, n_tok, issue, 0, unroll=8)
    for _ in range(2):
        pltpu.make_async_copy(h_ref, xs_ref.at[pl.ds(0, n_tok)], sem.at[0]).wait()


def _scatter_rows(dest, h2, xs):
    return pl.pallas_call(
        _scatter_rows_kernel,
        grid_spec=pltpu.PrefetchScalarGridSpec(
            num_scalar_prefetch=1, grid=(1,),
            in_specs=[pl.BlockSpec(h2.shape, lambda i, d: (0, 0)), pl.BlockSpec(memory_space=pl.ANY)],
            out_specs=pl.BlockSpec(memory_space=pl.ANY),
            scratch_shapes=[pltpu.SemaphoreType.DMA((1,))]),
        out_shape=jax.ShapeDtypeStruct(xs.shape, xs.dtype),
        input_output_aliases={2: 0},
        compiler_params=pltpu.CompilerParams(dimension_semantics=("arbitrary",)),
        name="scatter_rows",
    )(dest, h2, xs)


def _experts_kernel(be_ref, nu_ref, nv_ref, nxt_ref, par_ref, phys_ref, xs_ref, wg_ref, wu_ref, wd_ref, y_ref,
                    wg_s, wu_s, wd_s, wg_f, wu_f, wd_f, sem):
    del phys_ref
    j = pl.program_id(0)
    used = j < nu_ref[0]
    changed = jnp.logical_or(j == 0, be_ref[j] != be_ref[jnp.maximum(j - 1, 0)])

    def weight_copies(e, slot):
        return [pltpu.make_async_copy(src.at[e], dst.at[slot], sem.at[slot])
                for src, dst in ((wg_ref, wg_f), (wu_ref, wu_f), (wd_ref, wd_f))]

    @pl.when(jnp.logical_and(used, changed))
    def _():
        slot = par_ref[j]

        @pl.when(j == 0)
        def _():
            for c in weight_copies(be_ref[0], slot):
                c.start()

        for c in weight_copies(be_ref[j], slot):
            c.wait()

        @pl.when(nxt_ref[j] >= 0)
        def _():
            for c in weight_copies(nxt_ref[j], 1 - slot):
                c.start()

        wg_s[...] = wg_f[slot].astype(jnp.bfloat16)
        wu_s[...] = wu_f[slot].astype(jnp.bfloat16)
        wd_s[...] = wd_f[slot].astype(jnp.bfloat16)

    @pl.when(used)
    def _():
        hb = xs_ref.shape[0] // 2
        row = lax.broadcasted_iota(jnp.int32, (hb, 1), 0)

        def load(h):
            live = row + h * hb < nv_ref[j]
            return _unpack_bf16_pairs(jnp.where(live, xs_ref[h * hb:(h + 1) * hb, :], jnp.uint32(0)))

        def swiglu(g, u):
            return (g * jax.nn.sigmoid(g) * u).astype(jnp.bfloat16)

        x0 = load(0)
        g0 = jnp.dot(x0, wg_s[...], preferred_element_type=jnp.float32)
        u0 = jnp.dot(x0, wu_s[...], preferred_element_type=jnp.float32)
        x1 = load(1)
        g1 = jnp.dot(x1, wg_s[...], preferred_element_type=jnp.float32)
        a0 = swiglu(g0, u0)
        u1 = jnp.dot(x1, wu_s[...], preferred_element_type=jnp.float32)
        y0 = jnp.dot(a0, wd_s[...], preferred_element_type=jnp.float32)
        a1 = swiglu(g1, u1)
        y_ref[0:hb, :] = _pack_bf16_pairs(y0.astype(jnp.bfloat16))
        y1 = jnp.dot(a1, wd_s[...], preferred_element_type=jnp.float32)
        y_ref[hb:, :] = _pack_bf16_pairs(y1.astype(jnp.bfloat16))


def _experts(block_e, n_used, n_valid, next_e, parity, phys, xs, w_gate, w_up, w_down):
    bm = EXPERT_BLOCK
    n_blocks = xs.shape[0] // bm
    hbm = pl.BlockSpec(memory_space=pl.ANY)
    return pl.pallas_call(
        _experts_kernel,
        grid_spec=pltpu.PrefetchScalarGridSpec(
            num_scalar_prefetch=6, grid=(n_blocks,),
            in_specs=[pl.BlockSpec((bm, D_MODEL // 2), lambda j, be, nu, nv, nx, pa, ph: (ph[j], 0)),
                      hbm, hbm, hbm],
            out_specs=pl.BlockSpec((bm, D_MODEL // 2), lambda j, be, nu, nv, nx, pa, ph: (ph[j], 0)),
            scratch_shapes=[pltpu.VMEM((D_MODEL, EXPERT_HIDDEN), jnp.bfloat16),
                            pltpu.VMEM((D_MODEL, EXPERT_HIDDEN), jnp.bfloat16),
                            pltpu.VMEM((EXPERT_HIDDEN, D_MODEL), jnp.bfloat16),
                            pltpu.VMEM((2, D_MODEL, EXPERT_HIDDEN), jnp.float32),
                            pltpu.VMEM((2, D_MODEL, EXPERT_HIDDEN), jnp.float32),
                            pltpu.VMEM((2, EXPERT_HIDDEN, D_MODEL), jnp.float32),
                            pltpu.SemaphoreType.DMA((2,))]),
        out_shape=jax.ShapeDtypeStruct((xs.shape[0], D_MODEL // 2), jnp.uint32),
        compiler_params=pltpu.CompilerParams(dimension_semantics=("arbitrary",), vmem_limit_bytes=VMEM_LIMIT),
        name="experts",
    )(block_e, n_used, n_valid, next_e, parity, phys, xs, w_gate, w_up, w_down)


def _gather_rows(idx, yb):
    mesh = plsc.VectorSubcoreMesh(core_axis_name="core", subcore_axis_name="subcore")
    rows, window = SC_ROWS, SC_INDEX_WINDOW
    sub = window // rows
    n = idx.shape[1]
    width = yb.shape[1]
    assert n % window == 0

    @pl.kernel(out_type=jax.ShapeDtypeStruct((n, width), yb.dtype), mesh=mesh, scratch_types=[])
    def gather(y_hbm, i_hbm, o_hbm):
        def body(indices, i_vmem, o_vmem):
            pltpu.sync_copy(y_hbm.at[i_vmem.at[0, pl.ds(indices[1] * rows, rows)]], o_vmem)

        pltpu.emit_pipeline(
            body, grid=(n // window, sub),
            in_specs=[pl.BlockSpec((1, window), lambda i, j: (0, i))],
            out_specs=[pl.BlockSpec((rows, width), lambda i, j: (i * sub + j, 0))],
            core_axis_name=("core", "subcore"),
            dimension_semantics=(pltpu.PARALLEL, pltpu.ARBITRARY), _explicit_indices=True,
        )(i_hbm, o_hbm)

    return gather(yb, idx)


def _combine_kernel(x1_ref, gate_ref, rf_ref, fin_ref, y0_ref, y1_ref, *out_refs, per_token_gate):
    o_ref = out_refs[-1]
    gates = jnp.transpose(rf_ref[...])
    moe = _unpack_pairs_f32(y0_ref[...]) * gates[:, 0:1] + _unpack_pairs_f32(y1_ref[...]) * gates[:, 1:2]
    gate_f = gate_ref[...] if per_token_gate else gate_ref[...][5:6]
    o_ref[...] = _rms(x1_ref[...] + gate_f * moe, fin_ref[...])


def _combine(x1, gate, rf, final_norm, ys, out_prev, first_tile, n_tiles, *, per_token_gate):
    tc = COMBINE_TILE
    gate_spec = (pl.BlockSpec((tc, D_MODEL), lambda i: (first_tile + i, 0)) if per_token_gate
                 else pl.BlockSpec((6, D_MODEL), lambda i: (0, 0)))
    args = [x1, gate, rf, final_norm, ys, ys]
    in_specs = [pl.BlockSpec((tc, D_MODEL), lambda i: (first_tile + i, 0)),
                gate_spec,
                pl.BlockSpec((8, tc), lambda i: (0, first_tile + i)),
                pl.BlockSpec((1, D_MODEL), lambda i: (0, 0)),
                pl.BlockSpec((tc, D_MODEL // 2), lambda i: (i, 0)),
                pl.BlockSpec((tc, D_MODEL // 2), lambda i: (n_tiles + i, 0))]
    aliases = {}
    if out_prev is not None:
        args.append(out_prev)
        in_specs.append(pl.BlockSpec(memory_space=pl.ANY))
        aliases = {len(args) - 1: 0}
    return pl.pallas_call(
        functools.partial(_combine_kernel, per_token_gate=per_token_gate),
        grid=(n_tiles,), in_specs=in_specs,
        out_specs=pl.BlockSpec((tc, D_MODEL), lambda i: (first_tile + i, 0)),
        out_shape=jax.ShapeDtypeStruct(x1.shape, jnp.float32),
        input_output_aliases=aliases,
        compiler_params=pltpu.CompilerParams(dimension_semantics=("arbitrary",), vmem_limit_bytes=VMEM_LIMIT),
        name="combine",
    )(*args)


def _rope_tables(pos):
    half = HEAD_DIM // 2
    inv_freq = np.power(np.float32(ROPE_THETA), -(np.arange(half, dtype=np.float32) / np.float32(half)))
    ang = (pos.astype(np.float32)[:, None] * inv_freq[None, :].astype(np.float32)).astype(np.float64)
    cos, sin = np.cos(ang).astype(np.float32), np.sin(ang).astype(np.float32)
    cos = np.concatenate([cos, cos, cos, cos], axis=1)
    sin = np.concatenate([-sin, sin, -sin, sin], axis=1)
    return cos, sin


def _prompt_bias():
    r = np.arange(WINDOW)[:, None]
    j = np.arange(2 * WINDOW)[None, :]
    band = (j > r) & (j <= r + WINDOW)
    later = np.where(band, 0.0, NEG).astype(np.float32)
    first = np.where(band & (j >= WINDOW), 0.0, NEG).astype(np.float32)
    return jnp.asarray(np.stack([first, later]))


def _sample_bias(dec):
    cb = SAMPLE_CHUNK
    qb = np.repeat(np.arange(cb), dec)[:, None]
    qi = np.tile(np.arange(dec), cb)[:, None]
    kb = np.concatenate([np.repeat(np.arange(cb), WINDOW), np.repeat(np.arange(cb), dec)])[None, :]
    kj = np.concatenate([np.tile(np.arange(WINDOW), cb), np.tile(np.arange(dec), cb)])[None, :]
    is_new = (np.arange(cb * WINDOW + cb * dec) >= cb * WINDOW)[None, :]
    ok = (kb == qb) & np.where(is_new, kj <= qi, kj > qi)
    ok = np.pad(ok, ((0, 0), (0, SAMPLE_KEYS - ok.shape[1])))
    return jnp.asarray(np.where(ok, 0.0, NEG).astype(np.float32))


def kernel(x_prompt, x_sample, cache_k, cache_v, state_conv, c_prompt, c_sample, attn_norm, ffn_norm, w_mod, b_mod, w_in, conv_w, attn_sinks, out_norm_attn, out_norm_conv, w_out, w_group, b_group, w_expert, b_expert, w_gate, w_up, w_down, final_norm):
    assert attn_norm.shape[0] == 1, "one layer"
    bsz, seq, _ = x_prompt.shape
    nseq, dec, _ = x_sample.shape
    m_prompt = bsz * seq
    n_s = nseq * dec
    m_total = m_prompt + n_s
    tm = MIXER_TILE
    assert seq % tm == 0 and n_s % SAMPLE_TILE == 0 and SAMPLE_TILE % (SAMPLE_CHUNK * dec) == 0
    assert cache_k.shape[2] == WINDOW and m_prompt % SAMPLE_TILE == 0 and m_total % COMBINE_TILE == 0
    bf = jnp.bfloat16

    w_r = jnp.zeros((ROUTER_ROWS, D_MODEL), jnp.float32)
    w_r = w_r.at[0:N_GROUPS].set(w_group[0].T).at[8:8 + N_EXPERTS].set(w_expert[0].T).astype(bf)
    b_r = jnp.zeros((ROUTER_ROWS,), jnp.float32).at[0:N_GROUPS].set(b_group[0]).at[8:8 + N_EXPERTS].set(b_expert[0])
    b_r = jnp.broadcast_to(b_r[:, None], (ROUTER_ROWS, LANES))
    cos_p, sin_p = _rope_tables(np.arange(seq))
    cos_s, sin_s = _rope_tables(PAST_LEN + np.arange(dec))
    tri = jnp.asarray(np.triu(np.ones((tm, tm), np.float32), 1)).astype(bf)
    consts = dict(
        anorm=attn_norm[0][None], fnorm=ffn_norm[0][None], w_in=w_in[0].astype(bf),
        cos_p=jnp.asarray(cos_p), sin_p=jnp.asarray(sin_p),
        cos_s=jnp.asarray(np.tile(cos_s, (nseq, 1))), sin_s=jnp.asarray(np.tile(sin_s, (nseq, 1))),
        bias_p=_prompt_bias(), bias_s=_sample_bias(dec), conv_w=conv_w[0],
        gn_attn=out_norm_attn[0][None], gn_conv=out_norm_conv[0][None], w_out=w_out[0].astype(bf),
        w_r=w_r, b_r=b_r, tri=tri)
    sinks = attn_sinks[0]

    mod = _adaln(jnp.concatenate([c_prompt, c_sample], axis=0), w_mod[0], b_mod[0])
    mod_p = mod[:bsz].reshape(bsz, 6, D_MODEL)
    mod_s = mod[bsz:]

    bm = EXPERT_BLOCK
    eids = jnp.arange(N_EXPERTS, dtype=jnp.int32)
    lookup = lambda table, e: jnp.sum(jnp.where(e[..., None] == eids, table, 0), axis=-1)
    by_token = lambda a: jnp.transpose(a, (1, 0, 2)).reshape(8, -1)
    ceil_bm = lambda c: (c + bm - 1) // bm * bm
    x1_p, h2_p, ri_p, rf_p, cnt_p, k_p, v_p, u_p = _prompt_mixer(x_prompt, mod_p, sinks, consts)
    cnt_p = cnt_p[:, 0].astype(jnp.int32)
    padded_p = ceil_bm(cnt_p)
    pstart_p = jnp.cumsum(padded_p) - padded_p
    nb_p = -(-(2 * m_prompt + N_EXPERTS * (bm - 1)) // bm)
    nb_o = -(-(2 * n_s + N_EXPERTS * (bm - 1)) // bm)
    n_blocks = nb_p + nb_o
    ri_p = by_token(ri_p)
    dest_p = (lookup(pstart_p, ri_p[0:2]) + ri_p[2:4]).reshape(2 * m_prompt)

    st = state_conv[0]
    zrow = jnp.zeros((nseq, 1, CONV_DIM), jnp.float32)
    e1 = jnp.concatenate([st[:, 1:2]] + [zrow] * (dec - 1), axis=1).reshape(n_s, CONV_DIM)
    e2 = jnp.concatenate([st[:, 0:1], st[:, 1:2]] + [zrow] * (dec - 2), axis=1).reshape(n_s, CONV_DIM)
    dest_p, e1 = lax.optimization_barrier((dest_p, e1))
    xs = _dispatch(dest_p[None], h2_p, n_blocks * bm)
    ck = jnp.transpose(cache_k[0], (0, 2, 3, 1)).reshape(nseq * KV_DIM, WINDOW)
    cv = jnp.transpose(cache_v[0], (0, 2, 3, 1)).reshape(nseq * KV_DIM, WINDOW)
    x1_s, h2_s, ri_s, rf_s, cnt_s, k_s, v_s, u_s = _sample_mixer(
        x_sample.reshape(n_s, D_MODEL), mod_s, sinks, consts, ck, cv, e1, e2,
        jnp.zeros((N_EXPERTS, LANES), jnp.float32), dec)

    cnt_s = cnt_s[:, 0].astype(jnp.int32)
    fit = jnp.minimum(cnt_s, padded_p - cnt_p)
    over = cnt_s - fit
    padded_o = ceil_bm(over)
    ostart = nb_p * bm + jnp.cumsum(padded_o) - padded_o
    ri_s = by_token(ri_s)
    e_s, r_s = ri_s[0:2], ri_s[2:4]
    dest_s = jnp.where(r_s < lookup(fit, e_s), lookup(pstart_p + cnt_p, e_s) + r_s, lookup(ostart - fit, e_s) + r_s)
    dest_s = dest_s.reshape(2 * n_s)

    from_t = lambda a: jnp.transpose(a.reshape(nseq, N_KV_HEADS, HEAD_DIM, WINDOW), (0, 3, 1, 2))[None]
    new_k_s, new_v_s = from_t(k_s), from_t(v_s)
    xs = _scatter_rows(dest_s, h2_s, xs)

    np_e, no_e = padded_p // bm, padded_o // bm
    lend = jnp.cumsum(np_e + no_e)
    lstart = lend - np_e - no_e
    n_used = lend[-1].astype(jnp.int32)
    blk = jnp.arange(n_blocks, dtype=jnp.int32)
    present = np_e + no_e > 0
    last_e = jnp.max(jnp.where(present, eids, 0))
    block_e = jnp.sum((lend[None, :] <= blk[:, None]).astype(jnp.int32), axis=1)
    block_e = jnp.where(blk < n_used, jnp.minimum(block_e, N_EXPERTS - 1), last_e)
    of_block = lambda table: lookup(table, block_e)
    within = blk - of_block(lstart)
    in_prompt = within < of_block(np_e)
    phys = jnp.where(in_prompt, of_block(pstart_p) // bm + within, of_block(ostart) // bm + within - of_block(np_e))
    n_valid = jnp.where(in_prompt, of_block(cnt_p + fit) - within * bm, of_block(over) - (within - of_block(np_e)) * bm)
    n_valid = jnp.where(blk < n_used, jnp.clip(n_valid, 0, bm), 0).astype(jnp.int32)
    phys = jnp.where(blk < n_used, phys, jnp.sum(jnp.where(blk == n_used - 1, phys, 0))).astype(jnp.int32)
    later = jnp.logical_and(present[None, :], eids[None, :] > eids[:, None])
    next_of = jnp.min(jnp.where(later, eids[None, :], N_EXPERTS), axis=1)
    next_of = jnp.where(next_of == N_EXPERTS, -1, next_of)
    next_e = of_block(next_of).astype(jnp.int32)
    parity = of_block((jnp.cumsum(present.astype(jnp.int32)) - 1) % 2).astype(jnp.int32)
    yb = _experts(block_e, n_used.reshape(1), n_valid, next_e, parity, phys, xs, w_gate[0], w_up[0], w_down[0])

    fin = final_norm[None]
    tc = COMBINE_TILE
    rf_p, rf_s = by_token(rf_p), by_token(rf_s)
    y_p = None
    chunk = min(COMBINE_CHUNK, seq)
    assert seq % chunk == 0 and chunk % tc == 0
    for c in range(m_prompt // chunk):
        lo, hi = c * chunk, (c + 1) * chunk
        idx = jnp.concatenate([dest_p[lo:hi], dest_p[m_prompt + lo:m_prompt + hi]])[None]
        y_p = _combine(x1_p, mod_p[lo // seq], rf_p, fin, _gather_rows(idx, yb), y_p, lo // tc, chunk // tc,
                       per_token_gate=False)
    gate_s = jnp.repeat(mod_s[:, 5 * D_MODEL:], dec, axis=0)
    y_s = _combine(x1_s, gate_s, rf_s, fin, _gather_rows(dest_s[None], yb), None, 0, n_s // tc, per_token_gate=True)

    keep = WINDOW
    new_k_p = k_p.reshape(1, bsz, keep, N_KV_HEADS, HEAD_DIM)
    new_v_p = v_p.reshape(1, bsz, keep, N_KV_HEADS, HEAD_DIM)
    new_u_p = u_p[:, 6:8][None]
    u_all = jnp.concatenate([st, u_s.reshape(nseq, dec, CONV_DIM)], axis=1)
    new_u_s = u_all[:, -2:][None]
    return (y_p.reshape(bsz, seq, D_MODEL), y_s.reshape(nseq, dec, D_MODEL),
            new_k_p, new_v_p, new_u_p, new_k_s, new_v_s, new_u_s)
```

```python
import functools

import jax
import jax.numpy as jnp
import numpy as np
from jax import lax
from jax.experimental import pallas as pl
from jax.experimental.pallas import tpu as pltpu
from jax.experimental.pallas import tpu_sc as plsc

D_MODEL = 1024
HEAD_DIM = 64
ATTN_DIM = 512
N_Q_HEADS = 8
N_KV_HEADS = 2
KV_DIM = 128
CONV_DIM = 512
WINDOW = 128
IN_PROJ_DIM = ATTN_DIM + 2 * KV_DIM + 3 * CONV_DIM
N_GROUPS = 4
EXPERTS_PER_GROUP = 8
N_EXPERTS = 32
EXPERT_HIDDEN = 512
RMS_EPS = 1e-5
ROPE_THETA = 10000.0
PAST_LEN = 16384

LANES = 128
ROUTER_ROWS = 48
NEG = -1e30
LOG2_E = 1.4426950408889634
Q_SCALE = HEAD_DIM ** -0.5 * LOG2_E

MIXER_TILE = 1024
SAMPLE_TILE = 128
SAMPLE_CHUNK = 8
SAMPLE_KEYS = (SAMPLE_CHUNK + 1) * WINDOW
EXPERT_BLOCK = 512
COMBINE_TILE = 512
COMBINE_CHUNK = 4096
SC_ROWS = 32
SC_INDEX_WINDOW = 128
VMEM_LIMIT = 56 * 1024 * 1024


def _rms(x, gain):
    ms = jnp.mean(x * x, axis=-1, keepdims=True)
    return x * lax.rsqrt(ms + RMS_EPS) * gain


def _half_lane_mask(rows=1):
    return lax.broadcasted_iota(jnp.int32, (rows, LANES), 1) < HEAD_DIM


def _rope(x, cos, sin_signed):
    t, w = x.shape
    reps = w // LANES
    lane = lax.broadcasted_iota(jnp.int32, (t, w), 1)
    upper = (lane % HEAD_DIM) >= (HEAD_DIM // 2)
    partner = jnp.where(upper, pltpu.roll(x, HEAD_DIM // 2, 1), pltpu.roll(x, w - HEAD_DIM // 2, 1))
    if reps > 1:
        cos = jnp.concatenate([cos] * reps, axis=1)
        sin_signed = jnp.concatenate([sin_signed] * reps, axis=1)
    return x * cos + partner * sin_signed


def _swap_halves(x):
    return pltpu.roll(x, HEAD_DIM, 1)


def _group_norm64(x, gain):
    t, w = x.shape
    lo = _half_lane_mask(t)
    outs = []
    for c in range(w // LANES):
        xc = x[:, c * LANES:(c + 1) * LANES]
        sq = xc * xc
        s_lo = jnp.sum(jnp.where(lo, sq, 0.0), axis=-1, keepdims=True)
        s_hi = jnp.sum(sq, axis=-1, keepdims=True) - s_lo
        r = jnp.where(lo, lax.rsqrt(s_lo * (1.0 / HEAD_DIM) + RMS_EPS), lax.rsqrt(s_hi * (1.0 / HEAD_DIM) + RMS_EPS))
        outs.append(xc * r)
    return jnp.concatenate(outs, axis=1) * gain


def _attend(q_rows, keys, keys_sw, vals, vals_sw, bias, sinks_ref, group, transposed=False):
    r = q_rows.shape[0]
    lo = _half_lane_mask(r)
    outs = []
    for parity in range(2):
        heads = [4 * group + parity, 4 * group + parity + 2]
        kk, vv = (keys, vals) if parity == group else (keys_sw, vals_sw)
        half = lo if parity == 0 else jnp.logical_not(lo)
        qs = []
        for h in heads:
            pair = q_rows[:, (h // 2) * LANES:(h // 2 + 1) * LANES]
            qs.append(jnp.where(half, pair, jnp.zeros_like(pair)))
        qcat = jnp.concatenate(qs, axis=0)
        s = (jnp.dot(qcat, kk, preferred_element_type=jnp.float32) if transposed else
             lax.dot_general(qcat, kk, (((1,), (1,)), ((), ())), preferred_element_type=jnp.float32))
        ps, dens = [], []
        for i, h in enumerate(heads):
            sh = s[i * r:(i + 1) * r] + bias
            sink = sinks_ref[h] * LOG2_E
            m = jnp.maximum(jnp.max(sh, axis=-1, keepdims=True), sink)
            p = jnp.exp2(sh - m)
            dens.append(jnp.sum(p, axis=-1, keepdims=True) + jnp.exp2(sink - m))
            ps.append(p.astype(jnp.bfloat16))
        pcat = jnp.concatenate(ps, axis=0)
        o = (lax.dot_general(pcat, vv, (((1,), (1,)), ((), ())), preferred_element_type=jnp.float32) if transposed
             else jnp.dot(pcat, vv, preferred_element_type=jnp.float32))
        for i, h in enumerate(heads):
            oh = o[i * r:(i + 1) * r] / dens[i]
            ms = jnp.sum(jnp.where(half, oh * oh, 0.0), axis=-1, keepdims=True) * (1.0 / HEAD_DIM)
            outs.append((h, oh * lax.rsqrt(ms + RMS_EPS)))
    outs.sort(key=lambda t: t[0])
    return [o for _, o in outs]


def _merge_heads(head_outs, lo):
    pairs = [jnp.where(lo, head_outs[2 * j], head_outs[2 * j + 1]) for j in range(4)]
    return jnp.concatenate(pairs, axis=1)


def _route(h2b, wr_ref, br_ref, tri_ref, cnt_ref, ri_ref, rf_ref):
    t = h2b.shape[0]
    lg = lax.dot_general(wr_ref[...], h2b, (((1,), (1,)), ((), ())), preferred_element_type=jnp.float32)
    lg = lg + br_ref[...][:, 0:1]
    r8 = lax.broadcasted_iota(jnp.int32, (8, t), 0)
    lgrp = jnp.where(r8 < N_GROUPS, lg[0:8], -jnp.inf)
    gmax = jnp.max(lgrp, axis=0, keepdims=True)
    grp = jnp.min(jnp.where(lgrp == gmax, r8, 8), axis=0, keepdims=True)
    pg_sel = 1.0 / jnp.sum(jnp.exp(lgrp - gmax), axis=0, keepdims=True)
    r32 = lax.broadcasted_iota(jnp.int32, (N_EXPERTS, t), 0)
    le = jnp.where((r32 // EXPERTS_PER_GROUP) == grp, lg[8:8 + N_EXPERTS], -jnp.inf)
    v1 = jnp.max(le, axis=0, keepdims=True)
    i1 = jnp.min(jnp.where(le == v1, r32, N_EXPERTS), axis=0, keepdims=True)
    le2 = jnp.where(r32 == i1, -jnp.inf, le)
    v2 = jnp.max(le2, axis=0, keepdims=True)
    i2 = jnp.min(jnp.where(le2 == v2, r32, N_EXPERTS), axis=0, keepdims=True)
    e21 = jnp.exp(v2 - v1)
    gate1 = pg_sel / (1.0 + e21)
    gate2 = pg_sel * e21 / (1.0 + e21)
    hot1 = r32 == i1
    hot2 = r32 == i2
    onehot = jnp.where(jnp.logical_or(hot1, hot2), 1.0, 0.0)
    before = jnp.dot(onehot.astype(jnp.bfloat16), tri_ref[...], preferred_element_type=jnp.float32)
    pos = before + cnt_ref[...][:, 0:1]
    rank1 = jnp.sum(jnp.where(hot1, pos, 0.0), axis=0, keepdims=True)
    rank2 = jnp.sum(jnp.where(hot2, pos, 0.0), axis=0, keepdims=True)
    cnt_ref[...] = cnt_ref[...] + jnp.sum(onehot, axis=1, keepdims=True)
    zi = jnp.zeros((4, t), jnp.int32)
    ri_ref[...] = jnp.concatenate([i1, i2, rank1.astype(jnp.int32), rank2.astype(jnp.int32), zi], axis=0)
    rf_ref[...] = jnp.concatenate([gate1, gate2, jnp.zeros((6, t), jnp.float32)], axis=0)


def _pack_bf16_pairs(xb):
    w = xb.shape[1] // 2
    bits = pltpu.bitcast(xb.astype(jnp.float32), jnp.uint32)
    return (bits[:, :w] >> 16) | (bits[:, w:] & jnp.uint32(0xFFFF0000))


def _unpack_pairs_f32(words):
    lo = pltpu.bitcast(words << 16, jnp.float32)
    hi = pltpu.bitcast(words & jnp.uint32(0xFFFF0000), jnp.float32)
    return jnp.concatenate([lo, hi], axis=1)


def _unpack_bf16_pairs(words):
    return _unpack_pairs_f32(words).astype(jnp.bfloat16)


def _split_proj(proj):
    a = ATTN_DIM
    q = proj[:, :a]
    k = proj[:, a:a + KV_DIM]
    v = proj[:, a + KV_DIM:a + 2 * KV_DIM]
    c0 = a + 2 * KV_DIM
    h_conv = proj[:, c0:c0 + CONV_DIM]
    gate_b = proj[:, c0 + CONV_DIM:c0 + 2 * CONV_DIM]
    gate_c = proj[:, c0 + 2 * CONV_DIM:c0 + 3 * CONV_DIM]
    return q, k, v, gate_c * h_conv, gate_b


def _mixer_tail(x, attn_n, conv_n, mod, wout_ref, fnorm_ref, wr_ref, br_ref, tri_ref, cnt_ref,
                x1_ref, h2_ref, ri_ref, rf_ref):
    gate_a, shift_f, scale_f = mod
    cat = jnp.concatenate([attn_n.astype(jnp.bfloat16), conv_n.astype(jnp.bfloat16)], axis=1)
    mix = jnp.dot(cat, wout_ref[...], preferred_element_type=jnp.float32)
    x1 = x + gate_a * mix
    x1_ref[...] = x1
    h2 = _rms(x1, fnorm_ref[...] * (1.0 + scale_f)) + shift_f
    h2b = h2.astype(jnp.bfloat16)
    h2_ref[...] = _pack_bf16_pairs(h2b)
    _route(h2b, wr_ref, br_ref, tri_ref, cnt_ref, ri_ref, rf_ref)


def _adaln_kernel(c_ref, w_ref, b_ref, o_ref):
    c = c_ref[...]
    a = (c * jax.nn.sigmoid(c)).astype(jnp.bfloat16)
    o_ref[...] = jnp.dot(a, w_ref[...].astype(jnp.bfloat16), preferred_element_type=jnp.float32) + b_ref[...]


def _adaln(c, w_mod, b_mod):
    n = c.shape[0]
    tn = 1536
    return pl.pallas_call(
        _adaln_kernel,
        grid=(w_mod.shape[1] // tn,),
        in_specs=[pl.BlockSpec((n, D_MODEL), lambda j: (0, 0)),
                  pl.BlockSpec((D_MODEL, tn), lambda j: (0, j)),
                  pl.BlockSpec((1, tn), lambda j: (0, j))],
        out_specs=pl.BlockSpec((n, tn), lambda j: (0, j)),
        out_shape=jax.ShapeDtypeStruct((n, w_mod.shape[1]), jnp.float32),
        compiler_params=pltpu.CompilerParams(dimension_semantics=("arbitrary",), vmem_limit_bytes=VMEM_LIMIT),
        name="adaln",
    )(c, w_mod, b_mod.reshape(1, -1))


def _prompt_mixer_kernel(sinks_ref, x_ref, mod_ref, anorm_ref, fnorm_ref, win_ref, cos_ref, sin_ref, bias_ref,
                         convw_ref, gna_ref, gnc_ref, wout_ref, wr_ref, br_ref, tri_ref,
                         x1_ref, h2_ref, ri_ref, rf_ref, cnt_out_ref, knew_ref, vnew_ref, unew_ref,
                         q_s, k_s, ksw_s, v_s, vsw_s, u_s, att_s, cnt_s):
    b = pl.program_id(0)
    t = pl.program_id(1)
    tm = MIXER_TILE
    nblk = tm // WINDOW

    @pl.when(jnp.logical_and(b == 0, t == 0))
    def _():
        cnt_s[...] = jnp.zeros_like(cnt_s)

    @pl.when(t == 0)
    def _():
        z = jnp.zeros((WINDOW, KV_DIM), jnp.bfloat16)
        k_s[0:WINDOW, :] = z
        ksw_s[0:WINDOW, :] = z
        v_s[0:WINDOW, :] = z
        vsw_s[0:WINDOW, :] = z
        u_s[0:8, :] = jnp.zeros((8, CONV_DIM), jnp.float32)

    x = x_ref[...]
    mod = mod_ref[...]
    h = _rms(x, anorm_ref[...] * (1.0 + mod[1:2])) + mod[0:1]
    proj = jnp.dot(h.astype(jnp.bfloat16), win_ref[...], preferred_element_type=jnp.float32)
    q, k, v, u, gate_b = _split_proj(proj)
    cos = cos_ref[...]
    sin = sin_ref[...]
    q = _rope(q, cos, sin) * Q_SCALE
    k = _rope(k, cos, sin)
    q_s[...] = q.astype(jnp.bfloat16)
    k_s[WINDOW:, :] = k.astype(jnp.bfloat16)
    ksw_s[WINDOW:, :] = _swap_halves(k).astype(jnp.bfloat16)
    v_s[WINDOW:, :] = v.astype(jnp.bfloat16)
    vsw_s[WINDOW:, :] = _swap_halves(v).astype(jnp.bfloat16)
    knew_ref[...] = k[tm - WINDOW:, :]
    vnew_ref[...] = v[tm - WINDOW:, :]

    u_s[8:, :] = u
    unew_ref[...] = u[tm - 8:, :]
    cw = convw_ref[...]
    conv = (u_s[6:6 + tm, :] * cw[0:1] + u_s[7:7 + tm, :] * cw[1:2] + u * cw[2:3]) * gate_b
    u_s[0:8, :] = u[tm - 8:, :]
    conv_n = _group_norm64(conv, gnc_ref[...])

    lo = _half_lane_mask(WINDOW)
    first = t == 0

    def block(i, carry):
        r0 = pl.multiple_of(i * WINDOW, WINDOW)
        bias = bias_ref[jnp.where(jnp.logical_and(first, i == 0), 0, 1)]
        qb = q_s[pl.ds(r0, WINDOW), :]
        keys = k_s[pl.ds(r0, 2 * WINDOW), :]
        keys_sw = ksw_s[pl.ds(r0, 2 * WINDOW), :]
        vals = v_s[pl.ds(r0, 2 * WINDOW), :]
        vals_sw = vsw_s[pl.ds(r0, 2 * WINDOW), :]
        outs = []
        for g in range(N_KV_HEADS):
            outs += _attend(qb, keys, keys_sw, vals, vals_sw, bias, sinks_ref, g)
        att_s[pl.ds(r0, WINDOW), :] = _merge_heads(outs, lo)
        return carry

    lax.fori_loop(0, nblk, block, 0, unroll=True)
    k_s[0:WINDOW, :] = k_s[tm:tm + WINDOW, :]
    ksw_s[0:WINDOW, :] = ksw_s[tm:tm + WINDOW, :]
    v_s[0:WINDOW, :] = v_s[tm:tm + WINDOW, :]
    vsw_s[0:WINDOW, :] = vsw_s[tm:tm + WINDOW, :]

    attn_n = att_s[...] * gna_ref[...]
    _mixer_tail(x, attn_n, conv_n, (mod[2:3], mod[3:4], mod[4:5]), wout_ref, fnorm_ref, wr_ref, br_ref, tri_ref,
                cnt_s, x1_ref, h2_ref, ri_ref, rf_ref)
    cnt_out_ref[...] = cnt_s[...]


def _prompt_mixer(x, mod, sinks, consts):
    bsz, seq, _ = x.shape
    tm = MIXER_TILE
    nt = seq // tm
    full = lambda shape: pl.BlockSpec(shape, lambda b, t, s: (0,) * len(shape))
    in_specs = [
        pl.BlockSpec((None, tm, D_MODEL), lambda b, t, s: (b, t, 0)),
        pl.BlockSpec((None, 6, D_MODEL), lambda b, t, s: (b, 0, 0)),
        full((1, D_MODEL)), full((1, D_MODEL)),
        full((D_MODEL, IN_PROJ_DIM)),
        pl.BlockSpec((tm, LANES), lambda b, t, s: (t, 0)),
        pl.BlockSpec((tm, LANES), lambda b, t, s: (t, 0)),
        full((2, WINDOW, 2 * WINDOW)),
        full((3, CONV_DIM)), full((1, ATTN_DIM)), full((1, CONV_DIM)),
        full((D_MODEL, D_MODEL)),
        full((ROUTER_ROWS, D_MODEL)), full((ROUTER_ROWS, LANES)),
        full((tm, tm)),
    ]
    out_shape = [
        jax.ShapeDtypeStruct((bsz * seq, D_MODEL), jnp.float32),
        jax.ShapeDtypeStruct((bsz * seq, D_MODEL // 2), jnp.uint32),
        jax.ShapeDtypeStruct((bsz * nt, 8, tm), jnp.int32),
        jax.ShapeDtypeStruct((bsz * nt, 8, tm), jnp.float32),
        jax.ShapeDtypeStruct((N_EXPERTS, LANES), jnp.float32),
        jax.ShapeDtypeStruct((bsz, WINDOW, KV_DIM), jnp.float32),
        jax.ShapeDtypeStruct((bsz, WINDOW, KV_DIM), jnp.float32),
        jax.ShapeDtypeStruct((bsz, 8, CONV_DIM), jnp.float32),
    ]
    out_specs = [
        pl.BlockSpec((tm, D_MODEL), lambda b, t, s: (b * nt + t, 0)),
        pl.BlockSpec((tm, D_MODEL // 2), lambda b, t, s: (b * nt + t, 0)),
        pl.BlockSpec((None, 8, tm), lambda b, t, s: (b * nt + t, 0, 0)),
        pl.BlockSpec((None, 8, tm), lambda b, t, s: (b * nt + t, 0, 0)),
        pl.BlockSpec((N_EXPERTS, LANES), lambda b, t, s: (0, 0)),
        pl.BlockSpec((None, WINDOW, KV_DIM), lambda b, t, s: (b, 0, 0)),
        pl.BlockSpec((None, WINDOW, KV_DIM), lambda b, t, s: (b, 0, 0)),
        pl.BlockSpec((None, 8, CONV_DIM), lambda b, t, s: (b, 0, 0)),
    ]
    scratch = [
        pltpu.VMEM((tm, ATTN_DIM), jnp.bfloat16),
        pltpu.VMEM((tm + WINDOW, KV_DIM), jnp.bfloat16), pltpu.VMEM((tm + WINDOW, KV_DIM), jnp.bfloat16),
        pltpu.VMEM((tm + WINDOW, KV_DIM), jnp.bfloat16), pltpu.VMEM((tm + WINDOW, KV_DIM), jnp.bfloat16),
        pltpu.VMEM((tm + 8, CONV_DIM), jnp.float32),
        pltpu.VMEM((tm, ATTN_DIM), jnp.float32),
        pltpu.VMEM((N_EXPERTS, LANES), jnp.float32),
    ]
    return pl.pallas_call(
        _prompt_mixer_kernel,
        grid_spec=pltpu.PrefetchScalarGridSpec(
            num_scalar_prefetch=1, grid=(bsz, nt), in_specs=in_specs, out_specs=out_specs, scratch_shapes=scratch),
        out_shape=out_shape,
        compiler_params=pltpu.CompilerParams(dimension_semantics=("arbitrary", "arbitrary"),
                                             vmem_limit_bytes=VMEM_LIMIT),
        name="prompt_mixer",
    )(sinks, x, mod, consts["anorm"], consts["fnorm"], consts["w_in"], consts["cos_p"], consts["sin_p"],
      consts["bias_p"], consts["conv_w"], consts["gn_attn"], consts["gn_conv"], consts["w_out"],
      consts["w_r"], consts["b_r"], consts["tri"])


def _sample_mixer_kernel(sinks_ref, x_ref, mod_ref, anorm_ref, fnorm_ref, win_ref, cos_ref, sin_ref, bias_ref,
                         convw_ref, gna_ref, gnc_ref, wout_ref, wr_ref, br_ref, tri_ref,
                         ck_ref, cv_ref, e1_ref, e2_ref, cnt_in_ref,
                         x1_ref, h2_ref, ri_ref, rf_ref, cnt_out_ref, knew_ref, vnew_ref, unew_ref,
                         q_s, k_s, v_s, att_s, cnt_s, *, dec):
    n = x_ref.shape[0]
    cb = SAMPLE_CHUNK
    rows = cb * dec

    @pl.when(pl.program_id(0) == 0)
    def _():
        cnt_s[...] = cnt_in_ref[...]

    x = x_ref[...]
    mod = jnp.repeat(mod_ref[...], dec, axis=0)
    md = lambda j: mod[:, j * D_MODEL:(j + 1) * D_MODEL]
    h = _rms(x, anorm_ref[...]) * (1.0 + md(1)) + md(0)
    proj = jnp.dot(h.astype(jnp.bfloat16), win_ref[...], preferred_element_type=jnp.float32)
    q, k, v, u, gate_b = _split_proj(proj)
    cos = cos_ref[...]
    sin = sin_ref[...]
    q = _rope(q, cos, sin) * Q_SCALE
    k = _rope(k, cos, sin)
    q_s[...] = q.astype(jnp.bfloat16)
    k_s[...] = k
    v_s[...] = v
    unew_ref[...] = u

    pos = lax.broadcasted_iota(jnp.int32, (n, CONV_DIM), 0) % dec
    um1 = jnp.where(pos >= 1, pltpu.roll(u, 1, 0), 0.0) + e1_ref[...]
    um2 = jnp.where(pos >= 2, pltpu.roll(u, 2, 0), 0.0) + e2_ref[...]
    cw = convw_ref[...]
    conv = (um2 * cw[0:1] + um1 * cw[1:2] + u * cw[2:3]) * gate_b
    conv_n = _group_norm64(conv, gnc_ref[...])

    lo = _half_lane_mask(rows)
    bias = bias_ref[...]

    half = KV_DIM // 2
    lane = lax.broadcasted_iota(jnp.int32, (KV_DIM, WINDOW), 1)
    swap_rows = lambda t: jnp.concatenate([t[half:], t[:half]], axis=0)

    for c in range(n // rows):
        r0 = c * rows
        qb = q_s[r0:r0 + rows, :]
        cats = []
        for ref, new_rows, out_ref in ((ck_ref, k_s[r0:r0 + rows, :], knew_ref), (cv_ref, v_s[r0:r0 + rows, :], vnew_ref)):
            tiles = [ref[(c * cb + b) * KV_DIM:(c * cb + b + 1) * KV_DIM, :] for b in range(cb)]
            new_t = jnp.concatenate([jnp.transpose(new_rows), jnp.zeros((KV_DIM, WINDOW - rows), jnp.float32)], axis=1)
            cat = jnp.concatenate(tiles + [new_t], axis=1)
            cats.append((cat.astype(jnp.bfloat16), swap_rows(cat).astype(jnp.bfloat16)))
            for b in range(cb):
                kept = pltpu.roll(tiles[b], WINDOW - dec, 1)
                fresh = pltpu.roll(new_t, (WINDOW - dec - b * dec) % WINDOW, 1)
                out_ref[(c * cb + b) * KV_DIM:(c * cb + b + 1) * KV_DIM, :] = jnp.where(lane < WINDOW - dec, kept, fresh)
        (keys, keys_sw), (vals, vals_sw) = cats
        outs = []
        for g in range(N_KV_HEADS):
            outs += _attend(qb, keys, keys_sw, vals, vals_sw, bias, sinks_ref, g, transposed=True)
        att_s[r0:r0 + rows, :] = _merge_heads(outs, lo)

    attn_n = att_s[...] * gna_ref[...]
    _mixer_tail(x, attn_n, conv_n, (md(2), md(3), md(4)), wout_ref, fnorm_ref, wr_ref, br_ref, tri_ref,
                cnt_s, x1_ref, h2_ref, ri_ref, rf_ref)
    cnt_out_ref[...] = cnt_s[...]


def _sample_mixer(x, mod, sinks, consts, ck, cv, e1, e2, cnt_in, dec):
    n_s = x.shape[0]
    ts = SAMPLE_TILE
    seqs = ts // dec
    full = lambda a: pl.BlockSpec(a.shape, lambda i, s: (0,) * a.ndim)
    rows = lambda width: pl.BlockSpec((ts, width), lambda i, s: (i, 0))
    in_specs = [
        rows(D_MODEL), pl.BlockSpec((seqs, 6 * D_MODEL), lambda i, s: (i, 0)),
        full(consts["anorm"]), full(consts["fnorm"]), full(consts["w_in"]),
        rows(LANES), rows(LANES), full(consts["bias_s"]),
        full(consts["conv_w"]), full(consts["gn_attn"]), full(consts["gn_conv"]), full(consts["w_out"]),
        full(consts["w_r"]), full(consts["b_r"]),
        pl.BlockSpec((ts, ts), lambda i, s: (0, 0)),
        pl.BlockSpec((seqs * KV_DIM, WINDOW), lambda i, s: (i, 0)),
        pl.BlockSpec((seqs * KV_DIM, WINDOW), lambda i, s: (i, 0)),
        rows(CONV_DIM), rows(CONV_DIM), full(cnt_in),
    ]
    args = [x, mod, consts["anorm"], consts["fnorm"], consts["w_in"], consts["cos_s"], consts["sin_s"],
            consts["bias_s"], consts["conv_w"], consts["gn_attn"], consts["gn_conv"], consts["w_out"],
            consts["w_r"], consts["b_r"], consts["tri"], ck, cv, e1, e2, cnt_in]
    nt = n_s // ts
    out_shape = [
        jax.ShapeDtypeStruct((n_s, D_MODEL), jnp.float32),
        jax.ShapeDtypeStruct((n_s, D_MODEL // 2), jnp.uint32),
        jax.ShapeDtypeStruct((nt, 8, ts), jnp.int32),
        jax.ShapeDtypeStruct((nt, 8, ts), jnp.float32),
        jax.ShapeDtypeStruct((N_EXPERTS, LANES), jnp.float32),
        jax.ShapeDtypeStruct(ck.shape, jnp.float32),
        jax.ShapeDtypeStruct(cv.shape, jnp.float32),
        jax.ShapeDtypeStruct((n_s, CONV_DIM), jnp.float32),
    ]
    out_specs = [
        rows(D_MODEL),
        rows(D_MODEL // 2),
        pl.BlockSpec((None, 8, ts), lambda i, s: (i, 0, 0)),
        pl.BlockSpec((None, 8, ts), lambda i, s: (i, 0, 0)),
        pl.BlockSpec((N_EXPERTS, LANES), lambda i, s: (0, 0)),
        pl.BlockSpec((seqs * KV_DIM, WINDOW), lambda i, s: (i, 0)),
        pl.BlockSpec((seqs * KV_DIM, WINDOW), lambda i, s: (i, 0)),
        rows(CONV_DIM),
    ]
    scratch = [
        pltpu.VMEM((ts, ATTN_DIM), jnp.bfloat16),
        pltpu.VMEM((ts, KV_DIM), jnp.float32), pltpu.VMEM((ts, KV_DIM), jnp.float32),
        pltpu.VMEM((ts, ATTN_DIM), jnp.float32),
        pltpu.VMEM((N_EXPERTS, LANES), jnp.float32),
    ]
    return pl.pallas_call(
        functools.partial(_sample_mixer_kernel, dec=dec),
        grid_spec=pltpu.PrefetchScalarGridSpec(
            num_scalar_prefetch=1, grid=(nt,), in_specs=in_specs, out_specs=out_specs, scratch_shapes=scratch),
        out_shape=out_shape,
        compiler_params=pltpu.CompilerParams(dimension_semantics=("arbitrary",), vmem_limit_bytes=VMEM_LIMIT),
        name="sample_mixer",
    )(sinks, *args)


def _dispatch(idx, h2, n_rows):
    mesh = plsc.VectorSubcoreMesh(core_axis_name="core", subcore_axis_name="subcore")
    rows, window = SC_ROWS, SC_INDEX_WINDOW
    sub = window // rows
    n_tok, width = h2.shape
    assert n_tok % rows == 0 and (2 * n_tok) % window == 0

    moved = 3 * n_tok * width * h2.dtype.itemsize
    @pl.kernel(out_type=jax.ShapeDtypeStruct((n_rows, width), h2.dtype), mesh=mesh, scratch_types=[],
               cost_estimate=pl.CostEstimate(flops=0, transcendentals=0, bytes_accessed=moved))
    def scatter(h_hbm, i_hbm, xs_hbm):
        def body(indices, h_vmem, i_vmem):
            pltpu.sync_copy(h_vmem, xs_hbm.at[i_vmem.at[0, pl.ds(indices[1] * rows, rows)]])

        pltpu.emit_pipeline(
            body, grid=(2 * n_tok // window, sub),
            in_specs=[pl.BlockSpec((rows, width), lambda i, j: ((i * sub + j) % (n_tok // rows), 0)),
                      pl.BlockSpec((1, window), lambda i, j: (0, i))],
            out_specs=[], core_axis_name=("core", "subcore"),
            dimension_semantics=(pltpu.PARALLEL, pltpu.ARBITRARY), _explicit_indices=True,
        )(h_hbm, i_hbm)

    return scatter(h2, idx)


def _scatter_rows_kernel(dest_ref, h_ref, xs_in_ref, xs_ref, sem):
    del xs_in_ref
    n_tok = h_ref.shape[0]

    def issue(i, carry):
        for k in range(2):
            d = dest_ref[k * n_tok + i]
            pltpu.make_async_copy(h_ref.at[pl.ds(i, 1)], xs_ref.at[pl.ds(d, 1)], sem.at[0]).start(priority=k)
        return carry

    lax.fori_loop(0, n_tok, issue, 0, unroll=8)
    for _ in range(2):
        pltpu.make_async_copy(h_ref, xs_ref.at[pl.ds(0, n_tok)], sem.at[0]).wait()


def _scatter_rows(dest, h2, xs):
    return pl.pallas_call(
        _scatter_rows_kernel,
        grid_spec=pltpu.PrefetchScalarGridSpec(
            num_scalar_prefetch=1, grid=(1,),
            in_specs=[pl.BlockSpec(h2.shape, lambda i, d: (0, 0)), pl.BlockSpec(memory_space=pl.ANY)],
            out_specs=pl.BlockSpec(memory_space=pl.ANY),
            scratch_shapes=[pltpu.SemaphoreType.DMA((1,))]),
        out_shape=jax.ShapeDtypeStruct(xs.shape, xs.dtype),
        input_output_aliases={2: 0},
        compiler_params=pltpu.CompilerParams(dimension_semantics=("arbitrary",)),
        name="scatter_rows",
    )(dest, h2, xs)


def _experts_kernel(be_ref, nu_ref, nv_ref, nxt_ref, par_ref, phys_ref, xs_ref, wg_ref, wu_ref, wd_ref, y_ref,
                    wg_s, wu_s, wd_s, wg_f, wu_f, wd_f, sem):
    del phys_ref
    j = pl.program_id(0)
    used = j < nu_ref[0]
    changed = jnp.logical_or(j == 0, be_ref[j] != be_ref[jnp.maximum(j - 1, 0)])

    def weight_copies(e, slot):
        return [pltpu.make_async_copy(src.at[e], dst.at[slot], sem.at[slot])
                for src, dst in ((wg_ref, wg_f), (wu_ref, wu_f), (wd_ref, wd_f))]

    @pl.when(jnp.logical_and(used, changed))
    def _():
        slot = par_ref[j]

        @pl.when(j == 0)
        def _():
            for c in weight_copies(be_ref[0], slot):
                c.start()

        for c in weight_copies(be_ref[j], slot):
            c.wait()

        @pl.when(nxt_ref[j] >= 0)
        def _():
            for c in weight_copies(nxt_ref[j], 1 - slot):
                c.start()

        wg_s[...] = wg_f[slot].astype(jnp.bfloat16)
        wu_s[...] = wu_f[slot].astype(jnp.bfloat16)
        wd_s[...] = wd_f[slot].astype(jnp.bfloat16)

    @pl.when(used)
    def _():
        hb = xs_ref.shape[0] // 2
        row = lax.broadcasted_iota(jnp.int32, (hb, 1), 0)

        def load(h):
            live = row + h * hb < nv_ref[j]
            return _unpack_bf16_pairs(jnp.where(live, xs_ref[h * hb:(h + 1) * hb, :], jnp.uint32(0)))

        def swiglu(g, u):
            return (g * jax.nn.sigmoid(g) * u).astype(jnp.bfloat16)

        x0 = load(0)
        g0 = jnp.dot(x0, wg_s[...], preferred_element_type=jnp.float32)
        u0 = jnp.dot(x0, wu_s[...], preferred_element_type=jnp.float32)
        x1 = load(1)
        g1 = jnp.dot(x1, wg_s[...], preferred_element_type=jnp.float32)
        a0 = swiglu(g0, u0)
        u1 = jnp.dot(x1, wu_s[...], preferred_element_type=jnp.float32)
        y0 = jnp.dot(a0, wd_s[...], preferred_element_type=jnp.float32)
        a1 = swiglu(g1, u1)
        y_ref[0:hb, :] = _pack_bf16_pairs(y0.astype(jnp.bfloat16))
        y1 = jnp.dot(a1, wd_s[...], preferred_element_type=jnp.float32)
        y_ref[hb:, :] = _pack_bf16_pairs(y1.astype(jnp.bfloat16))


def _experts(block_e, n_used, n_valid, next_e, parity, phys, xs, w_gate, w_up, w_down):
    bm = EXPERT_BLOCK
    n_blocks = xs.shape[0] // bm
    hbm = pl.BlockSpec(memory_space=pl.ANY)
    return pl.pallas_call(
        _experts_kernel,
        grid_spec=pltpu.PrefetchScalarGridSpec(
            num_scalar_prefetch=6, grid=(n_blocks,),
            in_specs=[pl.BlockSpec((bm, D_MODEL // 2), lambda j, be, nu, nv, nx, pa, ph: (ph[j], 0)),
                      hbm, hbm, hbm],
            out_specs=pl.BlockSpec((bm, D_MODEL // 2), lambda j, be, nu, nv, nx, pa, ph: (ph[j], 0)),
            scratch_shapes=[pltpu.VMEM((D_MODEL, EXPERT_HIDDEN), jnp.bfloat16),
                            pltpu.VMEM((D_MODEL, EXPERT_HIDDEN), jnp.bfloat16),
                            pltpu.VMEM((EXPERT_HIDDEN, D_MODEL), jnp.bfloat16),
                            pltpu.VMEM((2, D_MODEL, EXPERT_HIDDEN), jnp.float32),
                            pltpu.VMEM((2, D_MODEL, EXPERT_HIDDEN), jnp.float32),
                            pltpu.VMEM((2, EXPERT_HIDDEN, D_MODEL), jnp.float32),
                            pltpu.SemaphoreType.DMA((2,))]),
        out_shape=jax.ShapeDtypeStruct((xs.shape[0], D_MODEL // 2), jnp.uint32),
        compiler_params=pltpu.CompilerParams(dimension_semantics=("arbitrary",), vmem_limit_bytes=VMEM_LIMIT),
        name="experts",
    )(block_e, n_used, n_valid, next_e, parity, phys, xs, w_gate, w_up, w_down)


def _gather_rows(idx, yb):
    mesh = plsc.VectorSubcoreMesh(core_axis_name="core", subcore_axis_name="subcore")
    rows, window = SC_ROWS, SC_INDEX_WINDOW
    sub = window // rows
    n = idx.shape[1]
    width = yb.shape[1]
    assert n % window == 0

    @pl.kernel(out_type=jax.ShapeDtypeStruct((n, width), yb.dtype), mesh=mesh, scratch_types=[])
    def gather(y_hbm, i_hbm, o_hbm):
        def body(indices, i_vmem, o_vmem):
            pltpu.sync_copy(y_hbm.at[i_vmem.at[0, pl.ds(indices[1] * rows, rows)]], o_vmem)

        pltpu.emit_pipeline(
            body, grid=(n // window, sub),
            in_specs=[pl.BlockSpec((1, window), lambda i, j: (0, i))],
            out_specs=[pl.BlockSpec((rows, width), lambda i, j: (i * sub + j, 0))],
            core_axis_name=("core", "subcore"),
            dimension_semantics=(pltpu.PARALLEL, pltpu.ARBITRARY), _explicit_indices=True,
        )(i_hbm, o_hbm)

    return gather(yb, idx)


def _combine_kernel(x1_ref, gate_ref, rf_ref, fin_ref, y0_ref, y1_ref, *out_refs, per_token_gate):
    o_ref = out_refs[-1]
    gates = jnp.transpose(rf_ref[...])
    moe = _unpack_pairs_f32(y0_ref[...]) * gates[:, 0:1] + _unpack_pairs_f32(y1_ref[...]) * gates[:, 1:2]
    gate_f = gate_ref[...] if per_token_gate else gate_ref[...][5:6]
    o_ref[...] = _rms(x1_ref[...] + gate_f * moe, fin_ref[...])


def _combine(x1, gate, rf, final_norm, ys, out_prev, first_tile, n_tiles, *, per_token_gate):
    tc = COMBINE_TILE
    gate_spec = (pl.BlockSpec((tc, D_MODEL), lambda i: (first_tile + i, 0)) if per_token_gate
                 else pl.BlockSpec((6, D_MODEL), lambda i: (0, 0)))
    args = [x1, gate, rf, final_norm, ys, ys]
    in_specs = [pl.BlockSpec((tc, D_MODEL), lambda i: (first_tile + i, 0)),
                gate_spec,
                pl.BlockSpec((8, tc), lambda i: (0, first_tile + i)),
                pl.BlockSpec((1, D_MODEL), lambda i: (0, 0)),
                pl.BlockSpec((tc, D_MODEL // 2), lambda i: (i, 0)),
                pl.BlockSpec((tc, D_MODEL // 2), lambda i: (n_tiles + i, 0))]
    aliases = {}
    if out_prev is not None:
        args.append(out_prev)
        in_specs.append(pl.BlockSpec(memory_space=pl.ANY))
        aliases = {len(args) - 1: 0}
    return pl.pallas_call(
        functools.partial(_combine_kernel, per_token_gate=per_token_gate),
        grid=(n_tiles,), in_specs=in_specs,
        out_specs=pl.BlockSpec((tc, D_MODEL), lambda i: (first_tile + i, 0)),
        out_shape=jax.ShapeDtypeStruct(x1.shape, jnp.float32),
        input_output_aliases=aliases,
        compiler_params=pltpu.CompilerParams(dimension_semantics=("arbitrary",), vmem_limit_bytes=VMEM_LIMIT),
        name="combine",
    )(*args)


def _rope_tables(pos):
    half = HEAD_DIM // 2
    inv_freq = np.power(np.float32(ROPE_THETA), -(np.arange(half, dtype=np.float32) / np.float32(half)))
    ang = (pos.astype(np.float32)[:, None] * inv_freq[None, :].astype(np.float32)).astype(np.float64)
    cos, sin = np.cos(ang).astype(np.float32), np.sin(ang).astype(np.float32)
    cos = np.concatenate([cos, cos, cos, cos], axis=1)
    sin = np.concatenate([-sin, sin, -sin, sin], axis=1)
    return cos, sin


def _prompt_bias():
    r = np.arange(WINDOW)[:, None]
    j = np.arange(2 * WINDOW)[None, :]
    band = (j > r) & (j <= r + WINDOW)
    later = np.where(band, 0.0, NEG).astype(np.float32)
    first = np.where(band & (j >= WINDOW), 0.0, NEG).astype(np.float32)
    return jnp.asarray(np.stack([first, later]))


def _sample_bias(dec):
    cb = SAMPLE_CHUNK
    qb = np.repeat(np.arange(cb), dec)[:, None]
    qi = np.tile(np.arange(dec), cb)[:, None]
    kb = np.concatenate([np.repeat(np.arange(cb), WINDOW), np.repeat(np.arange(cb), dec)])[None, :]
    kj = np.concatenate([np.tile(np.arange(WINDOW), cb), np.tile(np.arange(dec), cb)])[None, :]
    is_new = (np.arange(cb * WINDOW + cb * dec) >= cb * WINDOW)[None, :]
    ok = (kb == qb) & np.where(is_new, kj <= qi, kj > qi)
    ok = np.pad(ok, ((0, 0), (0, SAMPLE_KEYS - ok.shape[1])))
    return jnp.asarray(np.where(ok, 0.0, NEG).astype(np.float32))


def kernel(x_prompt, x_sample, cache_k, cache_v, state_conv, c_prompt, c_sample, attn_norm, ffn_norm, w_mod, b_mod, w_in, conv_w, attn_sinks, out_norm_attn, out_norm_conv, w_out, w_group, b_group, w_expert, b_expert, w_gate, w_up, w_down, final_norm):
    assert attn_norm.shape[0] == 1, "one layer"
    bsz, seq, _ = x_prompt.shape
    nseq, dec, _ = x_sample.shape
    m_prompt = bsz * seq
    n_s = nseq * dec
    m_total = m_prompt + n_s
    tm = MIXER_TILE
    assert seq % tm == 0 and n_s % SAMPLE_TILE == 0 and SAMPLE_TILE % (SAMPLE_CHUNK * dec) == 0
    assert cache_k.shape[2] == WINDOW and m_prompt % SAMPLE_TILE == 0 and m_total % COMBINE_TILE == 0
    bf = jnp.bfloat16

    w_r = jnp.zeros((ROUTER_ROWS, D_MODEL), jnp.float32)
    w_r = w_r.at[0:N_GROUPS].set(w_group[0].T).at[8:8 + N_EXPERTS].set(w_expert[0].T).astype(bf)
    b_r = jnp.zeros((ROUTER_ROWS,), jnp.float32).at[0:N_GROUPS].set(b_group[0]).at[8:8 + N_EXPERTS].set(b_expert[0])
    b_r = jnp.broadcast_to(b_r[:, None], (ROUTER_ROWS, LANES))
    cos_p, sin_p = _rope_tables(np.arange(seq))
    cos_s, sin_s = _rope_tables(PAST_LEN + np.arange(dec))
    tri = jnp.asarray(np.triu(np.ones((tm, tm), np.float32), 1)).astype(bf)
    consts = dict(
        anorm=attn_norm[0][None], fnorm=ffn_norm[0][None], w_in=w_in[0].astype(bf),
        cos_p=jnp.asarray(cos_p), sin_p=jnp.asarray(sin_p),
        cos_s=jnp.asarray(np.tile(cos_s, (nseq, 1))), sin_s=jnp.asarray(np.tile(sin_s, (nseq, 1))),
        bias_p=_prompt_bias(), bias_s=_sample_bias(dec), conv_w=conv_w[0],
        gn_attn=out_norm_attn[0][None], gn_conv=out_norm_conv[0][None], w_out=w_out[0].astype(bf),
        w_r=w_r, b_r=b_r, tri=tri)
    sinks = attn_sinks[0]

    mod = _adaln(jnp.concatenate([c_prompt, c_sample], axis=0), w_mod[0], b_mod[0])
    mod_p = mod[:bsz].reshape(bsz, 6, D_MODEL)
    mod_s = mod[bsz:]

    bm = EXPERT_BLOCK
    eids = jnp.arange(N_EXPERTS, dtype=jnp.int32)
    lookup = lambda table, e: jnp.sum(jnp.where(e[..., None] == eids, table, 0), axis=-1)
    by_token = lambda a: jnp.transpose(a, (1, 0, 2)).reshape(8, -1)
    ceil_bm = lambda c: (c + bm - 1) // bm * bm
    x1_p, h2_p, ri_p, rf_p, cnt_p, k_p, v_p, u_p = _prompt_mixer(x_prompt, mod_p, sinks, consts)
    cnt_p = cnt_p[:, 0].astype(jnp.int32)
    padded_p = ceil_bm(cnt_p)
    pstart_p = jnp.cumsum(padded_p) - padded_p
    nb_p = -(-(2 * m_prompt + N_EXPERTS * (bm - 1)) // bm)
    nb_o = -(-(2 * n_s + N_EXPERTS * (bm - 1)) // bm)
    n_blocks = nb_p + nb_o
    ri_p = by_token(ri_p)
    dest_p = (lookup(pstart_p, ri_p[0:2]) + ri_p[2:4]).reshape(2 * m_prompt)

    st = state_conv[0]
    zrow = jnp.zeros((nseq, 1, CONV_DIM), jnp.float32)
    e1 = jnp.concatenate([st[:, 1:2]] + [zrow] * (dec - 1), axis=1).reshape(n_s, CONV_DIM)
    e2 = jnp.concatenate([st[:, 0:1], st[:, 1:2]] + [zrow] * (dec - 2), axis=1).reshape(n_s, CONV_DIM)
    dest_p, e1 = lax.optimization_barrier((dest_p, e1))
    xs = _dispatch(dest_p[None], h2_p, n_blocks * bm)
    ck = jnp.transpose(cache_k[0], (0, 2, 3, 1)).reshape(nseq * KV_DIM, WINDOW)
    cv = jnp.transpose(cache_v[0], (0, 2, 3, 1)).reshape(nseq * KV_DIM, WINDOW)
    x1_s, h2_s, ri_s, rf_s, cnt_s, k_s, v_s, u_s = _sample_mixer(
        x_sample.reshape(n_s, D_MODEL), mod_s, sinks, consts, ck, cv, e1, e2,
        jnp.zeros((N_EXPERTS, LANES), jnp.float32), dec)

    cnt_s = cnt_s[:, 0].astype(jnp.int32)
    fit = jnp.minimum(cnt_s, padded_p - cnt_p)
    over = cnt_s - fit
    padded_o = ceil_bm(over)
    ostart = nb_p * bm + jnp.cumsum(padded_o) - padded_o
    ri_s = by_token(ri_s)
    e_s, r_s = ri_s[0:2], ri_s[2:4]
    dest_s = jnp.where(r_s < lookup(fit, e_s), lookup(pstart_p + cnt_p, e_s) + r_s, lookup(ostart - fit, e_s) + r_s)
    dest_s = dest_s.reshape(2 * n_s)

    from_t = lambda a: jnp.transpose(a.reshape(nseq, N_KV_HEADS, HEAD_DIM, WINDOW), (0, 3, 1, 2))[None]
    new_k_s, new_v_s = from_t(k_s), from_t(v_s)
    xs = _scatter_rows(dest_s, h2_s, xs)

    np_e, no_e = padded_p // bm, padded_o // bm
    lend = jnp.cumsum(np_e + no_e)
    lstart = lend - np_e - no_e
    n_used = lend[-1].astype(jnp.int32)
    blk = jnp.arange(n_blocks, dtype=jnp.int32)
    present = np_e + no_e > 0
    last_e = jnp.max(jnp.where(present, eids, 0))
    block_e = jnp.sum((lend[None, :] <= blk[:, None]).astype(jnp.int32), axis=1)
    block_e = jnp.where(blk < n_used, jnp.minimum(block_e, N_EXPERTS - 1), last_e)
    of_block = lambda table: lookup(table, block_e)
    within = blk - of_block(lstart)
    in_prompt = within < of_block(np_e)
    phys = jnp.where(in_prompt, of_block(pstart_p) // bm + within, of_block(ostart) // bm + within - of_block(np_e))
    n_valid = jnp.where(in_prompt, of_block(cnt_p + fit) - within * bm, of_block(over) - (within - of_block(np_e)) * bm)
    n_valid = jnp.where(blk < n_used, jnp.clip(n_valid, 0, bm), 0).astype(jnp.int32)
    phys = jnp.where(blk < n_used, phys, jnp.sum(jnp.where(blk == n_used - 1, phys, 0))).astype(jnp.int32)
    later = jnp.logical_and(present[None, :], eids[None, :] > eids[:, None])
    next_of = jnp.min(jnp.where(later, eids[None, :], N_EXPERTS), axis=1)
    next_of = jnp.where(next_of == N_EXPERTS, -1, next_of)
    next_e = of_block(next_of).astype(jnp.int32)
    parity = of_block((jnp.cumsum(present.astype(jnp.int32)) - 1) % 2).astype(jnp.int32)
    yb = _experts(block_e, n_used.reshape(1), n_valid, next_e, parity, phys, xs, w_gate[0], w_up[0], w_down[0])

    fin = final_norm[None]
    tc = COMBINE_TILE
    rf_p, rf_s = by_token(rf_p), by_token(rf_s)
    y_p = None
    chunk = min(COMBINE_CHUNK, seq)
    assert seq % chunk == 0 and chunk % tc == 0
    for c in range(m_prompt // chunk):
        lo, hi = c * chunk, (c + 1) * chunk
        idx = jnp.concatenate([dest_p[lo:hi], dest_p[m_prompt + lo:m_prompt + hi]])[None]
        y_p = _combine(x1_p, mod_p[lo // seq], rf_p, fin, _gather_rows(idx, yb), y_p, lo // tc, chunk // tc,
                       per_token_gate=False)
    gate_s = jnp.repeat(mod_s[:, 5 * D_MODEL:], dec, axis=0)
    y_s = _combine(x1_s, gate_s, rf_s, fin, _gather_rows(dest_s[None], yb), None, 0, n_s // tc, per_token_gate=True)

    keep = WINDOW
    new_k_p = k_p.reshape(1, bsz, keep, N_KV_HEADS, HEAD_DIM)
    new_v_p = v_p.reshape(1, bsz, keep, N_KV_HEADS, HEAD_DIM)
    new_u_p = u_p[:, 6:8][None]
    u_all = jnp.concatenate([st, u_s.reshape(nseq, dec, CONV_DIM)], axis=1)
    new_u_s = u_all[:, -2:][None]
    return (y_p.reshape(bsz, seq, D_MODEL), y_s.reshape(nseq, dec, D_MODEL),
            new_k_p, new_v_p, new_u_p, new_k_s, new_v_s, new_u_s)
```

```python
import functools

import jax
import jax.numpy as jnp
import numpy as np
from jax import lax
from jax.experimental import pallas as pl
from jax.experimental.pallas import tpu as pltpu
from jax.experimental.pallas import tpu_sc as plsc

D_MODEL = 1024
HEAD_DIM = 64
ATTN_DIM = 512
N_Q_HEADS = 8
N_KV_HEADS = 2
KV_DIM = 128
CONV_DIM = 512
WINDOW = 128
IN_PROJ_DIM = ATTN_DIM + 2 * KV_DIM + 3 * CONV_DIM
N_GROUPS = 4
EXPERTS_PER_GROUP = 8
N_EXPERTS = 32
EXPERT_HIDDEN = 512
RMS_EPS = 1e-5
ROPE_THETA = 10000.0
PAST_LEN = 16384

LANES = 128
ROUTER_ROWS = 48
NEG = -1e30
LOG2_E = 1.4426950408889634
Q_SCALE = HEAD_DIM ** -0.5 * LOG2_E

MIXER_TILE = 1024
SAMPLE_TILE = 128
SAMPLE_CHUNK = 8
SAMPLE_KEYS = (SAMPLE_CHUNK + 1) * WINDOW
EXPERT_BLOCK = 512
COMBINE_TILE = 512
PROMPT_COMBINE_TILE = 1024
COMBINE_CHUNK = 4096
SC_ROWS = 32
SC_INDEX_WINDOW = 128
VMEM_LIMIT = 56 * 1024 * 1024


def _rms(x, gain):
    ms = jnp.mean(x * x, axis=-1, keepdims=True)
    return x * lax.rsqrt(ms + RMS_EPS) * gain


def _half_lane_mask(rows=1):
    return lax.broadcasted_iota(jnp.int32, (rows, LANES), 1) < HEAD_DIM


def _rope(x, cos, sin_signed):
    t, w = x.shape
    reps = w // LANES
    lane = lax.broadcasted_iota(jnp.int32, (t, w), 1)
    upper = (lane % HEAD_DIM) >= (HEAD_DIM // 2)
    partner = jnp.where(upper, pltpu.roll(x, HEAD_DIM // 2, 1), pltpu.roll(x, w - HEAD_DIM // 2, 1))
    if reps > 1:
        cos = jnp.concatenate([cos] * reps, axis=1)
        sin_signed = jnp.concatenate([sin_signed] * reps, axis=1)
    return x * cos + partner * sin_signed


def _swap_halves(x):
    return pltpu.roll(x, HEAD_DIM, 1)


def _group_norm64(x, gain):
    t, w = x.shape
    lo = _half_lane_mask(t)
    outs = []
    for c in range(w // LANES):
        xc = x[:, c * LANES:(c + 1) * LANES]
        sq = xc * xc
        s_lo = jnp.sum(jnp.where(lo, sq, 0.0), axis=-1, keepdims=True)
        s_hi = jnp.sum(sq, axis=-1, keepdims=True) - s_lo
        r = jnp.where(lo, lax.rsqrt(s_lo * (1.0 / HEAD_DIM) + RMS_EPS), lax.rsqrt(s_hi * (1.0 / HEAD_DIM) + RMS_EPS))
        outs.append(xc * r)
    return jnp.concatenate(outs, axis=1) * gain


def _attend(q_rows, keys, keys_sw, vals, vals_sw, bias, sinks_ref, group, transposed=False):
    r = q_rows.shape[0]
    lo = _half_lane_mask(r)
    outs = []
    for parity in range(2):
        heads = [4 * group + parity, 4 * group + parity + 2]
        kk, vv = (keys, vals) if parity == group else (keys_sw, vals_sw)
        half = lo if parity == 0 else jnp.logical_not(lo)
        qs = []
        for h in heads:
            pair = q_rows[:, (h // 2) * LANES:(h // 2 + 1) * LANES]
            qs.append(jnp.where(half, pair, jnp.zeros_like(pair)))
        qcat = jnp.concatenate(qs, axis=0)
        s = (jnp.dot(qcat, kk, preferred_element_type=jnp.float32) if transposed else
             lax.dot_general(qcat, kk, (((1,), (1,)), ((), ())), preferred_element_type=jnp.float32))
        ps, dens = [], []
        for i, h in enumerate(heads):
            sh = s[i * r:(i + 1) * r] + bias
            sink = sinks_ref[h] * LOG2_E
            m = jnp.maximum(jnp.max(sh, axis=-1, keepdims=True), sink)
            p = jnp.exp2(sh - m)
            dens.append(jnp.sum(p, axis=-1, keepdims=True) + jnp.exp2(sink - m))
            ps.append(p.astype(jnp.bfloat16))
        pcat = jnp.concatenate(ps, axis=0)
        o = (lax.dot_general(pcat, vv, (((1,), (1,)), ((), ())), preferred_element_type=jnp.float32) if transposed
             else jnp.dot(pcat, vv, preferred_element_type=jnp.float32))
        for i, h in enumerate(heads):
            oh = o[i * r:(i + 1) * r] / dens[i]
            ms = jnp.sum(jnp.where(half, oh * oh, 0.0), axis=-1, keepdims=True) * (1.0 / HEAD_DIM)
            outs.append((h, oh * lax.rsqrt(ms + RMS_EPS)))
    outs.sort(key=lambda t: t[0])
    return [o for _, o in outs]


def _merge_heads(head_outs, lo):
    pairs = [jnp.where(lo, head_outs[2 * j], head_outs[2 * j + 1]) for j in range(4)]
    return jnp.concatenate(pairs, axis=1)


def _route(h2b, wr_ref, br_ref, tri_ref, cnt_ref, ri_ref, rf_ref):
    t = h2b.shape[0]
    lg = lax.dot_general(wr_ref[...], h2b, (((1,), (1,)), ((), ())), preferred_element_type=jnp.float32)
    lg = lg + br_ref[...][:, 0:1]
    r8 = lax.broadcasted_iota(jnp.int32, (8, t), 0)
    lgrp = jnp.where(r8 < N_GROUPS, lg[0:8], -jnp.inf)
    gmax = jnp.max(lgrp, axis=0, keepdims=True)
    grp = jnp.min(jnp.where(lgrp == gmax, r8, 8), axis=0, keepdims=True)
    pg_sel = 1.0 / jnp.sum(jnp.exp(lgrp - gmax), axis=0, keepdims=True)
    r32 = lax.broadcasted_iota(jnp.int32, (N_EXPERTS, t), 0)
    le = jnp.where((r32 // EXPERTS_PER_GROUP) == grp, lg[8:8 + N_EXPERTS], -jnp.inf)
    v1 = jnp.max(le, axis=0, keepdims=True)
    i1 = jnp.min(jnp.where(le == v1, r32, N_EXPERTS), axis=0, keepdims=True)
    le2 = jnp.where(r32 == i1, -jnp.inf, le)
    v2 = jnp.max(le2, axis=0, keepdims=True)
    i2 = jnp.min(jnp.where(le2 == v2, r32, N_EXPERTS), axis=0, keepdims=True)
    e21 = jnp.exp(v2 - v1)
    gate1 = pg_sel / (1.0 + e21)
    gate2 = pg_sel * e21 / (1.0 + e21)
    hot1 = r32 == i1
    hot2 = r32 == i2
    onehot = jnp.where(jnp.logical_or(hot1, hot2), 1.0, 0.0)
    before = jnp.dot(onehot.astype(jnp.bfloat16), tri_ref[...], preferred_element_type=jnp.float32)
    pos = before + cnt_ref[...][:, 0:1]
    rank1 = jnp.sum(jnp.where(hot1, pos, 0.0), axis=0, keepdims=True)
    rank2 = jnp.sum(jnp.where(hot2, pos, 0.0), axis=0, keepdims=True)
    cnt_ref[...] = cnt_ref[...] + jnp.sum(onehot, axis=1, keepdims=True)
    zi = jnp.zeros((4, t), jnp.int32)
    ri_ref[...] = jnp.concatenate([i1, i2, rank1.astype(jnp.int32), rank2.astype(jnp.int32), zi], axis=0)
    rf_ref[...] = jnp.concatenate([gate1, gate2, jnp.zeros((6, t), jnp.float32)], axis=0)


def _pack_bf16_pairs(xb):
    w = xb.shape[1] // 2
    bits = pltpu.bitcast(xb.astype(jnp.float32), jnp.uint32)
    return (bits[:, :w] >> 16) | (bits[:, w:] & jnp.uint32(0xFFFF0000))


def _unpack_pairs_f32(words):
    lo = pltpu.bitcast(words << 16, jnp.float32)
    hi = pltpu.bitcast(words & jnp.uint32(0xFFFF0000), jnp.float32)
    return jnp.concatenate([lo, hi], axis=1)


def _unpack_bf16_pairs(words):
    return _unpack_pairs_f32(words).astype(jnp.bfloat16)


def _split_proj(proj):
    a = ATTN_DIM
    q = proj[:, :a]
    k = proj[:, a:a + KV_DIM]
    v = proj[:, a + KV_DIM:a + 2 * KV_DIM]
    c0 = a + 2 * KV_DIM
    h_conv = proj[:, c0:c0 + CONV_DIM]
    gate_b = proj[:, c0 + CONV_DIM:c0 + 2 * CONV_DIM]
    gate_c = proj[:, c0 + 2 * CONV_DIM:c0 + 3 * CONV_DIM]
    return q, k, v, gate_c * h_conv, gate_b


def _mixer_tail(x, attn_n, conv_n, mod, wout_ref, fnorm_ref, wr_ref, br_ref, tri_ref, cnt_ref,
                x1_ref, h2_ref, ri_ref, rf_ref):
    gate_a, shift_f, scale_f = mod
    cat = jnp.concatenate([attn_n.astype(jnp.bfloat16), conv_n.astype(jnp.bfloat16)], axis=1)
    mix = jnp.dot(cat, wout_ref[...], preferred_element_type=jnp.float32)
    x1 = x + gate_a * mix
    x1_ref[...] = x1
    h2 = _rms(x1, fnorm_ref[...] * (1.0 + scale_f)) + shift_f
    h2b = h2.astype(jnp.bfloat16)
    h2_ref[...] = _pack_bf16_pairs(h2b)
    _route(h2b, wr_ref, br_ref, tri_ref, cnt_ref, ri_ref, rf_ref)


def _adaln_kernel(c_ref, w_ref, b_ref, o_ref):
    c = c_ref[...]
    a = (c * jax.nn.sigmoid(c)).astype(jnp.bfloat16)
    o_ref[...] = jnp.dot(a, w_ref[...].astype(jnp.bfloat16), preferred_element_type=jnp.float32) + b_ref[...]


def _adaln(c, w_mod, b_mod):
    n = c.shape[0]
    tn = 1536
    return pl.pallas_call(
        _adaln_kernel,
        grid=(w_mod.shape[1] // tn,),
        in_specs=[pl.BlockSpec((n, D_MODEL), lambda j: (0, 0)),
                  pl.BlockSpec((D_MODEL, tn), lambda j: (0, j)),
                  pl.BlockSpec((1, tn), lambda j: (0, j))],
        out_specs=pl.BlockSpec((n, tn), lambda j: (0, j)),
        out_shape=jax.ShapeDtypeStruct((n, w_mod.shape[1]), jnp.float32),
        compiler_params=pltpu.CompilerParams(dimension_semantics=("arbitrary",), vmem_limit_bytes=VMEM_LIMIT),
        name="adaln",
    )(c, w_mod, b_mod.reshape(1, -1))


def _prompt_mixer_kernel(sinks_ref, x_ref, mod_ref, anorm_ref, fnorm_ref, win_ref, cos_ref, sin_ref, bias_ref,
                         convw_ref, gna_ref, gnc_ref, wout_ref, wr_ref, br_ref, tri_ref,
                         x1_ref, h2_ref, ri_ref, rf_ref, cnt_out_ref, knew_ref, vnew_ref, unew_ref,
                         q_s, k_s, ksw_s, v_s, vsw_s, u_s, att_s, cnt_s):
    b = pl.program_id(0)
    t = pl.program_id(1)
    tm = MIXER_TILE
    nblk = tm // WINDOW

    @pl.when(jnp.logical_and(b == 0, t == 0))
    def _():
        cnt_s[...] = jnp.zeros_like(cnt_s)

    @pl.when(t == 0)
    def _():
        z = jnp.zeros((WINDOW, KV_DIM), jnp.bfloat16)
        k_s[0:WINDOW, :] = z
        ksw_s[0:WINDOW, :] = z
        v_s[0:WINDOW, :] = z
        vsw_s[0:WINDOW, :] = z
        u_s[0:8, :] = jnp.zeros((8, CONV_DIM), jnp.float32)

    x = x_ref[...]
    mod = mod_ref[...]
    h = _rms(x, anorm_ref[...] * (1.0 + mod[1:2])) + mod[0:1]
    proj = jnp.dot(h.astype(jnp.bfloat16), win_ref[...], preferred_element_type=jnp.float32)
    q, k, v, u, gate_b = _split_proj(proj)
    cos = cos_ref[...]
    sin = sin_ref[...]
    q = _rope(q, cos, sin) * Q_SCALE
    k = _rope(k, cos, sin)
    q_s[...] = q.astype(jnp.bfloat16)
    k_s[WINDOW:, :] = k.astype(jnp.bfloat16)
    ksw_s[WINDOW:, :] = _swap_halves(k).astype(jnp.bfloat16)
    v_s[WINDOW:, :] = v.astype(jnp.bfloat16)
    vsw_s[WINDOW:, :] = _swap_halves(v).astype(jnp.bfloat16)
    knew_ref[...] = k[tm - WINDOW:, :]
    vnew_ref[...] = v[tm - WINDOW:, :]

    u_s[8:, :] = u
    unew_ref[...] = u[tm - 8:, :]
    cw = convw_ref[...]
    conv = (u_s[6:6 + tm, :] * cw[0:1] + u_s[7:7 + tm, :] * cw[1:2] + u * cw[2:3]) * gate_b
    u_s[0:8, :] = u[tm - 8:, :]
    conv_n = _group_norm64(conv, gnc_ref[...])

    lo = _half_lane_mask(WINDOW)
    first = t == 0

    def block(i, carry):
        r0 = pl.multiple_of(i * WINDOW, WINDOW)
        bias = bias_ref[jnp.where(jnp.logical_and(first, i == 0), 0, 1)]
        qb = q_s[pl.ds(r0, WINDOW), :]
        keys = k_s[pl.ds(r0, 2 * WINDOW), :]
        keys_sw = ksw_s[pl.ds(r0, 2 * WINDOW), :]
        vals = v_s[pl.ds(r0, 2 * WINDOW), :]
        vals_sw = vsw_s[pl.ds(r0, 2 * WINDOW), :]
        outs = []
        for g in range(N_KV_HEADS):
            outs += _attend(qb, keys, keys_sw, vals, vals_sw, bias, sinks_ref, g)
        att_s[pl.ds(r0, WINDOW), :] = _merge_heads(outs, lo)
        return carry

    lax.fori_loop(0, nblk, block, 0, unroll=True)
    k_s[0:WINDOW, :] = k_s[tm:tm + WINDOW, :]
    ksw_s[0:WINDOW, :] = ksw_s[tm:tm + WINDOW, :]
    v_s[0:WINDOW, :] = v_s[tm:tm + WINDOW, :]
    vsw_s[0:WINDOW, :] = vsw_s[tm:tm + WINDOW, :]

    attn_n = att_s[...] * gna_ref[...]
    _mixer_tail(x, attn_n, conv_n, (mod[2:3], mod[3:4], mod[4:5]), wout_ref, fnorm_ref, wr_ref, br_ref, tri_ref,
                cnt_s, x1_ref, h2_ref, ri_ref, rf_ref)
    cnt_out_ref[...] = cnt_s[...]


def _prompt_mixer(x, mod, sinks, consts):
    bsz, seq, _ = x.shape
    tm = MIXER_TILE
    nt = seq // tm
    full = lambda shape: pl.BlockSpec(shape, lambda b, t, s: (0,) * len(shape))
    in_specs = [
        pl.BlockSpec((None, tm, D_MODEL), lambda b, t, s: (b, t, 0)),
        pl.BlockSpec((None, 6, D_MODEL), lambda b, t, s: (b, 0, 0)),
        full((1, D_MODEL)), full((1, D_MODEL)),
        full((D_MODEL, IN_PROJ_DIM)),
        pl.BlockSpec((tm, LANES), lambda b, t, s: (t, 0)),
        pl.BlockSpec((tm, LANES), lambda b, t, s: (t, 0)),
        full((2, WINDOW, 2 * WINDOW)),
        full((3, CONV_DIM)), full((1, ATTN_DIM)), full((1, CONV_DIM)),
        full((D_MODEL, D_MODEL)),
        full((ROUTER_ROWS, D_MODEL)), full((ROUTER_ROWS, LANES)),
        full((tm, tm)),
    ]
    out_shape = [
        jax.ShapeDtypeStruct((bsz * seq, D_MODEL), jnp.float32),
        jax.ShapeDtypeStruct((bsz * seq, D_MODEL // 2), jnp.uint32),
        jax.ShapeDtypeStruct((bsz * nt, 8, tm), jnp.int32),
        jax.ShapeDtypeStruct((bsz * nt, 8, tm), jnp.float32),
        jax.ShapeDtypeStruct((N_EXPERTS, LANES), jnp.float32),
        jax.ShapeDtypeStruct((bsz, WINDOW, KV_DIM), jnp.float32),
        jax.ShapeDtypeStruct((bsz, WINDOW, KV_DIM), jnp.float32),
        jax.ShapeDtypeStruct((bsz, 8, CONV_DIM), jnp.float32),
    ]
    out_specs = [
        pl.BlockSpec((tm, D_MODEL), lambda b, t, s: (b * nt + t, 0)),
        pl.BlockSpec((tm, D_MODEL // 2), lambda b, t, s: (b * nt + t, 0)),
        pl.BlockSpec((None, 8, tm), lambda b, t, s: (b * nt + t, 0, 0)),
        pl.BlockSpec((None, 8, tm), lambda b, t, s: (b * nt + t, 0, 0)),
        pl.BlockSpec((N_EXPERTS, LANES), lambda b, t, s: (0, 0)),
        pl.BlockSpec((None, WINDOW, KV_DIM), lambda b, t, s: (b, 0, 0)),
        pl.BlockSpec((None, WINDOW, KV_DIM), lambda b, t, s: (b, 0, 0)),
        pl.BlockSpec((None, 8, CONV_DIM), lambda b, t, s: (b, 0, 0)),
    ]
    scratch = [
        pltpu.VMEM((tm, ATTN_DIM), jnp.bfloat16),
        pltpu.VMEM((tm + WINDOW, KV_DIM), jnp.bfloat16), pltpu.VMEM((tm + WINDOW, KV_DIM), jnp.bfloat16),
        pltpu.VMEM((tm + WINDOW, KV_DIM), jnp.bfloat16), pltpu.VMEM((tm + WINDOW, KV_DIM), jnp.bfloat16),
        pltpu.VMEM((tm + 8, CONV_DIM), jnp.float32),
        pltpu.VMEM((tm, ATTN_DIM), jnp.float32),
        pltpu.VMEM((N_EXPERTS, LANES), jnp.float32),
    ]
    return pl.pallas_call(
        _prompt_mixer_kernel,
        grid_spec=pltpu.PrefetchScalarGridSpec(
            num_scalar_prefetch=1, grid=(bsz, nt), in_specs=in_specs, out_specs=out_specs, scratch_shapes=scratch),
        out_shape=out_shape,
        compiler_params=pltpu.CompilerParams(dimension_semantics=("arbitrary", "arbitrary"),
                                             vmem_limit_bytes=VMEM_LIMIT),
        name="prompt_mixer",
    )(sinks, x, mod, consts["anorm"], consts["fnorm"], consts["w_in"], consts["cos_p"], consts["sin_p"],
      consts["bias_p"], consts["conv_w"], consts["gn_attn"], consts["gn_conv"], consts["w_out"],
      consts["w_r"], consts["b_r"], consts["tri"])


def _sample_mixer_kernel(sinks_ref, x_ref, mod_ref, anorm_ref, fnorm_ref, win_ref, cos_ref, sin_ref, bias_ref,
                         convw_ref, gna_ref, gnc_ref, wout_ref, wr_ref, br_ref, tri_ref,
                         ck_ref, cv_ref, e1_ref, e2_ref, cnt_in_ref,
                         x1_ref, h2_ref, ri_ref, rf_ref, cnt_out_ref, knew_ref, vnew_ref, unew_ref,
                         q_s, k_s, v_s, att_s, cnt_s, *, dec):
    n = x_ref.shape[0]
    cb = SAMPLE_CHUNK
    rows = cb * dec

    @pl.when(pl.program_id(0) == 0)
    def _():
        cnt_s[...] = cnt_in_ref[...]

    x = x_ref[...]
    mod = jnp.repeat(mod_ref[...], dec, axis=0)
    md = lambda j: mod[:, j * D_MODEL:(j + 1) * D_MODEL]
    h = _rms(x, anorm_ref[...]) * (1.0 + md(1)) + md(0)
    proj = jnp.dot(h.astype(jnp.bfloat16), win_ref[...], preferred_element_type=jnp.float32)
    q, k, v, u, gate_b = _split_proj(proj)
    cos = cos_ref[...]
    sin = sin_ref[...]
    q = _rope(q, cos, sin) * Q_SCALE
    k = _rope(k, cos, sin)
    q_s[...] = q.astype(jnp.bfloat16)
    k_s[...] = k
    v_s[...] = v
    unew_ref[...] = u

    pos = lax.broadcasted_iota(jnp.int32, (n, CONV_DIM), 0) % dec
    um1 = jnp.where(pos >= 1, pltpu.roll(u, 1, 0), 0.0) + e1_ref[...]
    um2 = jnp.where(pos >= 2, pltpu.roll(u, 2, 0), 0.0) + e2_ref[...]
    cw = convw_ref[...]
    conv = (um2 * cw[0:1] + um1 * cw[1:2] + u * cw[2:3]) * gate_b
    conv_n = _group_norm64(conv, gnc_ref[...])

    lo = _half_lane_mask(rows)
    bias = bias_ref[...]

    half = KV_DIM // 2
    lane = lax.broadcasted_iota(jnp.int32, (KV_DIM, WINDOW), 1)
    swap_rows = lambda t: jnp.concatenate([t[half:], t[:half]], axis=0)

    for c in range(n // rows):
        r0 = c * rows
        qb = q_s[r0:r0 + rows, :]
        cats = []
        for ref, new_rows, out_ref in ((ck_ref, k_s[r0:r0 + rows, :], knew_ref), (cv_ref, v_s[r0:r0 + rows, :], vnew_ref)):
            tiles = [ref[(c * cb + b) * KV_DIM:(c * cb + b + 1) * KV_DIM, :] for b in range(cb)]
            new_t = jnp.concatenate([jnp.transpose(new_rows), jnp.zeros((KV_DIM, WINDOW - rows), jnp.float32)], axis=1)
            cat = jnp.concatenate(tiles + [new_t], axis=1)
            cats.append((cat.astype(jnp.bfloat16), swap_rows(cat).astype(jnp.bfloat16)))
            for b in range(cb):
                kept = pltpu.roll(tiles[b], WINDOW - dec, 1)
                fresh = pltpu.roll(new_t, (WINDOW - dec - b * dec) % WINDOW, 1)
                out_ref[(c * cb + b) * KV_DIM:(c * cb + b + 1) * KV_DIM, :] = jnp.where(lane < WINDOW - dec, kept, fresh)
        (keys, keys_sw), (vals, vals_sw) = cats
        outs = []
        for g in range(N_KV_HEADS):
            outs += _attend(qb, keys, keys_sw, vals, vals_sw, bias, sinks_ref, g, transposed=True)
        att_s[r0:r0 + rows, :] = _merge_heads(outs, lo)

    attn_n = att_s[...] * gna_ref[...]
    _mixer_tail(x, attn_n, conv_n, (md(2), md(3), md(4)), wout_ref, fnorm_ref, wr_ref, br_ref, tri_ref,
                cnt_s, x1_ref, h2_ref, ri_ref, rf_ref)
    cnt_out_ref[...] = cnt_s[...]


def _sample_mixer(x, mod, sinks, consts, ck, cv, e1, e2, cnt_in, dec):
    n_s = x.shape[0]
    ts = SAMPLE_TILE
    seqs = ts // dec
    full = lambda a: pl.BlockSpec(a.shape, lambda i, s: (0,) * a.ndim)
    rows = lambda width: pl.BlockSpec((ts, width), lambda i, s: (i, 0))
    in_specs = [
        rows(D_MODEL), pl.BlockSpec((seqs, 6 * D_MODEL), lambda i, s: (i, 0)),
        full(consts["anorm"]), full(consts["fnorm"]), full(consts["w_in"]),
        rows(LANES), rows(LANES), full(consts["bias_s"]),
        full(consts["conv_w"]), full(consts["gn_attn"]), full(consts["gn_conv"]), full(consts["w_out"]),
        full(consts["w_r"]), full(consts["b_r"]),
        pl.BlockSpec((ts, ts), lambda i, s: (0, 0)),
        pl.BlockSpec((seqs * KV_DIM, WINDOW), lambda i, s: (i, 0)),
        pl.BlockSpec((seqs * KV_DIM, WINDOW), lambda i, s: (i, 0)),
        rows(CONV_DIM), rows(CONV_DIM), full(cnt_in),
    ]
    args = [x, mod, consts["anorm"], consts["fnorm"], consts["w_in"], consts["cos_s"], consts["sin_s"],
            consts["bias_s"], consts["conv_w"], consts["gn_attn"], consts["gn_conv"], consts["w_out"],
            consts["w_r"], consts["b_r"], consts["tri"], ck, cv, e1, e2, cnt_in]
    nt = n_s // ts
    out_shape = [
        jax.ShapeDtypeStruct((n_s, D_MODEL), jnp.float32),
        jax.ShapeDtypeStruct((n_s, D_MODEL // 2), jnp.uint32),
        jax.ShapeDtypeStruct((nt, 8, ts), jnp.int32),
        jax.ShapeDtypeStruct((nt, 8, ts), jnp.float32),
        jax.ShapeDtypeStruct((N_EXPERTS, LANES), jnp.float32),
        jax.ShapeDtypeStruct(ck.shape, jnp.float32),
        jax.ShapeDtypeStruct(cv.shape, jnp.float32),
        jax.ShapeDtypeStruct((n_s, CONV_DIM), jnp.float32),
    ]
    out_specs = [
        rows(D_MODEL),
        rows(D_MODEL // 2),
        pl.BlockSpec((None, 8, ts), lambda i, s: (i, 0, 0)),
        pl.BlockSpec((None, 8, ts), lambda i, s: (i, 0, 0)),
        pl.BlockSpec((N_EXPERTS, LANES), lambda i, s: (0, 0)),
        pl.BlockSpec((seqs * KV_DIM, WINDOW), lambda i, s: (i, 0)),
        pl.BlockSpec((seqs * KV_DIM, WINDOW), lambda i, s: (i, 0)),
        rows(CONV_DIM),
    ]
    scratch = [
        pltpu.VMEM((ts, ATTN_DIM), jnp.bfloat16),
        pltpu.VMEM((ts, KV_DIM), jnp.float32), pltpu.VMEM((ts, KV_DIM), jnp.float32),
        pltpu.VMEM((ts, ATTN_DIM), jnp.float32),
        pltpu.VMEM((N_EXPERTS, LANES), jnp.float32),
    ]
    return pl.pallas_call(
        functools.partial(_sample_mixer_kernel, dec=dec),
        grid_spec=pltpu.PrefetchScalarGridSpec(
            num_scalar_prefetch=1, grid=(nt,), in_specs=in_specs, out_specs=out_specs, scratch_shapes=scratch),
        out_shape=out_shape,
        compiler_params=pltpu.CompilerParams(dimension_semantics=("arbitrary",), vmem_limit_bytes=VMEM_LIMIT),
        name="sample_mixer",
    )(sinks, *args)


def _dispatch(idx, h2, n_rows):
    mesh = plsc.VectorSubcoreMesh(core_axis_name="core", subcore_axis_name="subcore")
    rows, window = SC_ROWS, SC_INDEX_WINDOW
    sub = window // rows
    n_tok, width = h2.shape
    assert n_tok % rows == 0 and (2 * n_tok) % window == 0

    moved = 3 * n_tok * width * h2.dtype.itemsize
    @pl.kernel(out_type=jax.ShapeDtypeStruct((n_rows, width), h2.dtype), mesh=mesh, scratch_types=[],
               cost_estimate=pl.CostEstimate(flops=0, transcendentals=0, bytes_accessed=moved))
    def scatter(h_hbm, i_hbm, xs_hbm):
        def body(indices, h_vmem, i_vmem):
            pltpu.sync_copy(h_vmem, xs_hbm.at[i_vmem.at[0, pl.ds(indices[1] * rows, rows)]])

        pltpu.emit_pipeline(
            body, grid=(2 * n_tok // window, sub),
            in_specs=[pl.BlockSpec((rows, width), lambda i, j: ((i * sub + j) % (n_tok // rows), 0)),
                      pl.BlockSpec((1, window), lambda i, j: (0, i))],
            out_specs=[], core_axis_name=("core", "subcore"),
            dimension_semantics=(pltpu.PARALLEL, pltpu.ARBITRARY), _explicit_indices=True,
        )(h_hbm, i_hbm)

    return scatter(h2, idx)


def _scatter_rows_kernel(dest_ref, h_ref, xs_in_ref, xs_ref, sem):
    del xs_in_ref
    n_tok = h_ref.shape[0]

    def issue(i, carry):
        for k in range(2):
            d = dest_ref[k * n_tok + i]
            pltpu.make_async_copy(h_ref.at[pl.ds(i, 1)], xs_ref.at[pl.ds(d, 1)], sem.at[0]).start(priority=k)
        return carry

    lax.fori_loop(0, n_tok, issue, 0, unroll=8)
    for _ in range(2):
        pltpu.make_async_copy(h_ref, xs_ref.at[pl.ds(0, n_tok)], sem.at[0]).wait()


def _scatter_rows(dest, h2, xs):
    return pl.pallas_call(
        _scatter_rows_kernel,
        grid_spec=pltpu.PrefetchScalarGridSpec(
            num_scalar_prefetch=1, grid=(1,),
            in_specs=[pl.BlockSpec(h2.shape, lambda i, d: (0, 0)), pl.BlockSpec(memory_space=pl.ANY)],
            out_specs=pl.BlockSpec(memory_space=pl.ANY),
            scratch_shapes=[pltpu.SemaphoreType.DMA((1,))]),
        out_shape=jax.ShapeDtypeStruct(xs.shape, xs.dtype),
        input_output_aliases={2: 0},
        compiler_params=pltpu.CompilerParams(dimension_semantics=("arbitrary",)),
        name="scatter_rows",
    )(dest, h2, xs)


def _experts_kernel(be_ref, nu_ref, nv_ref, nxt_ref, par_ref, phys_ref, xs_ref, wg_ref, wu_ref, wd_ref, y_ref,
                    wg_s, wu_s, wd_s, wg_f, wu_f, wd_f, sem):
    del phys_ref
    j = pl.program_id(0)
    used = j < nu_ref[0]
    changed = jnp.logical_or(j == 0, be_ref[j] != be_ref[jnp.maximum(j - 1, 0)])

    def weight_copies(e, slot):
        return [pltpu.make_async_copy(src.at[e], dst.at[slot], sem.at[slot])
                for src, dst in ((wg_ref, wg_f), (wu_ref, wu_f), (wd_ref, wd_f))]

    @pl.when(jnp.logical_and(used, changed))
    def _():
        slot = par_ref[j]

        @pl.when(j == 0)
        def _():
            for c in weight_copies(be_ref[0], slot):
                c.start()

        for c in weight_copies(be_ref[j], slot):
            c.wait()

        @pl.when(nxt_ref[j] >= 0)
        def _():
            for c in weight_copies(nxt_ref[j], 1 - slot):
                c.start()

        wg_s[...] = wg_f[slot].astype(jnp.bfloat16)
        wu_s[...] = wu_f[slot].astype(jnp.bfloat16)
        wd_s[...] = wd_f[slot].astype(jnp.bfloat16)

    @pl.when(used)
    def _():
        hb = xs_ref.shape[0] // 2
        row = lax.broadcasted_iota(jnp.int32, (hb, 1), 0)

        def load(h):
            live = row + h * hb < nv_ref[j]
            return _unpack_bf16_pairs(jnp.where(live, xs_ref[h * hb:(h + 1) * hb, :], jnp.uint32(0)))

        def swiglu(g, u):
            return (g * jax.nn.sigmoid(g) * u).astype(jnp.bfloat16)

        x0 = load(0)
        g0 = jnp.dot(x0, wg_s[...], preferred_element_type=jnp.float32)
        u0 = jnp.dot(x0, wu_s[...], preferred_element_type=jnp.float32)
        x1 = load(1)
        g1 = jnp.dot(x1, wg_s[...], preferred_element_type=jnp.float32)
        a0 = swiglu(g0, u0)
        u1 = jnp.dot(x1, wu_s[...], preferred_element_type=jnp.float32)
        y0 = jnp.dot(a0, wd_s[...], preferred_element_type=jnp.float32)
        a1 = swiglu(g1, u1)
        y_ref[0:hb, :] = _pack_bf16_pairs(y0.astype(jnp.bfloat16))
        y1 = jnp.dot(a1, wd_s[...], preferred_element_type=jnp.float32)
        y_ref[hb:, :] = _pack_bf16_pairs(y1.astype(jnp.bfloat16))


def _experts(block_e, n_used, n_valid, next_e, parity, phys, xs, w_gate, w_up, w_down):
    bm = EXPERT_BLOCK
    n_blocks = xs.shape[0] // bm
    hbm = pl.BlockSpec(memory_space=pl.ANY)
    return pl.pallas_call(
        _experts_kernel,
        grid_spec=pltpu.PrefetchScalarGridSpec(
            num_scalar_prefetch=6, grid=(n_blocks,),
            in_specs=[pl.BlockSpec((bm, D_MODEL // 2), lambda j, be, nu, nv, nx, pa, ph: (ph[j], 0)),
                      hbm, hbm, hbm],
            out_specs=pl.BlockSpec((bm, D_MODEL // 2), lambda j, be, nu, nv, nx, pa, ph: (ph[j], 0)),
            scratch_shapes=[pltpu.VMEM((D_MODEL, EXPERT_HIDDEN), jnp.bfloat16),
                            pltpu.VMEM((D_MODEL, EXPERT_HIDDEN), jnp.bfloat16),
                            pltpu.VMEM((EXPERT_HIDDEN, D_MODEL), jnp.bfloat16),
                            pltpu.VMEM((2, D_MODEL, EXPERT_HIDDEN), jnp.float32),
                            pltpu.VMEM((2, D_MODEL, EXPERT_HIDDEN), jnp.float32),
                            pltpu.VMEM((2, EXPERT_HIDDEN, D_MODEL), jnp.float32),
                            pltpu.SemaphoreType.DMA((2,))]),
        out_shape=jax.ShapeDtypeStruct((xs.shape[0], D_MODEL // 2), jnp.uint32),
        compiler_params=pltpu.CompilerParams(dimension_semantics=("arbitrary",), vmem_limit_bytes=VMEM_LIMIT),
        name="experts",
    )(block_e, n_used, n_valid, next_e, parity, phys, xs, w_gate, w_up, w_down)


def _gather_rows(idx, yb):
    mesh = plsc.VectorSubcoreMesh(core_axis_name="core", subcore_axis_name="subcore")
    rows, window = SC_ROWS, SC_INDEX_WINDOW
    sub = window // rows
    n = idx.shape[1]
    width = yb.shape[1]
    assert n % window == 0

    @pl.kernel(out_type=jax.ShapeDtypeStruct((n, width), yb.dtype), mesh=mesh, scratch_types=[])
    def gather(y_hbm, i_hbm, o_hbm):
        def body(indices, i_vmem, o_vmem):
            pltpu.sync_copy(y_hbm.at[i_vmem.at[0, pl.ds(indices[1] * rows, rows)]], o_vmem)

        pltpu.emit_pipeline(
            body, grid=(n // window, sub),
            in_specs=[pl.BlockSpec((1, window), lambda i, j: (0, i))],
            out_specs=[pl.BlockSpec((rows, width), lambda i, j: (i * sub + j, 0))],
            core_axis_name=("core", "subcore"),
            dimension_semantics=(pltpu.PARALLEL, pltpu.ARBITRARY), _explicit_indices=True,
        )(i_hbm, o_hbm)

    return gather(yb, idx)


def _combine_kernel(x1_ref, gate_ref, rf_ref, fin_ref, y0_ref, y1_ref, *out_refs, per_token_gate):
    o_ref = out_refs[-1]
    gates = jnp.transpose(rf_ref[...])
    moe = _unpack_pairs_f32(y0_ref[...]) * gates[:, 0:1] + _unpack_pairs_f32(y1_ref[...]) * gates[:, 1:2]
    gate_f = gate_ref[...] if per_token_gate else gate_ref[...][5:6]
    o_ref[...] = _rms(x1_ref[...] + gate_f * moe, fin_ref[...])


def _combine(x1, gate, rf, final_norm, ys, out_prev, first_tile, n_tiles, *, per_token_gate, tile):
    tc = tile
    gate_spec = (pl.BlockSpec((tc, D_MODEL), lambda i: (first_tile + i, 0)) if per_token_gate
                 else pl.BlockSpec((6, D_MODEL), lambda i: (0, 0)))
    args = [x1, gate, rf, final_norm, ys, ys]
    in_specs = [pl.BlockSpec((tc, D_MODEL), lambda i: (first_tile + i, 0)),
                gate_spec,
                pl.BlockSpec((8, tc), lambda i: (0, first_tile + i)),
                pl.BlockSpec((1, D_MODEL), lambda i: (0, 0)),
                pl.BlockSpec((tc, D_MODEL // 2), lambda i: (i, 0)),
                pl.BlockSpec((tc, D_MODEL // 2), lambda i: (n_tiles + i, 0))]
    aliases = {}
    if out_prev is not None:
        args.append(out_prev)
        in_specs.append(pl.BlockSpec(memory_space=pl.ANY))
        aliases = {len(args) - 1: 0}
    return pl.pallas_call(
        functools.partial(_combine_kernel, per_token_gate=per_token_gate),
        grid=(n_tiles,), in_specs=in_specs,
        out_specs=pl.BlockSpec((tc, D_MODEL), lambda i: (first_tile + i, 0)),
        out_shape=jax.ShapeDtypeStruct(x1.shape, jnp.float32),
        input_output_aliases=aliases,
        compiler_params=pltpu.CompilerParams(dimension_semantics=("arbitrary",), vmem_limit_bytes=VMEM_LIMIT),
        name="combine",
    )(*args)


def _rope_tables(pos):
    half = HEAD_DIM // 2
    inv_freq = np.power(np.float32(ROPE_THETA), -(np.arange(half, dtype=np.float32) / np.float32(half)))
    ang = (pos.astype(np.float32)[:, None] * inv_freq[None, :].astype(np.float32)).astype(np.float64)
    cos, sin = np.cos(ang).astype(np.float32), np.sin(ang).astype(np.float32)
    cos = np.concatenate([cos, cos, cos, cos], axis=1)
    sin = np.concatenate([-sin, sin, -sin, sin], axis=1)
    return cos, sin


def _prompt_bias():
    r = np.arange(WINDOW)[:, None]
    j = np.arange(2 * WINDOW)[None, :]
    band = (j > r) & (j <= r + WINDOW)
    later = np.where(band, 0.0, NEG).astype(np.float32)
    first = np.where(band & (j >= WINDOW), 0.0, NEG).astype(np.float32)
    return jnp.asarray(np.stack([first, later]))


def _sample_bias(dec):
    cb = SAMPLE_CHUNK
    qb = np.repeat(np.arange(cb), dec)[:, None]
    qi = np.tile(np.arange(dec), cb)[:, None]
    kb = np.concatenate([np.repeat(np.arange(cb), WINDOW), np.repeat(np.arange(cb), dec)])[None, :]
    kj = np.concatenate([np.tile(np.arange(WINDOW), cb), np.tile(np.arange(dec), cb)])[None, :]
    is_new = (np.arange(cb * WINDOW + cb * dec) >= cb * WINDOW)[None, :]
    ok = (kb == qb) & np.where(is_new, kj <= qi, kj > qi)
    ok = np.pad(ok, ((0, 0), (0, SAMPLE_KEYS - ok.shape[1])))
    return jnp.asarray(np.where(ok, 0.0, NEG).astype(np.float32))


def kernel(x_prompt, x_sample, cache_k, cache_v, state_conv, c_prompt, c_sample, attn_norm, ffn_norm, w_mod, b_mod, w_in, conv_w, attn_sinks, out_norm_attn, out_norm_conv, w_out, w_group, b_group, w_expert, b_expert, w_gate, w_up, w_down, final_norm):
    assert attn_norm.shape[0] == 1, "one layer"
    bsz, seq, _ = x_prompt.shape
    nseq, dec, _ = x_sample.shape
    m_prompt = bsz * seq
    n_s = nseq * dec
    m_total = m_prompt + n_s
    tm = MIXER_TILE
    assert seq % tm == 0 and n_s % SAMPLE_TILE == 0 and SAMPLE_TILE % (SAMPLE_CHUNK * dec) == 0
    assert cache_k.shape[2] == WINDOW and m_prompt % SAMPLE_TILE == 0 and m_total % COMBINE_TILE == 0
    bf = jnp.bfloat16

    w_r = jnp.zeros((ROUTER_ROWS, D_MODEL), jnp.float32)
    w_r = w_r.at[0:N_GROUPS].set(w_group[0].T).at[8:8 + N_EXPERTS].set(w_expert[0].T).astype(bf)
    b_r = jnp.zeros((ROUTER_ROWS,), jnp.float32).at[0:N_GROUPS].set(b_group[0]).at[8:8 + N_EXPERTS].set(b_expert[0])
    b_r = jnp.broadcast_to(b_r[:, None], (ROUTER_ROWS, LANES))
    cos_p, sin_p = _rope_tables(np.arange(seq))
    cos_s, sin_s = _rope_tables(PAST_LEN + np.arange(dec))
    tri = jnp.asarray(np.triu(np.ones((tm, tm), np.float32), 1)).astype(bf)
    consts = dict(
        anorm=attn_norm[0][None], fnorm=ffn_norm[0][None], w_in=w_in[0].astype(bf),
        cos_p=jnp.asarray(cos_p), sin_p=jnp.asarray(sin_p),
        cos_s=jnp.asarray(np.tile(cos_s, (nseq, 1))), sin_s=jnp.asarray(np.tile(sin_s, (nseq, 1))),
        bias_p=_prompt_bias(), bias_s=_sample_bias(dec), conv_w=conv_w[0],
        gn_attn=out_norm_attn[0][None], gn_conv=out_norm_conv[0][None], w_out=w_out[0].astype(bf),
        w_r=w_r, b_r=b_r, tri=tri)
    sinks = attn_sinks[0]

    mod = _adaln(jnp.concatenate([c_prompt, c_sample], axis=0), w_mod[0], b_mod[0])
    mod_p = mod[:bsz].reshape(bsz, 6, D_MODEL)
    mod_s = mod[bsz:]

    bm = EXPERT_BLOCK
    eids = jnp.arange(N_EXPERTS, dtype=jnp.int32)
    lookup = lambda table, e: jnp.sum(jnp.where(e[..., None] == eids, table, 0), axis=-1)
    by_token = lambda a: jnp.transpose(a, (1, 0, 2)).reshape(8, -1)
    ceil_bm = lambda c: (c + bm - 1) // bm * bm
    x1_p, h2_p, ri_p, rf_p, cnt_p, k_p, v_p, u_p = _prompt_mixer(x_prompt, mod_p, sinks, consts)
    cnt_p = cnt_p[:, 0].astype(jnp.int32)
    padded_p = ceil_bm(cnt_p)
    pstart_p = jnp.cumsum(padded_p) - padded_p
    nb_p = -(-(2 * m_prompt + N_EXPERTS * (bm - 1)) // bm)
    nb_o = -(-(2 * n_s + N_EXPERTS * (bm - 1)) // bm)
    n_blocks = nb_p + nb_o
    ri_p = by_token(ri_p)
    dest_p = (lookup(pstart_p, ri_p[0:2]) + ri_p[2:4]).reshape(2 * m_prompt)

    st = state_conv[0]
    zrow = jnp.zeros((nseq, 1, CONV_DIM), jnp.float32)
    e1 = jnp.concatenate([st[:, 1:2]] + [zrow] * (dec - 1), axis=1).reshape(n_s, CONV_DIM)
    e2 = jnp.concatenate([st[:, 0:1], st[:, 1:2]] + [zrow] * (dec - 2), axis=1).reshape(n_s, CONV_DIM)
    dest_p, e1 = lax.optimization_barrier((dest_p, e1))
    xs = _dispatch(dest_p[None], h2_p, n_blocks * bm)
    ck = jnp.transpose(cache_k[0], (0, 2, 3, 1)).reshape(nseq * KV_DIM, WINDOW)
    cv = jnp.transpose(cache_v[0], (0, 2, 3, 1)).reshape(nseq * KV_DIM, WINDOW)
    x1_s, h2_s, ri_s, rf_s, cnt_s, k_s, v_s, u_s = _sample_mixer(
        x_sample.reshape(n_s, D_MODEL), mod_s, sinks, consts, ck, cv, e1, e2,
        jnp.zeros((N_EXPERTS, LANES), jnp.float32), dec)

    cnt_s = cnt_s[:, 0].astype(jnp.int32)
    fit = jnp.minimum(cnt_s, padded_p - cnt_p)
    over = cnt_s - fit
    padded_o = ceil_bm(over)
    ostart = nb_p * bm + jnp.cumsum(padded_o) - padded_o
    ri_s = by_token(ri_s)
    e_s, r_s = ri_s[0:2], ri_s[2:4]
    dest_s = jnp.where(r_s < lookup(fit, e_s), lookup(pstart_p + cnt_p, e_s) + r_s, lookup(ostart - fit, e_s) + r_s)
    dest_s = dest_s.reshape(2 * n_s)

    from_t = lambda a: jnp.transpose(a.reshape(nseq, N_KV_HEADS, HEAD_DIM, WINDOW), (0, 3, 1, 2))[None]
    new_k_s, new_v_s = from_t(k_s), from_t(v_s)
    xs = _scatter_rows(dest_s, h2_s, xs)

    np_e, no_e = padded_p // bm, padded_o // bm
    lend = jnp.cumsum(np_e + no_e)
    lstart = lend - np_e - no_e
    n_used = lend[-1].astype(jnp.int32)
    blk = jnp.arange(n_blocks, dtype=jnp.int32)
    present = np_e + no_e > 0
    last_e = jnp.max(jnp.where(present, eids, 0))
    block_e = jnp.sum((lend[None, :] <= blk[:, None]).astype(jnp.int32), axis=1)
    block_e = jnp.where(blk < n_used, jnp.minimum(block_e, N_EXPERTS - 1), last_e)
    of_block = lambda table: lookup(table, block_e)
    within = blk - of_block(lstart)
    in_prompt = within < of_block(np_e)
    phys = jnp.where(in_prompt, of_block(pstart_p) // bm + within, of_block(ostart) // bm + within - of_block(np_e))
    n_valid = jnp.where(in_prompt, of_block(cnt_p + fit) - within * bm, of_block(over) - (within - of_block(np_e)) * bm)
    n_valid = jnp.where(blk < n_used, jnp.clip(n_valid, 0, bm), 0).astype(jnp.int32)
    phys = jnp.where(blk < n_used, phys, jnp.sum(jnp.where(blk == n_used - 1, phys, 0))).astype(jnp.int32)
    later = jnp.logical_and(present[None, :], eids[None, :] > eids[:, None])
    next_of = jnp.min(jnp.where(later, eids[None, :], N_EXPERTS), axis=1)
    next_of = jnp.where(next_of == N_EXPERTS, -1, next_of)
    next_e = of_block(next_of).astype(jnp.int32)
    parity = of_block((jnp.cumsum(present.astype(jnp.int32)) - 1) % 2).astype(jnp.int32)
    yb = _experts(block_e, n_used.reshape(1), n_valid, next_e, parity, phys, xs, w_gate[0], w_up[0], w_down[0])

    fin = final_norm[None]
    tc = COMBINE_TILE
    rf_p, rf_s = by_token(rf_p), by_token(rf_s)
    y_p = None
    chunk = min(COMBINE_CHUNK, seq)
    tp = min(PROMPT_COMBINE_TILE, chunk)
    assert seq % chunk == 0 and chunk % tp == 0
    for c in range(m_prompt // chunk):
        lo, hi = c * chunk, (c + 1) * chunk
        idx = jnp.concatenate([dest_p[lo:hi], dest_p[m_prompt + lo:m_prompt + hi]])[None]
        y_p = _combine(x1_p, mod_p[lo // seq], rf_p, fin, _gather_rows(idx, yb), y_p, lo // tp, chunk // tp,
                       per_token_gate=False, tile=tp)
    gate_s = jnp.repeat(mod_s[:, 5 * D_MODEL:], dec, axis=0)
    y_s = _combine(x1_s, gate_s, rf_s, fin, _gather_rows(dest_s[None], yb), None, 0, n_s // tc, per_token_gate=True,
                   tile=tc)

    keep = WINDOW
    new_k_p = k_p.reshape(1, bsz, keep, N_KV_HEADS, HEAD_DIM)
    new_v_p = v_p.reshape(1, bsz, keep, N_KV_HEADS, HEAD_DIM)
    new_u_p = u_p[:, 6:8][None]
    u_all = jnp.concatenate([st, u_s.reshape(nseq, dec, CONV_DIM)], axis=1)
    new_u_s = u_all[:, -2:][None]
    return (y_p.reshape(bsz, seq, D_MODEL), y_s.reshape(nseq, dec, D_MODEL),
            new_k_p, new_v_p, new_u_p, new_k_s, new_v_s, new_u_s)
```
